```python
import jax
import jax.numpy as jnp
from jax import lax
import numpy as np

D_MODEL = 2048
BATCH = 16
SEQ = 256
DEPTH = 1
DEC_BATCH = 4
DEC_SEQ = 4096
PAST_LEN = 256

GRID_W = 64
MIX_WIDTH = D_MODEL
HEAD_DIM = 128
ATTN_WIDTH = MIX_WIDTH // 2
POOL_WIDTH = MIX_WIDTH - ATTN_WIDTH
N_HEADS = ATTN_WIDTH // HEAD_DIM
N_KV_HEADS = 2
Q_PER_KV = N_HEADS // N_KV_HEADS
KV_WIDTH = N_KV_HEADS * HEAD_DIM
IN_WIDTH = POOL_WIDTH + ATTN_WIDTH + 2 * KV_WIDTH
WINDOW = 128
BAND_BLOCK = 128
POOL_WINDOWS = (2, 4, 8, 16)
N_POOL_GROUPS = 4
POOL_GROUP_DIM = POOL_WIDTH // N_POOL_GROUPS
ROPE_BASE = 10000.0
ROPE_AXIS_DIM = HEAD_DIM // 2
N_EXPERT_GROUPS = 4
EXPERTS_PER_GROUP = 8
N_EXPERTS = N_EXPERT_GROUPS * EXPERTS_PER_GROUP
TOP_K = 2
D_EXPERT = D_MODEL // 4
EXPERT_BLOCK = 128
NORM_EPS = 1e-6
NEG_INF = -1e30
ATTN_SCALE = HEAD_DIM ** -0.5

kernel_name = "hybrid_pool_swa_hmoe_diffusion_step"


def rmsnorm(x, g):
    xf = x.astype(jnp.float32)
    y = xf * lax.rsqrt(jnp.mean(xf * xf, axis=-1, keepdims=True) + NORM_EPS)
    return (y * g.astype(jnp.float32)).astype(x.dtype)


def rope_tables(n_tokens):
    rows = n_tokens // GRID_W
    row = jnp.repeat(jnp.arange(rows, dtype=jnp.float32), GRID_W)
    col = jnp.tile(jnp.arange(GRID_W, dtype=jnp.float32), rows)
    half = ROPE_AXIS_DIM // 2
    inv_freq = ROPE_BASE ** (-jnp.arange(half, dtype=jnp.float32) / half)
    ang = jnp.concatenate([row[:, None] * inv_freq, col[:, None] * inv_freq], axis=-1)
    return jnp.cos(ang), jnp.sin(ang)


def apply_rope_2d(x, cos, sin):
    S = x.shape[1]
    half = ROPE_AXIS_DIM // 2
    xs = x.reshape(x.shape[:-1] + (2, 2, half))
    x1, x2 = xs[..., 0, :], xs[..., 1, :]
    c = cos.reshape(S, 2, half)[None, :, None].astype(x.dtype)
    s = sin.reshape(S, 2, half)[None, :, None].astype(x.dtype)
    out = jnp.stack([x1 * c - x2 * s, x1 * s + x2 * c], axis=-2)
    return out.reshape(x.shape)


def pool_mixer(u, pool_w, pool_scale):
    B, L, _ = u.shape
    uf = u.astype(jnp.float32)
    cs = jnp.pad(jnp.cumsum(uf, axis=1), ((0, 0), (1, 0), (0, 0)))
    t = jnp.arange(L)
    outs = []
    for gi, w in enumerate(POOL_WINDOWS):
        lo = jnp.clip(t - w // 2, 0, L)
        hi = jnp.clip(t + w - w // 2, 0, L)
        sl = slice(gi * POOL_GROUP_DIM, (gi + 1) * POOL_GROUP_DIM)
        seg = cs[:, :, sl]
        mean = (seg[:, hi] - seg[:, lo]) / (hi - lo).astype(jnp.float32)[None, :, None]
        outs.append(mean - uf[:, :, sl])
    d = jnp.stack(outs, axis=2).astype(u.dtype)
    y = jnp.einsum('blgc,gce->blge', d, pool_w).reshape(B, L, POOL_WIDTH)
    return y * pool_scale


def context_attention(q, k, v, sink):
    B, L = q.shape[:2]
    qg = q.reshape(B, L, N_KV_HEADS, Q_PER_KV, HEAD_DIM)
    s = jnp.einsum('bqhgd,bkhd->bhgqk', qg, k).astype(jnp.float32) * ATTN_SCALE
    sink_col = jnp.broadcast_to(sink.astype(jnp.float32).reshape(N_KV_HEADS, Q_PER_KV)[None, :, :, None, None],
                                (B, N_KV_HEADS, Q_PER_KV, L, 1))
    p = jax.nn.softmax(jnp.concatenate([s, sink_col], axis=-1), axis=-1)[..., :L]
    o = jnp.einsum('bhgqk,bkhd->bqhgd', p.astype(v.dtype), v)
    return o.reshape(B, L, ATTN_WIDTH)


def latent_attention(q, k, v, k_ctx, v_ctx, sink):
    B, S = q.shape[:2]
    nb = S // BAND_BLOCK
    Lc = k_ctx.shape[1]
    qb = q.reshape(B, nb, BAND_BLOCK, N_KV_HEADS, Q_PER_KV, HEAD_DIM)
    pad = ((0, 0), (BAND_BLOCK, BAND_BLOCK), (0, 0), (0, 0))

    def bands(t):
        return jnp.concatenate(
            [t[:, o:o + S].reshape(B, nb, BAND_BLOCK, N_KV_HEADS, HEAD_DIM)
             for o in (0, BAND_BLOCK, 2 * BAND_BLOCK)], axis=2)

    kb = bands(jnp.pad(k, pad))
    vb = bands(jnp.pad(v, pad))
    s_loc = jnp.einsum('bnqhgd,bnkhd->bnhgqk', qb, kb).astype(jnp.float32) * ATTN_SCALE
    blk = jnp.arange(nb)
    qpos = blk[:, None] * BAND_BLOCK + jnp.arange(BAND_BLOCK)[None, :]
    kpos = blk[:, None] * BAND_BLOCK - BAND_BLOCK + jnp.arange(3 * BAND_BLOCK)[None, :]
    valid = ((jnp.abs(qpos[:, :, None] - kpos[:, None, :]) <= WINDOW)
             & (kpos[:, None, :] >= 0) & (kpos[:, None, :] < S))
    s_loc = jnp.where(valid[None, :, None, None], s_loc, NEG_INF)
    s_ctx = jnp.einsum('bnqhgd,bkhd->bnhgqk', qb, k_ctx).astype(jnp.float32) * ATTN_SCALE
    sink_col = jnp.broadcast_to(
        sink.astype(jnp.float32).reshape(N_KV_HEADS, Q_PER_KV)[None, None, :, :, None, None],
        (B, nb, N_KV_HEADS, Q_PER_KV, BAND_BLOCK, 1))
    p = jax.nn.softmax(jnp.concatenate([s_loc, s_ctx, sink_col], axis=-1), axis=-1)
    p_loc = p[..., :3 * BAND_BLOCK]
    p_ctx = p[..., 3 * BAND_BLOCK:3 * BAND_BLOCK + Lc]
    o = (jnp.einsum('bnhgqk,bnkhd->bnqhgd', p_loc.astype(vb.dtype), vb)
         + jnp.einsum('bnhgqk,bkhd->bnqhgd', p_ctx.astype(v_ctx.dtype), v_ctx))
    return o.reshape(B, S, ATTN_WIDTH)


def moe_ffn(h, rg_w, rg_b, re_w, re_b, w_gate, w_up, w_down):
    T = h.shape[0]
    hf = h.astype(jnp.float32)
    g_logits = hf @ rg_w.astype(jnp.float32) + rg_b.astype(jnp.float32)
    g_prob = jax.nn.softmax(g_logits, axis=-1)
    g_idx = jnp.argmax(g_logits, axis=-1).astype(jnp.int32)
    p_g = jnp.take_along_axis(g_prob, g_idx[:, None], axis=1)
    e_all = jnp.einsum('td,gde->tge', hf, re_w.astype(jnp.float32)) + re_b.astype(jnp.float32)
    e_logits = jnp.take_along_axis(e_all, g_idx[:, None, None], axis=1)[:, 0]
    top_v, top_i = lax.top_k(e_logits, TOP_K)
    weights = p_g * jax.nn.softmax(top_v, axis=-1)
    expert = g_idx[:, None] * EXPERTS_PER_GROUP + top_i.astype(jnp.int32)

    A = T * TOP_K
    flat_e = expert.reshape(A)
    flat_tok = jnp.repeat(jnp.arange(T, dtype=jnp.int32), TOP_K)
    flat_w = weights.reshape(A)
    order = jnp.argsort(flat_e)
    sorted_e = flat_e[order]
    counts = jnp.bincount(flat_e, length=N_EXPERTS)
    padded = (counts + EXPERT_BLOCK - 1) // EXPERT_BLOCK * EXPERT_BLOCK
    pad_end = jnp.cumsum(padded)
    pad_start = pad_end - padded
    start = jnp.cumsum(counts) - counts
    dest = pad_start[sorted_e] + (jnp.arange(A) - start[sorted_e])
    n_blocks = (A + N_EXPERTS * (EXPERT_BLOCK - 1) + EXPERT_BLOCK - 1) // EXPERT_BLOCK
    cap = n_blocks * EXPERT_BLOCK
    buf_tok = jnp.zeros((cap,), jnp.int32).at[dest].set(flat_tok[order])
    buf_w = jnp.zeros((cap,), jnp.float32).at[dest].set(flat_w[order])
    block_start = jnp.arange(n_blocks) * EXPERT_BLOCK
    block_expert = jnp.clip(jnp.searchsorted(pad_end, block_start, side='right'), 0, N_EXPERTS - 1).astype(jnp.int32)

    def expert_block(args):
        tok, e = args
        xb = h[tok]
        a = xb @ w_gate[e]
        b = xb @ w_up[e]
        return (jax.nn.silu(a) * b) @ w_down[e]

    yb = lax.map(expert_block, (buf_tok.reshape(n_blocks, EXPERT_BLOCK), block_expert))
    yb = yb.reshape(cap, h.shape[1]) * buf_w[:, None].astype(h.dtype)
    return jnp.zeros_like(h).at[buf_tok].add(yb)


def trunk_layer(x, cond, lp, ctx):
    (w_mod, b_mod, g_mix, g_ffn, w_in, pool_w, pool_scale, sink, g_out_pool, g_out_attn,
     w_out, rg_w, rg_b, re_w, re_b, w_gate, w_up, w_down) = lp
    B, L, _ = x.shape
    mod = (jax.nn.silu(cond) @ w_mod + b_mod)[:, None, :]
    sh1, sc1, gt1, sh2, sc2, gt2 = jnp.split(mod, 6, axis=-1)
    h = rmsnorm(x, g_mix) * (1 + sc1) + sh1
    u = h @ w_in
    u_pool = u[..., :POOL_WIDTH]
    q = u[..., POOL_WIDTH:POOL_WIDTH + ATTN_WIDTH].reshape(B, L, N_HEADS, HEAD_DIM)
    k = u[..., POOL_WIDTH + ATTN_WIDTH:IN_WIDTH - KV_WIDTH].reshape(B, L, N_KV_HEADS, HEAD_DIM)
    v = u[..., IN_WIDTH - KV_WIDTH:].reshape(B, L, N_KV_HEADS, HEAD_DIM)
    pool_out = pool_mixer(u_pool, pool_w, pool_scale)
    if ctx is None:
        attn_out = context_attention(q, k, v, sink)
    else:
        k_ctx, v_ctx, cos, sin = ctx
        attn_out = latent_attention(apply_rope_2d(q, cos, sin), apply_rope_2d(k, cos, sin),
                                    v, k_ctx, v_ctx, sink)
    mix = jnp.concatenate([rmsnorm(pool_out, g_out_pool), rmsnorm(attn_out, g_out_attn)], axis=-1) @ w_out
    x = x + gt1 * mix
    h2 = rmsnorm(x, g_ffn) * (1 + sc2) + sh2
    ffn = moe_ffn(h2.reshape(B * L, D_MODEL), rg_w, rg_b, re_w, re_b, w_gate, w_up, w_down)
    x = x + gt2 * ffn.reshape(B, L, D_MODEL)
    return x, k, v


def setup_inputs(seed: int = 0) -> dict:
    key = jax.random.key(seed)
    ks = jax.random.split(key, 26)
    f32 = jnp.float32

    def nrm(k, shape, scale):
        return jax.random.normal(k, shape, f32) * scale

    def gain(k, shape):
        return 1.0 + 0.02 * jax.random.normal(k, shape, f32)

    kv_shape = (DEC_BATCH, DEPTH, PAST_LEN, N_KV_HEADS, HEAD_DIM)
    return {
        "x_prompt": nrm(ks[0], (BATCH, SEQ, D_MODEL), 1.0),
        "x_sample": nrm(ks[1], (DEC_BATCH, DEC_SEQ, D_MODEL), 1.0),
        "cache_k": nrm(ks[2], kv_shape, 1.0),
        "cache_v": nrm(ks[3], kv_shape, 1.0),
        "c": nrm(ks[4], (DEC_BATCH, D_MODEL), 1.0),
        "c_ctx": nrm(ks[5], (D_MODEL,), 1.0),
        "w_mod": nrm(ks[6], (DEPTH, D_MODEL, 6 * D_MODEL), 0.5 * D_MODEL ** -0.5),
        "b_mod": nrm(ks[7], (DEPTH, 6 * D_MODEL), 0.01),
        "norm_mix_g": gain(ks[8], (DEPTH, D_MODEL)),
        "norm_ffn_g": gain(ks[9], (DEPTH, D_MODEL)),
        "w_in": nrm(ks[10], (DEPTH, D_MODEL, IN_WIDTH), D_MODEL ** -0.5),
        "pool_w": nrm(ks[11], (DEPTH, N_POOL_GROUPS, POOL_GROUP_DIM, POOL_GROUP_DIM), POOL_GROUP_DIM ** -0.5),
        "pool_scale": gain(ks[12], (DEPTH, POOL_WIDTH)),
        "attn_sink": nrm(ks[13], (DEPTH, N_HEADS), 0.5),
        "out_norm_pool_g": gain(ks[14], (DEPTH, POOL_WIDTH)),
        "out_norm_attn_g": gain(ks[15], (DEPTH, ATTN_WIDTH)),
        "w_out": nrm(ks[16], (DEPTH, MIX_WIDTH, D_MODEL), MIX_WIDTH ** -0.5),
        "router_group_w": nrm(ks[17], (DEPTH, D_MODEL, N_EXPERT_GROUPS), D_MODEL ** -0.5),
        "router_group_b": nrm(ks[18], (DEPTH, N_EXPERT_GROUPS), 0.01),
        "router_expert_w": nrm(ks[19], (DEPTH, N_EXPERT_GROUPS, D_MODEL, EXPERTS_PER_GROUP), D_MODEL ** -0.5),
        "router_expert_b": nrm(ks[20], (DEPTH, N_EXPERT_GROUPS, EXPERTS_PER_GROUP), 0.01),
        "w_gate": nrm(ks[21], (DEPTH, N_EXPERTS, D_MODEL, D_EXPERT), D_MODEL ** -0.5),
        "w_up": nrm(ks[22], (DEPTH, N_EXPERTS, D_MODEL, D_EXPERT), D_MODEL ** -0.5),
        "w_down": nrm(ks[23], (DEPTH, N_EXPERTS, D_EXPERT, D_MODEL), D_EXPERT ** -0.5),
        "final_norm_g": gain(ks[24], (D_MODEL,)),
    }


def reference(x_prompt, x_sample, cache_k, cache_v, c, c_ctx, w_mod, b_mod, norm_mix_g, norm_ffn_g,
              w_in, pool_w, pool_scale, attn_sink, out_norm_pool_g, out_norm_attn_g, w_out,
              router_group_w, router_group_b, router_expert_w, router_expert_b,
              w_gate, w_up, w_down, final_norm_g):
    B = x_prompt.shape[0]
    cond_ctx = jnp.broadcast_to(c_ctx[None, :], (B, D_MODEL))
    cos, sin = rope_tables(x_sample.shape[1])
    xp = x_prompt
    xs = x_sample
    new_k = []
    new_v = []
    for l in range(DEPTH):
        lp = (w_mod[l], b_mod[l], norm_mix_g[l], norm_ffn_g[l], w_in[l], pool_w[l], pool_scale[l],
              attn_sink[l], out_norm_pool_g[l], out_norm_attn_g[l], w_out[l],
              router_group_w[l], router_group_b[l], router_expert_w[l], router_expert_b[l],
              w_gate[l], w_up[l], w_down[l])
        xp, k_l, v_l = trunk_layer(xp, cond_ctx, lp, None)
        new_k.append(k_l)
        new_v.append(v_l)
        xs, _, _ = trunk_layer(xs, c, lp, (cache_k[:, l], cache_v[:, l], cos, sin))
    y_prompt = rmsnorm(xp, final_norm_g)
    y_sample = rmsnorm(xs, final_norm_g)
    ctx_k = jnp.stack(new_k, axis=1)
    ctx_v = jnp.stack(new_v, axis=1)
    return (y_prompt, y_sample, ctx_k, ctx_v)
```

```python
import functools

import jax
import jax.numpy as jnp
from jax import lax
from jax.experimental import pallas as pl
from jax.experimental.pallas import tpu as pltpu

F32 = jnp.float32
BF16 = jnp.bfloat16

D_MODEL = 2048
GRID_W = 64
HEAD_DIM = 128
ATTN_WIDTH = 1024
POOL_WIDTH = 1024
N_HEADS = 8
N_KV_HEADS = 2
Q_PER_KV = 4
KV_WIDTH = 256
IN_WIDTH = 2560
WINDOW = 128
POOL_WINDOWS = (2, 4, 8, 16)
POOL_GROUP_DIM = 256
ROPE_BASE = 10000.0
ROPE_AXIS_DIM = 64
N_EXPERT_GROUPS = 4
EXPERTS_PER_GROUP = 8
N_EXPERTS = 32
TOP_K = 2
D_EXPERT = 512
NORM_EPS = 1e-6
NEG_INF = -1e30
ATTN_SCALE = HEAD_DIM ** -0.5

LANES = 128
VMEM_LIMIT = 48 * 1024 * 1024
TOKEN_TILE = 256
POOL_HALO = 16
Q_BLOCK = 128
EXPERT_ROWS = 256
ROUTER_LANES = 128


def _params(n_grid):
    return pltpu.CompilerParams(
        dimension_semantics=("arbitrary",) * n_grid, vmem_limit_bytes=VMEM_LIMIT)


def _rms(x):
    return x * lax.rsqrt(jnp.mean(x * x, axis=-1, keepdims=True) + NORM_EPS)


def _dot(a, b):
    return jnp.dot(a, b, preferred_element_type=F32)


def _dot_nt(a, b):
    return lax.dot_general(a, b, (((1,), (1,)), ((), ())), preferred_element_type=F32)


def _mod_kernel(cond_ref, w_ref, b_ref, o_ref):
    c = cond_ref[...]
    s = c / (1.0 + jnp.exp(-c))
    o_ref[...] = _dot(s.astype(BF16), w_ref[...].astype(BF16)) + b_ref[...]


def _modulation(cond8, w_mod, b_mod):
    n = w_mod.shape[1]
    tn = 1024
    return pl.pallas_call(
        _mod_kernel,
        grid=(n // tn,),
        in_specs=[
            pl.BlockSpec((8, D_MODEL), lambda j: (0, 0)),
            pl.BlockSpec((D_MODEL, tn), lambda j: (0, j)),
            pl.BlockSpec((1, tn), lambda j: (0, j)),
        ],
        out_specs=pl.BlockSpec((8, tn), lambda j: (0, j)),
        out_shape=jax.ShapeDtypeStruct((8, n), F32),
        compiler_params=_params(1),
        name="modulation",
    )(cond8, w_mod, b_mod.reshape(1, n))


def _rope(x, cos, sin_signed):
    lane = lax.broadcasted_iota(jnp.int32, x.shape, 1)
    partner = jnp.where((lane & 63) < 32,
                        pltpu.roll(x, HEAD_DIM - 32, axis=1),
                        pltpu.roll(x, 32, axis=1))
    return x * cos + partner * sin_signed


def _inproj_kernel(rows_ref, x_ref, mod_ref, g_ref, w_ref, *rest, rope):
    del rows_ref
    if rope:
        cos_ref, sin_ref, up_ref, q_ref, k_ref, v_ref = rest
    else:
        up_ref, q_ref, k_ref, v_ref = rest
    m = mod_ref[0]
    h = _rms(x_ref[...]) * g_ref[...]
    hb = (h * (1.0 + m[1:2]) + m[0:1]).astype(BF16)
    up_ref[...] = _dot(hb, w_ref[:, 0:POOL_WIDTH]).astype(BF16)
    q = _dot(hb, w_ref[:, POOL_WIDTH:POOL_WIDTH + ATTN_WIDTH])
    k = _dot(hb, w_ref[:, POOL_WIDTH + ATTN_WIDTH:IN_WIDTH - KV_WIDTH])
    v = _dot(hb, w_ref[:, IN_WIDTH - KV_WIDTH:IN_WIDTH])
    if rope:
        cos = cos_ref[...]
        sin = sin_ref[...]
        for hd in range(N_HEADS):
            sl = slice(hd * HEAD_DIM, (hd + 1) * HEAD_DIM)
            q_ref[:, sl] = (_rope(q[:, sl], cos, sin) * ATTN_SCALE).astype(BF16)
        for hd in range(N_KV_HEADS):
            sl = slice(hd * HEAD_DIM, (hd + 1) * HEAD_DIM)
            k_ref[:, sl] = _rope(k[:, sl], cos, sin).astype(k_ref.dtype)
    else:
        q_ref[...] = (q * ATTN_SCALE).astype(BF16)
        k_ref[...] = k.astype(k_ref.dtype)
    v_ref[...] = v.astype(v_ref.dtype)


def _in_projection(x, rows, mod3, g_mix, w_in_b, rope_tabs, kv_dtype):
    t = x.shape[0]
    tm = TOKEN_TILE
    rope = rope_tabs is not None
    in_specs = [
        pl.BlockSpec((tm, D_MODEL), lambda i, r: (i, 0)),
        pl.BlockSpec((1, 6, D_MODEL), lambda i, r: (r[i], 0, 0)),
        pl.BlockSpec((1, D_MODEL), lambda i, r: (0, 0)),
        pl.BlockSpec((D_MODEL, IN_WIDTH), lambda i, r: (0, 0), pipeline_mode=pl.Buffered(1)),
    ]
    args = [x, mod3, g_mix, w_in_b]
    if rope:
        seq_tiles = rope_tabs[0].shape[0] // tm
        for tab in rope_tabs:
            in_specs.append(pl.BlockSpec((tm, HEAD_DIM), lambda i, r: (i % seq_tiles, 0)))
            args.append(tab)
    out_shape = (
        jax.ShapeDtypeStruct((t, POOL_WIDTH), BF16),
        jax.ShapeDtypeStruct((t, ATTN_WIDTH), BF16),
        jax.ShapeDtypeStruct((t, KV_WIDTH), kv_dtype),
        jax.ShapeDtypeStruct((t, KV_WIDTH), kv_dtype),
    )
    out_specs = (
        pl.BlockSpec((tm, POOL_WIDTH), lambda i, r: (i, 0)),
        pl.BlockSpec((tm, ATTN_WIDTH), lambda i, r: (i, 0)),
        pl.BlockSpec((tm, KV_WIDTH), lambda i, r: (i, 0)),
        pl.BlockSpec((tm, KV_WIDTH), lambda i, r: (i, 0)),
    )
    return pl.pallas_call(
        functools.partial(_inproj_kernel, rope=rope),
        grid_spec=pltpu.PrefetchScalarGridSpec(
            num_scalar_prefetch=1, grid=(t // tm,), in_specs=in_specs, out_specs=out_specs),
        out_shape=out_shape,
        compiler_params=_params(1),
        name="in_projection_rope" if rope else "in_projection",
    )(rows, *args)


def _pool_kernel(u_ref, pw_ref, ps_ref, g_ref, o_ref, *, seq_len):
    tq = TOKEN_TILE
    win = min(seq_len, tq + 2 * POOL_HALO)
    t0 = pl.program_id(1) * tq
    src0 = pl.multiple_of(jnp.clip(t0 - POOL_HALO, 0, seq_len - win), POOL_HALO)
    u = u_ref[0, pl.ds(src0, win), :]
    t = t0 + lax.broadcasted_iota(jnp.int32, (tq, win), 0)
    j = src0 + lax.broadcasted_iota(jnp.int32, (tq, win), 1)
    tc = t0 + lax.broadcasted_iota(jnp.int32, (tq, 1), 0)
    ys = []
    for gi, w in enumerate(POOL_WINDOWS):
        lo = jnp.maximum(t - w // 2, 0)
        hi = jnp.minimum(t + (w - w // 2), seq_len)
        cnt = (hi - lo).astype(F32)
        a = jnp.where((j >= lo) & (j < hi), 1.0, 0.0) - jnp.where(j == t, cnt, 0.0)
        cnt_col = (jnp.minimum(tc + (w - w // 2), seq_len) - jnp.maximum(tc - w // 2, 0)).astype(F32)
        sl = slice(gi * POOL_GROUP_DIM, (gi + 1) * POOL_GROUP_DIM)
        d = _dot(a.astype(BF16), u[:, sl]) / cnt_col
        ys.append(_dot(d.astype(BF16), pw_ref[gi]))
    y = jnp.concatenate(ys, axis=-1) * ps_ref[...]
    o_ref[0] = (_rms(y) * g_ref[...]).astype(BF16)


def _pool_mixer(up, pool_w_b, pool_scale, g_out_pool):
    b, seq_len, _ = up.shape
    tq = TOKEN_TILE
    return pl.pallas_call(
        functools.partial(_pool_kernel, seq_len=seq_len),
        grid=(b, seq_len // tq),
        in_specs=[
            pl.BlockSpec((1, seq_len, POOL_WIDTH), lambda bi, i: (bi, 0, 0)),
            pl.BlockSpec((len(POOL_WINDOWS), POOL_GROUP_DIM, POOL_GROUP_DIM), lambda bi, i: (0, 0, 0)),
            pl.BlockSpec((1, POOL_WIDTH), lambda bi, i: (0, 0)),
            pl.BlockSpec((1, POOL_WIDTH), lambda bi, i: (0, 0)),
        ],
        out_specs=pl.BlockSpec((1, tq, POOL_WIDTH), lambda bi, i: (bi, i, 0)),
        out_shape=jax.ShapeDtypeStruct((b, seq_len, POOL_WIDTH), BF16),
        compiler_params=_params(2),
        name=f"pool_mixer_{seq_len}",
    )(up, pool_w_b, pool_scale, g_out_pool)


def _stack_heads(q, kh, rows):
    return jnp.concatenate(
        [q[:, (kh * Q_PER_KV + g) * HEAD_DIM:(kh * Q_PER_KV + g + 1) * HEAD_DIM]
         for g in range(Q_PER_KV)], axis=0)


def _sink_column(sink_ref, kh, rows):
    return jnp.concatenate(
        [jnp.broadcast_to(sink_ref[kh * Q_PER_KV + g:kh * Q_PER_KV + g + 1, 0:1], (rows, 1))
         for g in range(Q_PER_KV)], axis=0)


def _ctx_attn_kernel(q_ref, k_ref, v_ref, sink_ref, g_ref, o_ref):
    rows = q_ref.shape[0]
    q = q_ref[...]
    heads = [None] * N_HEADS
    for kh in range(N_KV_HEADS):
        sl = slice(kh * HEAD_DIM, (kh + 1) * HEAD_DIM)
        kb = k_ref[:, sl].astype(BF16)
        vb = v_ref[:, sl].astype(BF16)
        s = _dot_nt(_stack_heads(q, kh, rows), kb)
        sk = _sink_column(sink_ref, kh, rows)
        m = jnp.maximum(jnp.max(s, axis=-1, keepdims=True), sk)
        p = jnp.exp(s - m)
        den = jnp.sum(p, axis=-1, keepdims=True) + jnp.exp(sk - m)
        o = _dot(p.astype(BF16), vb) / den
        for g in range(Q_PER_KV):
            heads[kh * Q_PER_KV + g] = o[g * rows:(g + 1) * rows]
    y = jnp.concatenate(heads, axis=-1)
    o_ref[...] = (_rms(y) * g_ref[...]).astype(BF16)


def _context_attention(q, k, v, sink_b, g_out_attn, seq_len):
    t = q.shape[0]
    return pl.pallas_call(
        _ctx_attn_kernel,
        grid=(t // seq_len,),
        in_specs=[
            pl.BlockSpec((seq_len, ATTN_WIDTH), lambda b: (b, 0)),
            pl.BlockSpec((seq_len, KV_WIDTH), lambda b: (b, 0)),
            pl.BlockSpec((seq_len, KV_WIDTH), lambda b: (b, 0)),
            pl.BlockSpec((N_HEADS, LANES), lambda b: (0, 0)),
            pl.BlockSpec((1, ATTN_WIDTH), lambda b: (0, 0)),
        ],
        out_specs=pl.BlockSpec((seq_len, ATTN_WIDTH), lambda b: (b, 0)),
        out_shape=jax.ShapeDtypeStruct((t, ATTN_WIDTH), BF16),
        compiler_params=_params(1),
        name="context_attention",
    )(q, k, v, sink_b, g_out_attn)


def _lat_attn_kernel(q_ref, k_ref, v_ref, ck_ref, cv_ref, sink_ref, g_ref, o_ref, *, seq_len):
    rows = Q_BLOCK
    span = 3 * Q_BLOCK
    q0 = pl.program_id(1) * rows
    start = pl.multiple_of(jnp.clip(q0 - Q_BLOCK, 0, seq_len - span), Q_BLOCK)
    q = q_ref[...]
    kl = k_ref[0, pl.ds(start, span), :]
    vl = v_ref[0, pl.ds(start, span), :]
    shape = (Q_PER_KV * rows, span)
    qpos = q0 + (lax.broadcasted_iota(jnp.int32, shape, 0) & (rows - 1))
    kpos = start + lax.broadcasted_iota(jnp.int32, shape, 1)
    band = jnp.abs(qpos - kpos) <= WINDOW
    heads = [None] * N_HEADS
    for kh in range(N_KV_HEADS):
        sl = slice(kh * HEAD_DIM, (kh + 1) * HEAD_DIM)
        q4 = _stack_heads(q, kh, rows)
        s_loc = jnp.where(band, _dot_nt(q4, kl[:, sl]), NEG_INF)
        s_ctx = _dot_nt(q4, ck_ref[0, :, sl].astype(BF16))
        sk = _sink_column(sink_ref, kh, rows)
        m = jnp.maximum(jnp.maximum(jnp.max(s_loc, axis=-1, keepdims=True),
                                    jnp.max(s_ctx, axis=-1, keepdims=True)), sk)
        p_loc = jnp.exp(s_loc - m)
        p_ctx = jnp.exp(s_ctx - m)
        den = (jnp.sum(p_loc, axis=-1, keepdims=True) + jnp.sum(p_ctx, axis=-1, keepdims=True)
               + jnp.exp(sk - m))
        o = (_dot(p_loc.astype(BF16), vl[:, sl])
             + _dot(p_ctx.astype(BF16), cv_ref[0, :, sl].astype(BF16))) / den
        for g in range(Q_PER_KV):
            heads[kh * Q_PER_KV + g] = o[g * rows:(g + 1) * rows]
    y = jnp.concatenate(heads, axis=-1)
    o_ref[...] = (_rms(y) * g_ref[...]).astype(BF16)


def _latent_attention(q, k, v, cache_k, cache_v, sink_b, g_out_attn):
    b, seq_len, _ = k.shape
    nq = seq_len // Q_BLOCK
    past = cache_k.shape[1]
    return pl.pallas_call(
        functools.partial(_lat_attn_kernel, seq_len=seq_len),
        grid=(b, nq),
        in_specs=[
            pl.BlockSpec((Q_BLOCK, ATTN_WIDTH), lambda bi, n: (bi * nq + n, 0)),
            pl.BlockSpec((1, seq_len, KV_WIDTH), lambda bi, n: (bi, 0, 0)),
            pl.BlockSpec((1, seq_len, KV_WIDTH), lambda bi, n: (bi, 0, 0)),
            pl.BlockSpec((1, past, KV_WIDTH), lambda bi, n: (bi, 0, 0)),
            pl.BlockSpec((1, past, KV_WIDTH), lambda bi, n: (bi, 0, 0)),
            pl.BlockSpec((N_HEADS, LANES), lambda bi, n: (0, 0)),
            pl.BlockSpec((1, ATTN_WIDTH), lambda bi, n: (0, 0)),
        ],
        out_specs=pl.BlockSpec((Q_BLOCK, ATTN_WIDTH), lambda bi, n: (bi * nq + n, 0)),
        out_shape=jax.ShapeDtypeStruct((b * seq_len, ATTN_WIDTH), BF16),
        compiler_params=_params(2),
        name="latent_attention",
    )(q, k, v, cache_k, cache_v, sink_b, g_out_attn)


def _route(logits):
    lane = lax.broadcasted_iota(jnp.int32, logits.shape, 1).astype(F32)
    neg = -jnp.inf

    def first_argmax(x):
        mx = jnp.max(x, axis=-1, keepdims=True)
        return mx, jnp.min(jnp.where(x == mx, lane, float(ROUTER_LANES)), axis=-1, keepdims=True)

    gl = jnp.where(lane < N_EXPERT_GROUPS, logits, neg)
    gmax, g_idx = first_argmax(gl)
    p_g = 1.0 / jnp.sum(jnp.exp(gl - gmax), axis=-1, keepdims=True)
    base = N_EXPERT_GROUPS + EXPERTS_PER_GROUP * g_idx
    el = jnp.where((lane >= base) & (lane < base + EXPERTS_PER_GROUP), logits, neg)
    v1, i1 = first_argmax(el)
    v2, i2 = first_argmax(jnp.where(lane == i1, neg, el))
    e2 = jnp.exp(v2 - v1)
    w1 = p_g / (1.0 + e2)
    w2 = p_g * e2 / (1.0 + e2)
    ids = jnp.where(lane == 0.0, i1 - N_EXPERT_GROUPS, jnp.where(lane == 1.0, i2 - N_EXPERT_GROUPS, 0.0))
    wts = jnp.where(lane == 0.0, w1, jnp.where(lane == 1.0, w2, 0.0))
    return ids.astype(jnp.int32), wts


def _outproj_kernel(rows_ref, pool_ref, attn_ref, x_ref, mod_ref, g_ref, w_ref, rwh_ref, rwl_ref,
                    rb_ref, *rest):
    del rows_ref
    x1_ref, h2_ref, ids_ref, wts_ref = rest[-4:]
    m = mod_ref[0]
    mix = (_dot(pool_ref[...], w_ref[0:POOL_WIDTH, :])
           + _dot(attn_ref[...], w_ref[POOL_WIDTH:POOL_WIDTH + ATTN_WIDTH, :]))
    x1 = x_ref[...] + m[2:3] * mix
    x1_ref[...] = x1
    h2 = _rms(x1) * g_ref[...] * (1.0 + m[4:5]) + m[3:4]
    h2_ref[...] = h2
    hi = h2.astype(BF16)
    lo = (h2 - hi.astype(F32)).astype(BF16)
    logits = _dot(hi, rwh_ref[...]) + _dot(lo, rwh_ref[...]) + _dot(hi, rwl_ref[...]) + rb_ref[...]
    ids, wts = _route(logits)
    ids_ref[...] = ids
    wts_ref[...] = wts


def _out_projection(pool_n, attn_n, x, rows, mod3, g_ffn, w_out_b, rw_hi, rw_lo, rb, h2_all, tile_off):
    t = x.shape[0]
    tm = TOKEN_TILE
    t_all = h2_all.shape[0]
    in_specs = [
        pl.BlockSpec((tm, POOL_WIDTH), lambda i, r: (i, 0)),
        pl.BlockSpec((tm, ATTN_WIDTH), lambda i, r: (i, 0)),
        pl.BlockSpec((tm, D_MODEL), lambda i, r: (i, 0)),
        pl.BlockSpec((1, 6, D_MODEL), lambda i, r: (r[i], 0, 0)),
        pl.BlockSpec((1, D_MODEL), lambda i, r: (0, 0)),
        pl.BlockSpec((D_MODEL, D_MODEL), lambda i, r: (0, 0), pipeline_mode=pl.Buffered(1)),
        pl.BlockSpec((D_MODEL, ROUTER_LANES), lambda i, r: (0, 0)),
        pl.BlockSpec((D_MODEL, ROUTER_LANES), lambda i, r: (0, 0)),
        pl.BlockSpec((1, ROUTER_LANES), lambda i, r: (0, 0)),
        pl.BlockSpec(memory_space=pl.ANY),
    ]
    out_shape = (
        jax.ShapeDtypeStruct((t, D_MODEL), F32),
        jax.ShapeDtypeStruct((t_all, D_MODEL), F32),
        jax.ShapeDtypeStruct((t, ROUTER_LANES), jnp.int32),
        jax.ShapeDtypeStruct((t, ROUTER_LANES), F32),
    )
    out_specs = (
        pl.BlockSpec((tm, D_MODEL), lambda i, r: (i, 0)),
        pl.BlockSpec((tm, D_MODEL), lambda i, r: (i + tile_off, 0)),
        pl.BlockSpec((tm, ROUTER_LANES), lambda i, r: (i, 0)),
        pl.BlockSpec((tm, ROUTER_LANES), lambda i, r: (i, 0)),
    )
    return pl.pallas_call(
        _outproj_kernel,
        grid_spec=pltpu.PrefetchScalarGridSpec(
            num_scalar_prefetch=1, grid=(t // tm,), in_specs=in_specs, out_specs=out_specs),
        out_shape=out_shape,
        input_output_aliases={10: 1},
        compiler_params=_params(1),
        name=f"out_projection_{tile_off}",
    )(rows, pool_n, attn_n, x, mod3, g_ffn, w_out_b, rw_hi, rw_lo, rb, h2_all)


def _gather_rows(idx_ref, base, n_rows, src_hbm, dst, sem):
    def body(r, carry):
        tok = idx_ref[base + r]
        pltpu.make_async_copy(src_hbm.at[pl.ds(tok, 1)], dst.at[pl.ds(r, 1)], sem).start()
        return carry
    lax.fori_loop(0, n_rows, body, 0, unroll=8)


def _wait_rows(src_hbm, dst, sem):
    pltpu.make_async_copy(src_hbm.at[pl.ds(0, dst.shape[0])], dst, sem).wait()


def _moe_kernel(be_ref, nu_ref, tok_ref, h_hbm, wg_ref, wu_ref, wd_ref, y_ref, xg, sem):
    del be_ref
    b = pl.program_id(0)
    n_used = nu_ref[0]

    @pl.when(b == 0)
    def _():
        _gather_rows(tok_ref, 0, EXPERT_ROWS, h_hbm, xg.at[0], sem.at[0])

    @pl.when(b + 1 < n_used)
    def _():
        nxt = (b + 1) % 2
        _gather_rows(tok_ref, (b + 1) * EXPERT_ROWS, EXPERT_ROWS, h_hbm, xg.at[nxt], sem.at[nxt])

    @pl.when(b < n_used)
    def _():
        cur = b % 2
        _wait_rows(h_hbm, xg.at[cur], sem.at[cur])
        xb = xg[cur].astype(BF16)
        a = _dot(xb, wg_ref[0])
        u = _dot(xb, wu_ref[0])
        act = (a / (1.0 + jnp.exp(-a)) * u).astype(BF16)
        y_ref[...] = _dot(act, wd_ref[0])

    @pl.when(b >= n_used)
    def _():
        y_ref[...] = jnp.zeros_like(y_ref)


def _experts(block_expert, n_used, buf_tok, h2_all, wg_b, wu_b, wd_b):
    cap = buf_tok.shape[0]
    bm = EXPERT_ROWS
    return pl.pallas_call(
        _moe_kernel,
        grid_spec=pltpu.PrefetchScalarGridSpec(
            num_scalar_prefetch=3,
            grid=(cap // bm,),
            in_specs=[
                pl.BlockSpec(memory_space=pl.ANY),
                pl.BlockSpec((1, D_MODEL, D_EXPERT), lambda b, be, nu, tok: (be[b], 0, 0)),
                pl.BlockSpec((1, D_MODEL, D_EXPERT), lambda b, be, nu, tok: (be[b], 0, 0)),
                pl.BlockSpec((1, D_EXPERT, D_MODEL), lambda b, be, nu, tok: (be[b], 0, 0)),
            ],
            out_specs=pl.BlockSpec((bm, D_MODEL), lambda b, be, nu, tok: (b, 0)),
            scratch_shapes=[pltpu.VMEM((2, bm, D_MODEL), F32), pltpu.SemaphoreType.DMA((2,))],
        ),
        out_shape=jax.ShapeDtypeStruct((cap, D_MODEL), F32),
        compiler_params=_params(1),
        name="experts",
    )(block_expert, n_used, buf_tok, h2_all, wg_b, wu_b, wd_b)


def _combine_kernel(slot_ref, rows_ref, y_hbm, x1_ref, wts_ref, mod_ref, g_ref, o_ref, yg, sem,
                    *, tok_off, n_tiles):
    del rows_ref
    tm = TOKEN_TILE
    i = pl.program_id(0)

    def start(tile, buf):
        for k in range(TOP_K):
            _gather_rows(slot_ref, k * slot_ref.shape[0] // TOP_K + tok_off + tile * tm, tm,
                         y_hbm, yg.at[buf, k], sem.at[buf])

    @pl.when(i == 0)
    def _():
        start(0, 0)

    @pl.when(i + 1 < n_tiles)
    def _():
        start(i + 1, (i + 1) % 2)

    cur = i % 2
    for k in range(TOP_K):
        _wait_rows(y_hbm, yg.at[cur, k], sem.at[cur])
    w = wts_ref[...]
    ffn = yg[cur, 0] * w[:, 0:1] + yg[cur, 1] * w[:, 1:2]
    x2 = x1_ref[...] + mod_ref[0][5:6] * ffn
    o_ref[...] = _rms(x2) * g_ref[...]


def _combine(slots, rows, y_buf, x1, wts, mod3, g_final, tok_off):
    t = x1.shape[0]
    tm = TOKEN_TILE
    n_tiles = t // tm
    return pl.pallas_call(
        functools.partial(_combine_kernel, tok_off=tok_off, n_tiles=n_tiles),
        grid_spec=pltpu.PrefetchScalarGridSpec(
            num_scalar_prefetch=2,
            grid=(n_tiles,),
            in_specs=[
                pl.BlockSpec(memory_space=pl.ANY),
                pl.BlockSpec((tm, D_MODEL), lambda i, s, r: (i, 0)),
                pl.BlockSpec((tm, ROUTER_LANES), lambda i, s, r: (i, 0)),
                pl.BlockSpec((1, 6, D_MODEL), lambda i, s, r: (r[i], 0, 0)),
                pl.BlockSpec((1, D_MODEL), lambda i, s, r: (0, 0)),
            ],
            out_specs=pl.BlockSpec((tm, D_MODEL), lambda i, s, r: (i, 0)),
            scratch_shapes=[pltpu.VMEM((2, TOP_K, tm, D_MODEL), F32), pltpu.SemaphoreType.DMA((2,))],
        ),
        out_shape=jax.ShapeDtypeStruct((t, D_MODEL), F32),
        compiler_params=_params(1),
        name=f"combine_{tok_off}",
    )(slots, rows, y_buf, x1, wts, mod3, g_final)


def _rope_tables(n_tokens):
    rows = n_tokens // GRID_W
    row = jnp.repeat(jnp.arange(rows, dtype=F32), GRID_W)
    col = jnp.tile(jnp.arange(GRID_W, dtype=F32), rows)
    half = ROPE_AXIS_DIM // 2
    inv_freq = ROPE_BASE ** (-jnp.arange(half, dtype=F32) / half)
    ar = row[:, None] * inv_freq
    ac = col[:, None] * inv_freq
    cos = jnp.concatenate([jnp.cos(ar), jnp.cos(ar), jnp.cos(ac), jnp.cos(ac)], axis=-1)
    sin = jnp.concatenate([-jnp.sin(ar), jnp.sin(ar), -jnp.sin(ac), jnp.sin(ac)], axis=-1)
    return cos, sin


def _dispatch_tables(ids):
    t = ids.shape[0]
    bm = EXPERT_ROWS
    flat_e = ids.reshape(t * TOP_K)
    onehot = (flat_e[:, None] == jnp.arange(N_EXPERTS, dtype=jnp.int32)[None, :]).astype(jnp.int32)
    running = jnp.cumsum(onehot, axis=0)
    rank = jnp.sum(running * onehot, axis=1) - 1
    counts = running[-1]
    padded = (counts + bm - 1) // bm * bm
    pad_end = jnp.cumsum(padded)
    pad_start = pad_end - padded
    dest = (jnp.sum(pad_start[None, :] * onehot, axis=1) + rank).astype(jnp.int32)
    n_blocks = (t * TOP_K + N_EXPERTS * (bm - 1) + bm - 1) // bm
    flat_tok = jnp.repeat(jnp.arange(t, dtype=jnp.int32), TOP_K)
    buf_tok = jnp.zeros((n_blocks * bm,), jnp.int32).at[dest].set(flat_tok)
    n_used = (pad_end[-1] // bm).astype(jnp.int32)
    blk = jnp.arange(n_blocks, dtype=jnp.int32)
    blk = jnp.minimum(blk, n_used - 1)
    block_expert = jnp.clip(jnp.searchsorted(pad_end, blk * bm, side='right'), 0, N_EXPERTS - 1)
    slots = dest.reshape(t, TOP_K).T.reshape(TOP_K * t)
    return block_expert.astype(jnp.int32), n_used.reshape(1), buf_tok, slots


def kernel(x_prompt, x_sample, cache_k, cache_v, c, c_ctx, w_mod, b_mod, norm_mix_g, norm_ffn_g, w_in,
           pool_w, pool_scale, attn_sink, out_norm_pool_g, out_norm_attn_g, w_out, router_group_w,
           router_group_b, router_expert_w, router_expert_b, w_gate, w_up, w_down, final_norm_g):
    depth = w_mod.shape[0]
    assert depth == 1, "single trunk layer"
    bp, lp, _ = x_prompt.shape
    bs, ls, _ = x_sample.shape
    tp, ts = bp * lp, bs * ls
    tm = TOKEN_TILE
    l = 0

    cond8 = jnp.zeros((8, D_MODEL), F32).at[:bs].set(c).at[bs].set(c_ctx)
    mod3 = _modulation(cond8, w_mod[l], b_mod[l]).reshape(8, 6, D_MODEL)
    rows_p = jnp.full((tp // tm,), bs, jnp.int32)
    rows_s = jnp.arange(ts // tm, dtype=jnp.int32) // (ls // tm)

    w_in_b = w_in[l].astype(BF16)
    w_out_b = w_out[l].astype(BF16)
    pool_w_b = pool_w[l].astype(BF16)
    g_mix = norm_mix_g[l].reshape(1, D_MODEL)
    g_ffn = norm_ffn_g[l].reshape(1, D_MODEL)
    g_pool = out_norm_pool_g[l].reshape(1, POOL_WIDTH)
    g_attn = out_norm_attn_g[l].reshape(1, ATTN_WIDTH)
    p_scale = pool_scale[l].reshape(1, POOL_WIDTH)
    sink_b = jnp.broadcast_to(attn_sink[l][:, None], (N_HEADS, LANES))
    g_final = final_norm_g.reshape(1, D_MODEL)

    rw = jnp.concatenate(
        [router_group_w[l], jnp.transpose(router_expert_w[l], (1, 0, 2)).reshape(D_MODEL, N_EXPERTS)], axis=1)
    rw = jnp.pad(rw, ((0, 0), (0, ROUTER_LANES - rw.shape[1])))
    rw_hi = rw.astype(BF16)
    rw_lo = (rw - rw_hi.astype(F32)).astype(BF16)
    rb = jnp.concatenate([router_group_b[l], router_expert_b[l].reshape(N_EXPERTS)])
    rb = jnp.pad(rb, (0, ROUTER_LANES - rb.shape[0])).reshape(1, ROUTER_LANES)

    xp = x_prompt.reshape(tp, D_MODEL)
    xs = x_sample.reshape(ts, D_MODEL)

    up_p, q_p, k_p, v_p = _in_projection(xp, rows_p, mod3, g_mix, w_in_b, None, F32)
    pool_p = _pool_mixer(up_p.reshape(bp, lp, POOL_WIDTH), pool_w_b, p_scale, g_pool).reshape(tp, POOL_WIDTH)
    attn_p = _context_attention(q_p, k_p, v_p, sink_b, g_attn, lp)
    up_s, q_s, k_s, v_s = _in_projection(xs, rows_s, mod3, g_mix, w_in_b, _rope_tables(ls), BF16)
    pool_s = _pool_mixer(up_s.reshape(bs, ls, POOL_WIDTH), pool_w_b, p_scale, g_pool).reshape(ts, POOL_WIDTH)
    attn_s = _latent_attention(
        q_s, k_s.reshape(bs, ls, KV_WIDTH), v_s.reshape(bs, ls, KV_WIDTH),
        cache_k[:, l].reshape(bs, -1, KV_WIDTH), cache_v[:, l].reshape(bs, -1, KV_WIDTH), sink_b, g_attn)

    h2_all = jnp.zeros((tp + ts, D_MODEL), F32)
    x1_p, h2_all, ids_p, wts_p = _out_projection(
        pool_p, attn_p, xp, rows_p, mod3, g_ffn, w_out_b, rw_hi, rw_lo, rb, h2_all, 0)
    x1_s, h2_all, ids_s, wts_s = _out_projection(
        pool_s, attn_s, xs, rows_s, mod3, g_ffn, w_out_b, rw_hi, rw_lo, rb, h2_all, tp // tm)

    ids = jnp.concatenate([ids_p[:, :TOP_K], ids_s[:, :TOP_K]], axis=0)
    block_expert, n_used, buf_tok, slots = _dispatch_tables(ids)
    y_buf = _experts(block_expert, n_used, buf_tok, h2_all,
                     w_gate[l].astype(BF16), w_up[l].astype(BF16), w_down[l].astype(BF16))

    y_p = _combine(slots, rows_p, y_buf, x1_p, wts_p, mod3, g_final, 0)
    y_s = _combine(slots, rows_s, y_buf, x1_s, wts_s, mod3, g_final, tp)

    kv_shape = (bp, 1, lp, N_KV_HEADS, HEAD_DIM)
    return (y_p.reshape(bp, lp, D_MODEL), y_s.reshape(bs, ls, D_MODEL),
            k_p.reshape(kv_shape), v_p.reshape(kv_shape))
```

```python
import functools

import jax
import jax.numpy as jnp
from jax import lax
from jax.experimental import pallas as pl
from jax.experimental.pallas import tpu as pltpu

F32 = jnp.float32
BF16 = jnp.bfloat16

D_MODEL = 2048
GRID_W = 64
HEAD_DIM = 128
ATTN_WIDTH = 1024
POOL_WIDTH = 1024
N_HEADS = 8
N_KV_HEADS = 2
Q_PER_KV = 4
KV_WIDTH = 256
IN_WIDTH = 2560
WINDOW = 128
POOL_WINDOWS = (2, 4, 8, 16)
POOL_GROUP_DIM = 256
ROPE_BASE = 10000.0
ROPE_AXIS_DIM = 64
N_EXPERT_GROUPS = 4
EXPERTS_PER_GROUP = 8
N_EXPERTS = 32
TOP_K = 2
D_EXPERT = 512
NORM_EPS = 1e-6
NEG_INF = -1e30
ATTN_SCALE = HEAD_DIM ** -0.5

LANES = 128
VMEM_LIMIT = 48 * 1024 * 1024
EXPERT_VMEM_LIMIT = 56 * 1024 * 1024
TOKEN_TILE = 256
POOL_HALO = 16
Q_BLOCK = 128
EXPERT_ROWS = 256
ROUTER_LANES = 128


def _params(n_grid):
    return pltpu.CompilerParams(
        dimension_semantics=("arbitrary",) * n_grid, vmem_limit_bytes=VMEM_LIMIT)


def _rms(x):
    return x * lax.rsqrt(jnp.mean(x * x, axis=-1, keepdims=True) + NORM_EPS)


def _dot(a, b):
    return jnp.dot(a, b, preferred_element_type=F32)


def _dot_nt(a, b):
    return lax.dot_general(a, b, (((1,), (1,)), ((), ())), preferred_element_type=F32)


def _mod_kernel(cond_ref, w_ref, b_ref, o_ref):
    c = cond_ref[...]
    s = c / (1.0 + jnp.exp(-c))
    o_ref[...] = _dot(s.astype(BF16), w_ref[...].astype(BF16)) + b_ref[...]


def _modulation(cond8, w_mod, b_mod):
    n = w_mod.shape[1]
    tn = 1024
    return pl.pallas_call(
        _mod_kernel,
        grid=(n // tn,),
        in_specs=[
            pl.BlockSpec((8, D_MODEL), lambda j: (0, 0)),
            pl.BlockSpec((D_MODEL, tn), lambda j: (0, j)),
            pl.BlockSpec((1, tn), lambda j: (0, j)),
        ],
        out_specs=pl.BlockSpec((8, tn), lambda j: (0, j)),
        out_shape=jax.ShapeDtypeStruct((8, n), F32),
        compiler_params=_params(1),
        name="modulation",
    )(cond8, w_mod, b_mod.reshape(1, n))


def _rope(x, cos, sin_signed):
    lane = lax.broadcasted_iota(jnp.int32, x.shape, 1)
    partner = jnp.where((lane & 63) < 32,
                        pltpu.roll(x, HEAD_DIM - 32, axis=1),
                        pltpu.roll(x, 32, axis=1))
    return x * cos + partner * sin_signed


def _inproj_kernel(rows_ref, x_ref, mod_ref, g_ref, w_ref, *rest, rope):
    del rows_ref
    if rope:
        cos_ref, sin_ref, up_ref, q_ref, k_ref, v_ref = rest
    else:
        up_ref, q_ref, k_ref, v_ref = rest
    m = mod_ref[0]
    h = _rms(x_ref[...]) * g_ref[...]
    hb = (h * (1.0 + m[1:2]) + m[0:1]).astype(BF16)
    up_ref[...] = _dot(hb, w_ref[:, 0:POOL_WIDTH]).astype(BF16)
    q = _dot(hb, w_ref[:, POOL_WIDTH:POOL_WIDTH + ATTN_WIDTH])
    k = _dot(hb, w_ref[:, POOL_WIDTH + ATTN_WIDTH:IN_WIDTH - KV_WIDTH])
    v = _dot(hb, w_ref[:, IN_WIDTH - KV_WIDTH:IN_WIDTH])
    if rope:
        cos = cos_ref[...]
        sin = sin_ref[...]
        for hd in range(N_HEADS):
            sl = slice(hd * HEAD_DIM, (hd + 1) * HEAD_DIM)
            q_ref[:, sl] = (_rope(q[:, sl], cos, sin) * ATTN_SCALE).astype(BF16)
        for hd in range(N_KV_HEADS):
            sl = slice(hd * HEAD_DIM, (hd + 1) * HEAD_DIM)
            k_ref[:, sl] = _rope(k[:, sl], cos, sin).astype(k_ref.dtype)
    else:
        q_ref[...] = (q * ATTN_SCALE).astype(BF16)
        k_ref[...] = k.astype(k_ref.dtype)
    v_ref[...] = v.astype(v_ref.dtype)


def _in_projection(x, rows, mod3, g_mix, w_in_b, rope_tabs, kv_dtype):
    t = x.shape[0]
    tm = TOKEN_TILE
    rope = rope_tabs is not None
    in_specs = [
        pl.BlockSpec((tm, D_MODEL), lambda i, r: (i, 0)),
        pl.BlockSpec((1, 6, D_MODEL), lambda i, r: (r[i], 0, 0)),
        pl.BlockSpec((1, D_MODEL), lambda i, r: (0, 0)),
        pl.BlockSpec((D_MODEL, IN_WIDTH), lambda i, r: (0, 0), pipeline_mode=pl.Buffered(1)),
    ]
    args = [x, mod3, g_mix, w_in_b]
    if rope:
        seq_tiles = rope_tabs[0].shape[0] // tm
        for tab in rope_tabs:
            in_specs.append(pl.BlockSpec((tm, HEAD_DIM), lambda i, r: (i % seq_tiles, 0)))
            args.append(tab)
    out_shape = (
        jax.ShapeDtypeStruct((t, POOL_WIDTH), BF16),
        jax.ShapeDtypeStruct((t, ATTN_WIDTH), BF16),
        jax.ShapeDtypeStruct((t, KV_WIDTH), kv_dtype),
        jax.ShapeDtypeStruct((t, KV_WIDTH), kv_dtype),
    )
    out_specs = (
        pl.BlockSpec((tm, POOL_WIDTH), lambda i, r: (i, 0)),
        pl.BlockSpec((tm, ATTN_WIDTH), lambda i, r: (i, 0)),
        pl.BlockSpec((tm, KV_WIDTH), lambda i, r: (i, 0)),
        pl.BlockSpec((tm, KV_WIDTH), lambda i, r: (i, 0)),
    )
    return pl.pallas_call(
        functools.partial(_inproj_kernel, rope=rope),
        grid_spec=pltpu.PrefetchScalarGridSpec(
            num_scalar_prefetch=1, grid=(t // tm,), in_specs=in_specs, out_specs=out_specs),
        out_shape=out_shape,
        compiler_params=_params(1),
        name="in_projection_rope" if rope else "in_projection",
    )(rows, *args)


def _pool_kernel(u_ref, pw_ref, ps_ref, g_ref, o_ref, *, seq_len):
    tq = TOKEN_TILE
    win = min(seq_len, tq + 2 * POOL_HALO)
    t0 = pl.program_id(1) * tq
    src0 = pl.multiple_of(jnp.clip(t0 - POOL_HALO, 0, seq_len - win), POOL_HALO)
    u = u_ref[0, pl.ds(src0, win), :]
    t = t0 + lax.broadcasted_iota(jnp.int32, (tq, win), 0)
    j = src0 + lax.broadcasted_iota(jnp.int32, (tq, win), 1)
    tc = t0 + lax.broadcasted_iota(jnp.int32, (tq, 1), 0)
    ys = []
    for gi, w in enumerate(POOL_WINDOWS):
        lo = jnp.maximum(t - w // 2, 0)
        hi = jnp.minimum(t + (w - w // 2), seq_len)
        cnt = (hi - lo).astype(F32)
        a = jnp.where((j >= lo) & (j < hi), 1.0, 0.0) - jnp.where(j == t, cnt, 0.0)
        cnt_col = (jnp.minimum(tc + (w - w // 2), seq_len) - jnp.maximum(tc - w // 2, 0)).astype(F32)
        sl = slice(gi * POOL_GROUP_DIM, (gi + 1) * POOL_GROUP_DIM)
        d = _dot(a.astype(BF16), u[:, sl]) / cnt_col
        ys.append(_dot(d.astype(BF16), pw_ref[gi]))
    y = jnp.concatenate(ys, axis=-1) * ps_ref[...]
    o_ref[0] = (_rms(y) * g_ref[...]).astype(BF16)


def _pool_mixer(up, pool_w_b, pool_scale, g_out_pool):
    b, seq_len, _ = up.shape
    tq = TOKEN_TILE
    return pl.pallas_call(
        functools.partial(_pool_kernel, seq_len=seq_len),
        grid=(b, seq_len // tq),
        in_specs=[
            pl.BlockSpec((1, seq_len, POOL_WIDTH), lambda bi, i: (bi, 0, 0)),
            pl.BlockSpec((len(POOL_WINDOWS), POOL_GROUP_DIM, POOL_GROUP_DIM), lambda bi, i: (0, 0, 0)),
            pl.BlockSpec((1, POOL_WIDTH), lambda bi, i: (0, 0)),
            pl.BlockSpec((1, POOL_WIDTH), lambda bi, i: (0, 0)),
        ],
        out_specs=pl.BlockSpec((1, tq, POOL_WIDTH), lambda bi, i: (bi, i, 0)),
        out_shape=jax.ShapeDtypeStruct((b, seq_len, POOL_WIDTH), BF16),
        compiler_params=_params(2),
        name=f"pool_mixer_{seq_len}",
    )(up, pool_w_b, pool_scale, g_out_pool)


def _stack_heads(q, kh, rows):
    return jnp.concatenate(
        [q[:, (kh * Q_PER_KV + g) * HEAD_DIM:(kh * Q_PER_KV + g + 1) * HEAD_DIM]
         for g in range(Q_PER_KV)], axis=0)


def _sink_column(sink_ref, kh, rows):
    return jnp.concatenate(
        [jnp.broadcast_to(sink_ref[kh * Q_PER_KV + g:kh * Q_PER_KV + g + 1, 0:1], (rows, 1))
         for g in range(Q_PER_KV)], axis=0)


def _ctx_attn_kernel(q_ref, k_ref, v_ref, sink_ref, g_ref, o_ref):
    rows = q_ref.shape[0]
    q = q_ref[...]
    heads = [None] * N_HEADS
    for kh in range(N_KV_HEADS):
        sl = slice(kh * HEAD_DIM, (kh + 1) * HEAD_DIM)
        kb = k_ref[:, sl].astype(BF16)
        vb = v_ref[:, sl].astype(BF16)
        s = _dot_nt(_stack_heads(q, kh, rows), kb)
        sk = _sink_column(sink_ref, kh, rows)
        m = jnp.maximum(jnp.max(s, axis=-1, keepdims=True), sk)
        p = jnp.exp(s - m)
        den = jnp.sum(p, axis=-1, keepdims=True) + jnp.exp(sk - m)
        o = _dot(p.astype(BF16), vb) / den
        for g in range(Q_PER_KV):
            heads[kh * Q_PER_KV + g] = o[g * rows:(g + 1) * rows]
    y = jnp.concatenate(heads, axis=-1)
    o_ref[...] = (_rms(y) * g_ref[...]).astype(BF16)


def _context_attention(q, k, v, sink_b, g_out_attn, seq_len):
    t = q.shape[0]
    return pl.pallas_call(
        _ctx_attn_kernel,
        grid=(t // seq_len,),
        in_specs=[
            pl.BlockSpec((seq_len, ATTN_WIDTH), lambda b: (b, 0)),
            pl.BlockSpec((seq_len, KV_WIDTH), lambda b: (b, 0)),
            pl.BlockSpec((seq_len, KV_WIDTH), lambda b: (b, 0)),
            pl.BlockSpec((N_HEADS, LANES), lambda b: (0, 0)),
            pl.BlockSpec((1, ATTN_WIDTH), lambda b: (0, 0)),
        ],
        out_specs=pl.BlockSpec((seq_len, ATTN_WIDTH), lambda b: (b, 0)),
        out_shape=jax.ShapeDtypeStruct((t, ATTN_WIDTH), BF16),
        compiler_params=_params(1),
        name="context_attention",
    )(q, k, v, sink_b, g_out_attn)


def _lat_attn_kernel(q_ref, k_ref, v_ref, ck_ref, cv_ref, sink_ref, g_ref, o_ref, *, seq_len):
    rows = Q_BLOCK
    span = 3 * Q_BLOCK
    q0 = pl.program_id(1) * rows
    start = pl.multiple_of(jnp.clip(q0 - Q_BLOCK, 0, seq_len - span), Q_BLOCK)
    q = q_ref[...]
    kl = k_ref[0, pl.ds(start, span), :]
    vl = v_ref[0, pl.ds(start, span), :]
    shape = (Q_PER_KV * rows, span)
    qpos = q0 + (lax.broadcasted_iota(jnp.int32, shape, 0) & (rows - 1))
    kpos = start + lax.broadcasted_iota(jnp.int32, shape, 1)
    band = jnp.abs(qpos - kpos) <= WINDOW
    heads = [None] * N_HEADS
    for kh in range(N_KV_HEADS):
        sl = slice(kh * HEAD_DIM, (kh + 1) * HEAD_DIM)
        q4 = _stack_heads(q, kh, rows)
        s_loc = jnp.where(band, _dot_nt(q4, kl[:, sl]), NEG_INF)
        s_ctx = _dot_nt(q4, ck_ref[0, :, sl].astype(BF16))
        sk = _sink_column(sink_ref, kh, rows)
        m = jnp.maximum(jnp.maximum(jnp.max(s_loc, axis=-1, keepdims=True),
                                    jnp.max(s_ctx, axis=-1, keepdims=True)), sk)
        p_loc = jnp.exp(s_loc - m)
        p_ctx = jnp.exp(s_ctx - m)
        den = (jnp.sum(p_loc, axis=-1, keepdims=True) + jnp.sum(p_ctx, axis=-1, keepdims=True)
               + jnp.exp(sk - m))
        o = (_dot(p_loc.astype(BF16), vl[:, sl])
             + _dot(p_ctx.astype(BF16), cv_ref[0, :, sl].astype(BF16))) / den
        for g in range(Q_PER_KV):
            heads[kh * Q_PER_KV + g] = o[g * rows:(g + 1) * rows]
    y = jnp.concatenate(heads, axis=-1)
    o_ref[...] = (_rms(y) * g_ref[...]).astype(BF16)


def _latent_attention(q, k, v, cache_k, cache_v, sink_b, g_out_attn):
    b, seq_len, _ = k.shape
    nq = seq_len // Q_BLOCK
    past = cache_k.shape[1]
    return pl.pallas_call(
        functools.partial(_lat_attn_kernel, seq_len=seq_len),
        grid=(b, nq),
        in_specs=[
            pl.BlockSpec((Q_BLOCK, ATTN_WIDTH), lambda bi, n: (bi * nq + n, 0)),
            pl.BlockSpec((1, seq_len, KV_WIDTH), lambda bi, n: (bi, 0, 0)),
            pl.BlockSpec((1, seq_len, KV_WIDTH), lambda bi, n: (bi, 0, 0)),
            pl.BlockSpec((1, past, KV_WIDTH), lambda bi, n: (bi, 0, 0)),
            pl.BlockSpec((1, past, KV_WIDTH), lambda bi, n: (bi, 0, 0)),
            pl.BlockSpec((N_HEADS, LANES), lambda bi, n: (0, 0)),
            pl.BlockSpec((1, ATTN_WIDTH), lambda bi, n: (0, 0)),
        ],
        out_specs=pl.BlockSpec((Q_BLOCK, ATTN_WIDTH), lambda bi, n: (bi * nq + n, 0)),
        out_shape=jax.ShapeDtypeStruct((b * seq_len, ATTN_WIDTH), BF16),
        compiler_params=_params(2),
        name="latent_attention",
    )(q, k, v, cache_k, cache_v, sink_b, g_out_attn)


def _route(logits):
    lane = lax.broadcasted_iota(jnp.int32, logits.shape, 1).astype(F32)
    neg = -jnp.inf

    def first_argmax(x):
        mx = jnp.max(x, axis=-1, keepdims=True)
        return mx, jnp.min(jnp.where(x == mx, lane, float(ROUTER_LANES)), axis=-1, keepdims=True)

    gl = jnp.where(lane < N_EXPERT_GROUPS, logits, neg)
    gmax, g_idx = first_argmax(gl)
    p_g = 1.0 / jnp.sum(jnp.exp(gl - gmax), axis=-1, keepdims=True)
    base = N_EXPERT_GROUPS + EXPERTS_PER_GROUP * g_idx
    el = jnp.where((lane >= base) & (lane < base + EXPERTS_PER_GROUP), logits, neg)
    v1, i1 = first_argmax(el)
    v2, i2 = first_argmax(jnp.where(lane == i1, neg, el))
    e2 = jnp.exp(v2 - v1)
    w1 = p_g / (1.0 + e2)
    w2 = p_g * e2 / (1.0 + e2)
    ids = jnp.where(lane == 0.0, i1 - N_EXPERT_GROUPS, jnp.where(lane == 1.0, i2 - N_EXPERT_GROUPS, 0.0))
    wts = jnp.where(lane == 0.0, w1, jnp.where(lane == 1.0, w2, 0.0))
    return ids.astype(jnp.int32), wts


def _outproj_kernel(rows_ref, pool_p, attn_p, x_p, pool_s, attn_s, x_s, mod_ref, g_ref, w_ref, rwh_ref,
                    rwl_ref, rb_ref, x1_ref, h2_ref, ids_ref, wts_ref, *, n_ctx_tiles):
    del rows_ref
    i = pl.program_id(0)

    def body(pool_ref, attn_ref, x_ref):
        m = mod_ref[0]
        mix = (_dot(pool_ref[...], w_ref[0:POOL_WIDTH, :])
               + _dot(attn_ref[...], w_ref[POOL_WIDTH:POOL_WIDTH + ATTN_WIDTH, :]))
        x1 = x_ref[...] + m[2:3] * mix
        x1_ref[...] = x1
        h2 = _rms(x1) * g_ref[...] * (1.0 + m[4:5]) + m[3:4]
        h2_ref[...] = h2
        hi = h2.astype(BF16)
        lo = (h2 - hi.astype(F32)).astype(BF16)
        logits = _dot(hi, rwh_ref[...]) + _dot(lo, rwh_ref[...]) + _dot(hi, rwl_ref[...]) + rb_ref[...]
        ids, wts = _route(logits)
        ids_ref[...] = ids
        wts_ref[...] = wts

    @pl.when(i < n_ctx_tiles)
    def _():
        body(pool_p, attn_p, x_p)

    @pl.when(i >= n_ctx_tiles)
    def _():
        body(pool_s, attn_s, x_s)


def _out_projection(ctx, lat, rows, mod3, g_ffn, w_out_b, rw_hi, rw_lo, rb):
    tm = TOKEN_TILE
    n_ctx = ctx[2].shape[0] // tm
    n_lat = lat[2].shape[0] // tm
    t_all = (n_ctx + n_lat) * tm

    def ctx_map(i, r):
        return (jnp.minimum(i, n_ctx - 1), 0)

    def lat_map(i, r):
        return (jnp.maximum(i - n_ctx, 0), 0)

    in_specs = [
        pl.BlockSpec((tm, POOL_WIDTH), ctx_map),
        pl.BlockSpec((tm, ATTN_WIDTH), ctx_map),
        pl.BlockSpec((tm, D_MODEL), ctx_map),
        pl.BlockSpec((tm, POOL_WIDTH), lat_map),
        pl.BlockSpec((tm, ATTN_WIDTH), lat_map),
        pl.BlockSpec((tm, D_MODEL), lat_map),
        pl.BlockSpec((1, 6, D_MODEL), lambda i, r: (r[i], 0, 0)),
        pl.BlockSpec((1, D_MODEL), lambda i, r: (0, 0)),
        pl.BlockSpec((D_MODEL, D_MODEL), lambda i, r: (0, 0), pipeline_mode=pl.Buffered(1)),
        pl.BlockSpec((D_MODEL, ROUTER_LANES), lambda i, r: (0, 0)),
        pl.BlockSpec((D_MODEL, ROUTER_LANES), lambda i, r: (0, 0)),
        pl.BlockSpec((1, ROUTER_LANES), lambda i, r: (0, 0)),
    ]
    out_shape = (
        jax.ShapeDtypeStruct((t_all, D_MODEL), F32),
        jax.ShapeDtypeStruct((t_all, D_MODEL), F32),
        jax.ShapeDtypeStruct((t_all, ROUTER_LANES), jnp.int32),
        jax.ShapeDtypeStruct((t_all, ROUTER_LANES), F32),
    )
    out_specs = (
        pl.BlockSpec((tm, D_MODEL), lambda i, r: (i, 0)),
        pl.BlockSpec((tm, D_MODEL), lambda i, r: (i, 0)),
        pl.BlockSpec((tm, ROUTER_LANES), lambda i, r: (i, 0)),
        pl.BlockSpec((tm, ROUTER_LANES), lambda i, r: (i, 0)),
    )
    return pl.pallas_call(
        functools.partial(_outproj_kernel, n_ctx_tiles=n_ctx),
        grid_spec=pltpu.PrefetchScalarGridSpec(
            num_scalar_prefetch=1, grid=(n_ctx + n_lat,), in_specs=in_specs, out_specs=out_specs),
        out_shape=out_shape,
        compiler_params=_params(1),
        name="out_projection",
    )(rows, *ctx, *lat, mod3, g_ffn, w_out_b, rw_hi, rw_lo, rb)


def _gather_rows(idx_ref, base, n_rows, src_hbm, dst, sem, *, unrolled):
    def one(r):
        tok = idx_ref[base + r]
        pltpu.make_async_copy(src_hbm.at[pl.ds(tok, 1)], dst.at[pl.ds(r, 1)], sem).start()

    if unrolled:
        for r in range(n_rows):
            one(r)
    else:
        def body(r, carry):
            one(r)
            return carry
        lax.fori_loop(0, n_rows, body, 0, unroll=8)


def _wait_rows(src_hbm, dst, sem):
    pltpu.make_async_copy(src_hbm.at[pl.ds(0, dst.shape[0])], dst, sem).wait()


def _moe_kernel(be_ref, nu_ref, tok_ref, h_hbm, wg_ref, wu_ref, wd_ref, y_ref,
                xg0, xg1, wgb, wub, wdb, sem):
    b = pl.program_id(0)
    n_used = nu_ref[0]
    last = n_used - 1
    active = b < n_used

    @pl.when(b == 0)
    def _():
        _gather_rows(tok_ref, 0, EXPERT_ROWS, h_hbm, xg0, sem.at[0], unrolled=False)

    new_expert = (b == 0) | (be_ref[b] != be_ref[jnp.maximum(b - 1, 0)])

    @pl.when(active & new_expert)
    def _():
        wgb[...] = wg_ref[0].astype(BF16)
        wub[...] = wu_ref[0].astype(BF16)
        wdb[...] = wd_ref[0].astype(BF16)

    def step(cur, nxt, sem_cur, sem_nxt):
        _wait_rows(h_hbm, cur, sem_cur)
        _gather_rows(tok_ref, jnp.minimum(b + 1, last) * EXPERT_ROWS, EXPERT_ROWS, h_hbm, nxt, sem_nxt,
                     unrolled=True)
        xb = cur[...].astype(BF16)
        a = _dot(xb, wgb[...])
        u = _dot(xb, wub[...])
        act = (a / (1.0 + jnp.exp(-a)) * u).astype(BF16)
        y_ref[...] = _dot(act, wdb[...])

        @pl.when(b == last)
        def _():
            _wait_rows(h_hbm, nxt, sem_nxt)

    @pl.when(active & (b % 2 == 0))
    def _():
        step(xg0, xg1, sem.at[0], sem.at[1])

    @pl.when(active & (b % 2 == 1))
    def _():
        step(xg1, xg0, sem.at[1], sem.at[0])

    @pl.when(b >= n_used)
    def _():
        y_ref[...] = jnp.zeros_like(y_ref)


def _experts(block_expert, n_used, buf_tok, h2_all, w_gate, w_up, w_down):
    cap = buf_tok.shape[0]
    bm = EXPERT_ROWS
    return pl.pallas_call(
        _moe_kernel,
        grid_spec=pltpu.PrefetchScalarGridSpec(
            num_scalar_prefetch=3,
            grid=(cap // bm,),
            in_specs=[
                pl.BlockSpec(memory_space=pl.ANY),
                pl.BlockSpec((1, D_MODEL, D_EXPERT), lambda b, be, nu, tok: (be[b], 0, 0)),
                pl.BlockSpec((1, D_MODEL, D_EXPERT), lambda b, be, nu, tok: (be[b], 0, 0)),
                pl.BlockSpec((1, D_EXPERT, D_MODEL), lambda b, be, nu, tok: (be[b], 0, 0)),
            ],
            out_specs=pl.BlockSpec((bm, D_MODEL), lambda b, be, nu, tok: (b, 0)),
            scratch_shapes=[
                pltpu.VMEM((bm, D_MODEL), F32), pltpu.VMEM((bm, D_MODEL), F32),
                pltpu.VMEM((D_MODEL, D_EXPERT), BF16), pltpu.VMEM((D_MODEL, D_EXPERT), BF16),
                pltpu.VMEM((D_EXPERT, D_MODEL), BF16),
                pltpu.SemaphoreType.DMA((2,)),
            ],
        ),
        out_shape=jax.ShapeDtypeStruct((cap, D_MODEL), F32),
        compiler_params=pltpu.CompilerParams(
            dimension_semantics=("arbitrary",), vmem_limit_bytes=EXPERT_VMEM_LIMIT),
        name="experts",
    )(block_expert, n_used, buf_tok, h2_all, w_gate, w_up, w_down)


def _combine_kernel(slot_ref, rows_ref, y_hbm, x1_ref, wts_ref, mod_ref, g_ref, op_ref, os_ref, yg0, yg1, sem,
                    *, n_ctx_tiles, n_tiles):
    del rows_ref
    tm = TOKEN_TILE
    i = pl.program_id(0)
    n_slots = slot_ref.shape[0] // TOP_K

    def start(tile, buf, s, unrolled):
        for k in range(TOP_K):
            _gather_rows(slot_ref, k * n_slots + tile * tm, tm, y_hbm, buf.at[k], s, unrolled=unrolled)

    @pl.when(i == 0)
    def _():
        start(0, yg0, sem.at[0], False)

    def step(cur, nxt, sem_cur, sem_nxt):
        for k in range(TOP_K):
            _wait_rows(y_hbm, cur.at[k], sem_cur)
        start(jnp.minimum(i + 1, n_tiles - 1), nxt, sem_nxt, True)
        w = wts_ref[...]
        ffn = cur[0] * w[:, 0:1] + cur[1] * w[:, 1:2]
        x2 = x1_ref[...] + mod_ref[0][5:6] * ffn
        out = _rms(x2) * g_ref[...]

        @pl.when(i < n_ctx_tiles)
        def _():
            op_ref[...] = out

        @pl.when(i >= n_ctx_tiles)
        def _():
            os_ref[...] = out

        @pl.when(i == n_tiles - 1)
        def _():
            for k in range(TOP_K):
                _wait_rows(y_hbm, nxt.at[k], sem_nxt)

    @pl.when(i % 2 == 0)
    def _():
        step(yg0, yg1, sem.at[0], sem.at[1])

    @pl.when(i % 2 == 1)
    def _():
        step(yg1, yg0, sem.at[1], sem.at[0])


def _combine(slots, rows, y_buf, x1, wts, mod3, g_final, t_ctx):
    t = x1.shape[0]
    tm = TOKEN_TILE
    n_tiles = t // tm
    n_ctx = t_ctx // tm
    return pl.pallas_call(
        functools.partial(_combine_kernel, n_ctx_tiles=n_ctx, n_tiles=n_tiles),
        grid_spec=pltpu.PrefetchScalarGridSpec(
            num_scalar_prefetch=2,
            grid=(n_tiles,),
            in_specs=[
                pl.BlockSpec(memory_space=pl.ANY),
                pl.BlockSpec((tm, D_MODEL), lambda i, s, r: (i, 0)),
                pl.BlockSpec((tm, ROUTER_LANES), lambda i, s, r: (i, 0)),
                pl.BlockSpec((1, 6, D_MODEL), lambda i, s, r: (r[i], 0, 0)),
                pl.BlockSpec((1, D_MODEL), lambda i, s, r: (0, 0)),
            ],
            out_specs=(
                pl.BlockSpec((tm, D_MODEL), lambda i, s, r: (jnp.minimum(i, n_ctx - 1), 0)),
                pl.BlockSpec((tm, D_MODEL), lambda i, s, r: (jnp.maximum(i - n_ctx, 0), 0)),
            ),
            scratch_shapes=[pltpu.VMEM((TOP_K, tm, D_MODEL), F32), pltpu.VMEM((TOP_K, tm, D_MODEL), F32),
                            pltpu.SemaphoreType.DMA((2,))],
        ),
        out_shape=(jax.ShapeDtypeStruct((t_ctx, D_MODEL), F32),
                   jax.ShapeDtypeStruct((t - t_ctx, D_MODEL), F32)),
        compiler_params=_params(1),
        name="combine",
    )(slots, rows, y_buf, x1, wts, mod3, g_final)


def _rope_tables(n_tokens):
    rows = n_tokens // GRID_W
    row = jnp.repeat(jnp.arange(rows, dtype=F32), GRID_W)
    col = jnp.tile(jnp.arange(GRID_W, dtype=F32), rows)
    half = ROPE_AXIS_DIM // 2
    inv_freq = ROPE_BASE ** (-jnp.arange(half, dtype=F32) / half)
    ar = row[:, None] * inv_freq
    ac = col[:, None] * inv_freq
    cos = jnp.concatenate([jnp.cos(ar), jnp.cos(ar), jnp.cos(ac), jnp.cos(ac)], axis=-1)
    sin = jnp.concatenate([-jnp.sin(ar), jnp.sin(ar), -jnp.sin(ac), jnp.sin(ac)], axis=-1)
    return cos, sin


def _dispatch_tables(ids):
    t = ids.shape[0]
    bm = EXPERT_ROWS
    flat_e = ids.reshape(t * TOP_K)
    onehot = (flat_e[:, None] == jnp.arange(N_EXPERTS, dtype=jnp.int32)[None, :]).astype(jnp.int32)
    running = jnp.cumsum(onehot, axis=0)
    rank = jnp.sum(running * onehot, axis=1) - 1
    counts = running[-1]
    padded = (counts + bm - 1) // bm * bm
    pad_end = jnp.cumsum(padded)
    pad_start = pad_end - padded
    dest = (jnp.sum(pad_start[None, :] * onehot, axis=1) + rank).astype(jnp.int32)
    n_blocks = (t * TOP_K + N_EXPERTS * (bm - 1) + bm - 1) // bm
    flat_tok = jnp.repeat(jnp.arange(t, dtype=jnp.int32), TOP_K)
    buf_tok = jnp.zeros((n_blocks * bm,), jnp.int32).at[dest].set(flat_tok)
    n_used = (pad_end[-1] // bm).astype(jnp.int32)
    blk = jnp.arange(n_blocks, dtype=jnp.int32)
    blk = jnp.minimum(blk, n_used - 1)
    block_expert = jnp.sum((pad_end[None, :] <= (blk * bm)[:, None]).astype(jnp.int32), axis=1)
    block_expert = jnp.minimum(block_expert, N_EXPERTS - 1)
    slots = dest.reshape(t, TOP_K).T.reshape(TOP_K * t)
    return block_expert.astype(jnp.int32), n_used.reshape(1), buf_tok, slots


def kernel(x_prompt, x_sample, cache_k, cache_v, c, c_ctx, w_mod, b_mod, norm_mix_g, norm_ffn_g, w_in,
           pool_w, pool_scale, attn_sink, out_norm_pool_g, out_norm_attn_g, w_out, router_group_w,
           router_group_b, router_expert_w, router_expert_b, w_gate, w_up, w_down, final_norm_g):
    depth = w_mod.shape[0]
    assert depth == 1, "single trunk layer"
    bp, lp, _ = x_prompt.shape
    bs, ls, _ = x_sample.shape
    tp, ts = bp * lp, bs * ls
    tm = TOKEN_TILE
    l = 0

    cond8 = jnp.zeros((8, D_MODEL), F32).at[:bs].set(c).at[bs].set(c_ctx)
    mod3 = _modulation(cond8, w_mod[l], b_mod[l]).reshape(8, 6, D_MODEL)
    rows_p = jnp.full((tp // tm,), bs, jnp.int32)
    rows_s = jnp.arange(ts // tm, dtype=jnp.int32) // (ls // tm)

    w_in_b = w_in[l].astype(BF16)
    w_out_b = w_out[l].astype(BF16)
    pool_w_b = pool_w[l].astype(BF16)
    g_mix = norm_mix_g[l].reshape(1, D_MODEL)
    g_ffn = norm_ffn_g[l].reshape(1, D_MODEL)
    g_pool = out_norm_pool_g[l].reshape(1, POOL_WIDTH)
    g_attn = out_norm_attn_g[l].reshape(1, ATTN_WIDTH)
    p_scale = pool_scale[l].reshape(1, POOL_WIDTH)
    sink_b = jnp.broadcast_to(attn_sink[l][:, None], (N_HEADS, LANES))
    g_final = final_norm_g.reshape(1, D_MODEL)

    rw = jnp.concatenate(
        [router_group_w[l], jnp.transpose(router_expert_w[l], (1, 0, 2)).reshape(D_MODEL, N_EXPERTS)], axis=1)
    rw = jnp.pad(rw, ((0, 0), (0, ROUTER_LANES - rw.shape[1])))
    rw_hi = rw.astype(BF16)
    rw_lo = (rw - rw_hi.astype(F32)).astype(BF16)
    rb = jnp.concatenate([router_group_b[l], router_expert_b[l].reshape(N_EXPERTS)])
    rb = jnp.pad(rb, (0, ROUTER_LANES - rb.shape[0])).reshape(1, ROUTER_LANES)

    xp = x_prompt.reshape(tp, D_MODEL)
    xs = x_sample.reshape(ts, D_MODEL)

    up_p, q_p, k_p, v_p = _in_projection(xp, rows_p, mod3, g_mix, w_in_b, None, F32)
    pool_p = _pool_mixer(up_p.reshape(bp, lp, POOL_WIDTH), pool_w_b, p_scale, g_pool).reshape(tp, POOL_WIDTH)
    attn_p = _context_attention(q_p, k_p, v_p, sink_b, g_attn, lp)
    up_s, q_s, k_s, v_s = _in_projection(xs, rows_s, mod3, g_mix, w_in_b, _rope_tables(ls), BF16)
    pool_s = _pool_mixer(up_s.reshape(bs, ls, POOL_WIDTH), pool_w_b, p_scale, g_pool).reshape(ts, POOL_WIDTH)
    attn_s = _latent_attention(
        q_s, k_s.reshape(bs, ls, KV_WIDTH), v_s.reshape(bs, ls, KV_WIDTH),
        cache_k[:, l].reshape(bs, -1, KV_WIDTH), cache_v[:, l].reshape(bs, -1, KV_WIDTH), sink_b, g_attn)

    rows_all = jnp.concatenate([rows_p, rows_s])
    x1_all, h2_all, ids_all, wts_all = _out_projection(
        (pool_p, attn_p, xp), (pool_s, attn_s, xs), rows_all, mod3, g_ffn, w_out_b, rw_hi, rw_lo, rb)

    block_expert, n_used, buf_tok, slots = _dispatch_tables(ids_all[:, :TOP_K])
    y_buf = _experts(block_expert, n_used, buf_tok, h2_all, w_gate[l], w_up[l], w_down[l])

    y_p, y_s = _combine(slots, rows_all, y_buf, x1_all, wts_all, mod3, g_final, tp)

    kv_shape = (bp, 1, lp, N_KV_HEADS, HEAD_DIM)
    return (y_p.reshape(bp, lp, D_MODEL), y_s.reshape(bs, ls, D_MODEL),
            k_p.reshape(kv_shape), v_p.reshape(kv_shape))
```

```python
import functools

import jax
import jax.numpy as jnp
from jax import lax
from jax.experimental import pallas as pl
from jax.experimental.pallas import tpu as pltpu

F32 = jnp.float32
BF16 = jnp.bfloat16

D_MODEL = 2048
GRID_W = 64
HEAD_DIM = 128
ATTN_WIDTH = 1024
POOL_WIDTH = 1024
N_HEADS = 8
N_KV_HEADS = 2
Q_PER_KV = 4
KV_WIDTH = 256
IN_WIDTH = 2560
WINDOW = 128
POOL_WINDOWS = (2, 4, 8, 16)
POOL_GROUP_DIM = 256
ROPE_BASE = 10000.0
ROPE_AXIS_DIM = 64
N_EXPERT_GROUPS = 4
EXPERTS_PER_GROUP = 8
N_EXPERTS = 32
TOP_K = 2
D_EXPERT = 512
NORM_EPS = 1e-6
NEG_INF = -1e30
ATTN_SCALE = HEAD_DIM ** -0.5

LANES = 128
VMEM_LIMIT = 48 * 1024 * 1024
PACKED = D_MODEL // 2
TOKEN_TILE = 256
POOL_HALO = 16
Q_BLOCK = 128
EXPERT_ROWS = 256
ROUTER_LANES = 128


def _params(n_grid):
    return pltpu.CompilerParams(
        dimension_semantics=("arbitrary",) * n_grid, vmem_limit_bytes=VMEM_LIMIT)


def _rms(x):
    return x * lax.rsqrt(jnp.mean(x * x, axis=-1, keepdims=True) + NORM_EPS)


def _dot(a, b):
    return jnp.dot(a, b, preferred_element_type=F32)


def _dot_nt(a, b):
    return lax.dot_general(a, b, (((1,), (1,)), ((), ())), preferred_element_type=F32)


def _pack_bf16_pairs(x):
    c = x.shape[1] // 2
    lo = lax.bitcast_convert_type(x[:, :c].astype(BF16).astype(F32), jnp.uint32)
    hi = lax.bitcast_convert_type(x[:, c:].astype(BF16).astype(F32), jnp.uint32)
    return hi | (lo >> 16)


def _unpack_bf16_pairs(u):
    lo = lax.bitcast_convert_type(u << 16, F32)
    hi = lax.bitcast_convert_type(u & jnp.uint32(0xFFFF0000), F32)
    return lo, hi


def _mod_kernel(cond_ref, w_ref, b_ref, o_ref):
    c = cond_ref[...]
    s = c / (1.0 + jnp.exp(-c))
    o_ref[...] = _dot(s.astype(BF16), w_ref[...].astype(BF16)) + b_ref[...]


def _modulation(cond8, w_mod, b_mod):
    n = w_mod.shape[1]
    tn = 1024
    return pl.pallas_call(
        _mod_kernel,
        grid=(n // tn,),
        in_specs=[
            pl.BlockSpec((8, D_MODEL), lambda j: (0, 0)),
            pl.BlockSpec((D_MODEL, tn), lambda j: (0, j)),
            pl.BlockSpec((1, tn), lambda j: (0, j)),
        ],
        out_specs=pl.BlockSpec((8, tn), lambda j: (0, j)),
        out_shape=jax.ShapeDtypeStruct((8, n), F32),
        compiler_params=_params(1),
        name="modulation",
    )(cond8, w_mod, b_mod.reshape(1, n))


def _rope(x, cos, sin_signed):
    lane = lax.broadcasted_iota(jnp.int32, x.shape, 1)
    partner = jnp.where((lane & 63) < 32,
                        pltpu.roll(x, HEAD_DIM - 32, axis=1),
                        pltpu.roll(x, 32, axis=1))
    return x * cos + partner * sin_signed


def _inproj_kernel(rows_ref, x_ref, mod_ref, g_ref, w_ref, *rest, rope):
    del rows_ref
    if rope:
        cos_ref, sin_ref, up_ref, q_ref, k_ref, v_ref = rest
    else:
        up_ref, q_ref, k_ref, v_ref = rest
    m = mod_ref[0]
    h = _rms(x_ref[...]) * g_ref[...]
    hb = (h * (1.0 + m[1:2]) + m[0:1]).astype(BF16)
    up_ref[...] = _dot(hb, w_ref[:, 0:POOL_WIDTH]).astype(BF16)
    q = _dot(hb, w_ref[:, POOL_WIDTH:POOL_WIDTH + ATTN_WIDTH])
    k = _dot(hb, w_ref[:, POOL_WIDTH + ATTN_WIDTH:IN_WIDTH - KV_WIDTH])
    v = _dot(hb, w_ref[:, IN_WIDTH - KV_WIDTH:IN_WIDTH])
    if rope:
        cos = cos_ref[...]
        sin = sin_ref[...]
        for hd in range(N_HEADS):
            sl = slice(hd * HEAD_DIM, (hd + 1) * HEAD_DIM)
            q_ref[:, sl] = (_rope(q[:, sl], cos, sin) * ATTN_SCALE).astype(BF16)
        for hd in range(N_KV_HEADS):
            sl = slice(hd * HEAD_DIM, (hd + 1) * HEAD_DIM)
            k_ref[:, sl] = _rope(k[:, sl], cos, sin).astype(k_ref.dtype)
    else:
        q_ref[...] = (q * ATTN_SCALE).astype(BF16)
        k_ref[...] = k.astype(k_ref.dtype)
    v_ref[...] = v.astype(v_ref.dtype)


def _in_projection(x, rows, mod3, g_mix, w_in_b, rope_tabs, kv_dtype):
    t = x.shape[0]
    tm = TOKEN_TILE
    rope = rope_tabs is not None
    in_specs = [
        pl.BlockSpec((tm, D_MODEL), lambda i, r: (i, 0)),
        pl.BlockSpec((1, 6, D_MODEL), lambda i, r: (r[i], 0, 0)),
        pl.BlockSpec((1, D_MODEL), lambda i, r: (0, 0)),
        pl.BlockSpec((D_MODEL, IN_WIDTH), lambda i, r: (0, 0), pipeline_mode=pl.Buffered(1)),
    ]
    args = [x, mod3, g_mix, w_in_b]
    if rope:
        seq_tiles = rope_tabs[0].shape[0] // tm
        for tab in rope_tabs:
            in_specs.append(pl.BlockSpec((tm, HEAD_DIM), lambda i, r: (i % seq_tiles, 0)))
            args.append(tab)
    out_shape = (
        jax.ShapeDtypeStruct((t, POOL_WIDTH), BF16),
        jax.ShapeDtypeStruct((t, ATTN_WIDTH), BF16),
        jax.ShapeDtypeStruct((t, KV_WIDTH), kv_dtype),
        jax.ShapeDtypeStruct((t, KV_WIDTH), kv_dtype),
    )
    out_specs = (
        pl.BlockSpec((tm, POOL_WIDTH), lambda i, r: (i, 0)),
        pl.BlockSpec((tm, ATTN_WIDTH), lambda i, r: (i, 0)),
        pl.BlockSpec((tm, KV_WIDTH), lambda i, r: (i, 0)),
        pl.BlockSpec((tm, KV_WIDTH), lambda i, r: (i, 0)),
    )
    return pl.pallas_call(
        functools.partial(_inproj_kernel, rope=rope),
        grid_spec=pltpu.PrefetchScalarGridSpec(
            num_scalar_prefetch=1, grid=(t // tm,), in_specs=in_specs, out_specs=out_specs),
        out_shape=out_shape,
        compiler_params=_params(1),
        name="in_projection_rope" if rope else "in_projection",
    )(rows, *args)


def _pool_kernel(u_ref, pw_ref, ps_ref, g_ref, o_ref, *, seq_len):
    tq = TOKEN_TILE
    win = min(seq_len, tq + 2 * POOL_HALO)
    t0 = pl.program_id(1) * tq
    src0 = pl.multiple_of(jnp.clip(t0 - POOL_HALO, 0, seq_len - win), POOL_HALO)
    u = u_ref[0, pl.ds(src0, win), :]
    t = t0 + lax.broadcasted_iota(jnp.int32, (tq, win), 0)
    j = src0 + lax.broadcasted_iota(jnp.int32, (tq, win), 1)
    tc = t0 + lax.broadcasted_iota(jnp.int32, (tq, 1), 0)
    ys = []
    for gi, w in enumerate(POOL_WINDOWS):
        lo = jnp.maximum(t - w // 2, 0)
        hi = jnp.minimum(t + (w - w // 2), seq_len)
        cnt = (hi - lo).astype(F32)
        a = jnp.where((j >= lo) & (j < hi), 1.0, 0.0) - jnp.where(j == t, cnt, 0.0)
        cnt_col = (jnp.minimum(tc + (w - w // 2), seq_len) - jnp.maximum(tc - w // 2, 0)).astype(F32)
        sl = slice(gi * POOL_GROUP_DIM, (gi + 1) * POOL_GROUP_DIM)
        d = _dot(a.astype(BF16), u[:, sl]) / cnt_col
        ys.append(_dot(d.astype(BF16), pw_ref[gi]))
    y = jnp.concatenate(ys, axis=-1) * ps_ref[...]
    o_ref[0] = (_rms(y) * g_ref[...]).astype(BF16)


def _pool_mixer(up, pool_w_b, pool_scale, g_out_pool):
    b, seq_len, _ = up.shape
    tq = TOKEN_TILE
    return pl.pallas_call(
        functools.partial(_pool_kernel, seq_len=seq_len),
        grid=(b, seq_len // tq),
        in_specs=[
            pl.BlockSpec((1, seq_len, POOL_WIDTH), lambda bi, i: (bi, 0, 0)),
            pl.BlockSpec((len(POOL_WINDOWS), POOL_GROUP_DIM, POOL_GROUP_DIM), lambda bi, i: (0, 0, 0)),
            pl.BlockSpec((1, POOL_WIDTH), lambda bi, i: (0, 0)),
            pl.BlockSpec((1, POOL_WIDTH), lambda bi, i: (0, 0)),
        ],
        out_specs=pl.BlockSpec((1, tq, POOL_WIDTH), lambda bi, i: (bi, i, 0)),
        out_shape=jax.ShapeDtypeStruct((b, seq_len, POOL_WIDTH), BF16),
        compiler_params=_params(2),
        name=f"pool_mixer_{seq_len}",
    )(up, pool_w_b, pool_scale, g_out_pool)


def _stack_heads(q, kh, rows):
    return jnp.concatenate(
        [q[:, (kh * Q_PER_KV + g) * HEAD_DIM:(kh * Q_PER_KV + g + 1) * HEAD_DIM]
         for g in range(Q_PER_KV)], axis=0)


def _sink_column(sink_ref, kh, rows):
    return jnp.concatenate(
        [jnp.broadcast_to(sink_ref[kh * Q_PER_KV + g:kh * Q_PER_KV + g + 1, 0:1], (rows, 1))
         for g in range(Q_PER_KV)], axis=0)


def _ctx_attn_kernel(q_ref, k_ref, v_ref, sink_ref, g_ref, o_ref):
    rows = q_ref.shape[0]
    q = q_ref[...]
    heads = [None] * N_HEADS
    for kh in range(N_KV_HEADS):
        sl = slice(kh * HEAD_DIM, (kh + 1) * HEAD_DIM)
        kb = k_ref[:, sl].astype(BF16)
        vb = v_ref[:, sl].astype(BF16)
        s = _dot_nt(_stack_heads(q, kh, rows), kb)
        sk = _sink_column(sink_ref, kh, rows)
        m = jnp.maximum(jnp.max(s, axis=-1, keepdims=True), sk)
        p = jnp.exp(s - m)
        den = jnp.sum(p, axis=-1, keepdims=True) + jnp.exp(sk - m)
        o = _dot(p.astype(BF16), vb) / den
        for g in range(Q_PER_KV):
            heads[kh * Q_PER_KV + g] = o[g * rows:(g + 1) * rows]
    y = jnp.concatenate(heads, axis=-1)
    o_ref[...] = (_rms(y) * g_ref[...]).astype(BF16)


def _context_attention(q, k, v, sink_b, g_out_attn, seq_len):
    t = q.shape[0]
    return pl.pallas_call(
        _ctx_attn_kernel,
        grid=(t // seq_len,),
        in_specs=[
            pl.BlockSpec((seq_len, ATTN_WIDTH), lambda b: (b, 0)),
            pl.BlockSpec((seq_len, KV_WIDTH), lambda b: (b, 0)),
            pl.BlockSpec((seq_len, KV_WIDTH), lambda b: (b, 0)),
            pl.BlockSpec((N_HEADS, LANES), lambda b: (0, 0)),
            pl.BlockSpec((1, ATTN_WIDTH), lambda b: (0, 0)),
        ],
        out_specs=pl.BlockSpec((seq_len, ATTN_WIDTH), lambda b: (b, 0)),
        out_shape=jax.ShapeDtypeStruct((t, ATTN_WIDTH), BF16),
        compiler_params=_params(1),
        name="context_attention",
    )(q, k, v, sink_b, g_out_attn)


def _lat_attn_kernel(q_ref, k_ref, v_ref, ck_ref, cv_ref, sink_ref, g_ref, o_ref, *, seq_len):
    rows = Q_BLOCK
    span = 3 * Q_BLOCK
    q0 = pl.program_id(1) * rows
    start = pl.multiple_of(jnp.clip(q0 - Q_BLOCK, 0, seq_len - span), Q_BLOCK)
    q = q_ref[...]
    kl = k_ref[0, pl.ds(start, span), :]
    vl = v_ref[0, pl.ds(start, span), :]
    shape = (Q_PER_KV * rows, span)
    qpos = q0 + (lax.broadcasted_iota(jnp.int32, shape, 0) & (rows - 1))
    kpos = start + lax.broadcasted_iota(jnp.int32, shape, 1)
    band = jnp.abs(qpos - kpos) <= WINDOW
    heads = [None] * N_HEADS
    for kh in range(N_KV_HEADS):
        sl = slice(kh * HEAD_DIM, (kh + 1) * HEAD_DIM)
        q4 = _stack_heads(q, kh, rows)
        s_loc = jnp.where(band, _dot_nt(q4, kl[:, sl]), NEG_INF)
        s_ctx = _dot_nt(q4, ck_ref[0, :, sl].astype(BF16))
        sk = _sink_column(sink_ref, kh, rows)
        m = jnp.maximum(jnp.maximum(jnp.max(s_loc, axis=-1, keepdims=True),
                                    jnp.max(s_ctx, axis=-1, keepdims=True)), sk)
        p_loc = jnp.exp(s_loc - m)
        p_ctx = jnp.exp(s_ctx - m)
        den = (jnp.sum(p_loc, axis=-1, keepdims=True) + jnp.sum(p_ctx, axis=-1, keepdims=True)
               + jnp.exp(sk - m))
        o = (_dot(p_loc.astype(BF16), vl[:, sl])
             + _dot(p_ctx.astype(BF16), cv_ref[0, :, sl].astype(BF16))) / den
        for g in range(Q_PER_KV):
            heads[kh * Q_PER_KV + g] = o[g * rows:(g + 1) * rows]
    y = jnp.concatenate(heads, axis=-1)
    o_ref[...] = (_rms(y) * g_ref[...]).astype(BF16)


def _latent_attention(q, k, v, cache_k, cache_v, sink_b, g_out_attn):
    b, seq_len, _ = k.shape
    nq = seq_len // Q_BLOCK
    past = cache_k.shape[1]
    return pl.pallas_call(
        functools.partial(_lat_attn_kernel, seq_len=seq_len),
        grid=(b, nq),
        in_specs=[
            pl.BlockSpec((Q_BLOCK, ATTN_WIDTH), lambda bi, n: (bi * nq + n, 0)),
            pl.BlockSpec((1, seq_len, KV_WIDTH), lambda bi, n: (bi, 0, 0)),
            pl.BlockSpec((1, seq_len, KV_WIDTH), lambda bi, n: (bi, 0, 0)),
            pl.BlockSpec((1, past, KV_WIDTH), lambda bi, n: (bi, 0, 0)),
            pl.BlockSpec((1, past, KV_WIDTH), lambda bi, n: (bi, 0, 0)),
            pl.BlockSpec((N_HEADS, LANES), lambda bi, n: (0, 0)),
            pl.BlockSpec((1, ATTN_WIDTH), lambda bi, n: (0, 0)),
        ],
        out_specs=pl.BlockSpec((Q_BLOCK, ATTN_WIDTH), lambda bi, n: (bi * nq + n, 0)),
        out_shape=jax.ShapeDtypeStruct((b * seq_len, ATTN_WIDTH), BF16),
        compiler_params=_params(2),
        name="latent_attention",
    )(q, k, v, cache_k, cache_v, sink_b, g_out_attn)


def _route(logits):
    lane = lax.broadcasted_iota(jnp.int32, logits.shape, 1).astype(F32)
    neg = -jnp.inf

    def first_argmax(x):
        mx = jnp.max(x, axis=-1, keepdims=True)
        return mx, jnp.min(jnp.where(x == mx, lane, float(ROUTER_LANES)), axis=-1, keepdims=True)

    gl = jnp.where(lane < N_EXPERT_GROUPS, logits, neg)
    gmax, g_idx = first_argmax(gl)
    p_g = 1.0 / jnp.sum(jnp.exp(gl - gmax), axis=-1, keepdims=True)
    base = N_EXPERT_GROUPS + EXPERTS_PER_GROUP * g_idx
    el = jnp.where((lane >= base) & (lane < base + EXPERTS_PER_GROUP), logits, neg)
    v1, i1 = first_argmax(el)
    v2, i2 = first_argmax(jnp.where(lane == i1, neg, el))
    e2 = jnp.exp(v2 - v1)
    w1 = p_g / (1.0 + e2)
    w2 = p_g * e2 / (1.0 + e2)
    ids = jnp.where(lane == 0.0, i1 - N_EXPERT_GROUPS, jnp.where(lane == 1.0, i2 - N_EXPERT_GROUPS, 0.0))
    wts = jnp.where(lane == 0.0, w1, jnp.where(lane == 1.0, w2, 0.0))
    return ids.astype(jnp.int32), wts


def _outproj_kernel(rows_ref, pool_p, attn_p, x_p, pool_s, attn_s, x_s, mod_ref, g_ref, w_ref, rwh_ref,
                    rwl_ref, rb_ref, x1_ref, h2_ref, ids_ref, wts_ref, *, n_ctx_tiles):
    del rows_ref
    i = pl.program_id(0)

    def body(pool_ref, attn_ref, x_ref):
        m = mod_ref[0]
        mix = (_dot(pool_ref[...], w_ref[0:POOL_WIDTH, :])
               + _dot(attn_ref[...], w_ref[POOL_WIDTH:POOL_WIDTH + ATTN_WIDTH, :]))
        x1 = x_ref[...] + m[2:3] * mix
        x1_ref[...] = x1
        h2 = _rms(x1) * g_ref[...] * (1.0 + m[4:5]) + m[3:4]
        h2_ref[...] = _pack_bf16_pairs(h2)
        hi = h2.astype(BF16)
        lo = (h2 - hi.astype(F32)).astype(BF16)
        logits = _dot(hi, rwh_ref[...]) + _dot(lo, rwh_ref[...]) + _dot(hi, rwl_ref[...]) + rb_ref[...]
        ids, wts = _route(logits)
        ids_ref[...] = ids
        wts_ref[...] = wts

    @pl.when(i < n_ctx_tiles)
    def _():
        body(pool_p, attn_p, x_p)

    @pl.when(i >= n_ctx_tiles)
    def _():
        body(pool_s, attn_s, x_s)


def _out_projection(ctx, lat, rows, mod3, g_ffn, w_out_b, rw_hi, rw_lo, rb):
    tm = TOKEN_TILE
    n_ctx = ctx[2].shape[0] // tm
    n_lat = lat[2].shape[0] // tm
    t_all = (n_ctx + n_lat) * tm

    def ctx_map(i, r):
        return (jnp.minimum(i, n_ctx - 1), 0)

    def lat_map(i, r):
        return (jnp.maximum(i - n_ctx, 0), 0)

    in_specs = [
        pl.BlockSpec((tm, POOL_WIDTH), ctx_map),
        pl.BlockSpec((tm, ATTN_WIDTH), ctx_map),
        pl.BlockSpec((tm, D_MODEL), ctx_map),
        pl.BlockSpec((tm, POOL_WIDTH), lat_map),
        pl.BlockSpec((tm, ATTN_WIDTH), lat_map),
        pl.BlockSpec((tm, D_MODEL), lat_map),
        pl.BlockSpec((1, 6, D_MODEL), lambda i, r: (r[i], 0, 0)),
        pl.BlockSpec((1, D_MODEL), lambda i, r: (0, 0)),
        pl.BlockSpec((D_MODEL, D_MODEL), lambda i, r: (0, 0), pipeline_mode=pl.Buffered(1)),
        pl.BlockSpec((D_MODEL, ROUTER_LANES), lambda i, r: (0, 0)),
        pl.BlockSpec((D_MODEL, ROUTER_LANES), lambda i, r: (0, 0)),
        pl.BlockSpec((1, ROUTER_LANES), lambda i, r: (0, 0)),
    ]
    out_shape = (
        jax.ShapeDtypeStruct((t_all, D_MODEL), F32),
        jax.ShapeDtypeStruct((t_all, PACKED), jnp.uint32),
        jax.ShapeDtypeStruct((t_all, ROUTER_LANES), jnp.int32),
        jax.ShapeDtypeStruct((t_all, ROUTER_LANES), F32),
    )
    out_specs = (
        pl.BlockSpec((tm, D_MODEL), lambda i, r: (i, 0)),
        pl.BlockSpec((tm, PACKED), lambda i, r: (i, 0)),
        pl.BlockSpec((tm, ROUTER_LANES), lambda i, r: (i, 0)),
        pl.BlockSpec((tm, ROUTER_LANES), lambda i, r: (i, 0)),
    )
    return pl.pallas_call(
        functools.partial(_outproj_kernel, n_ctx_tiles=n_ctx),
        grid_spec=pltpu.PrefetchScalarGridSpec(
            num_scalar_prefetch=1, grid=(n_ctx + n_lat,), in_specs=in_specs, out_specs=out_specs),
        out_shape=out_shape,
        compiler_params=_params(1),
        name="out_projection",
    )(rows, *ctx, *lat, mod3, g_ffn, w_out_b, rw_hi, rw_lo, rb)


def _gather_rows(idx_ref, base, n_rows, src_hbm, dst, sem, *, unrolled):
    def one(r, priority):
        tok = idx_ref[base + r]
        pltpu.make_async_copy(src_hbm.at[pl.ds(tok, 1)], dst.at[pl.ds(r, 1)], sem).start(priority=priority)

    if unrolled:
        for r in range(n_rows):
            one(r, r % 2)
    else:
        def body(r, carry):
            one(r, 0)
            return carry
        lax.fori_loop(0, n_rows, body, 0, unroll=8)


def _wait_rows(src_hbm, dst, sem):
    pltpu.make_async_copy(src_hbm.at[pl.ds(0, dst.shape[0])], dst, sem).wait()


def _moe_kernel(be_ref, ne_ref, nu_ref, tok_ref, h_hbm, wg_hbm, wu_hbm, wd_hbm, y_ref,
                xg0, xg1, sg, su, sd, wgb, wub, wdb, sem, wsem):
    b = pl.program_id(0)
    n_used = nu_ref[0]
    last = n_used - 1
    active = b < n_used

    def weight_copies(e):
        return (pltpu.make_async_copy(wg_hbm.at[e], sg, wsem.at[0]),
                pltpu.make_async_copy(wu_hbm.at[e], su, wsem.at[1]),
                pltpu.make_async_copy(wd_hbm.at[e], sd, wsem.at[2]))

    @pl.when(b == 0)
    def _():
        for cp in weight_copies(be_ref[0]):
            cp.start()
        _gather_rows(tok_ref, 0, EXPERT_ROWS, h_hbm, xg0, sem.at[0], unrolled=False)

    new_expert = (b == 0) | (be_ref[b] != be_ref[jnp.maximum(b - 1, 0)])

    @pl.when(active & new_expert)
    def _():
        for cp in weight_copies(be_ref[b]):
            cp.wait()
        wgb[...] = sg[...].astype(BF16)
        wub[...] = su[...].astype(BF16)
        wdb[...] = sd[...].astype(BF16)

        @pl.when(ne_ref[b] >= 0)
        def _():
            for cp in weight_copies(ne_ref[b]):
                cp.start()

    def step(cur, nxt, sem_cur, sem_nxt):
        _wait_rows(h_hbm, cur, sem_cur)
        _gather_rows(tok_ref, jnp.minimum(b + 1, last) * EXPERT_ROWS, EXPERT_ROWS, h_hbm, nxt, sem_nxt,
                     unrolled=True)
        lo, hi = _unpack_bf16_pairs(cur[...])
        lo = lo.astype(BF16)
        hi = hi.astype(BF16)
        a = _dot(lo, wgb[0:PACKED, :]) + _dot(hi, wgb[PACKED:D_MODEL, :])
        u = _dot(lo, wub[0:PACKED, :]) + _dot(hi, wub[PACKED:D_MODEL, :])
        act = (a / (1.0 + jnp.exp(-a)) * u).astype(BF16)
        y_ref[...] = _pack_bf16_pairs(_dot(act, wdb[...]))

        @pl.when(b == last)
        def _():
            _wait_rows(h_hbm, nxt, sem_nxt)

    @pl.when(active & (b % 2 == 0))
    def _():
        step(xg0, xg1, sem.at[0], sem.at[1])

    @pl.when(active & (b % 2 == 1))
    def _():
        step(xg1, xg0, sem.at[1], sem.at[0])

    @pl.when(b >= n_used)
    def _():
        y_ref[...] = jnp.zeros_like(y_ref)


def _experts(block_expert, next_expert, n_used, buf_tok, h2_packed, w_gate, w_up, w_down):
    cap = buf_tok.shape[0]
    bm = EXPERT_ROWS
    return pl.pallas_call(
        _moe_kernel,
        grid_spec=pltpu.PrefetchScalarGridSpec(
            num_scalar_prefetch=4,
            grid=(cap // bm,),
            in_specs=[pl.BlockSpec(memory_space=pl.ANY)] * 4,
            out_specs=pl.BlockSpec((bm, PACKED), lambda b, *_: (b, 0)),
            scratch_shapes=[
                pltpu.VMEM((bm, PACKED), jnp.uint32), pltpu.VMEM((bm, PACKED), jnp.uint32),
                pltpu.VMEM((D_MODEL, D_EXPERT), F32), pltpu.VMEM((D_MODEL, D_EXPERT), F32),
                pltpu.VMEM((D_EXPERT, D_MODEL), F32),
                pltpu.VMEM((D_MODEL, D_EXPERT), BF16), pltpu.VMEM((D_MODEL, D_EXPERT), BF16),
                pltpu.VMEM((D_EXPERT, D_MODEL), BF16),
                pltpu.SemaphoreType.DMA((2,)), pltpu.SemaphoreType.DMA((3,)),
            ],
        ),
        out_shape=jax.ShapeDtypeStruct((cap, PACKED), jnp.uint32),
        compiler_params=_params(1),
        name="experts",
    )(block_expert, next_expert, n_used, buf_tok, h2_packed, w_gate, w_up, w_down)


def _combine_kernel(slot_ref, rows_ref, y_hbm, x1_ref, wts_ref, mod_ref, g_ref, op_ref, os_ref, yg0, yg1, sem,
                    *, n_ctx_tiles, n_tiles):
    del rows_ref
    tm = TOKEN_TILE
    i = pl.program_id(0)
    n_slots = slot_ref.shape[0] // TOP_K

    def start(tile, buf, s, unrolled):
        for k in range(TOP_K):
            _gather_rows(slot_ref, k * n_slots + tile * tm, tm, y_hbm, buf.at[k], s, unrolled=unrolled)

    @pl.when(i == 0)
    def _():
        start(0, yg0, sem.at[0], False)

    def step(cur, nxt, sem_cur, sem_nxt):
        for k in range(TOP_K):
            _wait_rows(y_hbm, cur.at[k], sem_cur)
        start(jnp.minimum(i + 1, n_tiles - 1), nxt, sem_nxt, True)
        w = wts_ref[...]
        lo0, hi0 = _unpack_bf16_pairs(cur[0])
        lo1, hi1 = _unpack_bf16_pairs(cur[1])
        ffn = jnp.concatenate([lo0 * w[:, 0:1] + lo1 * w[:, 1:2], hi0 * w[:, 0:1] + hi1 * w[:, 1:2]], axis=-1)
        x2 = x1_ref[...] + mod_ref[0][5:6] * ffn
        out = _rms(x2) * g_ref[...]

        @pl.when(i < n_ctx_tiles)
        def _():
            op_ref[...] = out

        @pl.when(i >= n_ctx_tiles)
        def _():
            os_ref[...] = out

        @pl.when(i == n_tiles - 1)
        def _():
            for k in range(TOP_K):
                _wait_rows(y_hbm, nxt.at[k], sem_nxt)

    @pl.when(i % 2 == 0)
    def _():
        step(yg0, yg1, sem.at[0], sem.at[1])

    @pl.when(i % 2 == 1)
    def _():
        step(yg1, yg0, sem.at[1], sem.at[0])


def _combine(slots, rows, y_buf, x1, wts, mod3, g_final, t_ctx):
    t = x1.shape[0]
    tm = TOKEN_TILE
    n_tiles = t // tm
    n_ctx = t_ctx // tm
    return pl.pallas_call(
        functools.partial(_combine_kernel, n_ctx_tiles=n_ctx, n_tiles=n_tiles),
        grid_spec=pltpu.PrefetchScalarGridSpec(
            num_scalar_prefetch=2,
            grid=(n_tiles,),
            in_specs=[
                pl.BlockSpec(memory_space=pl.ANY),
                pl.BlockSpec((tm, D_MODEL), lambda i, s, r: (i, 0)),
                pl.BlockSpec((tm, ROUTER_LANES), lambda i, s, r: (i, 0)),
                pl.BlockSpec((1, 6, D_MODEL), lambda i, s, r: (r[i], 0, 0)),
                pl.BlockSpec((1, D_MODEL), lambda i, s, r: (0, 0)),
            ],
            out_specs=(
                pl.BlockSpec((tm, D_MODEL), lambda i, s, r: (jnp.minimum(i, n_ctx - 1), 0)),
                pl.BlockSpec((tm, D_MODEL), lambda i, s, r: (jnp.maximum(i - n_ctx, 0), 0)),
            ),
            scratch_shapes=[pltpu.VMEM((TOP_K, tm, PACKED), jnp.uint32),
                            pltpu.VMEM((TOP_K, tm, PACKED), jnp.uint32),
                            pltpu.SemaphoreType.DMA((2,))],
        ),
        out_shape=(jax.ShapeDtypeStruct((t_ctx, D_MODEL), F32),
                   jax.ShapeDtypeStruct((t - t_ctx, D_MODEL), F32)),
        compiler_params=_params(1),
        name="combine",
    )(slots, rows, y_buf, x1, wts, mod3, g_final)


def _rope_tables(n_tokens):
    rows = n_tokens // GRID_W
    row = jnp.repeat(jnp.arange(rows, dtype=F32), GRID_W)
    col = jnp.tile(jnp.arange(GRID_W, dtype=F32), rows)
    half = ROPE_AXIS_DIM // 2
    inv_freq = ROPE_BASE ** (-jnp.arange(half, dtype=F32) / half)
    ar = row[:, None] * inv_freq
    ac = col[:, None] * inv_freq
    cos = jnp.concatenate([jnp.cos(ar), jnp.cos(ar), jnp.cos(ac), jnp.cos(ac)], axis=-1)
    sin = jnp.concatenate([-jnp.sin(ar), jnp.sin(ar), -jnp.sin(ac), jnp.sin(ac)], axis=-1)
    return cos, sin


def _dispatch_tables(ids):
    t = ids.shape[0]
    bm = EXPERT_ROWS
    flat_e = ids.reshape(t * TOP_K)
    onehot = (flat_e[:, None] == jnp.arange(N_EXPERTS, dtype=jnp.int32)[None, :]).astype(jnp.int32)
    running = jnp.cumsum(onehot, axis=0)
    rank = jnp.sum(running * onehot, axis=1) - 1
    counts = running[-1]
    padded = (counts + bm - 1) // bm * bm
    pad_end = jnp.cumsum(padded)
    pad_start = pad_end - padded
    dest = (jnp.sum(pad_start[None, :] * onehot, axis=1) + rank).astype(jnp.int32)
    n_blocks = (t * TOP_K + N_EXPERTS * (bm - 1) + bm - 1) // bm
    flat_tok = jnp.repeat(jnp.arange(t, dtype=jnp.int32), TOP_K)
    buf_tok = jnp.zeros((n_blocks * bm,), jnp.int32).at[dest].set(flat_tok)
    n_used = (pad_end[-1] // bm).astype(jnp.int32)
    blk = jnp.arange(n_blocks, dtype=jnp.int32)
    blk = jnp.minimum(blk, n_used - 1)
    block_expert = jnp.sum((pad_end[None, :] <= (blk * bm)[:, None]).astype(jnp.int32), axis=1)
    block_expert = jnp.minimum(block_expert, N_EXPERTS - 1)
    e_ids = jnp.arange(N_EXPERTS, dtype=jnp.int32)
    later = (e_ids[None, :] > e_ids[:, None]) & (counts[None, :] > 0)
    next_of = jnp.min(jnp.where(later, e_ids[None, :], N_EXPERTS), axis=1)
    next_of = jnp.where(next_of == N_EXPERTS, -1, next_of)
    next_expert = jnp.sum(next_of[None, :] * (block_expert[:, None] == e_ids[None, :]), axis=1)
    slots = dest.reshape(t, TOP_K).T.reshape(TOP_K * t)
    return (block_expert.astype(jnp.int32), next_expert.astype(jnp.int32), n_used.reshape(1), buf_tok, slots)


def kernel(x_prompt, x_sample, cache_k, cache_v, c, c_ctx, w_mod, b_mod, norm_mix_g, norm_ffn_g, w_in,
           pool_w, pool_scale, attn_sink, out_norm_pool_g, out_norm_attn_g, w_out, router_group_w,
           router_group_b, router_expert_w, router_expert_b, w_gate, w_up, w_down, final_norm_g):
    depth = w_mod.shape[0]
    assert depth == 1, "single trunk layer"
    bp, lp, _ = x_prompt.shape
    bs, ls, _ = x_sample.shape
    tp, ts = bp * lp, bs * ls
    tm = TOKEN_TILE
    l = 0

    cond8 = jnp.zeros((8, D_MODEL), F32).at[:bs].set(c).at[bs].set(c_ctx)
    mod3 = _modulation(cond8, w_mod[l], b_mod[l]).reshape(8, 6, D_MODEL)
    rows_p = jnp.full((tp // tm,), bs, jnp.int32)
    rows_s = jnp.arange(ts // tm, dtype=jnp.int32) // (ls // tm)

    w_in_b = w_in[l].astype(BF16)
    w_out_b = w_out[l].astype(BF16)
    pool_w_b = pool_w[l].astype(BF16)
    g_mix = norm_mix_g[l].reshape(1, D_MODEL)
    g_ffn = norm_ffn_g[l].reshape(1, D_MODEL)
    g_pool = out_norm_pool_g[l].reshape(1, POOL_WIDTH)
    g_attn = out_norm_attn_g[l].reshape(1, ATTN_WIDTH)
    p_scale = pool_scale[l].reshape(1, POOL_WIDTH)
    sink_b = jnp.broadcast_to(attn_sink[l][:, None], (N_HEADS, LANES))
    g_final = final_norm_g.reshape(1, D_MODEL)

    rw = jnp.concatenate(
        [router_group_w[l], jnp.transpose(router_expert_w[l], (1, 0, 2)).reshape(D_MODEL, N_EXPERTS)], axis=1)
    rw = jnp.pad(rw, ((0, 0), (0, ROUTER_LANES - rw.shape[1])))
    rw_hi = rw.astype(BF16)
    rw_lo = (rw - rw_hi.astype(F32)).astype(BF16)
    rb = jnp.concatenate([router_group_b[l], router_expert_b[l].reshape(N_EXPERTS)])
    rb = jnp.pad(rb, (0, ROUTER_LANES - rb.shape[0])).reshape(1, ROUTER_LANES)

    xp = x_prompt.reshape(tp, D_MODEL)
    xs = x_sample.reshape(ts, D_MODEL)

    up_p, q_p, k_p, v_p = _in_projection(xp, rows_p, mod3, g_mix, w_in_b, None, F32)
    pool_p = _pool_mixer(up_p.reshape(bp, lp, POOL_WIDTH), pool_w_b, p_scale, g_pool).reshape(tp, POOL_WIDTH)
    attn_p = _context_attention(q_p, k_p, v_p, sink_b, g_attn, lp)
    up_s, q_s, k_s, v_s = _in_projection(xs, rows_s, mod3, g_mix, w_in_b, _rope_tables(ls), BF16)
    pool_s = _pool_mixer(up_s.reshape(bs, ls, POOL_WIDTH), pool_w_b, p_scale, g_pool).reshape(ts, POOL_WIDTH)
    attn_s = _latent_attention(
        q_s, k_s.reshape(bs, ls, KV_WIDTH), v_s.reshape(bs, ls, KV_WIDTH),
        cache_k[:, l].reshape(bs, -1, KV_WIDTH), cache_v[:, l].reshape(bs, -1, KV_WIDTH), sink_b, g_attn)

    rows_all = jnp.concatenate([rows_p, rows_s])
    x1_all, h2_all, ids_all, wts_all = _out_projection(
        (pool_p, attn_p, xp), (pool_s, attn_s, xs), rows_all, mod3, g_ffn, w_out_b, rw_hi, rw_lo, rb)

    block_expert, next_expert, n_used, buf_tok, slots = _dispatch_tables(ids_all[:, :TOP_K])
    y_buf = _experts(block_expert, next_expert, n_used, buf_tok, h2_all, w_gate[l], w_up[l], w_down[l])

    y_p, y_s = _combine(slots, rows_all, y_buf, x1_all, wts_all, mod3, g_final, tp)

    kv_shape = (bp, 1, lp, N_KV_HEADS, HEAD_DIM)
    return (y_p.reshape(bp, lp, D_MODEL), y_s.reshape(bs, ls, D_MODEL),
            k_p.reshape(kv_shape), v_p.reshape(kv_shape))
```

```python
import functools

import jax
import jax.numpy as jnp
from jax import lax
from jax.experimental import pallas as pl
from jax.experimental.pallas import tpu as pltpu

F32 = jnp.float32
BF16 = jnp.bfloat16

D_MODEL = 2048
GRID_W = 64
HEAD_DIM = 128
ATTN_WIDTH = 1024
POOL_WIDTH = 1024
N_HEADS = 8
N_KV_HEADS = 2
Q_PER_KV = 4
KV_WIDTH = 256
IN_WIDTH = 2560
WINDOW = 128
POOL_WINDOWS = (2, 4, 8, 16)
POOL_GROUP_DIM = 256
ROPE_BASE = 10000.0
ROPE_AXIS_DIM = 64
N_EXPERT_GROUPS = 4
EXPERTS_PER_GROUP = 8
N_EXPERTS = 32
TOP_K = 2
D_EXPERT = 512
NORM_EPS = 1e-6
NEG_INF = -1e30
ATTN_SCALE = HEAD_DIM ** -0.5
LOG2E = 1.4426950408889634
Q_SCALE = ATTN_SCALE * LOG2E

LANES = 128
Q_BLOCKS_PER_STEP = 4
CTX_SEQS_PER_STEP = 2
VMEM_LIMIT = 48 * 1024 * 1024
PACKED = D_MODEL // 2
TOKEN_TILE = 256
IN_TILE = 512
POOL_HALO = 16
Q_BLOCK = 128
EXPERT_ROWS = 256
ROUTER_LANES = 128


def _params(n_grid):
    return pltpu.CompilerParams(
        dimension_semantics=("arbitrary",) * n_grid, vmem_limit_bytes=VMEM_LIMIT)


def _rms(x):
    return x * lax.rsqrt(jnp.mean(x * x, axis=-1, keepdims=True) + NORM_EPS)


def _dot(a, b):
    return jnp.dot(a, b, preferred_element_type=F32)


def _dot_nt(a, b):
    return lax.dot_general(a, b, (((1,), (1,)), ((), ())), preferred_element_type=F32)


def _pack_bf16_pairs(x):
    c = x.shape[1] // 2
    lo = lax.bitcast_convert_type(x[:, :c].astype(BF16).astype(F32), jnp.uint32)
    hi = lax.bitcast_convert_type(x[:, c:].astype(BF16).astype(F32), jnp.uint32)
    return hi | (lo >> 16)


def _unpack_bf16_pairs(u):
    lo = lax.bitcast_convert_type(u << 16, F32)
    hi = lax.bitcast_convert_type(u & jnp.uint32(0xFFFF0000), F32)
    return lo, hi


def _mod_kernel(cond_ref, w_ref, b_ref, o_ref):
    c = cond_ref[...]
    s = c / (1.0 + jnp.exp(-c))
    o_ref[...] = _dot(s.astype(BF16), w_ref[...].astype(BF16)) + b_ref[...]


def _modulation(cond8, w_mod, b_mod):
    n = w_mod.shape[1]
    tn = 1024
    return pl.pallas_call(
        _mod_kernel,
        grid=(n // tn,),
        in_specs=[
            pl.BlockSpec((8, D_MODEL), lambda j: (0, 0)),
            pl.BlockSpec((D_MODEL, tn), lambda j: (0, j)),
            pl.BlockSpec((1, tn), lambda j: (0, j)),
        ],
        out_specs=pl.BlockSpec((8, tn), lambda j: (0, j)),
        out_shape=jax.ShapeDtypeStruct((8, n), F32),
        compiler_params=_params(1),
        name="modulation",
    )(cond8, w_mod, b_mod.reshape(1, n))


def _rope(x, cos, sin_signed):
    lane = lax.broadcasted_iota(jnp.int32, x.shape, 1)
    partner = jnp.where((lane & 63) < 32,
                        pltpu.roll(x, HEAD_DIM - 32, axis=1),
                        pltpu.roll(x, 32, axis=1))
    return x * cos + partner * sin_signed


def _inproj_kernel(rows_ref, x_ref, mod_ref, g_ref, w_ref, *rest, rope):
    del rows_ref
    if rope:
        cos_ref, sin_ref, up_ref, q_ref, k_ref, v_ref = rest
    else:
        up_ref, q_ref, k_ref, v_ref = rest
    m = mod_ref[0]
    h = _rms(x_ref[...]) * g_ref[...]
    hb = (h * (1.0 + m[1:2]) + m[0:1]).astype(BF16)
    up_ref[...] = _dot(hb, w_ref[:, 0:POOL_WIDTH]).astype(BF16)
    q = _dot(hb, w_ref[:, POOL_WIDTH:POOL_WIDTH + ATTN_WIDTH])
    k = _dot(hb, w_ref[:, POOL_WIDTH + ATTN_WIDTH:IN_WIDTH - KV_WIDTH])
    v = _dot(hb, w_ref[:, IN_WIDTH - KV_WIDTH:IN_WIDTH])
    if rope:
        cos = cos_ref[...]
        sin = sin_ref[...]
        for hd in range(N_HEADS):
            sl = slice(hd * HEAD_DIM, (hd + 1) * HEAD_DIM)
            q_ref[:, sl] = (_rope(q[:, sl], cos, sin) * Q_SCALE).astype(BF16)
        for hd in range(N_KV_HEADS):
            sl = slice(hd * HEAD_DIM, (hd + 1) * HEAD_DIM)
            k_ref[:, sl] = _rope(k[:, sl], cos, sin).astype(k_ref.dtype)
    else:
        q_ref[...] = (q * Q_SCALE).astype(BF16)
        k_ref[...] = k.astype(k_ref.dtype)
    v_ref[...] = v.astype(v_ref.dtype)


def _in_projection(x, rows, mod3, g_mix, w_in_b, rope_tabs, kv_dtype):
    t = x.shape[0]
    tm = IN_TILE
    per = IN_TILE // TOKEN_TILE
    rope = rope_tabs is not None
    in_specs = [
        pl.BlockSpec((tm, D_MODEL), lambda i, r: (i, 0)),
        pl.BlockSpec((1, 6, D_MODEL), lambda i, r: (r[i * per], 0, 0)),
        pl.BlockSpec((1, D_MODEL), lambda i, r: (0, 0)),
        pl.BlockSpec((D_MODEL, IN_WIDTH), lambda i, r: (0, 0), pipeline_mode=pl.Buffered(1)),
    ]
    args = [x, mod3, g_mix, w_in_b]
    if rope:
        seq_tiles = rope_tabs[0].shape[0] // tm
        for tab in rope_tabs:
            in_specs.append(pl.BlockSpec((tm, HEAD_DIM), lambda i, r: (i % seq_tiles, 0)))
            args.append(tab)
    out_shape = (
        jax.ShapeDtypeStruct((t, POOL_WIDTH), BF16),
        jax.ShapeDtypeStruct((t, ATTN_WIDTH), BF16),
        jax.ShapeDtypeStruct((t, KV_WIDTH), kv_dtype),
        jax.ShapeDtypeStruct((t, KV_WIDTH), kv_dtype),
    )
    out_specs = (
        pl.BlockSpec((tm, POOL_WIDTH), lambda i, r: (i, 0)),
        pl.BlockSpec((tm, ATTN_WIDTH), lambda i, r: (i, 0)),
        pl.BlockSpec((tm, KV_WIDTH), lambda i, r: (i, 0)),
        pl.BlockSpec((tm, KV_WIDTH), lambda i, r: (i, 0)),
    )
    return pl.pallas_call(
        functools.partial(_inproj_kernel, rope=rope),
        grid_spec=pltpu.PrefetchScalarGridSpec(
            num_scalar_prefetch=1, grid=(t // tm,), in_specs=in_specs, out_specs=out_specs),
        out_shape=out_shape,
        compiler_params=_params(1),
        name="in_projection_rope" if rope else "in_projection",
    )(rows, *args)


def _pool_kernel(u_ref, pw_ref, ps_ref, g_ref, o_ref, *, seq_len):
    tq = TOKEN_TILE
    win = min(seq_len, tq + 2 * POOL_HALO)
    t0 = pl.program_id(1) * tq
    src0 = pl.multiple_of(jnp.clip(t0 - POOL_HALO, 0, seq_len - win), POOL_HALO)
    u = u_ref[0, pl.ds(src0, win), :]
    t = t0 + lax.broadcasted_iota(jnp.int32, (tq, win), 0)
    j = src0 + lax.broadcasted_iota(jnp.int32, (tq, win), 1)
    tc = t0 + lax.broadcasted_iota(jnp.int32, (tq, 1), 0)
    ys = []
    for gi, w in enumerate(POOL_WINDOWS):
        lo = jnp.maximum(t - w // 2, 0)
        hi = jnp.minimum(t + (w - w // 2), seq_len)
        cnt = (hi - lo).astype(F32)
        a = jnp.where((j >= lo) & (j < hi), 1.0, 0.0) - jnp.where(j == t, cnt, 0.0)
        cnt_col = (jnp.minimum(tc + (w - w // 2), seq_len) - jnp.maximum(tc - w // 2, 0)).astype(F32)
        sl = slice(gi * POOL_GROUP_DIM, (gi + 1) * POOL_GROUP_DIM)
        d = _dot(a.astype(BF16), u[:, sl]) / cnt_col
        ys.append(_dot(d.astype(BF16), pw_ref[gi]))
    y = jnp.concatenate(ys, axis=-1) * ps_ref[...]
    o_ref[0] = (_rms(y) * g_ref[...]).astype(BF16)


def _pool_mixer(up, pool_w_b, pool_scale, g_out_pool):
    b, seq_len, _ = up.shape
    tq = TOKEN_TILE
    return pl.pallas_call(
        functools.partial(_pool_kernel, seq_len=seq_len),
        grid=(b, seq_len // tq),
        in_specs=[
            pl.BlockSpec((1, seq_len, POOL_WIDTH), lambda bi, i: (bi, 0, 0)),
            pl.BlockSpec((len(POOL_WINDOWS), POOL_GROUP_DIM, POOL_GROUP_DIM), lambda bi, i: (0, 0, 0)),
            pl.BlockSpec((1, POOL_WIDTH), lambda bi, i: (0, 0)),
            pl.BlockSpec((1, POOL_WIDTH), lambda bi, i: (0, 0)),
        ],
        out_specs=pl.BlockSpec((1, tq, POOL_WIDTH), lambda bi, i: (bi, i, 0)),
        out_shape=jax.ShapeDtypeStruct((b, seq_len, POOL_WIDTH), BF16),
        compiler_params=_params(2),
        name=f"pool_mixer_{seq_len}",
    )(up, pool_w_b, pool_scale, g_out_pool)


def _stack_heads(q, kh, rows):
    return jnp.concatenate(
        [q[:, (kh * Q_PER_KV + g) * HEAD_DIM:(kh * Q_PER_KV + g + 1) * HEAD_DIM]
         for g in range(Q_PER_KV)], axis=0)


def _sink_column(sink_ref, kh, rows):
    return jnp.concatenate(
        [jnp.broadcast_to(sink_ref[kh * Q_PER_KV + g:kh * Q_PER_KV + g + 1, 0:1] * LOG2E, (rows, 1))
         for g in range(Q_PER_KV)], axis=0)


def _attend(q4, key_sets, value_sets, sk, band=None):
    s = [_dot_nt(q4, k) for k in key_sets]
    if band is not None:
        s[0] = jnp.where(band, s[0], NEG_INF)
    m = sk
    for si in s:
        m = jnp.maximum(m, jnp.max(si, axis=-1, keepdims=True))
    acc = None
    for si, v in zip(s, value_sets):
        v1 = jnp.concatenate([v, jnp.ones_like(v)], axis=-1)
        part = _dot(jnp.exp2(si - m).astype(BF16), v1)
        acc = part if acc is None else acc + part
    den = acc[:, HEAD_DIM:] + jnp.exp2(sk - m)
    return acc[:, :HEAD_DIM] / den


def _ctx_attn_kernel(q_ref, k_ref, v_ref, sink_ref, g_ref, o_ref, *, seq_len):
    rows = seq_len
    for j in range(q_ref.shape[0] // rows):
        rs = slice(j * rows, (j + 1) * rows)
        q = q_ref[rs, :]
        heads = [None] * N_HEADS
        for kh in range(N_KV_HEADS):
            sl = slice(kh * HEAD_DIM, (kh + 1) * HEAD_DIM)
            o = _attend(_stack_heads(q, kh, rows), [k_ref[rs, sl].astype(BF16)],
                        [v_ref[rs, sl].astype(BF16)], _sink_column(sink_ref, kh, rows))
            for g in range(Q_PER_KV):
                heads[kh * Q_PER_KV + g] = o[g * rows:(g + 1) * rows]
        y = jnp.concatenate(heads, axis=-1)
        o_ref[rs, :] = (_rms(y) * g_ref[...]).astype(BF16)


def _context_attention(q, k, v, sink_b, g_out_attn, seq_len):
    t = q.shape[0]
    blk = seq_len * CTX_SEQS_PER_STEP
    return pl.pallas_call(
        functools.partial(_ctx_attn_kernel, seq_len=seq_len),
        grid=(t // blk,),
        in_specs=[
            pl.BlockSpec((blk, ATTN_WIDTH), lambda b: (b, 0)),
            pl.BlockSpec((blk, KV_WIDTH), lambda b: (b, 0)),
            pl.BlockSpec((blk, KV_WIDTH), lambda b: (b, 0)),
            pl.BlockSpec((N_HEADS, LANES), lambda b: (0, 0)),
            pl.BlockSpec((1, ATTN_WIDTH), lambda b: (0, 0)),
        ],
        out_specs=pl.BlockSpec((blk, ATTN_WIDTH), lambda b: (b, 0)),
        out_shape=jax.ShapeDtypeStruct((t, ATTN_WIDTH), BF16),
        compiler_params=_params(1),
        name="context_attention",
    )(q, k, v, sink_b, g_out_attn)


def _lat_attn_kernel(q_ref, k_ref, v_ref, ck_ref, cv_ref, sink_ref, g_ref, o_ref, *, seq_len):
    rows = Q_BLOCK
    span = 3 * Q_BLOCK
    shape = (Q_PER_KV * rows, span)
    row_in_block = lax.broadcasted_iota(jnp.int32, shape, 0) & (rows - 1)
    col = lax.broadcasted_iota(jnp.int32, shape, 1)
    for j in range(Q_BLOCKS_PER_STEP):
        q0 = (pl.program_id(1) * Q_BLOCKS_PER_STEP + j) * rows
        start = pl.multiple_of(jnp.clip(q0 - Q_BLOCK, 0, seq_len - span), Q_BLOCK)
        q = q_ref[j * rows:(j + 1) * rows, :]
        kl = k_ref[0, pl.ds(start, span), :]
        vl = v_ref[0, pl.ds(start, span), :]
        band = jnp.abs(row_in_block - col + (q0 - start)) <= WINDOW
        heads = [None] * N_HEADS
        for kh in range(N_KV_HEADS):
            sl = slice(kh * HEAD_DIM, (kh + 1) * HEAD_DIM)
            o = _attend(_stack_heads(q, kh, rows),
                        [kl[:, sl], ck_ref[0, :, sl].astype(BF16)],
                        [vl[:, sl], cv_ref[0, :, sl].astype(BF16)],
                        _sink_column(sink_ref, kh, rows), band)
            for g in range(Q_PER_KV):
                heads[kh * Q_PER_KV + g] = o[g * rows:(g + 1) * rows]
        y = jnp.concatenate(heads, axis=-1)
        o_ref[j * rows:(j + 1) * rows, :] = (_rms(y) * g_ref[...]).astype(BF16)


def _latent_attention(q, k, v, cache_k, cache_v, sink_b, g_out_attn):
    b, seq_len, _ = k.shape
    q_rows = Q_BLOCK * Q_BLOCKS_PER_STEP
    nq = seq_len // q_rows
    past = cache_k.shape[1]
    return pl.pallas_call(
        functools.partial(_lat_attn_kernel, seq_len=seq_len),
        grid=(b, nq),
        in_specs=[
            pl.BlockSpec((q_rows, ATTN_WIDTH), lambda bi, n: (bi * nq + n, 0)),
            pl.BlockSpec((1, seq_len, KV_WIDTH), lambda bi, n: (bi, 0, 0)),
            pl.BlockSpec((1, seq_len, KV_WIDTH), lambda bi, n: (bi, 0, 0)),
            pl.BlockSpec((1, past, KV_WIDTH), lambda bi, n: (bi, 0, 0)),
            pl.BlockSpec((1, past, KV_WIDTH), lambda bi, n: (bi, 0, 0)),
            pl.BlockSpec((N_HEADS, LANES), lambda bi, n: (0, 0)),
            pl.BlockSpec((1, ATTN_WIDTH), lambda bi, n: (0, 0)),
        ],
        out_specs=pl.BlockSpec((q_rows, ATTN_WIDTH), lambda bi, n: (bi * nq + n, 0)),
        out_shape=jax.ShapeDtypeStruct((b * seq_len, ATTN_WIDTH), BF16),
        compiler_params=_params(2),
        name="latent_attention",
    )(q, k, v, cache_k, cache_v, sink_b, g_out_attn)


def _route(logits):
    lane = lax.broadcasted_iota(jnp.int32, logits.shape, 1).astype(F32)
    neg = -jnp.inf

    def first_argmax(x):
        mx = jnp.max(x, axis=-1, keepdims=True)
        return mx, jnp.min(jnp.where(x == mx, lane, float(ROUTER_LANES)), axis=-1, keepdims=True)

    gl = jnp.where(lane < N_EXPERT_GROUPS, logits, neg)
    gmax, g_idx = first_argmax(gl)
    p_g = 1.0 / jnp.sum(jnp.exp(gl - gmax), axis=-1, keepdims=True)
    base = N_EXPERT_GROUPS + EXPERTS_PER_GROUP * g_idx
    el = jnp.where((lane >= base) & (lane < base + EXPERTS_PER_GROUP), logits, neg)
    v1, i1 = first_argmax(el)
    v2, i2 = first_argmax(jnp.where(lane == i1, neg, el))
    e2 = jnp.exp(v2 - v1)
    w1 = p_g / (1.0 + e2)
    w2 = p_g * e2 / (1.0 + e2)
    ids = jnp.where(lane == 0.0, i1 - N_EXPERT_GROUPS, jnp.where(lane == 1.0, i2 - N_EXPERT_GROUPS, 0.0))
    wts = jnp.where(lane == 0.0, w1, jnp.where(lane == 1.0, w2, 0.0))
    return ids.astype(jnp.int32), wts


def _outproj_kernel(rows_ref, pool_p, attn_p, x_p, pool_s, attn_s, x_s, mod_ref, g_ref, w_ref, rwh_ref,
                    rwl_ref, rb_ref, x1_ref, h2_ref, ids_ref, wts_ref, *, n_ctx_tiles):
    del rows_ref
    i = pl.program_id(0)

    def body(pool_ref, attn_ref, x_ref):
        m = mod_ref[0]
        mix = (_dot(pool_ref[...], w_ref[0:POOL_WIDTH, :])
               + _dot(attn_ref[...], w_ref[POOL_WIDTH:POOL_WIDTH + ATTN_WIDTH, :]))
        x1 = x_ref[...] + m[2:3] * mix
        x1_ref[...] = x1
        h2 = _rms(x1) * g_ref[...] * (1.0 + m[4:5]) + m[3:4]
        h2_ref[...] = _pack_bf16_pairs(h2)
        hi = h2.astype(BF16)
        lo = (h2 - hi.astype(F32)).astype(BF16)
        logits = _dot(hi, rwh_ref[...]) + _dot(lo, rwh_ref[...]) + _dot(hi, rwl_ref[...]) + rb_ref[...]
        ids, wts = _route(logits)
        ids_ref[...] = ids
        wts_ref[...] = wts

    @pl.when(i < n_ctx_tiles)
    def _():
        body(pool_p, attn_p, x_p)

    @pl.when(i >= n_ctx_tiles)
    def _():
        body(pool_s, attn_s, x_s)


def _out_projection(ctx, lat, rows, mod3, g_ffn, w_out_b, rw_hi, rw_lo, rb):
    tm = TOKEN_TILE
    n_ctx = ctx[2].shape[0] // tm
    n_lat = lat[2].shape[0] // tm
    t_all = (n_ctx + n_lat) * tm

    def ctx_map(i, r):
        return (jnp.minimum(i, n_ctx - 1), 0)

    def lat_map(i, r):
        return (jnp.maximum(i - n_ctx, 0), 0)

    in_specs = [
        pl.BlockSpec((tm, POOL_WIDTH), ctx_map),
        pl.BlockSpec((tm, ATTN_WIDTH), ctx_map),
        pl.BlockSpec((tm, D_MODEL), ctx_map),
        pl.BlockSpec((tm, POOL_WIDTH), lat_map),
        pl.BlockSpec((tm, ATTN_WIDTH), lat_map),
        pl.BlockSpec((tm, D_MODEL), lat_map),
        pl.BlockSpec((1, 6, D_MODEL), lambda i, r: (r[i], 0, 0)),
        pl.BlockSpec((1, D_MODEL), lambda i, r: (0, 0)),
        pl.BlockSpec((D_MODEL, D_MODEL), lambda i, r: (0, 0), pipeline_mode=pl.Buffered(1)),
        pl.BlockSpec((D_MODEL, ROUTER_LANES), lambda i, r: (0, 0)),
        pl.BlockSpec((D_MODEL, ROUTER_LANES), lambda i, r: (0, 0)),
        pl.BlockSpec((1, ROUTER_LANES), lambda i, r: (0, 0)),
    ]
    out_shape = (
        jax.ShapeDtypeStruct((t_all, D_MODEL), F32),
        jax.ShapeDtypeStruct((t_all, PACKED), jnp.uint32),
        jax.ShapeDtypeStruct((t_all, ROUTER_LANES), jnp.int32),
        jax.ShapeDtypeStruct((t_all, ROUTER_LANES), F32),
    )
    out_specs = (
        pl.BlockSpec((tm, D_MODEL), lambda i, r: (i, 0)),
        pl.BlockSpec((tm, PACKED), lambda i, r: (i, 0)),
        pl.BlockSpec((tm, ROUTER_LANES), lambda i, r: (i, 0)),
        pl.BlockSpec((tm, ROUTER_LANES), lambda i, r: (i, 0)),
    )
    return pl.pallas_call(
        functools.partial(_outproj_kernel, n_ctx_tiles=n_ctx),
        grid_spec=pltpu.PrefetchScalarGridSpec(
            num_scalar_prefetch=1, grid=(n_ctx + n_lat,), in_specs=in_specs, out_specs=out_specs),
        out_shape=out_shape,
        compiler_params=_params(1),
        name="out_projection",
    )(rows, *ctx, *lat, mod3, g_ffn, w_out_b, rw_hi, rw_lo, rb)


def _gather_rows(idx_ref, base, n_rows, src_hbm, dst, sem, *, unrolled, both_queues=False):
    def one(r, priority):
        tok = idx_ref[base + r]
        pltpu.make_async_copy(src_hbm.at[pl.ds(tok, 1)], dst.at[pl.ds(r, 1)], sem).start(priority=priority)

    if unrolled:
        for r in range(n_rows):
            one(r, r % 2 if both_queues else 0)
    else:
        def body(r, carry):
            one(r, 0)
            return carry
        lax.fori_loop(0, n_rows, body, 0, unroll=8)


def _wait_rows(src_hbm, dst, sem):
    pltpu.make_async_copy(src_hbm.at[pl.ds(0, dst.shape[0])], dst, sem).wait()


def _moe_kernel(be_ref, ne_ref, nu_ref, tok_ref, h_hbm, wg_hbm, wu_hbm, wd_hbm, y_ref,
                xg0, xg1, sg, su, sd, wgb, wub, wdb, sem, wsem):
    b = pl.program_id(0)
    n_used = nu_ref[0]
    last = n_used - 1
    active = b < n_used

    def weight_copies(e):
        return (pltpu.make_async_copy(wg_hbm.at[e], sg, wsem.at[0]),
                pltpu.make_async_copy(wu_hbm.at[e], su, wsem.at[1]),
                pltpu.make_async_copy(wd_hbm.at[e], sd, wsem.at[2]))

    @pl.when(b == 0)
    def _():
        for cp in weight_copies(be_ref[0]):
            cp.start(priority=1)
        _gather_rows(tok_ref, 0, EXPERT_ROWS, h_hbm, xg0, sem.at[0], unrolled=False)

    new_expert = (b == 0) | (be_ref[b] != be_ref[jnp.maximum(b - 1, 0)])

    @pl.when(active & new_expert)
    def _():
        for cp in weight_copies(be_ref[b]):
            cp.wait()
        wgb[...] = sg[...].astype(BF16)
        wub[...] = su[...].astype(BF16)
        wdb[...] = sd[...].astype(BF16)

        @pl.when(ne_ref[b] >= 0)
        def _():
            for cp in weight_copies(ne_ref[b]):
                cp.start(priority=1)

    def step(cur, nxt, sem_cur, sem_nxt):
        _wait_rows(h_hbm, cur, sem_cur)
        _gather_rows(tok_ref, jnp.minimum(b + 1, last) * EXPERT_ROWS, EXPERT_ROWS, h_hbm, nxt, sem_nxt,
                     unrolled=True)
        lo, hi = _unpack_bf16_pairs(cur[...])
        lo = lo.astype(BF16)
        hi = hi.astype(BF16)
        a = _dot(lo, wgb[0:PACKED, :]) + _dot(hi, wgb[PACKED:D_MODEL, :])
        u = _dot(lo, wub[0:PACKED, :]) + _dot(hi, wub[PACKED:D_MODEL, :])
        act = (a / (1.0 + jnp.exp(-a)) * u).astype(BF16)
        y_ref[...] = _pack_bf16_pairs(_dot(act, wdb[...]))

        @pl.when(b == last)
        def _():
            _wait_rows(h_hbm, nxt, sem_nxt)

    @pl.when(active & (b % 2 == 0))
    def _():
        step(xg0, xg1, sem.at[0], sem.at[1])

    @pl.when(active & (b % 2 == 1))
    def _():
        step(xg1, xg0, sem.at[1], sem.at[0])

    @pl.when(b >= n_used)
    def _():
        y_ref[...] = jnp.zeros_like(y_ref)


def _experts(block_expert, next_expert, n_used, buf_tok, h2_packed, w_gate, w_up, w_down):
    cap = buf_tok.shape[0]
    bm = EXPERT_ROWS
    return pl.pallas_call(
        _moe_kernel,
        grid_spec=pltpu.PrefetchScalarGridSpec(
            num_scalar_prefetch=4,
            grid=(cap // bm,),
            in_specs=[pl.BlockSpec(memory_space=pl.ANY)] * 4,
            out_specs=pl.BlockSpec((bm, PACKED), lambda b, *_: (b, 0)),
            scratch_shapes=[
                pltpu.VMEM((bm, PACKED), jnp.uint32), pltpu.VMEM((bm, PACKED), jnp.uint32),
                pltpu.VMEM((D_MODEL, D_EXPERT), F32), pltpu.VMEM((D_MODEL, D_EXPERT), F32),
                pltpu.VMEM((D_EXPERT, D_MODEL), F32),
                pltpu.VMEM((D_MODEL, D_EXPERT), BF16), pltpu.VMEM((D_MODEL, D_EXPERT), BF16),
                pltpu.VMEM((D_EXPERT, D_MODEL), BF16),
                pltpu.SemaphoreType.DMA((2,)), pltpu.SemaphoreType.DMA((3,)),
            ],
        ),
        out_shape=jax.ShapeDtypeStruct((cap, PACKED), jnp.uint32),
        compiler_params=_params(1),
        name="experts",
    )(block_expert, next_expert, n_used, buf_tok, h2_packed, w_gate, w_up, w_down)


def _combine_kernel(slot_ref, rows_ref, y_hbm, x1_ref, wts_ref, mod_ref, g_ref, op_ref, os_ref, yg0, yg1, sem,
                    *, n_ctx_tiles, n_tiles):
    del rows_ref
    tm = TOKEN_TILE
    i = pl.program_id(0)
    n_slots = slot_ref.shape[0] // TOP_K

    def start(tile, buf, s, unrolled):
        for k in range(TOP_K):
            _gather_rows(slot_ref, k * n_slots + tile * tm, tm, y_hbm, buf.at[k], s, unrolled=unrolled,
                         both_queues=True)

    @pl.when(i == 0)
    def _():
        start(0, yg0, sem.at[0], False)

    def step(cur, nxt, sem_cur, sem_nxt):
        for k in range(TOP_K):
            _wait_rows(y_hbm, cur.at[k], sem_cur)
        start(jnp.minimum(i + 1, n_tiles - 1), nxt, sem_nxt, True)
        w = wts_ref[...]
        lo0, hi0 = _unpack_bf16_pairs(cur[0])
        lo1, hi1 = _unpack_bf16_pairs(cur[1])
        ffn = jnp.concatenate([lo0 * w[:, 0:1] + lo1 * w[:, 1:2], hi0 * w[:, 0:1] + hi1 * w[:, 1:2]], axis=-1)
        x2 = x1_ref[...] + mod_ref[0][5:6] * ffn
        out = _rms(x2) * g_ref[...]

        @pl.when(i < n_ctx_tiles)
        def _():
            op_ref[...] = out

        @pl.when(i >= n_ctx_tiles)
        def _():
            os_ref[...] = out

        @pl.when(i == n_tiles - 1)
        def _():
            for k in range(TOP_K):
                _wait_rows(y_hbm, nxt.at[k], sem_nxt)

    @pl.when(i % 2 == 0)
    def _():
        step(yg0, yg1, sem.at[0], sem.at[1])

    @pl.when(i % 2 == 1)
    def _():
        step(yg1, yg0, sem.at[1], sem.at[0])


def _combine(slots, rows, y_buf, x1, wts, mod3, g_final, t_ctx):
    t = x1.shape[0]
    tm = TOKEN_TILE
    n_tiles = t // tm
    n_ctx = t_ctx // tm
    return pl.pallas_call(
        functools.partial(_combine_kernel, n_ctx_tiles=n_ctx, n_tiles=n_tiles),
        grid_spec=pltpu.PrefetchScalarGridSpec(
            num_scalar_prefetch=2,
            grid=(n_tiles,),
            in_specs=[
                pl.BlockSpec(memory_space=pl.ANY),
                pl.BlockSpec((tm, D_MODEL), lambda i, s, r: (i, 0)),
                pl.BlockSpec((tm, ROUTER_LANES), lambda i, s, r: (i, 0)),
                pl.BlockSpec((1, 6, D_MODEL), lambda i, s, r: (r[i], 0, 0)),
                pl.BlockSpec((1, D_MODEL), lambda i, s, r: (0, 0)),
            ],
            out_specs=(
                pl.BlockSpec((tm, D_MODEL), lambda i, s, r: (jnp.minimum(i, n_ctx - 1), 0)),
                pl.BlockSpec((tm, D_MODEL), lambda i, s, r: (jnp.maximum(i - n_ctx, 0), 0)),
            ),
            scratch_shapes=[pltpu.VMEM((TOP_K, tm, PACKED), jnp.uint32),
                            pltpu.VMEM((TOP_K, tm, PACKED), jnp.uint32),
                            pltpu.SemaphoreType.DMA((2,))],
        ),
        out_shape=(jax.ShapeDtypeStruct((t_ctx, D_MODEL), F32),
                   jax.ShapeDtypeStruct((t - t_ctx, D_MODEL), F32)),
        compiler_params=_params(1),
        name="combine",
    )(slots, rows, y_buf, x1, wts, mod3, g_final)


def _rope_tables(n_tokens):
    rows = n_tokens // GRID_W
    row = jnp.repeat(jnp.arange(rows, dtype=F32), GRID_W)
    col = jnp.tile(jnp.arange(GRID_W, dtype=F32), rows)
    half = ROPE_AXIS_DIM // 2
    inv_freq = ROPE_BASE ** (-jnp.arange(half, dtype=F32) / half)
    ar = row[:, None] * inv_freq
    ac = col[:, None] * inv_freq
    cos = jnp.concatenate([jnp.cos(ar), jnp.cos(ar), jnp.cos(ac), jnp.cos(ac)], axis=-1)
    sin = jnp.concatenate([-jnp.sin(ar), jnp.sin(ar), -jnp.sin(ac), jnp.sin(ac)], axis=-1)
    return cos, sin


def _dispatch_tables(ids):
    t = ids.shape[0]
    bm = EXPERT_ROWS
    flat_e = ids.reshape(t * TOP_K)
    onehot = (flat_e[:, None] == jnp.arange(N_EXPERTS, dtype=jnp.int32)[None, :]).astype(jnp.int32)
    running = jnp.cumsum(onehot, axis=0)
    rank = jnp.sum(running * onehot, axis=1) - 1
    counts = running[-1]
    padded = (counts + bm - 1) // bm * bm
    pad_end = jnp.cumsum(padded)
    pad_start = pad_end - padded
    dest = (jnp.sum(pad_start[None, :] * onehot, axis=1) + rank).astype(jnp.int32)
    n_blocks = (t * TOP_K + N_EXPERTS * (bm - 1) + bm - 1) // bm
    flat_tok = jnp.repeat(jnp.arange(t, dtype=jnp.int32), TOP_K)
    buf_tok = jnp.zeros((n_blocks * bm,), jnp.int32).at[dest].set(flat_tok)
    n_used = (pad_end[-1] // bm).astype(jnp.int32)
    blk = jnp.arange(n_blocks, dtype=jnp.int32)
    blk = jnp.minimum(blk, n_used - 1)
    block_expert = jnp.sum((pad_end[None, :] <= (blk * bm)[:, None]).astype(jnp.int32), axis=1)
    block_expert = jnp.minimum(block_expert, N_EXPERTS - 1)
    e_ids = jnp.arange(N_EXPERTS, dtype=jnp.int32)
    later = (e_ids[None, :] > e_ids[:, None]) & (counts[None, :] > 0)
    next_of = jnp.min(jnp.where(later, e_ids[None, :], N_EXPERTS), axis=1)
    next_of = jnp.where(next_of == N_EXPERTS, -1, next_of)
    next_expert = jnp.sum(next_of[None, :] * (block_expert[:, None] == e_ids[None, :]), axis=1)
    slots = dest.reshape(t, TOP_K).T.reshape(TOP_K * t)
    return (block_expert.astype(jnp.int32), next_expert.astype(jnp.int32), n_used.reshape(1), buf_tok, slots)


def kernel(x_prompt, x_sample, cache_k, cache_v, c, c_ctx, w_mod, b_mod, norm_mix_g, norm_ffn_g, w_in,
           pool_w, pool_scale, attn_sink, out_norm_pool_g, out_norm_attn_g, w_out, router_group_w,
           router_group_b, router_expert_w, router_expert_b, w_gate, w_up, w_down, final_norm_g):
    depth = w_mod.shape[0]
    assert depth == 1, "single trunk layer"
    bp, lp, _ = x_prompt.shape
    bs, ls, _ = x_sample.shape
    tp, ts = bp * lp, bs * ls
    tm = TOKEN_TILE
    l = 0

    cond8 = jnp.zeros((8, D_MODEL), F32).at[:bs].set(c).at[bs].set(c_ctx)
    mod3 = _modulation(cond8, w_mod[l], b_mod[l]).reshape(8, 6, D_MODEL)
    rows_p = jnp.full((tp // tm,), bs, jnp.int32)
    rows_s = jnp.arange(ts // tm, dtype=jnp.int32) // (ls // tm)

    w_in_b = w_in[l].astype(BF16)
    w_out_b = w_out[l].astype(BF16)
    pool_w_b = pool_w[l].astype(BF16)
    g_mix = norm_mix_g[l].reshape(1, D_MODEL)
    g_ffn = norm_ffn_g[l].reshape(1, D_MODEL)
    g_pool = out_norm_pool_g[l].reshape(1, POOL_WIDTH)
    g_attn = out_norm_attn_g[l].reshape(1, ATTN_WIDTH)
    p_scale = pool_scale[l].reshape(1, POOL_WIDTH)
    sink_b = jnp.broadcast_to(attn_sink[l][:, None], (N_HEADS, LANES))
    g_final = final_norm_g.reshape(1, D_MODEL)

    rw = jnp.concatenate(
        [router_group_w[l], jnp.transpose(router_expert_w[l], (1, 0, 2)).reshape(D_MODEL, N_EXPERTS)], axis=1)
    rw = jnp.pad(rw, ((0, 0), (0, ROUTER_LANES - rw.shape[1])))
    rw_hi = rw.astype(BF16)
    rw_lo = (rw - rw_hi.astype(F32)).astype(BF16)
    rb = jnp.concatenate([router_group_b[l], router_expert_b[l].reshape(N_EXPERTS)])
    rb = jnp.pad(rb, (0, ROUTER_LANES - rb.shape[0])).reshape(1, ROUTER_LANES)

    xp = x_prompt.reshape(tp, D_MODEL)
    xs = x_sample.reshape(ts, D_MODEL)

    up_p, q_p, k_p, v_p = _in_projection(xp, rows_p, mod3, g_mix, w_in_b, None, F32)
    pool_p = _pool_mixer(up_p.reshape(bp, lp, POOL_WIDTH), pool_w_b, p_scale, g_pool).reshape(tp, POOL_WIDTH)
    attn_p = _context_attention(q_p, k_p, v_p, sink_b, g_attn, lp)
    up_s, q_s, k_s, v_s = _in_projection(xs, rows_s, mod3, g_mix, w_in_b, _rope_tables(ls), BF16)
    pool_s = _pool_mixer(up_s.reshape(bs, ls, POOL_WIDTH), pool_w_b, p_scale, g_pool).reshape(ts, POOL_WIDTH)
    attn_s = _latent_attention(
        q_s, k_s.reshape(bs, ls, KV_WIDTH), v_s.reshape(bs, ls, KV_WIDTH),
        cache_k[:, l].reshape(bs, -1, KV_WIDTH), cache_v[:, l].reshape(bs, -1, KV_WIDTH), sink_b, g_attn)

    rows_all = jnp.concatenate([rows_p, rows_s])
    x1_all, h2_all, ids_all, wts_all = _out_projection(
        (pool_p, attn_p, xp), (pool_s, attn_s, xs), rows_all, mod3, g_ffn, w_out_b, rw_hi, rw_lo, rb)

    block_expert, next_expert, n_used, buf_tok, slots = _dispatch_tables(ids_all[:, :TOP_K])
    y_buf = _experts(block_expert, next_expert, n_used, buf_tok, h2_all, w_gate[l], w_up[l], w_down[l])

    y_p, y_s = _combine(slots, rows_all, y_buf, x1_all, wts_all, mod3, g_final, tp)

    kv_shape = (bp, 1, lp, N_KV_HEADS, HEAD_DIM)
    return (y_p.reshape(bp, lp, D_MODEL), y_s.reshape(bs, ls, D_MODEL),
            k_p.reshape(kv_shape), v_p.reshape(kv_shape))
```

```python
import functools

import jax
import jax.numpy as jnp
from jax import lax
from jax.experimental import pallas as pl
from jax.experimental.pallas import tpu as pltpu

F32 = jnp.float32
BF16 = jnp.bfloat16

D_MODEL = 2048
GRID_W = 64
HEAD_DIM = 128
ATTN_WIDTH = 1024
POOL_WIDTH = 1024
N_HEADS = 8
N_KV_HEADS = 2
Q_PER_KV = 4
KV_WIDTH = 256
IN_WIDTH = 2560
WINDOW = 128
POOL_WINDOWS = (2, 4, 8, 16)
POOL_GROUP_DIM = 256
ROPE_BASE = 10000.0
ROPE_AXIS_DIM = 64
N_EXPERT_GROUPS = 4
EXPERTS_PER_GROUP = 8
N_EXPERTS = 32
TOP_K = 2
D_EXPERT = 512
NORM_EPS = 1e-6
NEG_INF = -1e30
ATTN_SCALE = HEAD_DIM ** -0.5
LOG2E = 1.4426950408889634
Q_SCALE = ATTN_SCALE * LOG2E

LANES = 128
Q_BLOCKS_PER_STEP = 4
CTX_SEQS_PER_STEP = 2
VMEM_LIMIT = 48 * 1024 * 1024
PACKED = D_MODEL // 2
TOKEN_TILE = 256
IN_TILE = 512
POOL_HALO = 16
Q_BLOCK = 128
EXPERT_ROWS = 256
GATHER_GROUPS = 16
ROUTER_LANES = 128


def _params(n_grid):
    return pltpu.CompilerParams(
        dimension_semantics=("arbitrary",) * n_grid, vmem_limit_bytes=VMEM_LIMIT)


def _rms(x):
    return x * lax.rsqrt(jnp.mean(x * x, axis=-1, keepdims=True) + NORM_EPS)


def _dot(a, b):
    return jnp.dot(a, b, preferred_element_type=F32)


def _dot_nt(a, b):
    return lax.dot_general(a, b, (((1,), (1,)), ((), ())), preferred_element_type=F32)


def _pack_bf16_pairs(x):
    c = x.shape[1] // 2
    lo = lax.bitcast_convert_type(x[:, :c].astype(BF16).astype(F32), jnp.uint32)
    hi = lax.bitcast_convert_type(x[:, c:].astype(BF16).astype(F32), jnp.uint32)
    return hi | (lo >> 16)


def _unpack_bf16_pairs(u):
    lo = lax.bitcast_convert_type(u << 16, F32)
    hi = lax.bitcast_convert_type(u & jnp.uint32(0xFFFF0000), F32)
    return lo, hi


def _mod_kernel(cond_ref, w_ref, b_ref, o_ref):
    c = cond_ref[...]
    s = c / (1.0 + jnp.exp(-c))
    o_ref[...] = _dot(s.astype(BF16), w_ref[...].astype(BF16)) + b_ref[...]


def _modulation(cond8, w_mod, b_mod):
    n = w_mod.shape[1]
    tn = 1024
    return pl.pallas_call(
        _mod_kernel,
        grid=(n // tn,),
        in_specs=[
            pl.BlockSpec((8, D_MODEL), lambda j: (0, 0)),
            pl.BlockSpec((D_MODEL, tn), lambda j: (0, j)),
            pl.BlockSpec((1, tn), lambda j: (0, j)),
        ],
        out_specs=pl.BlockSpec((8, tn), lambda j: (0, j)),
        out_shape=jax.ShapeDtypeStruct((8, n), F32),
        compiler_params=_params(1),
        name="modulation",
    )(cond8, w_mod, b_mod.reshape(1, n))


def _rope(x, cos, sin_signed):
    lane = lax.broadcasted_iota(jnp.int32, x.shape, 1)
    partner = jnp.where((lane & 63) < 32,
                        pltpu.roll(x, HEAD_DIM - 32, axis=1),
                        pltpu.roll(x, 32, axis=1))
    return x * cos + partner * sin_signed


def _inproj_kernel(rows_ref, x_ref, mod_ref, g_ref, w_ref, *rest, rope):
    del rows_ref
    if rope:
        cos_ref, sin_ref, up_ref, q_ref, k_ref, v_ref = rest
    else:
        up_ref, q_ref, k_ref, v_ref = rest
    m = mod_ref[0]
    h = _rms(x_ref[...]) * g_ref[...]
    hb = (h * (1.0 + m[1:2]) + m[0:1]).astype(BF16)
    up_ref[...] = _dot(hb, w_ref[:, 0:POOL_WIDTH]).astype(BF16)
    q = _dot(hb, w_ref[:, POOL_WIDTH:POOL_WIDTH + ATTN_WIDTH])
    k = _dot(hb, w_ref[:, POOL_WIDTH + ATTN_WIDTH:IN_WIDTH - KV_WIDTH])
    v = _dot(hb, w_ref[:, IN_WIDTH - KV_WIDTH:IN_WIDTH])
    if rope:
        cos = cos_ref[...]
        sin = sin_ref[...]
        for hd in range(N_HEADS):
            sl = slice(hd * HEAD_DIM, (hd + 1) * HEAD_DIM)
            q_ref[:, sl] = (_rope(q[:, sl], cos, sin) * Q_SCALE).astype(BF16)
        for hd in range(N_KV_HEADS):
            sl = slice(hd * HEAD_DIM, (hd + 1) * HEAD_DIM)
            k_ref[:, sl] = _rope(k[:, sl], cos, sin).astype(k_ref.dtype)
    else:
        q_ref[...] = (q * Q_SCALE).astype(BF16)
        k_ref[...] = k.astype(k_ref.dtype)
    v_ref[...] = v.astype(v_ref.dtype)


def _in_projection(x, rows, mod3, g_mix, w_in_b, rope_tabs, kv_dtype):
    t = x.shape[0]
    tm = IN_TILE
    per = IN_TILE // TOKEN_TILE
    rope = rope_tabs is not None
    in_specs = [
        pl.BlockSpec((tm, D_MODEL), lambda i, r: (i, 0)),
        pl.BlockSpec((1, 6, D_MODEL), lambda i, r: (r[i * per], 0, 0)),
        pl.BlockSpec((1, D_MODEL), lambda i, r: (0, 0)),
        pl.BlockSpec((D_MODEL, IN_WIDTH), lambda i, r: (0, 0), pipeline_mode=pl.Buffered(1)),
    ]
    args = [x, mod3, g_mix, w_in_b]
    if rope:
        seq_tiles = rope_tabs[0].shape[0] // tm
        for tab in rope_tabs:
            in_specs.append(pl.BlockSpec((tm, HEAD_DIM), lambda i, r: (i % seq_tiles, 0)))
            args.append(tab)
    out_shape = (
        jax.ShapeDtypeStruct((t, POOL_WIDTH), BF16),
        jax.ShapeDtypeStruct((t, ATTN_WIDTH), BF16),
        jax.ShapeDtypeStruct((t, KV_WIDTH), kv_dtype),
        jax.ShapeDtypeStruct((t, KV_WIDTH), kv_dtype),
    )
    out_specs = (
        pl.BlockSpec((tm, POOL_WIDTH), lambda i, r: (i, 0)),
        pl.BlockSpec((tm, ATTN_WIDTH), lambda i, r: (i, 0)),
        pl.BlockSpec((tm, KV_WIDTH), lambda i, r: (i, 0)),
        pl.BlockSpec((tm, KV_WIDTH), lambda i, r: (i, 0)),
    )
    return pl.pallas_call(
        functools.partial(_inproj_kernel, rope=rope),
        grid_spec=pltpu.PrefetchScalarGridSpec(
            num_scalar_prefetch=1, grid=(t // tm,), in_specs=in_specs, out_specs=out_specs),
        out_shape=out_shape,
        compiler_params=_params(1),
        name="in_projection_rope" if rope else "in_projection",
    )(rows, *args)


def _pool_kernel(u_ref, pw_ref, ps_ref, g_ref, o_ref, *, seq_len):
    tq = TOKEN_TILE
    win = min(seq_len, tq + 2 * POOL_HALO)
    t0 = pl.program_id(1) * tq
    src0 = pl.multiple_of(jnp.clip(t0 - POOL_HALO, 0, seq_len - win), POOL_HALO)
    u = u_ref[0, pl.ds(src0, win), :]
    t = t0 + lax.broadcasted_iota(jnp.int32, (tq, win), 0)
    j = src0 + lax.broadcasted_iota(jnp.int32, (tq, win), 1)
    tc = t0 + lax.broadcasted_iota(jnp.int32, (tq, 1), 0)
    ys = []
    for gi, w in enumerate(POOL_WINDOWS):
        lo = jnp.maximum(t - w // 2, 0)
        hi = jnp.minimum(t + (w - w // 2), seq_len)
        cnt = (hi - lo).astype(F32)
        a = jnp.where((j >= lo) & (j < hi), 1.0, 0.0) - jnp.where(j == t, cnt, 0.0)
        cnt_col = (jnp.minimum(tc + (w - w // 2), seq_len) - jnp.maximum(tc - w // 2, 0)).astype(F32)
        sl = slice(gi * POOL_GROUP_DIM, (gi + 1) * POOL_GROUP_DIM)
        d = _dot(a.astype(BF16), u[:, sl]) / cnt_col
        ys.append(_dot(d.astype(BF16), pw_ref[gi]))
    y = jnp.concatenate(ys, axis=-1) * ps_ref[...]
    o_ref[0] = (_rms(y) * g_ref[...]).astype(BF16)


def _pool_mixer(up, pool_w_b, pool_scale, g_out_pool):
    b, seq_len, _ = up.shape
    tq = TOKEN_TILE
    return pl.pallas_call(
        functools.partial(_pool_kernel, seq_len=seq_len),
        grid=(b, seq_len // tq),
        in_specs=[
            pl.BlockSpec((1, seq_len, POOL_WIDTH), lambda bi, i: (bi, 0, 0)),
            pl.BlockSpec((len(POOL_WINDOWS), POOL_GROUP_DIM, POOL_GROUP_DIM), lambda bi, i: (0, 0, 0)),
            pl.BlockSpec((1, POOL_WIDTH), lambda bi, i: (0, 0)),
            pl.BlockSpec((1, POOL_WIDTH), lambda bi, i: (0, 0)),
        ],
        out_specs=pl.BlockSpec((1, tq, POOL_WIDTH), lambda bi, i: (bi, i, 0)),
        out_shape=jax.ShapeDtypeStruct((b, seq_len, POOL_WIDTH), BF16),
        compiler_params=_params(2),
        name=f"pool_mixer_{seq_len}",
    )(up, pool_w_b, pool_scale, g_out_pool)


def _stack_heads(q, kh, rows):
    return jnp.concatenate(
        [q[:, (kh * Q_PER_KV + g) * HEAD_DIM:(kh * Q_PER_KV + g + 1) * HEAD_DIM]
         for g in range(Q_PER_KV)], axis=0)


def _sink_column(sink_ref, kh, rows):
    return jnp.concatenate(
        [jnp.broadcast_to(sink_ref[kh * Q_PER_KV + g:kh * Q_PER_KV + g + 1, 0:1] * LOG2E, (rows, 1))
         for g in range(Q_PER_KV)], axis=0)


def _attend(q4, key_sets, value_sets, sk, band=None):
    s = [_dot_nt(q4, k) for k in key_sets]
    if band is not None:
        s[0] = jnp.where(band, s[0], NEG_INF)
    m = sk
    for si in s:
        m = jnp.maximum(m, jnp.max(si, axis=-1, keepdims=True))
    acc = None
    for si, v in zip(s, value_sets):
        v1 = jnp.concatenate([v, jnp.ones_like(v)], axis=-1)
        part = _dot(jnp.exp2(si - m).astype(BF16), v1)
        acc = part if acc is None else acc + part
    den = acc[:, HEAD_DIM:] + jnp.exp2(sk - m)
    return acc[:, :HEAD_DIM] / den


def _ctx_attn_kernel(q_ref, k_ref, v_ref, sink_ref, g_ref, o_ref, *, seq_len):
    rows = seq_len
    for j in range(q_ref.shape[0] // rows):
        rs = slice(j * rows, (j + 1) * rows)
        q = q_ref[rs, :]
        heads = [None] * N_HEADS
        for kh in range(N_KV_HEADS):
            sl = slice(kh * HEAD_DIM, (kh + 1) * HEAD_DIM)
            o = _attend(_stack_heads(q, kh, rows), [k_ref[rs, sl].astype(BF16)],
                        [v_ref[rs, sl].astype(BF16)], _sink_column(sink_ref, kh, rows))
            for g in range(Q_PER_KV):
                heads[kh * Q_PER_KV + g] = o[g * rows:(g + 1) * rows]
        y = jnp.concatenate(heads, axis=-1)
        o_ref[rs, :] = (_rms(y) * g_ref[...]).astype(BF16)


def _context_attention(q, k, v, sink_b, g_out_attn, seq_len):
    t = q.shape[0]
    blk = seq_len * CTX_SEQS_PER_STEP
    return pl.pallas_call(
        functools.partial(_ctx_attn_kernel, seq_len=seq_len),
        grid=(t // blk,),
        in_specs=[
            pl.BlockSpec((blk, ATTN_WIDTH), lambda b: (b, 0)),
            pl.BlockSpec((blk, KV_WIDTH), lambda b: (b, 0)),
            pl.BlockSpec((blk, KV_WIDTH), lambda b: (b, 0)),
            pl.BlockSpec((N_HEADS, LANES), lambda b: (0, 0)),
            pl.BlockSpec((1, ATTN_WIDTH), lambda b: (0, 0)),
        ],
        out_specs=pl.BlockSpec((blk, ATTN_WIDTH), lambda b: (b, 0)),
        out_shape=jax.ShapeDtypeStruct((t, ATTN_WIDTH), BF16),
        compiler_params=_params(1),
        name="context_attention",
    )(q, k, v, sink_b, g_out_attn)


def _lat_attn_kernel(q_ref, k_ref, v_ref, ck_ref, cv_ref, sink_ref, g_ref, o_ref, *, seq_len):
    rows = Q_BLOCK
    span = 3 * Q_BLOCK
    shape = (Q_PER_KV * rows, span)
    row_in_block = lax.broadcasted_iota(jnp.int32, shape, 0) & (rows - 1)
    col = lax.broadcasted_iota(jnp.int32, shape, 1)
    for j in range(Q_BLOCKS_PER_STEP):
        q0 = (pl.program_id(1) * Q_BLOCKS_PER_STEP + j) * rows
        start = pl.multiple_of(jnp.clip(q0 - Q_BLOCK, 0, seq_len - span), Q_BLOCK)
        q = q_ref[j * rows:(j + 1) * rows, :]
        kl = k_ref[0, pl.ds(start, span), :]
        vl = v_ref[0, pl.ds(start, span), :]
        band = jnp.abs(row_in_block - col + (q0 - start)) <= WINDOW
        heads = [None] * N_HEADS
        for kh in range(N_KV_HEADS):
            sl = slice(kh * HEAD_DIM, (kh + 1) * HEAD_DIM)
            o = _attend(_stack_heads(q, kh, rows),
                        [kl[:, sl], ck_ref[0, :, sl].astype(BF16)],
                        [vl[:, sl], cv_ref[0, :, sl].astype(BF16)],
                        _sink_column(sink_ref, kh, rows), band)
            for g in range(Q_PER_KV):
                heads[kh * Q_PER_KV + g] = o[g * rows:(g + 1) * rows]
        y = jnp.concatenate(heads, axis=-1)
        o_ref[j * rows:(j + 1) * rows, :] = (_rms(y) * g_ref[...]).astype(BF16)


def _latent_attention(q, k, v, cache_k, cache_v, sink_b, g_out_attn):
    b, seq_len, _ = k.shape
    q_rows = Q_BLOCK * Q_BLOCKS_PER_STEP
    nq = seq_len // q_rows
    past = cache_k.shape[1]
    return pl.pallas_call(
        functools.partial(_lat_attn_kernel, seq_len=seq_len),
        grid=(b, nq),
        in_specs=[
            pl.BlockSpec((q_rows, ATTN_WIDTH), lambda bi, n: (bi * nq + n, 0)),
            pl.BlockSpec((1, seq_len, KV_WIDTH), lambda bi, n: (bi, 0, 0)),
            pl.BlockSpec((1, seq_len, KV_WIDTH), lambda bi, n: (bi, 0, 0)),
            pl.BlockSpec((1, past, KV_WIDTH), lambda bi, n: (bi, 0, 0)),
            pl.BlockSpec((1, past, KV_WIDTH), lambda bi, n: (bi, 0, 0)),
            pl.BlockSpec((N_HEADS, LANES), lambda bi, n: (0, 0)),
            pl.BlockSpec((1, ATTN_WIDTH), lambda bi, n: (0, 0)),
        ],
        out_specs=pl.BlockSpec((q_rows, ATTN_WIDTH), lambda bi, n: (bi * nq + n, 0)),
        out_shape=jax.ShapeDtypeStruct((b * seq_len, ATTN_WIDTH), BF16),
        compiler_params=_params(2),
        name="latent_attention",
    )(q, k, v, cache_k, cache_v, sink_b, g_out_attn)


def _route(logits):
    lane = lax.broadcasted_iota(jnp.int32, logits.shape, 1).astype(F32)
    neg = -jnp.inf

    def first_argmax(x):
        mx = jnp.max(x, axis=-1, keepdims=True)
        return mx, jnp.min(jnp.where(x == mx, lane, float(ROUTER_LANES)), axis=-1, keepdims=True)

    gl = jnp.where(lane < N_EXPERT_GROUPS, logits, neg)
    gmax, g_idx = first_argmax(gl)
    p_g = 1.0 / jnp.sum(jnp.exp(gl - gmax), axis=-1, keepdims=True)
    base = N_EXPERT_GROUPS + EXPERTS_PER_GROUP * g_idx
    el = jnp.where((lane >= base) & (lane < base + EXPERTS_PER_GROUP), logits, neg)
    v1, i1 = first_argmax(el)
    v2, i2 = first_argmax(jnp.where(lane == i1, neg, el))
    e2 = jnp.exp(v2 - v1)
    w1 = p_g / (1.0 + e2)
    w2 = p_g * e2 / (1.0 + e2)
    ids = jnp.where(lane == 0.0, i1 - N_EXPERT_GROUPS, jnp.where(lane == 1.0, i2 - N_EXPERT_GROUPS, 0.0))
    wts = jnp.where(lane == 0.0, w1, jnp.where(lane == 1.0, w2, 0.0))
    return ids.astype(jnp.int32), wts


def _outproj_kernel(rows_ref, pool_p, attn_p, x_p, pool_s, attn_s, x_s, mod_ref, g_ref, w_ref, rwh_ref,
                    rwl_ref, rb_ref, x1_ref, h2_ref, ids_ref, wts_ref, *, n_ctx_tiles):
    del rows_ref
    i = pl.program_id(0)

    def body(pool_ref, attn_ref, x_ref):
        m = mod_ref[0]
        mix = (_dot(pool_ref[...], w_ref[0:POOL_WIDTH, :])
               + _dot(attn_ref[...], w_ref[POOL_WIDTH:POOL_WIDTH + ATTN_WIDTH, :]))
        x1 = x_ref[...] + m[2:3] * mix
        x1_ref[...] = x1
        h2 = _rms(x1) * g_ref[...] * (1.0 + m[4:5]) + m[3:4]
        h2_ref[...] = _pack_bf16_pairs(h2)
        hi = h2.astype(BF16)
        lo = (h2 - hi.astype(F32)).astype(BF16)
        logits = _dot(hi, rwh_ref[...]) + _dot(lo, rwh_ref[...]) + _dot(hi, rwl_ref[...]) + rb_ref[...]
        ids, wts = _route(logits)
        ids_ref[...] = ids
        wts_ref[...] = wts

    @pl.when(i < n_ctx_tiles)
    def _():
        body(pool_p, attn_p, x_p)

    @pl.when(i >= n_ctx_tiles)
    def _():
        body(pool_s, attn_s, x_s)


def _out_projection(ctx, lat, rows, mod3, g_ffn, w_out_b, rw_hi, rw_lo, rb):
    tm = TOKEN_TILE
    n_ctx = ctx[2].shape[0] // tm
    n_lat = lat[2].shape[0] // tm
    t_all = (n_ctx + n_lat) * tm

    def ctx_map(i, r):
        return (jnp.minimum(i, n_ctx - 1), 0)

    def lat_map(i, r):
        return (jnp.maximum(i - n_ctx, 0), 0)

    in_specs = [
        pl.BlockSpec((tm, POOL_WIDTH), ctx_map),
        pl.BlockSpec((tm, ATTN_WIDTH), ctx_map),
        pl.BlockSpec((tm, D_MODEL), ctx_map),
        pl.BlockSpec((tm, POOL_WIDTH), lat_map),
        pl.BlockSpec((tm, ATTN_WIDTH), lat_map),
        pl.BlockSpec((tm, D_MODEL), lat_map),
        pl.BlockSpec((1, 6, D_MODEL), lambda i, r: (r[i], 0, 0)),
        pl.BlockSpec((1, D_MODEL), lambda i, r: (0, 0)),
        pl.BlockSpec((D_MODEL, D_MODEL), lambda i, r: (0, 0), pipeline_mode=pl.Buffered(1)),
        pl.BlockSpec((D_MODEL, ROUTER_LANES), lambda i, r: (0, 0)),
        pl.BlockSpec((D_MODEL, ROUTER_LANES), lambda i, r: (0, 0)),
        pl.BlockSpec((1, ROUTER_LANES), lambda i, r: (0, 0)),
    ]
    out_shape = (
        jax.ShapeDtypeStruct((t_all, D_MODEL), F32),
        jax.ShapeDtypeStruct((t_all, PACKED), jnp.uint32),
        jax.ShapeDtypeStruct((t_all, ROUTER_LANES), jnp.int32),
        jax.ShapeDtypeStruct((t_all, ROUTER_LANES), F32),
    )
    out_specs = (
        pl.BlockSpec((tm, D_MODEL), lambda i, r: (i, 0)),
        pl.BlockSpec((tm, PACKED), lambda i, r: (i, 0)),
        pl.BlockSpec((tm, ROUTER_LANES), lambda i, r: (i, 0)),
        pl.BlockSpec((tm, ROUTER_LANES), lambda i, r: (i, 0)),
    )
    return pl.pallas_call(
        functools.partial(_outproj_kernel, n_ctx_tiles=n_ctx),
        grid_spec=pltpu.PrefetchScalarGridSpec(
            num_scalar_prefetch=1, grid=(n_ctx + n_lat,), in_specs=in_specs, out_specs=out_specs),
        out_shape=out_shape,
        compiler_params=_params(1),
        name="out_projection",
    )(rows, *ctx, *lat, mod3, g_ffn, w_out_b, rw_hi, rw_lo, rb)


def _gather_rows(idx_ref, base, n_rows, src_hbm, dst, sem, *, unrolled, both_queues=False, dst_row0=0):
    def one(r, priority):
        tok = idx_ref[base + r]
        pltpu.make_async_copy(
            src_hbm.at[pl.ds(tok, 1)], dst.at[pl.ds(dst_row0 + r, 1)], sem).start(priority=priority)

    if unrolled:
        for r in range(n_rows):
            one(r, r % 2 if both_queues else 0)
    else:
        def body(r, carry):
            one(r, 0)
            return carry
        lax.fori_loop(0, n_rows, body, 0, unroll=8)


def _wait_rows(src_hbm, dst, sem):
    pltpu.make_async_copy(src_hbm.at[pl.ds(0, dst.shape[0])], dst, sem).wait()


def _late_zero(x):
    u = lax.bitcast_convert_type(x, jnp.uint32)
    return ((u >> 16) >> 16).astype(jnp.int32)[0, 0]


def _moe_kernel(be_ref, ne_ref, nu_ref, tok_ref, h_hbm, wg_hbm, wu_hbm, wd_hbm, y_ref,
                xg0, xg1, xg2, sg, su, sd, wgb, wub, wdb, sem, wsem):
    b = pl.program_id(0)
    n_used = nu_ref[0]
    last = n_used - 1
    active = b < n_used
    bufs = (xg0, xg1, xg2)
    n_buf = len(bufs)

    def weight_copies(e):
        return (pltpu.make_async_copy(wg_hbm.at[e], sg, wsem.at[0]),
                pltpu.make_async_copy(wu_hbm.at[e], su, wsem.at[1]),
                pltpu.make_async_copy(wd_hbm.at[e], sd, wsem.at[2]))

    @pl.when(b == 0)
    def _():
        for cp in weight_copies(be_ref[0]):
            cp.start(priority=1)
        for i in range(n_buf - 1):
            _gather_rows(tok_ref, jnp.minimum(i, last) * EXPERT_ROWS, EXPERT_ROWS, h_hbm, bufs[i],
                         sem.at[i], unrolled=False)

    new_expert = (b == 0) | (be_ref[b] != be_ref[jnp.maximum(b - 1, 0)])

    @pl.when(active & new_expert)
    def _():
        for cp in weight_copies(be_ref[b]):
            cp.wait()
        wgb[...] = sg[...].astype(BF16)
        wub[...] = su[...].astype(BF16)
        wdb[...] = sd[...].astype(BF16)

        @pl.when(ne_ref[b] >= 0)
        def _():
            for cp in weight_copies(ne_ref[b]):
                cp.start(priority=1)

    def step(slot):
        cur = bufs[slot]
        ahead = bufs[(slot + n_buf - 1) % n_buf]
        sem_ahead = sem.at[(slot + n_buf - 1) % n_buf]
        _wait_rows(h_hbm, cur, sem.at[slot])
        base = jnp.minimum(b + n_buf - 1, last) * EXPERT_ROWS
        per = EXPERT_ROWS // GATHER_GROUPS
        group = iter(range(GATHER_GROUPS))

        def issue(zero):
            g = next(group)
            _gather_rows(tok_ref, base + g * per + zero, per, h_hbm, ahead, sem_ahead, unrolled=True,
                         dst_row0=g * per)

        def corner(x, r, c):
            return _late_zero(x[r:r + 1, c:c + 1])

        bm, de = EXPERT_ROWS, D_EXPERT
        lo, hi = _unpack_bf16_pairs(cur[...])
        lo = lo.astype(BF16)
        hi = hi.astype(BF16)
        issue(0)
        a1 = _dot(lo, wgb[0:PACKED, :])
        issue(corner(a1, bm - 1, de - 1))
        a = a1 + _dot(hi, wgb[PACKED:D_MODEL, :])
        issue(corner(a, bm // 2 - 1, de // 2 - 1))
        issue(corner(a, bm - 1, de - 1))
        u1 = _dot(lo, wub[0:PACKED, :])
        issue(corner(u1, bm - 1, de - 1))
        u = u1 + _dot(hi, wub[PACKED:D_MODEL, :])
        issue(corner(u, bm // 2 - 1, de // 2 - 1))
        issue(corner(u, bm - 1, de - 1))
        act = (a / (1.0 + jnp.exp(-a)) * u).astype(BF16)
        y = _dot(act, wdb[...])
        n_left = GATHER_GROUPS - 7
        for i in range(n_left):
            issue(corner(y, bm - 1, (i + 1) * D_MODEL // n_left - 1))
        y_ref[...] = _pack_bf16_pairs(y)

        @pl.when(b == last)
        def _():
            for i in range(1, n_buf):
                _wait_rows(h_hbm, bufs[(slot + i) % n_buf], sem.at[(slot + i) % n_buf])

    for slot in range(n_buf):
        @pl.when(active & (b % n_buf == slot))
        def _(slot=slot):
            step(slot)

    @pl.when(b >= n_used)
    def _():
        y_ref[...] = jnp.zeros_like(y_ref)


def _experts(block_expert, next_expert, n_used, buf_tok, h2_packed, w_gate, w_up, w_down):
    cap = buf_tok.shape[0]
    bm = EXPERT_ROWS
    return pl.pallas_call(
        _moe_kernel,
        grid_spec=pltpu.PrefetchScalarGridSpec(
            num_scalar_prefetch=4,
            grid=(cap // bm,),
            in_specs=[pl.BlockSpec(memory_space=pl.ANY)] * 4,
            out_specs=pl.BlockSpec((bm, PACKED), lambda b, *_: (b, 0)),
            scratch_shapes=[
                pltpu.VMEM((bm, PACKED), jnp.uint32), pltpu.VMEM((bm, PACKED), jnp.uint32),
                pltpu.VMEM((bm, PACKED), jnp.uint32),
                pltpu.VMEM((D_MODEL, D_EXPERT), F32), pltpu.VMEM((D_MODEL, D_EXPERT), F32),
                pltpu.VMEM((D_EXPERT, D_MODEL), F32),
                pltpu.VMEM((D_MODEL, D_EXPERT), BF16), pltpu.VMEM((D_MODEL, D_EXPERT), BF16),
                pltpu.VMEM((D_EXPERT, D_MODEL), BF16),
                pltpu.SemaphoreType.DMA((3,)), pltpu.SemaphoreType.DMA((3,)),
            ],
        ),
        out_shape=jax.ShapeDtypeStruct((cap, PACKED), jnp.uint32),
        compiler_params=_params(1),
        name="experts",
    )(block_expert, next_expert, n_used, buf_tok, h2_packed, w_gate, w_up, w_down)


def _combine_kernel(slot_ref, rows_ref, y_hbm, x1_ref, wts_ref, mod_ref, g_ref, op_ref, os_ref, yg0, yg1, sem,
                    *, n_ctx_tiles, n_tiles):
    del rows_ref
    tm = TOKEN_TILE
    i = pl.program_id(0)
    n_slots = slot_ref.shape[0] // TOP_K

    def start(tile, buf, s, unrolled):
        for k in range(TOP_K):
            _gather_rows(slot_ref, k * n_slots + tile * tm, tm, y_hbm, buf.at[k], s, unrolled=unrolled,
                         both_queues=True)

    @pl.when(i == 0)
    def _():
        start(0, yg0, sem.at[0], False)

    def step(cur, nxt, sem_cur, sem_nxt):
        for k in range(TOP_K):
            _wait_rows(y_hbm, cur.at[k], sem_cur)
        start(jnp.minimum(i + 1, n_tiles - 1), nxt, sem_nxt, True)
        w = wts_ref[...]
        lo0, hi0 = _unpack_bf16_pairs(cur[0])
        lo1, hi1 = _unpack_bf16_pairs(cur[1])
        ffn = jnp.concatenate([lo0 * w[:, 0:1] + lo1 * w[:, 1:2], hi0 * w[:, 0:1] + hi1 * w[:, 1:2]], axis=-1)
        x2 = x1_ref[...] + mod_ref[0][5:6] * ffn
        out = _rms(x2) * g_ref[...]

        @pl.when(i < n_ctx_tiles)
        def _():
            op_ref[...] = out

        @pl.when(i >= n_ctx_tiles)
        def _():
            os_ref[...] = out

        @pl.when(i == n_tiles - 1)
        def _():
            for k in range(TOP_K):
                _wait_rows(y_hbm, nxt.at[k], sem_nxt)

    @pl.when(i % 2 == 0)
    def _():
        step(yg0, yg1, sem.at[0], sem.at[1])

    @pl.when(i % 2 == 1)
    def _():
        step(yg1, yg0, sem.at[1], sem.at[0])


def _combine(slots, rows, y_buf, x1, wts, mod3, g_final, t_ctx):
    t = x1.shape[0]
    tm = TOKEN_TILE
    n_tiles = t // tm
    n_ctx = t_ctx // tm
    return pl.pallas_call(
        functools.partial(_combine_kernel, n_ctx_tiles=n_ctx, n_tiles=n_tiles),
        grid_spec=pltpu.PrefetchScalarGridSpec(
            num_scalar_prefetch=2,
            grid=(n_tiles,),
            in_specs=[
                pl.BlockSpec(memory_space=pl.ANY),
                pl.BlockSpec((tm, D_MODEL), lambda i, s, r: (i, 0)),
                pl.BlockSpec((tm, ROUTER_LANES), lambda i, s, r: (i, 0)),
                pl.BlockSpec((1, 6, D_MODEL), lambda i, s, r: (r[i], 0, 0)),
                pl.BlockSpec((1, D_MODEL), lambda i, s, r: (0, 0)),
            ],
            out_specs=(
                pl.BlockSpec((tm, D_MODEL), lambda i, s, r: (jnp.minimum(i, n_ctx - 1), 0)),
                pl.BlockSpec((tm, D_MODEL), lambda i, s, r: (jnp.maximum(i - n_ctx, 0), 0)),
            ),
            scratch_shapes=[pltpu.VMEM((TOP_K, tm, PACKED), jnp.uint32),
                            pltpu.VMEM((TOP_K, tm, PACKED), jnp.uint32),
                            pltpu.SemaphoreType.DMA((2,))],
        ),
        out_shape=(jax.ShapeDtypeStruct((t_ctx, D_MODEL), F32),
                   jax.ShapeDtypeStruct((t - t_ctx, D_MODEL), F32)),
        compiler_params=_params(1),
        name="combine",
    )(slots, rows, y_buf, x1, wts, mod3, g_final)


def _rope_tables(n_tokens):
    rows = n_tokens // GRID_W
    row = jnp.repeat(jnp.arange(rows, dtype=F32), GRID_W)
    col = jnp.tile(jnp.arange(GRID_W, dtype=F32), rows)
    half = ROPE_AXIS_DIM // 2
    inv_freq = ROPE_BASE ** (-jnp.arange(half, dtype=F32) / half)
    ar = row[:, None] * inv_freq
    ac = col[:, None] * inv_freq
    cos = jnp.concatenate([jnp.cos(ar), jnp.cos(ar), jnp.cos(ac), jnp.cos(ac)], axis=-1)
    sin = jnp.concatenate([-jnp.sin(ar), jnp.sin(ar), -jnp.sin(ac), jnp.sin(ac)], axis=-1)
    return cos, sin


def _dispatch_tables(ids):
    t = ids.shape[0]
    bm = EXPERT_ROWS
    flat_e = ids.reshape(t * TOP_K)
    onehot = (flat_e[:, None] == jnp.arange(N_EXPERTS, dtype=jnp.int32)[None, :]).astype(jnp.int32)
    running = jnp.cumsum(onehot, axis=0)
    rank = jnp.sum(running * onehot, axis=1) - 1
    counts = running[-1]
    padded = (counts + bm - 1) // bm * bm
    pad_end = jnp.cumsum(padded)
    pad_start = pad_end - padded
    dest = (jnp.sum(pad_start[None, :] * onehot, axis=1) + rank).astype(jnp.int32)
    n_blocks = (t * TOP_K + N_EXPERTS * (bm - 1) + bm - 1) // bm
    flat_tok = jnp.repeat(jnp.arange(t, dtype=jnp.int32), TOP_K)
    buf_tok = jnp.zeros((n_blocks * bm,), jnp.int32).at[dest].set(flat_tok)
    n_used = (pad_end[-1] // bm).astype(jnp.int32)
    blk = jnp.arange(n_blocks, dtype=jnp.int32)
    blk = jnp.minimum(blk, n_used - 1)
    block_expert = jnp.sum((pad_end[None, :] <= (blk * bm)[:, None]).astype(jnp.int32), axis=1)
    block_expert = jnp.minimum(block_expert, N_EXPERTS - 1)
    e_ids = jnp.arange(N_EXPERTS, dtype=jnp.int32)
    later = (e_ids[None, :] > e_ids[:, None]) & (counts[None, :] > 0)
    next_of = jnp.min(jnp.where(later, e_ids[None, :], N_EXPERTS), axis=1)
    next_of = jnp.where(next_of == N_EXPERTS, -1, next_of)
    next_expert = jnp.sum(next_of[None, :] * (block_expert[:, None] == e_ids[None, :]), axis=1)
    slots = dest.reshape(t, TOP_K).T.reshape(TOP_K * t)
    return (block_expert.astype(jnp.int32), next_expert.astype(jnp.int32), n_used.reshape(1), buf_tok, slots)


def kernel(x_prompt, x_sample, cache_k, cache_v, c, c_ctx, w_mod, b_mod, norm_mix_g, norm_ffn_g, w_in,
           pool_w, pool_scale, attn_sink, out_norm_pool_g, out_norm_attn_g, w_out, router_group_w,
           router_group_b, router_expert_w, router_expert_b, w_gate, w_up, w_down, final_norm_g):
    depth = w_mod.shape[0]
    assert depth == 1, "single trunk layer"
    bp, lp, _ = x_prompt.shape
    bs, ls, _ = x_sample.shape
    tp, ts = bp * lp, bs * ls
    tm = TOKEN_TILE
    l = 0

    cond8 = jnp.zeros((8, D_MODEL), F32).at[:bs].set(c).at[bs].set(c_ctx)
    mod3 = _modulation(cond8, w_mod[l], b_mod[l]).reshape(8, 6, D_MODEL)
    rows_p = jnp.full((tp // tm,), bs, jnp.int32)
    rows_s = jnp.arange(ts // tm, dtype=jnp.int32) // (ls // tm)

    w_in_b = w_in[l].astype(BF16)
    w_out_b = w_out[l].astype(BF16)
    pool_w_b = pool_w[l].astype(BF16)
    g_mix = norm_mix_g[l].reshape(1, D_MODEL)
    g_ffn = norm_ffn_g[l].reshape(1, D_MODEL)
    g_pool = out_norm_pool_g[l].reshape(1, POOL_WIDTH)
    g_attn = out_norm_attn_g[l].reshape(1, ATTN_WIDTH)
    p_scale = pool_scale[l].reshape(1, POOL_WIDTH)
    sink_b = jnp.broadcast_to(attn_sink[l][:, None], (N_HEADS, LANES))
    g_final = final_norm_g.reshape(1, D_MODEL)

    rw = jnp.concatenate(
        [router_group_w[l], jnp.transpose(router_expert_w[l], (1, 0, 2)).reshape(D_MODEL, N_EXPERTS)], axis=1)
    rw = jnp.pad(rw, ((0, 0), (0, ROUTER_LANES - rw.shape[1])))
    rw_hi = rw.astype(BF16)
    rw_lo = (rw - rw_hi.astype(F32)).astype(BF16)
    rb = jnp.concatenate([router_group_b[l], router_expert_b[l].reshape(N_EXPERTS)])
    rb = jnp.pad(rb, (0, ROUTER_LANES - rb.shape[0])).reshape(1, ROUTER_LANES)

    xp = x_prompt.reshape(tp, D_MODEL)
    xs = x_sample.reshape(ts, D_MODEL)

    up_p, q_p, k_p, v_p = _in_projection(xp, rows_p, mod3, g_mix, w_in_b, None, F32)
    pool_p = _pool_mixer(up_p.reshape(bp, lp, POOL_WIDTH), pool_w_b, p_scale, g_pool).reshape(tp, POOL_WIDTH)
    attn_p = _context_attention(q_p, k_p, v_p, sink_b, g_attn, lp)
    up_s, q_s, k_s, v_s = _in_projection(xs, rows_s, mod3, g_mix, w_in_b, _rope_tables(ls), BF16)
    pool_s = _pool_mixer(up_s.reshape(bs, ls, POOL_WIDTH), pool_w_b, p_scale, g_pool).reshape(ts, POOL_WIDTH)
    attn_s = _latent_attention(
        q_s, k_s.reshape(bs, ls, KV_WIDTH), v_s.reshape(bs, ls, KV_WIDTH),
        cache_k[:, l].reshape(bs, -1, KV_WIDTH), cache_v[:, l].reshape(bs, -1, KV_WIDTH), sink_b, g_attn)

    rows_all = jnp.concatenate([rows_p, rows_s])
    x1_all, h2_all, ids_all, wts_all = _out_projection(
        (pool_p, attn_p, xp), (pool_s, attn_s, xs), rows_all, mod3, g_ffn, w_out_b, rw_hi, rw_lo, rb)

    block_expert, next_expert, n_used, buf_tok, slots = _dispatch_tables(ids_all[:, :TOP_K])
    y_buf = _experts(block_expert, next_expert, n_used, buf_tok, h2_all, w_gate[l], w_up[l], w_down[l])

    y_p, y_s = _combine(slots, rows_all, y_buf, x1_all, wts_all, mod3, g_final, tp)

    kv_shape = (bp, 1, lp, N_KV_HEADS, HEAD_DIM)
    return (y_p.reshape(bp, lp, D_MODEL), y_s.reshape(bs, ls, D_MODEL),
            k_p.reshape(kv_shape), v_p.reshape(kv_shape))
```

```python
import functools

import jax
import jax.numpy as jnp
from jax import lax
from jax.experimental import pallas as pl
from jax.experimental.pallas import tpu as pltpu

F32 = jnp.float32
BF16 = jnp.bfloat16

D_MODEL = 2048
GRID_W = 64
HEAD_DIM = 128
ATTN_WIDTH = 1024
POOL_WIDTH = 1024
N_HEADS = 8
N_KV_HEADS = 2
Q_PER_KV = 4
KV_WIDTH = 256
IN_WIDTH = 2560
WINDOW = 128
POOL_WINDOWS = (2, 4, 8, 16)
POOL_GROUP_DIM = 256
ROPE_BASE = 10000.0
ROPE_AXIS_DIM = 64
N_EXPERT_GROUPS = 4
EXPERTS_PER_GROUP = 8
N_EXPERTS = 32
TOP_K = 2
D_EXPERT = 512
NORM_EPS = 1e-6
NEG_INF = -1e30
ATTN_SCALE = HEAD_DIM ** -0.5
LOG2E = 1.4426950408889634
Q_SCALE = ATTN_SCALE * LOG2E

LANES = 128
Q_BLOCKS_PER_STEP = 4
CTX_SEQS_PER_STEP = 2
VMEM_LIMIT = 48 * 1024 * 1024
PACKED = D_MODEL // 2
TOKEN_TILE = 256
IN_TILE = 512
POOL_HALO = 16
Q_BLOCK = 128
EXPERT_ROWS = 256
GATHER_GROUPS = 16
ROUTER_LANES = 128


def _params(n_grid):
    return pltpu.CompilerParams(
        dimension_semantics=("arbitrary",) * n_grid, vmem_limit_bytes=VMEM_LIMIT)


def _rms(x):
    return x * lax.rsqrt(jnp.mean(x * x, axis=-1, keepdims=True) + NORM_EPS)


def _dot(a, b):
    return jnp.dot(a, b, preferred_element_type=F32)


def _dot_nt(a, b):
    return lax.dot_general(a, b, (((1,), (1,)), ((), ())), preferred_element_type=F32)


def _pack_bf16_pairs(x):
    c = x.shape[1] // 2
    lo = lax.bitcast_convert_type(x[:, :c].astype(BF16).astype(F32), jnp.uint32)
    hi = lax.bitcast_convert_type(x[:, c:].astype(BF16).astype(F32), jnp.uint32)
    return hi | (lo >> 16)


def _unpack_bf16_pairs(u):
    lo = lax.bitcast_convert_type(u << 16, F32)
    hi = lax.bitcast_convert_type(u & jnp.uint32(0xFFFF0000), F32)
    return lo, hi


def _mod_kernel(cond_ref, w_ref, b_ref, o_ref):
    c = cond_ref[...]
    s = c / (1.0 + jnp.exp(-c))
    o_ref[...] = _dot(s.astype(BF16), w_ref[...].astype(BF16)) + b_ref[...]


def _modulation(cond8, w_mod, b_mod):
    n = w_mod.shape[1]
    tn = 1024
    return pl.pallas_call(
        _mod_kernel,
        grid=(n // tn,),
        in_specs=[
            pl.BlockSpec((8, D_MODEL), lambda j: (0, 0)),
            pl.BlockSpec((D_MODEL, tn), lambda j: (0, j)),
            pl.BlockSpec((1, tn), lambda j: (0, j)),
        ],
        out_specs=pl.BlockSpec((8, tn), lambda j: (0, j)),
        out_shape=jax.ShapeDtypeStruct((8, n), F32),
        compiler_params=_params(1),
        name="modulation",
    )(cond8, w_mod, b_mod.reshape(1, n))


def _rope(x, cos, sin_signed):
    lane = lax.broadcasted_iota(jnp.int32, x.shape, 1)
    partner = jnp.where((lane & 63) < 32,
                        pltpu.roll(x, HEAD_DIM - 32, axis=1),
                        pltpu.roll(x, 32, axis=1))
    return x * cos + partner * sin_signed


def _inproj_kernel(rows_ref, x_ref, mod_ref, g_ref, w_ref, *rest, rope):
    del rows_ref
    if rope:
        cos_ref, sin_ref, up_ref, q_ref, k_ref, v_ref = rest
    else:
        up_ref, q_ref, k_ref, v_ref = rest
    m = mod_ref[0]
    h = _rms(x_ref[...]) * g_ref[...]
    hb = (h * (1.0 + m[1:2]) + m[0:1]).astype(BF16)
    up_ref[...] = _dot(hb, w_ref[:, 0:POOL_WIDTH]).astype(BF16)
    q = _dot(hb, w_ref[:, POOL_WIDTH:POOL_WIDTH + ATTN_WIDTH])
    k = _dot(hb, w_ref[:, POOL_WIDTH + ATTN_WIDTH:IN_WIDTH - KV_WIDTH])
    v = _dot(hb, w_ref[:, IN_WIDTH - KV_WIDTH:IN_WIDTH])
    if rope:
        cos = cos_ref[...]
        sin = sin_ref[...]
        for hd in range(N_HEADS):
            sl = slice(hd * HEAD_DIM, (hd + 1) * HEAD_DIM)
            q_ref[:, sl] = (_rope(q[:, sl], cos, sin) * Q_SCALE).astype(BF16)
        for hd in range(N_KV_HEADS):
            sl = slice(hd * HEAD_DIM, (hd + 1) * HEAD_DIM)
            k_ref[:, sl] = _rope(k[:, sl], cos, sin).astype(k_ref.dtype)
    else:
        q_ref[...] = (q * Q_SCALE).astype(BF16)
        k_ref[...] = k.astype(k_ref.dtype)
    v_ref[...] = v.astype(v_ref.dtype)


def _in_projection(x, rows, mod3, g_mix, w_in_b, rope_tabs, kv_dtype):
    t = x.shape[0]
    tm = IN_TILE
    per = IN_TILE // TOKEN_TILE
    rope = rope_tabs is not None
    in_specs = [
        pl.BlockSpec((tm, D_MODEL), lambda i, r: (i, 0)),
        pl.BlockSpec((1, 6, D_MODEL), lambda i, r: (r[i * per], 0, 0)),
        pl.BlockSpec((1, D_MODEL), lambda i, r: (0, 0)),
        pl.BlockSpec((D_MODEL, IN_WIDTH), lambda i, r: (0, 0), pipeline_mode=pl.Buffered(1)),
    ]
    args = [x, mod3, g_mix, w_in_b]
    if rope:
        seq_tiles = rope_tabs[0].shape[0] // tm
        for tab in rope_tabs:
            in_specs.append(pl.BlockSpec((tm, HEAD_DIM), lambda i, r: (i % seq_tiles, 0)))
            args.append(tab)
    out_shape = (
        jax.ShapeDtypeStruct((t, POOL_WIDTH), BF16),
        jax.ShapeDtypeStruct((t, ATTN_WIDTH), BF16),
        jax.ShapeDtypeStruct((t, KV_WIDTH), kv_dtype),
        jax.ShapeDtypeStruct((t, KV_WIDTH), kv_dtype),
    )
    out_specs = (
        pl.BlockSpec((tm, POOL_WIDTH), lambda i, r: (i, 0)),
        pl.BlockSpec((tm, ATTN_WIDTH), lambda i, r: (i, 0)),
        pl.BlockSpec((tm, KV_WIDTH), lambda i, r: (i, 0)),
        pl.BlockSpec((tm, KV_WIDTH), lambda i, r: (i, 0)),
    )
    return pl.pallas_call(
        functools.partial(_inproj_kernel, rope=rope),
        grid_spec=pltpu.PrefetchScalarGridSpec(
            num_scalar_prefetch=1, grid=(t // tm,), in_specs=in_specs, out_specs=out_specs),
        out_shape=out_shape,
        compiler_params=_params(1),
        name="in_projection_rope" if rope else "in_projection",
    )(rows, *args)


def _pool_kernel(u_ref, pw_ref, ps_ref, g_ref, o_ref, *, seq_len):
    tq = TOKEN_TILE
    win = min(seq_len, tq + 2 * POOL_HALO)
    t0 = pl.program_id(1) * tq
    src0 = pl.multiple_of(jnp.clip(t0 - POOL_HALO, 0, seq_len - win), POOL_HALO)
    u = u_ref[0, pl.ds(src0, win), :]
    t = t0 + lax.broadcasted_iota(jnp.int32, (tq, win), 0)
    j = src0 + lax.broadcasted_iota(jnp.int32, (tq, win), 1)
    tc = t0 + lax.broadcasted_iota(jnp.int32, (tq, 1), 0)
    ys = []
    for gi, w in enumerate(POOL_WINDOWS):
        lo = jnp.maximum(t - w // 2, 0)
        hi = jnp.minimum(t + (w - w // 2), seq_len)
        cnt = (hi - lo).astype(F32)
        a = jnp.where((j >= lo) & (j < hi), 1.0, 0.0) - jnp.where(j == t, cnt, 0.0)
        cnt_col = (jnp.minimum(tc + (w - w // 2), seq_len) - jnp.maximum(tc - w // 2, 0)).astype(F32)
        sl = slice(gi * POOL_GROUP_DIM, (gi + 1) * POOL_GROUP_DIM)
        d = _dot(a.astype(BF16), u[:, sl]) / cnt_col
        ys.append(_dot(d.astype(BF16), pw_ref[gi]))
    y = jnp.concatenate(ys, axis=-1) * ps_ref[...]
    o_ref[0] = (_rms(y) * g_ref[...]).astype(BF16)


def _pool_mixer(up, pool_w_b, pool_scale, g_out_pool):
    b, seq_len, _ = up.shape
    tq = TOKEN_TILE
    return pl.pallas_call(
        functools.partial(_pool_kernel, seq_len=seq_len),
        grid=(b, seq_len // tq),
        in_specs=[
            pl.BlockSpec((1, seq_len, POOL_WIDTH), lambda bi, i: (bi, 0, 0)),
            pl.BlockSpec((len(POOL_WINDOWS), POOL_GROUP_DIM, POOL_GROUP_DIM), lambda bi, i: (0, 0, 0)),
            pl.BlockSpec((1, POOL_WIDTH), lambda bi, i: (0, 0)),
            pl.BlockSpec((1, POOL_WIDTH), lambda bi, i: (0, 0)),
        ],
        out_specs=pl.BlockSpec((1, tq, POOL_WIDTH), lambda bi, i: (bi, i, 0)),
        out_shape=jax.ShapeDtypeStruct((b, seq_len, POOL_WIDTH), BF16),
        compiler_params=_params(2),
        name=f"pool_mixer_{seq_len}",
    )(up, pool_w_b, pool_scale, g_out_pool)


def _stack_heads(q, kh, rows):
    return jnp.concatenate(
        [q[:, (kh * Q_PER_KV + g) * HEAD_DIM:(kh * Q_PER_KV + g + 1) * HEAD_DIM]
         for g in range(Q_PER_KV)], axis=0)


def _sink_column(sink_ref, kh, rows):
    return jnp.concatenate(
        [jnp.broadcast_to(sink_ref[kh * Q_PER_KV + g:kh * Q_PER_KV + g + 1, 0:1] * LOG2E, (rows, 1))
         for g in range(Q_PER_KV)], axis=0)


def _attend(q4, key_sets, value_sets, sk, band=None):
    s = [_dot_nt(q4, k) for k in key_sets]
    if band is not None:
        s[0] = jnp.where(band, s[0], NEG_INF)
    m = sk
    for si in s:
        m = jnp.maximum(m, jnp.max(si, axis=-1, keepdims=True))
    acc = None
    for si, v in zip(s, value_sets):
        v1 = jnp.concatenate([v, jnp.ones_like(v)], axis=-1)
        part = _dot(jnp.exp2(si - m).astype(BF16), v1)
        acc = part if acc is None else acc + part
    den = acc[:, HEAD_DIM:] + jnp.exp2(sk - m)
    return acc[:, :HEAD_DIM] / den


def _ctx_attn_kernel(q_ref, k_ref, v_ref, sink_ref, g_ref, o_ref, *, seq_len):
    rows = seq_len
    for j in range(q_ref.shape[0] // rows):
        rs = slice(j * rows, (j + 1) * rows)
        q = q_ref[rs, :]
        heads = [None] * N_HEADS
        for kh in range(N_KV_HEADS):
            sl = slice(kh * HEAD_DIM, (kh + 1) * HEAD_DIM)
            o = _attend(_stack_heads(q, kh, rows), [k_ref[rs, sl].astype(BF16)],
                        [v_ref[rs, sl].astype(BF16)], _sink_column(sink_ref, kh, rows))
            for g in range(Q_PER_KV):
                heads[kh * Q_PER_KV + g] = o[g * rows:(g + 1) * rows]
        y = jnp.concatenate(heads, axis=-1)
        o_ref[rs, :] = (_rms(y) * g_ref[...]).astype(BF16)


def _context_attention(q, k, v, sink_b, g_out_attn, seq_len):
    t = q.shape[0]
    blk = seq_len * CTX_SEQS_PER_STEP
    return pl.pallas_call(
        functools.partial(_ctx_attn_kernel, seq_len=seq_len),
        grid=(t // blk,),
        in_specs=[
            pl.BlockSpec((blk, ATTN_WIDTH), lambda b: (b, 0)),
            pl.BlockSpec((blk, KV_WIDTH), lambda b: (b, 0)),
            pl.BlockSpec((blk, KV_WIDTH), lambda b: (b, 0)),
            pl.BlockSpec((N_HEADS, LANES), lambda b: (0, 0)),
            pl.BlockSpec((1, ATTN_WIDTH), lambda b: (0, 0)),
        ],
        out_specs=pl.BlockSpec((blk, ATTN_WIDTH), lambda b: (b, 0)),
        out_shape=jax.ShapeDtypeStruct((t, ATTN_WIDTH), BF16),
        compiler_params=_params(1),
        name="context_attention",
    )(q, k, v, sink_b, g_out_attn)


def _lat_attn_kernel(q_ref, k_ref, v_ref, ck_ref, cv_ref, sink_ref, g_ref, o_ref, *, seq_len):
    rows = Q_BLOCK
    span = 3 * Q_BLOCK
    shape = (Q_PER_KV * rows, span)
    row_in_block = lax.broadcasted_iota(jnp.int32, shape, 0) & (rows - 1)
    col = lax.broadcasted_iota(jnp.int32, shape, 1)
    for j in range(Q_BLOCKS_PER_STEP):
        q0 = (pl.program_id(1) * Q_BLOCKS_PER_STEP + j) * rows
        start = pl.multiple_of(jnp.clip(q0 - Q_BLOCK, 0, seq_len - span), Q_BLOCK)
        q = q_ref[j * rows:(j + 1) * rows, :]
        kl = k_ref[0, pl.ds(start, span), :]
        vl = v_ref[0, pl.ds(start, span), :]
        band = jnp.abs(row_in_block - col + (q0 - start)) <= WINDOW
        heads = [None] * N_HEADS
        for kh in range(N_KV_HEADS):
            sl = slice(kh * HEAD_DIM, (kh + 1) * HEAD_DIM)
            o = _attend(_stack_heads(q, kh, rows),
                        [kl[:, sl], ck_ref[0, :, sl].astype(BF16)],
                        [vl[:, sl], cv_ref[0, :, sl].astype(BF16)],
                        _sink_column(sink_ref, kh, rows), band)
            for g in range(Q_PER_KV):
                heads[kh * Q_PER_KV + g] = o[g * rows:(g + 1) * rows]
        y = jnp.concatenate(heads, axis=-1)
        o_ref[j * rows:(j + 1) * rows, :] = (_rms(y) * g_ref[...]).astype(BF16)


def _latent_attention(q, k, v, cache_k, cache_v, sink_b, g_out_attn):
    b, seq_len, _ = k.shape
    q_rows = Q_BLOCK * Q_BLOCKS_PER_STEP
    nq = seq_len // q_rows
    past = cache_k.shape[1]
    return pl.pallas_call(
        functools.partial(_lat_attn_kernel, seq_len=seq_len),
        grid=(b, nq),
        in_specs=[
            pl.BlockSpec((q_rows, ATTN_WIDTH), lambda bi, n: (bi * nq + n, 0)),
            pl.BlockSpec((1, seq_len, KV_WIDTH), lambda bi, n: (bi, 0, 0)),
            pl.BlockSpec((1, seq_len, KV_WIDTH), lambda bi, n: (bi, 0, 0)),
            pl.BlockSpec((1, past, KV_WIDTH), lambda bi, n: (bi, 0, 0)),
            pl.BlockSpec((1, past, KV_WIDTH), lambda bi, n: (bi, 0, 0)),
            pl.BlockSpec((N_HEADS, LANES), lambda bi, n: (0, 0)),
            pl.BlockSpec((1, ATTN_WIDTH), lambda bi, n: (0, 0)),
        ],
        out_specs=pl.BlockSpec((q_rows, ATTN_WIDTH), lambda bi, n: (bi * nq + n, 0)),
        out_shape=jax.ShapeDtypeStruct((b * seq_len, ATTN_WIDTH), BF16),
        compiler_params=_params(2),
        name="latent_attention",
    )(q, k, v, cache_k, cache_v, sink_b, g_out_attn)


def _route(logits):
    lane = lax.broadcasted_iota(jnp.int32, logits.shape, 1).astype(F32)
    neg = -jnp.inf

    def first_argmax(x):
        mx = jnp.max(x, axis=-1, keepdims=True)
        return mx, jnp.min(jnp.where(x == mx, lane, float(ROUTER_LANES)), axis=-1, keepdims=True)

    gl = jnp.where(lane < N_EXPERT_GROUPS, logits, neg)
    gmax, g_idx = first_argmax(gl)
    p_g = 1.0 / jnp.sum(jnp.exp(gl - gmax), axis=-1, keepdims=True)
    base = N_EXPERT_GROUPS + EXPERTS_PER_GROUP * g_idx
    el = jnp.where((lane >= base) & (lane < base + EXPERTS_PER_GROUP), logits, neg)
    v1, i1 = first_argmax(el)
    v2, i2 = first_argmax(jnp.where(lane == i1, neg, el))
    e2 = jnp.exp(v2 - v1)
    w1 = p_g / (1.0 + e2)
    w2 = p_g * e2 / (1.0 + e2)
    ids = jnp.where(lane == 0.0, i1 - N_EXPERT_GROUPS, jnp.where(lane == 1.0, i2 - N_EXPERT_GROUPS, 0.0))
    wts = jnp.where(lane == 0.0, w1, jnp.where(lane == 1.0, w2, 0.0))
    return ids.astype(jnp.int32), wts


def _outproj_kernel(rows_ref, pool_p, attn_p, x_p, pool_s, attn_s, x_s, mod_ref, g_ref, w_ref, rwh_ref,
                    rwl_ref, rb_ref, x1_ref, h2_ref, ids_ref, wts_ref, *, n_ctx_tiles):
    del rows_ref
    i = pl.program_id(0)

    def body(pool_ref, attn_ref, x_ref):
        m = mod_ref[0]
        mix = (_dot(pool_ref[...], w_ref[0:POOL_WIDTH, :])
               + _dot(attn_ref[...], w_ref[POOL_WIDTH:POOL_WIDTH + ATTN_WIDTH, :]))
        x1 = x_ref[...] + m[2:3] * mix
        x1_ref[...] = x1
        h2 = _rms(x1) * g_ref[...] * (1.0 + m[4:5]) + m[3:4]
        h2_ref[...] = _pack_bf16_pairs(h2)
        hi = h2.astype(BF16)
        lo = (h2 - hi.astype(F32)).astype(BF16)
        logits = _dot(hi, rwh_ref[...]) + _dot(lo, rwh_ref[...]) + _dot(hi, rwl_ref[...]) + rb_ref[...]
        ids, wts = _route(logits)
        ids_ref[...] = ids
        wts_ref[...] = wts

    @pl.when(i < n_ctx_tiles)
    def _():
        body(pool_p, attn_p, x_p)

    @pl.when(i >= n_ctx_tiles)
    def _():
        body(pool_s, attn_s, x_s)


def _out_projection(ctx, lat, rows, mod3, g_ffn, w_out_b, rw_hi, rw_lo, rb):
    tm = TOKEN_TILE
    n_ctx = ctx[2].shape[0] // tm
    n_lat = lat[2].shape[0] // tm
    t_all = (n_ctx + n_lat) * tm

    def ctx_map(i, r):
        return (jnp.minimum(i, n_ctx - 1), 0)

    def lat_map(i, r):
        return (jnp.maximum(i - n_ctx, 0), 0)

    in_specs = [
        pl.BlockSpec((tm, POOL_WIDTH), ctx_map),
        pl.BlockSpec((tm, ATTN_WIDTH), ctx_map),
        pl.BlockSpec((tm, D_MODEL), ctx_map),
        pl.BlockSpec((tm, POOL_WIDTH), lat_map),
        pl.BlockSpec((tm, ATTN_WIDTH), lat_map),
        pl.BlockSpec((tm, D_MODEL), lat_map),
        pl.BlockSpec((1, 6, D_MODEL), lambda i, r: (r[i], 0, 0)),
        pl.BlockSpec((1, D_MODEL), lambda i, r: (0, 0)),
        pl.BlockSpec((D_MODEL, D_MODEL), lambda i, r: (0, 0), pipeline_mode=pl.Buffered(1)),
        pl.BlockSpec((D_MODEL, ROUTER_LANES), lambda i, r: (0, 0)),
        pl.BlockSpec((D_MODEL, ROUTER_LANES), lambda i, r: (0, 0)),
        pl.BlockSpec((1, ROUTER_LANES), lambda i, r: (0, 0)),
    ]
    out_shape = (
        jax.ShapeDtypeStruct((t_all, D_MODEL), F32),
        jax.ShapeDtypeStruct((t_all, PACKED), jnp.uint32),
        jax.ShapeDtypeStruct((t_all, ROUTER_LANES), jnp.int32),
        jax.ShapeDtypeStruct((t_all, ROUTER_LANES), F32),
    )
    out_specs = (
        pl.BlockSpec((tm, D_MODEL), lambda i, r: (i, 0)),
        pl.BlockSpec((tm, PACKED), lambda i, r: (i, 0)),
        pl.BlockSpec((tm, ROUTER_LANES), lambda i, r: (i, 0)),
        pl.BlockSpec((tm, ROUTER_LANES), lambda i, r: (i, 0)),
    )
    return pl.pallas_call(
        functools.partial(_outproj_kernel, n_ctx_tiles=n_ctx),
        grid_spec=pltpu.PrefetchScalarGridSpec(
            num_scalar_prefetch=1, grid=(n_ctx + n_lat,), in_specs=in_specs, out_specs=out_specs),
        out_shape=out_shape,
        compiler_params=_params(1),
        name="out_projection",
    )(rows, *ctx, *lat, mod3, g_ffn, w_out_b, rw_hi, rw_lo, rb)


def _gather_rows(idx_ref, base, n_rows, src_hbm, dst, sem, *, unrolled, both_queues=False, dst_row0=0):
    def one(r, priority):
        tok = idx_ref[base + r]
        pltpu.make_async_copy(
            src_hbm.at[pl.ds(tok, 1)], dst.at[pl.ds(dst_row0 + r, 1)], sem).start(priority=priority)

    if unrolled:
        for r in range(n_rows):
            one(r, r % 2 if both_queues else 0)
    else:
        def body(r, carry):
            one(r, 0)
            return carry
        lax.fori_loop(0, n_rows, body, 0, unroll=8)


def _wait_rows(src_hbm, dst, sem):
    pltpu.make_async_copy(src_hbm.at[pl.ds(0, dst.shape[0])], dst, sem).wait()


def _late_zero(x):
    u = lax.bitcast_convert_type(x, jnp.uint32)
    return ((u >> 16) >> 16).astype(jnp.int32)[0, 0]


def _moe_kernel(be_ref, ne_ref, nu_ref, tok_ref, h_hbm, wg_hbm, wu_hbm, wd_hbm, y_ref,
                xg0, xg1, xg2, sg, su, sd, wgb, wub, wdb, sem, wsem):
    b = pl.program_id(0)
    n_used = nu_ref[0]
    last = n_used - 1
    active = b < n_used
    bufs = (xg0, xg1, xg2)
    n_buf = len(bufs)

    def weight_copies(e):
        return (pltpu.make_async_copy(wg_hbm.at[e], sg, wsem.at[0]),
                pltpu.make_async_copy(wu_hbm.at[e], su, wsem.at[1]),
                pltpu.make_async_copy(wd_hbm.at[e], sd, wsem.at[2]))

    @pl.when(b == 0)
    def _():
        for cp in weight_copies(be_ref[0]):
            cp.start(priority=1)
        for i in range(n_buf - 1):
            _gather_rows(tok_ref, jnp.minimum(i, last) * EXPERT_ROWS, EXPERT_ROWS, h_hbm, bufs[i],
                         sem.at[i], unrolled=False)

    new_expert = (b == 0) | (be_ref[b] != be_ref[jnp.maximum(b - 1, 0)])

    @pl.when(active & new_expert)
    def _():
        for cp in weight_copies(be_ref[b]):
            cp.wait()
        wgb[...] = sg[...].astype(BF16)
        wub[...] = su[...].astype(BF16)
        wdb[...] = sd[...].astype(BF16)

        @pl.when(ne_ref[b] >= 0)
        def _():
            for cp in weight_copies(ne_ref[b]):
                cp.start(priority=1)

    def step(slot):
        cur = bufs[slot]
        ahead = bufs[(slot + n_buf - 1) % n_buf]
        sem_ahead = sem.at[(slot + n_buf - 1) % n_buf]
        _wait_rows(h_hbm, cur, sem.at[slot])
        base = jnp.minimum(b + n_buf - 1, last) * EXPERT_ROWS
        per = EXPERT_ROWS // GATHER_GROUPS
        group = iter(range(GATHER_GROUPS))

        def issue(zero):
            g = next(group)
            _gather_rows(tok_ref, base + g * per + zero, per, h_hbm, ahead, sem_ahead, unrolled=True,
                         both_queues=True, dst_row0=g * per)

        def corner(x, r, c):
            return _late_zero(x[r:r + 1, c:c + 1])

        bm, de = EXPERT_ROWS, D_EXPERT
        lo, hi = _unpack_bf16_pairs(cur[...])
        lo = lo.astype(BF16)
        hi = hi.astype(BF16)
        issue(0)
        a1 = _dot(lo, wgb[0:PACKED, :])
        issue(corner(a1, bm - 1, de - 1))
        a = a1 + _dot(hi, wgb[PACKED:D_MODEL, :])
        issue(corner(a, bm // 2 - 1, de // 2 - 1))
        issue(corner(a, bm - 1, de - 1))
        u1 = _dot(lo, wub[0:PACKED, :])
        issue(corner(u1, bm - 1, de - 1))
        u = u1 + _dot(hi, wub[PACKED:D_MODEL, :])
        issue(corner(u, bm // 2 - 1, de // 2 - 1))
        issue(corner(u, bm - 1, de - 1))
        act = (a / (1.0 + jnp.exp(-a)) * u).astype(BF16)
        y = _dot(act, wdb[...])
        n_left = GATHER_GROUPS - 7
        for i in range(n_left):
            issue(corner(y, bm - 1, (i + 1) * D_MODEL // n_left - 1))
        y_ref[...] = _pack_bf16_pairs(y)

        @pl.when(b == last)
        def _():
            for i in range(1, n_buf):
                _wait_rows(h_hbm, bufs[(slot + i) % n_buf], sem.at[(slot + i) % n_buf])

    for slot in range(n_buf):
        @pl.when(active & (b % n_buf == slot))
        def _(slot=slot):
            step(slot)

    @pl.when(b >= n_used)
    def _():
        y_ref[...] = jnp.zeros_like(y_ref)


def _experts(block_expert, next_expert, n_used, buf_tok, h2_packed, w_gate, w_up, w_down):
    cap = buf_tok.shape[0]
    bm = EXPERT_ROWS
    return pl.pallas_call(
        _moe_kernel,
        grid_spec=pltpu.PrefetchScalarGridSpec(
            num_scalar_prefetch=4,
            grid=(cap // bm,),
            in_specs=[pl.BlockSpec(memory_space=pl.ANY)] * 4,
            out_specs=pl.BlockSpec((bm, PACKED), lambda b, *_: (b, 0)),
            scratch_shapes=[
                pltpu.VMEM((bm, PACKED), jnp.uint32), pltpu.VMEM((bm, PACKED), jnp.uint32),
                pltpu.VMEM((bm, PACKED), jnp.uint32),
                pltpu.VMEM((D_MODEL, D_EXPERT), F32), pltpu.VMEM((D_MODEL, D_EXPERT), F32),
                pltpu.VMEM((D_EXPERT, D_MODEL), F32),
                pltpu.VMEM((D_MODEL, D_EXPERT), BF16), pltpu.VMEM((D_MODEL, D_EXPERT), BF16),
                pltpu.VMEM((D_EXPERT, D_MODEL), BF16),
                pltpu.SemaphoreType.DMA((3,)), pltpu.SemaphoreType.DMA((3,)),
            ],
        ),
        out_shape=jax.ShapeDtypeStruct((cap, PACKED), jnp.uint32),
        compiler_params=_params(1),
        name="experts",
    )(block_expert, next_expert, n_used, buf_tok, h2_packed, w_gate, w_up, w_down)


def _combine_kernel(slot_ref, rows_ref, y_hbm, x1_ref, wts_ref, mod_ref, g_ref, op_ref, os_ref, yg0, yg1, sem,
                    *, n_ctx_tiles, n_tiles):
    del rows_ref
    tm = TOKEN_TILE
    i = pl.program_id(0)
    n_slots = slot_ref.shape[0] // TOP_K

    def start(tile, buf, s, unrolled):
        for k in range(TOP_K):
            _gather_rows(slot_ref, k * n_slots + tile * tm, tm, y_hbm, buf.at[k], s, unrolled=unrolled,
                         both_queues=True)

    @pl.when(i == 0)
    def _():
        start(0, yg0, sem.at[0], False)

    def step(cur, nxt, sem_cur, sem_nxt):
        for k in range(TOP_K):
            _wait_rows(y_hbm, cur.at[k], sem_cur)
        start(jnp.minimum(i + 1, n_tiles - 1), nxt, sem_nxt, True)
        w = wts_ref[...]
        lo0, hi0 = _unpack_bf16_pairs(cur[0])
        lo1, hi1 = _unpack_bf16_pairs(cur[1])
        ffn = jnp.concatenate([lo0 * w[:, 0:1] + lo1 * w[:, 1:2], hi0 * w[:, 0:1] + hi1 * w[:, 1:2]], axis=-1)
        x2 = x1_ref[...] + mod_ref[0][5:6] * ffn
        out = _rms(x2) * g_ref[...]

        @pl.when(i < n_ctx_tiles)
        def _():
            op_ref[...] = out

        @pl.when(i >= n_ctx_tiles)
        def _():
            os_ref[...] = out

        @pl.when(i == n_tiles - 1)
        def _():
            for k in range(TOP_K):
                _wait_rows(y_hbm, nxt.at[k], sem_nxt)

    @pl.when(i % 2 == 0)
    def _():
        step(yg0, yg1, sem.at[0], sem.at[1])

    @pl.when(i % 2 == 1)
    def _():
        step(yg1, yg0, sem.at[1], sem.at[0])


def _combine(slots, rows, y_buf, x1, wts, mod3, g_final, t_ctx):
    t = x1.shape[0]
    tm = TOKEN_TILE
    n_tiles = t // tm
    n_ctx = t_ctx // tm
    return pl.pallas_call(
        functools.partial(_combine_kernel, n_ctx_tiles=n_ctx, n_tiles=n_tiles),
        grid_spec=pltpu.PrefetchScalarGridSpec(
            num_scalar_prefetch=2,
            grid=(n_tiles,),
            in_specs=[
                pl.BlockSpec(memory_space=pl.ANY),
                pl.BlockSpec((tm, D_MODEL), lambda i, s, r: (i, 0)),
                pl.BlockSpec((tm, ROUTER_LANES), lambda i, s, r: (i, 0)),
                pl.BlockSpec((1, 6, D_MODEL), lambda i, s, r: (r[i], 0, 0)),
                pl.BlockSpec((1, D_MODEL), lambda i, s, r: (0, 0)),
            ],
            out_specs=(
                pl.BlockSpec((tm, D_MODEL), lambda i, s, r: (jnp.minimum(i, n_ctx - 1), 0)),
                pl.BlockSpec((tm, D_MODEL), lambda i, s, r: (jnp.maximum(i - n_ctx, 0), 0)),
            ),
            scratch_shapes=[pltpu.VMEM((TOP_K, tm, PACKED), jnp.uint32),
                            pltpu.VMEM((TOP_K, tm, PACKED), jnp.uint32),
                            pltpu.SemaphoreType.DMA((2,))],
        ),
        out_shape=(jax.ShapeDtypeStruct((t_ctx, D_MODEL), F32),
                   jax.ShapeDtypeStruct((t - t_ctx, D_MODEL), F32)),
        compiler_params=_params(1),
        name="combine",
    )(slots, rows, y_buf, x1, wts, mod3, g_final)


def _rope_tables(n_tokens):
    rows = n_tokens // GRID_W
    row = jnp.repeat(jnp.arange(rows, dtype=F32), GRID_W)
    col = jnp.tile(jnp.arange(GRID_W, dtype=F32), rows)
    half = ROPE_AXIS_DIM // 2
    inv_freq = ROPE_BASE ** (-jnp.arange(half, dtype=F32) / half)
    ar = row[:, None] * inv_freq
    ac = col[:, None] * inv_freq
    cos = jnp.concatenate([jnp.cos(ar), jnp.cos(ar), jnp.cos(ac), jnp.cos(ac)], axis=-1)
    sin = jnp.concatenate([-jnp.sin(ar), jnp.sin(ar), -jnp.sin(ac), jnp.sin(ac)], axis=-1)
    return cos, sin


def _dispatch_tables(ids):
    t = ids.shape[0]
    bm = EXPERT_ROWS
    flat_e = ids.reshape(t * TOP_K)
    onehot = (flat_e[:, None] == jnp.arange(N_EXPERTS, dtype=jnp.int32)[None, :]).astype(jnp.int32)
    running = jnp.cumsum(onehot, axis=0)
    rank = jnp.sum(running * onehot, axis=1) - 1
    counts = running[-1]
    padded = (counts + bm - 1) // bm * bm
    pad_end = jnp.cumsum(padded)
    pad_start = pad_end - padded
    dest = (jnp.sum(pad_start[None, :] * onehot, axis=1) + rank).astype(jnp.int32)
    n_blocks = (t * TOP_K + N_EXPERTS * (bm - 1) + bm - 1) // bm
    flat_tok = jnp.repeat(jnp.arange(t, dtype=jnp.int32), TOP_K)
    buf_tok = jnp.zeros((n_blocks * bm,), jnp.int32).at[dest].set(flat_tok)
    n_used = (pad_end[-1] // bm).astype(jnp.int32)
    blk = jnp.arange(n_blocks, dtype=jnp.int32)
    blk = jnp.minimum(blk, n_used - 1)
    block_expert = jnp.sum((pad_end[None, :] <= (blk * bm)[:, None]).astype(jnp.int32), axis=1)
    block_expert = jnp.minimum(block_expert, N_EXPERTS - 1)
    e_ids = jnp.arange(N_EXPERTS, dtype=jnp.int32)
    later = (e_ids[None, :] > e_ids[:, None]) & (counts[None, :] > 0)
    next_of = jnp.min(jnp.where(later, e_ids[None, :], N_EXPERTS), axis=1)
    next_of = jnp.where(next_of == N_EXPERTS, -1, next_of)
    next_expert = jnp.sum(next_of[None, :] * (block_expert[:, None] == e_ids[None, :]), axis=1)
    slots = dest.reshape(t, TOP_K).T.reshape(TOP_K * t)
    return (block_expert.astype(jnp.int32), next_expert.astype(jnp.int32), n_used.reshape(1), buf_tok, slots)


def kernel(x_prompt, x_sample, cache_k, cache_v, c, c_ctx, w_mod, b_mod, norm_mix_g, norm_ffn_g, w_in,
           pool_w, pool_scale, attn_sink, out_norm_pool_g, out_norm_attn_g, w_out, router_group_w,
           router_group_b, router_expert_w, router_expert_b, w_gate, w_up, w_down, final_norm_g):
    depth = w_mod.shape[0]
    assert depth == 1, "single trunk layer"
    bp, lp, _ = x_prompt.shape
    bs, ls, _ = x_sample.shape
    tp, ts = bp * lp, bs * ls
    tm = TOKEN_TILE
    l = 0

    cond8 = jnp.zeros((8, D_MODEL), F32).at[:bs].set(c).at[bs].set(c_ctx)
    mod3 = _modulation(cond8, w_mod[l], b_mod[l]).reshape(8, 6, D_MODEL)
    rows_p = jnp.full((tp // tm,), bs, jnp.int32)
    rows_s = jnp.arange(ts // tm, dtype=jnp.int32) // (ls // tm)

    w_in_b = w_in[l].astype(BF16)
    w_out_b = w_out[l].astype(BF16)
    pool_w_b = pool_w[l].astype(BF16)
    g_mix = norm_mix_g[l].reshape(1, D_MODEL)
    g_ffn = norm_ffn_g[l].reshape(1, D_MODEL)
    g_pool = out_norm_pool_g[l].reshape(1, POOL_WIDTH)
    g_attn = out_norm_attn_g[l].reshape(1, ATTN_WIDTH)
    p_scale = pool_scale[l].reshape(1, POOL_WIDTH)
    sink_b = jnp.broadcast_to(attn_sink[l][:, None], (N_HEADS, LANES))
    g_final = final_norm_g.reshape(1, D_MODEL)

    rw = jnp.concatenate(
        [router_group_w[l], jnp.transpose(router_expert_w[l], (1, 0, 2)).reshape(D_MODEL, N_EXPERTS)], axis=1)
    rw = jnp.pad(rw, ((0, 0), (0, ROUTER_LANES - rw.shape[1])))
    rw_hi = rw.astype(BF16)
    rw_lo = (rw - rw_hi.astype(F32)).astype(BF16)
    rb = jnp.concatenate([router_group_b[l], router_expert_b[l].reshape(N_EXPERTS)])
    rb = jnp.pad(rb, (0, ROUTER_LANES - rb.shape[0])).reshape(1, ROUTER_LANES)

    xp = x_prompt.reshape(tp, D_MODEL)
    xs = x_sample.reshape(ts, D_MODEL)

    up_p, q_p, k_p, v_p = _in_projection(xp, rows_p, mod3, g_mix, w_in_b, None, F32)
    pool_p = _pool_mixer(up_p.reshape(bp, lp, POOL_WIDTH), pool_w_b, p_scale, g_pool).reshape(tp, POOL_WIDTH)
    attn_p = _context_attention(q_p, k_p, v_p, sink_b, g_attn, lp)
    up_s, q_s, k_s, v_s = _in_projection(xs, rows_s, mod3, g_mix, w_in_b, _rope_tables(ls), BF16)
    pool_s = _pool_mixer(up_s.reshape(bs, ls, POOL_WIDTH), pool_w_b, p_scale, g_pool).reshape(ts, POOL_WIDTH)
    attn_s = _latent_attention(
        q_s, k_s.reshape(bs, ls, KV_WIDTH), v_s.reshape(bs, ls, KV_WIDTH),
        cache_k[:, l].reshape(bs, -1, KV_WIDTH), cache_v[:, l].reshape(bs, -1, KV_WIDTH), sink_b, g_attn)

    rows_all = jnp.concatenate([rows_p, rows_s])
    x1_all, h2_all, ids_all, wts_all = _out_projection(
        (pool_p, attn_p, xp), (pool_s, attn_s, xs), rows_all, mod3, g_ffn, w_out_b, rw_hi, rw_lo, rb)

    block_expert, next_expert, n_used, buf_tok, slots = _dispatch_tables(ids_all[:, :TOP_K])
    y_buf = _experts(block_expert, next_expert, n_used, buf_tok, h2_all, w_gate[l], w_up[l], w_down[l])

    y_p, y_s = _combine(slots, rows_all, y_buf, x1_all, wts_all, mod3, g_final, tp)

    kv_shape = (bp, 1, lp, N_KV_HEADS, HEAD_DIM)
    return (y_p.reshape(bp, lp, D_MODEL), y_s.reshape(bs, ls, D_MODEL),
            k_p.reshape(kv_shape), v_p.reshape(kv_shape))
```

```python
import functools

import jax
import jax.numpy as jnp
from jax import lax
from jax.experimental import pallas as pl
from jax.experimental.pallas import tpu as pltpu

F32 = jnp.float32
BF16 = jnp.bfloat16

D_MODEL = 2048
GRID_W = 64
HEAD_DIM = 128
ATTN_WIDTH = 1024
POOL_WIDTH = 1024
N_HEADS = 8
N_KV_HEADS = 2
Q_PER_KV = 4
KV_WIDTH = 256
IN_WIDTH = 2560
WINDOW = 128
POOL_WINDOWS = (2, 4, 8, 16)
POOL_GROUP_DIM = 256
ROPE_BASE = 10000.0
ROPE_AXIS_DIM = 64
N_EXPERT_GROUPS = 4
EXPERTS_PER_GROUP = 8
N_EXPERTS = 32
TOP_K = 2
D_EXPERT = 512
NORM_EPS = 1e-6
NEG_INF = -1e30
ATTN_SCALE = HEAD_DIM ** -0.5
LOG2E = 1.4426950408889634
Q_SCALE = ATTN_SCALE * LOG2E

LANES = 128
Q_BLOCKS_PER_STEP = 4
CTX_SEQS_PER_STEP = 2
VMEM_LIMIT = 48 * 1024 * 1024
PACKED = D_MODEL // 2
ROW_TILE = 8
assert PACKED == ROW_TILE * LANES, "a packed row must fill exactly one tile"
TOKEN_TILE = 256
IN_TILE = 512
POOL_HALO = 16
Q_BLOCK = 128
EXPERT_ROWS = 256
GATHER_GROUPS = 16
ROUTER_LANES = 128


def _params(n_grid):
    return pltpu.CompilerParams(
        dimension_semantics=("arbitrary",) * n_grid, vmem_limit_bytes=VMEM_LIMIT)


def _rms(x):
    return x * lax.rsqrt(jnp.mean(x * x, axis=-1, keepdims=True) + NORM_EPS)


def _dot(a, b):
    return jnp.dot(a, b, preferred_element_type=F32)


def _dot_nt(a, b):
    return lax.dot_general(a, b, (((1,), (1,)), ((), ())), preferred_element_type=F32)


def _pack_bf16_pairs(x):
    c = x.shape[1] // 2
    lo = lax.bitcast_convert_type(x[:, :c].astype(BF16).astype(F32), jnp.uint32)
    hi = lax.bitcast_convert_type(x[:, c:].astype(BF16).astype(F32), jnp.uint32)
    return hi | (lo >> 16)


def _store_row_tiles(ref, x):
    n = x.shape[0]
    for c in range(PACKED // LANES):
        ref[pl.ds(c, n, stride=ROW_TILE), :] = x[:, c * LANES:(c + 1) * LANES]


def _load_row_tiles(ref, n):
    return jnp.concatenate(
        [ref[pl.ds(c, n, stride=ROW_TILE), :] for c in range(PACKED // LANES)], axis=-1)


def _unpack_bf16_pairs(u):
    lo = lax.bitcast_convert_type(u << 16, F32)
    hi = lax.bitcast_convert_type(u & jnp.uint32(0xFFFF0000), F32)
    return lo, hi


def _mod_kernel(cond_ref, w_ref, b_ref, o_ref):
    c = cond_ref[...]
    s = c / (1.0 + jnp.exp(-c))
    o_ref[...] = _dot(s.astype(BF16), w_ref[...].astype(BF16)) + b_ref[...]


def _modulation(cond8, w_mod, b_mod):
    n = w_mod.shape[1]
    tn = 1024
    return pl.pallas_call(
        _mod_kernel,
        grid=(n // tn,),
        in_specs=[
            pl.BlockSpec((8, D_MODEL), lambda j: (0, 0)),
            pl.BlockSpec((D_MODEL, tn), lambda j: (0, j)),
            pl.BlockSpec((1, tn), lambda j: (0, j)),
        ],
        out_specs=pl.BlockSpec((8, tn), lambda j: (0, j)),
        out_shape=jax.ShapeDtypeStruct((8, n), F32),
        compiler_params=_params(1),
        name="modulation",
    )(cond8, w_mod, b_mod.reshape(1, n))


def _rope(x, cos, sin_signed):
    lane = lax.broadcasted_iota(jnp.int32, x.shape, 1)
    partner = jnp.where((lane & 63) < 32,
                        pltpu.roll(x, HEAD_DIM - 32, axis=1),
                        pltpu.roll(x, 32, axis=1))
    return x * cos + partner * sin_signed


def _inproj_kernel(rows_ref, x_ref, mod_ref, g_ref, w_ref, *rest, rope):
    del rows_ref
    if rope:
        cos_ref, sin_ref, up_ref, q_ref, k_ref, v_ref = rest
    else:
        up_ref, q_ref, k_ref, v_ref = rest
    m = mod_ref[0]
    h = _rms(x_ref[...]) * g_ref[...]
    hb = (h * (1.0 + m[1:2]) + m[0:1]).astype(BF16)
    up_ref[...] = _dot(hb, w_ref[:, 0:POOL_WIDTH]).astype(BF16)
    q = _dot(hb, w_ref[:, POOL_WIDTH:POOL_WIDTH + ATTN_WIDTH])
    k = _dot(hb, w_ref[:, POOL_WIDTH + ATTN_WIDTH:IN_WIDTH - KV_WIDTH])
    v = _dot(hb, w_ref[:, IN_WIDTH - KV_WIDTH:IN_WIDTH])
    if rope:
        cos = cos_ref[...]
        sin = sin_ref[...]
        for hd in range(N_HEADS):
            sl = slice(hd * HEAD_DIM, (hd + 1) * HEAD_DIM)
            q_ref[:, sl] = (_rope(q[:, sl], cos, sin) * Q_SCALE).astype(BF16)
        for hd in range(N_KV_HEADS):
            sl = slice(hd * HEAD_DIM, (hd + 1) * HEAD_DIM)
            k_ref[:, sl] = _rope(k[:, sl], cos, sin).astype(k_ref.dtype)
    else:
        q_ref[...] = (q * Q_SCALE).astype(BF16)
        k_ref[...] = k.astype(k_ref.dtype)
    v_ref[...] = v.astype(v_ref.dtype)


def _in_projection(x, rows, mod3, g_mix, w_in_b, rope_tabs, kv_dtype):
    t = x.shape[0]
    tm = IN_TILE
    per = IN_TILE // TOKEN_TILE
    rope = rope_tabs is not None
    in_specs = [
        pl.BlockSpec((tm, D_MODEL), lambda i, r: (i, 0)),
        pl.BlockSpec((1, 6, D_MODEL), lambda i, r: (r[i * per], 0, 0)),
        pl.BlockSpec((1, D_MODEL), lambda i, r: (0, 0)),
        pl.BlockSpec((D_MODEL, IN_WIDTH), lambda i, r: (0, 0), pipeline_mode=pl.Buffered(1)),
    ]
    args = [x, mod3, g_mix, w_in_b]
    if rope:
        seq_tiles = rope_tabs[0].shape[0] // tm
        for tab in rope_tabs:
            in_specs.append(pl.BlockSpec((tm, HEAD_DIM), lambda i, r: (i % seq_tiles, 0)))
            args.append(tab)
    out_shape = (
        jax.ShapeDtypeStruct((t, POOL_WIDTH), BF16),
        jax.ShapeDtypeStruct((t, ATTN_WIDTH), BF16),
        jax.ShapeDtypeStruct((t, KV_WIDTH), kv_dtype),
        jax.ShapeDtypeStruct((t, KV_WIDTH), kv_dtype),
    )
    out_specs = (
        pl.BlockSpec((tm, POOL_WIDTH), lambda i, r: (i, 0)),
        pl.BlockSpec((tm, ATTN_WIDTH), lambda i, r: (i, 0)),
        pl.BlockSpec((tm, KV_WIDTH), lambda i, r: (i, 0)),
        pl.BlockSpec((tm, KV_WIDTH), lambda i, r: (i, 0)),
    )
    return pl.pallas_call(
        functools.partial(_inproj_kernel, rope=rope),
        grid_spec=pltpu.PrefetchScalarGridSpec(
            num_scalar_prefetch=1, grid=(t // tm,), in_specs=in_specs, out_specs=out_specs),
        out_shape=out_shape,
        compiler_params=_params(1),
        name="in_projection_rope" if rope else "in_projection",
    )(rows, *args)


def _pool_kernel(u_ref, pw_ref, ps_ref, g_ref, o_ref, *, seq_len):
    tq = TOKEN_TILE
    win = min(seq_len, tq + 2 * POOL_HALO)
    t0 = pl.program_id(1) * tq
    src0 = pl.multiple_of(jnp.clip(t0 - POOL_HALO, 0, seq_len - win), POOL_HALO)
    u = u_ref[0, pl.ds(src0, win), :]
    t = t0 + lax.broadcasted_iota(jnp.int32, (tq, win), 0)
    j = src0 + lax.broadcasted_iota(jnp.int32, (tq, win), 1)
    tc = t0 + lax.broadcasted_iota(jnp.int32, (tq, 1), 0)
    ys = []
    for gi, w in enumerate(POOL_WINDOWS):
        lo = jnp.maximum(t - w // 2, 0)
        hi = jnp.minimum(t + (w - w // 2), seq_len)
        cnt = (hi - lo).astype(F32)
        a = jnp.where((j >= lo) & (j < hi), 1.0, 0.0) - jnp.where(j == t, cnt, 0.0)
        cnt_col = (jnp.minimum(tc + (w - w // 2), seq_len) - jnp.maximum(tc - w // 2, 0)).astype(F32)
        sl = slice(gi * POOL_GROUP_DIM, (gi + 1) * POOL_GROUP_DIM)
        d = _dot(a.astype(BF16), u[:, sl]) / cnt_col
        ys.append(_dot(d.astype(BF16), pw_ref[gi]))
    y = jnp.concatenate(ys, axis=-1) * ps_ref[...]
    o_ref[0] = (_rms(y) * g_ref[...]).astype(BF16)


def _pool_mixer(up, pool_w_b, pool_scale, g_out_pool):
    b, seq_len, _ = up.shape
    tq = TOKEN_TILE
    return pl.pallas_call(
        functools.partial(_pool_kernel, seq_len=seq_len),
        grid=(b, seq_len // tq),
        in_specs=[
            pl.BlockSpec((1, seq_len, POOL_WIDTH), lambda bi, i: (bi, 0, 0)),
            pl.BlockSpec((len(POOL_WINDOWS), POOL_GROUP_DIM, POOL_GROUP_DIM), lambda bi, i: (0, 0, 0)),
            pl.BlockSpec((1, POOL_WIDTH), lambda bi, i: (0, 0)),
            pl.BlockSpec((1, POOL_WIDTH), lambda bi, i: (0, 0)),
        ],
        out_specs=pl.BlockSpec((1, tq, POOL_WIDTH), lambda bi, i: (bi, i, 0)),
        out_shape=jax.ShapeDtypeStruct((b, seq_len, POOL_WIDTH), BF16),
        compiler_params=_params(2),
        name=f"pool_mixer_{seq_len}",
    )(up, pool_w_b, pool_scale, g_out_pool)


def _stack_heads(q, kh, rows):
    return jnp.concatenate(
        [q[:, (kh * Q_PER_KV + g) * HEAD_DIM:(kh * Q_PER_KV + g + 1) * HEAD_DIM]
         for g in range(Q_PER_KV)], axis=0)


def _sink_column(sink_ref, kh, rows):
    return jnp.concatenate(
        [jnp.broadcast_to(sink_ref[kh * Q_PER_KV + g:kh * Q_PER_KV + g + 1, 0:1] * LOG2E, (rows, 1))
         for g in range(Q_PER_KV)], axis=0)


def _attend(q4, key_sets, value_sets, sk, band=None):
    s = [_dot_nt(q4, k) for k in key_sets]
    if band is not None:
        s[0] = jnp.where(band, s[0], NEG_INF)
    m = sk
    for si in s:
        m = jnp.maximum(m, jnp.max(si, axis=-1, keepdims=True))
    acc = None
    for si, v in zip(s, value_sets):
        v1 = jnp.concatenate([v, jnp.ones_like(v)], axis=-1)
        part = _dot(jnp.exp2(si - m).astype(BF16), v1)
        acc = part if acc is None else acc + part
    den = acc[:, HEAD_DIM:] + jnp.exp2(sk - m)
    return acc[:, :HEAD_DIM] / den


def _ctx_attn_kernel(q_ref, k_ref, v_ref, sink_ref, g_ref, o_ref, *, seq_len):
    rows = seq_len
    for j in range(q_ref.shape[0] // rows):
        rs = slice(j * rows, (j + 1) * rows)
        q = q_ref[rs, :]
        heads = [None] * N_HEADS
        for kh in range(N_KV_HEADS):
            sl = slice(kh * HEAD_DIM, (kh + 1) * HEAD_DIM)
            o = _attend(_stack_heads(q, kh, rows), [k_ref[rs, sl].astype(BF16)],
                        [v_ref[rs, sl].astype(BF16)], _sink_column(sink_ref, kh, rows))
            for g in range(Q_PER_KV):
                heads[kh * Q_PER_KV + g] = o[g * rows:(g + 1) * rows]
        y = jnp.concatenate(heads, axis=-1)
        o_ref[rs, :] = (_rms(y) * g_ref[...]).astype(BF16)


def _context_attention(q, k, v, sink_b, g_out_attn, seq_len):
    t = q.shape[0]
    blk = seq_len * CTX_SEQS_PER_STEP
    return pl.pallas_call(
        functools.partial(_ctx_attn_kernel, seq_len=seq_len),
        grid=(t // blk,),
        in_specs=[
            pl.BlockSpec((blk, ATTN_WIDTH), lambda b: (b, 0)),
            pl.BlockSpec((blk, KV_WIDTH), lambda b: (b, 0)),
            pl.BlockSpec((blk, KV_WIDTH), lambda b: (b, 0)),
            pl.BlockSpec((N_HEADS, LANES), lambda b: (0, 0)),
            pl.BlockSpec((1, ATTN_WIDTH), lambda b: (0, 0)),
        ],
        out_specs=pl.BlockSpec((blk, ATTN_WIDTH), lambda b: (b, 0)),
        out_shape=jax.ShapeDtypeStruct((t, ATTN_WIDTH), BF16),
        compiler_params=_params(1),
        name="context_attention",
    )(q, k, v, sink_b, g_out_attn)


def _lat_attn_kernel(q_ref, k_ref, v_ref, ck_ref, cv_ref, sink_ref, g_ref, o_ref, *, seq_len):
    rows = Q_BLOCK
    span = 3 * Q_BLOCK
    shape = (Q_PER_KV * rows, span)
    row_in_block = lax.broadcasted_iota(jnp.int32, shape, 0) & (rows - 1)
    col = lax.broadcasted_iota(jnp.int32, shape, 1)
    for j in range(Q_BLOCKS_PER_STEP):
        q0 = (pl.program_id(1) * Q_BLOCKS_PER_STEP + j) * rows
        start = pl.multiple_of(jnp.clip(q0 - Q_BLOCK, 0, seq_len - span), Q_BLOCK)
        q = q_ref[j * rows:(j + 1) * rows, :]
        kl = k_ref[0, pl.ds(start, span), :]
        vl = v_ref[0, pl.ds(start, span), :]
        band = jnp.abs(row_in_block - col + (q0 - start)) <= WINDOW
        heads = [None] * N_HEADS
        for kh in range(N_KV_HEADS):
            sl = slice(kh * HEAD_DIM, (kh + 1) * HEAD_DIM)
            o = _attend(_stack_heads(q, kh, rows),
                        [kl[:, sl], ck_ref[0, :, sl].astype(BF16)],
                        [vl[:, sl], cv_ref[0, :, sl].astype(BF16)],
                        _sink_column(sink_ref, kh, rows), band)
            for g in range(Q_PER_KV):
                heads[kh * Q_PER_KV + g] = o[g * rows:(g + 1) * rows]
        y = jnp.concatenate(heads, axis=-1)
        o_ref[j * rows:(j + 1) * rows, :] = (_rms(y) * g_ref[...]).astype(BF16)


def _latent_attention(q, k, v, cache_k, cache_v, sink_b, g_out_attn):
    b, seq_len, _ = k.shape
    q_rows = Q_BLOCK * Q_BLOCKS_PER_STEP
    nq = seq_len // q_rows
    past = cache_k.shape[1]
    return pl.pallas_call(
        functools.partial(_lat_attn_kernel, seq_len=seq_len),
        grid=(b, nq),
        in_specs=[
            pl.BlockSpec((q_rows, ATTN_WIDTH), lambda bi, n: (bi * nq + n, 0)),
            pl.BlockSpec((1, seq_len, KV_WIDTH), lambda bi, n: (bi, 0, 0)),
            pl.BlockSpec((1, seq_len, KV_WIDTH), lambda bi, n: (bi, 0, 0)),
            pl.BlockSpec((1, past, KV_WIDTH), lambda bi, n: (bi, 0, 0)),
            pl.BlockSpec((1, past, KV_WIDTH), lambda bi, n: (bi, 0, 0)),
            pl.BlockSpec((N_HEADS, LANES), lambda bi, n: (0, 0)),
            pl.BlockSpec((1, ATTN_WIDTH), lambda bi, n: (0, 0)),
        ],
        out_specs=pl.BlockSpec((q_rows, ATTN_WIDTH), lambda bi, n: (bi * nq + n, 0)),
        out_shape=jax.ShapeDtypeStruct((b * seq_len, ATTN_WIDTH), BF16),
        compiler_params=_params(2),
        name="latent_attention",
    )(q, k, v, cache_k, cache_v, sink_b, g_out_attn)


def _route(logits):
    lane = lax.broadcasted_iota(jnp.int32, logits.shape, 1).astype(F32)
    neg = -jnp.inf

    def first_argmax(x):
        mx = jnp.max(x, axis=-1, keepdims=True)
        return mx, jnp.min(jnp.where(x == mx, lane, float(ROUTER_LANES)), axis=-1, keepdims=True)

    gl = jnp.where(lane < N_EXPERT_GROUPS, logits, neg)
    gmax, g_idx = first_argmax(gl)
    p_g = 1.0 / jnp.sum(jnp.exp(gl - gmax), axis=-1, keepdims=True)
    base = N_EXPERT_GROUPS + EXPERTS_PER_GROUP * g_idx
    el = jnp.where((lane >= base) & (lane < base + EXPERTS_PER_GROUP), logits, neg)
    v1, i1 = first_argmax(el)
    v2, i2 = first_argmax(jnp.where(lane == i1, neg, el))
    e2 = jnp.exp(v2 - v1)
    w1 = p_g / (1.0 + e2)
    w2 = p_g * e2 / (1.0 + e2)
    ids = jnp.where(lane == 0.0, i1 - N_EXPERT_GROUPS, jnp.where(lane == 1.0, i2 - N_EXPERT_GROUPS, 0.0))
    wts = jnp.where(lane == 0.0, w1, jnp.where(lane == 1.0, w2, 0.0))
    return ids.astype(jnp.int32), wts


def _outproj_kernel(rows_ref, pool_p, attn_p, x_p, pool_s, attn_s, x_s, mod_ref, g_ref, w_ref, rwh_ref,
                    rwl_ref, rb_ref, x1_ref, h2_ref, ids_ref, wts_ref, *, n_ctx_tiles):
    del rows_ref
    i = pl.program_id(0)

    def body(pool_ref, attn_ref, x_ref):
        m = mod_ref[0]
        mix = (_dot(pool_ref[...], w_ref[0:POOL_WIDTH, :])
               + _dot(attn_ref[...], w_ref[POOL_WIDTH:POOL_WIDTH + ATTN_WIDTH, :]))
        x1 = x_ref[...] + m[2:3] * mix
        x1_ref[...] = x1
        h2 = _rms(x1) * g_ref[...] * (1.0 + m[4:5]) + m[3:4]
        _store_row_tiles(h2_ref, _pack_bf16_pairs(h2))
        hi = h2.astype(BF16)
        lo = (h2 - hi.astype(F32)).astype(BF16)
        logits = _dot(hi, rwh_ref[...]) + _dot(lo, rwh_ref[...]) + _dot(hi, rwl_ref[...]) + rb_ref[...]
        ids, wts = _route(logits)
        ids_ref[...] = ids
        wts_ref[...] = wts

    @pl.when(i < n_ctx_tiles)
    def _():
        body(pool_p, attn_p, x_p)

    @pl.when(i >= n_ctx_tiles)
    def _():
        body(pool_s, attn_s, x_s)


def _out_projection(ctx, lat, rows, mod3, g_ffn, w_out_b, rw_hi, rw_lo, rb):
    tm = TOKEN_TILE
    n_ctx = ctx[2].shape[0] // tm
    n_lat = lat[2].shape[0] // tm
    t_all = (n_ctx + n_lat) * tm

    def ctx_map(i, r):
        return (jnp.minimum(i, n_ctx - 1), 0)

    def lat_map(i, r):
        return (jnp.maximum(i - n_ctx, 0), 0)

    in_specs = [
        pl.BlockSpec((tm, POOL_WIDTH), ctx_map),
        pl.BlockSpec((tm, ATTN_WIDTH), ctx_map),
        pl.BlockSpec((tm, D_MODEL), ctx_map),
        pl.BlockSpec((tm, POOL_WIDTH), lat_map),
        pl.BlockSpec((tm, ATTN_WIDTH), lat_map),
        pl.BlockSpec((tm, D_MODEL), lat_map),
        pl.BlockSpec((1, 6, D_MODEL), lambda i, r: (r[i], 0, 0)),
        pl.BlockSpec((1, D_MODEL), lambda i, r: (0, 0)),
        pl.BlockSpec((D_MODEL, D_MODEL), lambda i, r: (0, 0), pipeline_mode=pl.Buffered(1)),
        pl.BlockSpec((D_MODEL, ROUTER_LANES), lambda i, r: (0, 0)),
        pl.BlockSpec((D_MODEL, ROUTER_LANES), lambda i, r: (0, 0)),
        pl.BlockSpec((1, ROUTER_LANES), lambda i, r: (0, 0)),
    ]
    out_shape = (
        jax.ShapeDtypeStruct((t_all, D_MODEL), F32),
        jax.ShapeDtypeStruct((t_all * ROW_TILE, LANES), jnp.uint32),
        jax.ShapeDtypeStruct((t_all, ROUTER_LANES), jnp.int32),
        jax.ShapeDtypeStruct((t_all, ROUTER_LANES), F32),
    )
    out_specs = (
        pl.BlockSpec((tm, D_MODEL), lambda i, r: (i, 0)),
        pl.BlockSpec((tm * ROW_TILE, LANES), lambda i, r: (i, 0)),
        pl.BlockSpec((tm, ROUTER_LANES), lambda i, r: (i, 0)),
        pl.BlockSpec((tm, ROUTER_LANES), lambda i, r: (i, 0)),
    )
    return pl.pallas_call(
        functools.partial(_outproj_kernel, n_ctx_tiles=n_ctx),
        grid_spec=pltpu.PrefetchScalarGridSpec(
            num_scalar_prefetch=1, grid=(n_ctx + n_lat,), in_specs=in_specs, out_specs=out_specs),
        out_shape=out_shape,
        compiler_params=_params(1),
        name="out_projection",
    )(rows, *ctx, *lat, mod3, g_ffn, w_out_b, rw_hi, rw_lo, rb)


def _gather_rows(idx_ref, base, n_rows, src_hbm, dst, sem, *, unrolled, both_queues=False, dst_row0=0):
    def one(r, priority):
        src_row = pl.multiple_of(idx_ref[base + r], ROW_TILE)
        pltpu.make_async_copy(
            src_hbm.at[pl.ds(src_row, ROW_TILE)], dst.at[pl.ds((dst_row0 + r) * ROW_TILE, ROW_TILE)],
            sem).start(priority=priority)

    if unrolled:
        for r in range(n_rows):
            one(r, r % 2 if both_queues else 0)
    else:
        def body(r, carry):
            one(r, 0)
            return carry
        lax.fori_loop(0, n_rows, body, 0, unroll=8)


def _wait_rows(src_hbm, dst, sem):
    pltpu.make_async_copy(src_hbm.at[pl.ds(0, dst.shape[0])], dst, sem).wait()


def _late_zero(x):
    u = lax.bitcast_convert_type(x, jnp.uint32)
    return ((u >> 16) >> 16).astype(jnp.int32)[0, 0]


def _moe_kernel(be_ref, ne_ref, nu_ref, tok_ref, h_hbm, wg_hbm, wu_hbm, wd_hbm, y_ref,
                xg0, xg1, xg2, sg, su, sd, wgb, wub, wdb, sem, wsem):
    b = pl.program_id(0)
    n_used = nu_ref[0]
    last = n_used - 1
    active = b < n_used
    bufs = (xg0, xg1, xg2)
    n_buf = len(bufs)

    def weight_copies(e):
        return (pltpu.make_async_copy(wg_hbm.at[e], sg, wsem.at[0]),
                pltpu.make_async_copy(wu_hbm.at[e], su, wsem.at[1]),
                pltpu.make_async_copy(wd_hbm.at[e], sd, wsem.at[2]))

    @pl.when(b == 0)
    def _():
        for cp in weight_copies(be_ref[0]):
            cp.start(priority=1)
        for i in range(n_buf - 1):
            _gather_rows(tok_ref, jnp.minimum(i, last) * EXPERT_ROWS, EXPERT_ROWS, h_hbm, bufs[i],
                         sem.at[i], unrolled=False)

    new_expert = (b == 0) | (be_ref[b] != be_ref[jnp.maximum(b - 1, 0)])

    @pl.when(active & new_expert)
    def _():
        for cp in weight_copies(be_ref[b]):
            cp.wait()
        wgb[...] = sg[...].astype(BF16)
        wub[...] = su[...].astype(BF16)
        wdb[...] = sd[...].astype(BF16)

        @pl.when(ne_ref[b] >= 0)
        def _():
            for cp in weight_copies(ne_ref[b]):
                cp.start(priority=1)

    def step(slot):
        cur = bufs[slot]
        ahead = bufs[(slot + n_buf - 1) % n_buf]
        sem_ahead = sem.at[(slot + n_buf - 1) % n_buf]
        _wait_rows(h_hbm, cur, sem.at[slot])
        base = jnp.minimum(b + n_buf - 1, last) * EXPERT_ROWS
        per = EXPERT_ROWS // GATHER_GROUPS
        group = iter(range(GATHER_GROUPS))

        def issue(zero):
            g = next(group)
            _gather_rows(tok_ref, base + g * per + zero, per, h_hbm, ahead, sem_ahead, unrolled=True,
                         both_queues=True, dst_row0=g * per)

        def corner(x, r, c):
            return _late_zero(x[r:r + 1, c:c + 1])

        bm, de = EXPERT_ROWS, D_EXPERT
        lo, hi = _unpack_bf16_pairs(_load_row_tiles(cur, bm))
        lo = lo.astype(BF16)
        hi = hi.astype(BF16)
        issue(0)
        a1 = _dot(lo, wgb[0:PACKED, :])
        issue(corner(a1, bm - 1, de - 1))
        a = a1 + _dot(hi, wgb[PACKED:D_MODEL, :])
        issue(corner(a, bm // 2 - 1, de // 2 - 1))
        issue(corner(a, bm - 1, de - 1))
        u1 = _dot(lo, wub[0:PACKED, :])
        issue(corner(u1, bm - 1, de - 1))
        u = u1 + _dot(hi, wub[PACKED:D_MODEL, :])
        issue(corner(u, bm // 2 - 1, de // 2 - 1))
        issue(corner(u, bm - 1, de - 1))
        act = (a / (1.0 + jnp.exp(-a)) * u).astype(BF16)
        y = _dot(act, wdb[...])
        n_left = GATHER_GROUPS - 7
        for i in range(n_left):
            issue(corner(y, bm - 1, (i + 1) * D_MODEL // n_left - 1))
        _store_row_tiles(y_ref, _pack_bf16_pairs(y))

        @pl.when(b == last)
        def _():
            for i in range(1, n_buf):
                _wait_rows(h_hbm, bufs[(slot + i) % n_buf], sem.at[(slot + i) % n_buf])

    for slot in range(n_buf):
        @pl.when(active & (b % n_buf == slot))
        def _(slot=slot):
            step(slot)

    @pl.when(b >= n_used)
    def _():
        y_ref[...] = jnp.zeros_like(y_ref)


def _experts(block_expert, next_expert, n_used, buf_tok, h2_packed, w_gate, w_up, w_down):
    cap = buf_tok.shape[0]
    bm = EXPERT_ROWS
    xg = pltpu.VMEM((bm * ROW_TILE, LANES), jnp.uint32)
    return pl.pallas_call(
        _moe_kernel,
        grid_spec=pltpu.PrefetchScalarGridSpec(
            num_scalar_prefetch=4,
            grid=(cap // bm,),
            in_specs=[pl.BlockSpec(memory_space=pl.ANY)] * 4,
            out_specs=pl.BlockSpec((bm * ROW_TILE, LANES), lambda b, *_: (b, 0)),
            scratch_shapes=[
                xg, xg, xg,
                pltpu.VMEM((D_MODEL, D_EXPERT), F32), pltpu.VMEM((D_MODEL, D_EXPERT), F32),
                pltpu.VMEM((D_EXPERT, D_MODEL), F32),
                pltpu.VMEM((D_MODEL, D_EXPERT), BF16), pltpu.VMEM((D_MODEL, D_EXPERT), BF16),
                pltpu.VMEM((D_EXPERT, D_MODEL), BF16),
                pltpu.SemaphoreType.DMA((3,)), pltpu.SemaphoreType.DMA((3,)),
            ],
        ),
        out_shape=jax.ShapeDtypeStruct((cap * ROW_TILE, LANES), jnp.uint32),
        compiler_params=_params(1),
        name="experts",
    )(block_expert, next_expert, n_used, buf_tok, h2_packed, w_gate, w_up, w_down)


def _combine_kernel(slot_ref, rows_ref, y_hbm, x1_ref, wts_ref, mod_ref, g_ref, op_ref, os_ref, yg0, yg1, sem,
                    *, n_ctx_tiles, n_tiles):
    del rows_ref
    tm = TOKEN_TILE
    i = pl.program_id(0)
    n_slots = slot_ref.shape[0] // TOP_K

    def start(tile, buf, s, unrolled):
        for k in range(TOP_K):
            _gather_rows(slot_ref, k * n_slots + tile * tm, tm, y_hbm, buf.at[k], s, unrolled=unrolled,
                         both_queues=True)

    @pl.when(i == 0)
    def _():
        start(0, yg0, sem.at[0], False)

    def step(cur, nxt, sem_cur, sem_nxt):
        for k in range(TOP_K):
            _wait_rows(y_hbm, cur.at[k], sem_cur)
        start(jnp.minimum(i + 1, n_tiles - 1), nxt, sem_nxt, True)
        w = wts_ref[...]
        lo0, hi0 = _unpack_bf16_pairs(_load_row_tiles(cur.at[0], tm))
        lo1, hi1 = _unpack_bf16_pairs(_load_row_tiles(cur.at[1], tm))
        ffn = jnp.concatenate([lo0 * w[:, 0:1] + lo1 * w[:, 1:2], hi0 * w[:, 0:1] + hi1 * w[:, 1:2]], axis=-1)
        x2 = x1_ref[...] + mod_ref[0][5:6] * ffn
        out = _rms(x2) * g_ref[...]

        @pl.when(i < n_ctx_tiles)
        def _():
            op_ref[...] = out

        @pl.when(i >= n_ctx_tiles)
        def _():
            os_ref[...] = out

        @pl.when(i == n_tiles - 1)
        def _():
            for k in range(TOP_K):
                _wait_rows(y_hbm, nxt.at[k], sem_nxt)

    @pl.when(i % 2 == 0)
    def _():
        step(yg0, yg1, sem.at[0], sem.at[1])

    @pl.when(i % 2 == 1)
    def _():
        step(yg1, yg0, sem.at[1], sem.at[0])


def _combine(slots, rows, y_buf, x1, wts, mod3, g_final, t_ctx):
    t = x1.shape[0]
    tm = TOKEN_TILE
    n_tiles = t // tm
    n_ctx = t_ctx // tm
    return pl.pallas_call(
        functools.partial(_combine_kernel, n_ctx_tiles=n_ctx, n_tiles=n_tiles),
        grid_spec=pltpu.PrefetchScalarGridSpec(
            num_scalar_prefetch=2,
            grid=(n_tiles,),
            in_specs=[
                pl.BlockSpec(memory_space=pl.ANY),
                pl.BlockSpec((tm, D_MODEL), lambda i, s, r: (i, 0)),
                pl.BlockSpec((tm, ROUTER_LANES), lambda i, s, r: (i, 0)),
                pl.BlockSpec((1, 6, D_MODEL), lambda i, s, r: (r[i], 0, 0)),
                pl.BlockSpec((1, D_MODEL), lambda i, s, r: (0, 0)),
            ],
            out_specs=(
                pl.BlockSpec((tm, D_MODEL), lambda i, s, r: (jnp.minimum(i, n_ctx - 1), 0)),
                pl.BlockSpec((tm, D_MODEL), lambda i, s, r: (jnp.maximum(i - n_ctx, 0), 0)),
            ),
            scratch_shapes=[pltpu.VMEM((TOP_K, tm * ROW_TILE, LANES), jnp.uint32),
                            pltpu.VMEM((TOP_K, tm * ROW_TILE, LANES), jnp.uint32),
                            pltpu.SemaphoreType.DMA((2,))],
        ),
        out_shape=(jax.ShapeDtypeStruct((t_ctx, D_MODEL), F32),
                   jax.ShapeDtypeStruct((t - t_ctx, D_MODEL), F32)),
        compiler_params=_params(1),
        name="combine",
    )(slots, rows, y_buf, x1, wts, mod3, g_final)


def _rope_tables(n_tokens):
    rows = n_tokens // GRID_W
    row = jnp.repeat(jnp.arange(rows, dtype=F32), GRID_W)
    col = jnp.tile(jnp.arange(GRID_W, dtype=F32), rows)
    half = ROPE_AXIS_DIM // 2
    inv_freq = ROPE_BASE ** (-jnp.arange(half, dtype=F32) / half)
    ar = row[:, None] * inv_freq
    ac = col[:, None] * inv_freq
    cos = jnp.concatenate([jnp.cos(ar), jnp.cos(ar), jnp.cos(ac), jnp.cos(ac)], axis=-1)
    sin = jnp.concatenate([-jnp.sin(ar), jnp.sin(ar), -jnp.sin(ac), jnp.sin(ac)], axis=-1)
    return cos, sin


def _dispatch_tables(ids):
    t = ids.shape[0]
    bm = EXPERT_ROWS
    flat_e = ids.reshape(t * TOP_K)
    onehot = (flat_e[:, None] == jnp.arange(N_EXPERTS, dtype=jnp.int32)[None, :]).astype(jnp.int32)
    running = jnp.cumsum(onehot, axis=0)
    rank = jnp.sum(running * onehot, axis=1) - 1
    counts = running[-1]
    padded = (counts + bm - 1) // bm * bm
    pad_end = jnp.cumsum(padded)
    pad_start = pad_end - padded
    dest = (jnp.sum(pad_start[None, :] * onehot, axis=1) + rank).astype(jnp.int32)
    n_blocks = (t * TOP_K + N_EXPERTS * (bm - 1) + bm - 1) // bm
    flat_tok = jnp.repeat(jnp.arange(t, dtype=jnp.int32), TOP_K)
    buf_tok = jnp.zeros((n_blocks * bm,), jnp.int32).at[dest].set(flat_tok)
    n_used = (pad_end[-1] // bm).astype(jnp.int32)
    blk = jnp.arange(n_blocks, dtype=jnp.int32)
    blk = jnp.minimum(blk, n_used - 1)
    block_expert = jnp.sum((pad_end[None, :] <= (blk * bm)[:, None]).astype(jnp.int32), axis=1)
    block_expert = jnp.minimum(block_expert, N_EXPERTS - 1)
    e_ids = jnp.arange(N_EXPERTS, dtype=jnp.int32)
    later = (e_ids[None, :] > e_ids[:, None]) & (counts[None, :] > 0)
    next_of = jnp.min(jnp.where(later, e_ids[None, :], N_EXPERTS), axis=1)
    next_of = jnp.where(next_of == N_EXPERTS, -1, next_of)
    next_expert = jnp.sum(next_of[None, :] * (block_expert[:, None] == e_ids[None, :]), axis=1)
    slots = dest.reshape(t, TOP_K).T.reshape(TOP_K * t)
    return (block_expert.astype(jnp.int32), next_expert.astype(jnp.int32), n_used.reshape(1),
            buf_tok * ROW_TILE, slots * ROW_TILE)


def kernel(x_prompt, x_sample, cache_k, cache_v, c, c_ctx, w_mod, b_mod, norm_mix_g, norm_ffn_g, w_in,
           pool_w, pool_scale, attn_sink, out_norm_pool_g, out_norm_attn_g, w_out, router_group_w,
           router_group_b, router_expert_w, router_expert_b, w_gate, w_up, w_down, final_norm_g):
    depth = w_mod.shape[0]
    assert depth == 1, "single trunk layer"
    bp, lp, _ = x_prompt.shape
    bs, ls, _ = x_sample.shape
    tp, ts = bp * lp, bs * ls
    tm = TOKEN_TILE
    l = 0

    cond8 = jnp.zeros((8, D_MODEL), F32).at[:bs].set(c).at[bs].set(c_ctx)
    mod3 = _modulation(cond8, w_mod[l], b_mod[l]).reshape(8, 6, D_MODEL)
    rows_p = jnp.full((tp // tm,), bs, jnp.int32)
    rows_s = jnp.arange(ts // tm, dtype=jnp.int32) // (ls // tm)

    w_in_b = w_in[l].astype(BF16)
    w_out_b = w_out[l].astype(BF16)
    pool_w_b = pool_w[l].astype(BF16)
    g_mix = norm_mix_g[l].reshape(1, D_MODEL)
    g_ffn = norm_ffn_g[l].reshape(1, D_MODEL)
    g_pool = out_norm_pool_g[l].reshape(1, POOL_WIDTH)
    g_attn = out_norm_attn_g[l].reshape(1, ATTN_WIDTH)
    p_scale = pool_scale[l].reshape(1, POOL_WIDTH)
    sink_b = jnp.broadcast_to(attn_sink[l][:, None], (N_HEADS, LANES))
    g_final = final_norm_g.reshape(1, D_MODEL)

    rw = jnp.concatenate(
        [router_group_w[l], jnp.transpose(router_expert_w[l], (1, 0, 2)).reshape(D_MODEL, N_EXPERTS)], axis=1)
    rw = jnp.pad(rw, ((0, 0), (0, ROUTER_LANES - rw.shape[1])))
    rw_hi = rw.astype(BF16)
    rw_lo = (rw - rw_hi.astype(F32)).astype(BF16)
    rb = jnp.concatenate([router_group_b[l], router_expert_b[l].reshape(N_EXPERTS)])
    rb = jnp.pad(rb, (0, ROUTER_LANES - rb.shape[0])).reshape(1, ROUTER_LANES)

    xp = x_prompt.reshape(tp, D_MODEL)
    xs = x_sample.reshape(ts, D_MODEL)

    up_p, q_p, k_p, v_p = _in_projection(xp, rows_p, mod3, g_mix, w_in_b, None, F32)
    pool_p = _pool_mixer(up_p.reshape(bp, lp, POOL_WIDTH), pool_w_b, p_scale, g_pool).reshape(tp, POOL_WIDTH)
    attn_p = _context_attention(q_p, k_p, v_p, sink_b, g_attn, lp)
    up_s, q_s, k_s, v_s = _in_projection(xs, rows_s, mod3, g_mix, w_in_b, _rope_tables(ls), BF16)
    pool_s = _pool_mixer(up_s.reshape(bs, ls, POOL_WIDTH), pool_w_b, p_scale, g_pool).reshape(ts, POOL_WIDTH)
    attn_s = _latent_attention(
        q_s, k_s.reshape(bs, ls, KV_WIDTH), v_s.reshape(bs, ls, KV_WIDTH),
        cache_k[:, l].reshape(bs, -1, KV_WIDTH), cache_v[:, l].reshape(bs, -1, KV_WIDTH), sink_b, g_attn)

    rows_all = jnp.concatenate([rows_p, rows_s])
    x1_all, h2_all, ids_all, wts_all = _out_projection(
        (pool_p, attn_p, xp), (pool_s, attn_s, xs), rows_all, mod3, g_ffn, w_out_b, rw_hi, rw_lo, rb)

    block_expert, next_expert, n_used, buf_tok, slots = _dispatch_tables(ids_all[:, :TOP_K])
    y_buf = _experts(block_expert, next_expert, n_used, buf_tok, h2_all, w_gate[l], w_up[l], w_down[l])

    y_p, y_s = _combine(slots, rows_all, y_buf, x1_all, wts_all, mod3, g_final, tp)

    kv_shape = (bp, 1, lp, N_KV_HEADS, HEAD_DIM)
    return (y_p.reshape(bp, lp, D_MODEL), y_s.reshape(bs, ls, D_MODEL),
            k_p.reshape(kv_shape), v_p.reshape(kv_shape))
```

```python
import functools

import jax
import jax.numpy as jnp
from jax import lax
from jax.experimental import pallas as pl
from jax.experimental.pallas import tpu as pltpu

F32 = jnp.float32
BF16 = jnp.bfloat16

D_MODEL = 2048
GRID_W = 64
HEAD_DIM = 128
ATTN_WIDTH = 1024
POOL_WIDTH = 1024
N_HEADS = 8
N_KV_HEADS = 2
Q_PER_KV = 4
KV_WIDTH = 256
IN_WIDTH = 2560
WINDOW = 128
POOL_WINDOWS = (2, 4, 8, 16)
POOL_GROUP_DIM = 256
ROPE_BASE = 10000.0
ROPE_AXIS_DIM = 64
N_EXPERT_GROUPS = 4
EXPERTS_PER_GROUP = 8
N_EXPERTS = 32
TOP_K = 2
D_EXPERT = 512
NORM_EPS = 1e-6
NEG_INF = -1e30
ATTN_SCALE = HEAD_DIM ** -0.5
LOG2E = 1.4426950408889634
Q_SCALE = ATTN_SCALE * LOG2E

LANES = 128
Q_BLOCKS_PER_STEP = 4
CTX_SEQS_PER_STEP = 2
VMEM_LIMIT = 48 * 1024 * 1024
PACKED = D_MODEL // 2
ROW_TILE = 8
assert PACKED == ROW_TILE * LANES, "a packed row must fill exactly one tile"
TOKEN_TILE = 256
IN_TILE = 512
POOL_HALO = 16
Q_BLOCK = 128
EXPERT_ROWS = 256
GATHER_GROUPS = 64
ROUTER_LANES = 128


def _params(n_grid):
    return pltpu.CompilerParams(
        dimension_semantics=("arbitrary",) * n_grid, vmem_limit_bytes=VMEM_LIMIT)


def _rms(x):
    return x * lax.rsqrt(jnp.mean(x * x, axis=-1, keepdims=True) + NORM_EPS)


def _dot(a, b):
    return jnp.dot(a, b, preferred_element_type=F32)


def _dot_nt(a, b):
    return lax.dot_general(a, b, (((1,), (1,)), ((), ())), preferred_element_type=F32)


def _pack_bf16_pairs(x):
    c = x.shape[1] // 2
    lo = lax.bitcast_convert_type(x[:, :c].astype(BF16).astype(F32), jnp.uint32)
    hi = lax.bitcast_convert_type(x[:, c:].astype(BF16).astype(F32), jnp.uint32)
    return hi | (lo >> 16)


def _store_row_tiles(ref, x):
    n = x.shape[0]
    for c in range(PACKED // LANES):
        ref[pl.ds(c, n, stride=ROW_TILE), :] = x[:, c * LANES:(c + 1) * LANES]


def _load_row_tiles(ref, n):
    return jnp.concatenate(
        [ref[pl.ds(c, n, stride=ROW_TILE), :] for c in range(PACKED // LANES)], axis=-1)


def _unpack_bf16_pairs(u):
    lo = lax.bitcast_convert_type(u << 16, F32)
    hi = lax.bitcast_convert_type(u & jnp.uint32(0xFFFF0000), F32)
    return lo, hi


def _mod_kernel(cond_ref, w_ref, b_ref, o_ref):
    c = cond_ref[...]
    s = c / (1.0 + jnp.exp(-c))
    o_ref[...] = _dot(s.astype(BF16), w_ref[...].astype(BF16)) + b_ref[...]


def _modulation(cond8, w_mod, b_mod):
    n = w_mod.shape[1]
    tn = 1024
    return pl.pallas_call(
        _mod_kernel,
        grid=(n // tn,),
        in_specs=[
            pl.BlockSpec((8, D_MODEL), lambda j: (0, 0)),
            pl.BlockSpec((D_MODEL, tn), lambda j: (0, j)),
            pl.BlockSpec((1, tn), lambda j: (0, j)),
        ],
        out_specs=pl.BlockSpec((8, tn), lambda j: (0, j)),
        out_shape=jax.ShapeDtypeStruct((8, n), F32),
        compiler_params=_params(1),
        name="modulation",
    )(cond8, w_mod, b_mod.reshape(1, n))


def _rope(x, cos, sin_signed):
    lane = lax.broadcasted_iota(jnp.int32, x.shape, 1)
    partner = jnp.where((lane & 63) < 32,
                        pltpu.roll(x, HEAD_DIM - 32, axis=1),
                        pltpu.roll(x, 32, axis=1))
    return x * cos + partner * sin_signed


def _inproj_kernel(rows_ref, x_ref, mod_ref, g_ref, w_ref, *rest, rope):
    del rows_ref
    if rope:
        cos_ref, sin_ref, up_ref, q_ref, k_ref, v_ref = rest
    else:
        up_ref, q_ref, k_ref, v_ref = rest
    m = mod_ref[0]
    h = _rms(x_ref[...]) * g_ref[...]
    hb = (h * (1.0 + m[1:2]) + m[0:1]).astype(BF16)
    up_ref[...] = _dot(hb, w_ref[:, 0:POOL_WIDTH]).astype(BF16)
    q = _dot(hb, w_ref[:, POOL_WIDTH:POOL_WIDTH + ATTN_WIDTH])
    k = _dot(hb, w_ref[:, POOL_WIDTH + ATTN_WIDTH:IN_WIDTH - KV_WIDTH])
    v = _dot(hb, w_ref[:, IN_WIDTH - KV_WIDTH:IN_WIDTH])
    if rope:
        cos = cos_ref[...]
        sin = sin_ref[...]
        for hd in range(N_HEADS):
            sl = slice(hd * HEAD_DIM, (hd + 1) * HEAD_DIM)
            q_ref[:, sl] = (_rope(q[:, sl], cos, sin) * Q_SCALE).astype(BF16)
        for hd in range(N_KV_HEADS):
            sl = slice(hd * HEAD_DIM, (hd + 1) * HEAD_DIM)
            k_ref[:, sl] = _rope(k[:, sl], cos, sin).astype(k_ref.dtype)
    else:
        q_ref[...] = (q * Q_SCALE).astype(BF16)
        k_ref[...] = k.astype(k_ref.dtype)
    v_ref[...] = v.astype(v_ref.dtype)


def _in_projection(x, rows, mod3, g_mix, w_in_b, rope_tabs, kv_dtype):
    t = x.shape[0]
    tm = IN_TILE
    per = IN_TILE // TOKEN_TILE
    rope = rope_tabs is not None
    in_specs = [
        pl.BlockSpec((tm, D_MODEL), lambda i, r: (i, 0)),
        pl.BlockSpec((1, 6, D_MODEL), lambda i, r: (r[i * per], 0, 0)),
        pl.BlockSpec((1, D_MODEL), lambda i, r: (0, 0)),
        pl.BlockSpec((D_MODEL, IN_WIDTH), lambda i, r: (0, 0), pipeline_mode=pl.Buffered(1)),
    ]
    args = [x, mod3, g_mix, w_in_b]
    if rope:
        seq_tiles = rope_tabs[0].shape[0] // tm
        for tab in rope_tabs:
            in_specs.append(pl.BlockSpec((tm, HEAD_DIM), lambda i, r: (i % seq_tiles, 0)))
            args.append(tab)
    out_shape = (
        jax.ShapeDtypeStruct((t, POOL_WIDTH), BF16),
        jax.ShapeDtypeStruct((t, ATTN_WIDTH), BF16),
        jax.ShapeDtypeStruct((t, KV_WIDTH), kv_dtype),
        jax.ShapeDtypeStruct((t, KV_WIDTH), kv_dtype),
    )
    out_specs = (
        pl.BlockSpec((tm, POOL_WIDTH), lambda i, r: (i, 0)),
        pl.BlockSpec((tm, ATTN_WIDTH), lambda i, r: (i, 0)),
        pl.BlockSpec((tm, KV_WIDTH), lambda i, r: (i, 0)),
        pl.BlockSpec((tm, KV_WIDTH), lambda i, r: (i, 0)),
    )
    return pl.pallas_call(
        functools.partial(_inproj_kernel, rope=rope),
        grid_spec=pltpu.PrefetchScalarGridSpec(
            num_scalar_prefetch=1, grid=(t // tm,), in_specs=in_specs, out_specs=out_specs),
        out_shape=out_shape,
        compiler_params=_params(1),
        name="in_projection_rope" if rope else "in_projection",
    )(rows, *args)


def _pool_kernel(u_ref, pw_ref, ps_ref, g_ref, o_ref, *, seq_len):
    tq = TOKEN_TILE
    win = min(seq_len, tq + 2 * POOL_HALO)
    t0 = pl.program_id(1) * tq
    src0 = pl.multiple_of(jnp.clip(t0 - POOL_HALO, 0, seq_len - win), POOL_HALO)
    u = u_ref[0, pl.ds(src0, win), :]
    t = t0 + lax.broadcasted_iota(jnp.int32, (tq, win), 0)
    j = src0 + lax.broadcasted_iota(jnp.int32, (tq, win), 1)
    tc = t0 + lax.broadcasted_iota(jnp.int32, (tq, 1), 0)
    ys = []
    for gi, w in enumerate(POOL_WINDOWS):
        lo = jnp.maximum(t - w // 2, 0)
        hi = jnp.minimum(t + (w - w // 2), seq_len)
        cnt = (hi - lo).astype(F32)
        a = jnp.where((j >= lo) & (j < hi), 1.0, 0.0) - jnp.where(j == t, cnt, 0.0)
        cnt_col = (jnp.minimum(tc + (w - w // 2), seq_len) - jnp.maximum(tc - w // 2, 0)).astype(F32)
        sl = slice(gi * POOL_GROUP_DIM, (gi + 1) * POOL_GROUP_DIM)
        d = _dot(a.astype(BF16), u[:, sl]) / cnt_col
        ys.append(_dot(d.astype(BF16), pw_ref[gi]))
    y = jnp.concatenate(ys, axis=-1) * ps_ref[...]
    o_ref[0] = (_rms(y) * g_ref[...]).astype(BF16)


def _pool_mixer(up, pool_w_b, pool_scale, g_out_pool):
    b, seq_len, _ = up.shape
    tq = TOKEN_TILE
    return pl.pallas_call(
        functools.partial(_pool_kernel, seq_len=seq_len),
        grid=(b, seq_len // tq),
        in_specs=[
            pl.BlockSpec((1, seq_len, POOL_WIDTH), lambda bi, i: (bi, 0, 0)),
            pl.BlockSpec((len(POOL_WINDOWS), POOL_GROUP_DIM, POOL_GROUP_DIM), lambda bi, i: (0, 0, 0)),
            pl.BlockSpec((1, POOL_WIDTH), lambda bi, i: (0, 0)),
            pl.BlockSpec((1, POOL_WIDTH), lambda bi, i: (0, 0)),
        ],
        out_specs=pl.BlockSpec((1, tq, POOL_WIDTH), lambda bi, i: (bi, i, 0)),
        out_shape=jax.ShapeDtypeStruct((b, seq_len, POOL_WIDTH), BF16),
        compiler_params=_params(2),
        name=f"pool_mixer_{seq_len}",
    )(up, pool_w_b, pool_scale, g_out_pool)


def _stack_heads(q, kh, rows):
    return jnp.concatenate(
        [q[:, (kh * Q_PER_KV + g) * HEAD_DIM:(kh * Q_PER_KV + g + 1) * HEAD_DIM]
         for g in range(Q_PER_KV)], axis=0)


def _sink_column(sink_ref, kh, rows):
    return jnp.concatenate(
        [jnp.broadcast_to(sink_ref[kh * Q_PER_KV + g:kh * Q_PER_KV + g + 1, 0:1] * LOG2E, (rows, 1))
         for g in range(Q_PER_KV)], axis=0)


def _attend(q4, key_sets, value_sets, sk, band=None):
    s = [_dot_nt(q4, k) for k in key_sets]
    if band is not None:
        s[0] = jnp.where(band, s[0], NEG_INF)
    m = sk
    for si in s:
        m = jnp.maximum(m, jnp.max(si, axis=-1, keepdims=True))
    acc = None
    for si, v in zip(s, value_sets):
        v1 = jnp.concatenate([v, jnp.ones_like(v)], axis=-1)
        part = _dot(jnp.exp2(si - m).astype(BF16), v1)
        acc = part if acc is None else acc + part
    den = acc[:, HEAD_DIM:] + jnp.exp2(sk - m)
    return acc[:, :HEAD_DIM] / den


def _ctx_attn_kernel(q_ref, k_ref, v_ref, sink_ref, g_ref, o_ref, *, seq_len):
    rows = seq_len
    for j in range(q_ref.shape[0] // rows):
        rs = slice(j * rows, (j + 1) * rows)
        q = q_ref[rs, :]
        heads = [None] * N_HEADS
        for kh in range(N_KV_HEADS):
            sl = slice(kh * HEAD_DIM, (kh + 1) * HEAD_DIM)
            o = _attend(_stack_heads(q, kh, rows), [k_ref[rs, sl].astype(BF16)],
                        [v_ref[rs, sl].astype(BF16)], _sink_column(sink_ref, kh, rows))
            for g in range(Q_PER_KV):
                heads[kh * Q_PER_KV + g] = o[g * rows:(g + 1) * rows]
        y = jnp.concatenate(heads, axis=-1)
        o_ref[rs, :] = (_rms(y) * g_ref[...]).astype(BF16)


def _context_attention(q, k, v, sink_b, g_out_attn, seq_len):
    t = q.shape[0]
    blk = seq_len * CTX_SEQS_PER_STEP
    return pl.pallas_call(
        functools.partial(_ctx_attn_kernel, seq_len=seq_len),
        grid=(t // blk,),
        in_specs=[
            pl.BlockSpec((blk, ATTN_WIDTH), lambda b: (b, 0)),
            pl.BlockSpec((blk, KV_WIDTH), lambda b: (b, 0)),
            pl.BlockSpec((blk, KV_WIDTH), lambda b: (b, 0)),
            pl.BlockSpec((N_HEADS, LANES), lambda b: (0, 0)),
            pl.BlockSpec((1, ATTN_WIDTH), lambda b: (0, 0)),
        ],
        out_specs=pl.BlockSpec((blk, ATTN_WIDTH), lambda b: (b, 0)),
        out_shape=jax.ShapeDtypeStruct((t, ATTN_WIDTH), BF16),
        compiler_params=_params(1),
        name="context_attention",
    )(q, k, v, sink_b, g_out_attn)


def _lat_attn_kernel(q_ref, k_ref, v_ref, ck_ref, cv_ref, sink_ref, g_ref, o_ref, *, seq_len):
    rows = Q_BLOCK
    span = 3 * Q_BLOCK
    shape = (Q_PER_KV * rows, span)
    row_in_block = lax.broadcasted_iota(jnp.int32, shape, 0) & (rows - 1)
    col = lax.broadcasted_iota(jnp.int32, shape, 1)
    for j in range(Q_BLOCKS_PER_STEP):
        q0 = (pl.program_id(1) * Q_BLOCKS_PER_STEP + j) * rows
        start = pl.multiple_of(jnp.clip(q0 - Q_BLOCK, 0, seq_len - span), Q_BLOCK)
        q = q_ref[j * rows:(j + 1) * rows, :]
        kl = k_ref[0, pl.ds(start, span), :]
        vl = v_ref[0, pl.ds(start, span), :]
        band = jnp.abs(row_in_block - col + (q0 - start)) <= WINDOW
        heads = [None] * N_HEADS
        for kh in range(N_KV_HEADS):
            sl = slice(kh * HEAD_DIM, (kh + 1) * HEAD_DIM)
            o = _attend(_stack_heads(q, kh, rows),
                        [kl[:, sl], ck_ref[0, :, sl].astype(BF16)],
                        [vl[:, sl], cv_ref[0, :, sl].astype(BF16)],
                        _sink_column(sink_ref, kh, rows), band)
            for g in range(Q_PER_KV):
                heads[kh * Q_PER_KV + g] = o[g * rows:(g + 1) * rows]
        y = jnp.concatenate(heads, axis=-1)
        o_ref[j * rows:(j + 1) * rows, :] = (_rms(y) * g_ref[...]).astype(BF16)


def _latent_attention(q, k, v, cache_k, cache_v, sink_b, g_out_attn):
    b, seq_len, _ = k.shape
    q_rows = Q_BLOCK * Q_BLOCKS_PER_STEP
    nq = seq_len // q_rows
    past = cache_k.shape[1]
    return pl.pallas_call(
        functools.partial(_lat_attn_kernel, seq_len=seq_len),
        grid=(b, nq),
        in_specs=[
            pl.BlockSpec((q_rows, ATTN_WIDTH), lambda bi, n: (bi * nq + n, 0)),
            pl.BlockSpec((1, seq_len, KV_WIDTH), lambda bi, n: (bi, 0, 0)),
            pl.BlockSpec((1, seq_len, KV_WIDTH), lambda bi, n: (bi, 0, 0)),
            pl.BlockSpec((1, past, KV_WIDTH), lambda bi, n: (bi, 0, 0)),
            pl.BlockSpec((1, past, KV_WIDTH), lambda bi, n: (bi, 0, 0)),
            pl.BlockSpec((N_HEADS, LANES), lambda bi, n: (0, 0)),
            pl.BlockSpec((1, ATTN_WIDTH), lambda bi, n: (0, 0)),
        ],
        out_specs=pl.BlockSpec((q_rows, ATTN_WIDTH), lambda bi, n: (bi * nq + n, 0)),
        out_shape=jax.ShapeDtypeStruct((b * seq_len, ATTN_WIDTH), BF16),
        compiler_params=_params(2),
        name="latent_attention",
    )(q, k, v, cache_k, cache_v, sink_b, g_out_attn)


def _route(logits):
    lane = lax.broadcasted_iota(jnp.int32, logits.shape, 1).astype(F32)
    neg = -jnp.inf

    def first_argmax(x):
        mx = jnp.max(x, axis=-1, keepdims=True)
        return mx, jnp.min(jnp.where(x == mx, lane, float(ROUTER_LANES)), axis=-1, keepdims=True)

    gl = jnp.where(lane < N_EXPERT_GROUPS, logits, neg)
    gmax, g_idx = first_argmax(gl)
    p_g = 1.0 / jnp.sum(jnp.exp(gl - gmax), axis=-1, keepdims=True)
    base = N_EXPERT_GROUPS + EXPERTS_PER_GROUP * g_idx
    el = jnp.where((lane >= base) & (lane < base + EXPERTS_PER_GROUP), logits, neg)
    v1, i1 = first_argmax(el)
    v2, i2 = first_argmax(jnp.where(lane == i1, neg, el))
    e2 = jnp.exp(v2 - v1)
    w1 = p_g / (1.0 + e2)
    w2 = p_g * e2 / (1.0 + e2)
    ids = jnp.where(lane == 0.0, i1 - N_EXPERT_GROUPS, jnp.where(lane == 1.0, i2 - N_EXPERT_GROUPS, 0.0))
    wts = jnp.where(lane == 0.0, w1, jnp.where(lane == 1.0, w2, 0.0))
    return ids.astype(jnp.int32), wts


def _outproj_kernel(rows_ref, pool_p, attn_p, x_p, pool_s, attn_s, x_s, mod_ref, g_ref, w_ref, rwh_ref,
                    rwl_ref, rb_ref, x1_ref, h2_ref, ids_ref, wts_ref, *, n_ctx_tiles):
    del rows_ref
    i = pl.program_id(0)

    def body(pool_ref, attn_ref, x_ref):
        m = mod_ref[0]
        mix = (_dot(pool_ref[...], w_ref[0:POOL_WIDTH, :])
               + _dot(attn_ref[...], w_ref[POOL_WIDTH:POOL_WIDTH + ATTN_WIDTH, :]))
        x1 = x_ref[...] + m[2:3] * mix
        x1_ref[...] = x1
        h2 = _rms(x1) * g_ref[...] * (1.0 + m[4:5]) + m[3:4]
        _store_row_tiles(h2_ref, _pack_bf16_pairs(h2))
        hi = h2.astype(BF16)
        lo = (h2 - hi.astype(F32)).astype(BF16)
        logits = _dot(hi, rwh_ref[...]) + _dot(lo, rwh_ref[...]) + _dot(hi, rwl_ref[...]) + rb_ref[...]
        ids, wts = _route(logits)
        ids_ref[...] = ids
        wts_ref[...] = wts

    @pl.when(i < n_ctx_tiles)
    def _():
        body(pool_p, attn_p, x_p)

    @pl.when(i >= n_ctx_tiles)
    def _():
        body(pool_s, attn_s, x_s)


def _out_projection(ctx, lat, rows, mod3, g_ffn, w_out_b, rw_hi, rw_lo, rb):
    tm = TOKEN_TILE
    n_ctx = ctx[2].shape[0] // tm
    n_lat = lat[2].shape[0] // tm
    t_all = (n_ctx + n_lat) * tm

    def ctx_map(i, r):
        return (jnp.minimum(i, n_ctx - 1), 0)

    def lat_map(i, r):
        return (jnp.maximum(i - n_ctx, 0), 0)

    in_specs = [
        pl.BlockSpec((tm, POOL_WIDTH), ctx_map),
        pl.BlockSpec((tm, ATTN_WIDTH), ctx_map),
        pl.BlockSpec((tm, D_MODEL), ctx_map),
        pl.BlockSpec((tm, POOL_WIDTH), lat_map),
        pl.BlockSpec((tm, ATTN_WIDTH), lat_map),
        pl.BlockSpec((tm, D_MODEL), lat_map),
        pl.BlockSpec((1, 6, D_MODEL), lambda i, r: (r[i], 0, 0)),
        pl.BlockSpec((1, D_MODEL), lambda i, r: (0, 0)),
        pl.BlockSpec((D_MODEL, D_MODEL), lambda i, r: (0, 0), pipeline_mode=pl.Buffered(1)),
        pl.BlockSpec((D_MODEL, ROUTER_LANES), lambda i, r: (0, 0)),
        pl.BlockSpec((D_MODEL, ROUTER_LANES), lambda i, r: (0, 0)),
        pl.BlockSpec((1, ROUTER_LANES), lambda i, r: (0, 0)),
    ]
    out_shape = (
        jax.ShapeDtypeStruct((t_all, D_MODEL), F32),
        jax.ShapeDtypeStruct((t_all * ROW_TILE, LANES), jnp.uint32),
        jax.ShapeDtypeStruct((t_all, ROUTER_LANES), jnp.int32),
        jax.ShapeDtypeStruct((t_all, ROUTER_LANES), F32),
    )
    out_specs = (
        pl.BlockSpec((tm, D_MODEL), lambda i, r: (i, 0)),
        pl.BlockSpec((tm * ROW_TILE, LANES), lambda i, r: (i, 0)),
        pl.BlockSpec((tm, ROUTER_LANES), lambda i, r: (i, 0)),
        pl.BlockSpec((tm, ROUTER_LANES), lambda i, r: (i, 0)),
    )
    return pl.pallas_call(
        functools.partial(_outproj_kernel, n_ctx_tiles=n_ctx),
        grid_spec=pltpu.PrefetchScalarGridSpec(
            num_scalar_prefetch=1, grid=(n_ctx + n_lat,), in_specs=in_specs, out_specs=out_specs),
        out_shape=out_shape,
        compiler_params=_params(1),
        name="out_projection",
    )(rows, *ctx, *lat, mod3, g_ffn, w_out_b, rw_hi, rw_lo, rb)


def _gather_rows(idx_ref, base, n_rows, src_hbm, dst, sem, *, unrolled, both_queues=False, dst_row0=0):
    def one(r, priority):
        src_row = pl.multiple_of(idx_ref[base + r], ROW_TILE)
        pltpu.make_async_copy(
            src_hbm.at[pl.ds(src_row, ROW_TILE)], dst.at[pl.ds((dst_row0 + r) * ROW_TILE, ROW_TILE)],
            sem).start(priority=priority)

    if unrolled:
        for r in range(n_rows):
            one(r, r % 2 if both_queues else 0)
    else:
        def body(r, carry):
            one(r, 0)
            return carry
        lax.fori_loop(0, n_rows, body, 0, unroll=8)


def _wait_rows(src_hbm, dst, sem):
    pltpu.make_async_copy(src_hbm.at[pl.ds(0, dst.shape[0])], dst, sem).wait()


def _late_zero(x):
    u = lax.bitcast_convert_type(x, jnp.uint32)
    return ((u >> 16) >> 16).astype(jnp.int32)[0, 0]


def _moe_kernel(be_ref, ne_ref, nu_ref, tok_ref, h_hbm, wg_hbm, wu_hbm, wd_hbm, y_ref,
                xg0, xg1, xg2, sg, su, sd, wgb, wub, wdb, sem, wsem):
    b = pl.program_id(0)
    n_used = nu_ref[0]
    last = n_used - 1
    active = b < n_used
    bufs = (xg0, xg1, xg2)
    n_buf = len(bufs)

    def weight_copies(e):
        return (pltpu.make_async_copy(wg_hbm.at[e], sg, wsem.at[0]),
                pltpu.make_async_copy(wu_hbm.at[e], su, wsem.at[1]),
                pltpu.make_async_copy(wd_hbm.at[e], sd, wsem.at[2]))

    @pl.when(b == 0)
    def _():
        for cp in weight_copies(be_ref[0]):
            cp.start(priority=1)
        for i in range(n_buf - 1):
            _gather_rows(tok_ref, jnp.minimum(i, last) * EXPERT_ROWS, EXPERT_ROWS, h_hbm, bufs[i],
                         sem.at[i], unrolled=False)

    new_expert = (b == 0) | (be_ref[b] != be_ref[jnp.maximum(b - 1, 0)])

    @pl.when(active & new_expert)
    def _():
        for cp in weight_copies(be_ref[b]):
            cp.wait()
        wgb[...] = sg[...].astype(BF16)
        wub[...] = su[...].astype(BF16)
        wdb[...] = sd[...].astype(BF16)

        @pl.when(ne_ref[b] >= 0)
        def _():
            for cp in weight_copies(ne_ref[b]):
                cp.start(priority=1)

    def step(slot):
        cur = bufs[slot]
        ahead = bufs[(slot + n_buf - 1) % n_buf]
        sem_ahead = sem.at[(slot + n_buf - 1) % n_buf]
        _wait_rows(h_hbm, cur, sem.at[slot])
        base = jnp.minimum(b + n_buf - 1, last) * EXPERT_ROWS
        per = EXPERT_ROWS // GATHER_GROUPS
        group = iter(range(GATHER_GROUPS))

        def issue(zero):
            g = next(group)
            _gather_rows(tok_ref, base + g * per + zero, per, h_hbm, ahead, sem_ahead, unrolled=True,
                         both_queues=True, dst_row0=g * per)

        def corner(x, r, c):
            return _late_zero(x[r:r + 1, c:c + 1])

        bm, de = EXPERT_ROWS, D_EXPERT
        lo, hi = _unpack_bf16_pairs(_load_row_tiles(cur, bm))
        lo = lo.astype(BF16)
        hi = hi.astype(BF16)
        def issue_along(x, n):
            rows, cols = x.shape
            for i in range(n):
                issue(corner(x, (i + 1) * rows // n - 1, (i + 1) * cols // n - 1))

        n_each = GATHER_GROUPS // 8
        issue(0)
        a1 = _dot(lo, wgb[0:PACKED, :])
        issue_along(a1, n_each)
        a = a1 + _dot(hi, wgb[PACKED:D_MODEL, :])
        issue_along(a, n_each)
        u1 = _dot(lo, wub[0:PACKED, :])
        issue_along(u1, n_each)
        u = u1 + _dot(hi, wub[PACKED:D_MODEL, :])
        issue_along(u, n_each)
        act = (a / (1.0 + jnp.exp(-a)) * u).astype(BF16)
        y = _dot(act, wdb[...])
        issue_along(y, GATHER_GROUPS - 1 - 4 * n_each)
        _store_row_tiles(y_ref, _pack_bf16_pairs(y))

        @pl.when(b == last)
        def _():
            for i in range(1, n_buf):
                _wait_rows(h_hbm, bufs[(slot + i) % n_buf], sem.at[(slot + i) % n_buf])

    for slot in range(n_buf):
        @pl.when(active & (b % n_buf == slot))
        def _(slot=slot):
            step(slot)

    @pl.when(b >= n_used)
    def _():
        y_ref[...] = jnp.zeros_like(y_ref)


def _experts(block_expert, next_expert, n_used, buf_tok, h2_packed, w_gate, w_up, w_down):
    cap = buf_tok.shape[0]
    bm = EXPERT_ROWS
    xg = pltpu.VMEM((bm * ROW_TILE, LANES), jnp.uint32)
    return pl.pallas_call(
        _moe_kernel,
        grid_spec=pltpu.PrefetchScalarGridSpec(
            num_scalar_prefetch=4,
            grid=(cap // bm,),
            in_specs=[pl.BlockSpec(memory_space=pl.ANY)] * 4,
            out_specs=pl.BlockSpec((bm * ROW_TILE, LANES), lambda b, *_: (b, 0)),
            scratch_shapes=[
                xg, xg, xg,
                pltpu.VMEM((D_MODEL, D_EXPERT), F32), pltpu.VMEM((D_MODEL, D_EXPERT), F32),
                pltpu.VMEM((D_EXPERT, D_MODEL), F32),
                pltpu.VMEM((D_MODEL, D_EXPERT), BF16), pltpu.VMEM((D_MODEL, D_EXPERT), BF16),
                pltpu.VMEM((D_EXPERT, D_MODEL), BF16),
                pltpu.SemaphoreType.DMA((3,)), pltpu.SemaphoreType.DMA((3,)),
            ],
        ),
        out_shape=jax.ShapeDtypeStruct((cap * ROW_TILE, LANES), jnp.uint32),
        compiler_params=_params(1),
        name="experts",
    )(block_expert, next_expert, n_used, buf_tok, h2_packed, w_gate, w_up, w_down)


def _combine_kernel(slot_ref, rows_ref, y_hbm, x1_ref, wts_ref, mod_ref, g_ref, op_ref, os_ref, yg0, yg1, sem,
                    *, n_ctx_tiles, n_tiles):
    del rows_ref
    tm = TOKEN_TILE
    i = pl.program_id(0)
    n_slots = slot_ref.shape[0] // TOP_K

    def start(tile, buf, s, unrolled):
        for k in range(TOP_K):
            _gather_rows(slot_ref, k * n_slots + tile * tm, tm, y_hbm, buf.at[k], s, unrolled=unrolled,
                         both_queues=True)

    @pl.when(i == 0)
    def _():
        start(0, yg0, sem.at[0], False)

    def step(cur, nxt, sem_cur, sem_nxt):
        for k in range(TOP_K):
            _wait_rows(y_hbm, cur.at[k], sem_cur)
        start(jnp.minimum(i + 1, n_tiles - 1), nxt, sem_nxt, True)
        w = wts_ref[...]
        lo0, hi0 = _unpack_bf16_pairs(_load_row_tiles(cur.at[0], tm))
        lo1, hi1 = _unpack_bf16_pairs(_load_row_tiles(cur.at[1], tm))
        ffn = jnp.concatenate([lo0 * w[:, 0:1] + lo1 * w[:, 1:2], hi0 * w[:, 0:1] + hi1 * w[:, 1:2]], axis=-1)
        x2 = x1_ref[...] + mod_ref[0][5:6] * ffn
        out = _rms(x2) * g_ref[...]

        @pl.when(i < n_ctx_tiles)
        def _():
            op_ref[...] = out

        @pl.when(i >= n_ctx_tiles)
        def _():
            os_ref[...] = out

        @pl.when(i == n_tiles - 1)
        def _():
            for k in range(TOP_K):
                _wait_rows(y_hbm, nxt.at[k], sem_nxt)

    @pl.when(i % 2 == 0)
    def _():
        step(yg0, yg1, sem.at[0], sem.at[1])

    @pl.when(i % 2 == 1)
    def _():
        step(yg1, yg0, sem.at[1], sem.at[0])


def _combine(slots, rows, y_buf, x1, wts, mod3, g_final, t_ctx):
    t = x1.shape[0]
    tm = TOKEN_TILE
    n_tiles = t // tm
    n_ctx = t_ctx // tm
    return pl.pallas_call(
        functools.partial(_combine_kernel, n_ctx_tiles=n_ctx, n_tiles=n_tiles),
        grid_spec=pltpu.PrefetchScalarGridSpec(
            num_scalar_prefetch=2,
            grid=(n_tiles,),
            in_specs=[
                pl.BlockSpec(memory_space=pl.ANY),
                pl.BlockSpec((tm, D_MODEL), lambda i, s, r: (i, 0)),
                pl.BlockSpec((tm, ROUTER_LANES), lambda i, s, r: (i, 0)),
                pl.BlockSpec((1, 6, D_MODEL), lambda i, s, r: (r[i], 0, 0)),
                pl.BlockSpec((1, D_MODEL), lambda i, s, r: (0, 0)),
            ],
            out_specs=(
                pl.BlockSpec((tm, D_MODEL), lambda i, s, r: (jnp.minimum(i, n_ctx - 1), 0)),
                pl.BlockSpec((tm, D_MODEL), lambda i, s, r: (jnp.maximum(i - n_ctx, 0), 0)),
            ),
            scratch_shapes=[pltpu.VMEM((TOP_K, tm * ROW_TILE, LANES), jnp.uint32),
                            pltpu.VMEM((TOP_K, tm * ROW_TILE, LANES), jnp.uint32),
                            pltpu.SemaphoreType.DMA((2,))],
        ),
        out_shape=(jax.ShapeDtypeStruct((t_ctx, D_MODEL), F32),
                   jax.ShapeDtypeStruct((t - t_ctx, D_MODEL), F32)),
        compiler_params=_params(1),
        name="combine",
    )(slots, rows, y_buf, x1, wts, mod3, g_final)


def _rope_tables(n_tokens):
    rows = n_tokens // GRID_W
    row = jnp.repeat(jnp.arange(rows, dtype=F32), GRID_W)
    col = jnp.tile(jnp.arange(GRID_W, dtype=F32), rows)
    half = ROPE_AXIS_DIM // 2
    inv_freq = ROPE_BASE ** (-jnp.arange(half, dtype=F32) / half)
    ar = row[:, None] * inv_freq
    ac = col[:, None] * inv_freq
    cos = jnp.concatenate([jnp.cos(ar), jnp.cos(ar), jnp.cos(ac), jnp.cos(ac)], axis=-1)
    sin = jnp.concatenate([-jnp.sin(ar), jnp.sin(ar), -jnp.sin(ac), jnp.sin(ac)], axis=-1)
    return cos, sin


def _dispatch_tables(ids):
    t = ids.shape[0]
    bm = EXPERT_ROWS
    flat_e = ids.reshape(t * TOP_K)
    onehot = (flat_e[:, None] == jnp.arange(N_EXPERTS, dtype=jnp.int32)[None, :]).astype(jnp.int32)
    running = jnp.cumsum(onehot, axis=0)
    rank = jnp.sum(running * onehot, axis=1) - 1
    counts = running[-1]
    padded = (counts + bm - 1) // bm * bm
    pad_end = jnp.cumsum(padded)
    pad_start = pad_end - padded
    dest = (jnp.sum(pad_start[None, :] * onehot, axis=1) + rank).astype(jnp.int32)
    n_blocks = (t * TOP_K + N_EXPERTS * (bm - 1) + bm - 1) // bm
    flat_tok = jnp.repeat(jnp.arange(t, dtype=jnp.int32), TOP_K)
    buf_tok = jnp.zeros((n_blocks * bm,), jnp.int32).at[dest].set(flat_tok)
    n_used = (pad_end[-1] // bm).astype(jnp.int32)
    blk = jnp.arange(n_blocks, dtype=jnp.int32)
    blk = jnp.minimum(blk, n_used - 1)
    block_expert = jnp.sum((pad_end[None, :] <= (blk * bm)[:, None]).astype(jnp.int32), axis=1)
    block_expert = jnp.minimum(block_expert, N_EXPERTS - 1)
    e_ids = jnp.arange(N_EXPERTS, dtype=jnp.int32)
    later = (e_ids[None, :] > e_ids[:, None]) & (counts[None, :] > 0)
    next_of = jnp.min(jnp.where(later, e_ids[None, :], N_EXPERTS), axis=1)
    next_of = jnp.where(next_of == N_EXPERTS, -1, next_of)
    next_expert = jnp.sum(next_of[None, :] * (block_expert[:, None] == e_ids[None, :]), axis=1)
    slots = dest.reshape(t, TOP_K).T.reshape(TOP_K * t)
    return (block_expert.astype(jnp.int32), next_expert.astype(jnp.int32), n_used.reshape(1),
            buf_tok * ROW_TILE, slots * ROW_TILE)


def kernel(x_prompt, x_sample, cache_k, cache_v, c, c_ctx, w_mod, b_mod, norm_mix_g, norm_ffn_g, w_in,
           pool_w, pool_scale, attn_sink, out_norm_pool_g, out_norm_attn_g, w_out, router_group_w,
           router_group_b, router_expert_w, router_expert_b, w_gate, w_up, w_down, final_norm_g):
    depth = w_mod.shape[0]
    assert depth == 1, "single trunk layer"
    bp, lp, _ = x_prompt.shape
    bs, ls, _ = x_sample.shape
    tp, ts = bp * lp, bs * ls
    tm = TOKEN_TILE
    l = 0

    cond8 = jnp.zeros((8, D_MODEL), F32).at[:bs].set(c).at[bs].set(c_ctx)
    mod3 = _modulation(cond8, w_mod[l], b_mod[l]).reshape(8, 6, D_MODEL)
    rows_p = jnp.full((tp // tm,), bs, jnp.int32)
    rows_s = jnp.arange(ts // tm, dtype=jnp.int32) // (ls // tm)

    w_in_b = w_in[l].astype(BF16)
    w_out_b = w_out[l].astype(BF16)
    pool_w_b = pool_w[l].astype(BF16)
    g_mix = norm_mix_g[l].reshape(1, D_MODEL)
    g_ffn = norm_ffn_g[l].reshape(1, D_MODEL)
    g_pool = out_norm_pool_g[l].reshape(1, POOL_WIDTH)
    g_attn = out_norm_attn_g[l].reshape(1, ATTN_WIDTH)
    p_scale = pool_scale[l].reshape(1, POOL_WIDTH)
    sink_b = jnp.broadcast_to(attn_sink[l][:, None], (N_HEADS, LANES))
    g_final = final_norm_g.reshape(1, D_MODEL)

    rw = jnp.concatenate(
        [router_group_w[l], jnp.transpose(router_expert_w[l], (1, 0, 2)).reshape(D_MODEL, N_EXPERTS)], axis=1)
    rw = jnp.pad(rw, ((0, 0), (0, ROUTER_LANES - rw.shape[1])))
    rw_hi = rw.astype(BF16)
    rw_lo = (rw - rw_hi.astype(F32)).astype(BF16)
    rb = jnp.concatenate([router_group_b[l], router_expert_b[l].reshape(N_EXPERTS)])
    rb = jnp.pad(rb, (0, ROUTER_LANES - rb.shape[0])).reshape(1, ROUTER_LANES)

    xp = x_prompt.reshape(tp, D_MODEL)
    xs = x_sample.reshape(ts, D_MODEL)

    up_p, q_p, k_p, v_p = _in_projection(xp, rows_p, mod3, g_mix, w_in_b, None, F32)
    pool_p = _pool_mixer(up_p.reshape(bp, lp, POOL_WIDTH), pool_w_b, p_scale, g_pool).reshape(tp, POOL_WIDTH)
    attn_p = _context_attention(q_p, k_p, v_p, sink_b, g_attn, lp)
    up_s, q_s, k_s, v_s = _in_projection(xs, rows_s, mod3, g_mix, w_in_b, _rope_tables(ls), BF16)
    pool_s = _pool_mixer(up_s.reshape(bs, ls, POOL_WIDTH), pool_w_b, p_scale, g_pool).reshape(ts, POOL_WIDTH)
    attn_s = _latent_attention(
        q_s, k_s.reshape(bs, ls, KV_WIDTH), v_s.reshape(bs, ls, KV_WIDTH),
        cache_k[:, l].reshape(bs, -1, KV_WIDTH), cache_v[:, l].reshape(bs, -1, KV_WIDTH), sink_b, g_attn)

    rows_all = jnp.concatenate([rows_p, rows_s])
    x1_all, h2_all, ids_all, wts_all = _out_projection(
        (pool_p, attn_p, xp), (pool_s, attn_s, xs), rows_all, mod3, g_ffn, w_out_b, rw_hi, rw_lo, rb)

    block_expert, next_expert, n_used, buf_tok, slots = _dispatch_tables(ids_all[:, :TOP_K])
    y_buf = _experts(block_expert, next_expert, n_used, buf_tok, h2_all, w_gate[l], w_up[l], w_down[l])

    y_p, y_s = _combine(slots, rows_all, y_buf, x1_all, wts_all, mod3, g_final, tp)

    kv_shape = (bp, 1, lp, N_KV_HEADS, HEAD_DIM)
    return (y_p.reshape(bp, lp, D_MODEL), y_s.reshape(bs, ls, D_MODEL),
            k_p.reshape(kv_shape), v_p.reshape(kv_shape))
```

```python
import functools

import jax
import jax.numpy as jnp
from jax import lax
from jax.experimental import pallas as pl
from jax.experimental.pallas import tpu as pltpu

F32 = jnp.float32
BF16 = jnp.bfloat16

D_MODEL = 2048
GRID_W = 64
HEAD_DIM = 128
ATTN_WIDTH = 1024
POOL_WIDTH = 1024
N_HEADS = 8
N_KV_HEADS = 2
Q_PER_KV = 4
KV_WIDTH = 256
IN_WIDTH = 2560
WINDOW = 128
POOL_WINDOWS = (2, 4, 8, 16)
POOL_GROUP_DIM = 256
ROPE_BASE = 10000.0
ROPE_AXIS_DIM = 64
N_EXPERT_GROUPS = 4
EXPERTS_PER_GROUP = 8
N_EXPERTS = 32
TOP_K = 2
D_EXPERT = 512
NORM_EPS = 1e-6
NEG_INF = -1e30
ATTN_SCALE = HEAD_DIM ** -0.5
LOG2E = 1.4426950408889634
Q_SCALE = ATTN_SCALE * LOG2E

LANES = 128
Q_BLOCKS_PER_STEP = 4
CTX_SEQS_PER_STEP = 2
VMEM_LIMIT = 48 * 1024 * 1024
PACKED = D_MODEL // 2
ROW_TILE = 8
assert PACKED == ROW_TILE * LANES, "a packed row must fill exactly one tile"
TOKEN_TILE = 256
IN_TILE = 512
POOL_HALO = 16
Q_BLOCK = 128
EXPERT_ROWS = 256
GATHER_GROUPS = 64
ROUTER_LANES = 128


def _params(n_grid):
    return pltpu.CompilerParams(
        dimension_semantics=("arbitrary",) * n_grid, vmem_limit_bytes=VMEM_LIMIT)


def _rms(x):
    return x * lax.rsqrt(jnp.mean(x * x, axis=-1, keepdims=True) + NORM_EPS)


def _dot(a, b):
    return jnp.dot(a, b, preferred_element_type=F32)


def _dot_nt(a, b):
    return lax.dot_general(a, b, (((1,), (1,)), ((), ())), preferred_element_type=F32)


def _pack_bf16_pairs(x):
    c = x.shape[1] // 2
    lo = lax.bitcast_convert_type(x[:, :c].astype(BF16).astype(F32), jnp.uint32)
    hi = lax.bitcast_convert_type(x[:, c:].astype(BF16).astype(F32), jnp.uint32)
    return hi | (lo >> 16)


def _store_row_tiles(ref, x):
    n = x.shape[0]
    for c in range(PACKED // LANES):
        ref[pl.ds(c, n, stride=ROW_TILE), :] = x[:, c * LANES:(c + 1) * LANES]


def _load_row_tiles(ref, n):
    return jnp.concatenate(
        [ref[pl.ds(c, n, stride=ROW_TILE), :] for c in range(PACKED // LANES)], axis=-1)


def _unpack_bf16_pairs(u):
    lo = lax.bitcast_convert_type(u << 16, F32)
    hi = lax.bitcast_convert_type(u & jnp.uint32(0xFFFF0000), F32)
    return lo, hi


def _mod_kernel(cond_ref, w_ref, b_ref, o_ref):
    c = cond_ref[...]
    s = c / (1.0 + jnp.exp(-c))
    o_ref[...] = _dot(s.astype(BF16), w_ref[...].astype(BF16)) + b_ref[...]


def _modulation(cond8, w_mod, b_mod):
    n = w_mod.shape[1]
    tn = 1024
    return pl.pallas_call(
        _mod_kernel,
        grid=(n // tn,),
        in_specs=[
            pl.BlockSpec((8, D_MODEL), lambda j: (0, 0)),
            pl.BlockSpec((D_MODEL, tn), lambda j: (0, j)),
            pl.BlockSpec((1, tn), lambda j: (0, j)),
        ],
        out_specs=pl.BlockSpec((8, tn), lambda j: (0, j)),
        out_shape=jax.ShapeDtypeStruct((8, n), F32),
        compiler_params=_params(1),
        name="modulation",
    )(cond8, w_mod, b_mod.reshape(1, n))


def _rope(x, cos, sin_signed):
    lane = lax.broadcasted_iota(jnp.int32, x.shape, 1)
    partner = jnp.where((lane & 63) < 32,
                        pltpu.roll(x, HEAD_DIM - 32, axis=1),
                        pltpu.roll(x, 32, axis=1))
    return x * cos + partner * sin_signed


def _inproj_kernel(rows_ref, x_ref, mod_ref, g_ref, w_ref, *rest, rope):
    del rows_ref
    if rope:
        cos_ref, sin_ref, up_ref, q_ref, k_ref, v_ref = rest
    else:
        up_ref, q_ref, k_ref, v_ref = rest
    m = mod_ref[0]
    h = _rms(x_ref[...]) * g_ref[...]
    hb = (h * (1.0 + m[1:2]) + m[0:1]).astype(BF16)
    up_ref[...] = _dot(hb, w_ref[:, 0:POOL_WIDTH]).astype(BF16)
    q = _dot(hb, w_ref[:, POOL_WIDTH:POOL_WIDTH + ATTN_WIDTH])
    k = _dot(hb, w_ref[:, POOL_WIDTH + ATTN_WIDTH:IN_WIDTH - KV_WIDTH])
    v = _dot(hb, w_ref[:, IN_WIDTH - KV_WIDTH:IN_WIDTH])
    if rope:
        cos = cos_ref[...]
        sin = sin_ref[...]
        for hd in range(N_HEADS):
            sl = slice(hd * HEAD_DIM, (hd + 1) * HEAD_DIM)
            q_ref[:, sl] = (_rope(q[:, sl], cos, sin) * Q_SCALE).astype(BF16)
        for hd in range(N_KV_HEADS):
            sl = slice(hd * HEAD_DIM, (hd + 1) * HEAD_DIM)
            k_ref[:, sl] = _rope(k[:, sl], cos, sin).astype(k_ref.dtype)
    else:
        q_ref[...] = (q * Q_SCALE).astype(BF16)
        k_ref[...] = k.astype(k_ref.dtype)
    v_ref[...] = v.astype(v_ref.dtype)


def _in_projection(x, rows, mod3, g_mix, w_in_b, rope_tabs, kv_dtype):
    t = x.shape[0]
    tm = IN_TILE
    per = IN_TILE // TOKEN_TILE
    rope = rope_tabs is not None
    in_specs = [
        pl.BlockSpec((tm, D_MODEL), lambda i, r: (i, 0)),
        pl.BlockSpec((1, 6, D_MODEL), lambda i, r: (r[i * per], 0, 0)),
        pl.BlockSpec((1, D_MODEL), lambda i, r: (0, 0)),
        pl.BlockSpec((D_MODEL, IN_WIDTH), lambda i, r: (0, 0), pipeline_mode=pl.Buffered(1)),
    ]
    args = [x, mod3, g_mix, w_in_b]
    if rope:
        seq_tiles = rope_tabs[0].shape[0] // tm
        for tab in rope_tabs:
            in_specs.append(pl.BlockSpec((tm, HEAD_DIM), lambda i, r: (i % seq_tiles, 0)))
            args.append(tab)
    out_shape = (
        jax.ShapeDtypeStruct((t, POOL_WIDTH), BF16),
        jax.ShapeDtypeStruct((t, ATTN_WIDTH), BF16),
        jax.ShapeDtypeStruct((t, KV_WIDTH), kv_dtype),
        jax.ShapeDtypeStruct((t, KV_WIDTH), kv_dtype),
    )
    out_specs = (
        pl.BlockSpec((tm, POOL_WIDTH), lambda i, r: (i, 0)),
        pl.BlockSpec((tm, ATTN_WIDTH), lambda i, r: (i, 0)),
        pl.BlockSpec((tm, KV_WIDTH), lambda i, r: (i, 0)),
        pl.BlockSpec((tm, KV_WIDTH), lambda i, r: (i, 0)),
    )
    return pl.pallas_call(
        functools.partial(_inproj_kernel, rope=rope),
        grid_spec=pltpu.PrefetchScalarGridSpec(
            num_scalar_prefetch=1, grid=(t // tm,), in_specs=in_specs, out_specs=out_specs),
        out_shape=out_shape,
        compiler_params=_params(1),
        name="in_projection_rope" if rope else "in_projection",
    )(rows, *args)


def _pool_kernel(u_ref, pw_ref, ps_ref, g_ref, o_ref, a_ref, cnt_ref, *, seq_len):
    tq = TOKEN_TILE
    win = min(seq_len, tq + 2 * POOL_HALO)
    i = pl.program_id(1)
    n_tiles = seq_len // tq
    t0 = i * tq
    src0 = pl.multiple_of(jnp.clip(t0 - POOL_HALO, 0, seq_len - win), POOL_HALO)
    u = u_ref[0, pl.ds(src0, win), :]

    @pl.when((i <= 1) | (i == n_tiles - 1))
    def _():
        t = t0 + lax.broadcasted_iota(jnp.int32, (tq, win), 0)
        j = src0 + lax.broadcasted_iota(jnp.int32, (tq, win), 1)
        tc = t0 + lax.broadcasted_iota(jnp.int32, (tq, 1), 0)
        for gi, w in enumerate(POOL_WINDOWS):
            lo = jnp.maximum(t - w // 2, 0)
            hi = jnp.minimum(t + (w - w // 2), seq_len)
            cnt = (hi - lo).astype(F32)
            a = jnp.where((j >= lo) & (j < hi), 1.0, 0.0) - jnp.where(j == t, cnt, 0.0)
            a_ref[gi] = a.astype(BF16)
            cnt_ref[gi] = (jnp.minimum(tc + (w - w // 2), seq_len) - jnp.maximum(tc - w // 2, 0)).astype(F32)

    ys = []
    for gi in range(len(POOL_WINDOWS)):
        sl = slice(gi * POOL_GROUP_DIM, (gi + 1) * POOL_GROUP_DIM)
        d = _dot(a_ref[gi], u[:, sl]) / cnt_ref[gi]
        ys.append(_dot(d.astype(BF16), pw_ref[gi]))
    y = jnp.concatenate(ys, axis=-1) * ps_ref[...]
    o_ref[0] = (_rms(y) * g_ref[...]).astype(BF16)


def _pool_mixer(up, pool_w_b, pool_scale, g_out_pool):
    b, seq_len, _ = up.shape
    tq = TOKEN_TILE
    return pl.pallas_call(
        functools.partial(_pool_kernel, seq_len=seq_len),
        grid=(b, seq_len // tq),
        in_specs=[
            pl.BlockSpec((1, seq_len, POOL_WIDTH), lambda bi, i: (bi, 0, 0)),
            pl.BlockSpec((len(POOL_WINDOWS), POOL_GROUP_DIM, POOL_GROUP_DIM), lambda bi, i: (0, 0, 0)),
            pl.BlockSpec((1, POOL_WIDTH), lambda bi, i: (0, 0)),
            pl.BlockSpec((1, POOL_WIDTH), lambda bi, i: (0, 0)),
        ],
        out_specs=pl.BlockSpec((1, tq, POOL_WIDTH), lambda bi, i: (bi, i, 0)),
        out_shape=jax.ShapeDtypeStruct((b, seq_len, POOL_WIDTH), BF16),
        scratch_shapes=[
            pltpu.VMEM((len(POOL_WINDOWS), tq, min(seq_len, tq + 2 * POOL_HALO)), BF16),
            pltpu.VMEM((len(POOL_WINDOWS), tq, 1), F32),
        ],
        compiler_params=_params(2),
        name=f"pool_mixer_{seq_len}",
    )(up, pool_w_b, pool_scale, g_out_pool)


def _stack_heads(q, kh, rows):
    return jnp.concatenate(
        [q[:, (kh * Q_PER_KV + g) * HEAD_DIM:(kh * Q_PER_KV + g + 1) * HEAD_DIM]
         for g in range(Q_PER_KV)], axis=0)


def _sink_column(sink_ref, kh, rows):
    return jnp.concatenate(
        [jnp.broadcast_to(sink_ref[kh * Q_PER_KV + g:kh * Q_PER_KV + g + 1, 0:1] * LOG2E, (rows, 1))
         for g in range(Q_PER_KV)], axis=0)


def _attend(q4, key_sets, value_sets, sk, band=None):
    s = [_dot_nt(q4, k) for k in key_sets]
    if band is not None:
        s[0] = jnp.where(band, s[0], NEG_INF)
    m = sk
    for si in s:
        m = jnp.maximum(m, jnp.max(si, axis=-1, keepdims=True))
    acc = None
    for si, v in zip(s, value_sets):
        v1 = jnp.concatenate([v, jnp.ones_like(v)], axis=-1)
        part = _dot(jnp.exp2(si - m).astype(BF16), v1)
        acc = part if acc is None else acc + part
    den = acc[:, HEAD_DIM:] + jnp.exp2(sk - m)
    return acc[:, :HEAD_DIM] / den


def _ctx_attn_kernel(q_ref, k_ref, v_ref, sink_ref, g_ref, o_ref, *, seq_len):
    rows = seq_len
    for j in range(q_ref.shape[0] // rows):
        rs = slice(j * rows, (j + 1) * rows)
        q = q_ref[rs, :]
        heads = [None] * N_HEADS
        for kh in range(N_KV_HEADS):
            sl = slice(kh * HEAD_DIM, (kh + 1) * HEAD_DIM)
            o = _attend(_stack_heads(q, kh, rows), [k_ref[rs, sl].astype(BF16)],
                        [v_ref[rs, sl].astype(BF16)], _sink_column(sink_ref, kh, rows))
            for g in range(Q_PER_KV):
                heads[kh * Q_PER_KV + g] = o[g * rows:(g + 1) * rows]
        y = jnp.concatenate(heads, axis=-1)
        o_ref[rs, :] = (_rms(y) * g_ref[...]).astype(BF16)


def _context_attention(q, k, v, sink_b, g_out_attn, seq_len):
    t = q.shape[0]
    blk = seq_len * CTX_SEQS_PER_STEP
    return pl.pallas_call(
        functools.partial(_ctx_attn_kernel, seq_len=seq_len),
        grid=(t // blk,),
        in_specs=[
            pl.BlockSpec((blk, ATTN_WIDTH), lambda b: (b, 0)),
            pl.BlockSpec((blk, KV_WIDTH), lambda b: (b, 0)),
            pl.BlockSpec((blk, KV_WIDTH), lambda b: (b, 0)),
            pl.BlockSpec((N_HEADS, LANES), lambda b: (0, 0)),
            pl.BlockSpec((1, ATTN_WIDTH), lambda b: (0, 0)),
        ],
        out_specs=pl.BlockSpec((blk, ATTN_WIDTH), lambda b: (b, 0)),
        out_shape=jax.ShapeDtypeStruct((t, ATTN_WIDTH), BF16),
        compiler_params=_params(1),
        name="context_attention",
    )(q, k, v, sink_b, g_out_attn)


def _lat_attn_kernel(q_ref, k_ref, v_ref, ck_ref, cv_ref, sink_ref, g_ref, o_ref, *, seq_len):
    rows = Q_BLOCK
    span = 3 * Q_BLOCK
    shape = (Q_PER_KV * rows, span)
    row_in_block = lax.broadcasted_iota(jnp.int32, shape, 0) & (rows - 1)
    col = lax.broadcasted_iota(jnp.int32, shape, 1)
    for j in range(Q_BLOCKS_PER_STEP):
        q0 = (pl.program_id(1) * Q_BLOCKS_PER_STEP + j) * rows
        start = pl.multiple_of(jnp.clip(q0 - Q_BLOCK, 0, seq_len - span), Q_BLOCK)
        q = q_ref[j * rows:(j + 1) * rows, :]
        kl = k_ref[0, pl.ds(start, span), :]
        vl = v_ref[0, pl.ds(start, span), :]
        band = jnp.abs(row_in_block - col + (q0 - start)) <= WINDOW
        heads = [None] * N_HEADS
        for kh in range(N_KV_HEADS):
            sl = slice(kh * HEAD_DIM, (kh + 1) * HEAD_DIM)
            o = _attend(_stack_heads(q, kh, rows),
                        [kl[:, sl], ck_ref[0, :, sl].astype(BF16)],
                        [vl[:, sl], cv_ref[0, :, sl].astype(BF16)],
                        _sink_column(sink_ref, kh, rows), band)
            for g in range(Q_PER_KV):
                heads[kh * Q_PER_KV + g] = o[g * rows:(g + 1) * rows]
        y = jnp.concatenate(heads, axis=-1)
        o_ref[j * rows:(j + 1) * rows, :] = (_rms(y) * g_ref[...]).astype(BF16)


def _latent_attention(q, k, v, cache_k, cache_v, sink_b, g_out_attn):
    b, seq_len, _ = k.shape
    q_rows = Q_BLOCK * Q_BLOCKS_PER_STEP
    nq = seq_len // q_rows
    past = cache_k.shape[1]
    return pl.pallas_call(
        functools.partial(_lat_attn_kernel, seq_len=seq_len),
        grid=(b, nq),
        in_specs=[
            pl.BlockSpec((q_rows, ATTN_WIDTH), lambda bi, n: (bi * nq + n, 0)),
            pl.BlockSpec((1, seq_len, KV_WIDTH), lambda bi, n: (bi, 0, 0)),
            pl.BlockSpec((1, seq_len, KV_WIDTH), lambda bi, n: (bi, 0, 0)),
            pl.BlockSpec((1, past, KV_WIDTH), lambda bi, n: (bi, 0, 0)),
            pl.BlockSpec((1, past, KV_WIDTH), lambda bi, n: (bi, 0, 0)),
            pl.BlockSpec((N_HEADS, LANES), lambda bi, n: (0, 0)),
            pl.BlockSpec((1, ATTN_WIDTH), lambda bi, n: (0, 0)),
        ],
        out_specs=pl.BlockSpec((q_rows, ATTN_WIDTH), lambda bi, n: (bi * nq + n, 0)),
        out_shape=jax.ShapeDtypeStruct((b * seq_len, ATTN_WIDTH), BF16),
        compiler_params=_params(2),
        name="latent_attention",
    )(q, k, v, cache_k, cache_v, sink_b, g_out_attn)


def _route(logits):
    lane = lax.broadcasted_iota(jnp.int32, logits.shape, 1).astype(F32)
    neg = -jnp.inf

    def first_argmax(x):
        mx = jnp.max(x, axis=-1, keepdims=True)
        return mx, jnp.min(jnp.where(x == mx, lane, float(ROUTER_LANES)), axis=-1, keepdims=True)

    gl = jnp.where(lane < N_EXPERT_GROUPS, logits, neg)
    gmax, g_idx = first_argmax(gl)
    p_g = 1.0 / jnp.sum(jnp.exp(gl - gmax), axis=-1, keepdims=True)
    base = N_EXPERT_GROUPS + EXPERTS_PER_GROUP * g_idx
    el = jnp.where((lane >= base) & (lane < base + EXPERTS_PER_GROUP), logits, neg)
    v1, i1 = first_argmax(el)
    v2, i2 = first_argmax(jnp.where(lane == i1, neg, el))
    e2 = jnp.exp(v2 - v1)
    w1 = p_g / (1.0 + e2)
    w2 = p_g * e2 / (1.0 + e2)
    ids = jnp.where(lane == 0.0, i1 - N_EXPERT_GROUPS, jnp.where(lane == 1.0, i2 - N_EXPERT_GROUPS, 0.0))
    wts = jnp.where(lane == 0.0, w1, jnp.where(lane == 1.0, w2, 0.0))
    return ids.astype(jnp.int32), wts


def _outproj_kernel(rows_ref, pool_p, attn_p, x_p, pool_s, attn_s, x_s, mod_ref, g_ref, w_ref, rw_ref,
                    rb_ref, x1_ref, h2_ref, ids_ref, wts_ref, *, n_ctx_tiles):
    del rows_ref
    i = pl.program_id(0)

    def body(pool_ref, attn_ref, x_ref):
        m = mod_ref[0]
        mix = (_dot(pool_ref[...], w_ref[0:POOL_WIDTH, :])
               + _dot(attn_ref[...], w_ref[POOL_WIDTH:POOL_WIDTH + ATTN_WIDTH, :]))
        x1 = x_ref[...] + m[2:3] * mix
        x1_ref[...] = x1
        h2 = _rms(x1) * g_ref[...] * (1.0 + m[4:5]) + m[3:4]
        _store_row_tiles(h2_ref, _pack_bf16_pairs(h2))
        hi = h2.astype(BF16)
        lo = (h2 - hi.astype(F32)).astype(BF16)
        both = _dot(hi, rw_ref[...])
        logits = (both[:, :ROUTER_LANES] + both[:, ROUTER_LANES:] + _dot(lo, rw_ref[:, :ROUTER_LANES])
                  + rb_ref[...])
        ids, wts = _route(logits)
        ids_ref[...] = ids
        wts_ref[...] = wts

    @pl.when(i < n_ctx_tiles)
    def _():
        body(pool_p, attn_p, x_p)

    @pl.when(i >= n_ctx_tiles)
    def _():
        body(pool_s, attn_s, x_s)


def _out_projection(ctx, lat, rows, mod3, g_ffn, w_out_b, rw_split, rb):
    tm = TOKEN_TILE
    n_ctx = ctx[2].shape[0] // tm
    n_lat = lat[2].shape[0] // tm
    t_all = (n_ctx + n_lat) * tm

    def ctx_map(i, r):
        return (jnp.minimum(i, n_ctx - 1), 0)

    def lat_map(i, r):
        return (jnp.maximum(i - n_ctx, 0), 0)

    in_specs = [
        pl.BlockSpec((tm, POOL_WIDTH), ctx_map),
        pl.BlockSpec((tm, ATTN_WIDTH), ctx_map),
        pl.BlockSpec((tm, D_MODEL), ctx_map),
        pl.BlockSpec((tm, POOL_WIDTH), lat_map),
        pl.BlockSpec((tm, ATTN_WIDTH), lat_map),
        pl.BlockSpec((tm, D_MODEL), lat_map),
        pl.BlockSpec((1, 6, D_MODEL), lambda i, r: (r[i], 0, 0)),
        pl.BlockSpec((1, D_MODEL), lambda i, r: (0, 0)),
        pl.BlockSpec((D_MODEL, D_MODEL), lambda i, r: (0, 0), pipeline_mode=pl.Buffered(1)),
        pl.BlockSpec((D_MODEL, 2 * ROUTER_LANES), lambda i, r: (0, 0)),
        pl.BlockSpec((1, ROUTER_LANES), lambda i, r: (0, 0)),
    ]
    out_shape = (
        jax.ShapeDtypeStruct((t_all, D_MODEL), F32),
        jax.ShapeDtypeStruct((t_all * ROW_TILE, LANES), jnp.uint32),
        jax.ShapeDtypeStruct((t_all, ROUTER_LANES), jnp.int32),
        jax.ShapeDtypeStruct((t_all, ROUTER_LANES), F32),
    )
    out_specs = (
        pl.BlockSpec((tm, D_MODEL), lambda i, r: (i, 0)),
        pl.BlockSpec((tm * ROW_TILE, LANES), lambda i, r: (i, 0)),
        pl.BlockSpec((tm, ROUTER_LANES), lambda i, r: (i, 0)),
        pl.BlockSpec((tm, ROUTER_LANES), lambda i, r: (i, 0)),
    )
    return pl.pallas_call(
        functools.partial(_outproj_kernel, n_ctx_tiles=n_ctx),
        grid_spec=pltpu.PrefetchScalarGridSpec(
            num_scalar_prefetch=1, grid=(n_ctx + n_lat,), in_specs=in_specs, out_specs=out_specs),
        out_shape=out_shape,
        compiler_params=_params(1),
        name="out_projection",
    )(rows, *ctx, *lat, mod3, g_ffn, w_out_b, rw_split, rb)


def _gather_rows(idx_ref, base, n_rows, src_hbm, dst, sem, *, unrolled, both_queues=False, dst_row0=0):
    def one(r, priority):
        src_row = pl.multiple_of(idx_ref[base + r], ROW_TILE)
        pltpu.make_async_copy(
            src_hbm.at[pl.ds(src_row, ROW_TILE)], dst.at[pl.ds((dst_row0 + r) * ROW_TILE, ROW_TILE)],
            sem).start(priority=priority)

    if unrolled:
        for r in range(n_rows):
            one(r, r % 2 if both_queues else 0)
    else:
        def body(r, carry):
            one(r, 0)
            return carry
        lax.fori_loop(0, n_rows, body, 0, unroll=8)


def _wait_rows(src_hbm, dst, sem):
    pltpu.make_async_copy(src_hbm.at[pl.ds(0, dst.shape[0])], dst, sem).wait()


def _late_zero(x):
    u = lax.bitcast_convert_type(x, jnp.uint32)
    return ((u >> 16) >> 16).astype(jnp.int32)[0, 0]


def _moe_kernel(be_ref, ne_ref, nu_ref, tok_ref, h_hbm, wg_hbm, wu_hbm, wd_hbm, y_ref,
                xg0, xg1, xg2, sg, su, sd, wgb, wub, wdb, sem, wsem):
    b = pl.program_id(0)
    n_used = nu_ref[0]
    last = n_used - 1
    active = b < n_used
    bufs = (xg0, xg1, xg2)
    n_buf = len(bufs)

    def weight_copies(e):
        return (pltpu.make_async_copy(wg_hbm.at[e], sg, wsem.at[0]),
                pltpu.make_async_copy(wu_hbm.at[e], su, wsem.at[1]),
                pltpu.make_async_copy(wd_hbm.at[e], sd, wsem.at[2]))

    @pl.when(b == 0)
    def _():
        for cp in weight_copies(be_ref[0]):
            cp.start(priority=1)
        for i in range(n_buf - 1):
            _gather_rows(tok_ref, jnp.minimum(i, last) * EXPERT_ROWS, EXPERT_ROWS, h_hbm, bufs[i],
                         sem.at[i], unrolled=False)

    new_expert = (b == 0) | (be_ref[b] != be_ref[jnp.maximum(b - 1, 0)])

    @pl.when(active & new_expert)
    def _():
        for cp in weight_copies(be_ref[b]):
            cp.wait()
        wgb[...] = sg[...].astype(BF16)
        wub[...] = su[...].astype(BF16)
        wdb[...] = sd[...].astype(BF16)

        @pl.when(ne_ref[b] >= 0)
        def _():
            for cp in weight_copies(ne_ref[b]):
                cp.start(priority=1)

    def step(slot):
        cur = bufs[slot]
        ahead = bufs[(slot + n_buf - 1) % n_buf]
        sem_ahead = sem.at[(slot + n_buf - 1) % n_buf]
        _wait_rows(h_hbm, cur, sem.at[slot])
        base = jnp.minimum(b + n_buf - 1, last) * EXPERT_ROWS
        per = EXPERT_ROWS // GATHER_GROUPS
        group = iter(range(GATHER_GROUPS))

        def issue(zero):
            g = next(group)
            _gather_rows(tok_ref, base + g * per + zero, per, h_hbm, ahead, sem_ahead, unrolled=True,
                         both_queues=True, dst_row0=g * per)

        def corner(x, r, c):
            return _late_zero(x[r:r + 1, c:c + 1])

        bm, de = EXPERT_ROWS, D_EXPERT
        lo, hi = _unpack_bf16_pairs(_load_row_tiles(cur, bm))
        lo = lo.astype(BF16)
        hi = hi.astype(BF16)
        def issue_along(x, n):
            rows, cols = x.shape
            for i in range(n):
                issue(corner(x, (i + 1) * rows // n - 1, (i + 1) * cols // n - 1))

        n_each = GATHER_GROUPS // 8
        issue(0)
        a1 = _dot(lo, wgb[0:PACKED, :])
        issue_along(a1, n_each)
        a = a1 + _dot(hi, wgb[PACKED:D_MODEL, :])
        issue_along(a, n_each)
        u1 = _dot(lo, wub[0:PACKED, :])
        issue_along(u1, n_each)
        u = u1 + _dot(hi, wub[PACKED:D_MODEL, :])
        issue_along(u, n_each)
        act = (a / (1.0 + jnp.exp(-a)) * u).astype(BF16)
        y = _dot(act, wdb[...])
        issue_along(y, GATHER_GROUPS - 1 - 4 * n_each)
        _store_row_tiles(y_ref, _pack_bf16_pairs(y))

        @pl.when(b == last)
        def _():
            for i in range(1, n_buf):
                _wait_rows(h_hbm, bufs[(slot + i) % n_buf], sem.at[(slot + i) % n_buf])

    for slot in range(n_buf):
        @pl.when(active & (b % n_buf == slot))
        def _(slot=slot):
            step(slot)

    @pl.when(b >= n_used)
    def _():
        y_ref[...] = jnp.zeros_like(y_ref)


def _experts(block_expert, next_expert, n_used, buf_tok, h2_packed, w_gate, w_up, w_down):
    cap = buf_tok.shape[0]
    bm = EXPERT_ROWS
    xg = pltpu.VMEM((bm * ROW_TILE, LANES), jnp.uint32)
    return pl.pallas_call(
        _moe_kernel,
        grid_spec=pltpu.PrefetchScalarGridSpec(
            num_scalar_prefetch=4,
            grid=(cap // bm,),
            in_specs=[pl.BlockSpec(memory_space=pl.ANY)] * 4,
            out_specs=pl.BlockSpec((bm * ROW_TILE, LANES), lambda b, *_: (b, 0)),
            scratch_shapes=[
                xg, xg, xg,
                pltpu.VMEM((D_MODEL, D_EXPERT), F32), pltpu.VMEM((D_MODEL, D_EXPERT), F32),
                pltpu.VMEM((D_EXPERT, D_MODEL), F32),
                pltpu.VMEM((D_MODEL, D_EXPERT), BF16), pltpu.VMEM((D_MODEL, D_EXPERT), BF16),
                pltpu.VMEM((D_EXPERT, D_MODEL), BF16),
                pltpu.SemaphoreType.DMA((3,)), pltpu.SemaphoreType.DMA((3,)),
            ],
        ),
        out_shape=jax.ShapeDtypeStruct((cap * ROW_TILE, LANES), jnp.uint32),
        compiler_params=_params(1),
        name="experts",
    )(block_expert, next_expert, n_used, buf_tok, h2_packed, w_gate, w_up, w_down)


def _combine_kernel(slot_ref, rows_ref, y_hbm, x1_ref, wts_ref, mod_ref, g_ref, op_ref, os_ref, yg0, yg1, sem,
                    *, n_ctx_tiles, n_tiles):
    del rows_ref
    tm = TOKEN_TILE
    i = pl.program_id(0)
    n_slots = slot_ref.shape[0] // TOP_K

    def start(tile, buf, s, unrolled):
        for k in range(TOP_K):
            _gather_rows(slot_ref, k * n_slots + tile * tm, tm, y_hbm, buf.at[k], s, unrolled=unrolled,
                         both_queues=True)

    @pl.when(i == 0)
    def _():
        start(0, yg0, sem.at[0], False)

    def step(cur, nxt, sem_cur, sem_nxt):
        for k in range(TOP_K):
            _wait_rows(y_hbm, cur.at[k], sem_cur)
        start(jnp.minimum(i + 1, n_tiles - 1), nxt, sem_nxt, True)
        w = wts_ref[...]
        lo0, hi0 = _unpack_bf16_pairs(_load_row_tiles(cur.at[0], tm))
        lo1, hi1 = _unpack_bf16_pairs(_load_row_tiles(cur.at[1], tm))
        ffn = jnp.concatenate([lo0 * w[:, 0:1] + lo1 * w[:, 1:2], hi0 * w[:, 0:1] + hi1 * w[:, 1:2]], axis=-1)
        x2 = x1_ref[...] + mod_ref[0][5:6] * ffn
        out = _rms(x2) * g_ref[...]

        @pl.when(i < n_ctx_tiles)
        def _():
            op_ref[...] = out

        @pl.when(i >= n_ctx_tiles)
        def _():
            os_ref[...] = out

        @pl.when(i == n_tiles - 1)
        def _():
            for k in range(TOP_K):
                _wait_rows(y_hbm, nxt.at[k], sem_nxt)

    @pl.when(i % 2 == 0)
    def _():
        step(yg0, yg1, sem.at[0], sem.at[1])

    @pl.when(i % 2 == 1)
    def _():
        step(yg1, yg0, sem.at[1], sem.at[0])


def _combine(slots, rows, y_buf, x1, wts, mod3, g_final, t_ctx):
    t = x1.shape[0]
    tm = TOKEN_TILE
    n_tiles = t // tm
    n_ctx = t_ctx // tm
    return pl.pallas_call(
        functools.partial(_combine_kernel, n_ctx_tiles=n_ctx, n_tiles=n_tiles),
        grid_spec=pltpu.PrefetchScalarGridSpec(
            num_scalar_prefetch=2,
            grid=(n_tiles,),
            in_specs=[
                pl.BlockSpec(memory_space=pl.ANY),
                pl.BlockSpec((tm, D_MODEL), lambda i, s, r: (i, 0)),
                pl.BlockSpec((tm, ROUTER_LANES), lambda i, s, r: (i, 0)),
                pl.BlockSpec((1, 6, D_MODEL), lambda i, s, r: (r[i], 0, 0)),
                pl.BlockSpec((1, D_MODEL), lambda i, s, r: (0, 0)),
            ],
            out_specs=(
                pl.BlockSpec((tm, D_MODEL), lambda i, s, r: (jnp.minimum(i, n_ctx - 1), 0)),
                pl.BlockSpec((tm, D_MODEL), lambda i, s, r: (jnp.maximum(i - n_ctx, 0), 0)),
            ),
            scratch_shapes=[pltpu.VMEM((TOP_K, tm * ROW_TILE, LANES), jnp.uint32),
                            pltpu.VMEM((TOP_K, tm * ROW_TILE, LANES), jnp.uint32),
                            pltpu.SemaphoreType.DMA((2,))],
        ),
        out_shape=(jax.ShapeDtypeStruct((t_ctx, D_MODEL), F32),
                   jax.ShapeDtypeStruct((t - t_ctx, D_MODEL), F32)),
        compiler_params=_params(1),
        name="combine",
    )(slots, rows, y_buf, x1, wts, mod3, g_final)


def _rope_tables(n_tokens):
    rows = n_tokens // GRID_W
    row = jnp.repeat(jnp.arange(rows, dtype=F32), GRID_W)
    col = jnp.tile(jnp.arange(GRID_W, dtype=F32), rows)
    half = ROPE_AXIS_DIM // 2
    inv_freq = ROPE_BASE ** (-jnp.arange(half, dtype=F32) / half)
    ar = row[:, None] * inv_freq
    ac = col[:, None] * inv_freq
    cos = jnp.concatenate([jnp.cos(ar), jnp.cos(ar), jnp.cos(ac), jnp.cos(ac)], axis=-1)
    sin = jnp.concatenate([-jnp.sin(ar), jnp.sin(ar), -jnp.sin(ac), jnp.sin(ac)], axis=-1)
    return cos, sin


def _invert_kernel(dest_ref, gap_lo_ref, gap_hi_ref, out_ref):
    def clear(i, carry):
        out_ref[i] = 0
        return carry

    def clear_gap(e, carry):
        lax.fori_loop(gap_lo_ref[e], gap_hi_ref[e], clear, 0)
        return carry
    lax.fori_loop(0, gap_lo_ref.shape[0], clear_gap, 0)

    group = 16

    def place(i, carry):
        a0 = i * group
        tile0 = i * (group // TOP_K * ROW_TILE)
        rows = [dest_ref[a0 + k] for k in range(group)]
        for k in range(group):
            out_ref[rows[k]] = tile0 + (k // TOP_K) * ROW_TILE
        return carry
    lax.fori_loop(0, dest_ref.shape[0] // group, place, 0)


def _invert_slots(dest, gap_lo, gap_hi, cap):
    return pl.pallas_call(
        _invert_kernel,
        grid_spec=pltpu.PrefetchScalarGridSpec(
            num_scalar_prefetch=3, grid=(1,), in_specs=[],
            out_specs=pl.BlockSpec(memory_space=pltpu.SMEM)),
        out_shape=jax.ShapeDtypeStruct((cap,), jnp.int32),
        compiler_params=_params(1),
        name="invert_slots",
    )(dest, gap_lo, gap_hi)


def _dispatch_tables(ids):
    t = ids.shape[0]
    bm = EXPERT_ROWS
    flat_e = ids.reshape(t * TOP_K)
    onehot = (flat_e[:, None] == jnp.arange(N_EXPERTS, dtype=jnp.int32)[None, :]).astype(jnp.int32)
    running = jnp.cumsum(onehot, axis=0)
    rank = jnp.sum(running * onehot, axis=1) - 1
    counts = running[-1]
    padded = (counts + bm - 1) // bm * bm
    pad_end = jnp.cumsum(padded)
    pad_start = pad_end - padded
    dest = (jnp.sum(pad_start[None, :] * onehot, axis=1) + rank).astype(jnp.int32)
    n_blocks = (t * TOP_K + N_EXPERTS * (bm - 1) + bm - 1) // bm
    cap = n_blocks * bm
    gap_lo = jnp.concatenate([pad_start + counts, pad_end[-1:]]).astype(jnp.int32)
    gap_hi = jnp.concatenate([pad_end, jnp.full((1,), cap, pad_end.dtype)]).astype(jnp.int32)
    buf_tok = _invert_slots(dest, gap_lo, gap_hi, cap)
    n_used = (pad_end[-1] // bm).astype(jnp.int32)
    blk = jnp.arange(n_blocks, dtype=jnp.int32)
    blk = jnp.minimum(blk, n_used - 1)
    block_expert = jnp.sum((pad_end[None, :] <= (blk * bm)[:, None]).astype(jnp.int32), axis=1)
    block_expert = jnp.minimum(block_expert, N_EXPERTS - 1)
    e_ids = jnp.arange(N_EXPERTS, dtype=jnp.int32)
    later = (e_ids[None, :] > e_ids[:, None]) & (counts[None, :] > 0)
    next_of = jnp.min(jnp.where(later, e_ids[None, :], N_EXPERTS), axis=1)
    next_of = jnp.where(next_of == N_EXPERTS, -1, next_of)
    next_expert = jnp.sum(next_of[None, :] * (block_expert[:, None] == e_ids[None, :]), axis=1)
    slots = dest.reshape(t, TOP_K).T.reshape(TOP_K * t)
    return (block_expert.astype(jnp.int32), next_expert.astype(jnp.int32), n_used.reshape(1),
            buf_tok, slots * ROW_TILE)


def kernel(x_prompt, x_sample, cache_k, cache_v, c, c_ctx, w_mod, b_mod, norm_mix_g, norm_ffn_g, w_in,
           pool_w, pool_scale, attn_sink, out_norm_pool_g, out_norm_attn_g, w_out, router_group_w,
           router_group_b, router_expert_w, router_expert_b, w_gate, w_up, w_down, final_norm_g):
    depth = w_mod.shape[0]
    assert depth == 1, "single trunk layer"
    bp, lp, _ = x_prompt.shape
    bs, ls, _ = x_sample.shape
    tp, ts = bp * lp, bs * ls
    tm = TOKEN_TILE
    l = 0

    cond8 = jnp.zeros((8, D_MODEL), F32).at[:bs].set(c).at[bs].set(c_ctx)
    mod3 = _modulation(cond8, w_mod[l], b_mod[l]).reshape(8, 6, D_MODEL)
    rows_p = jnp.full((tp // tm,), bs, jnp.int32)
    rows_s = jnp.arange(ts // tm, dtype=jnp.int32) // (ls // tm)

    w_in_b = w_in[l].astype(BF16)
    w_out_b = w_out[l].astype(BF16)
    pool_w_b = pool_w[l].astype(BF16)
    g_mix = norm_mix_g[l].reshape(1, D_MODEL)
    g_ffn = norm_ffn_g[l].reshape(1, D_MODEL)
    g_pool = out_norm_pool_g[l].reshape(1, POOL_WIDTH)
    g_attn = out_norm_attn_g[l].reshape(1, ATTN_WIDTH)
    p_scale = pool_scale[l].reshape(1, POOL_WIDTH)
    sink_b = jnp.broadcast_to(attn_sink[l][:, None], (N_HEADS, LANES))
    g_final = final_norm_g.reshape(1, D_MODEL)

    rw = jnp.concatenate(
        [router_group_w[l], jnp.transpose(router_expert_w[l], (1, 0, 2)).reshape(D_MODEL, N_EXPERTS)], axis=1)
    rw = jnp.pad(rw, ((0, 0), (0, ROUTER_LANES - rw.shape[1])))
    rw_hi = rw.astype(BF16)
    rw_split = jnp.concatenate([rw_hi, (rw - rw_hi.astype(F32)).astype(BF16)], axis=1)
    rb = jnp.concatenate([router_group_b[l], router_expert_b[l].reshape(N_EXPERTS)])
    rb = jnp.pad(rb, (0, ROUTER_LANES - rb.shape[0])).reshape(1, ROUTER_LANES)

    xp = x_prompt.reshape(tp, D_MODEL)
    xs = x_sample.reshape(ts, D_MODEL)

    up_p, q_p, k_p, v_p = _in_projection(xp, rows_p, mod3, g_mix, w_in_b, None, F32)
    pool_p = _pool_mixer(up_p.reshape(bp, lp, POOL_WIDTH), pool_w_b, p_scale, g_pool).reshape(tp, POOL_WIDTH)
    attn_p = _context_attention(q_p, k_p, v_p, sink_b, g_attn, lp)
    up_s, q_s, k_s, v_s = _in_projection(xs, rows_s, mod3, g_mix, w_in_b, _rope_tables(ls), BF16)
    pool_s = _pool_mixer(up_s.reshape(bs, ls, POOL_WIDTH), pool_w_b, p_scale, g_pool).reshape(ts, POOL_WIDTH)
    attn_s = _latent_attention(
        q_s, k_s.reshape(bs, ls, KV_WIDTH), v_s.reshape(bs, ls, KV_WIDTH),
        cache_k[:, l].reshape(bs, -1, KV_WIDTH), cache_v[:, l].reshape(bs, -1, KV_WIDTH), sink_b, g_attn)

    rows_all = jnp.concatenate([rows_p, rows_s])
    x1_all, h2_all, ids_all, wts_all = _out_projection(
        (pool_p, attn_p, xp), (pool_s, attn_s, xs), rows_all, mod3, g_ffn, w_out_b, rw_split, rb)

    block_expert, next_expert, n_used, buf_tok, slots = _dispatch_tables(ids_all[:, :TOP_K])
    y_buf = _experts(block_expert, next_expert, n_used, buf_tok, h2_all, w_gate[l], w_up[l], w_down[l])

    y_p, y_s = _combine(slots, rows_all, y_buf, x1_all, wts_all, mod3, g_final, tp)

    kv_shape = (bp, 1, lp, N_KV_HEADS, HEAD_DIM)
    return (y_p.reshape(bp, lp, D_MODEL), y_s.reshape(bs, ls, D_MODEL),
            k_p.reshape(kv_shape), v_p.reshape(kv_shape))
```

```python
import functools

import jax
import jax.numpy as jnp
from jax import lax
from jax.experimental import pallas as pl
from jax.experimental.pallas import tpu as pltpu

F32 = jnp.float32
BF16 = jnp.bfloat16

D_MODEL = 2048
GRID_W = 64
HEAD_DIM = 128
ATTN_WIDTH = 1024
POOL_WIDTH = 1024
N_HEADS = 8
N_KV_HEADS = 2
Q_PER_KV = 4
KV_WIDTH = 256
IN_WIDTH = 2560
WINDOW = 128
POOL_WINDOWS = (2, 4, 8, 16)
POOL_GROUP_DIM = 256
ROPE_BASE = 10000.0
ROPE_AXIS_DIM = 64
N_EXPERT_GROUPS = 4
EXPERTS_PER_GROUP = 8
N_EXPERTS = 32
TOP_K = 2
D_EXPERT = 512
NORM_EPS = 1e-6
NEG_INF = -1e30
ATTN_SCALE = HEAD_DIM ** -0.5
LOG2E = 1.4426950408889634
Q_SCALE = ATTN_SCALE * LOG2E

LANES = 128
Q_BLOCKS_PER_STEP = 4
CTX_SEQS_PER_STEP = 2
VMEM_LIMIT = 48 * 1024 * 1024
PACKED = D_MODEL // 2
ROW_TILE = 8
assert PACKED == ROW_TILE * LANES, "a packed row must fill exactly one tile"
TOKEN_TILE = 256
IN_TILE = 512
POOL_TILES_PER_STEP = 2
POOL_HALO = 16
Q_BLOCK = 128
EXPERT_ROWS = 256
GATHER_GROUPS = 64
ROUTER_LANES = 128


def _params(n_grid):
    return pltpu.CompilerParams(
        dimension_semantics=("arbitrary",) * n_grid, vmem_limit_bytes=VMEM_LIMIT)


def _rms(x):
    return x * lax.rsqrt(jnp.mean(x * x, axis=-1, keepdims=True) + NORM_EPS)


def _dot(a, b):
    return jnp.dot(a, b, preferred_element_type=F32)


def _dot_nt(a, b):
    return lax.dot_general(a, b, (((1,), (1,)), ((), ())), preferred_element_type=F32)


def _pack_bf16_pairs(x):
    c = x.shape[1] // 2
    lo = lax.bitcast_convert_type(x[:, :c].astype(BF16).astype(F32), jnp.uint32)
    hi = lax.bitcast_convert_type(x[:, c:].astype(BF16).astype(F32), jnp.uint32)
    return hi | (lo >> 16)


def _store_row_tiles(ref, x):
    n = x.shape[0]
    for c in range(PACKED // LANES):
        ref[pl.ds(c, n, stride=ROW_TILE), :] = x[:, c * LANES:(c + 1) * LANES]


def _load_row_tiles(ref, n):
    return jnp.concatenate(
        [ref[pl.ds(c, n, stride=ROW_TILE), :] for c in range(PACKED // LANES)], axis=-1)


def _unpack_bf16_pairs(u):
    lo = lax.bitcast_convert_type(u << 16, F32)
    hi = lax.bitcast_convert_type(u & jnp.uint32(0xFFFF0000), F32)
    return lo, hi


def _mod_kernel(cond_ref, w_ref, b_ref, o_ref):
    c = cond_ref[...]
    s = c / (1.0 + jnp.exp(-c))
    o_ref[...] = _dot(s.astype(BF16), w_ref[...].astype(BF16)) + b_ref[...]


def _modulation(cond8, w_mod, b_mod):
    n = w_mod.shape[1]
    tn = 1024
    return pl.pallas_call(
        _mod_kernel,
        grid=(n // tn,),
        in_specs=[
            pl.BlockSpec((8, D_MODEL), lambda j: (0, 0)),
            pl.BlockSpec((D_MODEL, tn), lambda j: (0, j)),
            pl.BlockSpec((1, tn), lambda j: (0, j)),
        ],
        out_specs=pl.BlockSpec((8, tn), lambda j: (0, j)),
        out_shape=jax.ShapeDtypeStruct((8, n), F32),
        compiler_params=_params(1),
        name="modulation",
    )(cond8, w_mod, b_mod.reshape(1, n))


def _rope(x, cos, sin_signed):
    lane = lax.broadcasted_iota(jnp.int32, x.shape, 1)
    partner = jnp.where((lane & 63) < 32,
                        pltpu.roll(x, HEAD_DIM - 32, axis=1),
                        pltpu.roll(x, 32, axis=1))
    return x * cos + partner * sin_signed


def _inproj_kernel(rows_ref, x_ref, mod_ref, g_ref, w_ref, *rest, rope):
    del rows_ref
    if rope:
        cos_ref, sin_ref, up_ref, q_ref, k_ref, v_ref = rest
    else:
        up_ref, q_ref, k_ref, v_ref = rest
    m = mod_ref[0]
    h = _rms(x_ref[...]) * g_ref[...]
    hb = (h * (1.0 + m[1:2]) + m[0:1]).astype(BF16)
    up_ref[...] = _dot(hb, w_ref[:, 0:POOL_WIDTH]).astype(BF16)
    q = _dot(hb, w_ref[:, POOL_WIDTH:POOL_WIDTH + ATTN_WIDTH])
    k = _dot(hb, w_ref[:, POOL_WIDTH + ATTN_WIDTH:IN_WIDTH - KV_WIDTH])
    v = _dot(hb, w_ref[:, IN_WIDTH - KV_WIDTH:IN_WIDTH])
    if rope:
        cos = cos_ref[...]
        sin = sin_ref[...]
        for hd in range(N_HEADS):
            sl = slice(hd * HEAD_DIM, (hd + 1) * HEAD_DIM)
            q_ref[:, sl] = (_rope(q[:, sl], cos, sin) * Q_SCALE).astype(BF16)
        for hd in range(N_KV_HEADS):
            sl = slice(hd * HEAD_DIM, (hd + 1) * HEAD_DIM)
            k_ref[:, sl] = _rope(k[:, sl], cos, sin).astype(k_ref.dtype)
    else:
        q_ref[...] = (q * Q_SCALE).astype(BF16)
        k_ref[...] = k.astype(k_ref.dtype)
    v_ref[...] = v.astype(v_ref.dtype)


def _in_projection(x, rows, mod3, g_mix, w_in_b, rope_tabs, kv_dtype):
    t = x.shape[0]
    tm = IN_TILE
    per = IN_TILE // TOKEN_TILE
    rope = rope_tabs is not None
    in_specs = [
        pl.BlockSpec((tm, D_MODEL), lambda i, r: (i, 0)),
        pl.BlockSpec((1, 6, D_MODEL), lambda i, r: (r[i * per], 0, 0)),
        pl.BlockSpec((1, D_MODEL), lambda i, r: (0, 0)),
        pl.BlockSpec((D_MODEL, IN_WIDTH), lambda i, r: (0, 0), pipeline_mode=pl.Buffered(1)),
    ]
    args = [x, mod3, g_mix, w_in_b]
    if rope:
        seq_tiles = rope_tabs[0].shape[0] // tm
        for tab in rope_tabs:
            in_specs.append(pl.BlockSpec((tm, HEAD_DIM), lambda i, r: (i % seq_tiles, 0)))
            args.append(tab)
    out_shape = (
        jax.ShapeDtypeStruct((t, POOL_WIDTH), BF16),
        jax.ShapeDtypeStruct((t, ATTN_WIDTH), BF16),
        jax.ShapeDtypeStruct((t, KV_WIDTH), kv_dtype),
        jax.ShapeDtypeStruct((t, KV_WIDTH), kv_dtype),
    )
    out_specs = (
        pl.BlockSpec((tm, POOL_WIDTH), lambda i, r: (i, 0)),
        pl.BlockSpec((tm, ATTN_WIDTH), lambda i, r: (i, 0)),
        pl.BlockSpec((tm, KV_WIDTH), lambda i, r: (i, 0)),
        pl.BlockSpec((tm, KV_WIDTH), lambda i, r: (i, 0)),
    )
    return pl.pallas_call(
        functools.partial(_inproj_kernel, rope=rope),
        grid_spec=pltpu.PrefetchScalarGridSpec(
            num_scalar_prefetch=1, grid=(t // tm,), in_specs=in_specs, out_specs=out_specs),
        out_shape=out_shape,
        compiler_params=_params(1),
        name="in_projection_rope" if rope else "in_projection",
    )(rows, *args)


def _pool_kernel(u_ref, pw_ref, ps_ref, g_ref, o_ref, *, seq_len):
    tq = TOKEN_TILE
    win = min(seq_len, tq + 2 * POOL_HALO)
    for sub in range(o_ref.shape[1] // tq):
        t0 = pl.program_id(1) * o_ref.shape[1] + sub * tq
        src0 = pl.multiple_of(jnp.clip(t0 - POOL_HALO, 0, seq_len - win), POOL_HALO)
        u = u_ref[0, pl.ds(src0, win), :]
        t = t0 + lax.broadcasted_iota(jnp.int32, (tq, win), 0)
        j = src0 + lax.broadcasted_iota(jnp.int32, (tq, win), 1)
        tc = t0 + lax.broadcasted_iota(jnp.int32, (tq, 1), 0)
        ys = []
        for gi, w in enumerate(POOL_WINDOWS):
            lo = jnp.maximum(t - w // 2, 0)
            hi = jnp.minimum(t + (w - w // 2), seq_len)
            cnt = (hi - lo).astype(F32)
            a = jnp.where((j >= lo) & (j < hi), 1.0, 0.0) - jnp.where(j == t, cnt, 0.0)
            cnt_col = (jnp.minimum(tc + (w - w // 2), seq_len) - jnp.maximum(tc - w // 2, 0)).astype(F32)
            sl = slice(gi * POOL_GROUP_DIM, (gi + 1) * POOL_GROUP_DIM)
            d = _dot(a.astype(BF16), u[:, sl]) / cnt_col
            ys.append(_dot(d.astype(BF16), pw_ref[gi]))
        y = jnp.concatenate(ys, axis=-1) * ps_ref[...]
        o_ref[0, sub * tq:(sub + 1) * tq, :] = (_rms(y) * g_ref[...]).astype(BF16)


def _pool_mixer(up, pool_w_b, pool_scale, g_out_pool):
    b, seq_len, _ = up.shape
    tq = TOKEN_TILE * min(POOL_TILES_PER_STEP, seq_len // TOKEN_TILE)
    return pl.pallas_call(
        functools.partial(_pool_kernel, seq_len=seq_len),
        grid=(b, seq_len // tq),
        in_specs=[
            pl.BlockSpec((1, seq_len, POOL_WIDTH), lambda bi, i: (bi, 0, 0)),
            pl.BlockSpec((len(POOL_WINDOWS), POOL_GROUP_DIM, POOL_GROUP_DIM), lambda bi, i: (0, 0, 0)),
            pl.BlockSpec((1, POOL_WIDTH), lambda bi, i: (0, 0)),
            pl.BlockSpec((1, POOL_WIDTH), lambda bi, i: (0, 0)),
        ],
        out_specs=pl.BlockSpec((1, tq, POOL_WIDTH), lambda bi, i: (bi, i, 0)),
        out_shape=jax.ShapeDtypeStruct((b, seq_len, POOL_WIDTH), BF16),
        compiler_params=_params(2),
        name=f"pool_mixer_{seq_len}",
    )(up, pool_w_b, pool_scale, g_out_pool)


def _stack_heads(q, kh, rows):
    return jnp.concatenate(
        [q[:, (kh * Q_PER_KV + g) * HEAD_DIM:(kh * Q_PER_KV + g + 1) * HEAD_DIM]
         for g in range(Q_PER_KV)], axis=0)


def _sink_column(sink_ref, kh, rows):
    return jnp.concatenate(
        [jnp.broadcast_to(sink_ref[kh * Q_PER_KV + g:kh * Q_PER_KV + g + 1, 0:1] * LOG2E, (rows, 1))
         for g in range(Q_PER_KV)], axis=0)


def _attend(q4, key_sets, value_sets, sk, band=None):
    s = [_dot_nt(q4, k) for k in key_sets]
    if band is not None:
        s[0] = jnp.where(band, s[0], NEG_INF)
    m = sk
    for si in s:
        m = jnp.maximum(m, jnp.max(si, axis=-1, keepdims=True))
    acc = None
    for si, v in zip(s, value_sets):
        v1 = jnp.concatenate([v, jnp.ones_like(v)], axis=-1)
        part = _dot(jnp.exp2(si - m).astype(BF16), v1)
        acc = part if acc is None else acc + part
    den = acc[:, HEAD_DIM:] + jnp.exp2(sk - m)
    return acc[:, :HEAD_DIM] / den


def _ctx_attn_kernel(q_ref, k_ref, v_ref, sink_ref, g_ref, o_ref, *, seq_len):
    rows = seq_len
    for j in range(q_ref.shape[0] // rows):
        rs = slice(j * rows, (j + 1) * rows)
        q = q_ref[rs, :]
        heads = [None] * N_HEADS
        for kh in range(N_KV_HEADS):
            sl = slice(kh * HEAD_DIM, (kh + 1) * HEAD_DIM)
            o = _attend(_stack_heads(q, kh, rows), [k_ref[rs, sl].astype(BF16)],
                        [v_ref[rs, sl].astype(BF16)], _sink_column(sink_ref, kh, rows))
            for g in range(Q_PER_KV):
                heads[kh * Q_PER_KV + g] = o[g * rows:(g + 1) * rows]
        y = jnp.concatenate(heads, axis=-1)
        o_ref[rs, :] = (_rms(y) * g_ref[...]).astype(BF16)


def _context_attention(q, k, v, sink_b, g_out_attn, seq_len):
    t = q.shape[0]
    blk = seq_len * CTX_SEQS_PER_STEP
    return pl.pallas_call(
        functools.partial(_ctx_attn_kernel, seq_len=seq_len),
        grid=(t // blk,),
        in_specs=[
            pl.BlockSpec((blk, ATTN_WIDTH), lambda b: (b, 0)),
            pl.BlockSpec((blk, KV_WIDTH), lambda b: (b, 0)),
            pl.BlockSpec((blk, KV_WIDTH), lambda b: (b, 0)),
            pl.BlockSpec((N_HEADS, LANES), lambda b: (0, 0)),
            pl.BlockSpec((1, ATTN_WIDTH), lambda b: (0, 0)),
        ],
        out_specs=pl.BlockSpec((blk, ATTN_WIDTH), lambda b: (b, 0)),
        out_shape=jax.ShapeDtypeStruct((t, ATTN_WIDTH), BF16),
        compiler_params=_params(1),
        name="context_attention",
    )(q, k, v, sink_b, g_out_attn)


def _lat_attn_kernel(q_ref, k_ref, v_ref, ck_ref, cv_ref, sink_ref, g_ref, o_ref, *, seq_len):
    rows = Q_BLOCK
    span = 3 * Q_BLOCK
    shape = (Q_PER_KV * rows, span)
    row_in_block = lax.broadcasted_iota(jnp.int32, shape, 0) & (rows - 1)
    col = lax.broadcasted_iota(jnp.int32, shape, 1)
    for j in range(Q_BLOCKS_PER_STEP):
        q0 = (pl.program_id(1) * Q_BLOCKS_PER_STEP + j) * rows
        start = pl.multiple_of(jnp.clip(q0 - Q_BLOCK, 0, seq_len - span), Q_BLOCK)
        q = q_ref[j * rows:(j + 1) * rows, :]
        kl = k_ref[0, pl.ds(start, span), :]
        vl = v_ref[0, pl.ds(start, span), :]
        band = jnp.abs(row_in_block - col + (q0 - start)) <= WINDOW
        heads = [None] * N_HEADS
        for kh in range(N_KV_HEADS):
            sl = slice(kh * HEAD_DIM, (kh + 1) * HEAD_DIM)
            o = _attend(_stack_heads(q, kh, rows),
                        [kl[:, sl], ck_ref[0, :, sl].astype(BF16)],
                        [vl[:, sl], cv_ref[0, :, sl].astype(BF16)],
                        _sink_column(sink_ref, kh, rows), band)
            for g in range(Q_PER_KV):
                heads[kh * Q_PER_KV + g] = o[g * rows:(g + 1) * rows]
        y = jnp.concatenate(heads, axis=-1)
        o_ref[j * rows:(j + 1) * rows, :] = (_rms(y) * g_ref[...]).astype(BF16)


def _latent_attention(q, k, v, cache_k, cache_v, sink_b, g_out_attn):
    b, seq_len, _ = k.shape
    q_rows = Q_BLOCK * Q_BLOCKS_PER_STEP
    nq = seq_len // q_rows
    past = cache_k.shape[1]
    return pl.pallas_call(
        functools.partial(_lat_attn_kernel, seq_len=seq_len),
        grid=(b, nq),
        in_specs=[
            pl.BlockSpec((q_rows, ATTN_WIDTH), lambda bi, n: (bi * nq + n, 0)),
            pl.BlockSpec((1, seq_len, KV_WIDTH), lambda bi, n: (bi, 0, 0)),
            pl.BlockSpec((1, seq_len, KV_WIDTH), lambda bi, n: (bi, 0, 0)),
            pl.BlockSpec((1, past, KV_WIDTH), lambda bi, n: (bi, 0, 0)),
            pl.BlockSpec((1, past, KV_WIDTH), lambda bi, n: (bi, 0, 0)),
            pl.BlockSpec((N_HEADS, LANES), lambda bi, n: (0, 0)),
            pl.BlockSpec((1, ATTN_WIDTH), lambda bi, n: (0, 0)),
        ],
        out_specs=pl.BlockSpec((q_rows, ATTN_WIDTH), lambda bi, n: (bi * nq + n, 0)),
        out_shape=jax.ShapeDtypeStruct((b * seq_len, ATTN_WIDTH), BF16),
        compiler_params=_params(2),
        name="latent_attention",
    )(q, k, v, cache_k, cache_v, sink_b, g_out_attn)


def _route(logits):
    lane = lax.broadcasted_iota(jnp.int32, logits.shape, 1).astype(F32)
    neg = -jnp.inf

    def first_argmax(x):
        mx = jnp.max(x, axis=-1, keepdims=True)
        return mx, jnp.min(jnp.where(x == mx, lane, float(ROUTER_LANES)), axis=-1, keepdims=True)

    gl = jnp.where(lane < N_EXPERT_GROUPS, logits, neg)
    gmax, g_idx = first_argmax(gl)
    p_g = 1.0 / jnp.sum(jnp.exp(gl - gmax), axis=-1, keepdims=True)
    base = N_EXPERT_GROUPS + EXPERTS_PER_GROUP * g_idx
    el = jnp.where((lane >= base) & (lane < base + EXPERTS_PER_GROUP), logits, neg)
    v1, i1 = first_argmax(el)
    v2, i2 = first_argmax(jnp.where(lane == i1, neg, el))
    e2 = jnp.exp(v2 - v1)
    w1 = p_g / (1.0 + e2)
    w2 = p_g * e2 / (1.0 + e2)
    ids = jnp.where(lane == 0.0, i1 - N_EXPERT_GROUPS, jnp.where(lane == 1.0, i2 - N_EXPERT_GROUPS, 0.0))
    wts = jnp.where(lane == 0.0, w1, jnp.where(lane == 1.0, w2, 0.0))
    return ids.astype(jnp.int32), wts


def _outproj_kernel(rows_ref, pool_p, attn_p, x_p, pool_s, attn_s, x_s, mod_ref, g_ref, w_ref, rw_ref,
                    rb_ref, x1_ref, h2_ref, ids_ref, wts_ref, *, n_ctx_tiles):
    del rows_ref
    i = pl.program_id(0)

    def body(pool_ref, attn_ref, x_ref):
        m = mod_ref[0]
        mix = (_dot(pool_ref[...], w_ref[0:POOL_WIDTH, :])
               + _dot(attn_ref[...], w_ref[POOL_WIDTH:POOL_WIDTH + ATTN_WIDTH, :]))
        x1 = x_ref[...] + m[2:3] * mix
        x1_ref[...] = x1
        h2 = _rms(x1) * g_ref[...] * (1.0 + m[4:5]) + m[3:4]
        _store_row_tiles(h2_ref, _pack_bf16_pairs(h2))
        hi = h2.astype(BF16)
        lo = (h2 - hi.astype(F32)).astype(BF16)
        both = _dot(hi, rw_ref[...])
        logits = (both[:, :ROUTER_LANES] + both[:, ROUTER_LANES:] + _dot(lo, rw_ref[:, :ROUTER_LANES])
                  + rb_ref[...])
        ids, wts = _route(logits)
        ids_ref[...] = ids
        wts_ref[...] = wts

    @pl.when(i < n_ctx_tiles)
    def _():
        body(pool_p, attn_p, x_p)

    @pl.when(i >= n_ctx_tiles)
    def _():
        body(pool_s, attn_s, x_s)


def _out_projection(ctx, lat, rows, mod3, g_ffn, w_out_b, rw_split, rb):
    tm = TOKEN_TILE
    n_ctx = ctx[2].shape[0] // tm
    n_lat = lat[2].shape[0] // tm
    t_all = (n_ctx + n_lat) * tm

    def ctx_map(i, r):
        return (jnp.minimum(i, n_ctx - 1), 0)

    def lat_map(i, r):
        return (jnp.maximum(i - n_ctx, 0), 0)

    in_specs = [
        pl.BlockSpec((tm, POOL_WIDTH), ctx_map),
        pl.BlockSpec((tm, ATTN_WIDTH), ctx_map),
        pl.BlockSpec((tm, D_MODEL), ctx_map),
        pl.BlockSpec((tm, POOL_WIDTH), lat_map),
        pl.BlockSpec((tm, ATTN_WIDTH), lat_map),
        pl.BlockSpec((tm, D_MODEL), lat_map),
        pl.BlockSpec((1, 6, D_MODEL), lambda i, r: (r[i], 0, 0)),
        pl.BlockSpec((1, D_MODEL), lambda i, r: (0, 0)),
        pl.BlockSpec((D_MODEL, D_MODEL), lambda i, r: (0, 0), pipeline_mode=pl.Buffered(1)),
        pl.BlockSpec((D_MODEL, 2 * ROUTER_LANES), lambda i, r: (0, 0)),
        pl.BlockSpec((1, ROUTER_LANES), lambda i, r: (0, 0)),
    ]
    out_shape = (
        jax.ShapeDtypeStruct((t_all, D_MODEL), F32),
        jax.ShapeDtypeStruct((t_all * ROW_TILE, LANES), jnp.uint32),
        jax.ShapeDtypeStruct((t_all, ROUTER_LANES), jnp.int32),
        jax.ShapeDtypeStruct((t_all, ROUTER_LANES), F32),
    )
    out_specs = (
        pl.BlockSpec((tm, D_MODEL), lambda i, r: (i, 0)),
        pl.BlockSpec((tm * ROW_TILE, LANES), lambda i, r: (i, 0)),
        pl.BlockSpec((tm, ROUTER_LANES), lambda i, r: (i, 0)),
        pl.BlockSpec((tm, ROUTER_LANES), lambda i, r: (i, 0)),
    )
    return pl.pallas_call(
        functools.partial(_outproj_kernel, n_ctx_tiles=n_ctx),
        grid_spec=pltpu.PrefetchScalarGridSpec(
            num_scalar_prefetch=1, grid=(n_ctx + n_lat,), in_specs=in_specs, out_specs=out_specs),
        out_shape=out_shape,
        compiler_params=_params(1),
        name="out_projection",
    )(rows, *ctx, *lat, mod3, g_ffn, w_out_b, rw_split, rb)


def _gather_rows(idx_ref, base, n_rows, src_hbm, dst, sem, *, unrolled, both_queues=False, dst_row0=0):
    def one(r, priority):
        src_row = pl.multiple_of(idx_ref[base + r], ROW_TILE)
        pltpu.make_async_copy(
            src_hbm.at[pl.ds(src_row, ROW_TILE)], dst.at[pl.ds((dst_row0 + r) * ROW_TILE, ROW_TILE)],
            sem).start(priority=priority)

    if unrolled:
        for r in range(n_rows):
            one(r, r % 2 if both_queues else 0)
    else:
        def body(r, carry):
            one(r, 0)
            return carry
        lax.fori_loop(0, n_rows, body, 0, unroll=8)


def _wait_rows(src_hbm, dst, sem):
    pltpu.make_async_copy(src_hbm.at[pl.ds(0, dst.shape[0])], dst, sem).wait()


def _late_zero(x):
    u = lax.bitcast_convert_type(x, jnp.uint32)
    return ((u >> 16) >> 16).astype(jnp.int32)[0, 0]


def _moe_kernel(be_ref, ne_ref, nu_ref, tok_ref, h_hbm, wg_hbm, wu_hbm, wd_hbm, y_ref,
                xg0, xg1, xg2, sg, su, sd, wgb, wub, wdb, sem, wsem):
    b = pl.program_id(0)
    n_used = nu_ref[0]
    last = n_used - 1
    active = b < n_used
    bufs = (xg0, xg1, xg2)
    n_buf = len(bufs)

    def weight_copies(e):
        return (pltpu.make_async_copy(wg_hbm.at[e], sg, wsem.at[0]),
                pltpu.make_async_copy(wu_hbm.at[e], su, wsem.at[1]),
                pltpu.make_async_copy(wd_hbm.at[e], sd, wsem.at[2]))

    @pl.when(b == 0)
    def _():
        for cp in weight_copies(be_ref[0]):
            cp.start(priority=1)
        for i in range(n_buf - 1):
            _gather_rows(tok_ref, jnp.minimum(i, last) * EXPERT_ROWS, EXPERT_ROWS, h_hbm, bufs[i],
                         sem.at[i], unrolled=False)

    new_expert = (b == 0) | (be_ref[b] != be_ref[jnp.maximum(b - 1, 0)])

    @pl.when(active & new_expert)
    def _():
        for cp in weight_copies(be_ref[b]):
            cp.wait()
        wgb[...] = sg[...].astype(BF16)
        wub[...] = su[...].astype(BF16)
        wdb[...] = sd[...].astype(BF16)

        @pl.when(ne_ref[b] >= 0)
        def _():
            for cp in weight_copies(ne_ref[b]):
                cp.start(priority=1)

    def step(slot):
        cur = bufs[slot]
        ahead = bufs[(slot + n_buf - 1) % n_buf]
        sem_ahead = sem.at[(slot + n_buf - 1) % n_buf]
        _wait_rows(h_hbm, cur, sem.at[slot])
        base = jnp.minimum(b + n_buf - 1, last) * EXPERT_ROWS
        per = EXPERT_ROWS // GATHER_GROUPS
        group = iter(range(GATHER_GROUPS))

        def issue(zero):
            g = next(group)
            _gather_rows(tok_ref, base + g * per + zero, per, h_hbm, ahead, sem_ahead, unrolled=True,
                         both_queues=True, dst_row0=g * per)

        def corner(x, r, c):
            return _late_zero(x[r:r + 1, c:c + 1])

        bm, de = EXPERT_ROWS, D_EXPERT
        lo, hi = _unpack_bf16_pairs(_load_row_tiles(cur, bm))
        lo = lo.astype(BF16)
        hi = hi.astype(BF16)
        def issue_along(x, n):
            rows, cols = x.shape
            for i in range(n):
                issue(corner(x, (i + 1) * rows // n - 1, (i + 1) * cols // n - 1))

        n_each = GATHER_GROUPS // 8
        issue(0)
        a1 = _dot(lo, wgb[0:PACKED, :])
        issue_along(a1, n_each)
        a = a1 + _dot(hi, wgb[PACKED:D_MODEL, :])
        issue_along(a, n_each)
        u1 = _dot(lo, wub[0:PACKED, :])
        issue_along(u1, n_each)
        u = u1 + _dot(hi, wub[PACKED:D_MODEL, :])
        issue_along(u, n_each)
        act = (a / (1.0 + jnp.exp(-a)) * u).astype(BF16)
        y = _dot(act, wdb[...])
        issue_along(y, GATHER_GROUPS - 1 - 4 * n_each)
        _store_row_tiles(y_ref, _pack_bf16_pairs(y))

        @pl.when(b == last)
        def _():
            for i in range(1, n_buf):
                _wait_rows(h_hbm, bufs[(slot + i) % n_buf], sem.at[(slot + i) % n_buf])

    for slot in range(n_buf):
        @pl.when(active & (b % n_buf == slot))
        def _(slot=slot):
            step(slot)

    @pl.when(b >= n_used)
    def _():
        y_ref[...] = jnp.zeros_like(y_ref)


def _experts(block_expert, next_expert, n_used, buf_tok, h2_packed, w_gate, w_up, w_down):
    cap = buf_tok.shape[0]
    bm = EXPERT_ROWS
    xg = pltpu.VMEM((bm * ROW_TILE, LANES), jnp.uint32)
    return pl.pallas_call(
        _moe_kernel,
        grid_spec=pltpu.PrefetchScalarGridSpec(
            num_scalar_prefetch=4,
            grid=(cap // bm,),
            in_specs=[pl.BlockSpec(memory_space=pl.ANY)] * 4,
            out_specs=pl.BlockSpec((bm * ROW_TILE, LANES), lambda b, *_: (b, 0)),
            scratch_shapes=[
                xg, xg, xg,
                pltpu.VMEM((D_MODEL, D_EXPERT), F32), pltpu.VMEM((D_MODEL, D_EXPERT), F32),
                pltpu.VMEM((D_EXPERT, D_MODEL), F32),
                pltpu.VMEM((D_MODEL, D_EXPERT), BF16), pltpu.VMEM((D_MODEL, D_EXPERT), BF16),
                pltpu.VMEM((D_EXPERT, D_MODEL), BF16),
                pltpu.SemaphoreType.DMA((3,)), pltpu.SemaphoreType.DMA((3,)),
            ],
        ),
        out_shape=jax.ShapeDtypeStruct((cap * ROW_TILE, LANES), jnp.uint32),
        compiler_params=_params(1),
        name="experts",
    )(block_expert, next_expert, n_used, buf_tok, h2_packed, w_gate, w_up, w_down)


def _combine_kernel(slot_ref, rows_ref, y_hbm, x1_ref, wts_ref, mod_ref, g_ref, op_ref, os_ref, yg0, yg1, sem,
                    *, n_ctx_tiles, n_tiles):
    del rows_ref
    tm = TOKEN_TILE
    i = pl.program_id(0)
    n_slots = slot_ref.shape[0] // TOP_K

    def start(tile, buf, s, unrolled):
        for k in range(TOP_K):
            _gather_rows(slot_ref, k * n_slots + tile * tm, tm, y_hbm, buf.at[k], s, unrolled=unrolled,
                         both_queues=True)

    @pl.when(i == 0)
    def _():
        start(0, yg0, sem.at[0], False)

    def step(cur, nxt, sem_cur, sem_nxt):
        for k in range(TOP_K):
            _wait_rows(y_hbm, cur.at[k], sem_cur)
        start(jnp.minimum(i + 1, n_tiles - 1), nxt, sem_nxt, True)
        w = wts_ref[...]
        lo0, hi0 = _unpack_bf16_pairs(_load_row_tiles(cur.at[0], tm))
        lo1, hi1 = _unpack_bf16_pairs(_load_row_tiles(cur.at[1], tm))
        ffn = jnp.concatenate([lo0 * w[:, 0:1] + lo1 * w[:, 1:2], hi0 * w[:, 0:1] + hi1 * w[:, 1:2]], axis=-1)
        x2 = x1_ref[...] + mod_ref[0][5:6] * ffn
        out = _rms(x2) * g_ref[...]

        @pl.when(i < n_ctx_tiles)
        def _():
            op_ref[...] = out

        @pl.when(i >= n_ctx_tiles)
        def _():
            os_ref[...] = out

        @pl.when(i == n_tiles - 1)
        def _():
            for k in range(TOP_K):
                _wait_rows(y_hbm, nxt.at[k], sem_nxt)

    @pl.when(i % 2 == 0)
    def _():
        step(yg0, yg1, sem.at[0], sem.at[1])

    @pl.when(i % 2 == 1)
    def _():
        step(yg1, yg0, sem.at[1], sem.at[0])


def _combine(slots, rows, y_buf, x1, wts, mod3, g_final, t_ctx):
    t = x1.shape[0]
    tm = TOKEN_TILE
    n_tiles = t // tm
    n_ctx = t_ctx // tm
    return pl.pallas_call(
        functools.partial(_combine_kernel, n_ctx_tiles=n_ctx, n_tiles=n_tiles),
        grid_spec=pltpu.PrefetchScalarGridSpec(
            num_scalar_prefetch=2,
            grid=(n_tiles,),
            in_specs=[
                pl.BlockSpec(memory_space=pl.ANY),
                pl.BlockSpec((tm, D_MODEL), lambda i, s, r: (i, 0)),
                pl.BlockSpec((tm, ROUTER_LANES), lambda i, s, r: (i, 0)),
                pl.BlockSpec((1, 6, D_MODEL), lambda i, s, r: (r[i], 0, 0)),
                pl.BlockSpec((1, D_MODEL), lambda i, s, r: (0, 0)),
            ],
            out_specs=(
                pl.BlockSpec((tm, D_MODEL), lambda i, s, r: (jnp.minimum(i, n_ctx - 1), 0)),
                pl.BlockSpec((tm, D_MODEL), lambda i, s, r: (jnp.maximum(i - n_ctx, 0), 0)),
            ),
            scratch_shapes=[pltpu.VMEM((TOP_K, tm * ROW_TILE, LANES), jnp.uint32),
                            pltpu.VMEM((TOP_K, tm * ROW_TILE, LANES), jnp.uint32),
                            pltpu.SemaphoreType.DMA((2,))],
        ),
        out_shape=(jax.ShapeDtypeStruct((t_ctx, D_MODEL), F32),
                   jax.ShapeDtypeStruct((t - t_ctx, D_MODEL), F32)),
        compiler_params=_params(1),
        name="combine",
    )(slots, rows, y_buf, x1, wts, mod3, g_final)


def _rope_tables(n_tokens):
    rows = n_tokens // GRID_W
    row = jnp.repeat(jnp.arange(rows, dtype=F32), GRID_W)
    col = jnp.tile(jnp.arange(GRID_W, dtype=F32), rows)
    half = ROPE_AXIS_DIM // 2
    inv_freq = ROPE_BASE ** (-jnp.arange(half, dtype=F32) / half)
    ar = row[:, None] * inv_freq
    ac = col[:, None] * inv_freq
    cos = jnp.concatenate([jnp.cos(ar), jnp.cos(ar), jnp.cos(ac), jnp.cos(ac)], axis=-1)
    sin = jnp.concatenate([-jnp.sin(ar), jnp.sin(ar), -jnp.sin(ac), jnp.sin(ac)], axis=-1)
    return cos, sin


def _invert_kernel(dest_ref, gap_lo_ref, gap_hi_ref, out_ref):
    group = 16
    cap = out_ref.shape[0]

    def clear_gap(e, carry):
        lo = gap_lo_ref[e]

        def clear(c, inner):
            for k in range(group):
                out_ref[jnp.minimum(lo + c * group + k, cap - 1)] = 0
            return inner
        lax.fori_loop(0, (gap_hi_ref[e] - lo + group - 1) // group, clear, 0)
        return carry
    lax.fori_loop(0, gap_lo_ref.shape[0], clear_gap, 0)

    def place(i, carry):
        a0 = i * group
        tile0 = i * (group // TOP_K * ROW_TILE)
        rows = [dest_ref[a0 + k] for k in range(group)]
        for k in range(group):
            out_ref[rows[k]] = tile0 + (k // TOP_K) * ROW_TILE
        return carry
    lax.fori_loop(0, dest_ref.shape[0] // group, place, 0)


def _invert_slots(dest, gap_lo, gap_hi, cap):
    return pl.pallas_call(
        _invert_kernel,
        grid_spec=pltpu.PrefetchScalarGridSpec(
            num_scalar_prefetch=3, grid=(1,), in_specs=[],
            out_specs=pl.BlockSpec(memory_space=pltpu.SMEM)),
        out_shape=jax.ShapeDtypeStruct((cap,), jnp.int32),
        compiler_params=_params(1),
        name="invert_slots",
    )(dest, gap_lo, gap_hi)


def _dispatch_tables(ids):
    t = ids.shape[0]
    bm = EXPERT_ROWS
    flat_e = ids.reshape(t * TOP_K)
    onehot = (flat_e[:, None] == jnp.arange(N_EXPERTS, dtype=jnp.int32)[None, :]).astype(jnp.int32)
    running = jnp.cumsum(onehot, axis=0)
    rank = jnp.sum(running * onehot, axis=1) - 1
    counts = running[-1]
    padded = (counts + bm - 1) // bm * bm
    pad_end = jnp.cumsum(padded)
    pad_start = pad_end - padded
    dest = (jnp.sum(pad_start[None, :] * onehot, axis=1) + rank).astype(jnp.int32)
    n_blocks = (t * TOP_K + N_EXPERTS * (bm - 1) + bm - 1) // bm
    cap = n_blocks * bm
    gap_lo = jnp.concatenate([pad_start + counts, pad_end[-1:]]).astype(jnp.int32)
    gap_hi = jnp.concatenate([pad_end, jnp.full((1,), cap, pad_end.dtype)]).astype(jnp.int32)
    buf_tok = _invert_slots(dest, gap_lo, gap_hi, cap)
    n_used = (pad_end[-1] // bm).astype(jnp.int32)
    blk = jnp.arange(n_blocks, dtype=jnp.int32)
    blk = jnp.minimum(blk, n_used - 1)
    block_expert = jnp.sum((pad_end[None, :] <= (blk * bm)[:, None]).astype(jnp.int32), axis=1)
    block_expert = jnp.minimum(block_expert, N_EXPERTS - 1)
    e_ids = jnp.arange(N_EXPERTS, dtype=jnp.int32)
    later = (e_ids[None, :] > e_ids[:, None]) & (counts[None, :] > 0)
    next_of = jnp.min(jnp.where(later, e_ids[None, :], N_EXPERTS), axis=1)
    next_of = jnp.where(next_of == N_EXPERTS, -1, next_of)
    next_expert = jnp.sum(next_of[None, :] * (block_expert[:, None] == e_ids[None, :]), axis=1)
    slots = dest.reshape(t, TOP_K).T.reshape(TOP_K * t)
    return (block_expert.astype(jnp.int32), next_expert.astype(jnp.int32), n_used.reshape(1),
            buf_tok, slots * ROW_TILE)


def kernel(x_prompt, x_sample, cache_k, cache_v, c, c_ctx, w_mod, b_mod, norm_mix_g, norm_ffn_g, w_in,
           pool_w, pool_scale, attn_sink, out_norm_pool_g, out_norm_attn_g, w_out, router_group_w,
           router_group_b, router_expert_w, router_expert_b, w_gate, w_up, w_down, final_norm_g):
    depth = w_mod.shape[0]
    assert depth == 1, "single trunk layer"
    bp, lp, _ = x_prompt.shape
    bs, ls, _ = x_sample.shape
    tp, ts = bp * lp, bs * ls
    tm = TOKEN_TILE
    l = 0

    cond8 = jnp.zeros((8, D_MODEL), F32).at[:bs].set(c).at[bs].set(c_ctx)
    mod3 = _modulation(cond8, w_mod[l], b_mod[l]).reshape(8, 6, D_MODEL)
    rows_p = jnp.full((tp // tm,), bs, jnp.int32)
    rows_s = jnp.arange(ts // tm, dtype=jnp.int32) // (ls // tm)

    w_in_b = w_in[l].astype(BF16)
    w_out_b = w_out[l].astype(BF16)
    pool_w_b = pool_w[l].astype(BF16)
    g_mix = norm_mix_g[l].reshape(1, D_MODEL)
    g_ffn = norm_ffn_g[l].reshape(1, D_MODEL)
    g_pool = out_norm_pool_g[l].reshape(1, POOL_WIDTH)
    g_attn = out_norm_attn_g[l].reshape(1, ATTN_WIDTH)
    p_scale = pool_scale[l].reshape(1, POOL_WIDTH)
    sink_b = jnp.broadcast_to(attn_sink[l][:, None], (N_HEADS, LANES))
    g_final = final_norm_g.reshape(1, D_MODEL)

    rw = jnp.concatenate(
        [router_group_w[l], jnp.transpose(router_expert_w[l], (1, 0, 2)).reshape(D_MODEL, N_EXPERTS)], axis=1)
    rw = jnp.pad(rw, ((0, 0), (0, ROUTER_LANES - rw.shape[1])))
    rw_hi = rw.astype(BF16)
    rw_split = jnp.concatenate([rw_hi, (rw - rw_hi.astype(F32)).astype(BF16)], axis=1)
    rb = jnp.concatenate([router_group_b[l], router_expert_b[l].reshape(N_EXPERTS)])
    rb = jnp.pad(rb, (0, ROUTER_LANES - rb.shape[0])).reshape(1, ROUTER_LANES)

    xp = x_prompt.reshape(tp, D_MODEL)
    xs = x_sample.reshape(ts, D_MODEL)

    up_p, q_p, k_p, v_p = _in_projection(xp, rows_p, mod3, g_mix, w_in_b, None, F32)
    pool_p = _pool_mixer(up_p.reshape(bp, lp, POOL_WIDTH), pool_w_b, p_scale, g_pool).reshape(tp, POOL_WIDTH)
    attn_p = _context_attention(q_p, k_p, v_p, sink_b, g_attn, lp)
    up_s, q_s, k_s, v_s = _in_projection(xs, rows_s, mod3, g_mix, w_in_b, _rope_tables(ls), BF16)
    pool_s = _pool_mixer(up_s.reshape(bs, ls, POOL_WIDTH), pool_w_b, p_scale, g_pool).reshape(ts, POOL_WIDTH)
    attn_s = _latent_attention(
        q_s, k_s.reshape(bs, ls, KV_WIDTH), v_s.reshape(bs, ls, KV_WIDTH),
        cache_k[:, l].reshape(bs, -1, KV_WIDTH), cache_v[:, l].reshape(bs, -1, KV_WIDTH), sink_b, g_attn)

    rows_all = jnp.concatenate([rows_p, rows_s])
    x1_all, h2_all, ids_all, wts_all = _out_projection(
        (pool_p, attn_p, xp), (pool_s, attn_s, xs), rows_all, mod3, g_ffn, w_out_b, rw_split, rb)

    block_expert, next_expert, n_used, buf_tok, slots = _dispatch_tables(ids_all[:, :TOP_K])
    y_buf = _experts(block_expert, next_expert, n_used, buf_tok, h2_all, w_gate[l], w_up[l], w_down[l])

    y_p, y_s = _combine(slots, rows_all, y_buf, x1_all, wts_all, mod3, g_final, tp)

    kv_shape = (bp, 1, lp, N_KV_HEADS, HEAD_DIM)
    return (y_p.reshape(bp, lp, D_MODEL), y_s.reshape(bs, ls, D_MODEL),
            k_p.reshape(kv_shape), v_p.reshape(kv_shape))
```

```python
import functools

import jax
import jax.numpy as jnp
from jax import lax
from jax.experimental import pallas as pl
from jax.experimental.pallas import tpu as pltpu

F32 = jnp.float32
BF16 = jnp.bfloat16

D_MODEL = 2048
GRID_W = 64
HEAD_DIM = 128
ATTN_WIDTH = 1024
POOL_WIDTH = 1024
N_HEADS = 8
N_KV_HEADS = 2
Q_PER_KV = 4
KV_WIDTH = 256
IN_WIDTH = 2560
WINDOW = 128
POOL_WINDOWS = (2, 4, 8, 16)
POOL_GROUP_DIM = 256
ROPE_BASE = 10000.0
ROPE_AXIS_DIM = 64
N_EXPERT_GROUPS = 4
EXPERTS_PER_GROUP = 8
N_EXPERTS = 32
TOP_K = 2
D_EXPERT = 512
NORM_EPS = 1e-6
NEG_INF = -1e30
ATTN_SCALE = HEAD_DIM ** -0.5
LOG2E = 1.4426950408889634
Q_SCALE = ATTN_SCALE * LOG2E

LANES = 128
Q_BLOCKS_PER_STEP = 4
CTX_SEQS_PER_STEP = 2
VMEM_LIMIT = 48 * 1024 * 1024
PACKED = D_MODEL // 2
ROW_TILE = 8
assert PACKED == ROW_TILE * LANES, "a packed row must fill exactly one tile"
TOKEN_TILE = 256
IN_TILE = 512
POOL_TILES_PER_STEP = 2
POOL_HALO = 16
Q_BLOCK = 128
EXPERT_ROWS = 256
GATHER_GROUPS = 64
ROUTER_LANES = 128


def _params(n_grid):
    return pltpu.CompilerParams(
        dimension_semantics=("arbitrary",) * n_grid, vmem_limit_bytes=VMEM_LIMIT)


def _rms(x):
    return x * lax.rsqrt(jnp.mean(x * x, axis=-1, keepdims=True) + NORM_EPS)


def _dot(a, b):
    return jnp.dot(a, b, preferred_element_type=F32)


def _dot_nt(a, b):
    return lax.dot_general(a, b, (((1,), (1,)), ((), ())), preferred_element_type=F32)


def _pack_bf16_pairs(x):
    c = x.shape[1] // 2
    lo = lax.bitcast_convert_type(x[:, :c].astype(BF16).astype(F32), jnp.uint32)
    hi = lax.bitcast_convert_type(x[:, c:].astype(BF16).astype(F32), jnp.uint32)
    return hi | (lo >> 16)


def _store_row_tiles(ref, x):
    n = x.shape[0]
    for c in range(PACKED // LANES):
        ref[pl.ds(c, n, stride=ROW_TILE), :] = x[:, c * LANES:(c + 1) * LANES]


def _load_row_tiles(ref, n):
    return jnp.concatenate(
        [ref[pl.ds(c, n, stride=ROW_TILE), :] for c in range(PACKED // LANES)], axis=-1)


def _unpack_bf16_pairs(u):
    lo = lax.bitcast_convert_type(u << 16, F32)
    hi = lax.bitcast_convert_type(u & jnp.uint32(0xFFFF0000), F32)
    return lo, hi


def _mod_kernel(cond_ref, w_ref, b_ref, o_ref):
    c = cond_ref[...]
    s = c / (1.0 + jnp.exp(-c))
    o_ref[...] = _dot(s.astype(BF16), w_ref[...].astype(BF16)) + b_ref[...]


def _modulation(cond8, w_mod, b_mod):
    n = w_mod.shape[1]
    tn = 1024
    return pl.pallas_call(
        _mod_kernel,
        grid=(n // tn,),
        in_specs=[
            pl.BlockSpec((8, D_MODEL), lambda j: (0, 0)),
            pl.BlockSpec((D_MODEL, tn), lambda j: (0, j)),
            pl.BlockSpec((1, tn), lambda j: (0, j)),
        ],
        out_specs=pl.BlockSpec((8, tn), lambda j: (0, j)),
        out_shape=jax.ShapeDtypeStruct((8, n), F32),
        compiler_params=_params(1),
        name="modulation",
    )(cond8, w_mod, b_mod.reshape(1, n))


def _rope(x, cos, sin_signed):
    lane = lax.broadcasted_iota(jnp.int32, x.shape, 1)
    partner = jnp.where((lane & 63) < 32,
                        pltpu.roll(x, HEAD_DIM - 32, axis=1),
                        pltpu.roll(x, 32, axis=1))
    return x * cos + partner * sin_signed


def _inproj_kernel(rows_ref, x_ref, mod_ref, g_ref, w_ref, *rest, rope):
    del rows_ref
    if rope:
        cos_ref, sin_ref, up_ref, q_ref, k_ref, v_ref = rest
    else:
        up_ref, q_ref, k_ref, v_ref = rest
    m = mod_ref[0]
    h = _rms(x_ref[...]) * g_ref[...]
    hb = (h * (1.0 + m[1:2]) + m[0:1]).astype(BF16)
    up_ref[...] = _dot(hb, w_ref[:, 0:POOL_WIDTH]).astype(BF16)
    q = _dot(hb, w_ref[:, POOL_WIDTH:POOL_WIDTH + ATTN_WIDTH])
    k = _dot(hb, w_ref[:, POOL_WIDTH + ATTN_WIDTH:IN_WIDTH - KV_WIDTH])
    v = _dot(hb, w_ref[:, IN_WIDTH - KV_WIDTH:IN_WIDTH])
    if rope:
        cos = cos_ref[...]
        sin = sin_ref[...]
        for hd in range(N_HEADS):
            sl = slice(hd * HEAD_DIM, (hd + 1) * HEAD_DIM)
            q_ref[:, sl] = (_rope(q[:, sl], cos, sin) * Q_SCALE).astype(BF16)
        for hd in range(N_KV_HEADS):
            sl = slice(hd * HEAD_DIM, (hd + 1) * HEAD_DIM)
            k_ref[:, sl] = _rope(k[:, sl], cos, sin).astype(k_ref.dtype)
        v_ref[...] = v.astype(v_ref.dtype)
    else:
        q_ref[...] = (q * Q_SCALE).astype(BF16)
        _store_kv_heads(k_ref, k)
        _store_kv_heads(v_ref, v)


def _store_kv_heads(ref, x):
    n_seq, _, seq, _, _ = ref.shape
    for s in range(n_seq):
        for hd in range(N_KV_HEADS):
            ref[s, 0, :, hd, :] = x[s * seq:(s + 1) * seq, hd * HEAD_DIM:(hd + 1) * HEAD_DIM]


def _in_projection(x, rows, mod3, g_mix, w_in_b, rope_tabs, ctx_seq_len=None):
    t = x.shape[0]
    tm = IN_TILE
    per = IN_TILE // TOKEN_TILE
    rope = rope_tabs is not None
    in_specs = [
        pl.BlockSpec((tm, D_MODEL), lambda i, r: (i, 0)),
        pl.BlockSpec((1, 6, D_MODEL), lambda i, r: (r[i * per], 0, 0)),
        pl.BlockSpec((1, D_MODEL), lambda i, r: (0, 0)),
        pl.BlockSpec((D_MODEL, IN_WIDTH), lambda i, r: (0, 0), pipeline_mode=pl.Buffered(1)),
    ]
    args = [x, mod3, g_mix, w_in_b]
    if rope:
        seq_tiles = rope_tabs[0].shape[0] // tm
        for tab in rope_tabs:
            in_specs.append(pl.BlockSpec((tm, HEAD_DIM), lambda i, r: (i % seq_tiles, 0)))
            args.append(tab)
    if rope:
        kv_shape = jax.ShapeDtypeStruct((t, KV_WIDTH), BF16)
        kv_spec = pl.BlockSpec((tm, KV_WIDTH), lambda i, r: (i, 0))
    else:
        n_seq = tm // ctx_seq_len
        kv_shape = jax.ShapeDtypeStruct((t // ctx_seq_len, 1, ctx_seq_len, N_KV_HEADS, HEAD_DIM), F32)
        kv_spec = pl.BlockSpec((n_seq, 1, ctx_seq_len, N_KV_HEADS, HEAD_DIM), lambda i, r: (i, 0, 0, 0, 0))
    out_shape = (
        jax.ShapeDtypeStruct((t, POOL_WIDTH), BF16),
        jax.ShapeDtypeStruct((t, ATTN_WIDTH), BF16),
        kv_shape,
        kv_shape,
    )
    out_specs = (
        pl.BlockSpec((tm, POOL_WIDTH), lambda i, r: (i, 0)),
        pl.BlockSpec((tm, ATTN_WIDTH), lambda i, r: (i, 0)),
        kv_spec,
        kv_spec,
    )
    return pl.pallas_call(
        functools.partial(_inproj_kernel, rope=rope),
        grid_spec=pltpu.PrefetchScalarGridSpec(
            num_scalar_prefetch=1, grid=(t // tm,), in_specs=in_specs, out_specs=out_specs),
        out_shape=out_shape,
        compiler_params=_params(1),
        name="in_projection_rope" if rope else "in_projection",
    )(rows, *args)


def _pool_kernel(u_ref, pw_ref, ps_ref, g_ref, o_ref, *, seq_len):
    tq = TOKEN_TILE
    win = min(seq_len, tq + 2 * POOL_HALO)
    for sub in range(o_ref.shape[1] // tq):
        t0 = pl.program_id(1) * o_ref.shape[1] + sub * tq
        src0 = pl.multiple_of(jnp.clip(t0 - POOL_HALO, 0, seq_len - win), POOL_HALO)
        u = u_ref[0, pl.ds(src0, win), :]
        t = t0 + lax.broadcasted_iota(jnp.int32, (tq, win), 0)
        j = src0 + lax.broadcasted_iota(jnp.int32, (tq, win), 1)
        tc = t0 + lax.broadcasted_iota(jnp.int32, (tq, 1), 0)
        ys = []
        for gi, w in enumerate(POOL_WINDOWS):
            lo = jnp.maximum(t - w // 2, 0)
            hi = jnp.minimum(t + (w - w // 2), seq_len)
            cnt = (hi - lo).astype(F32)
            a = jnp.where((j >= lo) & (j < hi), 1.0, 0.0) - jnp.where(j == t, cnt, 0.0)
            cnt_col = (jnp.minimum(tc + (w - w // 2), seq_len) - jnp.maximum(tc - w // 2, 0)).astype(F32)
            sl = slice(gi * POOL_GROUP_DIM, (gi + 1) * POOL_GROUP_DIM)
            d = _dot(a.astype(BF16), u[:, sl]) / cnt_col
            ys.append(_dot(d.astype(BF16), pw_ref[gi]))
        y = jnp.concatenate(ys, axis=-1) * ps_ref[...]
        o_ref[0, sub * tq:(sub + 1) * tq, :] = (_rms(y) * g_ref[...]).astype(BF16)


def _pool_mixer(up, pool_w_b, pool_scale, g_out_pool):
    b, seq_len, _ = up.shape
    tq = TOKEN_TILE * min(POOL_TILES_PER_STEP, seq_len // TOKEN_TILE)
    return pl.pallas_call(
        functools.partial(_pool_kernel, seq_len=seq_len),
        grid=(b, seq_len // tq),
        in_specs=[
            pl.BlockSpec((1, seq_len, POOL_WIDTH), lambda bi, i: (bi, 0, 0)),
            pl.BlockSpec((len(POOL_WINDOWS), POOL_GROUP_DIM, POOL_GROUP_DIM), lambda bi, i: (0, 0, 0)),
            pl.BlockSpec((1, POOL_WIDTH), lambda bi, i: (0, 0)),
            pl.BlockSpec((1, POOL_WIDTH), lambda bi, i: (0, 0)),
        ],
        out_specs=pl.BlockSpec((1, tq, POOL_WIDTH), lambda bi, i: (bi, i, 0)),
        out_shape=jax.ShapeDtypeStruct((b, seq_len, POOL_WIDTH), BF16),
        compiler_params=_params(2),
        name=f"pool_mixer_{seq_len}",
    )(up, pool_w_b, pool_scale, g_out_pool)


def _stack_heads(q, kh, rows):
    return jnp.concatenate(
        [q[:, (kh * Q_PER_KV + g) * HEAD_DIM:(kh * Q_PER_KV + g + 1) * HEAD_DIM]
         for g in range(Q_PER_KV)], axis=0)


def _sink_column(sink_ref, kh, rows):
    return jnp.concatenate(
        [jnp.broadcast_to(sink_ref[kh * Q_PER_KV + g:kh * Q_PER_KV + g + 1, 0:1] * LOG2E, (rows, 1))
         for g in range(Q_PER_KV)], axis=0)


def _attend(q4, key_sets, value_sets, sk, band=None):
    s = [_dot_nt(q4, k) for k in key_sets]
    if band is not None:
        s[0] = jnp.where(band, s[0], NEG_INF)
    m = sk
    for si in s:
        m = jnp.maximum(m, jnp.max(si, axis=-1, keepdims=True))
    acc = None
    for si, v in zip(s, value_sets):
        v1 = jnp.concatenate([v, jnp.ones_like(v)], axis=-1)
        part = _dot(jnp.exp2(si - m).astype(BF16), v1)
        acc = part if acc is None else acc + part
    den = acc[:, HEAD_DIM:] + jnp.exp2(sk - m)
    return acc[:, :HEAD_DIM] / den


def _ctx_attn_kernel(q_ref, k_ref, v_ref, sink_ref, g_ref, o_ref, *, seq_len):
    rows = seq_len
    for j in range(q_ref.shape[0] // rows):
        rs = slice(j * rows, (j + 1) * rows)
        q = q_ref[rs, :]
        heads = [None] * N_HEADS
        for kh in range(N_KV_HEADS):
            o = _attend(_stack_heads(q, kh, rows), [k_ref[j, 0, :, kh, :].astype(BF16)],
                        [v_ref[j, 0, :, kh, :].astype(BF16)], _sink_column(sink_ref, kh, rows))
            for g in range(Q_PER_KV):
                heads[kh * Q_PER_KV + g] = o[g * rows:(g + 1) * rows]
        y = jnp.concatenate(heads, axis=-1)
        o_ref[rs, :] = (_rms(y) * g_ref[...]).astype(BF16)


def _context_attention(q, k, v, sink_b, g_out_attn, seq_len):
    t = q.shape[0]
    blk = seq_len * CTX_SEQS_PER_STEP
    return pl.pallas_call(
        functools.partial(_ctx_attn_kernel, seq_len=seq_len),
        grid=(t // blk,),
        in_specs=[
            pl.BlockSpec((blk, ATTN_WIDTH), lambda b: (b, 0)),
            pl.BlockSpec((CTX_SEQS_PER_STEP, 1, seq_len, N_KV_HEADS, HEAD_DIM), lambda b: (b, 0, 0, 0, 0)),
            pl.BlockSpec((CTX_SEQS_PER_STEP, 1, seq_len, N_KV_HEADS, HEAD_DIM), lambda b: (b, 0, 0, 0, 0)),
            pl.BlockSpec((N_HEADS, LANES), lambda b: (0, 0)),
            pl.BlockSpec((1, ATTN_WIDTH), lambda b: (0, 0)),
        ],
        out_specs=pl.BlockSpec((blk, ATTN_WIDTH), lambda b: (b, 0)),
        out_shape=jax.ShapeDtypeStruct((t, ATTN_WIDTH), BF16),
        compiler_params=_params(1),
        name="context_attention",
    )(q, k, v, sink_b, g_out_attn)


def _lat_attn_kernel(q_ref, k_ref, v_ref, ck_ref, cv_ref, sink_ref, g_ref, o_ref, *, seq_len):
    rows = Q_BLOCK
    span = 3 * Q_BLOCK
    shape = (Q_PER_KV * rows, span)
    row_in_block = lax.broadcasted_iota(jnp.int32, shape, 0) & (rows - 1)
    col = lax.broadcasted_iota(jnp.int32, shape, 1)
    ctx_k = [ck_ref[0, 0, :, kh, :].astype(BF16) for kh in range(N_KV_HEADS)]
    ctx_v = [cv_ref[0, 0, :, kh, :].astype(BF16) for kh in range(N_KV_HEADS)]
    for j in range(Q_BLOCKS_PER_STEP):
        q0 = (pl.program_id(1) * Q_BLOCKS_PER_STEP + j) * rows
        start = pl.multiple_of(jnp.clip(q0 - Q_BLOCK, 0, seq_len - span), Q_BLOCK)
        q = q_ref[j * rows:(j + 1) * rows, :]
        kl = k_ref[0, pl.ds(start, span), :]
        vl = v_ref[0, pl.ds(start, span), :]
        band = jnp.abs(row_in_block - col + (q0 - start)) <= WINDOW
        heads = [None] * N_HEADS
        for kh in range(N_KV_HEADS):
            sl = slice(kh * HEAD_DIM, (kh + 1) * HEAD_DIM)
            o = _attend(_stack_heads(q, kh, rows),
                        [kl[:, sl], ctx_k[kh]], [vl[:, sl], ctx_v[kh]],
                        _sink_column(sink_ref, kh, rows), band)
            for g in range(Q_PER_KV):
                heads[kh * Q_PER_KV + g] = o[g * rows:(g + 1) * rows]
        y = jnp.concatenate(heads, axis=-1)
        o_ref[j * rows:(j + 1) * rows, :] = (_rms(y) * g_ref[...]).astype(BF16)


def _latent_attention(q, k, v, cache_k, cache_v, layer, sink_b, g_out_attn):
    b, seq_len, _ = k.shape
    q_rows = Q_BLOCK * Q_BLOCKS_PER_STEP
    nq = seq_len // q_rows
    past = cache_k.shape[2]
    cache_spec = pl.BlockSpec((1, 1, past, N_KV_HEADS, HEAD_DIM), lambda bi, n: (bi, layer, 0, 0, 0))
    return pl.pallas_call(
        functools.partial(_lat_attn_kernel, seq_len=seq_len),
        grid=(b, nq),
        in_specs=[
            pl.BlockSpec((q_rows, ATTN_WIDTH), lambda bi, n: (bi * nq + n, 0)),
            pl.BlockSpec((1, seq_len, KV_WIDTH), lambda bi, n: (bi, 0, 0)),
            pl.BlockSpec((1, seq_len, KV_WIDTH), lambda bi, n: (bi, 0, 0)),
            cache_spec,
            cache_spec,
            pl.BlockSpec((N_HEADS, LANES), lambda bi, n: (0, 0)),
            pl.BlockSpec((1, ATTN_WIDTH), lambda bi, n: (0, 0)),
        ],
        out_specs=pl.BlockSpec((q_rows, ATTN_WIDTH), lambda bi, n: (bi * nq + n, 0)),
        out_shape=jax.ShapeDtypeStruct((b * seq_len, ATTN_WIDTH), BF16),
        compiler_params=_params(2),
        name="latent_attention",
    )(q, k, v, cache_k, cache_v, sink_b, g_out_attn)


def _route(logits):
    lane = lax.broadcasted_iota(jnp.int32, logits.shape, 1).astype(F32)
    neg = -jnp.inf

    def first_argmax(x):
        mx = jnp.max(x, axis=-1, keepdims=True)
        return mx, jnp.min(jnp.where(x == mx, lane, float(ROUTER_LANES)), axis=-1, keepdims=True)

    gl = jnp.where(lane < N_EXPERT_GROUPS, logits, neg)
    gmax, g_idx = first_argmax(gl)
    p_g = 1.0 / jnp.sum(jnp.exp(gl - gmax), axis=-1, keepdims=True)
    base = N_EXPERT_GROUPS + EXPERTS_PER_GROUP * g_idx
    el = jnp.where((lane >= base) & (lane < base + EXPERTS_PER_GROUP), logits, neg)
    v1, i1 = first_argmax(el)
    v2, i2 = first_argmax(jnp.where(lane == i1, neg, el))
    e2 = jnp.exp(v2 - v1)
    w1 = p_g / (1.0 + e2)
    w2 = p_g * e2 / (1.0 + e2)
    ids = jnp.where(lane == 0.0, i1 - N_EXPERT_GROUPS, jnp.where(lane == 1.0, i2 - N_EXPERT_GROUPS, 0.0))
    wts = jnp.where(lane == 0.0, w1, jnp.where(lane == 1.0, w2, 0.0))
    return ids.astype(jnp.int32), wts


def _outproj_kernel(rows_ref, pool_p, attn_p, x_p, pool_s, attn_s, x_s, mod_ref, g_ref, w_ref, rw_ref,
                    rb_ref, x1_ref, h2_ref, ids_ref, wts_ref, *, n_ctx_tiles):
    del rows_ref
    i = pl.program_id(0)

    def body(pool_ref, attn_ref, x_ref):
        m = mod_ref[0]
        mix = (_dot(pool_ref[...], w_ref[0:POOL_WIDTH, :])
               + _dot(attn_ref[...], w_ref[POOL_WIDTH:POOL_WIDTH + ATTN_WIDTH, :]))
        x1 = x_ref[...] + m[2:3] * mix
        x1_ref[...] = x1
        h2 = _rms(x1) * g_ref[...] * (1.0 + m[4:5]) + m[3:4]
        _store_row_tiles(h2_ref, _pack_bf16_pairs(h2))
        hi = h2.astype(BF16)
        lo = (h2 - hi.astype(F32)).astype(BF16)
        both = _dot(hi, rw_ref[...])
        logits = (both[:, :ROUTER_LANES] + both[:, ROUTER_LANES:] + _dot(lo, rw_ref[:, :ROUTER_LANES])
                  + rb_ref[...])
        ids, wts = _route(logits)
        ids_ref[...] = ids
        wts_ref[...] = wts

    @pl.when(i < n_ctx_tiles)
    def _():
        body(pool_p, attn_p, x_p)

    @pl.when(i >= n_ctx_tiles)
    def _():
        body(pool_s, attn_s, x_s)


def _out_projection(ctx, lat, rows, mod3, g_ffn, w_out_b, rw_split, rb):
    tm = TOKEN_TILE
    n_ctx = ctx[2].shape[0] // tm
    n_lat = lat[2].shape[0] // tm
    t_all = (n_ctx + n_lat) * tm

    def ctx_map(i, r):
        return (jnp.minimum(i, n_ctx - 1), 0)

    def lat_map(i, r):
        return (jnp.maximum(i - n_ctx, 0), 0)

    in_specs = [
        pl.BlockSpec((tm, POOL_WIDTH), ctx_map),
        pl.BlockSpec((tm, ATTN_WIDTH), ctx_map),
        pl.BlockSpec((tm, D_MODEL), ctx_map),
        pl.BlockSpec((tm, POOL_WIDTH), lat_map),
        pl.BlockSpec((tm, ATTN_WIDTH), lat_map),
        pl.BlockSpec((tm, D_MODEL), lat_map),
        pl.BlockSpec((1, 6, D_MODEL), lambda i, r: (r[i], 0, 0)),
        pl.BlockSpec((1, D_MODEL), lambda i, r: (0, 0)),
        pl.BlockSpec((D_MODEL, D_MODEL), lambda i, r: (0, 0), pipeline_mode=pl.Buffered(1)),
        pl.BlockSpec((D_MODEL, 2 * ROUTER_LANES), lambda i, r: (0, 0)),
        pl.BlockSpec((1, ROUTER_LANES), lambda i, r: (0, 0)),
    ]
    out_shape = (
        jax.ShapeDtypeStruct((t_all, D_MODEL), F32),
        jax.ShapeDtypeStruct((t_all * ROW_TILE, LANES), jnp.uint32),
        jax.ShapeDtypeStruct((t_all, ROUTER_LANES), jnp.int32),
        jax.ShapeDtypeStruct((t_all, ROUTER_LANES), F32),
    )
    out_specs = (
        pl.BlockSpec((tm, D_MODEL), lambda i, r: (i, 0)),
        pl.BlockSpec((tm * ROW_TILE, LANES), lambda i, r: (i, 0)),
        pl.BlockSpec((tm, ROUTER_LANES), lambda i, r: (i, 0)),
        pl.BlockSpec((tm, ROUTER_LANES), lambda i, r: (i, 0)),
    )
    return pl.pallas_call(
        functools.partial(_outproj_kernel, n_ctx_tiles=n_ctx),
        grid_spec=pltpu.PrefetchScalarGridSpec(
            num_scalar_prefetch=1, grid=(n_ctx + n_lat,), in_specs=in_specs, out_specs=out_specs),
        out_shape=out_shape,
        compiler_params=_params(1),
        name="out_projection",
    )(rows, *ctx, *lat, mod3, g_ffn, w_out_b, rw_split, rb)


def _gather_rows(idx_ref, base, n_rows, src_hbm, dst, sem, *, unrolled, both_queues=False, dst_row0=0):
    def one(r, priority):
        src_row = pl.multiple_of(idx_ref[base + r], ROW_TILE)
        pltpu.make_async_copy(
            src_hbm.at[pl.ds(src_row, ROW_TILE)], dst.at[pl.ds((dst_row0 + r) * ROW_TILE, ROW_TILE)],
            sem).start(priority=priority)

    if unrolled:
        for r in range(n_rows):
            one(r, r % 2 if both_queues else 0)
    else:
        def body(r, carry):
            one(r, 0)
            return carry
        lax.fori_loop(0, n_rows, body, 0, unroll=8)


def _wait_rows(src_hbm, dst, sem):
    pltpu.make_async_copy(src_hbm.at[pl.ds(0, dst.shape[0])], dst, sem).wait()


def _late_zero(x):
    u = lax.bitcast_convert_type(x, jnp.uint32)
    return ((u >> 16) >> 16).astype(jnp.int32)[0, 0]


def _moe_kernel(be_ref, ne_ref, nu_ref, tok_ref, h_hbm, wg_hbm, wu_hbm, wd_hbm, y_ref,
                xg0, xg1, xg2, sg, su, sd, wgb, wub, wdb, sem, wsem):
    b = pl.program_id(0)
    n_used = nu_ref[0]
    last = n_used - 1
    active = b < n_used
    bufs = (xg0, xg1, xg2)
    n_buf = len(bufs)

    def weight_copies(e):
        return (pltpu.make_async_copy(wg_hbm.at[e], sg, wsem.at[0]),
                pltpu.make_async_copy(wu_hbm.at[e], su, wsem.at[1]),
                pltpu.make_async_copy(wd_hbm.at[e], sd, wsem.at[2]))

    @pl.when(b == 0)
    def _():
        for cp in weight_copies(be_ref[0]):
            cp.start(priority=1)
        for i in range(n_buf - 1):
            _gather_rows(tok_ref, jnp.minimum(i, last) * EXPERT_ROWS, EXPERT_ROWS, h_hbm, bufs[i],
                         sem.at[i], unrolled=False)

    new_expert = (b == 0) | (be_ref[b] != be_ref[jnp.maximum(b - 1, 0)])

    @pl.when(active & new_expert)
    def _():
        for cp in weight_copies(be_ref[b]):
            cp.wait()
        wgb[...] = sg[...].astype(BF16)
        wub[...] = su[...].astype(BF16)
        wdb[...] = sd[...].astype(BF16)

        @pl.when(ne_ref[b] >= 0)
        def _():
            for cp in weight_copies(ne_ref[b]):
                cp.start(priority=1)

    def step(slot):
        cur = bufs[slot]
        ahead = bufs[(slot + n_buf - 1) % n_buf]
        sem_ahead = sem.at[(slot + n_buf - 1) % n_buf]
        _wait_rows(h_hbm, cur, sem.at[slot])
        base = jnp.minimum(b + n_buf - 1, last) * EXPERT_ROWS
        per = EXPERT_ROWS // GATHER_GROUPS
        group = iter(range(GATHER_GROUPS))

        def issue(zero):
            g = next(group)
            _gather_rows(tok_ref, base + g * per + zero, per, h_hbm, ahead, sem_ahead, unrolled=True,
                         both_queues=True, dst_row0=g * per)

        def corner(x, r, c):
            return _late_zero(x[r:r + 1, c:c + 1])

        bm, de = EXPERT_ROWS, D_EXPERT
        lo, hi = _unpack_bf16_pairs(_load_row_tiles(cur, bm))
        lo = lo.astype(BF16)
        hi = hi.astype(BF16)
        def issue_along(x, n):
            rows, cols = x.shape
            for i in range(n):
                issue(corner(x, (i + 1) * rows // n - 1, (i + 1) * cols // n - 1))

        n_each = GATHER_GROUPS // 8
        issue(0)
        a1 = _dot(lo, wgb[0:PACKED, :])
        issue_along(a1, n_each)
        a = a1 + _dot(hi, wgb[PACKED:D_MODEL, :])
        issue_along(a, n_each)
        u1 = _dot(lo, wub[0:PACKED, :])
        issue_along(u1, n_each)
        u = u1 + _dot(hi, wub[PACKED:D_MODEL, :])
        issue_along(u, n_each)
        act = (a / (1.0 + jnp.exp(-a)) * u).astype(BF16)
        y = _dot(act, wdb[...])
        issue_along(y, GATHER_GROUPS - 1 - 4 * n_each)
        _store_row_tiles(y_ref, _pack_bf16_pairs(y))

        @pl.when(b == last)
        def _():
            for i in range(1, n_buf):
                _wait_rows(h_hbm, bufs[(slot + i) % n_buf], sem.at[(slot + i) % n_buf])

    for slot in range(n_buf):
        @pl.when(active & (b % n_buf == slot))
        def _(slot=slot):
            step(slot)

    @pl.when(b >= n_used)
    def _():
        y_ref[...] = jnp.zeros_like(y_ref)


def _experts(block_expert, next_expert, n_used, buf_tok, h2_packed, w_gate, w_up, w_down):
    cap = buf_tok.shape[0]
    bm = EXPERT_ROWS
    xg = pltpu.VMEM((bm * ROW_TILE, LANES), jnp.uint32)
    return pl.pallas_call(
        _moe_kernel,
        grid_spec=pltpu.PrefetchScalarGridSpec(
            num_scalar_prefetch=4,
            grid=(cap // bm,),
            in_specs=[pl.BlockSpec(memory_space=pl.ANY)] * 4,
            out_specs=pl.BlockSpec((bm * ROW_TILE, LANES), lambda b, *_: (b, 0)),
            scratch_shapes=[
                xg, xg, xg,
                pltpu.VMEM((D_MODEL, D_EXPERT), F32), pltpu.VMEM((D_MODEL, D_EXPERT), F32),
                pltpu.VMEM((D_EXPERT, D_MODEL), F32),
                pltpu.VMEM((D_MODEL, D_EXPERT), BF16), pltpu.VMEM((D_MODEL, D_EXPERT), BF16),
                pltpu.VMEM((D_EXPERT, D_MODEL), BF16),
                pltpu.SemaphoreType.DMA((3,)), pltpu.SemaphoreType.DMA((3,)),
            ],
        ),
        out_shape=jax.ShapeDtypeStruct((cap * ROW_TILE, LANES), jnp.uint32),
        compiler_params=_params(1),
        name="experts",
    )(block_expert, next_expert, n_used, buf_tok, h2_packed, w_gate, w_up, w_down)


def _combine_kernel(slot_ref, rows_ref, y_hbm, x1_ref, wts_ref, mod_ref, g_ref, op_ref, os_ref, yg0, yg1, sem,
                    *, n_ctx_tiles, n_tiles):
    del rows_ref
    tm = TOKEN_TILE
    i = pl.program_id(0)
    n_slots = slot_ref.shape[0] // TOP_K

    def start(tile, buf, s, unrolled):
        for k in range(TOP_K):
            _gather_rows(slot_ref, k * n_slots + tile * tm, tm, y_hbm, buf.at[k], s, unrolled=unrolled,
                         both_queues=True)

    @pl.when(i == 0)
    def _():
        start(0, yg0, sem.at[0], False)

    def step(cur, nxt, sem_cur, sem_nxt):
        for k in range(TOP_K):
            _wait_rows(y_hbm, cur.at[k], sem_cur)
        start(jnp.minimum(i + 1, n_tiles - 1), nxt, sem_nxt, True)
        w = wts_ref[...]
        lo0, hi0 = _unpack_bf16_pairs(_load_row_tiles(cur.at[0], tm))
        lo1, hi1 = _unpack_bf16_pairs(_load_row_tiles(cur.at[1], tm))
        ffn = jnp.concatenate([lo0 * w[:, 0:1] + lo1 * w[:, 1:2], hi0 * w[:, 0:1] + hi1 * w[:, 1:2]], axis=-1)
        x2 = x1_ref[...] + mod_ref[0][5:6] * ffn
        out = _rms(x2) * g_ref[...]

        @pl.when(i < n_ctx_tiles)
        def _():
            op_ref[...] = out

        @pl.when(i >= n_ctx_tiles)
        def _():
            os_ref[...] = out

        @pl.when(i == n_tiles - 1)
        def _():
            for k in range(TOP_K):
                _wait_rows(y_hbm, nxt.at[k], sem_nxt)

    @pl.when(i % 2 == 0)
    def _():
        step(yg0, yg1, sem.at[0], sem.at[1])

    @pl.when(i % 2 == 1)
    def _():
        step(yg1, yg0, sem.at[1], sem.at[0])


def _combine(slots, rows, y_buf, x1, wts, mod3, g_final, t_ctx):
    t = x1.shape[0]
    tm = TOKEN_TILE
    n_tiles = t // tm
    n_ctx = t_ctx // tm
    return pl.pallas_call(
        functools.partial(_combine_kernel, n_ctx_tiles=n_ctx, n_tiles=n_tiles),
        grid_spec=pltpu.PrefetchScalarGridSpec(
            num_scalar_prefetch=2,
            grid=(n_tiles,),
            in_specs=[
                pl.BlockSpec(memory_space=pl.ANY),
                pl.BlockSpec((tm, D_MODEL), lambda i, s, r: (i, 0)),
                pl.BlockSpec((tm, ROUTER_LANES), lambda i, s, r: (i, 0)),
                pl.BlockSpec((1, 6, D_MODEL), lambda i, s, r: (r[i], 0, 0)),
                pl.BlockSpec((1, D_MODEL), lambda i, s, r: (0, 0)),
            ],
            out_specs=(
                pl.BlockSpec((tm, D_MODEL), lambda i, s, r: (jnp.minimum(i, n_ctx - 1), 0)),
                pl.BlockSpec((tm, D_MODEL), lambda i, s, r: (jnp.maximum(i - n_ctx, 0), 0)),
            ),
            scratch_shapes=[pltpu.VMEM((TOP_K, tm * ROW_TILE, LANES), jnp.uint32),
                            pltpu.VMEM((TOP_K, tm * ROW_TILE, LANES), jnp.uint32),
                            pltpu.SemaphoreType.DMA((2,))],
        ),
        out_shape=(jax.ShapeDtypeStruct((t_ctx, D_MODEL), F32),
                   jax.ShapeDtypeStruct((t - t_ctx, D_MODEL), F32)),
        compiler_params=_params(1),
        name="combine",
    )(slots, rows, y_buf, x1, wts, mod3, g_final)


def _rope_tables(n_tokens):
    rows = n_tokens // GRID_W
    row = jnp.repeat(jnp.arange(rows, dtype=F32), GRID_W)
    col = jnp.tile(jnp.arange(GRID_W, dtype=F32), rows)
    half = ROPE_AXIS_DIM // 2
    inv_freq = ROPE_BASE ** (-jnp.arange(half, dtype=F32) / half)
    ar = row[:, None] * inv_freq
    ac = col[:, None] * inv_freq
    cos = jnp.concatenate([jnp.cos(ar), jnp.cos(ar), jnp.cos(ac), jnp.cos(ac)], axis=-1)
    sin = jnp.concatenate([-jnp.sin(ar), jnp.sin(ar), -jnp.sin(ac), jnp.sin(ac)], axis=-1)
    return cos, sin


def _invert_kernel(dest_ref, gap_lo_ref, gap_hi_ref, out_ref):
    group = 16
    cap = out_ref.shape[0]

    def clear_gap(e, carry):
        lo = gap_lo_ref[e]

        def clear(c, inner):
            for k in range(group):
                out_ref[jnp.minimum(lo + c * group + k, cap - 1)] = 0
            return inner
        lax.fori_loop(0, (gap_hi_ref[e] - lo + group - 1) // group, clear, 0)
        return carry
    lax.fori_loop(0, gap_lo_ref.shape[0], clear_gap, 0)

    def place(i, carry):
        a0 = i * group
        tile0 = i * (group // TOP_K * ROW_TILE)
        rows = [dest_ref[a0 + k] for k in range(group)]
        for k in range(group):
            out_ref[rows[k]] = tile0 + (k // TOP_K) * ROW_TILE
        return carry
    lax.fori_loop(0, dest_ref.shape[0] // group, place, 0)


def _invert_slots(dest, gap_lo, gap_hi, cap):
    return pl.pallas_call(
        _invert_kernel,
        grid_spec=pltpu.PrefetchScalarGridSpec(
            num_scalar_prefetch=3, grid=(1,), in_specs=[],
            out_specs=pl.BlockSpec(memory_space=pltpu.SMEM)),
        out_shape=jax.ShapeDtypeStruct((cap,), jnp.int32),
        compiler_params=_params(1),
        name="invert_slots",
    )(dest, gap_lo, gap_hi)


def _dispatch_tables(ids):
    t = ids.shape[0]
    bm = EXPERT_ROWS
    flat_e = ids.reshape(t * TOP_K)
    onehot = (flat_e[:, None] == jnp.arange(N_EXPERTS, dtype=jnp.int32)[None, :]).astype(jnp.int32)
    running = jnp.cumsum(onehot, axis=0)
    rank = jnp.sum(running * onehot, axis=1) - 1
    counts = running[-1]
    padded = (counts + bm - 1) // bm * bm
    pad_end = jnp.cumsum(padded)
    pad_start = pad_end - padded
    dest = (jnp.sum(pad_start[None, :] * onehot, axis=1) + rank).astype(jnp.int32)
    n_blocks = (t * TOP_K + N_EXPERTS * (bm - 1) + bm - 1) // bm
    cap = n_blocks * bm
    gap_lo = jnp.concatenate([pad_start + counts, pad_end[-1:]]).astype(jnp.int32)
    gap_hi = jnp.concatenate([pad_end, jnp.full((1,), cap, pad_end.dtype)]).astype(jnp.int32)
    buf_tok = _invert_slots(dest, gap_lo, gap_hi, cap)
    n_used = (pad_end[-1] // bm).astype(jnp.int32)
    blk = jnp.arange(n_blocks, dtype=jnp.int32)
    blk = jnp.minimum(blk, n_used - 1)
    block_expert = jnp.sum((pad_end[None, :] <= (blk * bm)[:, None]).astype(jnp.int32), axis=1)
    block_expert = jnp.minimum(block_expert, N_EXPERTS - 1)
    e_ids = jnp.arange(N_EXPERTS, dtype=jnp.int32)
    later = (e_ids[None, :] > e_ids[:, None]) & (counts[None, :] > 0)
    next_of = jnp.min(jnp.where(later, e_ids[None, :], N_EXPERTS), axis=1)
    next_of = jnp.where(next_of == N_EXPERTS, -1, next_of)
    next_expert = jnp.sum(next_of[None, :] * (block_expert[:, None] == e_ids[None, :]), axis=1)
    slots = dest.reshape(t, TOP_K).T.reshape(TOP_K * t)
    return (block_expert.astype(jnp.int32), next_expert.astype(jnp.int32), n_used.reshape(1),
            buf_tok, slots * ROW_TILE)


def kernel(x_prompt, x_sample, cache_k, cache_v, c, c_ctx, w_mod, b_mod, norm_mix_g, norm_ffn_g, w_in,
           pool_w, pool_scale, attn_sink, out_norm_pool_g, out_norm_attn_g, w_out, router_group_w,
           router_group_b, router_expert_w, router_expert_b, w_gate, w_up, w_down, final_norm_g):
    depth = w_mod.shape[0]
    assert depth == 1, "single trunk layer"
    bp, lp, _ = x_prompt.shape
    bs, ls, _ = x_sample.shape
    tp, ts = bp * lp, bs * ls
    tm = TOKEN_TILE
    l = 0

    cond8 = jnp.zeros((8, D_MODEL), F32).at[:bs].set(c).at[bs].set(c_ctx)
    mod3 = _modulation(cond8, w_mod[l], b_mod[l]).reshape(8, 6, D_MODEL)
    rows_p = jnp.full((tp // tm,), bs, jnp.int32)
    rows_s = jnp.arange(ts // tm, dtype=jnp.int32) // (ls // tm)

    w_in_b = w_in[l].astype(BF16)
    w_out_b = w_out[l].astype(BF16)
    pool_w_b = pool_w[l].astype(BF16)
    g_mix = norm_mix_g[l].reshape(1, D_MODEL)
    g_ffn = norm_ffn_g[l].reshape(1, D_MODEL)
    g_pool = out_norm_pool_g[l].reshape(1, POOL_WIDTH)
    g_attn = out_norm_attn_g[l].reshape(1, ATTN_WIDTH)
    p_scale = pool_scale[l].reshape(1, POOL_WIDTH)
    sink_b = jnp.broadcast_to(attn_sink[l][:, None], (N_HEADS, LANES))
    g_final = final_norm_g.reshape(1, D_MODEL)

    rw = jnp.concatenate(
        [router_group_w[l], jnp.transpose(router_expert_w[l], (1, 0, 2)).reshape(D_MODEL, N_EXPERTS)], axis=1)
    rw = jnp.pad(rw, ((0, 0), (0, ROUTER_LANES - rw.shape[1])))
    rw_hi = rw.astype(BF16)
    rw_split = jnp.concatenate([rw_hi, (rw - rw_hi.astype(F32)).astype(BF16)], axis=1)
    rb = jnp.concatenate([router_group_b[l], router_expert_b[l].reshape(N_EXPERTS)])
    rb = jnp.pad(rb, (0, ROUTER_LANES - rb.shape[0])).reshape(1, ROUTER_LANES)

    xp = x_prompt.reshape(tp, D_MODEL)
    xs = x_sample.reshape(ts, D_MODEL)

    up_p, q_p, k_p, v_p = _in_projection(xp, rows_p, mod3, g_mix, w_in_b, None, lp)
    pool_p = _pool_mixer(up_p.reshape(bp, lp, POOL_WIDTH), pool_w_b, p_scale, g_pool).reshape(tp, POOL_WIDTH)
    attn_p = _context_attention(q_p, k_p, v_p, sink_b, g_attn, lp)
    up_s, q_s, k_s, v_s = _in_projection(xs, rows_s, mod3, g_mix, w_in_b, _rope_tables(ls))
    pool_s = _pool_mixer(up_s.reshape(bs, ls, POOL_WIDTH), pool_w_b, p_scale, g_pool).reshape(ts, POOL_WIDTH)
    attn_s = _latent_attention(
        q_s, k_s.reshape(bs, ls, KV_WIDTH), v_s.reshape(bs, ls, KV_WIDTH), cache_k, cache_v, l, sink_b, g_attn)

    rows_all = jnp.concatenate([rows_p, rows_s])
    x1_all, h2_all, ids_all, wts_all = _out_projection(
        (pool_p, attn_p, xp), (pool_s, attn_s, xs), rows_all, mod3, g_ffn, w_out_b, rw_split, rb)

    block_expert, next_expert, n_used, buf_tok, slots = _dispatch_tables(ids_all[:, :TOP_K])
    y_buf = _experts(block_expert, next_expert, n_used, buf_tok, h2_all, w_gate[l], w_up[l], w_down[l])

    y_p, y_s = _combine(slots, rows_all, y_buf, x1_all, wts_all, mod3, g_final, tp)

    return y_p.reshape(bp, lp, D_MODEL), y_s.reshape(bs, ls, D_MODEL), k_p, v_p
```

```python
import functools

import jax
import jax.numpy as jnp
from jax import lax
from jax.experimental import pallas as pl
from jax.experimental.pallas import tpu as pltpu

F32 = jnp.float32
BF16 = jnp.bfloat16

D_MODEL = 2048
GRID_W = 64
HEAD_DIM = 128
ATTN_WIDTH = 1024
POOL_WIDTH = 1024
N_HEADS = 8
N_KV_HEADS = 2
Q_PER_KV = 4
KV_WIDTH = 256
IN_WIDTH = 2560
WINDOW = 128
POOL_WINDOWS = (2, 4, 8, 16)
POOL_GROUP_DIM = 256
ROPE_BASE = 10000.0
ROPE_AXIS_DIM = 64
N_EXPERT_GROUPS = 4
EXPERTS_PER_GROUP = 8
N_EXPERTS = 32
TOP_K = 2
D_EXPERT = 512
NORM_EPS = 1e-6
NEG_INF = -1e30
ATTN_SCALE = HEAD_DIM ** -0.5
LOG2E = 1.4426950408889634
Q_SCALE = ATTN_SCALE * LOG2E

LANES = 128
Q_BLOCKS_PER_STEP = 4
CTX_SEQS_PER_STEP = 2
VMEM_LIMIT = 48 * 1024 * 1024
PACKED = D_MODEL // 2
ROW_TILE = 8
assert PACKED == ROW_TILE * LANES, "a packed row must fill exactly one tile"
MOD_TILE = 1024
TOKEN_TILE = 256
IN_TILE = 512
POOL_TILES_PER_STEP = 2
POOL_HALO = 16
Q_BLOCK = 128
EXPERT_ROWS = 256
GATHER_GROUPS = 64
ROUTER_LANES = 128


def _params(n_grid):
    return pltpu.CompilerParams(
        dimension_semantics=("arbitrary",) * n_grid, vmem_limit_bytes=VMEM_LIMIT)


def _rms(x):
    return x * lax.rsqrt(jnp.mean(x * x, axis=-1, keepdims=True) + NORM_EPS)


def _dot(a, b):
    return jnp.dot(a, b, preferred_element_type=F32)


def _dot_nt(a, b):
    return lax.dot_general(a, b, (((1,), (1,)), ((), ())), preferred_element_type=F32)


def _pack_bf16_pairs(x):
    c = x.shape[1] // 2
    lo = lax.bitcast_convert_type(x[:, :c].astype(BF16).astype(F32), jnp.uint32)
    hi = lax.bitcast_convert_type(x[:, c:].astype(BF16).astype(F32), jnp.uint32)
    return hi | (lo >> 16)


def _store_row_tiles(ref, x):
    n = x.shape[0]
    for c in range(PACKED // LANES):
        ref[pl.ds(c, n, stride=ROW_TILE), :] = x[:, c * LANES:(c + 1) * LANES]


def _load_row_tiles(ref, n):
    return jnp.concatenate(
        [ref[pl.ds(c, n, stride=ROW_TILE), :] for c in range(PACKED // LANES)], axis=-1)


def _unpack_bf16_pairs(u):
    lo = lax.bitcast_convert_type(u << 16, F32)
    hi = lax.bitcast_convert_type(u & jnp.uint32(0xFFFF0000), F32)
    return lo, hi


def _mod_kernel(cond_ref, w_ref, b_ref, o_ref):
    c = cond_ref[...]
    s = c / (1.0 + jnp.exp(-c))
    o_ref[...] = _dot(s.astype(BF16), w_ref[...].astype(BF16)) + b_ref[...]


def _modulation(cond8, w_mod, b_mod):
    n = w_mod.shape[1]
    tn = MOD_TILE
    return pl.pallas_call(
        _mod_kernel,
        grid=(n // tn,),
        in_specs=[
            pl.BlockSpec((8, D_MODEL), lambda j: (0, 0)),
            pl.BlockSpec((D_MODEL, tn), lambda j: (0, j)),
            pl.BlockSpec((1, tn), lambda j: (0, j)),
        ],
        out_specs=pl.BlockSpec((8, tn), lambda j: (0, j)),
        out_shape=jax.ShapeDtypeStruct((8, n), F32),
        compiler_params=_params(1),
        name="modulation",
    )(cond8, w_mod, b_mod.reshape(1, n))


def _rope(x, cos, sin_signed):
    lane = lax.broadcasted_iota(jnp.int32, x.shape, 1)
    partner = jnp.where((lane & 63) < 32,
                        pltpu.roll(x, HEAD_DIM - 32, axis=1),
                        pltpu.roll(x, 32, axis=1))
    return x * cos + partner * sin_signed


def _inproj_kernel(rows_ref, x_ref, mod_ref, g_ref, w_ref, *rest, rope):
    del rows_ref
    if rope:
        cos_ref, sin_ref, up_ref, q_ref, k_ref, v_ref = rest
    else:
        up_ref, q_ref, k_ref, v_ref = rest
    m = mod_ref[0]
    h = _rms(x_ref[...]) * g_ref[...]
    hb = (h * (1.0 + m[1:2]) + m[0:1]).astype(BF16)
    up_ref[...] = _dot(hb, w_ref[:, 0:POOL_WIDTH]).astype(BF16)
    q = _dot(hb, w_ref[:, POOL_WIDTH:POOL_WIDTH + ATTN_WIDTH])
    k = _dot(hb, w_ref[:, POOL_WIDTH + ATTN_WIDTH:IN_WIDTH - KV_WIDTH])
    v = _dot(hb, w_ref[:, IN_WIDTH - KV_WIDTH:IN_WIDTH])
    if rope:
        cos = cos_ref[...]
        sin = sin_ref[...]
        for hd in range(N_HEADS):
            sl = slice(hd * HEAD_DIM, (hd + 1) * HEAD_DIM)
            q_ref[:, sl] = (_rope(q[:, sl], cos, sin) * Q_SCALE).astype(BF16)
        for hd in range(N_KV_HEADS):
            sl = slice(hd * HEAD_DIM, (hd + 1) * HEAD_DIM)
            k_ref[:, sl] = _rope(k[:, sl], cos, sin).astype(k_ref.dtype)
    else:
        q_ref[...] = (q * Q_SCALE).astype(BF16)
        k_ref[...] = k.astype(k_ref.dtype)
    v_ref[...] = v.astype(v_ref.dtype)


def _in_projection(x, rows, mod3, g_mix, w_in_b, rope_tabs, kv_dtype):
    t = x.shape[0]
    tm = IN_TILE
    per = IN_TILE // TOKEN_TILE
    rope = rope_tabs is not None
    in_specs = [
        pl.BlockSpec((tm, D_MODEL), lambda i, r: (i, 0)),
        pl.BlockSpec((1, 6, D_MODEL), lambda i, r: (r[i * per], 0, 0)),
        pl.BlockSpec((1, D_MODEL), lambda i, r: (0, 0)),
        pl.BlockSpec((D_MODEL, IN_WIDTH), lambda i, r: (0, 0), pipeline_mode=pl.Buffered(1)),
    ]
    args = [x, mod3, g_mix, w_in_b]
    if rope:
        seq_tiles = rope_tabs[0].shape[0] // tm
        for tab in rope_tabs:
            in_specs.append(pl.BlockSpec((tm, HEAD_DIM), lambda i, r: (i % seq_tiles, 0)))
            args.append(tab)
    out_shape = (
        jax.ShapeDtypeStruct((t, POOL_WIDTH), BF16),
        jax.ShapeDtypeStruct((t, ATTN_WIDTH), BF16),
        jax.ShapeDtypeStruct((t, KV_WIDTH), kv_dtype),
        jax.ShapeDtypeStruct((t, KV_WIDTH), kv_dtype),
    )
    out_specs = (
        pl.BlockSpec((tm, POOL_WIDTH), lambda i, r: (i, 0)),
        pl.BlockSpec((tm, ATTN_WIDTH), lambda i, r: (i, 0)),
        pl.BlockSpec((tm, KV_WIDTH), lambda i, r: (i, 0)),
        pl.BlockSpec((tm, KV_WIDTH), lambda i, r: (i, 0)),
    )
    return pl.pallas_call(
        functools.partial(_inproj_kernel, rope=rope),
        grid_spec=pltpu.PrefetchScalarGridSpec(
            num_scalar_prefetch=1, grid=(t // tm,), in_specs=in_specs, out_specs=out_specs),
        out_shape=out_shape,
        compiler_params=_params(1),
        name="in_projection_rope" if rope else "in_projection",
    )(rows, *args)


def _pool_kernel(u_ref, pw_ref, ps_ref, g_ref, o_ref, *, seq_len):
    tq = TOKEN_TILE
    win = min(seq_len, tq + 2 * POOL_HALO)
    for sub in range(o_ref.shape[1] // tq):
        t0 = pl.program_id(1) * o_ref.shape[1] + sub * tq
        src0 = pl.multiple_of(jnp.clip(t0 - POOL_HALO, 0, seq_len - win), POOL_HALO)
        u = u_ref[0, pl.ds(src0, win), :]
        t = t0 + lax.broadcasted_iota(jnp.int32, (tq, win), 0)
        j = src0 + lax.broadcasted_iota(jnp.int32, (tq, win), 1)
        tc = t0 + lax.broadcasted_iota(jnp.int32, (tq, 1), 0)
        ys = []
        for gi, w in enumerate(POOL_WINDOWS):
            lo = jnp.maximum(t - w // 2, 0)
            hi = jnp.minimum(t + (w - w // 2), seq_len)
            cnt = (hi - lo).astype(F32)
            a = jnp.where((j >= lo) & (j < hi), 1.0, 0.0) - jnp.where(j == t, cnt, 0.0)
            cnt_col = (jnp.minimum(tc + (w - w // 2), seq_len) - jnp.maximum(tc - w // 2, 0)).astype(F32)
            sl = slice(gi * POOL_GROUP_DIM, (gi + 1) * POOL_GROUP_DIM)
            d = _dot(a.astype(BF16), u[:, sl]) / cnt_col
            ys.append(_dot(d.astype(BF16), pw_ref[gi]))
        y = jnp.concatenate(ys, axis=-1) * ps_ref[...]
        o_ref[0, sub * tq:(sub + 1) * tq, :] = (_rms(y) * g_ref[...]).astype(BF16)


def _pool_mixer(up, pool_w_b, pool_scale, g_out_pool):
    b, seq_len, _ = up.shape
    tq = TOKEN_TILE * min(POOL_TILES_PER_STEP, seq_len // TOKEN_TILE)
    return pl.pallas_call(
        functools.partial(_pool_kernel, seq_len=seq_len),
        grid=(b, seq_len // tq),
        in_specs=[
            pl.BlockSpec((1, seq_len, POOL_WIDTH), lambda bi, i: (bi, 0, 0)),
            pl.BlockSpec((len(POOL_WINDOWS), POOL_GROUP_DIM, POOL_GROUP_DIM), lambda bi, i: (0, 0, 0)),
            pl.BlockSpec((1, POOL_WIDTH), lambda bi, i: (0, 0)),
            pl.BlockSpec((1, POOL_WIDTH), lambda bi, i: (0, 0)),
        ],
        out_specs=pl.BlockSpec((1, tq, POOL_WIDTH), lambda bi, i: (bi, i, 0)),
        out_shape=jax.ShapeDtypeStruct((b, seq_len, POOL_WIDTH), BF16),
        compiler_params=_params(2),
        name=f"pool_mixer_{seq_len}",
    )(up, pool_w_b, pool_scale, g_out_pool)


def _stack_heads(q, kh, rows):
    return jnp.concatenate(
        [q[:, (kh * Q_PER_KV + g) * HEAD_DIM:(kh * Q_PER_KV + g + 1) * HEAD_DIM]
         for g in range(Q_PER_KV)], axis=0)


def _sink_column(sink_ref, kh, rows):
    return jnp.concatenate(
        [jnp.broadcast_to(sink_ref[kh * Q_PER_KV + g:kh * Q_PER_KV + g + 1, 0:1] * LOG2E, (rows, 1))
         for g in range(Q_PER_KV)], axis=0)


def _attend(q4, key_sets, value_sets, sk, band=None):
    s = [_dot_nt(q4, k) for k in key_sets]
    if band is not None:
        s[0] = jnp.where(band, s[0], NEG_INF)
    m = sk
    for si in s:
        m = jnp.maximum(m, jnp.max(si, axis=-1, keepdims=True))
    acc = None
    for si, v in zip(s, value_sets):
        v1 = jnp.concatenate([v, jnp.ones_like(v)], axis=-1)
        part = _dot(jnp.exp2(si - m).astype(BF16), v1)
        acc = part if acc is None else acc + part
    den = acc[:, HEAD_DIM:] + jnp.exp2(sk - m)
    return acc[:, :HEAD_DIM] / den


def _ctx_attn_kernel(q_ref, k_ref, v_ref, sink_ref, g_ref, o_ref, *, seq_len):
    rows = seq_len
    for j in range(q_ref.shape[0] // rows):
        rs = slice(j * rows, (j + 1) * rows)
        q = q_ref[rs, :]
        heads = [None] * N_HEADS
        for kh in range(N_KV_HEADS):
            sl = slice(kh * HEAD_DIM, (kh + 1) * HEAD_DIM)
            o = _attend(_stack_heads(q, kh, rows), [k_ref[rs, sl].astype(BF16)],
                        [v_ref[rs, sl].astype(BF16)], _sink_column(sink_ref, kh, rows))
            for g in range(Q_PER_KV):
                heads[kh * Q_PER_KV + g] = o[g * rows:(g + 1) * rows]
        y = jnp.concatenate(heads, axis=-1)
        o_ref[rs, :] = (_rms(y) * g_ref[...]).astype(BF16)


def _context_attention(q, k, v, sink_b, g_out_attn, seq_len):
    t = q.shape[0]
    blk = seq_len * CTX_SEQS_PER_STEP
    return pl.pallas_call(
        functools.partial(_ctx_attn_kernel, seq_len=seq_len),
        grid=(t // blk,),
        in_specs=[
            pl.BlockSpec((blk, ATTN_WIDTH), lambda b: (b, 0)),
            pl.BlockSpec((blk, KV_WIDTH), lambda b: (b, 0)),
            pl.BlockSpec((blk, KV_WIDTH), lambda b: (b, 0)),
            pl.BlockSpec((N_HEADS, LANES), lambda b: (0, 0)),
            pl.BlockSpec((1, ATTN_WIDTH), lambda b: (0, 0)),
        ],
        out_specs=pl.BlockSpec((blk, ATTN_WIDTH), lambda b: (b, 0)),
        out_shape=jax.ShapeDtypeStruct((t, ATTN_WIDTH), BF16),
        compiler_params=_params(1),
        name="context_attention",
    )(q, k, v, sink_b, g_out_attn)


def _lat_attn_kernel(q_ref, k_ref, v_ref, ck_ref, cv_ref, sink_ref, g_ref, o_ref, *, seq_len):
    rows = Q_BLOCK
    span = 3 * Q_BLOCK
    shape = (Q_PER_KV * rows, span)
    row_in_block = lax.broadcasted_iota(jnp.int32, shape, 0) & (rows - 1)
    col = lax.broadcasted_iota(jnp.int32, shape, 1)
    for j in range(Q_BLOCKS_PER_STEP):
        q0 = (pl.program_id(1) * Q_BLOCKS_PER_STEP + j) * rows
        start = pl.multiple_of(jnp.clip(q0 - Q_BLOCK, 0, seq_len - span), Q_BLOCK)
        q = q_ref[j * rows:(j + 1) * rows, :]
        kl = k_ref[0, pl.ds(start, span), :]
        vl = v_ref[0, pl.ds(start, span), :]
        band = jnp.abs(row_in_block - col + (q0 - start)) <= WINDOW
        heads = [None] * N_HEADS
        for kh in range(N_KV_HEADS):
            sl = slice(kh * HEAD_DIM, (kh + 1) * HEAD_DIM)
            o = _attend(_stack_heads(q, kh, rows),
                        [kl[:, sl], ck_ref[0, :, sl].astype(BF16)],
                        [vl[:, sl], cv_ref[0, :, sl].astype(BF16)],
                        _sink_column(sink_ref, kh, rows), band)
            for g in range(Q_PER_KV):
                heads[kh * Q_PER_KV + g] = o[g * rows:(g + 1) * rows]
        y = jnp.concatenate(heads, axis=-1)
        o_ref[j * rows:(j + 1) * rows, :] = (_rms(y) * g_ref[...]).astype(BF16)


def _latent_attention(q, k, v, cache_k, cache_v, sink_b, g_out_attn):
    b, seq_len, _ = k.shape
    q_rows = Q_BLOCK * Q_BLOCKS_PER_STEP
    nq = seq_len // q_rows
    past = cache_k.shape[1]
    return pl.pallas_call(
        functools.partial(_lat_attn_kernel, seq_len=seq_len),
        grid=(b, nq),
        in_specs=[
            pl.BlockSpec((q_rows, ATTN_WIDTH), lambda bi, n: (bi * nq + n, 0)),
            pl.BlockSpec((1, seq_len, KV_WIDTH), lambda bi, n: (bi, 0, 0)),
            pl.BlockSpec((1, seq_len, KV_WIDTH), lambda bi, n: (bi, 0, 0)),
            pl.BlockSpec((1, past, KV_WIDTH), lambda bi, n: (bi, 0, 0)),
            pl.BlockSpec((1, past, KV_WIDTH), lambda bi, n: (bi, 0, 0)),
            pl.BlockSpec((N_HEADS, LANES), lambda bi, n: (0, 0)),
            pl.BlockSpec((1, ATTN_WIDTH), lambda bi, n: (0, 0)),
        ],
        out_specs=pl.BlockSpec((q_rows, ATTN_WIDTH), lambda bi, n: (bi * nq + n, 0)),
        out_shape=jax.ShapeDtypeStruct((b * seq_len, ATTN_WIDTH), BF16),
        compiler_params=_params(2),
        name="latent_attention",
    )(q, k, v, cache_k, cache_v, sink_b, g_out_attn)


def _route(logits):
    lane = lax.broadcasted_iota(jnp.int32, logits.shape, 1).astype(F32)
    neg = -jnp.inf

    def first_argmax(x):
        mx = jnp.max(x, axis=-1, keepdims=True)
        return mx, jnp.min(jnp.where(x == mx, lane, float(ROUTER_LANES)), axis=-1, keepdims=True)

    gl = jnp.where(lane < N_EXPERT_GROUPS, logits, neg)
    gmax, g_idx = first_argmax(gl)
    p_g = 1.0 / jnp.sum(jnp.exp(gl - gmax), axis=-1, keepdims=True)
    base = N_EXPERT_GROUPS + EXPERTS_PER_GROUP * g_idx
    el = jnp.where((lane >= base) & (lane < base + EXPERTS_PER_GROUP), logits, neg)
    v1, i1 = first_argmax(el)
    v2, i2 = first_argmax(jnp.where(lane == i1, neg, el))
    e2 = jnp.exp(v2 - v1)
    w1 = p_g / (1.0 + e2)
    w2 = p_g * e2 / (1.0 + e2)
    ids = jnp.where(lane == 0.0, i1 - N_EXPERT_GROUPS, jnp.where(lane == 1.0, i2 - N_EXPERT_GROUPS, 0.0))
    wts = jnp.where(lane == 0.0, w1, jnp.where(lane == 1.0, w2, 0.0))
    return ids.astype(jnp.int32), wts


def _outproj_kernel(rows_ref, pool_p, attn_p, x_p, pool_s, attn_s, x_s, mod_ref, g_ref, w_ref, rw_ref,
                    rb_ref, x1_ref, h2_ref, ids_ref, wts_ref, *, n_ctx_tiles):
    del rows_ref
    i = pl.program_id(0)

    def body(pool_ref, attn_ref, x_ref):
        m = mod_ref[0]
        mix = (_dot(pool_ref[...], w_ref[0:POOL_WIDTH, :])
               + _dot(attn_ref[...], w_ref[POOL_WIDTH:POOL_WIDTH + ATTN_WIDTH, :]))
        x1 = x_ref[...] + m[2:3] * mix
        x1_ref[...] = x1
        h2 = _rms(x1) * g_ref[...] * (1.0 + m[4:5]) + m[3:4]
        _store_row_tiles(h2_ref, _pack_bf16_pairs(h2))
        hi = h2.astype(BF16)
        lo = (h2 - hi.astype(F32)).astype(BF16)
        both = _dot(hi, rw_ref[...])
        logits = (both[:, :ROUTER_LANES] + both[:, ROUTER_LANES:] + _dot(lo, rw_ref[:, :ROUTER_LANES])
                  + rb_ref[...])
        ids, wts = _route(logits)
        ids_ref[...] = ids
        wts_ref[...] = wts

    @pl.when(i < n_ctx_tiles)
    def _():
        body(pool_p, attn_p, x_p)

    @pl.when(i >= n_ctx_tiles)
    def _():
        body(pool_s, attn_s, x_s)


def _out_projection(ctx, lat, rows, mod3, g_ffn, w_out_b, rw_split, rb):
    tm = TOKEN_TILE
    n_ctx = ctx[2].shape[0] // tm
    n_lat = lat[2].shape[0] // tm
    t_all = (n_ctx + n_lat) * tm

    def ctx_map(i, r):
        return (jnp.minimum(i, n_ctx - 1), 0)

    def lat_map(i, r):
        return (jnp.maximum(i - n_ctx, 0), 0)

    in_specs = [
        pl.BlockSpec((tm, POOL_WIDTH), ctx_map),
        pl.BlockSpec((tm, ATTN_WIDTH), ctx_map),
        pl.BlockSpec((tm, D_MODEL), ctx_map),
        pl.BlockSpec((tm, POOL_WIDTH), lat_map),
        pl.BlockSpec((tm, ATTN_WIDTH), lat_map),
        pl.BlockSpec((tm, D_MODEL), lat_map),
        pl.BlockSpec((1, 6, D_MODEL), lambda i, r: (r[i], 0, 0)),
        pl.BlockSpec((1, D_MODEL), lambda i, r: (0, 0)),
        pl.BlockSpec((D_MODEL, D_MODEL), lambda i, r: (0, 0), pipeline_mode=pl.Buffered(1)),
        pl.BlockSpec((D_MODEL, 2 * ROUTER_LANES), lambda i, r: (0, 0)),
        pl.BlockSpec((1, ROUTER_LANES), lambda i, r: (0, 0)),
    ]
    out_shape = (
        jax.ShapeDtypeStruct((t_all, D_MODEL), F32),
        jax.ShapeDtypeStruct((t_all * ROW_TILE, LANES), jnp.uint32),
        jax.ShapeDtypeStruct((t_all, ROUTER_LANES), jnp.int32),
        jax.ShapeDtypeStruct((t_all, ROUTER_LANES), F32),
    )
    out_specs = (
        pl.BlockSpec((tm, D_MODEL), lambda i, r: (i, 0)),
        pl.BlockSpec((tm * ROW_TILE, LANES), lambda i, r: (i, 0)),
        pl.BlockSpec((tm, ROUTER_LANES), lambda i, r: (i, 0)),
        pl.BlockSpec((tm, ROUTER_LANES), lambda i, r: (i, 0)),
    )
    return pl.pallas_call(
        functools.partial(_outproj_kernel, n_ctx_tiles=n_ctx),
        grid_spec=pltpu.PrefetchScalarGridSpec(
            num_scalar_prefetch=1, grid=(n_ctx + n_lat,), in_specs=in_specs, out_specs=out_specs),
        out_shape=out_shape,
        compiler_params=_params(1),
        name="out_projection",
    )(rows, *ctx, *lat, mod3, g_ffn, w_out_b, rw_split, rb)


def _gather_rows(idx_ref, base, n_rows, src_hbm, dst, sem, *, unrolled, both_queues=False, dst_row0=0):
    def one(r, priority):
        src_row = pl.multiple_of(idx_ref[base + r], ROW_TILE)
        pltpu.make_async_copy(
            src_hbm.at[pl.ds(src_row, ROW_TILE)], dst.at[pl.ds((dst_row0 + r) * ROW_TILE, ROW_TILE)],
            sem).start(priority=priority)

    if unrolled:
        for r in range(n_rows):
            one(r, r % 2 if both_queues else 0)
    else:
        def body(r, carry):
            one(r, 0)
            return carry
        lax.fori_loop(0, n_rows, body, 0, unroll=8)


def _wait_rows(src_hbm, dst, sem):
    pltpu.make_async_copy(src_hbm.at[pl.ds(0, dst.shape[0])], dst, sem).wait()


def _late_zero(x):
    u = lax.bitcast_convert_type(x, jnp.uint32)
    return ((u >> 16) >> 16).astype(jnp.int32)[0, 0]


def _moe_kernel(be_ref, ne_ref, nu_ref, tok_ref, h_hbm, wg_hbm, wu_hbm, wd_hbm, y_ref,
                xg0, xg1, xg2, sg, su, sd, wgb, wub, wdb, sem, wsem):
    b = pl.program_id(0)
    n_used = nu_ref[0]
    last = n_used - 1
    active = b < n_used
    bufs = (xg0, xg1, xg2)
    n_buf = len(bufs)

    def weight_copies(e):
        return (pltpu.make_async_copy(wg_hbm.at[e], sg, wsem.at[0]),
                pltpu.make_async_copy(wu_hbm.at[e], su, wsem.at[1]),
                pltpu.make_async_copy(wd_hbm.at[e], sd, wsem.at[2]))

    @pl.when(b == 0)
    def _():
        for cp in weight_copies(be_ref[0]):
            cp.start(priority=1)
        for i in range(n_buf - 1):
            _gather_rows(tok_ref, jnp.minimum(i, last) * EXPERT_ROWS, EXPERT_ROWS, h_hbm, bufs[i],
                         sem.at[i], unrolled=False)

    new_expert = (b == 0) | (be_ref[b] != be_ref[jnp.maximum(b - 1, 0)])

    @pl.when(active & new_expert)
    def _():
        for cp in weight_copies(be_ref[b]):
            cp.wait()
        wgb[...] = sg[...].astype(BF16)
        wub[...] = su[...].astype(BF16)
        wdb[...] = sd[...].astype(BF16)

        @pl.when(ne_ref[b] >= 0)
        def _():
            for cp in weight_copies(ne_ref[b]):
                cp.start(priority=1)

    def step(slot):
        cur = bufs[slot]
        ahead = bufs[(slot + n_buf - 1) % n_buf]
        sem_ahead = sem.at[(slot + n_buf - 1) % n_buf]
        _wait_rows(h_hbm, cur, sem.at[slot])
        base = jnp.minimum(b + n_buf - 1, last) * EXPERT_ROWS
        per = EXPERT_ROWS // GATHER_GROUPS
        group = iter(range(GATHER_GROUPS))

        def issue(zero):
            g = next(group)
            _gather_rows(tok_ref, base + g * per + zero, per, h_hbm, ahead, sem_ahead, unrolled=True,
                         both_queues=True, dst_row0=g * per)

        def corner(x, r, c):
            return _late_zero(x[r:r + 1, c:c + 1])

        bm, de = EXPERT_ROWS, D_EXPERT
        lo, hi = _unpack_bf16_pairs(_load_row_tiles(cur, bm))
        lo = lo.astype(BF16)
        hi = hi.astype(BF16)

        def issue_along(x, n):
            rows, cols = x.shape
            for i in range(n):
                issue(corner(x, (i + 1) * rows // n - 1, (i + 1) * cols // n - 1))

        n_each = GATHER_GROUPS // 8
        issue(0)
        a1 = _dot(lo, wgb[0:PACKED, :])
        issue_along(a1, n_each)
        a = a1 + _dot(hi, wgb[PACKED:D_MODEL, :])
        issue_along(a, n_each)
        u1 = _dot(lo, wub[0:PACKED, :])
        issue_along(u1, n_each)
        u = u1 + _dot(hi, wub[PACKED:D_MODEL, :])
        issue_along(u, n_each)
        act = (a / (1.0 + jnp.exp(-a)) * u).astype(BF16)
        y = _dot(act, wdb[...])
        issue_along(y, GATHER_GROUPS - 1 - 4 * n_each)
        _store_row_tiles(y_ref, _pack_bf16_pairs(y))

        @pl.when(b == last)
        def _():
            for i in range(1, n_buf):
                _wait_rows(h_hbm, bufs[(slot + i) % n_buf], sem.at[(slot + i) % n_buf])

    for slot in range(n_buf):
        @pl.when(active & (b % n_buf == slot))
        def _(slot=slot):
            step(slot)

    @pl.when(b >= n_used)
    def _():
        y_ref[...] = jnp.zeros_like(y_ref)


def _experts(block_expert, next_expert, n_used, buf_tok, h2_packed, w_gate, w_up, w_down):
    cap = buf_tok.shape[0]
    bm = EXPERT_ROWS
    xg = pltpu.VMEM((bm * ROW_TILE, LANES), jnp.uint32)
    return pl.pallas_call(
        _moe_kernel,
        grid_spec=pltpu.PrefetchScalarGridSpec(
            num_scalar_prefetch=4,
            grid=(cap // bm,),
            in_specs=[pl.BlockSpec(memory_space=pl.ANY)] * 4,
            out_specs=pl.BlockSpec((bm * ROW_TILE, LANES), lambda b, *_: (b, 0)),
            scratch_shapes=[
                xg, xg, xg,
                pltpu.VMEM((D_MODEL, D_EXPERT), F32), pltpu.VMEM((D_MODEL, D_EXPERT), F32),
                pltpu.VMEM((D_EXPERT, D_MODEL), F32),
                pltpu.VMEM((D_MODEL, D_EXPERT), BF16), pltpu.VMEM((D_MODEL, D_EXPERT), BF16),
                pltpu.VMEM((D_EXPERT, D_MODEL), BF16),
                pltpu.SemaphoreType.DMA((3,)), pltpu.SemaphoreType.DMA((3,)),
            ],
        ),
        out_shape=jax.ShapeDtypeStruct((cap * ROW_TILE, LANES), jnp.uint32),
        compiler_params=_params(1),
        name="experts",
    )(block_expert, next_expert, n_used, buf_tok, h2_packed, w_gate, w_up, w_down)


def _combine_kernel(slot_ref, rows_ref, y_hbm, x1_ref, wts_ref, mod_ref, g_ref, op_ref, os_ref, yg0, yg1, sem,
                    *, n_ctx_tiles, n_tiles):
    del rows_ref
    tm = TOKEN_TILE
    i = pl.program_id(0)
    n_slots = slot_ref.shape[0] // TOP_K

    def start(tile, buf, s, unrolled):
        for k in range(TOP_K):
            _gather_rows(slot_ref, k * n_slots + tile * tm, tm, y_hbm, buf.at[k], s, unrolled=unrolled,
                         both_queues=True)

    @pl.when(i == 0)
    def _():
        start(0, yg0, sem.at[0], False)

    def step(cur, nxt, sem_cur, sem_nxt):
        for k in range(TOP_K):
            _wait_rows(y_hbm, cur.at[k], sem_cur)
        start(jnp.minimum(i + 1, n_tiles - 1), nxt, sem_nxt, True)
        w = wts_ref[...]
        lo0, hi0 = _unpack_bf16_pairs(_load_row_tiles(cur.at[0], tm))
        lo1, hi1 = _unpack_bf16_pairs(_load_row_tiles(cur.at[1], tm))
        ffn = jnp.concatenate([lo0 * w[:, 0:1] + lo1 * w[:, 1:2], hi0 * w[:, 0:1] + hi1 * w[:, 1:2]], axis=-1)
        x2 = x1_ref[...] + mod_ref[0][5:6] * ffn
        out = _rms(x2) * g_ref[...]

        @pl.when(i < n_ctx_tiles)
        def _():
            op_ref[...] = out

        @pl.when(i >= n_ctx_tiles)
        def _():
            os_ref[...] = out

        @pl.when(i == n_tiles - 1)
        def _():
            for k in range(TOP_K):
                _wait_rows(y_hbm, nxt.at[k], sem_nxt)

    @pl.when(i % 2 == 0)
    def _():
        step(yg0, yg1, sem.at[0], sem.at[1])

    @pl.when(i % 2 == 1)
    def _():
        step(yg1, yg0, sem.at[1], sem.at[0])


def _combine(slots, rows, y_buf, x1, wts, mod3, g_final, t_ctx):
    t = x1.shape[0]
    tm = TOKEN_TILE
    n_tiles = t // tm
    n_ctx = t_ctx // tm
    return pl.pallas_call(
        functools.partial(_combine_kernel, n_ctx_tiles=n_ctx, n_tiles=n_tiles),
        grid_spec=pltpu.PrefetchScalarGridSpec(
            num_scalar_prefetch=2,
            grid=(n_tiles,),
            in_specs=[
                pl.BlockSpec(memory_space=pl.ANY),
                pl.BlockSpec((tm, D_MODEL), lambda i, s, r: (i, 0)),
                pl.BlockSpec((tm, ROUTER_LANES), lambda i, s, r: (i, 0)),
                pl.BlockSpec((1, 6, D_MODEL), lambda i, s, r: (r[i], 0, 0)),
                pl.BlockSpec((1, D_MODEL), lambda i, s, r: (0, 0)),
            ],
            out_specs=(
                pl.BlockSpec((tm, D_MODEL), lambda i, s, r: (jnp.minimum(i, n_ctx - 1), 0)),
                pl.BlockSpec((tm, D_MODEL), lambda i, s, r: (jnp.maximum(i - n_ctx, 0), 0)),
            ),
            scratch_shapes=[pltpu.VMEM((TOP_K, tm * ROW_TILE, LANES), jnp.uint32),
                            pltpu.VMEM((TOP_K, tm * ROW_TILE, LANES), jnp.uint32),
                            pltpu.SemaphoreType.DMA((2,))],
        ),
        out_shape=(jax.ShapeDtypeStruct((t_ctx, D_MODEL), F32),
                   jax.ShapeDtypeStruct((t - t_ctx, D_MODEL), F32)),
        compiler_params=_params(1),
        name="combine",
    )(slots, rows, y_buf, x1, wts, mod3, g_final)


def _rope_tables(n_tokens):
    rows = n_tokens // GRID_W
    row = jnp.repeat(jnp.arange(rows, dtype=F32), GRID_W)
    col = jnp.tile(jnp.arange(GRID_W, dtype=F32), rows)
    half = ROPE_AXIS_DIM // 2
    inv_freq = ROPE_BASE ** (-jnp.arange(half, dtype=F32) / half)
    ar = row[:, None] * inv_freq
    ac = col[:, None] * inv_freq
    cos = jnp.concatenate([jnp.cos(ar), jnp.cos(ar), jnp.cos(ac), jnp.cos(ac)], axis=-1)
    sin = jnp.concatenate([-jnp.sin(ar), jnp.sin(ar), -jnp.sin(ac), jnp.sin(ac)], axis=-1)
    return cos, sin


def _invert_kernel(dest_ref, gap_lo_ref, gap_hi_ref, out_ref):
    group = 16
    cap = out_ref.shape[0]

    def clear_gap(e, carry):
        lo = gap_lo_ref[e]

        def clear(c, inner):
            for k in range(group):
                out_ref[jnp.minimum(lo + c * group + k, cap - 1)] = 0
            return inner
        lax.fori_loop(0, (gap_hi_ref[e] - lo + group - 1) // group, clear, 0)
        return carry
    lax.fori_loop(0, gap_lo_ref.shape[0], clear_gap, 0)

    n_tokens = dest_ref.shape[0] // TOP_K

    def place(i, carry):
        a0 = i * group
        tile0 = jnp.where(a0 >= n_tokens, a0 - n_tokens, a0) * ROW_TILE
        rows = [dest_ref[a0 + k] for k in range(group)]
        for k in range(group):
            out_ref[rows[k]] = tile0 + k * ROW_TILE
        return carry
    lax.fori_loop(0, dest_ref.shape[0] // group, place, 0)


def _invert_slots(dest, gap_lo, gap_hi, cap):
    return pl.pallas_call(
        _invert_kernel,
        grid_spec=pltpu.PrefetchScalarGridSpec(
            num_scalar_prefetch=3, grid=(1,), in_specs=[],
            out_specs=pl.BlockSpec(memory_space=pltpu.SMEM)),
        out_shape=jax.ShapeDtypeStruct((cap,), jnp.int32),
        compiler_params=_params(1),
        name="invert_slots",
    )(dest, gap_lo, gap_hi)


def _dispatch_tables(expert_ids):
    t = expert_ids[0].shape[0]
    assert t % 16 == 0, "the slot inversion places assignments 16 at a time"
    bm = EXPERT_ROWS
    flat_e = jnp.concatenate(expert_ids)
    onehot = (flat_e[:, None] == jnp.arange(N_EXPERTS, dtype=jnp.int32)[None, :]).astype(jnp.int32)
    running = jnp.cumsum(onehot, axis=0)
    rank = jnp.sum(running * onehot, axis=1) - 1
    counts = running[-1]
    padded = (counts + bm - 1) // bm * bm
    pad_end = jnp.cumsum(padded)
    pad_start = pad_end - padded
    dest = (jnp.sum(pad_start[None, :] * onehot, axis=1) + rank).astype(jnp.int32)
    n_blocks = (t * TOP_K + N_EXPERTS * (bm - 1) + bm - 1) // bm
    cap = n_blocks * bm
    gap_lo = jnp.concatenate([pad_start + counts, pad_end[-1:]]).astype(jnp.int32)
    gap_hi = jnp.concatenate([pad_end, jnp.full((1,), cap, pad_end.dtype)]).astype(jnp.int32)
    buf_tok = _invert_slots(dest, gap_lo, gap_hi, cap)
    n_used = (pad_end[-1] // bm).astype(jnp.int32)
    blk = jnp.arange(n_blocks, dtype=jnp.int32)
    blk = jnp.minimum(blk, n_used - 1)
    block_expert = jnp.sum((pad_end[None, :] <= (blk * bm)[:, None]).astype(jnp.int32), axis=1)
    block_expert = jnp.minimum(block_expert, N_EXPERTS - 1)
    e_ids = jnp.arange(N_EXPERTS, dtype=jnp.int32)
    later = (e_ids[None, :] > e_ids[:, None]) & (counts[None, :] > 0)
    next_of = jnp.min(jnp.where(later, e_ids[None, :], N_EXPERTS), axis=1)
    next_of = jnp.where(next_of == N_EXPERTS, -1, next_of)
    next_expert = jnp.sum(next_of[None, :] * (block_expert[:, None] == e_ids[None, :]), axis=1)
    return (block_expert.astype(jnp.int32), next_expert.astype(jnp.int32), n_used.reshape(1),
            buf_tok, dest * ROW_TILE)


def kernel(x_prompt, x_sample, cache_k, cache_v, c, c_ctx, w_mod, b_mod, norm_mix_g, norm_ffn_g, w_in,
           pool_w, pool_scale, attn_sink, out_norm_pool_g, out_norm_attn_g, w_out, router_group_w,
           router_group_b, router_expert_w, router_expert_b, w_gate, w_up, w_down, final_norm_g):
    depth = w_mod.shape[0]
    assert depth == 1, "single trunk layer"
    bp, lp, _ = x_prompt.shape
    bs, ls, _ = x_sample.shape
    tp, ts = bp * lp, bs * ls
    tm = TOKEN_TILE
    l = 0

    cond8 = jnp.zeros((8, D_MODEL), F32).at[:bs].set(c).at[bs].set(c_ctx)
    mod3 = _modulation(cond8, w_mod[l], b_mod[l]).reshape(8, 6, D_MODEL)
    rows_p = jnp.full((tp // tm,), bs, jnp.int32)
    rows_s = jnp.arange(ts // tm, dtype=jnp.int32) // (ls // tm)

    w_in_b = w_in[l].astype(BF16)
    w_out_b = w_out[l].astype(BF16)
    pool_w_b = pool_w[l].astype(BF16)
    g_mix = norm_mix_g[l].reshape(1, D_MODEL)
    g_ffn = norm_ffn_g[l].reshape(1, D_MODEL)
    g_pool = out_norm_pool_g[l].reshape(1, POOL_WIDTH)
    g_attn = out_norm_attn_g[l].reshape(1, ATTN_WIDTH)
    p_scale = pool_scale[l].reshape(1, POOL_WIDTH)
    sink_b = jnp.broadcast_to(attn_sink[l][:, None], (N_HEADS, LANES))
    g_final = final_norm_g.reshape(1, D_MODEL)

    rw = jnp.concatenate(
        [router_group_w[l], jnp.transpose(router_expert_w[l], (1, 0, 2)).reshape(D_MODEL, N_EXPERTS)], axis=1)
    rw = jnp.pad(rw, ((0, 0), (0, ROUTER_LANES - rw.shape[1])))
    rw_hi = rw.astype(BF16)
    rw_split = jnp.concatenate([rw_hi, (rw - rw_hi.astype(F32)).astype(BF16)], axis=1)
    rb = jnp.concatenate([router_group_b[l], router_expert_b[l].reshape(N_EXPERTS)])
    rb = jnp.pad(rb, (0, ROUTER_LANES - rb.shape[0])).reshape(1, ROUTER_LANES)

    xp = x_prompt.reshape(tp, D_MODEL)
    xs = x_sample.reshape(ts, D_MODEL)

    up_p, q_p, k_p, v_p = _in_projection(xp, rows_p, mod3, g_mix, w_in_b, None, F32)
    pool_p = _pool_mixer(up_p.reshape(bp, lp, POOL_WIDTH), pool_w_b, p_scale, g_pool).reshape(tp, POOL_WIDTH)
    attn_p = _context_attention(q_p, k_p, v_p, sink_b, g_attn, lp)
    up_s, q_s, k_s, v_s = _in_projection(xs, rows_s, mod3, g_mix, w_in_b, _rope_tables(ls), BF16)
    pool_s = _pool_mixer(up_s.reshape(bs, ls, POOL_WIDTH), pool_w_b, p_scale, g_pool).reshape(ts, POOL_WIDTH)
    attn_s = _latent_attention(
        q_s, k_s.reshape(bs, ls, KV_WIDTH), v_s.reshape(bs, ls, KV_WIDTH),
        cache_k[:, l].reshape(bs, -1, KV_WIDTH), cache_v[:, l].reshape(bs, -1, KV_WIDTH), sink_b, g_attn)

    rows_all = jnp.concatenate([rows_p, rows_s])
    x1_all, h2_all, ids_all, wts_all = _out_projection(
        (pool_p, attn_p, xp), (pool_s, attn_s, xs), rows_all, mod3, g_ffn, w_out_b, rw_split, rb)

    block_expert, next_expert, n_used, buf_tok, slots = _dispatch_tables(
        [ids_all[:, k] for k in range(TOP_K)])
    y_buf = _experts(block_expert, next_expert, n_used, buf_tok, h2_all, w_gate[l], w_up[l], w_down[l])

    y_p, y_s = _combine(slots, rows_all, y_buf, x1_all, wts_all, mod3, g_final, tp)

    kv_shape = (bp, 1, lp, N_KV_HEADS, HEAD_DIM)
    return (y_p.reshape(bp, lp, D_MODEL), y_s.reshape(bs, ls, D_MODEL),
            k_p.reshape(kv_shape), v_p.reshape(kv_shape))
```

```python
import functools

import jax
import jax.numpy as jnp
from jax import lax
from jax.experimental import pallas as pl
from jax.experimental.pallas import tpu as pltpu

F32 = jnp.float32
BF16 = jnp.bfloat16

D_MODEL = 2048
GRID_W = 64
HEAD_DIM = 128
ATTN_WIDTH = 1024
POOL_WIDTH = 1024
N_HEADS = 8
N_KV_HEADS = 2
Q_PER_KV = 4
KV_WIDTH = 256
IN_WIDTH = 2560
WINDOW = 128
POOL_WINDOWS = (2, 4, 8, 16)
POOL_GROUP_DIM = 256
ROPE_BASE = 10000.0
ROPE_AXIS_DIM = 64
N_EXPERT_GROUPS = 4
EXPERTS_PER_GROUP = 8
N_EXPERTS = 32
TOP_K = 2
D_EXPERT = 512
NORM_EPS = 1e-6
NEG_INF = -1e30
ATTN_SCALE = HEAD_DIM ** -0.5
LOG2E = 1.4426950408889634
Q_SCALE = ATTN_SCALE * LOG2E

LANES = 128
Q_BLOCKS_PER_STEP = 4
CTX_SEQS_PER_STEP = 2
VMEM_LIMIT = 48 * 1024 * 1024
PACKED = D_MODEL // 2
ROW_TILE = 8
assert PACKED == ROW_TILE * LANES, "a packed row must fill exactly one tile"
MOD_TILE = 1024
TOKEN_TILE = 256
IN_TILE = 512
POOL_TILES_PER_STEP = 2
POOL_HALO = 16
Q_BLOCK = 128
EXPERT_ROWS = 256
GATHER_GROUPS = 64
ROUTER_LANES = 128


def _params(n_grid):
    return pltpu.CompilerParams(
        dimension_semantics=("arbitrary",) * n_grid, vmem_limit_bytes=VMEM_LIMIT)


def _rms(x):
    return x * lax.rsqrt(jnp.mean(x * x, axis=-1, keepdims=True) + NORM_EPS)


def _dot(a, b):
    return jnp.dot(a, b, preferred_element_type=F32)


def _dot_nt(a, b):
    return lax.dot_general(a, b, (((1,), (1,)), ((), ())), preferred_element_type=F32)


def _pack_bf16_pairs(x):
    c = x.shape[1] // 2
    lo = lax.bitcast_convert_type(x[:, :c].astype(BF16).astype(F32), jnp.uint32)
    hi = lax.bitcast_convert_type(x[:, c:].astype(BF16).astype(F32), jnp.uint32)
    return hi | (lo >> 16)


def _store_row_tiles(ref, x):
    n = x.shape[0]
    for c in range(PACKED // LANES):
        ref[pl.ds(c, n, stride=ROW_TILE), :] = x[:, c * LANES:(c + 1) * LANES]


def _load_row_tiles(ref, n):
    return jnp.concatenate(
        [ref[pl.ds(c, n, stride=ROW_TILE), :] for c in range(PACKED // LANES)], axis=-1)


def _unpack_bf16_pairs(u):
    lo = lax.bitcast_convert_type(u << 16, F32)
    hi = lax.bitcast_convert_type(u & jnp.uint32(0xFFFF0000), F32)
    return lo, hi


def _mod_kernel(cond_ref, w_ref, b_ref, o_ref):
    c = cond_ref[...]
    s = c / (1.0 + jnp.exp(-c))
    o_ref[...] = _dot(s.astype(BF16), w_ref[...].astype(BF16)) + b_ref[...]


def _modulation(cond8, w_mod, b_mod):
    n = w_mod.shape[1]
    tn = MOD_TILE
    return pl.pallas_call(
        _mod_kernel,
        grid=(n // tn,),
        in_specs=[
            pl.BlockSpec((8, D_MODEL), lambda j: (0, 0)),
            pl.BlockSpec((D_MODEL, tn), lambda j: (0, j)),
            pl.BlockSpec((1, tn), lambda j: (0, j)),
        ],
        out_specs=pl.BlockSpec((8, tn), lambda j: (0, j)),
        out_shape=jax.ShapeDtypeStruct((8, n), F32),
        compiler_params=_params(1),
        name="modulation",
    )(cond8, w_mod, b_mod.reshape(1, n))


def _rope(x, cos, sin_signed):
    lane = lax.broadcasted_iota(jnp.int32, x.shape, 1)
    partner = jnp.where((lane & 63) < 32,
                        pltpu.roll(x, HEAD_DIM - 32, axis=1),
                        pltpu.roll(x, 32, axis=1))
    return x * cos + partner * sin_signed


def _inproj_kernel(rows_ref, x_ref, mod_ref, g_ref, w_ref, *rest, rope):
    del rows_ref
    if rope:
        cos_ref, sin_ref, up_ref, q_ref, k_ref, v_ref = rest
    else:
        up_ref, q_ref, k_ref, v_ref = rest
    m = mod_ref[0]
    h = _rms(x_ref[...]) * g_ref[...]
    hb = (h * (1.0 + m[1:2]) + m[0:1]).astype(BF16)
    up_ref[...] = _dot(hb, w_ref[:, 0:POOL_WIDTH]).astype(BF16)
    q = _dot(hb, w_ref[:, POOL_WIDTH:POOL_WIDTH + ATTN_WIDTH])
    k = _dot(hb, w_ref[:, POOL_WIDTH + ATTN_WIDTH:IN_WIDTH - KV_WIDTH])
    v = _dot(hb, w_ref[:, IN_WIDTH - KV_WIDTH:IN_WIDTH])
    if rope:
        cos = cos_ref[...]
        sin = sin_ref[...]
        for hd in range(N_HEADS):
            sl = slice(hd * HEAD_DIM, (hd + 1) * HEAD_DIM)
            q_ref[:, sl] = (_rope(q[:, sl], cos, sin) * Q_SCALE).astype(BF16)
        for hd in range(N_KV_HEADS):
            sl = slice(hd * HEAD_DIM, (hd + 1) * HEAD_DIM)
            k_ref[:, sl] = _rope(k[:, sl], cos, sin).astype(k_ref.dtype)
    else:
        q_ref[...] = (q * Q_SCALE).astype(BF16)
        k_ref[...] = k.astype(k_ref.dtype)
    v_ref[...] = v.astype(v_ref.dtype)


def _in_projection(x, rows, mod3, g_mix, w_in_b, rope_tabs, kv_dtype):
    t = x.shape[0]
    tm = IN_TILE
    per = IN_TILE // TOKEN_TILE
    rope = rope_tabs is not None
    in_specs = [
        pl.BlockSpec((tm, D_MODEL), lambda i, r: (i, 0)),
        pl.BlockSpec((1, 6, D_MODEL), lambda i, r: (r[i * per], 0, 0)),
        pl.BlockSpec((1, D_MODEL), lambda i, r: (0, 0)),
        pl.BlockSpec((D_MODEL, IN_WIDTH), lambda i, r: (0, 0), pipeline_mode=pl.Buffered(1)),
    ]
    args = [x, mod3, g_mix, w_in_b]
    if rope:
        seq_tiles = rope_tabs[0].shape[0] // tm
        for tab in rope_tabs:
            in_specs.append(pl.BlockSpec((tm, HEAD_DIM), lambda i, r: (i % seq_tiles, 0)))
            args.append(tab)
    out_shape = (
        jax.ShapeDtypeStruct((t, POOL_WIDTH), BF16),
        jax.ShapeDtypeStruct((t, ATTN_WIDTH), BF16),
        jax.ShapeDtypeStruct((t, KV_WIDTH), kv_dtype),
        jax.ShapeDtypeStruct((t, KV_WIDTH), kv_dtype),
    )
    out_specs = (
        pl.BlockSpec((tm, POOL_WIDTH), lambda i, r: (i, 0)),
        pl.BlockSpec((tm, ATTN_WIDTH), lambda i, r: (i, 0)),
        pl.BlockSpec((tm, KV_WIDTH), lambda i, r: (i, 0)),
        pl.BlockSpec((tm, KV_WIDTH), lambda i, r: (i, 0)),
    )
    return pl.pallas_call(
        functools.partial(_inproj_kernel, rope=rope),
        grid_spec=pltpu.PrefetchScalarGridSpec(
            num_scalar_prefetch=1, grid=(t // tm,), in_specs=in_specs, out_specs=out_specs),
        out_shape=out_shape,
        compiler_params=_params(1),
        name="in_projection_rope" if rope else "in_projection",
    )(rows, *args)


def _pool_kernel(u_ref, pw_ref, ps_ref, g_ref, o_ref, *, seq_len):
    tq = TOKEN_TILE
    win = min(seq_len, tq + 2 * POOL_HALO)
    for sub in range(o_ref.shape[1] // tq):
        t0 = pl.program_id(1) * o_ref.shape[1] + sub * tq
        src0 = pl.multiple_of(jnp.clip(t0 - POOL_HALO, 0, seq_len - win), POOL_HALO)
        u = u_ref[0, pl.ds(src0, win), :]
        t = t0 + lax.broadcasted_iota(jnp.int32, (tq, win), 0)
        j = src0 + lax.broadcasted_iota(jnp.int32, (tq, win), 1)
        tc = t0 + lax.broadcasted_iota(jnp.int32, (tq, 1), 0)
        ys = []
        for gi, w in enumerate(POOL_WINDOWS):
            lo = jnp.maximum(t - w // 2, 0)
            hi = jnp.minimum(t + (w - w // 2), seq_len)
            cnt = (hi - lo).astype(F32)
            a = jnp.where((j >= lo) & (j < hi), 1.0, 0.0) - jnp.where(j == t, cnt, 0.0)
            cnt_col = (jnp.minimum(tc + (w - w // 2), seq_len) - jnp.maximum(tc - w // 2, 0)).astype(F32)
            sl = slice(gi * POOL_GROUP_DIM, (gi + 1) * POOL_GROUP_DIM)
            d = _dot(a.astype(BF16), u[:, sl]) / cnt_col
            ys.append(_dot(d.astype(BF16), pw_ref[gi]))
        y = jnp.concatenate(ys, axis=-1) * ps_ref[...]
        o_ref[0, sub * tq:(sub + 1) * tq, :] = (_rms(y) * g_ref[...]).astype(BF16)


def _pool_mixer(up, pool_w_b, pool_scale, g_out_pool):
    b, seq_len, _ = up.shape
    tq = TOKEN_TILE * min(POOL_TILES_PER_STEP, seq_len // TOKEN_TILE)
    return pl.pallas_call(
        functools.partial(_pool_kernel, seq_len=seq_len),
        grid=(b, seq_len // tq),
        in_specs=[
            pl.BlockSpec((1, seq_len, POOL_WIDTH), lambda bi, i: (bi, 0, 0)),
            pl.BlockSpec((len(POOL_WINDOWS), POOL_GROUP_DIM, POOL_GROUP_DIM), lambda bi, i: (0, 0, 0)),
            pl.BlockSpec((1, POOL_WIDTH), lambda bi, i: (0, 0)),
            pl.BlockSpec((1, POOL_WIDTH), lambda bi, i: (0, 0)),
        ],
        out_specs=pl.BlockSpec((1, tq, POOL_WIDTH), lambda bi, i: (bi, i, 0)),
        out_shape=jax.ShapeDtypeStruct((b, seq_len, POOL_WIDTH), BF16),
        compiler_params=_params(2),
        name=f"pool_mixer_{seq_len}",
    )(up, pool_w_b, pool_scale, g_out_pool)


def _stack_heads(q, kh, rows):
    return jnp.concatenate(
        [q[:, (kh * Q_PER_KV + g) * HEAD_DIM:(kh * Q_PER_KV + g + 1) * HEAD_DIM]
         for g in range(Q_PER_KV)], axis=0)


def _sink_column(sink_ref, kh, rows):
    return jnp.concatenate(
        [jnp.broadcast_to(sink_ref[kh * Q_PER_KV + g:kh * Q_PER_KV + g + 1, 0:1] * LOG2E, (rows, 1))
         for g in range(Q_PER_KV)], axis=0)


def _attend(q4, key_sets, value_sets, sk, band=None):
    s = [_dot_nt(q4, k) for k in key_sets]
    if band is not None:
        s[0] = jnp.where(band, s[0], NEG_INF)
    m = sk
    for si in s:
        m = jnp.maximum(m, jnp.max(si, axis=-1, keepdims=True))
    acc = None
    for si, v in zip(s, value_sets):
        v1 = jnp.concatenate([v, jnp.ones_like(v)], axis=-1)
        part = _dot(jnp.exp2(si - m).astype(BF16), v1)
        acc = part if acc is None else acc + part
    den = acc[:, HEAD_DIM:] + jnp.exp2(sk - m)
    return acc[:, :HEAD_DIM] / den


def _ctx_attn_kernel(q_ref, k_ref, v_ref, sink_ref, g_ref, o_ref, *, seq_len):
    rows = seq_len
    for j in range(q_ref.shape[0] // rows):
        rs = slice(j * rows, (j + 1) * rows)
        q = q_ref[rs, :]
        heads = [None] * N_HEADS
        for kh in range(N_KV_HEADS):
            sl = slice(kh * HEAD_DIM, (kh + 1) * HEAD_DIM)
            o = _attend(_stack_heads(q, kh, rows), [k_ref[rs, sl].astype(BF16)],
                        [v_ref[rs, sl].astype(BF16)], _sink_column(sink_ref, kh, rows))
            for g in range(Q_PER_KV):
                heads[kh * Q_PER_KV + g] = o[g * rows:(g + 1) * rows]
        y = jnp.concatenate(heads, axis=-1)
        o_ref[rs, :] = (_rms(y) * g_ref[...]).astype(BF16)


def _context_attention(q, k, v, sink_b, g_out_attn, seq_len):
    t = q.shape[0]
    blk = seq_len * CTX_SEQS_PER_STEP
    return pl.pallas_call(
        functools.partial(_ctx_attn_kernel, seq_len=seq_len),
        grid=(t // blk,),
        in_specs=[
            pl.BlockSpec((blk, ATTN_WIDTH), lambda b: (b, 0)),
            pl.BlockSpec((blk, KV_WIDTH), lambda b: (b, 0)),
            pl.BlockSpec((blk, KV_WIDTH), lambda b: (b, 0)),
            pl.BlockSpec((N_HEADS, LANES), lambda b: (0, 0)),
            pl.BlockSpec((1, ATTN_WIDTH), lambda b: (0, 0)),
        ],
        out_specs=pl.BlockSpec((blk, ATTN_WIDTH), lambda b: (b, 0)),
        out_shape=jax.ShapeDtypeStruct((t, ATTN_WIDTH), BF16),
        compiler_params=_params(1),
        name="context_attention",
    )(q, k, v, sink_b, g_out_attn)


def _lat_attn_kernel(q_ref, k_ref, v_ref, ck_ref, cv_ref, sink_ref, g_ref, o_ref, *, seq_len):
    rows = Q_BLOCK
    span = 3 * Q_BLOCK
    shape = (Q_PER_KV * rows, span)
    row_in_block = lax.broadcasted_iota(jnp.int32, shape, 0) & (rows - 1)
    col = lax.broadcasted_iota(jnp.int32, shape, 1)
    for j in range(Q_BLOCKS_PER_STEP):
        q0 = (pl.program_id(1) * Q_BLOCKS_PER_STEP + j) * rows
        start = pl.multiple_of(jnp.clip(q0 - Q_BLOCK, 0, seq_len - span), Q_BLOCK)
        q = q_ref[j * rows:(j + 1) * rows, :]
        kl = k_ref[0, pl.ds(start, span), :]
        vl = v_ref[0, pl.ds(start, span), :]
        band = jnp.abs(row_in_block - col + (q0 - start)) <= WINDOW
        heads = [None] * N_HEADS
        for kh in range(N_KV_HEADS):
            sl = slice(kh * HEAD_DIM, (kh + 1) * HEAD_DIM)
            o = _attend(_stack_heads(q, kh, rows),
                        [kl[:, sl], ck_ref[0, :, sl].astype(BF16)],
                        [vl[:, sl], cv_ref[0, :, sl].astype(BF16)],
                        _sink_column(sink_ref, kh, rows), band)
            for g in range(Q_PER_KV):
                heads[kh * Q_PER_KV + g] = o[g * rows:(g + 1) * rows]
        y = jnp.concatenate(heads, axis=-1)
        o_ref[j * rows:(j + 1) * rows, :] = (_rms(y) * g_ref[...]).astype(BF16)


def _latent_attention(q, k, v, cache_k, cache_v, sink_b, g_out_attn):
    b, seq_len, _ = k.shape
    q_rows = Q_BLOCK * Q_BLOCKS_PER_STEP
    nq = seq_len // q_rows
    past = cache_k.shape[1]
    return pl.pallas_call(
        functools.partial(_lat_attn_kernel, seq_len=seq_len),
        grid=(b, nq),
        in_specs=[
            pl.BlockSpec((q_rows, ATTN_WIDTH), lambda bi, n: (bi * nq + n, 0)),
            pl.BlockSpec((1, seq_len, KV_WIDTH), lambda bi, n: (bi, 0, 0)),
            pl.BlockSpec((1, seq_len, KV_WIDTH), lambda bi, n: (bi, 0, 0)),
            pl.BlockSpec((1, past, KV_WIDTH), lambda bi, n: (bi, 0, 0)),
            pl.BlockSpec((1, past, KV_WIDTH), lambda bi, n: (bi, 0, 0)),
            pl.BlockSpec((N_HEADS, LANES), lambda bi, n: (0, 0)),
            pl.BlockSpec((1, ATTN_WIDTH), lambda bi, n: (0, 0)),
        ],
        out_specs=pl.BlockSpec((q_rows, ATTN_WIDTH), lambda bi, n: (bi * nq + n, 0)),
        out_shape=jax.ShapeDtypeStruct((b * seq_len, ATTN_WIDTH), BF16),
        compiler_params=_params(2),
        name="latent_attention",
    )(q, k, v, cache_k, cache_v, sink_b, g_out_attn)


def _route(logits):
    lane = lax.broadcasted_iota(jnp.int32, logits.shape, 1).astype(F32)
    neg = -jnp.inf

    def first_argmax(x):
        mx = jnp.max(x, axis=-1, keepdims=True)
        return mx, jnp.min(jnp.where(x == mx, lane, float(ROUTER_LANES)), axis=-1, keepdims=True)

    gl = jnp.where(lane < N_EXPERT_GROUPS, logits, neg)
    gmax, g_idx = first_argmax(gl)
    p_g = 1.0 / jnp.sum(jnp.exp(gl - gmax), axis=-1, keepdims=True)
    base = N_EXPERT_GROUPS + EXPERTS_PER_GROUP * g_idx
    el = jnp.where((lane >= base) & (lane < base + EXPERTS_PER_GROUP), logits, neg)
    v1, i1 = first_argmax(el)
    v2, i2 = first_argmax(jnp.where(lane == i1, neg, el))
    e2 = jnp.exp(v2 - v1)
    w1 = p_g / (1.0 + e2)
    w2 = p_g * e2 / (1.0 + e2)
    ids = jnp.where(lane == 0.0, i1 - N_EXPERT_GROUPS, jnp.where(lane == 1.0, i2 - N_EXPERT_GROUPS, 0.0))
    wts = jnp.where(lane == 0.0, w1, jnp.where(lane == 1.0, w2, 0.0))
    return ids.astype(jnp.int32), wts


def _outproj_kernel(rows_ref, pool_p, attn_p, x_p, pool_s, attn_s, x_s, mod_ref, g_ref, w_ref, rw_ref,
                    rb_ref, x1_ref, h2_ref, ids_ref, wts_ref, mix0, mix1, *, n_ctx_tiles, n_tiles):
    del rows_ref
    i = pl.program_id(0)

    def project(pool_ref, attn_ref, mix_ref):
        mix_ref[...] = (_dot(pool_ref[...], w_ref[0:POOL_WIDTH, :])
                        + _dot(attn_ref[...], w_ref[POOL_WIDTH:POOL_WIDTH + ATTN_WIDTH, :]))

    def tail(x_ref, mix_ref):
        m = mod_ref[0]
        x1 = x_ref[...] + m[2:3] * mix_ref[...]
        x1_ref[...] = x1
        h2 = _rms(x1) * g_ref[...] * (1.0 + m[4:5]) + m[3:4]
        _store_row_tiles(h2_ref, _pack_bf16_pairs(h2))
        hi = h2.astype(BF16)
        lo = (h2 - hi.astype(F32)).astype(BF16)
        both = _dot(hi, rw_ref[...])
        logits = (both[:, :ROUTER_LANES] + both[:, ROUTER_LANES:] + _dot(lo, rw_ref[:, :ROUTER_LANES])
                  + rb_ref[...])
        ids, wts = _route(logits)
        ids_ref[...] = ids
        wts_ref[...] = wts

    ctx_in = (pool_p, attn_p)
    lat_in = (pool_s, attn_s)
    for parity, (cur, prev) in enumerate(((mix0, mix1), (mix1, mix0))):
        mine = (i % 2) == parity

        @pl.when(mine & (i == 0))
        def _(cur=cur):
            project(*ctx_in, cur)

        @pl.when(mine & (i >= 1) & (i < n_ctx_tiles))
        def _(cur=cur, prev=prev):
            tail(x_p, prev)
            project(*ctx_in, cur)

        @pl.when(mine & (i == n_ctx_tiles))
        def _(cur=cur, prev=prev):
            tail(x_p, prev)
            project(*lat_in, cur)

        @pl.when(mine & (i > n_ctx_tiles) & (i < n_tiles))
        def _(cur=cur, prev=prev):
            tail(x_s, prev)
            project(*lat_in, cur)

        @pl.when(mine & (i == n_tiles))
        def _(prev=prev):
            tail(x_s, prev)


def _out_projection(ctx, lat, rows, mod3, g_ffn, w_out_b, rw_split, rb):
    tm = TOKEN_TILE
    n_ctx = ctx[2].shape[0] // tm
    n_lat = lat[2].shape[0] // tm
    n_tiles = n_ctx + n_lat
    t_all = n_tiles * tm
    assert n_ctx >= 1 and n_lat >= 1

    def ctx_map(i, r):
        return (jnp.minimum(i, n_ctx - 1), 0)

    def lat_map(i, r):
        return (jnp.clip(i - n_ctx, 0, n_lat - 1), 0)

    def ctx_prev_map(i, r):
        return (jnp.clip(i - 1, 0, n_ctx - 1), 0)

    def lat_prev_map(i, r):
        return (jnp.clip(i - 1 - n_ctx, 0, n_lat - 1), 0)

    def prev_tile(i):
        return jnp.maximum(i - 1, 0)

    in_specs = [
        pl.BlockSpec((tm, POOL_WIDTH), ctx_map),
        pl.BlockSpec((tm, ATTN_WIDTH), ctx_map),
        pl.BlockSpec((tm, D_MODEL), ctx_prev_map),
        pl.BlockSpec((tm, POOL_WIDTH), lat_map),
        pl.BlockSpec((tm, ATTN_WIDTH), lat_map),
        pl.BlockSpec((tm, D_MODEL), lat_prev_map),
        pl.BlockSpec((1, 6, D_MODEL), lambda i, r: (r[prev_tile(i)], 0, 0)),
        pl.BlockSpec((1, D_MODEL), lambda i, r: (0, 0)),
        pl.BlockSpec((D_MODEL, D_MODEL), lambda i, r: (0, 0), pipeline_mode=pl.Buffered(1)),
        pl.BlockSpec((D_MODEL, 2 * ROUTER_LANES), lambda i, r: (0, 0)),
        pl.BlockSpec((1, ROUTER_LANES), lambda i, r: (0, 0)),
    ]
    out_shape = (
        jax.ShapeDtypeStruct((t_all, D_MODEL), F32),
        jax.ShapeDtypeStruct((t_all * ROW_TILE, LANES), jnp.uint32),
        jax.ShapeDtypeStruct((t_all, ROUTER_LANES), jnp.int32),
        jax.ShapeDtypeStruct((t_all, ROUTER_LANES), F32),
    )
    out_specs = (
        pl.BlockSpec((tm, D_MODEL), lambda i, r: (prev_tile(i), 0)),
        pl.BlockSpec((tm * ROW_TILE, LANES), lambda i, r: (prev_tile(i), 0)),
        pl.BlockSpec((tm, ROUTER_LANES), lambda i, r: (prev_tile(i), 0)),
        pl.BlockSpec((tm, ROUTER_LANES), lambda i, r: (prev_tile(i), 0)),
    )
    return pl.pallas_call(
        functools.partial(_outproj_kernel, n_ctx_tiles=n_ctx, n_tiles=n_tiles),
        grid_spec=pltpu.PrefetchScalarGridSpec(
            num_scalar_prefetch=1, grid=(n_tiles + 1,), in_specs=in_specs, out_specs=out_specs,
            scratch_shapes=[pltpu.VMEM((tm, D_MODEL), F32), pltpu.VMEM((tm, D_MODEL), F32)]),
        out_shape=out_shape,
        compiler_params=_params(1),
        name="out_projection",
    )(rows, *ctx, *lat, mod3, g_ffn, w_out_b, rw_split, rb)


def _gather_rows(idx_ref, base, n_rows, src_hbm, dst, sem, *, unrolled, both_queues=False, dst_row0=0):
    def one(r, priority):
        src_row = pl.multiple_of(idx_ref[base + r], ROW_TILE)
        pltpu.make_async_copy(
            src_hbm.at[pl.ds(src_row, ROW_TILE)], dst.at[pl.ds((dst_row0 + r) * ROW_TILE, ROW_TILE)],
            sem).start(priority=priority)

    if unrolled:
        for r in range(n_rows):
            one(r, r % 2 if both_queues else 0)
    else:
        def body(r, carry):
            one(r, 0)
            return carry
        lax.fori_loop(0, n_rows, body, 0, unroll=8)


def _wait_rows(src_hbm, dst, sem):
    pltpu.make_async_copy(src_hbm.at[pl.ds(0, dst.shape[0])], dst, sem).wait()


def _late_zero(x):
    u = lax.bitcast_convert_type(x, jnp.uint32)
    return ((u >> 16) >> 16).astype(jnp.int32)[0, 0]


def _moe_kernel(be_ref, ne_ref, nu_ref, tok_ref, h_hbm, wg_hbm, wu_hbm, wd_hbm, y_ref,
                xg0, xg1, xg2, sg, su, sd, wgb, wub, wdb, sem, wsem):
    b = pl.program_id(0)
    n_used = nu_ref[0]
    last = n_used - 1
    active = b < n_used
    bufs = (xg0, xg1, xg2)
    n_buf = len(bufs)

    def weight_copies(e):
        return (pltpu.make_async_copy(wg_hbm.at[e], sg, wsem.at[0]),
                pltpu.make_async_copy(wu_hbm.at[e], su, wsem.at[1]),
                pltpu.make_async_copy(wd_hbm.at[e], sd, wsem.at[2]))

    @pl.when(b == 0)
    def _():
        for cp in weight_copies(be_ref[0]):
            cp.start(priority=1)
        for i in range(n_buf - 1):
            _gather_rows(tok_ref, jnp.minimum(i, last) * EXPERT_ROWS, EXPERT_ROWS, h_hbm, bufs[i],
                         sem.at[i], unrolled=False)

    new_expert = (b == 0) | (be_ref[b] != be_ref[jnp.maximum(b - 1, 0)])

    @pl.when(active & new_expert)
    def _():
        for cp in weight_copies(be_ref[b]):
            cp.wait()
        wgb[...] = sg[...].astype(BF16)
        wub[...] = su[...].astype(BF16)
        wdb[...] = sd[...].astype(BF16)

        @pl.when(ne_ref[b] >= 0)
        def _():
            for cp in weight_copies(ne_ref[b]):
                cp.start(priority=1)

    def step(slot):
        cur = bufs[slot]
        ahead = bufs[(slot + n_buf - 1) % n_buf]
        sem_ahead = sem.at[(slot + n_buf - 1) % n_buf]
        _wait_rows(h_hbm, cur, sem.at[slot])
        base = jnp.minimum(b + n_buf - 1, last) * EXPERT_ROWS
        per = EXPERT_ROWS // GATHER_GROUPS
        group = iter(range(GATHER_GROUPS))

        def issue(zero):
            g = next(group)
            _gather_rows(tok_ref, base + g * per + zero, per, h_hbm, ahead, sem_ahead, unrolled=True,
                         both_queues=True, dst_row0=g * per)

        def corner(x, r, c):
            return _late_zero(x[r:r + 1, c:c + 1])

        bm, de = EXPERT_ROWS, D_EXPERT
        lo, hi = _unpack_bf16_pairs(_load_row_tiles(cur, bm))
        lo = lo.astype(BF16)
        hi = hi.astype(BF16)

        def issue_along(x, n):
            rows, cols = x.shape
            for i in range(n):
                issue(corner(x, (i + 1) * rows // n - 1, (i + 1) * cols // n - 1))

        n_each = GATHER_GROUPS // 8
        issue(0)
        a1 = _dot(lo, wgb[0:PACKED, :])
        issue_along(a1, n_each)
        a = a1 + _dot(hi, wgb[PACKED:D_MODEL, :])
        issue_along(a, n_each)
        u1 = _dot(lo, wub[0:PACKED, :])
        issue_along(u1, n_each)
        u = u1 + _dot(hi, wub[PACKED:D_MODEL, :])
        issue_along(u, n_each)
        act = (a / (1.0 + jnp.exp(-a)) * u).astype(BF16)
        y = _dot(act, wdb[...])
        issue_along(y, GATHER_GROUPS - 1 - 4 * n_each)
        _store_row_tiles(y_ref, _pack_bf16_pairs(y))

        @pl.when(b == last)
        def _():
            for i in range(1, n_buf):
                _wait_rows(h_hbm, bufs[(slot + i) % n_buf], sem.at[(slot + i) % n_buf])

    for slot in range(n_buf):
        @pl.when(active & (b % n_buf == slot))
        def _(slot=slot):
            step(slot)

    @pl.when(b >= n_used)
    def _():
        y_ref[...] = jnp.zeros_like(y_ref)


def _experts(block_expert, next_expert, n_used, buf_tok, h2_packed, w_gate, w_up, w_down):
    cap = buf_tok.shape[0]
    bm = EXPERT_ROWS
    xg = pltpu.VMEM((bm * ROW_TILE, LANES), jnp.uint32)
    return pl.pallas_call(
        _moe_kernel,
        grid_spec=pltpu.PrefetchScalarGridSpec(
            num_scalar_prefetch=4,
            grid=(cap // bm,),
            in_specs=[pl.BlockSpec(memory_space=pl.ANY)] * 4,
            out_specs=pl.BlockSpec((bm * ROW_TILE, LANES), lambda b, *_: (b, 0)),
            scratch_shapes=[
                xg, xg, xg,
                pltpu.VMEM((D_MODEL, D_EXPERT), F32), pltpu.VMEM((D_MODEL, D_EXPERT), F32),
                pltpu.VMEM((D_EXPERT, D_MODEL), F32),
                pltpu.VMEM((D_MODEL, D_EXPERT), BF16), pltpu.VMEM((D_MODEL, D_EXPERT), BF16),
                pltpu.VMEM((D_EXPERT, D_MODEL), BF16),
                pltpu.SemaphoreType.DMA((3,)), pltpu.SemaphoreType.DMA((3,)),
            ],
        ),
        out_shape=jax.ShapeDtypeStruct((cap * ROW_TILE, LANES), jnp.uint32),
        compiler_params=_params(1),
        name="experts",
    )(block_expert, next_expert, n_used, buf_tok, h2_packed, w_gate, w_up, w_down)


def _combine_kernel(slot_ref, rows_ref, y_hbm, x1_ref, wts_ref, mod_ref, g_ref, op_ref, os_ref, yg0, yg1, sem,
                    *, n_ctx_tiles, n_tiles):
    del rows_ref
    tm = TOKEN_TILE
    i = pl.program_id(0)
    n_slots = slot_ref.shape[0] // TOP_K

    def start(tile, buf, s, unrolled):
        for k in range(TOP_K):
            _gather_rows(slot_ref, k * n_slots + tile * tm, tm, y_hbm, buf.at[k], s, unrolled=unrolled,
                         both_queues=True)

    @pl.when(i == 0)
    def _():
        start(0, yg0, sem.at[0], False)

    def step(cur, nxt, sem_cur, sem_nxt):
        for k in range(TOP_K):
            _wait_rows(y_hbm, cur.at[k], sem_cur)
        start(jnp.minimum(i + 1, n_tiles - 1), nxt, sem_nxt, True)
        w = wts_ref[...]
        lo0, hi0 = _unpack_bf16_pairs(_load_row_tiles(cur.at[0], tm))
        lo1, hi1 = _unpack_bf16_pairs(_load_row_tiles(cur.at[1], tm))
        ffn = jnp.concatenate([lo0 * w[:, 0:1] + lo1 * w[:, 1:2], hi0 * w[:, 0:1] + hi1 * w[:, 1:2]], axis=-1)
        x2 = x1_ref[...] + mod_ref[0][5:6] * ffn
        out = _rms(x2) * g_ref[...]

        @pl.when(i < n_ctx_tiles)
        def _():
            op_ref[...] = out

        @pl.when(i >= n_ctx_tiles)
        def _():
            os_ref[...] = out

        @pl.when(i == n_tiles - 1)
        def _():
            for k in range(TOP_K):
                _wait_rows(y_hbm, nxt.at[k], sem_nxt)

    @pl.when(i % 2 == 0)
    def _():
        step(yg0, yg1, sem.at[0], sem.at[1])

    @pl.when(i % 2 == 1)
    def _():
        step(yg1, yg0, sem.at[1], sem.at[0])


def _combine(slots, rows, y_buf, x1, wts, mod3, g_final, t_ctx):
    t = x1.shape[0]
    tm = TOKEN_TILE
    n_tiles = t // tm
    n_ctx = t_ctx // tm
    return pl.pallas_call(
        functools.partial(_combine_kernel, n_ctx_tiles=n_ctx, n_tiles=n_tiles),
        grid_spec=pltpu.PrefetchScalarGridSpec(
            num_scalar_prefetch=2,
            grid=(n_tiles,),
            in_specs=[
                pl.BlockSpec(memory_space=pl.ANY),
                pl.BlockSpec((tm, D_MODEL), lambda i, s, r: (i, 0)),
                pl.BlockSpec((tm, ROUTER_LANES), lambda i, s, r: (i, 0)),
                pl.BlockSpec((1, 6, D_MODEL), lambda i, s, r: (r[i], 0, 0)),
                pl.BlockSpec((1, D_MODEL), lambda i, s, r: (0, 0)),
            ],
            out_specs=(
                pl.BlockSpec((tm, D_MODEL), lambda i, s, r: (jnp.minimum(i, n_ctx - 1), 0)),
                pl.BlockSpec((tm, D_MODEL), lambda i, s, r: (jnp.maximum(i - n_ctx, 0), 0)),
            ),
            scratch_shapes=[pltpu.VMEM((TOP_K, tm * ROW_TILE, LANES), jnp.uint32),
                            pltpu.VMEM((TOP_K, tm * ROW_TILE, LANES), jnp.uint32),
                            pltpu.SemaphoreType.DMA((2,))],
        ),
        out_shape=(jax.ShapeDtypeStruct((t_ctx, D_MODEL), F32),
                   jax.ShapeDtypeStruct((t - t_ctx, D_MODEL), F32)),
        compiler_params=_params(1),
        name="combine",
    )(slots, rows, y_buf, x1, wts, mod3, g_final)


def _rope_tables(n_tokens):
    rows = n_tokens // GRID_W
    row = jnp.repeat(jnp.arange(rows, dtype=F32), GRID_W)
    col = jnp.tile(jnp.arange(GRID_W, dtype=F32), rows)
    half = ROPE_AXIS_DIM // 2
    inv_freq = ROPE_BASE ** (-jnp.arange(half, dtype=F32) / half)
    ar = row[:, None] * inv_freq
    ac = col[:, None] * inv_freq
    cos = jnp.concatenate([jnp.cos(ar), jnp.cos(ar), jnp.cos(ac), jnp.cos(ac)], axis=-1)
    sin = jnp.concatenate([-jnp.sin(ar), jnp.sin(ar), -jnp.sin(ac), jnp.sin(ac)], axis=-1)
    return cos, sin


def _invert_kernel(dest_ref, gap_lo_ref, gap_hi_ref, out_ref):
    group = 16
    cap = out_ref.shape[0]

    def clear_gap(e, carry):
        lo = gap_lo_ref[e]

        def clear(c, inner):
            for k in range(group):
                out_ref[jnp.minimum(lo + c * group + k, cap - 1)] = 0
            return inner
        lax.fori_loop(0, (gap_hi_ref[e] - lo + group - 1) // group, clear, 0)
        return carry
    lax.fori_loop(0, gap_lo_ref.shape[0], clear_gap, 0)

    n_tokens = dest_ref.shape[0] // TOP_K

    def place(i, carry):
        a0 = i * group
        tile0 = jnp.where(a0 >= n_tokens, a0 - n_tokens, a0) * ROW_TILE
        rows = [dest_ref[a0 + k] for k in range(group)]
        for k in range(group):
            out_ref[rows[k]] = tile0 + k * ROW_TILE
        return carry
    lax.fori_loop(0, dest_ref.shape[0] // group, place, 0)


def _invert_slots(dest, gap_lo, gap_hi, cap):
    return pl.pallas_call(
        _invert_kernel,
        grid_spec=pltpu.PrefetchScalarGridSpec(
            num_scalar_prefetch=3, grid=(1,), in_specs=[],
            out_specs=pl.BlockSpec(memory_space=pltpu.SMEM)),
        out_shape=jax.ShapeDtypeStruct((cap,), jnp.int32),
        compiler_params=_params(1),
        name="invert_slots",
    )(dest, gap_lo, gap_hi)


def _dispatch_tables(expert_ids):
    t = expert_ids[0].shape[0]
    assert t % 16 == 0, "the slot inversion places assignments 16 at a time"
    bm = EXPERT_ROWS
    flat_e = jnp.concatenate(expert_ids)
    onehot = (flat_e[:, None] == jnp.arange(N_EXPERTS, dtype=jnp.int32)[None, :]).astype(jnp.int32)
    running = jnp.cumsum(onehot, axis=0)
    rank = jnp.sum(running * onehot, axis=1) - 1
    counts = running[-1]
    padded = (counts + bm - 1) // bm * bm
    pad_end = jnp.cumsum(padded)
    pad_start = pad_end - padded
    dest = (jnp.sum(pad_start[None, :] * onehot, axis=1) + rank).astype(jnp.int32)
    n_blocks = (t * TOP_K + N_EXPERTS * (bm - 1) + bm - 1) // bm
    cap = n_blocks * bm
    gap_lo = jnp.concatenate([pad_start + counts, pad_end[-1:]]).astype(jnp.int32)
    gap_hi = jnp.concatenate([pad_end, jnp.full((1,), cap, pad_end.dtype)]).astype(jnp.int32)
    buf_tok = _invert_slots(dest, gap_lo, gap_hi, cap)
    n_used = (pad_end[-1] // bm).astype(jnp.int32)
    blk = jnp.arange(n_blocks, dtype=jnp.int32)
    blk = jnp.minimum(blk, n_used - 1)
    block_expert = jnp.sum((pad_end[None, :] <= (blk * bm)[:, None]).astype(jnp.int32), axis=1)
    block_expert = jnp.minimum(block_expert, N_EXPERTS - 1)
    e_ids = jnp.arange(N_EXPERTS, dtype=jnp.int32)
    later = (e_ids[None, :] > e_ids[:, None]) & (counts[None, :] > 0)
    next_of = jnp.min(jnp.where(later, e_ids[None, :], N_EXPERTS), axis=1)
    next_of = jnp.where(next_of == N_EXPERTS, -1, next_of)
    next_expert = jnp.sum(next_of[None, :] * (block_expert[:, None] == e_ids[None, :]), axis=1)
    return (block_expert.astype(jnp.int32), next_expert.astype(jnp.int32), n_used.reshape(1),
            buf_tok, dest * ROW_TILE)


def kernel(x_prompt, x_sample, cache_k, cache_v, c, c_ctx, w_mod, b_mod, norm_mix_g, norm_ffn_g, w_in,
           pool_w, pool_scale, attn_sink, out_norm_pool_g, out_norm_attn_g, w_out, router_group_w,
           router_group_b, router_expert_w, router_expert_b, w_gate, w_up, w_down, final_norm_g):
    depth = w_mod.shape[0]
    assert depth == 1, "single trunk layer"
    bp, lp, _ = x_prompt.shape
    bs, ls, _ = x_sample.shape
    tp, ts = bp * lp, bs * ls
    tm = TOKEN_TILE
    l = 0

    cond8 = jnp.zeros((8, D_MODEL), F32).at[:bs].set(c).at[bs].set(c_ctx)
    mod3 = _modulation(cond8, w_mod[l], b_mod[l]).reshape(8, 6, D_MODEL)
    rows_p = jnp.full((tp // tm,), bs, jnp.int32)
    rows_s = jnp.arange(ts // tm, dtype=jnp.int32) // (ls // tm)

    w_in_b = w_in[l].astype(BF16)
    w_out_b = w_out[l].astype(BF16)
    pool_w_b = pool_w[l].astype(BF16)
    g_mix = norm_mix_g[l].reshape(1, D_MODEL)
    g_ffn = norm_ffn_g[l].reshape(1, D_MODEL)
    g_pool = out_norm_pool_g[l].reshape(1, POOL_WIDTH)
    g_attn = out_norm_attn_g[l].reshape(1, ATTN_WIDTH)
    p_scale = pool_scale[l].reshape(1, POOL_WIDTH)
    sink_b = jnp.broadcast_to(attn_sink[l][:, None], (N_HEADS, LANES))
    g_final = final_norm_g.reshape(1, D_MODEL)

    rw = jnp.concatenate(
        [router_group_w[l], jnp.transpose(router_expert_w[l], (1, 0, 2)).reshape(D_MODEL, N_EXPERTS)], axis=1)
    rw = jnp.pad(rw, ((0, 0), (0, ROUTER_LANES - rw.shape[1])))
    rw_hi = rw.astype(BF16)
    rw_split = jnp.concatenate([rw_hi, (rw - rw_hi.astype(F32)).astype(BF16)], axis=1)
    rb = jnp.concatenate([router_group_b[l], router_expert_b[l].reshape(N_EXPERTS)])
    rb = jnp.pad(rb, (0, ROUTER_LANES - rb.shape[0])).reshape(1, ROUTER_LANES)

    xp = x_prompt.reshape(tp, D_MODEL)
    xs = x_sample.reshape(ts, D_MODEL)

    up_p, q_p, k_p, v_p = _in_projection(xp, rows_p, mod3, g_mix, w_in_b, None, F32)
    pool_p = _pool_mixer(up_p.reshape(bp, lp, POOL_WIDTH), pool_w_b, p_scale, g_pool).reshape(tp, POOL_WIDTH)
    attn_p = _context_attention(q_p, k_p, v_p, sink_b, g_attn, lp)
    up_s, q_s, k_s, v_s = _in_projection(xs, rows_s, mod3, g_mix, w_in_b, _rope_tables(ls), BF16)
    pool_s = _pool_mixer(up_s.reshape(bs, ls, POOL_WIDTH), pool_w_b, p_scale, g_pool).reshape(ts, POOL_WIDTH)
    attn_s = _latent_attention(
        q_s, k_s.reshape(bs, ls, KV_WIDTH), v_s.reshape(bs, ls, KV_WIDTH),
        cache_k[:, l].reshape(bs, -1, KV_WIDTH), cache_v[:, l].reshape(bs, -1, KV_WIDTH), sink_b, g_attn)

    rows_all = jnp.concatenate([rows_p, rows_s])
    x1_all, h2_all, ids_all, wts_all = _out_projection(
        (pool_p, attn_p, xp), (pool_s, attn_s, xs), rows_all, mod3, g_ffn, w_out_b, rw_split, rb)

    block_expert, next_expert, n_used, buf_tok, slots = _dispatch_tables(
        [ids_all[:, k] for k in range(TOP_K)])
    y_buf = _experts(block_expert, next_expert, n_used, buf_tok, h2_all, w_gate[l], w_up[l], w_down[l])

    y_p, y_s = _combine(slots, rows_all, y_buf, x1_all, wts_all, mod3, g_final, tp)

    kv_shape = (bp, 1, lp, N_KV_HEADS, HEAD_DIM)
    return (y_p.reshape(bp, lp, D_MODEL), y_s.reshape(bs, ls, D_MODEL),
            k_p.reshape(kv_shape), v_p.reshape(kv_shape))
```

```python
import functools

import jax
import jax.numpy as jnp
from jax import lax
from jax.experimental import pallas as pl
from jax.experimental.pallas import tpu as pltpu

F32 = jnp.float32
BF16 = jnp.bfloat16

D_MODEL = 2048
GRID_W = 64
HEAD_DIM = 128
ATTN_WIDTH = 1024
POOL_WIDTH = 1024
N_HEADS = 8
N_KV_HEADS = 2
Q_PER_KV = 4
KV_WIDTH = 256
IN_WIDTH = 2560
WINDOW = 128
POOL_WINDOWS = (2, 4, 8, 16)
POOL_GROUP_DIM = 256
ROPE_BASE = 10000.0
ROPE_AXIS_DIM = 64
N_EXPERT_GROUPS = 4
EXPERTS_PER_GROUP = 8
N_EXPERTS = 32
TOP_K = 2
D_EXPERT = 512
NORM_EPS = 1e-6
NEG_INF = -1e30
ATTN_SCALE = HEAD_DIM ** -0.5
LOG2E = 1.4426950408889634
Q_SCALE = ATTN_SCALE * LOG2E

LANES = 128
Q_BLOCKS_PER_STEP = 4
CTX_SEQS_PER_STEP = 2
VMEM_LIMIT = 48 * 1024 * 1024
PACKED = D_MODEL // 2
ROW_TILE = 8
assert PACKED == ROW_TILE * LANES, "a packed row must fill exactly one tile"
MOD_TILE = 1024
TOKEN_TILE = 256
IN_TILE = 512
POOL_TILES_PER_STEP = 2
POOL_HALO = 16
Q_BLOCK = 128
EXPERT_ROWS = 256
GATHER_GROUPS = 64
ROUTER_LANES = 128


def _params(n_grid):
    return pltpu.CompilerParams(
        dimension_semantics=("arbitrary",) * n_grid, vmem_limit_bytes=VMEM_LIMIT)


def _rms(x):
    return x * lax.rsqrt(jnp.mean(x * x, axis=-1, keepdims=True) + NORM_EPS)


def _dot(a, b):
    return jnp.dot(a, b, preferred_element_type=F32)


def _dot_nt(a, b):
    return lax.dot_general(a, b, (((1,), (1,)), ((), ())), preferred_element_type=F32)


def _pack_bf16_pairs(x):
    c = x.shape[1] // 2
    lo = lax.bitcast_convert_type(x[:, :c].astype(BF16).astype(F32), jnp.uint32)
    hi = lax.bitcast_convert_type(x[:, c:].astype(BF16).astype(F32), jnp.uint32)
    return hi | (lo >> 16)


def _store_row_tiles(ref, x):
    n = x.shape[0]
    for c in range(PACKED // LANES):
        ref[pl.ds(c, n, stride=ROW_TILE), :] = x[:, c * LANES:(c + 1) * LANES]


def _load_row_tiles(ref, n):
    return jnp.concatenate(
        [ref[pl.ds(c, n, stride=ROW_TILE), :] for c in range(PACKED // LANES)], axis=-1)


def _unpack_bf16_pairs(u):
    lo = lax.bitcast_convert_type(u << 16, F32)
    hi = lax.bitcast_convert_type(u & jnp.uint32(0xFFFF0000), F32)
    return lo, hi


def _mod_kernel(cond_ref, w_ref, b_ref, o_ref):
    c = cond_ref[...]
    s = c / (1.0 + jnp.exp(-c))
    o_ref[...] = _dot(s.astype(BF16), w_ref[...].astype(BF16)) + b_ref[...]


def _modulation(cond8, w_mod, b_mod):
    n = w_mod.shape[1]
    tn = MOD_TILE
    return pl.pallas_call(
        _mod_kernel,
        grid=(n // tn,),
        in_specs=[
            pl.BlockSpec((8, D_MODEL), lambda j: (0, 0)),
            pl.BlockSpec((D_MODEL, tn), lambda j: (0, j)),
            pl.BlockSpec((1, tn), lambda j: (0, j)),
        ],
        out_specs=pl.BlockSpec((8, tn), lambda j: (0, j)),
        out_shape=jax.ShapeDtypeStruct((8, n), F32),
        compiler_params=_params(1),
        name="modulation",
    )(cond8, w_mod, b_mod.reshape(1, n))


def _rope(x, cos, sin_signed):
    lane = lax.broadcasted_iota(jnp.int32, x.shape, 1)
    partner = jnp.where((lane & 63) < 32,
                        pltpu.roll(x, HEAD_DIM - 32, axis=1),
                        pltpu.roll(x, 32, axis=1))
    return x * cos + partner * sin_signed


def _inproj_kernel(rows_ref, x_ref, mod_ref, g_ref, w_ref, *rest, rope):
    del rows_ref
    if rope:
        cos_ref, sin_ref, up_ref, q_ref, k_ref, v_ref = rest
    else:
        up_ref, q_ref, k_ref, v_ref = rest
    m = mod_ref[0]
    h = _rms(x_ref[...]) * g_ref[...]
    hb = (h * (1.0 + m[1:2]) + m[0:1]).astype(BF16)
    up_ref[...] = _dot(hb, w_ref[:, 0:POOL_WIDTH]).astype(BF16)
    q = _dot(hb, w_ref[:, POOL_WIDTH:POOL_WIDTH + ATTN_WIDTH])
    k = _dot(hb, w_ref[:, POOL_WIDTH + ATTN_WIDTH:IN_WIDTH - KV_WIDTH])
    v = _dot(hb, w_ref[:, IN_WIDTH - KV_WIDTH:IN_WIDTH])
    if rope:
        cos = cos_ref[...]
        sin = sin_ref[...]
        for hd in range(N_HEADS):
            sl = slice(hd * HEAD_DIM, (hd + 1) * HEAD_DIM)
            q_ref[:, sl] = (_rope(q[:, sl], cos, sin) * Q_SCALE).astype(BF16)
        for hd in range(N_KV_HEADS):
            sl = slice(hd * HEAD_DIM, (hd + 1) * HEAD_DIM)
            k_ref[:, sl] = _rope(k[:, sl], cos, sin).astype(k_ref.dtype)
    else:
        q_ref[...] = (q * Q_SCALE).astype(BF16)
        k_ref[...] = k.astype(k_ref.dtype)
    v_ref[...] = v.astype(v_ref.dtype)


def _in_projection(x, rows, mod3, g_mix, w_in_b, rope_tabs, kv_dtype):
    t = x.shape[0]
    tm = IN_TILE
    per = IN_TILE // TOKEN_TILE
    rope = rope_tabs is not None
    in_specs = [
        pl.BlockSpec((tm, D_MODEL), lambda i, r: (i, 0)),
        pl.BlockSpec((1, 6, D_MODEL), lambda i, r: (r[i * per], 0, 0)),
        pl.BlockSpec((1, D_MODEL), lambda i, r: (0, 0)),
        pl.BlockSpec((D_MODEL, IN_WIDTH), lambda i, r: (0, 0), pipeline_mode=pl.Buffered(1)),
    ]
    args = [x, mod3, g_mix, w_in_b]
    if rope:
        seq_tiles = rope_tabs[0].shape[0] // tm
        for tab in rope_tabs:
            in_specs.append(pl.BlockSpec((tm, HEAD_DIM), lambda i, r: (i % seq_tiles, 0)))
            args.append(tab)
    out_shape = (
        jax.ShapeDtypeStruct((t, POOL_WIDTH), BF16),
        jax.ShapeDtypeStruct((t, ATTN_WIDTH), BF16),
        jax.ShapeDtypeStruct((t, KV_WIDTH), kv_dtype),
        jax.ShapeDtypeStruct((t, KV_WIDTH), kv_dtype),
    )
    out_specs = (
        pl.BlockSpec((tm, POOL_WIDTH), lambda i, r: (i, 0)),
        pl.BlockSpec((tm, ATTN_WIDTH), lambda i, r: (i, 0)),
        pl.BlockSpec((tm, KV_WIDTH), lambda i, r: (i, 0)),
        pl.BlockSpec((tm, KV_WIDTH), lambda i, r: (i, 0)),
    )
    return pl.pallas_call(
        functools.partial(_inproj_kernel, rope=rope),
        grid_spec=pltpu.PrefetchScalarGridSpec(
            num_scalar_prefetch=1, grid=(t // tm,), in_specs=in_specs, out_specs=out_specs),
        out_shape=out_shape,
        compiler_params=_params(1),
        name="in_projection_rope" if rope else "in_projection",
    )(rows, *args)


def _pool_kernel(u_ref, pw_ref, ps_ref, g_ref, o_ref, *, seq_len):
    tq = TOKEN_TILE
    win = min(seq_len, tq + 2 * POOL_HALO)
    for sub in range(o_ref.shape[1] // tq):
        t0 = pl.program_id(1) * o_ref.shape[1] + sub * tq
        src0 = pl.multiple_of(jnp.clip(t0 - POOL_HALO, 0, seq_len - win), POOL_HALO)
        u = u_ref[0, pl.ds(src0, win), :]
        t = t0 + lax.broadcasted_iota(jnp.int32, (tq, win), 0)
        j = src0 + lax.broadcasted_iota(jnp.int32, (tq, win), 1)
        tc = t0 + lax.broadcasted_iota(jnp.int32, (tq, 1), 0)
        ys = []
        for gi, w in enumerate(POOL_WINDOWS):
            lo = jnp.maximum(t - w // 2, 0)
            hi = jnp.minimum(t + (w - w // 2), seq_len)
            cnt = (hi - lo).astype(F32)
            a = jnp.where((j >= lo) & (j < hi), 1.0, 0.0) - jnp.where(j == t, cnt, 0.0)
            cnt_col = (jnp.minimum(tc + (w - w // 2), seq_len) - jnp.maximum(tc - w // 2, 0)).astype(F32)
            sl = slice(gi * POOL_GROUP_DIM, (gi + 1) * POOL_GROUP_DIM)
            d = _dot(a.astype(BF16), u[:, sl]) / cnt_col
            ys.append(_dot(d.astype(BF16), pw_ref[gi]))
        y = jnp.concatenate(ys, axis=-1) * ps_ref[...]
        o_ref[0, sub * tq:(sub + 1) * tq, :] = (_rms(y) * g_ref[...]).astype(BF16)


def _pool_mixer(up, pool_w_b, pool_scale, g_out_pool):
    b, seq_len, _ = up.shape
    tq = TOKEN_TILE * min(POOL_TILES_PER_STEP, seq_len // TOKEN_TILE)
    return pl.pallas_call(
        functools.partial(_pool_kernel, seq_len=seq_len),
        grid=(b, seq_len // tq),
        in_specs=[
            pl.BlockSpec((1, seq_len, POOL_WIDTH), lambda bi, i: (bi, 0, 0)),
            pl.BlockSpec((len(POOL_WINDOWS), POOL_GROUP_DIM, POOL_GROUP_DIM), lambda bi, i: (0, 0, 0)),
            pl.BlockSpec((1, POOL_WIDTH), lambda bi, i: (0, 0)),
            pl.BlockSpec((1, POOL_WIDTH), lambda bi, i: (0, 0)),
        ],
        out_specs=pl.BlockSpec((1, tq, POOL_WIDTH), lambda bi, i: (bi, i, 0)),
        out_shape=jax.ShapeDtypeStruct((b, seq_len, POOL_WIDTH), BF16),
        compiler_params=_params(2),
        name=f"pool_mixer_{seq_len}",
    )(up, pool_w_b, pool_scale, g_out_pool)


def _stack_heads(q, kh, rows):
    return jnp.concatenate(
        [q[:, (kh * Q_PER_KV + g) * HEAD_DIM:(kh * Q_PER_KV + g + 1) * HEAD_DIM]
         for g in range(Q_PER_KV)], axis=0)


def _sink_column(sink_ref, kh, rows):
    return jnp.concatenate(
        [jnp.broadcast_to(sink_ref[kh * Q_PER_KV + g:kh * Q_PER_KV + g + 1, 0:1] * LOG2E, (rows, 1))
         for g in range(Q_PER_KV)], axis=0)


def _attend(q4, key_sets, value_sets, sk, band=None):
    s = [_dot_nt(q4, k) for k in key_sets]
    if band is not None:
        s[0] = jnp.where(band, s[0], NEG_INF)
    m = sk
    for si in s:
        m = jnp.maximum(m, jnp.max(si, axis=-1, keepdims=True))
    acc = None
    for si, v in zip(s, value_sets):
        v1 = jnp.concatenate([v, jnp.ones_like(v)], axis=-1)
        part = _dot(jnp.exp2(si - m).astype(BF16), v1)
        acc = part if acc is None else acc + part
    den = acc[:, HEAD_DIM:] + jnp.exp2(sk - m)
    return acc[:, :HEAD_DIM] / den


def _ctx_attn_kernel(q_ref, k_ref, v_ref, sink_ref, g_ref, o_ref, *, seq_len):
    rows = seq_len
    for j in range(q_ref.shape[0] // rows):
        rs = slice(j * rows, (j + 1) * rows)
        q = q_ref[rs, :]
        heads = [None] * N_HEADS
        for kh in range(N_KV_HEADS):
            sl = slice(kh * HEAD_DIM, (kh + 1) * HEAD_DIM)
            o = _attend(_stack_heads(q, kh, rows), [k_ref[rs, sl].astype(BF16)],
                        [v_ref[rs, sl].astype(BF16)], _sink_column(sink_ref, kh, rows))
            for g in range(Q_PER_KV):
                heads[kh * Q_PER_KV + g] = o[g * rows:(g + 1) * rows]
        y = jnp.concatenate(heads, axis=-1)
        o_ref[rs, :] = (_rms(y) * g_ref[...]).astype(BF16)


def _context_attention(q, k, v, sink_b, g_out_attn, seq_len):
    t = q.shape[0]
    blk = seq_len * CTX_SEQS_PER_STEP
    return pl.pallas_call(
        functools.partial(_ctx_attn_kernel, seq_len=seq_len),
        grid=(t // blk,),
        in_specs=[
            pl.BlockSpec((blk, ATTN_WIDTH), lambda b: (b, 0)),
            pl.BlockSpec((blk, KV_WIDTH), lambda b: (b, 0)),
            pl.BlockSpec((blk, KV_WIDTH), lambda b: (b, 0)),
            pl.BlockSpec((N_HEADS, LANES), lambda b: (0, 0)),
            pl.BlockSpec((1, ATTN_WIDTH), lambda b: (0, 0)),
        ],
        out_specs=pl.BlockSpec((blk, ATTN_WIDTH), lambda b: (b, 0)),
        out_shape=jax.ShapeDtypeStruct((t, ATTN_WIDTH), BF16),
        compiler_params=_params(1),
        name="context_attention",
    )(q, k, v, sink_b, g_out_attn)


def _lat_attn_kernel(q_ref, k_ref, v_ref, ck_ref, cv_ref, sink_ref, g_ref, o_ref, *, seq_len):
    rows = Q_BLOCK
    span = 3 * Q_BLOCK
    shape = (Q_PER_KV * rows, span)
    row_in_block = lax.broadcasted_iota(jnp.int32, shape, 0) & (rows - 1)
    col = lax.broadcasted_iota(jnp.int32, shape, 1)
    for j in range(Q_BLOCKS_PER_STEP):
        q0 = (pl.program_id(1) * Q_BLOCKS_PER_STEP + j) * rows
        start = pl.multiple_of(jnp.clip(q0 - Q_BLOCK, 0, seq_len - span), Q_BLOCK)
        q = q_ref[j * rows:(j + 1) * rows, :]
        kl = k_ref[0, pl.ds(start, span), :]
        vl = v_ref[0, pl.ds(start, span), :]
        band = jnp.abs(row_in_block - col + (q0 - start)) <= WINDOW
        heads = [None] * N_HEADS
        for kh in range(N_KV_HEADS):
            sl = slice(kh * HEAD_DIM, (kh + 1) * HEAD_DIM)
            o = _attend(_stack_heads(q, kh, rows),
                        [kl[:, sl], ck_ref[0, :, sl].astype(BF16)],
                        [vl[:, sl], cv_ref[0, :, sl].astype(BF16)],
                        _sink_column(sink_ref, kh, rows), band)
            for g in range(Q_PER_KV):
                heads[kh * Q_PER_KV + g] = o[g * rows:(g + 1) * rows]
        y = jnp.concatenate(heads, axis=-1)
        o_ref[j * rows:(j + 1) * rows, :] = (_rms(y) * g_ref[...]).astype(BF16)


def _latent_attention(q, k, v, cache_k, cache_v, sink_b, g_out_attn):
    b, seq_len, _ = k.shape
    q_rows = Q_BLOCK * Q_BLOCKS_PER_STEP
    nq = seq_len // q_rows
    past = cache_k.shape[1]
    return pl.pallas_call(
        functools.partial(_lat_attn_kernel, seq_len=seq_len),
        grid=(b, nq),
        in_specs=[
            pl.BlockSpec((q_rows, ATTN_WIDTH), lambda bi, n: (bi * nq + n, 0)),
            pl.BlockSpec((1, seq_len, KV_WIDTH), lambda bi, n: (bi, 0, 0)),
            pl.BlockSpec((1, seq_len, KV_WIDTH), lambda bi, n: (bi, 0, 0)),
            pl.BlockSpec((1, past, KV_WIDTH), lambda bi, n: (bi, 0, 0)),
            pl.BlockSpec((1, past, KV_WIDTH), lambda bi, n: (bi, 0, 0)),
            pl.BlockSpec((N_HEADS, LANES), lambda bi, n: (0, 0)),
            pl.BlockSpec((1, ATTN_WIDTH), lambda bi, n: (0, 0)),
        ],
        out_specs=pl.BlockSpec((q_rows, ATTN_WIDTH), lambda bi, n: (bi * nq + n, 0)),
        out_shape=jax.ShapeDtypeStruct((b * seq_len, ATTN_WIDTH), BF16),
        compiler_params=_params(2),
        name="latent_attention",
    )(q, k, v, cache_k, cache_v, sink_b, g_out_attn)


def _route(logits):
    lane = lax.broadcasted_iota(jnp.int32, logits.shape, 1).astype(F32)
    neg = -jnp.inf

    def first_argmax(x):
        mx = jnp.max(x, axis=-1, keepdims=True)
        return mx, jnp.min(jnp.where(x == mx, lane, float(ROUTER_LANES)), axis=-1, keepdims=True)

    gl = jnp.where(lane < N_EXPERT_GROUPS, logits, neg)
    gmax, g_idx = first_argmax(gl)
    p_g = 1.0 / jnp.sum(jnp.exp(gl - gmax), axis=-1, keepdims=True)
    base = N_EXPERT_GROUPS + EXPERTS_PER_GROUP * g_idx
    el = jnp.where((lane >= base) & (lane < base + EXPERTS_PER_GROUP), logits, neg)
    v1, i1 = first_argmax(el)
    v2, i2 = first_argmax(jnp.where(lane == i1, neg, el))
    e2 = jnp.exp(v2 - v1)
    w1 = p_g / (1.0 + e2)
    w2 = p_g * e2 / (1.0 + e2)
    ids = jnp.where(lane == 0.0, i1 - N_EXPERT_GROUPS, jnp.where(lane == 1.0, i2 - N_EXPERT_GROUPS, 0.0))
    wts = jnp.where(lane == 0.0, w1, jnp.where(lane == 1.0, w2, 0.0))
    return ids.astype(jnp.int32), wts


def _outproj_kernel(rows_ref, pool_p, attn_p, x_p, pool_s, attn_s, x_s, mod_ref, g_ref, w_ref, rw_ref,
                    rb_ref, x1_ref, h2_ref, ids_ref, wts_ref, mix0, mix1, *, n_ctx_tiles, n_tiles):
    del rows_ref
    i = pl.program_id(0)

    def project(pool_ref, attn_ref, mix_ref):
        mix_ref[...] = (_dot(pool_ref[...], w_ref[0:POOL_WIDTH, :])
                        + _dot(attn_ref[...], w_ref[POOL_WIDTH:POOL_WIDTH + ATTN_WIDTH, :]))

    def tail(x_ref, mix_ref):
        m = mod_ref[0]
        x1 = x_ref[...] + m[2:3] * mix_ref[...]
        x1_ref[...] = x1
        h2 = _rms(x1) * g_ref[...] * (1.0 + m[4:5]) + m[3:4]
        _store_row_tiles(h2_ref, _pack_bf16_pairs(h2))
        hi = h2.astype(BF16)
        lo = (h2 - hi.astype(F32)).astype(BF16)
        both = _dot(hi, rw_ref[...])
        logits = (both[:, :ROUTER_LANES] + both[:, ROUTER_LANES:] + _dot(lo, rw_ref[:, :ROUTER_LANES])
                  + rb_ref[...])
        ids, wts = _route(logits)
        ids_ref[...] = ids
        wts_ref[...] = wts

    ctx_in = (pool_p, attn_p)
    lat_in = (pool_s, attn_s)
    for parity, (cur, prev) in enumerate(((mix0, mix1), (mix1, mix0))):
        mine = (i % 2) == parity

        @pl.when(mine & (i == 0))
        def _(cur=cur):
            project(*ctx_in, cur)

        @pl.when(mine & (i >= 1) & (i < n_ctx_tiles))
        def _(cur=cur, prev=prev):
            tail(x_p, prev)
            project(*ctx_in, cur)

        @pl.when(mine & (i == n_ctx_tiles))
        def _(cur=cur, prev=prev):
            tail(x_p, prev)
            project(*lat_in, cur)

        @pl.when(mine & (i > n_ctx_tiles) & (i < n_tiles))
        def _(cur=cur, prev=prev):
            tail(x_s, prev)
            project(*lat_in, cur)

        @pl.when(mine & (i == n_tiles))
        def _(prev=prev):
            tail(x_s, prev)


def _out_projection(ctx, lat, rows, mod3, g_ffn, w_out_b, rw_split, rb):
    tm = TOKEN_TILE
    n_ctx = ctx[2].shape[0] // tm
    n_lat = lat[2].shape[0] // tm
    n_tiles = n_ctx + n_lat
    t_all = n_tiles * tm
    assert n_ctx >= 1 and n_lat >= 1

    def ctx_map(i, r):
        return (jnp.minimum(i, n_ctx - 1), 0)

    def lat_map(i, r):
        return (jnp.clip(i - n_ctx, 0, n_lat - 1), 0)

    def ctx_prev_map(i, r):
        return (jnp.clip(i - 1, 0, n_ctx - 1), 0)

    def lat_prev_map(i, r):
        return (jnp.clip(i - 1 - n_ctx, 0, n_lat - 1), 0)

    def prev_tile(i):
        return jnp.maximum(i - 1, 0)

    in_specs = [
        pl.BlockSpec((tm, POOL_WIDTH), ctx_map),
        pl.BlockSpec((tm, ATTN_WIDTH), ctx_map),
        pl.BlockSpec((tm, D_MODEL), ctx_prev_map),
        pl.BlockSpec((tm, POOL_WIDTH), lat_map),
        pl.BlockSpec((tm, ATTN_WIDTH), lat_map),
        pl.BlockSpec((tm, D_MODEL), lat_prev_map),
        pl.BlockSpec((1, 6, D_MODEL), lambda i, r: (r[prev_tile(i)], 0, 0)),
        pl.BlockSpec((1, D_MODEL), lambda i, r: (0, 0)),
        pl.BlockSpec((D_MODEL, D_MODEL), lambda i, r: (0, 0), pipeline_mode=pl.Buffered(1)),
        pl.BlockSpec((D_MODEL, 2 * ROUTER_LANES), lambda i, r: (0, 0)),
        pl.BlockSpec((1, ROUTER_LANES), lambda i, r: (0, 0)),
    ]
    out_shape = (
        jax.ShapeDtypeStruct((t_all, D_MODEL), F32),
        jax.ShapeDtypeStruct((t_all * ROW_TILE, LANES), jnp.uint32),
        jax.ShapeDtypeStruct((t_all, ROUTER_LANES), jnp.int32),
        jax.ShapeDtypeStruct((t_all, ROUTER_LANES), F32),
    )
    out_specs = (
        pl.BlockSpec((tm, D_MODEL), lambda i, r: (prev_tile(i), 0)),
        pl.BlockSpec((tm * ROW_TILE, LANES), lambda i, r: (prev_tile(i), 0)),
        pl.BlockSpec((tm, ROUTER_LANES), lambda i, r: (prev_tile(i), 0)),
        pl.BlockSpec((tm, ROUTER_LANES), lambda i, r: (prev_tile(i), 0)),
    )
    return pl.pallas_call(
        functools.partial(_outproj_kernel, n_ctx_tiles=n_ctx, n_tiles=n_tiles),
        grid_spec=pltpu.PrefetchScalarGridSpec(
            num_scalar_prefetch=1, grid=(n_tiles + 1,), in_specs=in_specs, out_specs=out_specs,
            scratch_shapes=[pltpu.VMEM((tm, D_MODEL), F32), pltpu.VMEM((tm, D_MODEL), F32)]),
        out_shape=out_shape,
        compiler_params=_params(1),
        name="out_projection",
    )(rows, *ctx, *lat, mod3, g_ffn, w_out_b, rw_split, rb)


def _gather_rows(idx_ref, base, n_rows, src_hbm, dst, sem, *, unrolled, both_queues=False, dst_row0=0):
    def one(r, priority):
        src_row = pl.multiple_of(idx_ref[base + r], ROW_TILE)
        pltpu.make_async_copy(
            src_hbm.at[pl.ds(src_row, ROW_TILE)], dst.at[pl.ds((dst_row0 + r) * ROW_TILE, ROW_TILE)],
            sem).start(priority=priority)

    if unrolled:
        for r in range(n_rows):
            one(r, r % 2 if both_queues else 0)
    else:
        def body(r, carry):
            one(r, 0)
            return carry
        lax.fori_loop(0, n_rows, body, 0, unroll=8)


def _wait_rows(src_hbm, dst, sem):
    pltpu.make_async_copy(src_hbm.at[pl.ds(0, dst.shape[0])], dst, sem).wait()


def _late_zero(x):
    u = lax.bitcast_convert_type(x, jnp.uint32)
    return ((u >> 16) >> 16).astype(jnp.int32)[0, 0]


def _moe_kernel(be_ref, ne_ref, nu_ref, first_ref, tok_ref, h_hbm, wg_hbm, wu_hbm, wd_hbm, y_ref,
                xg0, xg1, xg2, sg, su, sd, wgb, wub, wdb, sem, wsem):
    b = pl.program_id(0)
    n_used = nu_ref[0]
    last = n_used - 1
    active = b < n_used
    bufs = (xg0, xg1, xg2)
    n_buf = len(bufs)

    def weight_copies(e):
        return (pltpu.make_async_copy(wg_hbm.at[e], sg, wsem.at[0]),
                pltpu.make_async_copy(wu_hbm.at[e], su, wsem.at[1]),
                pltpu.make_async_copy(wd_hbm.at[e], sd, wsem.at[2]))

    @pl.when(b == 0)
    def _():
        for cp in weight_copies(be_ref[0]):
            cp.start(priority=1)
        for i in range(n_buf - 1):
            _gather_rows(tok_ref, first_ref[jnp.minimum(i, last)], EXPERT_ROWS, h_hbm, bufs[i], sem.at[i],
                         unrolled=False)

    new_expert = (b == 0) | (be_ref[b] != be_ref[jnp.maximum(b - 1, 0)])

    @pl.when(active & new_expert)
    def _():
        for cp in weight_copies(be_ref[b]):
            cp.wait()
        wgb[...] = sg[...].astype(BF16)
        wub[...] = su[...].astype(BF16)
        wdb[...] = sd[...].astype(BF16)

        @pl.when(ne_ref[b] >= 0)
        def _():
            for cp in weight_copies(ne_ref[b]):
                cp.start(priority=1)

    def step(slot):
        cur = bufs[slot]
        ahead = bufs[(slot + n_buf - 1) % n_buf]
        sem_ahead = sem.at[(slot + n_buf - 1) % n_buf]
        _wait_rows(h_hbm, cur, sem.at[slot])
        base = first_ref[jnp.minimum(b + n_buf - 1, last)]
        per = EXPERT_ROWS // GATHER_GROUPS
        group = iter(range(GATHER_GROUPS))

        def issue(zero):
            g = next(group)
            _gather_rows(tok_ref, base + g * per + zero, per, h_hbm, ahead, sem_ahead, unrolled=True,
                         both_queues=True, dst_row0=g * per)

        def corner(x, r, c):
            return _late_zero(x[r:r + 1, c:c + 1])

        bm, de = EXPERT_ROWS, D_EXPERT
        lo, hi = _unpack_bf16_pairs(_load_row_tiles(cur, bm))
        lo = lo.astype(BF16)
        hi = hi.astype(BF16)

        def issue_along(x, n):
            rows, cols = x.shape
            for i in range(n):
                issue(corner(x, (i + 1) * rows // n - 1, (i + 1) * cols // n - 1))

        n_each = GATHER_GROUPS // 8
        issue(0)
        a1 = _dot(lo, wgb[0:PACKED, :])
        issue_along(a1, n_each)
        a = a1 + _dot(hi, wgb[PACKED:D_MODEL, :])
        issue_along(a, n_each)
        u1 = _dot(lo, wub[0:PACKED, :])
        issue_along(u1, n_each)
        u = u1 + _dot(hi, wub[PACKED:D_MODEL, :])
        issue_along(u, n_each)
        act = (a / (1.0 + jnp.exp(-a)) * u).astype(BF16)
        y = _dot(act, wdb[...])
        issue_along(y, GATHER_GROUPS - 1 - 4 * n_each)
        _store_row_tiles(y_ref, _pack_bf16_pairs(y))

        @pl.when(b == last)
        def _():
            for i in range(1, n_buf):
                _wait_rows(h_hbm, bufs[(slot + i) % n_buf], sem.at[(slot + i) % n_buf])

    for slot in range(n_buf):
        @pl.when(active & (b % n_buf == slot))
        def _(slot=slot):
            step(slot)

    @pl.when(b >= n_used)
    def _():
        y_ref[...] = jnp.zeros_like(y_ref)


def _experts(tables, cap, h2_packed, w_gate, w_up, w_down):
    bm = EXPERT_ROWS
    xg = pltpu.VMEM((bm * ROW_TILE, LANES), jnp.uint32)
    return pl.pallas_call(
        _moe_kernel,
        grid_spec=pltpu.PrefetchScalarGridSpec(
            num_scalar_prefetch=len(tables),
            grid=(cap // bm,),
            in_specs=[pl.BlockSpec(memory_space=pl.ANY)] * 4,
            out_specs=pl.BlockSpec((bm * ROW_TILE, LANES), lambda b, *_: (b, 0)),
            scratch_shapes=[
                xg, xg, xg,
                pltpu.VMEM((D_MODEL, D_EXPERT), F32), pltpu.VMEM((D_MODEL, D_EXPERT), F32),
                pltpu.VMEM((D_EXPERT, D_MODEL), F32),
                pltpu.VMEM((D_MODEL, D_EXPERT), BF16), pltpu.VMEM((D_MODEL, D_EXPERT), BF16),
                pltpu.VMEM((D_EXPERT, D_MODEL), BF16),
                pltpu.SemaphoreType.DMA((3,)), pltpu.SemaphoreType.DMA((3,)),
            ],
        ),
        out_shape=jax.ShapeDtypeStruct((cap * ROW_TILE, LANES), jnp.uint32),
        compiler_params=_params(1),
        name="experts",
    )(*tables, h2_packed, w_gate, w_up, w_down)


def _combine_kernel(slot_ref, rows_ref, y_hbm, x1_ref, wts_ref, mod_ref, g_ref, op_ref, os_ref, yg0, yg1, sem,
                    *, n_ctx_tiles, n_tiles):
    del rows_ref
    tm = TOKEN_TILE
    i = pl.program_id(0)
    n_slots = slot_ref.shape[0] // TOP_K

    def start(tile, buf, s, unrolled):
        for k in range(TOP_K):
            _gather_rows(slot_ref, k * n_slots + tile * tm, tm, y_hbm, buf.at[k], s, unrolled=unrolled,
                         both_queues=True)

    @pl.when(i == 0)
    def _():
        start(0, yg0, sem.at[0], False)

    def step(cur, nxt, sem_cur, sem_nxt):
        for k in range(TOP_K):
            _wait_rows(y_hbm, cur.at[k], sem_cur)
        start(jnp.minimum(i + 1, n_tiles - 1), nxt, sem_nxt, True)
        w = wts_ref[...]
        lo0, hi0 = _unpack_bf16_pairs(_load_row_tiles(cur.at[0], tm))
        lo1, hi1 = _unpack_bf16_pairs(_load_row_tiles(cur.at[1], tm))
        ffn = jnp.concatenate([lo0 * w[:, 0:1] + lo1 * w[:, 1:2], hi0 * w[:, 0:1] + hi1 * w[:, 1:2]], axis=-1)
        x2 = x1_ref[...] + mod_ref[0][5:6] * ffn
        out = _rms(x2) * g_ref[...]

        @pl.when(i < n_ctx_tiles)
        def _():
            op_ref[...] = out

        @pl.when(i >= n_ctx_tiles)
        def _():
            os_ref[...] = out

        @pl.when(i == n_tiles - 1)
        def _():
            for k in range(TOP_K):
                _wait_rows(y_hbm, nxt.at[k], sem_nxt)

    @pl.when(i % 2 == 0)
    def _():
        step(yg0, yg1, sem.at[0], sem.at[1])

    @pl.when(i % 2 == 1)
    def _():
        step(yg1, yg0, sem.at[1], sem.at[0])


def _combine(slots, rows, y_buf, x1, wts, mod3, g_final, t_ctx):
    t = x1.shape[0]
    tm = TOKEN_TILE
    n_tiles = t // tm
    n_ctx = t_ctx // tm
    return pl.pallas_call(
        functools.partial(_combine_kernel, n_ctx_tiles=n_ctx, n_tiles=n_tiles),
        grid_spec=pltpu.PrefetchScalarGridSpec(
            num_scalar_prefetch=2,
            grid=(n_tiles,),
            in_specs=[
                pl.BlockSpec(memory_space=pl.ANY),
                pl.BlockSpec((tm, D_MODEL), lambda i, s, r: (i, 0)),
                pl.BlockSpec((tm, ROUTER_LANES), lambda i, s, r: (i, 0)),
                pl.BlockSpec((1, 6, D_MODEL), lambda i, s, r: (r[i], 0, 0)),
                pl.BlockSpec((1, D_MODEL), lambda i, s, r: (0, 0)),
            ],
            out_specs=(
                pl.BlockSpec((tm, D_MODEL), lambda i, s, r: (jnp.minimum(i, n_ctx - 1), 0)),
                pl.BlockSpec((tm, D_MODEL), lambda i, s, r: (jnp.maximum(i - n_ctx, 0), 0)),
            ),
            scratch_shapes=[pltpu.VMEM((TOP_K, tm * ROW_TILE, LANES), jnp.uint32),
                            pltpu.VMEM((TOP_K, tm * ROW_TILE, LANES), jnp.uint32),
                            pltpu.SemaphoreType.DMA((2,))],
        ),
        out_shape=(jax.ShapeDtypeStruct((t_ctx, D_MODEL), F32),
                   jax.ShapeDtypeStruct((t - t_ctx, D_MODEL), F32)),
        compiler_params=_params(1),
        name="combine",
    )(slots, rows, y_buf, x1, wts, mod3, g_final)


def _rope_tables(n_tokens):
    rows = n_tokens // GRID_W
    row = jnp.repeat(jnp.arange(rows, dtype=F32), GRID_W)
    col = jnp.tile(jnp.arange(GRID_W, dtype=F32), rows)
    half = ROPE_AXIS_DIM // 2
    inv_freq = ROPE_BASE ** (-jnp.arange(half, dtype=F32) / half)
    ar = row[:, None] * inv_freq
    ac = col[:, None] * inv_freq
    cos = jnp.concatenate([jnp.cos(ar), jnp.cos(ar), jnp.cos(ac), jnp.cos(ac)], axis=-1)
    sin = jnp.concatenate([-jnp.sin(ar), jnp.sin(ar), -jnp.sin(ac), jnp.sin(ac)], axis=-1)
    return cos, sin


def _dispatch_tables(expert_ids):
    t = expert_ids[0].shape[0]
    bm = EXPERT_ROWS
    flat_e = jnp.concatenate(expert_ids)
    onehot = (flat_e[:, None] == jnp.arange(N_EXPERTS, dtype=jnp.int32)[None, :]).astype(jnp.int32)
    running = jnp.cumsum(onehot, axis=0)
    rank = jnp.sum(running * onehot, axis=1) - 1
    counts = running[-1]
    padded = (counts + bm - 1) // bm * bm
    pad_end = jnp.cumsum(padded)
    pad_start = pad_end - padded
    dest = (jnp.sum(pad_start[None, :] * onehot, axis=1) + rank).astype(jnp.int32)
    n_blocks = (t * TOP_K + N_EXPERTS * (bm - 1) + bm - 1) // bm
    order = jnp.argsort(flat_e).astype(jnp.int32)
    src_tile = jnp.where(order >= t, order - t, order) * ROW_TILE
    src_tile = jnp.concatenate([src_tile, jnp.zeros((bm,), jnp.int32)])
    n_used = (pad_end[-1] // bm).astype(jnp.int32)
    blk = jnp.arange(n_blocks, dtype=jnp.int32)
    blk = jnp.minimum(blk, n_used - 1)
    block_expert = jnp.sum((pad_end[None, :] <= (blk * bm)[:, None]).astype(jnp.int32), axis=1)
    block_expert = jnp.minimum(block_expert, N_EXPERTS - 1)
    e_ids = jnp.arange(N_EXPERTS, dtype=jnp.int32)
    later = (e_ids[None, :] > e_ids[:, None]) & (counts[None, :] > 0)
    next_of = jnp.min(jnp.where(later, e_ids[None, :], N_EXPERTS), axis=1)
    next_of = jnp.where(next_of == N_EXPERTS, -1, next_of)
    of_block = (block_expert[:, None] == e_ids[None, :]).astype(jnp.int32)
    next_expert = jnp.sum(next_of[None, :] * of_block, axis=1)
    start = jnp.cumsum(counts) - counts
    first_src = jnp.sum((start - pad_start)[None, :] * of_block, axis=1) + blk * bm
    return (block_expert.astype(jnp.int32), next_expert.astype(jnp.int32), n_used.reshape(1),
            first_src.astype(jnp.int32), src_tile, dest * ROW_TILE, n_blocks * bm)


def kernel(x_prompt, x_sample, cache_k, cache_v, c, c_ctx, w_mod, b_mod, norm_mix_g, norm_ffn_g, w_in,
           pool_w, pool_scale, attn_sink, out_norm_pool_g, out_norm_attn_g, w_out, router_group_w,
           router_group_b, router_expert_w, router_expert_b, w_gate, w_up, w_down, final_norm_g):
    depth = w_mod.shape[0]
    assert depth == 1, "single trunk layer"
    bp, lp, _ = x_prompt.shape
    bs, ls, _ = x_sample.shape
    tp, ts = bp * lp, bs * ls
    tm = TOKEN_TILE
    l = 0

    cond8 = jnp.zeros((8, D_MODEL), F32).at[:bs].set(c).at[bs].set(c_ctx)
    mod3 = _modulation(cond8, w_mod[l], b_mod[l]).reshape(8, 6, D_MODEL)
    rows_p = jnp.full((tp // tm,), bs, jnp.int32)
    rows_s = jnp.arange(ts // tm, dtype=jnp.int32) // (ls // tm)

    w_in_b = w_in[l].astype(BF16)
    w_out_b = w_out[l].astype(BF16)
    pool_w_b = pool_w[l].astype(BF16)
    g_mix = norm_mix_g[l].reshape(1, D_MODEL)
    g_ffn = norm_ffn_g[l].reshape(1, D_MODEL)
    g_pool = out_norm_pool_g[l].reshape(1, POOL_WIDTH)
    g_attn = out_norm_attn_g[l].reshape(1, ATTN_WIDTH)
    p_scale = pool_scale[l].reshape(1, POOL_WIDTH)
    sink_b = jnp.broadcast_to(attn_sink[l][:, None], (N_HEADS, LANES))
    g_final = final_norm_g.reshape(1, D_MODEL)

    rw = jnp.concatenate(
        [router_group_w[l], jnp.transpose(router_expert_w[l], (1, 0, 2)).reshape(D_MODEL, N_EXPERTS)], axis=1)
    rw = jnp.pad(rw, ((0, 0), (0, ROUTER_LANES - rw.shape[1])))
    rw_hi = rw.astype(BF16)
    rw_split = jnp.concatenate([rw_hi, (rw - rw_hi.astype(F32)).astype(BF16)], axis=1)
    rb = jnp.concatenate([router_group_b[l], router_expert_b[l].reshape(N_EXPERTS)])
    rb = jnp.pad(rb, (0, ROUTER_LANES - rb.shape[0])).reshape(1, ROUTER_LANES)

    xp = x_prompt.reshape(tp, D_MODEL)
    xs = x_sample.reshape(ts, D_MODEL)

    up_p, q_p, k_p, v_p = _in_projection(xp, rows_p, mod3, g_mix, w_in_b, None, F32)
    pool_p = _pool_mixer(up_p.reshape(bp, lp, POOL_WIDTH), pool_w_b, p_scale, g_pool).reshape(tp, POOL_WIDTH)
    attn_p = _context_attention(q_p, k_p, v_p, sink_b, g_attn, lp)
    up_s, q_s, k_s, v_s = _in_projection(xs, rows_s, mod3, g_mix, w_in_b, _rope_tables(ls), BF16)
    pool_s = _pool_mixer(up_s.reshape(bs, ls, POOL_WIDTH), pool_w_b, p_scale, g_pool).reshape(ts, POOL_WIDTH)
    attn_s = _latent_attention(
        q_s, k_s.reshape(bs, ls, KV_WIDTH), v_s.reshape(bs, ls, KV_WIDTH),
        cache_k[:, l].reshape(bs, -1, KV_WIDTH), cache_v[:, l].reshape(bs, -1, KV_WIDTH), sink_b, g_attn)

    rows_all = jnp.concatenate([rows_p, rows_s])
    x1_all, h2_all, ids_all, wts_all = _out_projection(
        (pool_p, attn_p, xp), (pool_s, attn_s, xs), rows_all, mod3, g_ffn, w_out_b, rw_split, rb)

    *expert_tables, slots, cap = _dispatch_tables([ids_all[:, k] for k in range(TOP_K)])
    y_buf = _experts(expert_tables, cap, h2_all, w_gate[l], w_up[l], w_down[l])

    y_p, y_s = _combine(slots, rows_all, y_buf, x1_all, wts_all, mod3, g_final, tp)

    kv_shape = (bp, 1, lp, N_KV_HEADS, HEAD_DIM)
    return (y_p.reshape(bp, lp, D_MODEL), y_s.reshape(bs, ls, D_MODEL),
            k_p.reshape(kv_shape), v_p.reshape(kv_shape))
```

```python
import functools

import jax
import jax.numpy as jnp
from jax import lax
from jax.experimental import pallas as pl
from jax.experimental.pallas import tpu as pltpu

F32 = jnp.float32
BF16 = jnp.bfloat16

D_MODEL = 2048
GRID_W = 64
HEAD_DIM = 128
ATTN_WIDTH = 1024
POOL_WIDTH = 1024
N_HEADS = 8
N_KV_HEADS = 2
Q_PER_KV = 4
KV_WIDTH = 256
IN_WIDTH = 2560
WINDOW = 128
POOL_WINDOWS = (2, 4, 8, 16)
POOL_GROUP_DIM = 256
ROPE_BASE = 10000.0
ROPE_AXIS_DIM = 64
N_EXPERT_GROUPS = 4
EXPERTS_PER_GROUP = 8
N_EXPERTS = 32
TOP_K = 2
D_EXPERT = 512
NORM_EPS = 1e-6
NEG_INF = -1e30
ATTN_SCALE = HEAD_DIM ** -0.5
LOG2E = 1.4426950408889634
Q_SCALE = ATTN_SCALE * LOG2E

LANES = 128
Q_BLOCKS_PER_STEP = 4
CTX_SEQS_PER_STEP = 2
VMEM_LIMIT = 48 * 1024 * 1024
PACKED = D_MODEL // 2
ROW_TILE = 8
assert PACKED == ROW_TILE * LANES, "a packed row must fill exactly one tile"
MOD_TILE = 1024
TOKEN_TILE = 256
IN_TILE = 512
POOL_TILES_PER_STEP = 2
POOL_HALO = 16
Q_BLOCK = 128
EXPERT_ROWS = 256
GATHER_GROUPS = 8
ROUTER_LANES = 128


def _params(n_grid):
    return pltpu.CompilerParams(
        dimension_semantics=("arbitrary",) * n_grid, vmem_limit_bytes=VMEM_LIMIT)


def _rms(x):
    return x * lax.rsqrt(jnp.mean(x * x, axis=-1, keepdims=True) + NORM_EPS)


def _dot(a, b):
    return jnp.dot(a, b, preferred_element_type=F32)


def _dot_nt(a, b):
    return lax.dot_general(a, b, (((1,), (1,)), ((), ())), preferred_element_type=F32)


def _pack_bf16_pairs(x):
    c = x.shape[1] // 2
    lo = lax.bitcast_convert_type(x[:, :c].astype(BF16).astype(F32), jnp.uint32)
    hi = lax.bitcast_convert_type(x[:, c:].astype(BF16).astype(F32), jnp.uint32)
    return hi | (lo >> 16)


def _store_row_tiles(ref, x):
    n = x.shape[0]
    for c in range(PACKED // LANES):
        ref[pl.ds(c, n, stride=ROW_TILE), :] = x[:, c * LANES:(c + 1) * LANES]


def _load_row_tiles(ref, n):
    return jnp.concatenate(
        [ref[pl.ds(c, n, stride=ROW_TILE), :] for c in range(PACKED // LANES)], axis=-1)


def _unpack_bf16_pairs(u):
    lo = lax.bitcast_convert_type(u << 16, F32)
    hi = lax.bitcast_convert_type(u & jnp.uint32(0xFFFF0000), F32)
    return lo, hi


def _mod_kernel(cond_ref, w_ref, b_ref, o_ref):
    c = cond_ref[...]
    s = c / (1.0 + jnp.exp(-c))
    o_ref[...] = _dot(s.astype(BF16), w_ref[...].astype(BF16)) + b_ref[...]


def _modulation(cond8, w_mod, b_mod):
    n = w_mod.shape[1]
    tn = MOD_TILE
    return pl.pallas_call(
        _mod_kernel,
        grid=(n // tn,),
        in_specs=[
            pl.BlockSpec((8, D_MODEL), lambda j: (0, 0)),
            pl.BlockSpec((D_MODEL, tn), lambda j: (0, j)),
            pl.BlockSpec((1, tn), lambda j: (0, j)),
        ],
        out_specs=pl.BlockSpec((8, tn), lambda j: (0, j)),
        out_shape=jax.ShapeDtypeStruct((8, n), F32),
        compiler_params=_params(1),
        name="modulation",
    )(cond8, w_mod, b_mod.reshape(1, n))


def _rope(x, cos, sin_signed):
    lane = lax.broadcasted_iota(jnp.int32, x.shape, 1)
    partner = jnp.where((lane & 63) < 32,
                        pltpu.roll(x, HEAD_DIM - 32, axis=1),
                        pltpu.roll(x, 32, axis=1))
    return x * cos + partner * sin_signed


def _inproj_kernel(rows_ref, x_ref, mod_ref, g_ref, w_ref, *rest, rope):
    del rows_ref
    if rope:
        cos_ref, sin_ref, up_ref, q_ref, k_ref, v_ref = rest
    else:
        up_ref, q_ref, k_ref, v_ref = rest
    m = mod_ref[0]
    h = _rms(x_ref[...]) * g_ref[...]
    hb = (h * (1.0 + m[1:2]) + m[0:1]).astype(BF16)
    up_ref[...] = _dot(hb, w_ref[:, 0:POOL_WIDTH]).astype(BF16)
    q = _dot(hb, w_ref[:, POOL_WIDTH:POOL_WIDTH + ATTN_WIDTH])
    k = _dot(hb, w_ref[:, POOL_WIDTH + ATTN_WIDTH:IN_WIDTH - KV_WIDTH])
    v = _dot(hb, w_ref[:, IN_WIDTH - KV_WIDTH:IN_WIDTH])
    if rope:
        cos = cos_ref[...]
        sin = sin_ref[...]
        for hd in range(N_HEADS):
            sl = slice(hd * HEAD_DIM, (hd + 1) * HEAD_DIM)
            q_ref[:, sl] = (_rope(q[:, sl], cos, sin) * Q_SCALE).astype(BF16)
        for hd in range(N_KV_HEADS):
            sl = slice(hd * HEAD_DIM, (hd + 1) * HEAD_DIM)
            k_ref[:, sl] = _rope(k[:, sl], cos, sin).astype(k_ref.dtype)
    else:
        q_ref[...] = (q * Q_SCALE).astype(BF16)
        k_ref[...] = k.astype(k_ref.dtype)
    v_ref[...] = v.astype(v_ref.dtype)


def _in_projection(x, rows, mod3, g_mix, w_in_b, rope_tabs, kv_dtype):
    t = x.shape[0]
    tm = IN_TILE
    per = IN_TILE // TOKEN_TILE
    rope = rope_tabs is not None
    in_specs = [
        pl.BlockSpec((tm, D_MODEL), lambda i, r: (i, 0)),
        pl.BlockSpec((1, 6, D_MODEL), lambda i, r: (r[i * per], 0, 0)),
        pl.BlockSpec((1, D_MODEL), lambda i, r: (0, 0)),
        pl.BlockSpec((D_MODEL, IN_WIDTH), lambda i, r: (0, 0), pipeline_mode=pl.Buffered(1)),
    ]
    args = [x, mod3, g_mix, w_in_b]
    if rope:
        seq_tiles = rope_tabs[0].shape[0] // tm
        for tab in rope_tabs:
            in_specs.append(pl.BlockSpec((tm, HEAD_DIM), lambda i, r: (i % seq_tiles, 0)))
            args.append(tab)
    out_shape = (
        jax.ShapeDtypeStruct((t, POOL_WIDTH), BF16),
        jax.ShapeDtypeStruct((t, ATTN_WIDTH), BF16),
        jax.ShapeDtypeStruct((t, KV_WIDTH), kv_dtype),
        jax.ShapeDtypeStruct((t, KV_WIDTH), kv_dtype),
    )
    out_specs = (
        pl.BlockSpec((tm, POOL_WIDTH), lambda i, r: (i, 0)),
        pl.BlockSpec((tm, ATTN_WIDTH), lambda i, r: (i, 0)),
        pl.BlockSpec((tm, KV_WIDTH), lambda i, r: (i, 0)),
        pl.BlockSpec((tm, KV_WIDTH), lambda i, r: (i, 0)),
    )
    return pl.pallas_call(
        functools.partial(_inproj_kernel, rope=rope),
        grid_spec=pltpu.PrefetchScalarGridSpec(
            num_scalar_prefetch=1, grid=(t // tm,), in_specs=in_specs, out_specs=out_specs),
        out_shape=out_shape,
        compiler_params=_params(1),
        name="in_projection_rope" if rope else "in_projection",
    )(rows, *args)


def _pool_kernel(u_ref, pw_ref, ps_ref, g_ref, o_ref, *, seq_len):
    tq = TOKEN_TILE
    win = min(seq_len, tq + 2 * POOL_HALO)
    for sub in range(o_ref.shape[1] // tq):
        t0 = pl.program_id(1) * o_ref.shape[1] + sub * tq
        src0 = pl.multiple_of(jnp.clip(t0 - POOL_HALO, 0, seq_len - win), POOL_HALO)
        u = u_ref[0, pl.ds(src0, win), :]
        t = t0 + lax.broadcasted_iota(jnp.int32, (tq, win), 0)
        j = src0 + lax.broadcasted_iota(jnp.int32, (tq, win), 1)
        tc = t0 + lax.broadcasted_iota(jnp.int32, (tq, 1), 0)
        ys = []
        for gi, w in enumerate(POOL_WINDOWS):
            lo = jnp.maximum(t - w // 2, 0)
            hi = jnp.minimum(t + (w - w // 2), seq_len)
            cnt = (hi - lo).astype(F32)
            a = jnp.where((j >= lo) & (j < hi), 1.0, 0.0) - jnp.where(j == t, cnt, 0.0)
            cnt_col = (jnp.minimum(tc + (w - w // 2), seq_len) - jnp.maximum(tc - w // 2, 0)).astype(F32)
            sl = slice(gi * POOL_GROUP_DIM, (gi + 1) * POOL_GROUP_DIM)
            d = _dot(a.astype(BF16), u[:, sl]) / cnt_col
            ys.append(_dot(d.astype(BF16), pw_ref[gi]))
        y = jnp.concatenate(ys, axis=-1) * ps_ref[...]
        o_ref[0, sub * tq:(sub + 1) * tq, :] = (_rms(y) * g_ref[...]).astype(BF16)


def _pool_mixer(up, pool_w_b, pool_scale, g_out_pool):
    b, seq_len, _ = up.shape
    tq = TOKEN_TILE * min(POOL_TILES_PER_STEP, seq_len // TOKEN_TILE)
    return pl.pallas_call(
        functools.partial(_pool_kernel, seq_len=seq_len),
        grid=(b, seq_len // tq),
        in_specs=[
            pl.BlockSpec((1, seq_len, POOL_WIDTH), lambda bi, i: (bi, 0, 0)),
            pl.BlockSpec((len(POOL_WINDOWS), POOL_GROUP_DIM, POOL_GROUP_DIM), lambda bi, i: (0, 0, 0)),
            pl.BlockSpec((1, POOL_WIDTH), lambda bi, i: (0, 0)),
            pl.BlockSpec((1, POOL_WIDTH), lambda bi, i: (0, 0)),
        ],
        out_specs=pl.BlockSpec((1, tq, POOL_WIDTH), lambda bi, i: (bi, i, 0)),
        out_shape=jax.ShapeDtypeStruct((b, seq_len, POOL_WIDTH), BF16),
        compiler_params=_params(2),
        name=f"pool_mixer_{seq_len}",
    )(up, pool_w_b, pool_scale, g_out_pool)


def _stack_heads(q, kh, rows):
    return jnp.concatenate(
        [q[:, (kh * Q_PER_KV + g) * HEAD_DIM:(kh * Q_PER_KV + g + 1) * HEAD_DIM]
         for g in range(Q_PER_KV)], axis=0)


def _sink_column(sink_ref, kh, rows):
    return jnp.concatenate(
        [jnp.broadcast_to(sink_ref[kh * Q_PER_KV + g:kh * Q_PER_KV + g + 1, 0:1] * LOG2E, (rows, 1))
         for g in range(Q_PER_KV)], axis=0)


def _attend(q4, key_sets, value_sets, sk, band=None):
    s = [_dot_nt(q4, k) for k in key_sets]
    if band is not None:
        s[0] = jnp.where(band, s[0], NEG_INF)
    m = sk
    for si in s:
        m = jnp.maximum(m, jnp.max(si, axis=-1, keepdims=True))
    acc = None
    for si, v in zip(s, value_sets):
        v1 = jnp.concatenate([v, jnp.ones_like(v)], axis=-1)
        part = _dot(jnp.exp2(si - m).astype(BF16), v1)
        acc = part if acc is None else acc + part
    den = acc[:, HEAD_DIM:] + jnp.exp2(sk - m)
    return acc[:, :HEAD_DIM] / den


def _ctx_attn_kernel(q_ref, k_ref, v_ref, sink_ref, g_ref, o_ref, *, seq_len):
    rows = seq_len
    for j in range(q_ref.shape[0] // rows):
        rs = slice(j * rows, (j + 1) * rows)
        q = q_ref[rs, :]
        heads = [None] * N_HEADS
        for kh in range(N_KV_HEADS):
            sl = slice(kh * HEAD_DIM, (kh + 1) * HEAD_DIM)
            o = _attend(_stack_heads(q, kh, rows), [k_ref[rs, sl].astype(BF16)],
                        [v_ref[rs, sl].astype(BF16)], _sink_column(sink_ref, kh, rows))
            for g in range(Q_PER_KV):
                heads[kh * Q_PER_KV + g] = o[g * rows:(g + 1) * rows]
        y = jnp.concatenate(heads, axis=-1)
        o_ref[rs, :] = (_rms(y) * g_ref[...]).astype(BF16)


def _context_attention(q, k, v, sink_b, g_out_attn, seq_len):
    t = q.shape[0]
    blk = seq_len * CTX_SEQS_PER_STEP
    return pl.pallas_call(
        functools.partial(_ctx_attn_kernel, seq_len=seq_len),
        grid=(t // blk,),
        in_specs=[
            pl.BlockSpec((blk, ATTN_WIDTH), lambda b: (b, 0)),
            pl.BlockSpec((blk, KV_WIDTH), lambda b: (b, 0)),
            pl.BlockSpec((blk, KV_WIDTH), lambda b: (b, 0)),
            pl.BlockSpec((N_HEADS, LANES), lambda b: (0, 0)),
            pl.BlockSpec((1, ATTN_WIDTH), lambda b: (0, 0)),
        ],
        out_specs=pl.BlockSpec((blk, ATTN_WIDTH), lambda b: (b, 0)),
        out_shape=jax.ShapeDtypeStruct((t, ATTN_WIDTH), BF16),
        compiler_params=_params(1),
        name="context_attention",
    )(q, k, v, sink_b, g_out_attn)


def _lat_attn_kernel(q_ref, k_ref, v_ref, ck_ref, cv_ref, sink_ref, g_ref, o_ref, *, seq_len):
    rows = Q_BLOCK
    span = 3 * Q_BLOCK
    shape = (Q_PER_KV * rows, span)
    row_in_block = lax.broadcasted_iota(jnp.int32, shape, 0) & (rows - 1)
    col = lax.broadcasted_iota(jnp.int32, shape, 1)
    for j in range(Q_BLOCKS_PER_STEP):
        q0 = (pl.program_id(1) * Q_BLOCKS_PER_STEP + j) * rows
        start = pl.multiple_of(jnp.clip(q0 - Q_BLOCK, 0, seq_len - span), Q_BLOCK)
        q = q_ref[j * rows:(j + 1) * rows, :]
        kl = k_ref[0, pl.ds(start, span), :]
        vl = v_ref[0, pl.ds(start, span), :]
        band = jnp.abs(row_in_block - col + (q0 - start)) <= WINDOW
        heads = [None] * N_HEADS
        for kh in range(N_KV_HEADS):
            sl = slice(kh * HEAD_DIM, (kh + 1) * HEAD_DIM)
            o = _attend(_stack_heads(q, kh, rows),
                        [kl[:, sl], ck_ref[0, :, sl].astype(BF16)],
                        [vl[:, sl], cv_ref[0, :, sl].astype(BF16)],
                        _sink_column(sink_ref, kh, rows), band)
            for g in range(Q_PER_KV):
                heads[kh * Q_PER_KV + g] = o[g * rows:(g + 1) * rows]
        y = jnp.concatenate(heads, axis=-1)
        o_ref[j * rows:(j + 1) * rows, :] = (_rms(y) * g_ref[...]).astype(BF16)


def _latent_attention(q, k, v, cache_k, cache_v, sink_b, g_out_attn):
    b, seq_len, _ = k.shape
    q_rows = Q_BLOCK * Q_BLOCKS_PER_STEP
    nq = seq_len // q_rows
    past = cache_k.shape[1]
    return pl.pallas_call(
        functools.partial(_lat_attn_kernel, seq_len=seq_len),
        grid=(b, nq),
        in_specs=[
            pl.BlockSpec((q_rows, ATTN_WIDTH), lambda bi, n: (bi * nq + n, 0)),
            pl.BlockSpec((1, seq_len, KV_WIDTH), lambda bi, n: (bi, 0, 0)),
            pl.BlockSpec((1, seq_len, KV_WIDTH), lambda bi, n: (bi, 0, 0)),
            pl.BlockSpec((1, past, KV_WIDTH), lambda bi, n: (bi, 0, 0)),
            pl.BlockSpec((1, past, KV_WIDTH), lambda bi, n: (bi, 0, 0)),
            pl.BlockSpec((N_HEADS, LANES), lambda bi, n: (0, 0)),
            pl.BlockSpec((1, ATTN_WIDTH), lambda bi, n: (0, 0)),
        ],
        out_specs=pl.BlockSpec((q_rows, ATTN_WIDTH), lambda bi, n: (bi * nq + n, 0)),
        out_shape=jax.ShapeDtypeStruct((b * seq_len, ATTN_WIDTH), BF16),
        compiler_params=_params(2),
        name="latent_attention",
    )(q, k, v, cache_k, cache_v, sink_b, g_out_attn)


def _route(logits):
    lane = lax.broadcasted_iota(jnp.int32, logits.shape, 1).astype(F32)
    neg = -jnp.inf

    def first_argmax(x):
        mx = jnp.max(x, axis=-1, keepdims=True)
        return mx, jnp.min(jnp.where(x == mx, lane, float(ROUTER_LANES)), axis=-1, keepdims=True)

    gl = jnp.where(lane < N_EXPERT_GROUPS, logits, neg)
    gmax, g_idx = first_argmax(gl)
    p_g = 1.0 / jnp.sum(jnp.exp(gl - gmax), axis=-1, keepdims=True)
    base = N_EXPERT_GROUPS + EXPERTS_PER_GROUP * g_idx
    el = jnp.where((lane >= base) & (lane < base + EXPERTS_PER_GROUP), logits, neg)
    v1, i1 = first_argmax(el)
    v2, i2 = first_argmax(jnp.where(lane == i1, neg, el))
    e2 = jnp.exp(v2 - v1)
    w1 = p_g / (1.0 + e2)
    w2 = p_g * e2 / (1.0 + e2)
    ids = jnp.where(lane == 0.0, i1 - N_EXPERT_GROUPS, jnp.where(lane == 1.0, i2 - N_EXPERT_GROUPS, 0.0))
    wts = jnp.where(lane == 0.0, w1, jnp.where(lane == 1.0, w2, 0.0))
    return ids.astype(jnp.int32), wts


def _outproj_kernel(rows_ref, pool_p, attn_p, x_p, pool_s, attn_s, x_s, mod_ref, g_ref, w_ref, rw_ref,
                    rb_ref, x1_ref, h2_ref, ids_ref, wts_ref, mix0, mix1, *, n_ctx_tiles, n_tiles):
    del rows_ref
    i = pl.program_id(0)

    def project(pool_ref, attn_ref, mix_ref):
        mix_ref[...] = (_dot(pool_ref[...], w_ref[0:POOL_WIDTH, :])
                        + _dot(attn_ref[...], w_ref[POOL_WIDTH:POOL_WIDTH + ATTN_WIDTH, :]))

    def tail(x_ref, mix_ref):
        m = mod_ref[0]
        x1 = x_ref[...] + m[2:3] * mix_ref[...]
        x1_ref[...] = x1
        h2 = _rms(x1) * g_ref[...] * (1.0 + m[4:5]) + m[3:4]
        _store_row_tiles(h2_ref, _pack_bf16_pairs(h2))
        hi = h2.astype(BF16)
        lo = (h2 - hi.astype(F32)).astype(BF16)
        both = _dot(hi, rw_ref[...])
        logits = (both[:, :ROUTER_LANES] + both[:, ROUTER_LANES:] + _dot(lo, rw_ref[:, :ROUTER_LANES])
                  + rb_ref[...])
        ids, wts = _route(logits)
        ids_ref[...] = ids
        wts_ref[...] = wts

    ctx_in = (pool_p, attn_p)
    lat_in = (pool_s, attn_s)
    for parity, (cur, prev) in enumerate(((mix0, mix1), (mix1, mix0))):
        mine = (i % 2) == parity

        @pl.when(mine & (i == 0))
        def _(cur=cur):
            project(*ctx_in, cur)

        @pl.when(mine & (i >= 1) & (i < n_ctx_tiles))
        def _(cur=cur, prev=prev):
            tail(x_p, prev)
            project(*ctx_in, cur)

        @pl.when(mine & (i == n_ctx_tiles))
        def _(cur=cur, prev=prev):
            tail(x_p, prev)
            project(*lat_in, cur)

        @pl.when(mine & (i > n_ctx_tiles) & (i < n_tiles))
        def _(cur=cur, prev=prev):
            tail(x_s, prev)
            project(*lat_in, cur)

        @pl.when(mine & (i == n_tiles))
        def _(prev=prev):
            tail(x_s, prev)


def _out_projection(ctx, lat, rows, mod3, g_ffn, w_out_b, rw_split, rb):
    tm = TOKEN_TILE
    n_ctx = ctx[2].shape[0] // tm
    n_lat = lat[2].shape[0] // tm
    n_tiles = n_ctx + n_lat
    t_all = n_tiles * tm
    assert n_ctx >= 1 and n_lat >= 1

    def ctx_map(i, r):
        return (jnp.minimum(i, n_ctx - 1), 0)

    def lat_map(i, r):
        return (jnp.clip(i - n_ctx, 0, n_lat - 1), 0)

    def ctx_prev_map(i, r):
        return (jnp.clip(i - 1, 0, n_ctx - 1), 0)

    def lat_prev_map(i, r):
        return (jnp.clip(i - 1 - n_ctx, 0, n_lat - 1), 0)

    def prev_tile(i):
        return jnp.maximum(i - 1, 0)

    in_specs = [
        pl.BlockSpec((tm, POOL_WIDTH), ctx_map),
        pl.BlockSpec((tm, ATTN_WIDTH), ctx_map),
        pl.BlockSpec((tm, D_MODEL), ctx_prev_map),
        pl.BlockSpec((tm, POOL_WIDTH), lat_map),
        pl.BlockSpec((tm, ATTN_WIDTH), lat_map),
        pl.BlockSpec((tm, D_MODEL), lat_prev_map),
        pl.BlockSpec((1, 6, D_MODEL), lambda i, r: (r[prev_tile(i)], 0, 0)),
        pl.BlockSpec((1, D_MODEL), lambda i, r: (0, 0)),
        pl.BlockSpec((D_MODEL, D_MODEL), lambda i, r: (0, 0), pipeline_mode=pl.Buffered(1)),
        pl.BlockSpec((D_MODEL, 2 * ROUTER_LANES), lambda i, r: (0, 0)),
        pl.BlockSpec((1, ROUTER_LANES), lambda i, r: (0, 0)),
    ]
    out_shape = (
        jax.ShapeDtypeStruct((t_all, D_MODEL), F32),
        jax.ShapeDtypeStruct((t_all * ROW_TILE, LANES), jnp.uint32),
        jax.ShapeDtypeStruct((t_all, ROUTER_LANES), jnp.int32),
        jax.ShapeDtypeStruct((t_all, ROUTER_LANES), F32),
    )
    out_specs = (
        pl.BlockSpec((tm, D_MODEL), lambda i, r: (prev_tile(i), 0)),
        pl.BlockSpec((tm * ROW_TILE, LANES), lambda i, r: (prev_tile(i), 0)),
        pl.BlockSpec((tm, ROUTER_LANES), lambda i, r: (prev_tile(i), 0)),
        pl.BlockSpec((tm, ROUTER_LANES), lambda i, r: (prev_tile(i), 0)),
    )
    return pl.pallas_call(
        functools.partial(_outproj_kernel, n_ctx_tiles=n_ctx, n_tiles=n_tiles),
        grid_spec=pltpu.PrefetchScalarGridSpec(
            num_scalar_prefetch=1, grid=(n_tiles + 1,), in_specs=in_specs, out_specs=out_specs,
            scratch_shapes=[pltpu.VMEM((tm, D_MODEL), F32), pltpu.VMEM((tm, D_MODEL), F32)]),
        out_shape=out_shape,
        compiler_params=_params(1),
        name="out_projection",
    )(rows, *ctx, *lat, mod3, g_ffn, w_out_b, rw_split, rb)


def _gather_rows(idx_ref, base, n_rows, src_hbm, dst, sem, *, unrolled, both_queues=False, dst_row0=0):
    def one(r, priority):
        src_row = pl.multiple_of(idx_ref[base + r], ROW_TILE)
        pltpu.make_async_copy(
            src_hbm.at[pl.ds(src_row, ROW_TILE)], dst.at[pl.ds((dst_row0 + r) * ROW_TILE, ROW_TILE)],
            sem).start(priority=priority)

    if unrolled:
        for r in range(n_rows):
            one(r, r % 2 if both_queues else 0)
    else:
        def body(r, carry):
            one(r, 0)
            return carry
        lax.fori_loop(0, n_rows, body, 0, unroll=8)


def _wait_rows(src_hbm, dst, sem):
    pltpu.make_async_copy(src_hbm.at[pl.ds(0, dst.shape[0])], dst, sem).wait()


def _late_zero(x):
    u = lax.bitcast_convert_type(x, jnp.uint32)
    return ((u >> 16) >> 16).astype(jnp.int32)[0, 0]


def _moe_kernel(be_ref, ne_ref, nu_ref, first_ref, tok_ref, h_hbm, wg_hbm, wu_hbm, wd_hbm, y_ref,
                xg0, xg1, xg2, sg, su, sd, wgb, wub, wdb, sem, wsem):
    b = pl.program_id(0)
    n_used = nu_ref[0]
    last = n_used - 1
    active = b < n_used
    bufs = (xg0, xg1, xg2)
    n_buf = len(bufs)

    def weight_copies(e):
        return (pltpu.make_async_copy(wg_hbm.at[e], sg, wsem.at[0]),
                pltpu.make_async_copy(wu_hbm.at[e], su, wsem.at[1]),
                pltpu.make_async_copy(wd_hbm.at[e], sd, wsem.at[2]))

    @pl.when(b == 0)
    def _():
        for cp in weight_copies(be_ref[0]):
            cp.start(priority=1)
        for i in range(n_buf - 1):
            _gather_rows(tok_ref, first_ref[jnp.minimum(i, last)], EXPERT_ROWS, h_hbm, bufs[i], sem.at[i],
                         unrolled=False)

    new_expert = (b == 0) | (be_ref[b] != be_ref[jnp.maximum(b - 1, 0)])

    @pl.when(active & new_expert)
    def _():
        for cp in weight_copies(be_ref[b]):
            cp.wait()
        wgb[...] = sg[...].astype(BF16)
        wub[...] = su[...].astype(BF16)
        wdb[...] = sd[...].astype(BF16)

        @pl.when(ne_ref[b] >= 0)
        def _():
            for cp in weight_copies(ne_ref[b]):
                cp.start(priority=1)

    def step(slot):
        cur = bufs[slot]
        ahead = bufs[(slot + n_buf - 1) % n_buf]
        sem_ahead = sem.at[(slot + n_buf - 1) % n_buf]
        _wait_rows(h_hbm, cur, sem.at[slot])
        base = first_ref[jnp.minimum(b + n_buf - 1, last)]
        per = EXPERT_ROWS // GATHER_GROUPS
        group = iter(range(GATHER_GROUPS))

        def issue(zero):
            g = next(group)
            _gather_rows(tok_ref, base + g * per + zero, per, h_hbm, ahead, sem_ahead, unrolled=True,
                         both_queues=True, dst_row0=g * per)

        def corner(x, r, c):
            return _late_zero(x[r:r + 1, c:c + 1])

        bm, de = EXPERT_ROWS, D_EXPERT
        lo, hi = _unpack_bf16_pairs(_load_row_tiles(cur, bm))
        lo = lo.astype(BF16)
        hi = hi.astype(BF16)

        def issue_along(x, n):
            rows, cols = x.shape
            for i in range(n):
                issue(corner(x, (i + 1) * rows // n - 1, (i + 1) * cols // n - 1))

        n_each = GATHER_GROUPS // 8
        issue(0)
        a1 = _dot(lo, wgb[0:PACKED, :])
        issue_along(a1, n_each)
        a = a1 + _dot(hi, wgb[PACKED:D_MODEL, :])
        issue_along(a, n_each)
        u1 = _dot(lo, wub[0:PACKED, :])
        issue_along(u1, n_each)
        u = u1 + _dot(hi, wub[PACKED:D_MODEL, :])
        issue_along(u, n_each)
        act = (a / (1.0 + jnp.exp(-a)) * u).astype(BF16)
        y = _dot(act, wdb[...])
        issue_along(y, GATHER_GROUPS - 1 - 4 * n_each)
        _store_row_tiles(y_ref, _pack_bf16_pairs(y))

        @pl.when(b == last)
        def _():
            for i in range(1, n_buf):
                _wait_rows(h_hbm, bufs[(slot + i) % n_buf], sem.at[(slot + i) % n_buf])

    for slot in range(n_buf):
        @pl.when(active & (b % n_buf == slot))
        def _(slot=slot):
            step(slot)

    @pl.when(b >= n_used)
    def _():
        y_ref[...] = jnp.zeros_like(y_ref)


def _experts(tables, cap, h2_packed, w_gate, w_up, w_down):
    bm = EXPERT_ROWS
    xg = pltpu.VMEM((bm * ROW_TILE, LANES), jnp.uint32)
    return pl.pallas_call(
        _moe_kernel,
        grid_spec=pltpu.PrefetchScalarGridSpec(
            num_scalar_prefetch=len(tables),
            grid=(cap // bm,),
            in_specs=[pl.BlockSpec(memory_space=pl.ANY)] * 4,
            out_specs=pl.BlockSpec((bm * ROW_TILE, LANES), lambda b, *_: (b, 0)),
            scratch_shapes=[
                xg, xg, xg,
                pltpu.VMEM((D_MODEL, D_EXPERT), F32), pltpu.VMEM((D_MODEL, D_EXPERT), F32),
                pltpu.VMEM((D_EXPERT, D_MODEL), F32),
                pltpu.VMEM((D_MODEL, D_EXPERT), BF16), pltpu.VMEM((D_MODEL, D_EXPERT), BF16),
                pltpu.VMEM((D_EXPERT, D_MODEL), BF16),
                pltpu.SemaphoreType.DMA((3,)), pltpu.SemaphoreType.DMA((3,)),
            ],
        ),
        out_shape=jax.ShapeDtypeStruct((cap * ROW_TILE, LANES), jnp.uint32),
        compiler_params=_params(1),
        name="experts",
    )(*tables, h2_packed, w_gate, w_up, w_down)


def _combine_kernel(slot_ref, rows_ref, y_hbm, x1_ref, wts_ref, mod_ref, g_ref, op_ref, os_ref, yg0, yg1, sem,
                    *, n_ctx_tiles, n_tiles):
    del rows_ref
    tm = TOKEN_TILE
    i = pl.program_id(0)
    n_slots = slot_ref.shape[0] // TOP_K

    def start(tile, buf, s, unrolled):
        for k in range(TOP_K):
            _gather_rows(slot_ref, k * n_slots + tile * tm, tm, y_hbm, buf.at[k], s, unrolled=unrolled,
                         both_queues=True)

    @pl.when(i == 0)
    def _():
        start(0, yg0, sem.at[0], False)

    def step(cur, nxt, sem_cur, sem_nxt):
        for k in range(TOP_K):
            _wait_rows(y_hbm, cur.at[k], sem_cur)
        start(jnp.minimum(i + 1, n_tiles - 1), nxt, sem_nxt, True)
        w = wts_ref[...]
        lo0, hi0 = _unpack_bf16_pairs(_load_row_tiles(cur.at[0], tm))
        lo1, hi1 = _unpack_bf16_pairs(_load_row_tiles(cur.at[1], tm))
        ffn = jnp.concatenate([lo0 * w[:, 0:1] + lo1 * w[:, 1:2], hi0 * w[:, 0:1] + hi1 * w[:, 1:2]], axis=-1)
        x2 = x1_ref[...] + mod_ref[0][5:6] * ffn
        out = _rms(x2) * g_ref[...]

        @pl.when(i < n_ctx_tiles)
        def _():
            op_ref[...] = out

        @pl.when(i >= n_ctx_tiles)
        def _():
            os_ref[...] = out

        @pl.when(i == n_tiles - 1)
        def _():
            for k in range(TOP_K):
                _wait_rows(y_hbm, nxt.at[k], sem_nxt)

    @pl.when(i % 2 == 0)
    def _():
        step(yg0, yg1, sem.at[0], sem.at[1])

    @pl.when(i % 2 == 1)
    def _():
        step(yg1, yg0, sem.at[1], sem.at[0])


def _combine(slots, rows, y_buf, x1, wts, mod3, g_final, t_ctx):
    t = x1.shape[0]
    tm = TOKEN_TILE
    n_tiles = t // tm
    n_ctx = t_ctx // tm
    return pl.pallas_call(
        functools.partial(_combine_kernel, n_ctx_tiles=n_ctx, n_tiles=n_tiles),
        grid_spec=pltpu.PrefetchScalarGridSpec(
            num_scalar_prefetch=2,
            grid=(n_tiles,),
            in_specs=[
                pl.BlockSpec(memory_space=pl.ANY),
                pl.BlockSpec((tm, D_MODEL), lambda i, s, r: (i, 0)),
                pl.BlockSpec((tm, ROUTER_LANES), lambda i, s, r: (i, 0)),
                pl.BlockSpec((1, 6, D_MODEL), lambda i, s, r: (r[i], 0, 0)),
                pl.BlockSpec((1, D_MODEL), lambda i, s, r: (0, 0)),
            ],
            out_specs=(
                pl.BlockSpec((tm, D_MODEL), lambda i, s, r: (jnp.minimum(i, n_ctx - 1), 0)),
                pl.BlockSpec((tm, D_MODEL), lambda i, s, r: (jnp.maximum(i - n_ctx, 0), 0)),
            ),
            scratch_shapes=[pltpu.VMEM((TOP_K, tm * ROW_TILE, LANES), jnp.uint32),
                            pltpu.VMEM((TOP_K, tm * ROW_TILE, LANES), jnp.uint32),
                            pltpu.SemaphoreType.DMA((2,))],
        ),
        out_shape=(jax.ShapeDtypeStruct((t_ctx, D_MODEL), F32),
                   jax.ShapeDtypeStruct((t - t_ctx, D_MODEL), F32)),
        compiler_params=_params(1),
        name="combine",
    )(slots, rows, y_buf, x1, wts, mod3, g_final)


def _rope_tables(n_tokens):
    rows = n_tokens // GRID_W
    row = jnp.repeat(jnp.arange(rows, dtype=F32), GRID_W)
    col = jnp.tile(jnp.arange(GRID_W, dtype=F32), rows)
    half = ROPE_AXIS_DIM // 2
    inv_freq = ROPE_BASE ** (-jnp.arange(half, dtype=F32) / half)
    ar = row[:, None] * inv_freq
    ac = col[:, None] * inv_freq
    cos = jnp.concatenate([jnp.cos(ar), jnp.cos(ar), jnp.cos(ac), jnp.cos(ac)], axis=-1)
    sin = jnp.concatenate([-jnp.sin(ar), jnp.sin(ar), -jnp.sin(ac), jnp.sin(ac)], axis=-1)
    return cos, sin


def _dispatch_tables(expert_ids):
    t = expert_ids[0].shape[0]
    bm = EXPERT_ROWS
    flat_e = jnp.concatenate(expert_ids)
    onehot = (flat_e[:, None] == jnp.arange(N_EXPERTS, dtype=jnp.int32)[None, :]).astype(jnp.int32)
    running = jnp.cumsum(onehot, axis=0)
    rank = jnp.sum(running * onehot, axis=1) - 1
    counts = running[-1]
    padded = (counts + bm - 1) // bm * bm
    pad_end = jnp.cumsum(padded)
    pad_start = pad_end - padded
    dest = (jnp.sum(pad_start[None, :] * onehot, axis=1) + rank).astype(jnp.int32)
    n_blocks = (t * TOP_K + N_EXPERTS * (bm - 1) + bm - 1) // bm
    order = jnp.argsort(flat_e).astype(jnp.int32)
    src_tile = jnp.where(order >= t, order - t, order) * ROW_TILE
    src_tile = jnp.concatenate([src_tile, jnp.zeros((bm,), jnp.int32)])
    n_used = (pad_end[-1] // bm).astype(jnp.int32)
    blk = jnp.arange(n_blocks, dtype=jnp.int32)
    blk = jnp.minimum(blk, n_used - 1)
    block_expert = jnp.sum((pad_end[None, :] <= (blk * bm)[:, None]).astype(jnp.int32), axis=1)
    block_expert = jnp.minimum(block_expert, N_EXPERTS - 1)
    e_ids = jnp.arange(N_EXPERTS, dtype=jnp.int32)
    later = (e_ids[None, :] > e_ids[:, None]) & (counts[None, :] > 0)
    next_of = jnp.min(jnp.where(later, e_ids[None, :], N_EXPERTS), axis=1)
    next_of = jnp.where(next_of == N_EXPERTS, -1, next_of)
    of_block = (block_expert[:, None] == e_ids[None, :]).astype(jnp.int32)
    next_expert = jnp.sum(next_of[None, :] * of_block, axis=1)
    start = jnp.cumsum(counts) - counts
    first_src = jnp.sum((start - pad_start)[None, :] * of_block, axis=1) + blk * bm
    return (block_expert.astype(jnp.int32), next_expert.astype(jnp.int32), n_used.reshape(1),
            first_src.astype(jnp.int32), src_tile, dest * ROW_TILE, n_blocks * bm)


def kernel(x_prompt, x_sample, cache_k, cache_v, c, c_ctx, w_mod, b_mod, norm_mix_g, norm_ffn_g, w_in,
           pool_w, pool_scale, attn_sink, out_norm_pool_g, out_norm_attn_g, w_out, router_group_w,
           router_group_b, router_expert_w, router_expert_b, w_gate, w_up, w_down, final_norm_g):
    depth = w_mod.shape[0]
    assert depth == 1, "single trunk layer"
    bp, lp, _ = x_prompt.shape
    bs, ls, _ = x_sample.shape
    tp, ts = bp * lp, bs * ls
    tm = TOKEN_TILE
    l = 0

    cond8 = jnp.zeros((8, D_MODEL), F32).at[:bs].set(c).at[bs].set(c_ctx)
    mod3 = _modulation(cond8, w_mod[l], b_mod[l]).reshape(8, 6, D_MODEL)
    rows_p = jnp.full((tp // tm,), bs, jnp.int32)
    rows_s = jnp.arange(ts // tm, dtype=jnp.int32) // (ls // tm)

    w_in_b = w_in[l].astype(BF16)
    w_out_b = w_out[l].astype(BF16)
    pool_w_b = pool_w[l].astype(BF16)
    g_mix = norm_mix_g[l].reshape(1, D_MODEL)
    g_ffn = norm_ffn_g[l].reshape(1, D_MODEL)
    g_pool = out_norm_pool_g[l].reshape(1, POOL_WIDTH)
    g_attn = out_norm_attn_g[l].reshape(1, ATTN_WIDTH)
    p_scale = pool_scale[l].reshape(1, POOL_WIDTH)
    sink_b = jnp.broadcast_to(attn_sink[l][:, None], (N_HEADS, LANES))
    g_final = final_norm_g.reshape(1, D_MODEL)

    rw = jnp.concatenate(
        [router_group_w[l], jnp.transpose(router_expert_w[l], (1, 0, 2)).reshape(D_MODEL, N_EXPERTS)], axis=1)
    rw = jnp.pad(rw, ((0, 0), (0, ROUTER_LANES - rw.shape[1])))
    rw_hi = rw.astype(BF16)
    rw_split = jnp.concatenate([rw_hi, (rw - rw_hi.astype(F32)).astype(BF16)], axis=1)
    rb = jnp.concatenate([router_group_b[l], router_expert_b[l].reshape(N_EXPERTS)])
    rb = jnp.pad(rb, (0, ROUTER_LANES - rb.shape[0])).reshape(1, ROUTER_LANES)

    xp = x_prompt.reshape(tp, D_MODEL)
    xs = x_sample.reshape(ts, D_MODEL)

    up_p, q_p, k_p, v_p = _in_projection(xp, rows_p, mod3, g_mix, w_in_b, None, F32)
    pool_p = _pool_mixer(up_p.reshape(bp, lp, POOL_WIDTH), pool_w_b, p_scale, g_pool).reshape(tp, POOL_WIDTH)
    attn_p = _context_attention(q_p, k_p, v_p, sink_b, g_attn, lp)
    up_s, q_s, k_s, v_s = _in_projection(xs, rows_s, mod3, g_mix, w_in_b, _rope_tables(ls), BF16)
    pool_s = _pool_mixer(up_s.reshape(bs, ls, POOL_WIDTH), pool_w_b, p_scale, g_pool).reshape(ts, POOL_WIDTH)
    attn_s = _latent_attention(
        q_s, k_s.reshape(bs, ls, KV_WIDTH), v_s.reshape(bs, ls, KV_WIDTH),
        cache_k[:, l].reshape(bs, -1, KV_WIDTH), cache_v[:, l].reshape(bs, -1, KV_WIDTH), sink_b, g_attn)

    rows_all = jnp.concatenate([rows_p, rows_s])
    x1_all, h2_all, ids_all, wts_all = _out_projection(
        (pool_p, attn_p, xp), (pool_s, attn_s, xs), rows_all, mod3, g_ffn, w_out_b, rw_split, rb)

    *expert_tables, slots, cap = _dispatch_tables([ids_all[:, k] for k in range(TOP_K)])
    y_buf = _experts(expert_tables, cap, h2_all, w_gate[l], w_up[l], w_down[l])

    y_p, y_s = _combine(slots, rows_all, y_buf, x1_all, wts_all, mod3, g_final, tp)

    kv_shape = (bp, 1, lp, N_KV_HEADS, HEAD_DIM)
    return (y_p.reshape(bp, lp, D_MODEL), y_s.reshape(bs, ls, D_MODEL),
            k_p.reshape(kv_shape), v_p.reshape(kv_shape))
```

```python
import functools

import jax
import jax.numpy as jnp
from jax import lax
from jax.experimental import pallas as pl
from jax.experimental.pallas import tpu as pltpu

F32 = jnp.float32
BF16 = jnp.bfloat16

D_MODEL = 2048
GRID_W = 64
HEAD_DIM = 128
ATTN_WIDTH = 1024
POOL_WIDTH = 1024
N_HEADS = 8
N_KV_HEADS = 2
Q_PER_KV = 4
KV_WIDTH = 256
IN_WIDTH = 2560
WINDOW = 128
POOL_WINDOWS = (2, 4, 8, 16)
POOL_GROUP_DIM = 256
ROPE_BASE = 10000.0
ROPE_AXIS_DIM = 64
N_EXPERT_GROUPS = 4
EXPERTS_PER_GROUP = 8
N_EXPERTS = 32
TOP_K = 2
D_EXPERT = 512
NORM_EPS = 1e-6
NEG_INF = -1e30
ATTN_SCALE = HEAD_DIM ** -0.5
LOG2E = 1.4426950408889634
Q_SCALE = ATTN_SCALE * LOG2E

LANES = 128
Q_BLOCKS_PER_STEP = 4
CTX_SEQS_PER_STEP = 2
VMEM_LIMIT = 48 * 1024 * 1024
PACKED = D_MODEL // 2
ROW_TILE = 8
assert PACKED == ROW_TILE * LANES, "a packed row must fill exactly one tile"
MOD_TILE = 1024
TOKEN_TILE = 256
IN_TILE = 512
POOL_TILES_PER_STEP = 2
POOL_HALO = 16
Q_BLOCK = 128
EXPERT_ROWS = 256
GATHER_GROUPS = 1
ROUTER_LANES = 128


def _params(n_grid):
    return pltpu.CompilerParams(
        dimension_semantics=("arbitrary",) * n_grid, vmem_limit_bytes=VMEM_LIMIT)


def _rms(x):
    return x * lax.rsqrt(jnp.mean(x * x, axis=-1, keepdims=True) + NORM_EPS)


def _dot(a, b):
    return jnp.dot(a, b, preferred_element_type=F32)


def _dot_nt(a, b):
    return lax.dot_general(a, b, (((1,), (1,)), ((), ())), preferred_element_type=F32)


def _pack_bf16_pairs(x):
    c = x.shape[1] // 2
    lo = lax.bitcast_convert_type(x[:, :c].astype(BF16).astype(F32), jnp.uint32)
    hi = lax.bitcast_convert_type(x[:, c:].astype(BF16).astype(F32), jnp.uint32)
    return hi | (lo >> 16)


def _store_row_tiles(ref, x):
    n = x.shape[0]
    for c in range(PACKED // LANES):
        ref[pl.ds(c, n, stride=ROW_TILE), :] = x[:, c * LANES:(c + 1) * LANES]


def _load_row_tiles(ref, n):
    return jnp.concatenate(
        [ref[pl.ds(c, n, stride=ROW_TILE), :] for c in range(PACKED // LANES)], axis=-1)


def _unpack_bf16_pairs(u):
    lo = lax.bitcast_convert_type(u << 16, F32)
    hi = lax.bitcast_convert_type(u & jnp.uint32(0xFFFF0000), F32)
    return lo, hi


def _mod_kernel(cond_ref, w_ref, b_ref, o_ref):
    c = cond_ref[...]
    s = c / (1.0 + jnp.exp(-c))
    o_ref[...] = _dot(s.astype(BF16), w_ref[...].astype(BF16)) + b_ref[...]


def _modulation(cond8, w_mod, b_mod):
    n = w_mod.shape[1]
    tn = MOD_TILE
    return pl.pallas_call(
        _mod_kernel,
        grid=(n // tn,),
        in_specs=[
            pl.BlockSpec((8, D_MODEL), lambda j: (0, 0)),
            pl.BlockSpec((D_MODEL, tn), lambda j: (0, j)),
            pl.BlockSpec((1, tn), lambda j: (0, j)),
        ],
        out_specs=pl.BlockSpec((8, tn), lambda j: (0, j)),
        out_shape=jax.ShapeDtypeStruct((8, n), F32),
        compiler_params=_params(1),
        name="modulation",
    )(cond8, w_mod, b_mod.reshape(1, n))


def _rope(x, cos, sin_signed):
    lane = lax.broadcasted_iota(jnp.int32, x.shape, 1)
    partner = jnp.where((lane & 63) < 32,
                        pltpu.roll(x, HEAD_DIM - 32, axis=1),
                        pltpu.roll(x, 32, axis=1))
    return x * cos + partner * sin_signed


def _inproj_kernel(rows_ref, x_ref, mod_ref, g_ref, w_ref, *rest, rope):
    del rows_ref
    if rope:
        cos_ref, sin_ref, up_ref, q_ref, k_ref, v_ref = rest
    else:
        up_ref, q_ref, k_ref, v_ref = rest
    m = mod_ref[0]
    h = _rms(x_ref[...]) * g_ref[...]
    hb = (h * (1.0 + m[1:2]) + m[0:1]).astype(BF16)
    up_ref[...] = _dot(hb, w_ref[:, 0:POOL_WIDTH]).astype(BF16)
    q = _dot(hb, w_ref[:, POOL_WIDTH:POOL_WIDTH + ATTN_WIDTH])
    k = _dot(hb, w_ref[:, POOL_WIDTH + ATTN_WIDTH:IN_WIDTH - KV_WIDTH])
    v = _dot(hb, w_ref[:, IN_WIDTH - KV_WIDTH:IN_WIDTH])
    if rope:
        cos = cos_ref[...]
        sin = sin_ref[...]
        for hd in range(N_HEADS):
            sl = slice(hd * HEAD_DIM, (hd + 1) * HEAD_DIM)
            q_ref[:, sl] = (_rope(q[:, sl], cos, sin) * Q_SCALE).astype(BF16)
        for hd in range(N_KV_HEADS):
            sl = slice(hd * HEAD_DIM, (hd + 1) * HEAD_DIM)
            k_ref[:, sl] = _rope(k[:, sl], cos, sin).astype(k_ref.dtype)
    else:
        q_ref[...] = (q * Q_SCALE).astype(BF16)
        k_ref[...] = k.astype(k_ref.dtype)
    v_ref[...] = v.astype(v_ref.dtype)


def _in_projection(x, rows, mod3, g_mix, w_in_b, rope_tabs, kv_dtype):
    t = x.shape[0]
    tm = IN_TILE
    per = IN_TILE // TOKEN_TILE
    rope = rope_tabs is not None
    in_specs = [
        pl.BlockSpec((tm, D_MODEL), lambda i, r: (i, 0)),
        pl.BlockSpec((1, 6, D_MODEL), lambda i, r: (r[i * per], 0, 0)),
        pl.BlockSpec((1, D_MODEL), lambda i, r: (0, 0)),
        pl.BlockSpec((D_MODEL, IN_WIDTH), lambda i, r: (0, 0), pipeline_mode=pl.Buffered(1)),
    ]
    args = [x, mod3, g_mix, w_in_b]
    if rope:
        seq_tiles = rope_tabs[0].shape[0] // tm
        for tab in rope_tabs:
            in_specs.append(pl.BlockSpec((tm, HEAD_DIM), lambda i, r: (i % seq_tiles, 0)))
            args.append(tab)
    out_shape = (
        jax.ShapeDtypeStruct((t, POOL_WIDTH), BF16),
        jax.ShapeDtypeStruct((t, ATTN_WIDTH), BF16),
        jax.ShapeDtypeStruct((t, KV_WIDTH), kv_dtype),
        jax.ShapeDtypeStruct((t, KV_WIDTH), kv_dtype),
    )
    out_specs = (
        pl.BlockSpec((tm, POOL_WIDTH), lambda i, r: (i, 0)),
        pl.BlockSpec((tm, ATTN_WIDTH), lambda i, r: (i, 0)),
        pl.BlockSpec((tm, KV_WIDTH), lambda i, r: (i, 0)),
        pl.BlockSpec((tm, KV_WIDTH), lambda i, r: (i, 0)),
    )
    return pl.pallas_call(
        functools.partial(_inproj_kernel, rope=rope),
        grid_spec=pltpu.PrefetchScalarGridSpec(
            num_scalar_prefetch=1, grid=(t // tm,), in_specs=in_specs, out_specs=out_specs),
        out_shape=out_shape,
        compiler_params=_params(1),
        name="in_projection_rope" if rope else "in_projection",
    )(rows, *args)


def _pool_kernel(u_ref, pw_ref, ps_ref, g_ref, o_ref, *, seq_len):
    tq = TOKEN_TILE
    win = min(seq_len, tq + 2 * POOL_HALO)
    for sub in range(o_ref.shape[1] // tq):
        t0 = pl.program_id(1) * o_ref.shape[1] + sub * tq
        src0 = pl.multiple_of(jnp.clip(t0 - POOL_HALO, 0, seq_len - win), POOL_HALO)
        u = u_ref[0, pl.ds(src0, win), :]
        t = t0 + lax.broadcasted_iota(jnp.int32, (tq, win), 0)
        j = src0 + lax.broadcasted_iota(jnp.int32, (tq, win), 1)
        tc = t0 + lax.broadcasted_iota(jnp.int32, (tq, 1), 0)
        ys = []
        for gi, w in enumerate(POOL_WINDOWS):
            lo = jnp.maximum(t - w // 2, 0)
            hi = jnp.minimum(t + (w - w // 2), seq_len)
            cnt = (hi - lo).astype(F32)
            a = jnp.where((j >= lo) & (j < hi), 1.0, 0.0) - jnp.where(j == t, cnt, 0.0)
            cnt_col = (jnp.minimum(tc + (w - w // 2), seq_len) - jnp.maximum(tc - w // 2, 0)).astype(F32)
            sl = slice(gi * POOL_GROUP_DIM, (gi + 1) * POOL_GROUP_DIM)
            d = _dot(a.astype(BF16), u[:, sl]) / cnt_col
            ys.append(_dot(d.astype(BF16), pw_ref[gi]))
        y = jnp.concatenate(ys, axis=-1) * ps_ref[...]
        o_ref[0, sub * tq:(sub + 1) * tq, :] = (_rms(y) * g_ref[...]).astype(BF16)


def _pool_mixer(up, pool_w_b, pool_scale, g_out_pool):
    b, seq_len, _ = up.shape
    tq = TOKEN_TILE * min(POOL_TILES_PER_STEP, seq_len // TOKEN_TILE)
    return pl.pallas_call(
        functools.partial(_pool_kernel, seq_len=seq_len),
        grid=(b, seq_len // tq),
        in_specs=[
            pl.BlockSpec((1, seq_len, POOL_WIDTH), lambda bi, i: (bi, 0, 0)),
            pl.BlockSpec((len(POOL_WINDOWS), POOL_GROUP_DIM, POOL_GROUP_DIM), lambda bi, i: (0, 0, 0)),
            pl.BlockSpec((1, POOL_WIDTH), lambda bi, i: (0, 0)),
            pl.BlockSpec((1, POOL_WIDTH), lambda bi, i: (0, 0)),
        ],
        out_specs=pl.BlockSpec((1, tq, POOL_WIDTH), lambda bi, i: (bi, i, 0)),
        out_shape=jax.ShapeDtypeStruct((b, seq_len, POOL_WIDTH), BF16),
        compiler_params=_params(2),
        name=f"pool_mixer_{seq_len}",
    )(up, pool_w_b, pool_scale, g_out_pool)


def _stack_heads(q, kh, rows):
    return jnp.concatenate(
        [q[:, (kh * Q_PER_KV + g) * HEAD_DIM:(kh * Q_PER_KV + g + 1) * HEAD_DIM]
         for g in range(Q_PER_KV)], axis=0)


def _sink_column(sink_ref, kh, rows):
    return jnp.concatenate(
        [jnp.broadcast_to(sink_ref[kh * Q_PER_KV + g:kh * Q_PER_KV + g + 1, 0:1] * LOG2E, (rows, 1))
         for g in range(Q_PER_KV)], axis=0)


def _attend(q4, key_sets, value_sets, sk, band=None):
    s = [_dot_nt(q4, k) for k in key_sets]
    if band is not None:
        s[0] = jnp.where(band, s[0], NEG_INF)
    m = sk
    for si in s:
        m = jnp.maximum(m, jnp.max(si, axis=-1, keepdims=True))
    acc = None
    for si, v in zip(s, value_sets):
        v1 = jnp.concatenate([v, jnp.ones_like(v)], axis=-1)
        part = _dot(jnp.exp2(si - m).astype(BF16), v1)
        acc = part if acc is None else acc + part
    den = acc[:, HEAD_DIM:] + jnp.exp2(sk - m)
    return acc[:, :HEAD_DIM] / den


def _ctx_attn_kernel(q_ref, k_ref, v_ref, sink_ref, g_ref, o_ref, *, seq_len):
    rows = seq_len
    for j in range(q_ref.shape[0] // rows):
        rs = slice(j * rows, (j + 1) * rows)
        q = q_ref[rs, :]
        heads = [None] * N_HEADS
        for kh in range(N_KV_HEADS):
            sl = slice(kh * HEAD_DIM, (kh + 1) * HEAD_DIM)
            o = _attend(_stack_heads(q, kh, rows), [k_ref[rs, sl].astype(BF16)],
                        [v_ref[rs, sl].astype(BF16)], _sink_column(sink_ref, kh, rows))
            for g in range(Q_PER_KV):
                heads[kh * Q_PER_KV + g] = o[g * rows:(g + 1) * rows]
        y = jnp.concatenate(heads, axis=-1)
        o_ref[rs, :] = (_rms(y) * g_ref[...]).astype(BF16)


def _context_attention(q, k, v, sink_b, g_out_attn, seq_len):
    t = q.shape[0]
    blk = seq_len * CTX_SEQS_PER_STEP
    return pl.pallas_call(
        functools.partial(_ctx_attn_kernel, seq_len=seq_len),
        grid=(t // blk,),
        in_specs=[
            pl.BlockSpec((blk, ATTN_WIDTH), lambda b: (b, 0)),
            pl.BlockSpec((blk, KV_WIDTH), lambda b: (b, 0)),
            pl.BlockSpec((blk, KV_WIDTH), lambda b: (b, 0)),
            pl.BlockSpec((N_HEADS, LANES), lambda b: (0, 0)),
            pl.BlockSpec((1, ATTN_WIDTH), lambda b: (0, 0)),
        ],
        out_specs=pl.BlockSpec((blk, ATTN_WIDTH), lambda b: (b, 0)),
        out_shape=jax.ShapeDtypeStruct((t, ATTN_WIDTH), BF16),
        compiler_params=_params(1),
        name="context_attention",
    )(q, k, v, sink_b, g_out_attn)


def _lat_attn_kernel(q_ref, k_ref, v_ref, ck_ref, cv_ref, sink_ref, g_ref, o_ref, *, seq_len):
    rows = Q_BLOCK
    span = 3 * Q_BLOCK
    shape = (Q_PER_KV * rows, span)
    row_in_block = lax.broadcasted_iota(jnp.int32, shape, 0) & (rows - 1)
    col = lax.broadcasted_iota(jnp.int32, shape, 1)
    for j in range(Q_BLOCKS_PER_STEP):
        q0 = (pl.program_id(1) * Q_BLOCKS_PER_STEP + j) * rows
        start = pl.multiple_of(jnp.clip(q0 - Q_BLOCK, 0, seq_len - span), Q_BLOCK)
        q = q_ref[j * rows:(j + 1) * rows, :]
        kl = k_ref[0, pl.ds(start, span), :]
        vl = v_ref[0, pl.ds(start, span), :]
        band = jnp.abs(row_in_block - col + (q0 - start)) <= WINDOW
        heads = [None] * N_HEADS
        for kh in range(N_KV_HEADS):
            sl = slice(kh * HEAD_DIM, (kh + 1) * HEAD_DIM)
            o = _attend(_stack_heads(q, kh, rows),
                        [kl[:, sl], ck_ref[0, :, sl].astype(BF16)],
                        [vl[:, sl], cv_ref[0, :, sl].astype(BF16)],
                        _sink_column(sink_ref, kh, rows), band)
            for g in range(Q_PER_KV):
                heads[kh * Q_PER_KV + g] = o[g * rows:(g + 1) * rows]
        y = jnp.concatenate(heads, axis=-1)
        o_ref[j * rows:(j + 1) * rows, :] = (_rms(y) * g_ref[...]).astype(BF16)


def _latent_attention(q, k, v, cache_k, cache_v, sink_b, g_out_attn):
    b, seq_len, _ = k.shape
    q_rows = Q_BLOCK * Q_BLOCKS_PER_STEP
    nq = seq_len // q_rows
    past = cache_k.shape[1]
    return pl.pallas_call(
        functools.partial(_lat_attn_kernel, seq_len=seq_len),
        grid=(b, nq),
        in_specs=[
            pl.BlockSpec((q_rows, ATTN_WIDTH), lambda bi, n: (bi * nq + n, 0)),
            pl.BlockSpec((1, seq_len, KV_WIDTH), lambda bi, n: (bi, 0, 0)),
            pl.BlockSpec((1, seq_len, KV_WIDTH), lambda bi, n: (bi, 0, 0)),
            pl.BlockSpec((1, past, KV_WIDTH), lambda bi, n: (bi, 0, 0)),
            pl.BlockSpec((1, past, KV_WIDTH), lambda bi, n: (bi, 0, 0)),
            pl.BlockSpec((N_HEADS, LANES), lambda bi, n: (0, 0)),
            pl.BlockSpec((1, ATTN_WIDTH), lambda bi, n: (0, 0)),
        ],
        out_specs=pl.BlockSpec((q_rows, ATTN_WIDTH), lambda bi, n: (bi * nq + n, 0)),
        out_shape=jax.ShapeDtypeStruct((b * seq_len, ATTN_WIDTH), BF16),
        compiler_params=_params(2),
        name="latent_attention",
    )(q, k, v, cache_k, cache_v, sink_b, g_out_attn)


def _route(logits):
    lane = lax.broadcasted_iota(jnp.int32, logits.shape, 1).astype(F32)
    neg = -jnp.inf

    def first_argmax(x):
        mx = jnp.max(x, axis=-1, keepdims=True)
        return mx, jnp.min(jnp.where(x == mx, lane, float(ROUTER_LANES)), axis=-1, keepdims=True)

    gl = jnp.where(lane < N_EXPERT_GROUPS, logits, neg)
    gmax, g_idx = first_argmax(gl)
    p_g = 1.0 / jnp.sum(jnp.exp(gl - gmax), axis=-1, keepdims=True)
    base = N_EXPERT_GROUPS + EXPERTS_PER_GROUP * g_idx
    el = jnp.where((lane >= base) & (lane < base + EXPERTS_PER_GROUP), logits, neg)
    v1, i1 = first_argmax(el)
    v2, i2 = first_argmax(jnp.where(lane == i1, neg, el))
    e2 = jnp.exp(v2 - v1)
    w1 = p_g / (1.0 + e2)
    w2 = p_g * e2 / (1.0 + e2)
    ids = jnp.where(lane == 0.0, i1 - N_EXPERT_GROUPS, jnp.where(lane == 1.0, i2 - N_EXPERT_GROUPS, 0.0))
    wts = jnp.where(lane == 0.0, w1, jnp.where(lane == 1.0, w2, 0.0))
    return ids.astype(jnp.int32), wts


def _outproj_kernel(rows_ref, pool_p, attn_p, x_p, pool_s, attn_s, x_s, mod_ref, g_ref, w_ref, rw_ref,
                    rb_ref, x1_ref, h2_ref, ids_ref, wts_ref, mix0, mix1, *, n_ctx_tiles, n_tiles):
    del rows_ref
    i = pl.program_id(0)

    def project(pool_ref, attn_ref, mix_ref):
        mix_ref[...] = (_dot(pool_ref[...], w_ref[0:POOL_WIDTH, :])
                        + _dot(attn_ref[...], w_ref[POOL_WIDTH:POOL_WIDTH + ATTN_WIDTH, :]))

    def tail(x_ref, mix_ref):
        m = mod_ref[0]
        x1 = x_ref[...] + m[2:3] * mix_ref[...]
        x1_ref[...] = x1
        h2 = _rms(x1) * g_ref[...] * (1.0 + m[4:5]) + m[3:4]
        _store_row_tiles(h2_ref, _pack_bf16_pairs(h2))
        hi = h2.astype(BF16)
        lo = (h2 - hi.astype(F32)).astype(BF16)
        both = _dot(hi, rw_ref[...])
        logits = (both[:, :ROUTER_LANES] + both[:, ROUTER_LANES:] + _dot(lo, rw_ref[:, :ROUTER_LANES])
                  + rb_ref[...])
        ids, wts = _route(logits)
        ids_ref[...] = ids
        wts_ref[...] = wts

    ctx_in = (pool_p, attn_p)
    lat_in = (pool_s, attn_s)
    for parity, (cur, prev) in enumerate(((mix0, mix1), (mix1, mix0))):
        mine = (i % 2) == parity

        @pl.when(mine & (i == 0))
        def _(cur=cur):
            project(*ctx_in, cur)

        @pl.when(mine & (i >= 1) & (i < n_ctx_tiles))
        def _(cur=cur, prev=prev):
            tail(x_p, prev)
            project(*ctx_in, cur)

        @pl.when(mine & (i == n_ctx_tiles))
        def _(cur=cur, prev=prev):
            tail(x_p, prev)
            project(*lat_in, cur)

        @pl.when(mine & (i > n_ctx_tiles) & (i < n_tiles))
        def _(cur=cur, prev=prev):
            tail(x_s, prev)
            project(*lat_in, cur)

        @pl.when(mine & (i == n_tiles))
        def _(prev=prev):
            tail(x_s, prev)


def _out_projection(ctx, lat, rows, mod3, g_ffn, w_out_b, rw_split, rb):
    tm = TOKEN_TILE
    n_ctx = ctx[2].shape[0] // tm
    n_lat = lat[2].shape[0] // tm
    n_tiles = n_ctx + n_lat
    t_all = n_tiles * tm
    assert n_ctx >= 1 and n_lat >= 1

    def ctx_map(i, r):
        return (jnp.minimum(i, n_ctx - 1), 0)

    def lat_map(i, r):
        return (jnp.clip(i - n_ctx, 0, n_lat - 1), 0)

    def ctx_prev_map(i, r):
        return (jnp.clip(i - 1, 0, n_ctx - 1), 0)

    def lat_prev_map(i, r):
        return (jnp.clip(i - 1 - n_ctx, 0, n_lat - 1), 0)

    def prev_tile(i):
        return jnp.maximum(i - 1, 0)

    in_specs = [
        pl.BlockSpec((tm, POOL_WIDTH), ctx_map),
        pl.BlockSpec((tm, ATTN_WIDTH), ctx_map),
        pl.BlockSpec((tm, D_MODEL), ctx_prev_map),
        pl.BlockSpec((tm, POOL_WIDTH), lat_map),
        pl.BlockSpec((tm, ATTN_WIDTH), lat_map),
        pl.BlockSpec((tm, D_MODEL), lat_prev_map),
        pl.BlockSpec((1, 6, D_MODEL), lambda i, r: (r[prev_tile(i)], 0, 0)),
        pl.BlockSpec((1, D_MODEL), lambda i, r: (0, 0)),
        pl.BlockSpec((D_MODEL, D_MODEL), lambda i, r: (0, 0), pipeline_mode=pl.Buffered(1)),
        pl.BlockSpec((D_MODEL, 2 * ROUTER_LANES), lambda i, r: (0, 0)),
        pl.BlockSpec((1, ROUTER_LANES), lambda i, r: (0, 0)),
    ]
    out_shape = (
        jax.ShapeDtypeStruct((t_all, D_MODEL), F32),
        jax.ShapeDtypeStruct((t_all * ROW_TILE, LANES), jnp.uint32),
        jax.ShapeDtypeStruct((t_all, ROUTER_LANES), jnp.int32),
        jax.ShapeDtypeStruct((t_all, ROUTER_LANES), F32),
    )
    out_specs = (
        pl.BlockSpec((tm, D_MODEL), lambda i, r: (prev_tile(i), 0)),
        pl.BlockSpec((tm * ROW_TILE, LANES), lambda i, r: (prev_tile(i), 0)),
        pl.BlockSpec((tm, ROUTER_LANES), lambda i, r: (prev_tile(i), 0)),
        pl.BlockSpec((tm, ROUTER_LANES), lambda i, r: (prev_tile(i), 0)),
    )
    return pl.pallas_call(
        functools.partial(_outproj_kernel, n_ctx_tiles=n_ctx, n_tiles=n_tiles),
        grid_spec=pltpu.PrefetchScalarGridSpec(
            num_scalar_prefetch=1, grid=(n_tiles + 1,), in_specs=in_specs, out_specs=out_specs,
            scratch_shapes=[pltpu.VMEM((tm, D_MODEL), F32), pltpu.VMEM((tm, D_MODEL), F32)]),
        out_shape=out_shape,
        compiler_params=_params(1),
        name="out_projection",
    )(rows, *ctx, *lat, mod3, g_ffn, w_out_b, rw_split, rb)


def _gather_rows(idx_ref, base, n_rows, src_hbm, dst, sem, *, unrolled, both_queues=False, dst_row0=0):
    def one(r, priority):
        src_row = pl.multiple_of(idx_ref[base + r], ROW_TILE)
        pltpu.make_async_copy(
            src_hbm.at[pl.ds(src_row, ROW_TILE)], dst.at[pl.ds((dst_row0 + r) * ROW_TILE, ROW_TILE)],
            sem).start(priority=priority)

    if unrolled:
        for r in range(n_rows):
            one(r, r % 2 if both_queues else 0)
    else:
        def body(r, carry):
            one(r, 0)
            return carry
        lax.fori_loop(0, n_rows, body, 0, unroll=8)


def _wait_rows(src_hbm, dst, sem):
    pltpu.make_async_copy(src_hbm.at[pl.ds(0, dst.shape[0])], dst, sem).wait()


def _late_zero(x):
    u = lax.bitcast_convert_type(x, jnp.uint32)
    return ((u >> 16) >> 16).astype(jnp.int32)[0, 0]


def _moe_kernel(be_ref, ne_ref, nu_ref, first_ref, tok_ref, h_hbm, wg_hbm, wu_hbm, wd_hbm, y_ref,
                xg0, xg1, xg2, sg, su, sd, wgb, wub, wdb, sem, wsem):
    b = pl.program_id(0)
    n_used = nu_ref[0]
    last = n_used - 1
    active = b < n_used
    bufs = (xg0, xg1, xg2)
    n_buf = len(bufs)

    def weight_copies(e):
        return (pltpu.make_async_copy(wg_hbm.at[e], sg, wsem.at[0]),
                pltpu.make_async_copy(wu_hbm.at[e], su, wsem.at[1]),
                pltpu.make_async_copy(wd_hbm.at[e], sd, wsem.at[2]))

    @pl.when(b == 0)
    def _():
        for cp in weight_copies(be_ref[0]):
            cp.start(priority=1)
        for i in range(n_buf - 1):
            _gather_rows(tok_ref, first_ref[jnp.minimum(i, last)], EXPERT_ROWS, h_hbm, bufs[i], sem.at[i],
                         unrolled=False)

    new_expert = (b == 0) | (be_ref[b] != be_ref[jnp.maximum(b - 1, 0)])

    @pl.when(active & new_expert)
    def _():
        for cp in weight_copies(be_ref[b]):
            cp.wait()
        wgb[...] = sg[...].astype(BF16)
        wub[...] = su[...].astype(BF16)
        wdb[...] = sd[...].astype(BF16)

        @pl.when(ne_ref[b] >= 0)
        def _():
            for cp in weight_copies(ne_ref[b]):
                cp.start(priority=1)

    def step(slot):
        cur = bufs[slot]
        ahead = bufs[(slot + n_buf - 1) % n_buf]
        sem_ahead = sem.at[(slot + n_buf - 1) % n_buf]
        _wait_rows(h_hbm, cur, sem.at[slot])
        base = first_ref[jnp.minimum(b + n_buf - 1, last)]
        per = EXPERT_ROWS // GATHER_GROUPS
        group = iter(range(GATHER_GROUPS))

        def issue(zero):
            g = next(group)
            _gather_rows(tok_ref, base + g * per + zero, per, h_hbm, ahead, sem_ahead, unrolled=True,
                         both_queues=True, dst_row0=g * per)

        def corner(x, r, c):
            return _late_zero(x[r:r + 1, c:c + 1])

        bm, de = EXPERT_ROWS, D_EXPERT
        lo, hi = _unpack_bf16_pairs(_load_row_tiles(cur, bm))
        lo = lo.astype(BF16)
        hi = hi.astype(BF16)

        def issue_along(x, n):
            rows, cols = x.shape
            for i in range(n):
                issue(corner(x, (i + 1) * rows // n - 1, (i + 1) * cols // n - 1))

        n_each = GATHER_GROUPS // 8
        issue(0)
        a1 = _dot(lo, wgb[0:PACKED, :])
        issue_along(a1, n_each)
        a = a1 + _dot(hi, wgb[PACKED:D_MODEL, :])
        issue_along(a, n_each)
        u1 = _dot(lo, wub[0:PACKED, :])
        issue_along(u1, n_each)
        u = u1 + _dot(hi, wub[PACKED:D_MODEL, :])
        issue_along(u, n_each)
        act = (a / (1.0 + jnp.exp(-a)) * u).astype(BF16)
        y = _dot(act, wdb[...])
        issue_along(y, GATHER_GROUPS - 1 - 4 * n_each)
        _store_row_tiles(y_ref, _pack_bf16_pairs(y))

        @pl.when(b == last)
        def _():
            for i in range(1, n_buf):
                _wait_rows(h_hbm, bufs[(slot + i) % n_buf], sem.at[(slot + i) % n_buf])

    for slot in range(n_buf):
        @pl.when(active & (b % n_buf == slot))
        def _(slot=slot):
            step(slot)

    @pl.when(b >= n_used)
    def _():
        y_ref[...] = jnp.zeros_like(y_ref)


def _experts(tables, cap, h2_packed, w_gate, w_up, w_down):
    bm = EXPERT_ROWS
    xg = pltpu.VMEM((bm * ROW_TILE, LANES), jnp.uint32)
    return pl.pallas_call(
        _moe_kernel,
        grid_spec=pltpu.PrefetchScalarGridSpec(
            num_scalar_prefetch=len(tables),
            grid=(cap // bm,),
            in_specs=[pl.BlockSpec(memory_space=pl.ANY)] * 4,
            out_specs=pl.BlockSpec((bm * ROW_TILE, LANES), lambda b, *_: (b, 0)),
            scratch_shapes=[
                xg, xg, xg,
                pltpu.VMEM((D_MODEL, D_EXPERT), F32), pltpu.VMEM((D_MODEL, D_EXPERT), F32),
                pltpu.VMEM((D_EXPERT, D_MODEL), F32),
                pltpu.VMEM((D_MODEL, D_EXPERT), BF16), pltpu.VMEM((D_MODEL, D_EXPERT), BF16),
                pltpu.VMEM((D_EXPERT, D_MODEL), BF16),
                pltpu.SemaphoreType.DMA((3,)), pltpu.SemaphoreType.DMA((3,)),
            ],
        ),
        out_shape=jax.ShapeDtypeStruct((cap * ROW_TILE, LANES), jnp.uint32),
        compiler_params=_params(1),
        name="experts",
    )(*tables, h2_packed, w_gate, w_up, w_down)


def _combine_kernel(slot_ref, rows_ref, y_hbm, x1_ref, wts_ref, mod_ref, g_ref, op_ref, os_ref, yg0, yg1, sem,
                    *, n_ctx_tiles, n_tiles):
    del rows_ref
    tm = TOKEN_TILE
    i = pl.program_id(0)
    n_slots = slot_ref.shape[0] // TOP_K

    def start(tile, buf, s, unrolled):
        for k in range(TOP_K):
            _gather_rows(slot_ref, k * n_slots + tile * tm, tm, y_hbm, buf.at[k], s, unrolled=unrolled,
                         both_queues=True)

    @pl.when(i == 0)
    def _():
        start(0, yg0, sem.at[0], False)

    def step(cur, nxt, sem_cur, sem_nxt):
        for k in range(TOP_K):
            _wait_rows(y_hbm, cur.at[k], sem_cur)
        start(jnp.minimum(i + 1, n_tiles - 1), nxt, sem_nxt, True)
        w = wts_ref[...]
        lo0, hi0 = _unpack_bf16_pairs(_load_row_tiles(cur.at[0], tm))
        lo1, hi1 = _unpack_bf16_pairs(_load_row_tiles(cur.at[1], tm))
        ffn = jnp.concatenate([lo0 * w[:, 0:1] + lo1 * w[:, 1:2], hi0 * w[:, 0:1] + hi1 * w[:, 1:2]], axis=-1)
        x2 = x1_ref[...] + mod_ref[0][5:6] * ffn
        out = _rms(x2) * g_ref[...]

        @pl.when(i < n_ctx_tiles)
        def _():
            op_ref[...] = out

        @pl.when(i >= n_ctx_tiles)
        def _():
            os_ref[...] = out

        @pl.when(i == n_tiles - 1)
        def _():
            for k in range(TOP_K):
                _wait_rows(y_hbm, nxt.at[k], sem_nxt)

    @pl.when(i % 2 == 0)
    def _():
        step(yg0, yg1, sem.at[0], sem.at[1])

    @pl.when(i % 2 == 1)
    def _():
        step(yg1, yg0, sem.at[1], sem.at[0])


def _combine(slots, rows, y_buf, x1, wts, mod3, g_final, t_ctx):
    t = x1.shape[0]
    tm = TOKEN_TILE
    n_tiles = t // tm
    n_ctx = t_ctx // tm
    return pl.pallas_call(
        functools.partial(_combine_kernel, n_ctx_tiles=n_ctx, n_tiles=n_tiles),
        grid_spec=pltpu.PrefetchScalarGridSpec(
            num_scalar_prefetch=2,
            grid=(n_tiles,),
            in_specs=[
                pl.BlockSpec(memory_space=pl.ANY),
                pl.BlockSpec((tm, D_MODEL), lambda i, s, r: (i, 0)),
                pl.BlockSpec((tm, ROUTER_LANES), lambda i, s, r: (i, 0)),
                pl.BlockSpec((1, 6, D_MODEL), lambda i, s, r: (r[i], 0, 0)),
                pl.BlockSpec((1, D_MODEL), lambda i, s, r: (0, 0)),
            ],
            out_specs=(
                pl.BlockSpec((tm, D_MODEL), lambda i, s, r: (jnp.minimum(i, n_ctx - 1), 0)),
                pl.BlockSpec((tm, D_MODEL), lambda i, s, r: (jnp.maximum(i - n_ctx, 0), 0)),
            ),
            scratch_shapes=[pltpu.VMEM((TOP_K, tm * ROW_TILE, LANES), jnp.uint32),
                            pltpu.VMEM((TOP_K, tm * ROW_TILE, LANES), jnp.uint32),
                            pltpu.SemaphoreType.DMA((2,))],
        ),
        out_shape=(jax.ShapeDtypeStruct((t_ctx, D_MODEL), F32),
                   jax.ShapeDtypeStruct((t - t_ctx, D_MODEL), F32)),
        compiler_params=_params(1),
        name="combine",
    )(slots, rows, y_buf, x1, wts, mod3, g_final)


def _rope_tables(n_tokens):
    rows = n_tokens // GRID_W
    row = jnp.repeat(jnp.arange(rows, dtype=F32), GRID_W)
    col = jnp.tile(jnp.arange(GRID_W, dtype=F32), rows)
    half = ROPE_AXIS_DIM // 2
    inv_freq = ROPE_BASE ** (-jnp.arange(half, dtype=F32) / half)
    ar = row[:, None] * inv_freq
    ac = col[:, None] * inv_freq
    cos = jnp.concatenate([jnp.cos(ar), jnp.cos(ar), jnp.cos(ac), jnp.cos(ac)], axis=-1)
    sin = jnp.concatenate([-jnp.sin(ar), jnp.sin(ar), -jnp.sin(ac), jnp.sin(ac)], axis=-1)
    return cos, sin


def _dispatch_tables(expert_ids):
    t = expert_ids[0].shape[0]
    bm = EXPERT_ROWS
    flat_e = jnp.concatenate(expert_ids)
    onehot = (flat_e[:, None] == jnp.arange(N_EXPERTS, dtype=jnp.int32)[None, :]).astype(jnp.int32)
    running = jnp.cumsum(onehot, axis=0)
    rank = jnp.sum(running * onehot, axis=1) - 1
    counts = running[-1]
    padded = (counts + bm - 1) // bm * bm
    pad_end = jnp.cumsum(padded)
    pad_start = pad_end - padded
    dest = (jnp.sum(pad_start[None, :] * onehot, axis=1) + rank).astype(jnp.int32)
    n_blocks = (t * TOP_K + N_EXPERTS * (bm - 1) + bm - 1) // bm
    order = jnp.argsort(flat_e).astype(jnp.int32)
    src_tile = jnp.where(order >= t, order - t, order) * ROW_TILE
    src_tile = jnp.concatenate([src_tile, jnp.zeros((bm,), jnp.int32)])
    n_used = (pad_end[-1] // bm).astype(jnp.int32)
    blk = jnp.arange(n_blocks, dtype=jnp.int32)
    blk = jnp.minimum(blk, n_used - 1)
    block_expert = jnp.sum((pad_end[None, :] <= (blk * bm)[:, None]).astype(jnp.int32), axis=1)
    block_expert = jnp.minimum(block_expert, N_EXPERTS - 1)
    e_ids = jnp.arange(N_EXPERTS, dtype=jnp.int32)
    later = (e_ids[None, :] > e_ids[:, None]) & (counts[None, :] > 0)
    next_of = jnp.min(jnp.where(later, e_ids[None, :], N_EXPERTS), axis=1)
    next_of = jnp.where(next_of == N_EXPERTS, -1, next_of)
    of_block = (block_expert[:, None] == e_ids[None, :]).astype(jnp.int32)
    next_expert = jnp.sum(next_of[None, :] * of_block, axis=1)
    start = jnp.cumsum(counts) - counts
    first_src = jnp.sum((start - pad_start)[None, :] * of_block, axis=1) + blk * bm
    return (block_expert.astype(jnp.int32), next_expert.astype(jnp.int32), n_used.reshape(1),
            first_src.astype(jnp.int32), src_tile, dest * ROW_TILE, n_blocks * bm)


def kernel(x_prompt, x_sample, cache_k, cache_v, c, c_ctx, w_mod, b_mod, norm_mix_g, norm_ffn_g, w_in,
           pool_w, pool_scale, attn_sink, out_norm_pool_g, out_norm_attn_g, w_out, router_group_w,
           router_group_b, router_expert_w, router_expert_b, w_gate, w_up, w_down, final_norm_g):
    depth = w_mod.shape[0]
    assert depth == 1, "single trunk layer"
    bp, lp, _ = x_prompt.shape
    bs, ls, _ = x_sample.shape
    tp, ts = bp * lp, bs * ls
    tm = TOKEN_TILE
    l = 0

    cond8 = jnp.zeros((8, D_MODEL), F32).at[:bs].set(c).at[bs].set(c_ctx)
    mod3 = _modulation(cond8, w_mod[l], b_mod[l]).reshape(8, 6, D_MODEL)
    rows_p = jnp.full((tp // tm,), bs, jnp.int32)
    rows_s = jnp.arange(ts // tm, dtype=jnp.int32) // (ls // tm)

    w_in_b = w_in[l].astype(BF16)
    w_out_b = w_out[l].astype(BF16)
    pool_w_b = pool_w[l].astype(BF16)
    g_mix = norm_mix_g[l].reshape(1, D_MODEL)
    g_ffn = norm_ffn_g[l].reshape(1, D_MODEL)
    g_pool = out_norm_pool_g[l].reshape(1, POOL_WIDTH)
    g_attn = out_norm_attn_g[l].reshape(1, ATTN_WIDTH)
    p_scale = pool_scale[l].reshape(1, POOL_WIDTH)
    sink_b = jnp.broadcast_to(attn_sink[l][:, None], (N_HEADS, LANES))
    g_final = final_norm_g.reshape(1, D_MODEL)

    rw = jnp.concatenate(
        [router_group_w[l], jnp.transpose(router_expert_w[l], (1, 0, 2)).reshape(D_MODEL, N_EXPERTS)], axis=1)
    rw = jnp.pad(rw, ((0, 0), (0, ROUTER_LANES - rw.shape[1])))
    rw_hi = rw.astype(BF16)
    rw_split = jnp.concatenate([rw_hi, (rw - rw_hi.astype(F32)).astype(BF16)], axis=1)
    rb = jnp.concatenate([router_group_b[l], router_expert_b[l].reshape(N_EXPERTS)])
    rb = jnp.pad(rb, (0, ROUTER_LANES - rb.shape[0])).reshape(1, ROUTER_LANES)

    xp = x_prompt.reshape(tp, D_MODEL)
    xs = x_sample.reshape(ts, D_MODEL)

    up_p, q_p, k_p, v_p = _in_projection(xp, rows_p, mod3, g_mix, w_in_b, None, F32)
    pool_p = _pool_mixer(up_p.reshape(bp, lp, POOL_WIDTH), pool_w_b, p_scale, g_pool).reshape(tp, POOL_WIDTH)
    attn_p = _context_attention(q_p, k_p, v_p, sink_b, g_attn, lp)
    up_s, q_s, k_s, v_s = _in_projection(xs, rows_s, mod3, g_mix, w_in_b, _rope_tables(ls), BF16)
    pool_s = _pool_mixer(up_s.reshape(bs, ls, POOL_WIDTH), pool_w_b, p_scale, g_pool).reshape(ts, POOL_WIDTH)
    attn_s = _latent_attention(
        q_s, k_s.reshape(bs, ls, KV_WIDTH), v_s.reshape(bs, ls, KV_WIDTH),
        cache_k[:, l].reshape(bs, -1, KV_WIDTH), cache_v[:, l].reshape(bs, -1, KV_WIDTH), sink_b, g_attn)

    rows_all = jnp.concatenate([rows_p, rows_s])
    x1_all, h2_all, ids_all, wts_all = _out_projection(
        (pool_p, attn_p, xp), (pool_s, attn_s, xs), rows_all, mod3, g_ffn, w_out_b, rw_split, rb)

    *expert_tables, slots, cap = _dispatch_tables([ids_all[:, k] for k in range(TOP_K)])
    y_buf = _experts(expert_tables, cap, h2_all, w_gate[l], w_up[l], w_down[l])

    y_p, y_s = _combine(slots, rows_all, y_buf, x1_all, wts_all, mod3, g_final, tp)

    kv_shape = (bp, 1, lp, N_KV_HEADS, HEAD_DIM)
    return (y_p.reshape(bp, lp, D_MODEL), y_s.reshape(bs, ls, D_MODEL),
            k_p.reshape(kv_shape), v_p.reshape(kv_shape))
```

```python
import functools

import jax
import jax.numpy as jnp
from jax import lax
from jax.experimental import pallas as pl
from jax.experimental.pallas import tpu as pltpu

F32 = jnp.float32
BF16 = jnp.bfloat16

D_MODEL = 2048
GRID_W = 64
HEAD_DIM = 128
ATTN_WIDTH = 1024
POOL_WIDTH = 1024
N_HEADS = 8
N_KV_HEADS = 2
Q_PER_KV = 4
KV_WIDTH = 256
IN_WIDTH = 2560
WINDOW = 128
POOL_WINDOWS = (2, 4, 8, 16)
POOL_GROUP_DIM = 256
ROPE_BASE = 10000.0
ROPE_AXIS_DIM = 64
N_EXPERT_GROUPS = 4
EXPERTS_PER_GROUP = 8
N_EXPERTS = 32
TOP_K = 2
D_EXPERT = 512
NORM_EPS = 1e-6
NEG_INF = -1e30
ATTN_SCALE = HEAD_DIM ** -0.5
LOG2E = 1.4426950408889634
Q_SCALE = ATTN_SCALE * LOG2E

LANES = 128
Q_BLOCKS_PER_STEP = 4
CTX_SEQS_PER_STEP = 2
VMEM_LIMIT = 48 * 1024 * 1024
PACKED = D_MODEL // 2
ROW_TILE = 8
assert PACKED == ROW_TILE * LANES, "a packed row must fill exactly one tile"
MOD_TILE = 1024
TOKEN_TILE = 256
IN_TILE = 512
POOL_TILES_PER_STEP = 2
POOL_HALO = 16
Q_BLOCK = 128
EXPERT_ROWS = 256
ROUTER_LANES = 128


def _params(n_grid):
    return pltpu.CompilerParams(
        dimension_semantics=("arbitrary",) * n_grid, vmem_limit_bytes=VMEM_LIMIT)


def _rms(x):
    return x * lax.rsqrt(jnp.mean(x * x, axis=-1, keepdims=True) + NORM_EPS)


def _dot(a, b):
    return jnp.dot(a, b, preferred_element_type=F32)


def _dot_nt(a, b):
    return lax.dot_general(a, b, (((1,), (1,)), ((), ())), preferred_element_type=F32)


def _pack_bf16_pairs(x):
    c = x.shape[1] // 2
    lo = lax.bitcast_convert_type(x[:, :c].astype(BF16).astype(F32), jnp.uint32)
    hi = lax.bitcast_convert_type(x[:, c:].astype(BF16).astype(F32), jnp.uint32)
    return hi | (lo >> 16)


def _store_row_tiles(ref, x):
    n = x.shape[0]
    for c in range(PACKED // LANES):
        ref[pl.ds(c, n, stride=ROW_TILE), :] = x[:, c * LANES:(c + 1) * LANES]


def _load_row_tiles(ref, n):
    return jnp.concatenate(
        [ref[pl.ds(c, n, stride=ROW_TILE), :] for c in range(PACKED // LANES)], axis=-1)


def _unpack_bf16_pairs(u):
    lo = lax.bitcast_convert_type(u << 16, F32)
    hi = lax.bitcast_convert_type(u & jnp.uint32(0xFFFF0000), F32)
    return lo, hi


def _mod_kernel(cond_ref, w_ref, b_ref, o_ref):
    c = cond_ref[...]
    s = c / (1.0 + jnp.exp(-c))
    o_ref[...] = _dot(s.astype(BF16), w_ref[...].astype(BF16)) + b_ref[...]


def _modulation(cond8, w_mod, b_mod):
    n = w_mod.shape[1]
    tn = MOD_TILE
    return pl.pallas_call(
        _mod_kernel,
        grid=(n // tn,),
        in_specs=[
            pl.BlockSpec((8, D_MODEL), lambda j: (0, 0)),
            pl.BlockSpec((D_MODEL, tn), lambda j: (0, j)),
            pl.BlockSpec((1, tn), lambda j: (0, j)),
        ],
        out_specs=pl.BlockSpec((8, tn), lambda j: (0, j)),
        out_shape=jax.ShapeDtypeStruct((8, n), F32),
        compiler_params=_params(1),
        name="modulation",
    )(cond8, w_mod, b_mod.reshape(1, n))


def _rope(x, cos, sin_signed):
    lane = lax.broadcasted_iota(jnp.int32, x.shape, 1)
    partner = jnp.where((lane & 63) < 32,
                        pltpu.roll(x, HEAD_DIM - 32, axis=1),
                        pltpu.roll(x, 32, axis=1))
    return x * cos + partner * sin_signed


def _inproj_kernel(rows_ref, x_ref, mod_ref, g_ref, w_ref, *rest, rope):
    del rows_ref
    if rope:
        cos_ref, sin_ref, up_ref, q_ref, k_ref, v_ref = rest
    else:
        up_ref, q_ref, k_ref, v_ref = rest
    m = mod_ref[0]
    h = _rms(x_ref[...]) * g_ref[...]
    hb = (h * (1.0 + m[1:2]) + m[0:1]).astype(BF16)
    up_ref[...] = _dot(hb, w_ref[:, 0:POOL_WIDTH]).astype(BF16)
    q = _dot(hb, w_ref[:, POOL_WIDTH:POOL_WIDTH + ATTN_WIDTH])
    k = _dot(hb, w_ref[:, POOL_WIDTH + ATTN_WIDTH:IN_WIDTH - KV_WIDTH])
    v = _dot(hb, w_ref[:, IN_WIDTH - KV_WIDTH:IN_WIDTH])
    if rope:
        cos = cos_ref[...]
        sin = sin_ref[...]
        for hd in range(N_HEADS):
            sl = slice(hd * HEAD_DIM, (hd + 1) * HEAD_DIM)
            q_ref[:, sl] = (_rope(q[:, sl], cos, sin) * Q_SCALE).astype(BF16)
        for hd in range(N_KV_HEADS):
            sl = slice(hd * HEAD_DIM, (hd + 1) * HEAD_DIM)
            k_ref[:, sl] = _rope(k[:, sl], cos, sin).astype(k_ref.dtype)
    else:
        q_ref[...] = (q * Q_SCALE).astype(BF16)
        k_ref[...] = k.astype(k_ref.dtype)
    v_ref[...] = v.astype(v_ref.dtype)


def _in_projection(x, rows, mod3, g_mix, w_in_b, rope_tabs, kv_dtype):
    t = x.shape[0]
    tm = IN_TILE
    per = IN_TILE // TOKEN_TILE
    rope = rope_tabs is not None
    in_specs = [
        pl.BlockSpec((tm, D_MODEL), lambda i, r: (i, 0)),
        pl.BlockSpec((1, 6, D_MODEL), lambda i, r: (r[i * per], 0, 0)),
        pl.BlockSpec((1, D_MODEL), lambda i, r: (0, 0)),
        pl.BlockSpec((D_MODEL, IN_WIDTH), lambda i, r: (0, 0), pipeline_mode=pl.Buffered(1)),
    ]
    args = [x, mod3, g_mix, w_in_b]
    if rope:
        seq_tiles = rope_tabs[0].shape[0] // tm
        for tab in rope_tabs:
            in_specs.append(pl.BlockSpec((tm, HEAD_DIM), lambda i, r: (i % seq_tiles, 0)))
            args.append(tab)
    out_shape = (
        jax.ShapeDtypeStruct((t, POOL_WIDTH), BF16),
        jax.ShapeDtypeStruct((t, ATTN_WIDTH), BF16),
        jax.ShapeDtypeStruct((t, KV_WIDTH), kv_dtype),
        jax.ShapeDtypeStruct((t, KV_WIDTH), kv_dtype),
    )
    out_specs = (
        pl.BlockSpec((tm, POOL_WIDTH), lambda i, r: (i, 0)),
        pl.BlockSpec((tm, ATTN_WIDTH), lambda i, r: (i, 0)),
        pl.BlockSpec((tm, KV_WIDTH), lambda i, r: (i, 0)),
        pl.BlockSpec((tm, KV_WIDTH), lambda i, r: (i, 0)),
    )
    return pl.pallas_call(
        functools.partial(_inproj_kernel, rope=rope),
        grid_spec=pltpu.PrefetchScalarGridSpec(
            num_scalar_prefetch=1, grid=(t // tm,), in_specs=in_specs, out_specs=out_specs),
        out_shape=out_shape,
        compiler_params=_params(1),
        name="in_projection_rope" if rope else "in_projection",
    )(rows, *args)


def _pool_kernel(u_ref, pw_ref, ps_ref, g_ref, o_ref, *, seq_len):
    tq = TOKEN_TILE
    win = min(seq_len, tq + 2 * POOL_HALO)
    for sub in range(o_ref.shape[1] // tq):
        t0 = pl.program_id(1) * o_ref.shape[1] + sub * tq
        src0 = pl.multiple_of(jnp.clip(t0 - POOL_HALO, 0, seq_len - win), POOL_HALO)
        u = u_ref[0, pl.ds(src0, win), :]
        t = t0 + lax.broadcasted_iota(jnp.int32, (tq, win), 0)
        j = src0 + lax.broadcasted_iota(jnp.int32, (tq, win), 1)
        tc = t0 + lax.broadcasted_iota(jnp.int32, (tq, 1), 0)
        ys = []
        for gi, w in enumerate(POOL_WINDOWS):
            lo = jnp.maximum(t - w // 2, 0)
            hi = jnp.minimum(t + (w - w // 2), seq_len)
            cnt = (hi - lo).astype(F32)
            a = jnp.where((j >= lo) & (j < hi), 1.0, 0.0) - jnp.where(j == t, cnt, 0.0)
            cnt_col = (jnp.minimum(tc + (w - w // 2), seq_len) - jnp.maximum(tc - w // 2, 0)).astype(F32)
            sl = slice(gi * POOL_GROUP_DIM, (gi + 1) * POOL_GROUP_DIM)
            d = _dot(a.astype(BF16), u[:, sl]) / cnt_col
            ys.append(_dot(d.astype(BF16), pw_ref[gi]))
        y = jnp.concatenate(ys, axis=-1) * ps_ref[...]
        o_ref[0, sub * tq:(sub + 1) * tq, :] = (_rms(y) * g_ref[...]).astype(BF16)


def _pool_mixer(up, pool_w_b, pool_scale, g_out_pool):
    b, seq_len, _ = up.shape
    tq = TOKEN_TILE * min(POOL_TILES_PER_STEP, seq_len // TOKEN_TILE)
    return pl.pallas_call(
        functools.partial(_pool_kernel, seq_len=seq_len),
        grid=(b, seq_len // tq),
        in_specs=[
            pl.BlockSpec((1, seq_len, POOL_WIDTH), lambda bi, i: (bi, 0, 0)),
            pl.BlockSpec((len(POOL_WINDOWS), POOL_GROUP_DIM, POOL_GROUP_DIM), lambda bi, i: (0, 0, 0)),
            pl.BlockSpec((1, POOL_WIDTH), lambda bi, i: (0, 0)),
            pl.BlockSpec((1, POOL_WIDTH), lambda bi, i: (0, 0)),
        ],
        out_specs=pl.BlockSpec((1, tq, POOL_WIDTH), lambda bi, i: (bi, i, 0)),
        out_shape=jax.ShapeDtypeStruct((b, seq_len, POOL_WIDTH), BF16),
        compiler_params=_params(2),
        name=f"pool_mixer_{seq_len}",
    )(up, pool_w_b, pool_scale, g_out_pool)


def _stack_heads(q, kh, rows):
    return jnp.concatenate(
        [q[:, (kh * Q_PER_KV + g) * HEAD_DIM:(kh * Q_PER_KV + g + 1) * HEAD_DIM]
         for g in range(Q_PER_KV)], axis=0)


def _sink_column(sink_ref, kh, rows):
    return jnp.concatenate(
        [jnp.broadcast_to(sink_ref[kh * Q_PER_KV + g:kh * Q_PER_KV + g + 1, 0:1] * LOG2E, (rows, 1))
         for g in range(Q_PER_KV)], axis=0)


def _attend(q4, key_sets, value_sets, sk, band=None):
    s = [_dot_nt(q4, k) for k in key_sets]
    if band is not None:
        s[0] = jnp.where(band, s[0], NEG_INF)
    m = sk
    for si in s:
        m = jnp.maximum(m, jnp.max(si, axis=-1, keepdims=True))
    acc = None
    for si, v in zip(s, value_sets):
        v1 = jnp.concatenate([v, jnp.ones_like(v)], axis=-1)
        part = _dot(jnp.exp2(si - m).astype(BF16), v1)
        acc = part if acc is None else acc + part
    den = acc[:, HEAD_DIM:] + jnp.exp2(sk - m)
    return acc[:, :HEAD_DIM] / den


def _ctx_attn_kernel(q_ref, k_ref, v_ref, sink_ref, g_ref, o_ref, *, seq_len):
    rows = seq_len
    for j in range(q_ref.shape[0] // rows):
        rs = slice(j * rows, (j + 1) * rows)
        q = q_ref[rs, :]
        heads = [None] * N_HEADS
        for kh in range(N_KV_HEADS):
            sl = slice(kh * HEAD_DIM, (kh + 1) * HEAD_DIM)
            o = _attend(_stack_heads(q, kh, rows), [k_ref[rs, sl].astype(BF16)],
                        [v_ref[rs, sl].astype(BF16)], _sink_column(sink_ref, kh, rows))
            for g in range(Q_PER_KV):
                heads[kh * Q_PER_KV + g] = o[g * rows:(g + 1) * rows]
        y = jnp.concatenate(heads, axis=-1)
        o_ref[rs, :] = (_rms(y) * g_ref[...]).astype(BF16)


def _context_attention(q, k, v, sink_b, g_out_attn, seq_len):
    t = q.shape[0]
    blk = seq_len * CTX_SEQS_PER_STEP
    return pl.pallas_call(
        functools.partial(_ctx_attn_kernel, seq_len=seq_len),
        grid=(t // blk,),
        in_specs=[
            pl.BlockSpec((blk, ATTN_WIDTH), lambda b: (b, 0)),
            pl.BlockSpec((blk, KV_WIDTH), lambda b: (b, 0)),
            pl.BlockSpec((blk, KV_WIDTH), lambda b: (b, 0)),
            pl.BlockSpec((N_HEADS, LANES), lambda b: (0, 0)),
            pl.BlockSpec((1, ATTN_WIDTH), lambda b: (0, 0)),
        ],
        out_specs=pl.BlockSpec((blk, ATTN_WIDTH), lambda b: (b, 0)),
        out_shape=jax.ShapeDtypeStruct((t, ATTN_WIDTH), BF16),
        compiler_params=_params(1),
        name="context_attention",
    )(q, k, v, sink_b, g_out_attn)


def _lat_attn_kernel(q_ref, k_ref, v_ref, ck_ref, cv_ref, sink_ref, g_ref, o_ref, *, seq_len):
    rows = Q_BLOCK
    span = 3 * Q_BLOCK
    shape = (Q_PER_KV * rows, span)
    row_in_block = lax.broadcasted_iota(jnp.int32, shape, 0) & (rows - 1)
    col = lax.broadcasted_iota(jnp.int32, shape, 1)
    for j in range(Q_BLOCKS_PER_STEP):
        q0 = (pl.program_id(1) * Q_BLOCKS_PER_STEP + j) * rows
        start = pl.multiple_of(jnp.clip(q0 - Q_BLOCK, 0, seq_len - span), Q_BLOCK)
        q = q_ref[j * rows:(j + 1) * rows, :]
        kl = k_ref[0, pl.ds(start, span), :]
        vl = v_ref[0, pl.ds(start, span), :]
        band = jnp.abs(row_in_block - col + (q0 - start)) <= WINDOW
        heads = [None] * N_HEADS
        for kh in range(N_KV_HEADS):
            sl = slice(kh * HEAD_DIM, (kh + 1) * HEAD_DIM)
            o = _attend(_stack_heads(q, kh, rows),
                        [kl[:, sl], ck_ref[0, :, sl].astype(BF16)],
                        [vl[:, sl], cv_ref[0, :, sl].astype(BF16)],
                        _sink_column(sink_ref, kh, rows), band)
            for g in range(Q_PER_KV):
                heads[kh * Q_PER_KV + g] = o[g * rows:(g + 1) * rows]
        y = jnp.concatenate(heads, axis=-1)
        o_ref[j * rows:(j + 1) * rows, :] = (_rms(y) * g_ref[...]).astype(BF16)


def _latent_attention(q, k, v, cache_k, cache_v, sink_b, g_out_attn):
    b, seq_len, _ = k.shape
    q_rows = Q_BLOCK * Q_BLOCKS_PER_STEP
    nq = seq_len // q_rows
    past = cache_k.shape[1]
    return pl.pallas_call(
        functools.partial(_lat_attn_kernel, seq_len=seq_len),
        grid=(b, nq),
        in_specs=[
            pl.BlockSpec((q_rows, ATTN_WIDTH), lambda bi, n: (bi * nq + n, 0)),
            pl.BlockSpec((1, seq_len, KV_WIDTH), lambda bi, n: (bi, 0, 0)),
            pl.BlockSpec((1, seq_len, KV_WIDTH), lambda bi, n: (bi, 0, 0)),
            pl.BlockSpec((1, past, KV_WIDTH), lambda bi, n: (bi, 0, 0)),
            pl.BlockSpec((1, past, KV_WIDTH), lambda bi, n: (bi, 0, 0)),
            pl.BlockSpec((N_HEADS, LANES), lambda bi, n: (0, 0)),
            pl.BlockSpec((1, ATTN_WIDTH), lambda bi, n: (0, 0)),
        ],
        out_specs=pl.BlockSpec((q_rows, ATTN_WIDTH), lambda bi, n: (bi * nq + n, 0)),
        out_shape=jax.ShapeDtypeStruct((b * seq_len, ATTN_WIDTH), BF16),
        compiler_params=_params(2),
        name="latent_attention",
    )(q, k, v, cache_k, cache_v, sink_b, g_out_attn)


def _route(logits):
    lane = lax.broadcasted_iota(jnp.int32, logits.shape, 1).astype(F32)
    neg = -jnp.inf

    def first_argmax(x):
        mx = jnp.max(x, axis=-1, keepdims=True)
        return mx, jnp.min(jnp.where(x == mx, lane, float(ROUTER_LANES)), axis=-1, keepdims=True)

    gl = jnp.where(lane < N_EXPERT_GROUPS, logits, neg)
    gmax, g_idx = first_argmax(gl)
    p_g = 1.0 / jnp.sum(jnp.exp(gl - gmax), axis=-1, keepdims=True)
    base = N_EXPERT_GROUPS + EXPERTS_PER_GROUP * g_idx
    el = jnp.where((lane >= base) & (lane < base + EXPERTS_PER_GROUP), logits, neg)
    v1, i1 = first_argmax(el)
    v2, i2 = first_argmax(jnp.where(lane == i1, neg, el))
    e2 = jnp.exp(v2 - v1)
    w1 = p_g / (1.0 + e2)
    w2 = p_g * e2 / (1.0 + e2)
    ids = jnp.where(lane == 0.0, i1 - N_EXPERT_GROUPS, jnp.where(lane == 1.0, i2 - N_EXPERT_GROUPS, 0.0))
    wts = jnp.where(lane == 0.0, w1, jnp.where(lane == 1.0, w2, 0.0))
    return ids.astype(jnp.int32), wts


def _outproj_kernel(rows_ref, pool_p, attn_p, x_p, pool_s, attn_s, x_s, mod_ref, g_ref, w_ref, rw_ref,
                    rb_ref, x1_ref, h2_ref, ids_ref, wts_ref, mix0, mix1, *, n_ctx_tiles, n_tiles):
    del rows_ref
    i = pl.program_id(0)

    def project(pool_ref, attn_ref, mix_ref):
        mix_ref[...] = (_dot(pool_ref[...], w_ref[0:POOL_WIDTH, :])
                        + _dot(attn_ref[...], w_ref[POOL_WIDTH:POOL_WIDTH + ATTN_WIDTH, :]))

    def tail(x_ref, mix_ref):
        m = mod_ref[0]
        x1 = x_ref[...] + m[2:3] * mix_ref[...]
        x1_ref[...] = x1
        h2 = _rms(x1) * g_ref[...] * (1.0 + m[4:5]) + m[3:4]
        _store_row_tiles(h2_ref, _pack_bf16_pairs(h2))
        hi = h2.astype(BF16)
        lo = (h2 - hi.astype(F32)).astype(BF16)
        both = _dot(hi, rw_ref[...])
        logits = (both[:, :ROUTER_LANES] + both[:, ROUTER_LANES:] + _dot(lo, rw_ref[:, :ROUTER_LANES])
                  + rb_ref[...])
        ids, wts = _route(logits)
        ids_ref[...] = ids
        wts_ref[...] = wts

    ctx_in = (pool_p, attn_p)
    lat_in = (pool_s, attn_s)
    for parity, (cur, prev) in enumerate(((mix0, mix1), (mix1, mix0))):
        mine = (i % 2) == parity

        @pl.when(mine & (i == 0))
        def _(cur=cur):
            project(*ctx_in, cur)

        @pl.when(mine & (i >= 1) & (i < n_ctx_tiles))
        def _(cur=cur, prev=prev):
            tail(x_p, prev)
            project(*ctx_in, cur)

        @pl.when(mine & (i == n_ctx_tiles))
        def _(cur=cur, prev=prev):
            tail(x_p, prev)
            project(*lat_in, cur)

        @pl.when(mine & (i > n_ctx_tiles) & (i < n_tiles))
        def _(cur=cur, prev=prev):
            tail(x_s, prev)
            project(*lat_in, cur)

        @pl.when(mine & (i == n_tiles))
        def _(prev=prev):
            tail(x_s, prev)


def _out_projection(ctx, lat, rows, mod3, g_ffn, w_out_b, rw_split, rb):
    tm = TOKEN_TILE
    n_ctx = ctx[2].shape[0] // tm
    n_lat = lat[2].shape[0] // tm
    n_tiles = n_ctx + n_lat
    t_all = n_tiles * tm
    assert n_ctx >= 1 and n_lat >= 1

    def ctx_map(i, r):
        return (jnp.minimum(i, n_ctx - 1), 0)

    def lat_map(i, r):
        return (jnp.clip(i - n_ctx, 0, n_lat - 1), 0)

    def ctx_prev_map(i, r):
        return (jnp.clip(i - 1, 0, n_ctx - 1), 0)

    def lat_prev_map(i, r):
        return (jnp.clip(i - 1 - n_ctx, 0, n_lat - 1), 0)

    def prev_tile(i):
        return jnp.maximum(i - 1, 0)

    in_specs = [
        pl.BlockSpec((tm, POOL_WIDTH), ctx_map),
        pl.BlockSpec((tm, ATTN_WIDTH), ctx_map),
        pl.BlockSpec((tm, D_MODEL), ctx_prev_map),
        pl.BlockSpec((tm, POOL_WIDTH), lat_map),
        pl.BlockSpec((tm, ATTN_WIDTH), lat_map),
        pl.BlockSpec((tm, D_MODEL), lat_prev_map),
        pl.BlockSpec((1, 6, D_MODEL), lambda i, r: (r[prev_tile(i)], 0, 0)),
        pl.BlockSpec((1, D_MODEL), lambda i, r: (0, 0)),
        pl.BlockSpec((D_MODEL, D_MODEL), lambda i, r: (0, 0), pipeline_mode=pl.Buffered(1)),
        pl.BlockSpec((D_MODEL, 2 * ROUTER_LANES), lambda i, r: (0, 0)),
        pl.BlockSpec((1, ROUTER_LANES), lambda i, r: (0, 0)),
    ]
    out_shape = (
        jax.ShapeDtypeStruct((t_all, D_MODEL), F32),
        jax.ShapeDtypeStruct((t_all * ROW_TILE, LANES), jnp.uint32),
        jax.ShapeDtypeStruct((t_all, ROUTER_LANES), jnp.int32),
        jax.ShapeDtypeStruct((t_all, ROUTER_LANES), F32),
    )
    out_specs = (
        pl.BlockSpec((tm, D_MODEL), lambda i, r: (prev_tile(i), 0)),
        pl.BlockSpec((tm * ROW_TILE, LANES), lambda i, r: (prev_tile(i), 0)),
        pl.BlockSpec((tm, ROUTER_LANES), lambda i, r: (prev_tile(i), 0)),
        pl.BlockSpec((tm, ROUTER_LANES), lambda i, r: (prev_tile(i), 0)),
    )
    return pl.pallas_call(
        functools.partial(_outproj_kernel, n_ctx_tiles=n_ctx, n_tiles=n_tiles),
        grid_spec=pltpu.PrefetchScalarGridSpec(
            num_scalar_prefetch=1, grid=(n_tiles + 1,), in_specs=in_specs, out_specs=out_specs,
            scratch_shapes=[pltpu.VMEM((tm, D_MODEL), F32), pltpu.VMEM((tm, D_MODEL), F32)]),
        out_shape=out_shape,
        compiler_params=_params(1),
        name="out_projection",
    )(rows, *ctx, *lat, mod3, g_ffn, w_out_b, rw_split, rb)


def _gather_rows(idx_ref, base, n_rows, src_hbm, dst, sem, *, unrolled, both_queues=False):
    def one(r, priority):
        src_row = pl.multiple_of(idx_ref[base + r], ROW_TILE)
        pltpu.make_async_copy(
            src_hbm.at[pl.ds(src_row, ROW_TILE)], dst.at[pl.ds(r * ROW_TILE, ROW_TILE)],
            sem).start(priority=priority)

    if unrolled:
        for r in range(n_rows):
            one(r, r % 2 if both_queues else 0)
    else:
        def body(r, carry):
            one(r, 0)
            return carry
        lax.fori_loop(0, n_rows, body, 0, unroll=8)


def _wait_rows(src_hbm, dst, sem):
    pltpu.make_async_copy(src_hbm.at[pl.ds(0, dst.shape[0])], dst, sem).wait()


def _moe_kernel(be_ref, ne_ref, nu_ref, first_ref, tok_ref, h_hbm, wg_hbm, wu_hbm, wd_hbm, y_ref,
                xg0, xg1, xg2, sg, su, sd, wgb, wub, wdb, sem, wsem):
    b = pl.program_id(0)
    n_used = nu_ref[0]
    last = n_used - 1
    active = b < n_used
    bufs = (xg0, xg1, xg2)
    n_buf = len(bufs)

    def weight_copies(e):
        return (pltpu.make_async_copy(wg_hbm.at[e], sg, wsem.at[0]),
                pltpu.make_async_copy(wu_hbm.at[e], su, wsem.at[1]),
                pltpu.make_async_copy(wd_hbm.at[e], sd, wsem.at[2]))

    @pl.when(b == 0)
    def _():
        for cp in weight_copies(be_ref[0]):
            cp.start(priority=1)
        for i in range(n_buf - 1):
            _gather_rows(tok_ref, first_ref[jnp.minimum(i, last)], EXPERT_ROWS, h_hbm, bufs[i], sem.at[i],
                         unrolled=False)

    new_expert = (b == 0) | (be_ref[b] != be_ref[jnp.maximum(b - 1, 0)])

    @pl.when(active & new_expert)
    def _():
        for cp in weight_copies(be_ref[b]):
            cp.wait()
        wgb[...] = sg[...].astype(BF16)
        wub[...] = su[...].astype(BF16)
        wdb[...] = sd[...].astype(BF16)

        @pl.when(ne_ref[b] >= 0)
        def _():
            for cp in weight_copies(ne_ref[b]):
                cp.start(priority=1)

    def step(slot):
        cur = bufs[slot]
        ahead = bufs[(slot + n_buf - 1) % n_buf]
        sem_ahead = sem.at[(slot + n_buf - 1) % n_buf]
        _wait_rows(h_hbm, cur, sem.at[slot])
        _gather_rows(tok_ref, first_ref[jnp.minimum(b + n_buf - 1, last)], EXPERT_ROWS, h_hbm, ahead,
                     sem_ahead, unrolled=True, both_queues=True)
        lo, hi = _unpack_bf16_pairs(_load_row_tiles(cur, EXPERT_ROWS))
        lo = lo.astype(BF16)
        hi = hi.astype(BF16)
        a = _dot(lo, wgb[0:PACKED, :]) + _dot(hi, wgb[PACKED:D_MODEL, :])
        u = _dot(lo, wub[0:PACKED, :]) + _dot(hi, wub[PACKED:D_MODEL, :])
        act = (a / (1.0 + jnp.exp(-a)) * u).astype(BF16)
        _store_row_tiles(y_ref, _pack_bf16_pairs(_dot(act, wdb[...])))

        @pl.when(b == last)
        def _():
            for i in range(1, n_buf):
                _wait_rows(h_hbm, bufs[(slot + i) % n_buf], sem.at[(slot + i) % n_buf])

    for slot in range(n_buf):
        @pl.when(active & (b % n_buf == slot))
        def _(slot=slot):
            step(slot)

    @pl.when(b >= n_used)
    def _():
        y_ref[...] = jnp.zeros_like(y_ref)


def _experts(tables, cap, h2_packed, w_gate, w_up, w_down):
    bm = EXPERT_ROWS
    xg = pltpu.VMEM((bm * ROW_TILE, LANES), jnp.uint32)
    return pl.pallas_call(
        _moe_kernel,
        grid_spec=pltpu.PrefetchScalarGridSpec(
            num_scalar_prefetch=len(tables),
            grid=(cap // bm,),
            in_specs=[pl.BlockSpec(memory_space=pl.ANY)] * 4,
            out_specs=pl.BlockSpec((bm * ROW_TILE, LANES), lambda b, *_: (b, 0)),
            scratch_shapes=[
                xg, xg, xg,
                pltpu.VMEM((D_MODEL, D_EXPERT), F32), pltpu.VMEM((D_MODEL, D_EXPERT), F32),
                pltpu.VMEM((D_EXPERT, D_MODEL), F32),
                pltpu.VMEM((D_MODEL, D_EXPERT), BF16), pltpu.VMEM((D_MODEL, D_EXPERT), BF16),
                pltpu.VMEM((D_EXPERT, D_MODEL), BF16),
                pltpu.SemaphoreType.DMA((3,)), pltpu.SemaphoreType.DMA((3,)),
            ],
        ),
        out_shape=jax.ShapeDtypeStruct((cap * ROW_TILE, LANES), jnp.uint32),
        compiler_params=_params(1),
        name="experts",
    )(*tables, h2_packed, w_gate, w_up, w_down)


def _combine_kernel(slot_ref, rows_ref, y_hbm, x1_ref, wts_ref, mod_ref, g_ref, op_ref, os_ref, yg0, yg1, sem,
                    *, n_ctx_tiles, n_tiles):
    del rows_ref
    tm = TOKEN_TILE
    i = pl.program_id(0)
    n_slots = slot_ref.shape[0] // TOP_K

    def start(tile, buf, s, unrolled):
        for k in range(TOP_K):
            _gather_rows(slot_ref, k * n_slots + tile * tm, tm, y_hbm, buf.at[k], s, unrolled=unrolled,
                         both_queues=True)

    @pl.when(i == 0)
    def _():
        start(0, yg0, sem.at[0], False)

    def step(cur, nxt, sem_cur, sem_nxt):
        for k in range(TOP_K):
            _wait_rows(y_hbm, cur.at[k], sem_cur)
        start(jnp.minimum(i + 1, n_tiles - 1), nxt, sem_nxt, True)
        w = wts_ref[...]
        lo0, hi0 = _unpack_bf16_pairs(_load_row_tiles(cur.at[0], tm))
        lo1, hi1 = _unpack_bf16_pairs(_load_row_tiles(cur.at[1], tm))
        ffn = jnp.concatenate([lo0 * w[:, 0:1] + lo1 * w[:, 1:2], hi0 * w[:, 0:1] + hi1 * w[:, 1:2]], axis=-1)
        x2 = x1_ref[...] + mod_ref[0][5:6] * ffn
        out = _rms(x2) * g_ref[...]

        @pl.when(i < n_ctx_tiles)
        def _():
            op_ref[...] = out

        @pl.when(i >= n_ctx_tiles)
        def _():
            os_ref[...] = out

        @pl.when(i == n_tiles - 1)
        def _():
            for k in range(TOP_K):
                _wait_rows(y_hbm, nxt.at[k], sem_nxt)

    @pl.when(i % 2 == 0)
    def _():
        step(yg0, yg1, sem.at[0], sem.at[1])

    @pl.when(i % 2 == 1)
    def _():
        step(yg1, yg0, sem.at[1], sem.at[0])


def _combine(slots, rows, y_buf, x1, wts, mod3, g_final, t_ctx):
    t = x1.shape[0]
    tm = TOKEN_TILE
    n_tiles = t // tm
    n_ctx = t_ctx // tm
    return pl.pallas_call(
        functools.partial(_combine_kernel, n_ctx_tiles=n_ctx, n_tiles=n_tiles),
        grid_spec=pltpu.PrefetchScalarGridSpec(
            num_scalar_prefetch=2,
            grid=(n_tiles,),
            in_specs=[
                pl.BlockSpec(memory_space=pl.ANY),
                pl.BlockSpec((tm, D_MODEL), lambda i, s, r: (i, 0)),
                pl.BlockSpec((tm, ROUTER_LANES), lambda i, s, r: (i, 0)),
                pl.BlockSpec((1, 6, D_MODEL), lambda i, s, r: (r[i], 0, 0)),
                pl.BlockSpec((1, D_MODEL), lambda i, s, r: (0, 0)),
            ],
            out_specs=(
                pl.BlockSpec((tm, D_MODEL), lambda i, s, r: (jnp.minimum(i, n_ctx - 1), 0)),
                pl.BlockSpec((tm, D_MODEL), lambda i, s, r: (jnp.maximum(i - n_ctx, 0), 0)),
            ),
            scratch_shapes=[pltpu.VMEM((TOP_K, tm * ROW_TILE, LANES), jnp.uint32),
                            pltpu.VMEM((TOP_K, tm * ROW_TILE, LANES), jnp.uint32),
                            pltpu.SemaphoreType.DMA((2,))],
        ),
        out_shape=(jax.ShapeDtypeStruct((t_ctx, D_MODEL), F32),
                   jax.ShapeDtypeStruct((t - t_ctx, D_MODEL), F32)),
        compiler_params=_params(1),
        name="combine",
    )(slots, rows, y_buf, x1, wts, mod3, g_final)


def _rope_tables(n_tokens):
    rows = n_tokens // GRID_W
    row = jnp.repeat(jnp.arange(rows, dtype=F32), GRID_W)
    col = jnp.tile(jnp.arange(GRID_W, dtype=F32), rows)
    half = ROPE_AXIS_DIM // 2
    inv_freq = ROPE_BASE ** (-jnp.arange(half, dtype=F32) / half)
    ar = row[:, None] * inv_freq
    ac = col[:, None] * inv_freq
    cos = jnp.concatenate([jnp.cos(ar), jnp.cos(ar), jnp.cos(ac), jnp.cos(ac)], axis=-1)
    sin = jnp.concatenate([-jnp.sin(ar), jnp.sin(ar), -jnp.sin(ac), jnp.sin(ac)], axis=-1)
    return cos, sin


def _dispatch_tables(expert_ids):
    t = expert_ids[0].shape[0]
    bm = EXPERT_ROWS
    flat_e = jnp.concatenate(expert_ids)
    onehot = (flat_e[:, None] == jnp.arange(N_EXPERTS, dtype=jnp.int32)[None, :]).astype(jnp.int32)
    running = jnp.cumsum(onehot, axis=0)
    rank = jnp.sum(running * onehot, axis=1) - 1
    counts = running[-1]
    padded = (counts + bm - 1) // bm * bm
    pad_end = jnp.cumsum(padded)
    pad_start = pad_end - padded
    dest = (jnp.sum(pad_start[None, :] * onehot, axis=1) + rank).astype(jnp.int32)
    n_blocks = (t * TOP_K + N_EXPERTS * (bm - 1) + bm - 1) // bm
    order = jnp.argsort(flat_e).astype(jnp.int32)
    src_tile = jnp.where(order >= t, order - t, order) * ROW_TILE
    src_tile = jnp.concatenate([src_tile, jnp.zeros((bm,), jnp.int32)])
    n_used = (pad_end[-1] // bm).astype(jnp.int32)
    blk = jnp.arange(n_blocks, dtype=jnp.int32)
    blk = jnp.minimum(blk, n_used - 1)
    block_expert = jnp.sum((pad_end[None, :] <= (blk * bm)[:, None]).astype(jnp.int32), axis=1)
    block_expert = jnp.minimum(block_expert, N_EXPERTS - 1)
    e_ids = jnp.arange(N_EXPERTS, dtype=jnp.int32)
    later = (e_ids[None, :] > e_ids[:, None]) & (counts[None, :] > 0)
    next_of = jnp.min(jnp.where(later, e_ids[None, :], N_EXPERTS), axis=1)
    next_of = jnp.where(next_of == N_EXPERTS, -1, next_of)
    of_block = (block_expert[:, None] == e_ids[None, :]).astype(jnp.int32)
    next_expert = jnp.sum(next_of[None, :] * of_block, axis=1)
    start = jnp.cumsum(counts) - counts
    first_src = jnp.sum((start - pad_start)[None, :] * of_block, axis=1) + blk * bm
    return (block_expert.astype(jnp.int32), next_expert.astype(jnp.int32), n_used.reshape(1),
            first_src.astype(jnp.int32), src_tile, dest * ROW_TILE, n_blocks * bm)


def kernel(x_prompt, x_sample, cache_k, cache_v, c, c_ctx, w_mod, b_mod, norm_mix_g, norm_ffn_g, w_in,
           pool_w, pool_scale, attn_sink, out_norm_pool_g, out_norm_attn_g, w_out, router_group_w,
           router_group_b, router_expert_w, router_expert_b, w_gate, w_up, w_down, final_norm_g):
    depth = w_mod.shape[0]
    assert depth == 1, "single trunk layer"
    bp, lp, _ = x_prompt.shape
    bs, ls, _ = x_sample.shape
    tp, ts = bp * lp, bs * ls
    tm = TOKEN_TILE
    l = 0

    cond8 = jnp.zeros((8, D_MODEL), F32).at[:bs].set(c).at[bs].set(c_ctx)
    mod3 = _modulation(cond8, w_mod[l], b_mod[l]).reshape(8, 6, D_MODEL)
    rows_p = jnp.full((tp // tm,), bs, jnp.int32)
    rows_s = jnp.arange(ts // tm, dtype=jnp.int32) // (ls // tm)

    w_in_b = w_in[l].astype(BF16)
    w_out_b = w_out[l].astype(BF16)
    pool_w_b = pool_w[l].astype(BF16)
    g_mix = norm_mix_g[l].reshape(1, D_MODEL)
    g_ffn = norm_ffn_g[l].reshape(1, D_MODEL)
    g_pool = out_norm_pool_g[l].reshape(1, POOL_WIDTH)
    g_attn = out_norm_attn_g[l].reshape(1, ATTN_WIDTH)
    p_scale = pool_scale[l].reshape(1, POOL_WIDTH)
    sink_b = jnp.broadcast_to(attn_sink[l][:, None], (N_HEADS, LANES))
    g_final = final_norm_g.reshape(1, D_MODEL)

    rw = jnp.concatenate(
        [router_group_w[l], jnp.transpose(router_expert_w[l], (1, 0, 2)).reshape(D_MODEL, N_EXPERTS)], axis=1)
    rw = jnp.pad(rw, ((0, 0), (0, ROUTER_LANES - rw.shape[1])))
    rw_hi = rw.astype(BF16)
    rw_split = jnp.concatenate([rw_hi, (rw - rw_hi.astype(F32)).astype(BF16)], axis=1)
    rb = jnp.concatenate([router_group_b[l], router_expert_b[l].reshape(N_EXPERTS)])
    rb = jnp.pad(rb, (0, ROUTER_LANES - rb.shape[0])).reshape(1, ROUTER_LANES)

    xp = x_prompt.reshape(tp, D_MODEL)
    xs = x_sample.reshape(ts, D_MODEL)

    up_p, q_p, k_p, v_p = _in_projection(xp, rows_p, mod3, g_mix, w_in_b, None, F32)
    pool_p = _pool_mixer(up_p.reshape(bp, lp, POOL_WIDTH), pool_w_b, p_scale, g_pool).reshape(tp, POOL_WIDTH)
    attn_p = _context_attention(q_p, k_p, v_p, sink_b, g_attn, lp)
    up_s, q_s, k_s, v_s = _in_projection(xs, rows_s, mod3, g_mix, w_in_b, _rope_tables(ls), BF16)
    pool_s = _pool_mixer(up_s.reshape(bs, ls, POOL_WIDTH), pool_w_b, p_scale, g_pool).reshape(ts, POOL_WIDTH)
    attn_s = _latent_attention(
        q_s, k_s.reshape(bs, ls, KV_WIDTH), v_s.reshape(bs, ls, KV_WIDTH),
        cache_k[:, l].reshape(bs, -1, KV_WIDTH), cache_v[:, l].reshape(bs, -1, KV_WIDTH), sink_b, g_attn)

    rows_all = jnp.concatenate([rows_p, rows_s])
    x1_all, h2_all, ids_all, wts_all = _out_projection(
        (pool_p, attn_p, xp), (pool_s, attn_s, xs), rows_all, mod3, g_ffn, w_out_b, rw_split, rb)

    *expert_tables, slots, cap = _dispatch_tables([ids_all[:, k] for k in range(TOP_K)])
    y_buf = _experts(expert_tables, cap, h2_all, w_gate[l], w_up[l], w_down[l])

    y_p, y_s = _combine(slots, rows_all, y_buf, x1_all, wts_all, mod3, g_final, tp)

    kv_shape = (bp, 1, lp, N_KV_HEADS, HEAD_DIM)
    return (y_p.reshape(bp, lp, D_MODEL), y_s.reshape(bs, ls, D_MODEL),
            k_p.reshape(kv_shape), v_p.reshape(kv_shape))
```

```python
import functools

import jax
import jax.numpy as jnp
from jax import lax
from jax.experimental import pallas as pl
from jax.experimental.pallas import tpu as pltpu

F32 = jnp.float32
BF16 = jnp.bfloat16

D_MODEL = 2048
GRID_W = 64
HEAD_DIM = 128
ATTN_WIDTH = 1024
POOL_WIDTH = 1024
N_HEADS = 8
N_KV_HEADS = 2
Q_PER_KV = 4
KV_WIDTH = 256
IN_WIDTH = 2560
WINDOW = 128
POOL_WINDOWS = (2, 4, 8, 16)
POOL_GROUP_DIM = 256
ROPE_BASE = 10000.0
ROPE_AXIS_DIM = 64
N_EXPERT_GROUPS = 4
EXPERTS_PER_GROUP = 8
N_EXPERTS = 32
TOP_K = 2
D_EXPERT = 512
NORM_EPS = 1e-6
NEG_INF = -1e30
ATTN_SCALE = HEAD_DIM ** -0.5
LOG2E = 1.4426950408889634
Q_SCALE = ATTN_SCALE * LOG2E

LANES = 128
Q_BLOCKS_PER_STEP = 4
CTX_SEQS_PER_STEP = 2
VMEM_LIMIT = 48 * 1024 * 1024
PACKED = D_MODEL // 2
ROW_TILE = 8
assert PACKED == ROW_TILE * LANES, "a packed row must fill exactly one tile"
MOD_TILE = 1024
TOKEN_TILE = 256
IN_TILE = 512
POOL_TILES_PER_STEP = 2
POOL_HALO = 16
Q_BLOCK = 128
EXPERT_ROWS = 256
ROUTER_LANES = 128


def _params(n_grid):
    return pltpu.CompilerParams(
        dimension_semantics=("arbitrary",) * n_grid, vmem_limit_bytes=VMEM_LIMIT)


def _rms(x):
    return x * lax.rsqrt(jnp.mean(x * x, axis=-1, keepdims=True) + NORM_EPS)


def _dot(a, b):
    return jnp.dot(a, b, preferred_element_type=F32)


def _dot_nt(a, b):
    return lax.dot_general(a, b, (((1,), (1,)), ((), ())), preferred_element_type=F32)


def _pack_bf16_pairs(x):
    c = x.shape[1] // 2
    lo = lax.bitcast_convert_type(x[:, :c].astype(BF16).astype(F32), jnp.uint32)
    hi = lax.bitcast_convert_type(x[:, c:].astype(BF16).astype(F32), jnp.uint32)
    return hi | (lo >> 16)


def _store_row_tiles(ref, x):
    n = x.shape[0]
    for c in range(PACKED // LANES):
        ref[pl.ds(c, n, stride=ROW_TILE), :] = x[:, c * LANES:(c + 1) * LANES]


def _load_row_tiles(ref, n):
    return jnp.concatenate(
        [ref[pl.ds(c, n, stride=ROW_TILE), :] for c in range(PACKED // LANES)], axis=-1)


def _unpack_bf16_pairs(u):
    lo = lax.bitcast_convert_type(u << 16, F32)
    hi = lax.bitcast_convert_type(u & jnp.uint32(0xFFFF0000), F32)
    return lo, hi


def _mod_kernel(cond_ref, w_ref, b_ref, o_ref):
    c = cond_ref[...]
    s = c / (1.0 + jnp.exp(-c))
    o_ref[...] = _dot(s.astype(BF16), w_ref[...].astype(BF16)) + b_ref[...]


def _modulation(cond8, w_mod, b_mod):
    n = w_mod.shape[1]
    tn = MOD_TILE
    return pl.pallas_call(
        _mod_kernel,
        grid=(n // tn,),
        in_specs=[
            pl.BlockSpec((8, D_MODEL), lambda j: (0, 0)),
            pl.BlockSpec((D_MODEL, tn), lambda j: (0, j)),
            pl.BlockSpec((1, tn), lambda j: (0, j)),
        ],
        out_specs=pl.BlockSpec((8, tn), lambda j: (0, j)),
        out_shape=jax.ShapeDtypeStruct((8, n), F32),
        compiler_params=_params(1),
        name="modulation",
    )(cond8, w_mod, b_mod.reshape(1, n))


def _rope(x, cos, sin_signed):
    lane = lax.broadcasted_iota(jnp.int32, x.shape, 1)
    partner = jnp.where((lane & 63) < 32,
                        pltpu.roll(x, HEAD_DIM - 32, axis=1),
                        pltpu.roll(x, 32, axis=1))
    return x * cos + partner * sin_signed


def _inproj_kernel(rows_ref, x_ref, mod_ref, g_ref, w_ref, *rest, rope):
    del rows_ref
    if rope:
        cos_ref, sin_ref, up_ref, q_ref, k_ref, v_ref = rest
    else:
        up_ref, q_ref, k_ref, v_ref = rest
    m = mod_ref[0]
    h = _rms(x_ref[...]) * g_ref[...]
    hb = (h * (1.0 + m[1:2]) + m[0:1]).astype(BF16)
    up_ref[...] = _dot(hb, w_ref[:, 0:POOL_WIDTH]).astype(BF16)
    q = _dot(hb, w_ref[:, POOL_WIDTH:POOL_WIDTH + ATTN_WIDTH])
    k = _dot(hb, w_ref[:, POOL_WIDTH + ATTN_WIDTH:IN_WIDTH - KV_WIDTH])
    v = _dot(hb, w_ref[:, IN_WIDTH - KV_WIDTH:IN_WIDTH])
    if rope:
        cos = cos_ref[...]
        sin = sin_ref[...]
        for hd in range(N_HEADS):
            sl = slice(hd * HEAD_DIM, (hd + 1) * HEAD_DIM)
            q_ref[:, sl] = (_rope(q[:, sl], cos, sin) * Q_SCALE).astype(BF16)
        for hd in range(N_KV_HEADS):
            sl = slice(hd * HEAD_DIM, (hd + 1) * HEAD_DIM)
            k_ref[:, sl] = _rope(k[:, sl], cos, sin).astype(k_ref.dtype)
    else:
        q_ref[...] = (q * Q_SCALE).astype(BF16)
        k_ref[...] = k.astype(k_ref.dtype)
    v_ref[...] = v.astype(v_ref.dtype)


def _in_projection(x, rows, mod3, g_mix, w_in_b, rope_tabs, kv_dtype):
    t = x.shape[0]
    tm = IN_TILE
    per = IN_TILE // TOKEN_TILE
    rope = rope_tabs is not None
    in_specs = [
        pl.BlockSpec((tm, D_MODEL), lambda i, r: (i, 0)),
        pl.BlockSpec((1, 6, D_MODEL), lambda i, r: (r[i * per], 0, 0)),
        pl.BlockSpec((1, D_MODEL), lambda i, r: (0, 0)),
        pl.BlockSpec((D_MODEL, IN_WIDTH), lambda i, r: (0, 0), pipeline_mode=pl.Buffered(1)),
    ]
    args = [x, mod3, g_mix, w_in_b]
    if rope:
        seq_tiles = rope_tabs[0].shape[0] // tm
        for tab in rope_tabs:
            in_specs.append(pl.BlockSpec((tm, HEAD_DIM), lambda i, r: (i % seq_tiles, 0)))
            args.append(tab)
    out_shape = (
        jax.ShapeDtypeStruct((t, POOL_WIDTH), BF16),
        jax.ShapeDtypeStruct((t, ATTN_WIDTH), BF16),
        jax.ShapeDtypeStruct((t, KV_WIDTH), kv_dtype),
        jax.ShapeDtypeStruct((t, KV_WIDTH), kv_dtype),
    )
    out_specs = (
        pl.BlockSpec((tm, POOL_WIDTH), lambda i, r: (i, 0)),
        pl.BlockSpec((tm, ATTN_WIDTH), lambda i, r: (i, 0)),
        pl.BlockSpec((tm, KV_WIDTH), lambda i, r: (i, 0)),
        pl.BlockSpec((tm, KV_WIDTH), lambda i, r: (i, 0)),
    )
    return pl.pallas_call(
        functools.partial(_inproj_kernel, rope=rope),
        grid_spec=pltpu.PrefetchScalarGridSpec(
            num_scalar_prefetch=1, grid=(t // tm,), in_specs=in_specs, out_specs=out_specs),
        out_shape=out_shape,
        compiler_params=_params(1),
        name="in_projection_rope" if rope else "in_projection",
    )(rows, *args)


def _pool_kernel(u_ref, pw_ref, ps_ref, g_ref, o_ref, *, seq_len):
    tq = TOKEN_TILE
    win = min(seq_len, tq + 2 * POOL_HALO)
    for sub in range(o_ref.shape[1] // tq):
        t0 = pl.program_id(1) * o_ref.shape[1] + sub * tq
        src0 = pl.multiple_of(jnp.clip(t0 - POOL_HALO, 0, seq_len - win), POOL_HALO)
        u = u_ref[0, pl.ds(src0, win), :]
        t = t0 + lax.broadcasted_iota(jnp.int32, (tq, win), 0)
        j = src0 + lax.broadcasted_iota(jnp.int32, (tq, win), 1)
        tc = t0 + lax.broadcasted_iota(jnp.int32, (tq, 1), 0)
        ys = []
        for gi, w in enumerate(POOL_WINDOWS):
            lo = jnp.maximum(t - w // 2, 0)
            hi = jnp.minimum(t + (w - w // 2), seq_len)
            cnt = (hi - lo).astype(F32)
            a = jnp.where((j >= lo) & (j < hi), 1.0, 0.0) - jnp.where(j == t, cnt, 0.0)
            cnt_col = (jnp.minimum(tc + (w - w // 2), seq_len) - jnp.maximum(tc - w // 2, 0)).astype(F32)
            sl = slice(gi * POOL_GROUP_DIM, (gi + 1) * POOL_GROUP_DIM)
            d = _dot(a.astype(BF16), u[:, sl]) / cnt_col
            ys.append(_dot(d.astype(BF16), pw_ref[gi]))
        y = jnp.concatenate(ys, axis=-1) * ps_ref[...]
        o_ref[0, sub * tq:(sub + 1) * tq, :] = (_rms(y) * g_ref[...]).astype(BF16)


def _pool_mixer(up, pool_w_b, pool_scale, g_out_pool):
    b, seq_len, _ = up.shape
    tq = TOKEN_TILE * min(POOL_TILES_PER_STEP, seq_len // TOKEN_TILE)
    return pl.pallas_call(
        functools.partial(_pool_kernel, seq_len=seq_len),
        grid=(b, seq_len // tq),
        in_specs=[
            pl.BlockSpec((1, seq_len, POOL_WIDTH), lambda bi, i: (bi, 0, 0)),
            pl.BlockSpec((len(POOL_WINDOWS), POOL_GROUP_DIM, POOL_GROUP_DIM), lambda bi, i: (0, 0, 0)),
            pl.BlockSpec((1, POOL_WIDTH), lambda bi, i: (0, 0)),
            pl.BlockSpec((1, POOL_WIDTH), lambda bi, i: (0, 0)),
        ],
        out_specs=pl.BlockSpec((1, tq, POOL_WIDTH), lambda bi, i: (bi, i, 0)),
        out_shape=jax.ShapeDtypeStruct((b, seq_len, POOL_WIDTH), BF16),
        compiler_params=_params(2),
        name=f"pool_mixer_{seq_len}",
    )(up, pool_w_b, pool_scale, g_out_pool)


def _stack_heads(q, kh, rows):
    return jnp.concatenate(
        [q[:, (kh * Q_PER_KV + g) * HEAD_DIM:(kh * Q_PER_KV + g + 1) * HEAD_DIM]
         for g in range(Q_PER_KV)], axis=0)


def _sink_column(sink_ref, kh, rows):
    return jnp.concatenate(
        [jnp.broadcast_to(sink_ref[kh * Q_PER_KV + g:kh * Q_PER_KV + g + 1, 0:1] * LOG2E, (rows, 1))
         for g in range(Q_PER_KV)], axis=0)


def _attend(q4, key_sets, value_sets, sk, band=None):
    s = [_dot_nt(q4, k) for k in key_sets]
    if band is not None:
        s[0] = jnp.where(band, s[0], NEG_INF)
    m = sk
    for si in s:
        m = jnp.maximum(m, jnp.max(si, axis=-1, keepdims=True))
    acc = None
    for si, v in zip(s, value_sets):
        v1 = jnp.concatenate([v, jnp.ones_like(v)], axis=-1)
        part = _dot(jnp.exp2(si - m).astype(BF16), v1)
        acc = part if acc is None else acc + part
    den = acc[:, HEAD_DIM:] + jnp.exp2(sk - m)
    return acc[:, :HEAD_DIM] / den


def _ctx_attn_kernel(q_ref, k_ref, v_ref, sink_ref, g_ref, o_ref, *, seq_len):
    rows = seq_len
    for j in range(q_ref.shape[0] // rows):
        rs = slice(j * rows, (j + 1) * rows)
        q = q_ref[rs, :]
        heads = [None] * N_HEADS
        for kh in range(N_KV_HEADS):
            sl = slice(kh * HEAD_DIM, (kh + 1) * HEAD_DIM)
            o = _attend(_stack_heads(q, kh, rows), [k_ref[rs, sl].astype(BF16)],
                        [v_ref[rs, sl].astype(BF16)], _sink_column(sink_ref, kh, rows))
            for g in range(Q_PER_KV):
                heads[kh * Q_PER_KV + g] = o[g * rows:(g + 1) * rows]
        y = jnp.concatenate(heads, axis=-1)
        o_ref[rs, :] = (_rms(y) * g_ref[...]).astype(BF16)


def _context_attention(q, k, v, sink_b, g_out_attn, seq_len):
    t = q.shape[0]
    blk = seq_len * CTX_SEQS_PER_STEP
    return pl.pallas_call(
        functools.partial(_ctx_attn_kernel, seq_len=seq_len),
        grid=(t // blk,),
        in_specs=[
            pl.BlockSpec((blk, ATTN_WIDTH), lambda b: (b, 0)),
            pl.BlockSpec((blk, KV_WIDTH), lambda b: (b, 0)),
            pl.BlockSpec((blk, KV_WIDTH), lambda b: (b, 0)),
            pl.BlockSpec((N_HEADS, LANES), lambda b: (0, 0)),
            pl.BlockSpec((1, ATTN_WIDTH), lambda b: (0, 0)),
        ],
        out_specs=pl.BlockSpec((blk, ATTN_WIDTH), lambda b: (b, 0)),
        out_shape=jax.ShapeDtypeStruct((t, ATTN_WIDTH), BF16),
        compiler_params=_params(1),
        name="context_attention",
    )(q, k, v, sink_b, g_out_attn)


def _lat_attn_kernel(q_ref, k_ref, v_ref, ck_ref, cv_ref, sink_ref, g_ref, o_ref, *, seq_len):
    rows = Q_BLOCK
    span = 3 * Q_BLOCK
    shape = (Q_PER_KV * rows, span)
    row_in_block = lax.broadcasted_iota(jnp.int32, shape, 0) & (rows - 1)
    col = lax.broadcasted_iota(jnp.int32, shape, 1)
    for j in range(Q_BLOCKS_PER_STEP):
        q0 = (pl.program_id(1) * Q_BLOCKS_PER_STEP + j) * rows
        start = pl.multiple_of(jnp.clip(q0 - Q_BLOCK, 0, seq_len - span), Q_BLOCK)
        q = q_ref[j * rows:(j + 1) * rows, :]
        kl = k_ref[0, pl.ds(start, span), :]
        vl = v_ref[0, pl.ds(start, span), :]
        band = jnp.abs(row_in_block - col + (q0 - start)) <= WINDOW
        heads = [None] * N_HEADS
        for kh in range(N_KV_HEADS):
            sl = slice(kh * HEAD_DIM, (kh + 1) * HEAD_DIM)
            o = _attend(_stack_heads(q, kh, rows),
                        [kl[:, sl], ck_ref[0, :, sl].astype(BF16)],
                        [vl[:, sl], cv_ref[0, :, sl].astype(BF16)],
                        _sink_column(sink_ref, kh, rows), band)
            for g in range(Q_PER_KV):
                heads[kh * Q_PER_KV + g] = o[g * rows:(g + 1) * rows]
        y = jnp.concatenate(heads, axis=-1)
        o_ref[j * rows:(j + 1) * rows, :] = (_rms(y) * g_ref[...]).astype(BF16)


def _latent_attention(q, k, v, cache_k, cache_v, sink_b, g_out_attn):
    b, seq_len, _ = k.shape
    q_rows = Q_BLOCK * Q_BLOCKS_PER_STEP
    nq = seq_len // q_rows
    past = cache_k.shape[1]
    return pl.pallas_call(
        functools.partial(_lat_attn_kernel, seq_len=seq_len),
        grid=(b, nq),
        in_specs=[
            pl.BlockSpec((q_rows, ATTN_WIDTH), lambda bi, n: (bi * nq + n, 0)),
            pl.BlockSpec((1, seq_len, KV_WIDTH), lambda bi, n: (bi, 0, 0)),
            pl.BlockSpec((1, seq_len, KV_WIDTH), lambda bi, n: (bi, 0, 0)),
            pl.BlockSpec((1, past, KV_WIDTH), lambda bi, n: (bi, 0, 0)),
            pl.BlockSpec((1, past, KV_WIDTH), lambda bi, n: (bi, 0, 0)),
            pl.BlockSpec((N_HEADS, LANES), lambda bi, n: (0, 0)),
            pl.BlockSpec((1, ATTN_WIDTH), lambda bi, n: (0, 0)),
        ],
        out_specs=pl.BlockSpec((q_rows, ATTN_WIDTH), lambda bi, n: (bi * nq + n, 0)),
        out_shape=jax.ShapeDtypeStruct((b * seq_len, ATTN_WIDTH), BF16),
        compiler_params=_params(2),
        name="latent_attention",
    )(q, k, v, cache_k, cache_v, sink_b, g_out_attn)


def _route(logits):
    lane = lax.broadcasted_iota(jnp.int32, logits.shape, 1).astype(F32)
    neg = -jnp.inf

    def first_argmax(x):
        mx = jnp.max(x, axis=-1, keepdims=True)
        return mx, jnp.min(jnp.where(x == mx, lane, float(ROUTER_LANES)), axis=-1, keepdims=True)

    gl = jnp.where(lane < N_EXPERT_GROUPS, logits, neg)
    gmax, g_idx = first_argmax(gl)
    p_g = 1.0 / jnp.sum(jnp.exp(gl - gmax), axis=-1, keepdims=True)
    base = N_EXPERT_GROUPS + EXPERTS_PER_GROUP * g_idx
    el = jnp.where((lane >= base) & (lane < base + EXPERTS_PER_GROUP), logits, neg)
    v1, i1 = first_argmax(el)
    v2, i2 = first_argmax(jnp.where(lane == i1, neg, el))
    e2 = jnp.exp(v2 - v1)
    w1 = p_g / (1.0 + e2)
    w2 = p_g * e2 / (1.0 + e2)
    ids = jnp.where(lane == 0.0, i1 - N_EXPERT_GROUPS, jnp.where(lane == 1.0, i2 - N_EXPERT_GROUPS, 0.0))
    wts = jnp.where(lane == 0.0, w1, jnp.where(lane == 1.0, w2, 0.0))
    return ids.astype(jnp.int32), wts


def _outproj_kernel(rows_ref, pool_p, attn_p, x_p, pool_s, attn_s, x_s, mod_ref, g_ref, w_ref, rw_ref,
                    rb_ref, x1_ref, h2_ref, ids_ref, wts_ref, mix0, mix1, *, n_ctx_tiles, n_tiles):
    del rows_ref
    i = pl.program_id(0)

    def project(pool_ref, attn_ref, mix_ref):
        mix_ref[...] = (_dot(pool_ref[...], w_ref[0:POOL_WIDTH, :])
                        + _dot(attn_ref[...], w_ref[POOL_WIDTH:POOL_WIDTH + ATTN_WIDTH, :]))

    def tail(x_ref, mix_ref):
        m = mod_ref[0]
        x1 = x_ref[...] + m[2:3] * mix_ref[...]
        x1_ref[...] = x1
        h2 = _rms(x1) * g_ref[...] * (1.0 + m[4:5]) + m[3:4]
        _store_row_tiles(h2_ref, _pack_bf16_pairs(h2))
        hi = h2.astype(BF16)
        lo = (h2 - hi.astype(F32)).astype(BF16)
        both = _dot(hi, rw_ref[...])
        logits = (both[:, :ROUTER_LANES] + both[:, ROUTER_LANES:] + _dot(lo, rw_ref[:, :ROUTER_LANES])
                  + rb_ref[...])
        ids, wts = _route(logits)
        ids_ref[...] = ids
        wts_ref[...] = wts

    ctx_in = (pool_p, attn_p)
    lat_in = (pool_s, attn_s)
    for parity, (cur, prev) in enumerate(((mix0, mix1), (mix1, mix0))):
        mine = (i % 2) == parity

        @pl.when(mine & (i == 0))
        def _(cur=cur):
            project(*ctx_in, cur)

        @pl.when(mine & (i >= 1) & (i < n_ctx_tiles))
        def _(cur=cur, prev=prev):
            tail(x_p, prev)
            project(*ctx_in, cur)

        @pl.when(mine & (i == n_ctx_tiles))
        def _(cur=cur, prev=prev):
            tail(x_p, prev)
            project(*lat_in, cur)

        @pl.when(mine & (i > n_ctx_tiles) & (i < n_tiles))
        def _(cur=cur, prev=prev):
            tail(x_s, prev)
            project(*lat_in, cur)

        @pl.when(mine & (i == n_tiles))
        def _(prev=prev):
            tail(x_s, prev)


def _out_projection(ctx, lat, rows, mod3, g_ffn, w_out_b, rw_split, rb):
    tm = TOKEN_TILE
    n_ctx = ctx[2].shape[0] // tm
    n_lat = lat[2].shape[0] // tm
    n_tiles = n_ctx + n_lat
    t_all = n_tiles * tm
    assert n_ctx >= 1 and n_lat >= 1

    def ctx_map(i, r):
        return (jnp.minimum(i, n_ctx - 1), 0)

    def lat_map(i, r):
        return (jnp.clip(i - n_ctx, 0, n_lat - 1), 0)

    def ctx_prev_map(i, r):
        return (jnp.clip(i - 1, 0, n_ctx - 1), 0)

    def lat_prev_map(i, r):
        return (jnp.clip(i - 1 - n_ctx, 0, n_lat - 1), 0)

    def prev_tile(i):
        return jnp.maximum(i - 1, 0)

    in_specs = [
        pl.BlockSpec((tm, POOL_WIDTH), ctx_map),
        pl.BlockSpec((tm, ATTN_WIDTH), ctx_map),
        pl.BlockSpec((tm, D_MODEL), ctx_prev_map),
        pl.BlockSpec((tm, POOL_WIDTH), lat_map),
        pl.BlockSpec((tm, ATTN_WIDTH), lat_map),
        pl.BlockSpec((tm, D_MODEL), lat_prev_map),
        pl.BlockSpec((1, 6, D_MODEL), lambda i, r: (r[prev_tile(i)], 0, 0)),
        pl.BlockSpec((1, D_MODEL), lambda i, r: (0, 0)),
        pl.BlockSpec((D_MODEL, D_MODEL), lambda i, r: (0, 0), pipeline_mode=pl.Buffered(1)),
        pl.BlockSpec((D_MODEL, 2 * ROUTER_LANES), lambda i, r: (0, 0)),
        pl.BlockSpec((1, ROUTER_LANES), lambda i, r: (0, 0)),
    ]
    out_shape = (
        jax.ShapeDtypeStruct((t_all, D_MODEL), F32),
        jax.ShapeDtypeStruct((t_all * ROW_TILE, LANES), jnp.uint32),
        jax.ShapeDtypeStruct((t_all, ROUTER_LANES), jnp.int32),
        jax.ShapeDtypeStruct((t_all, ROUTER_LANES), F32),
    )
    out_specs = (
        pl.BlockSpec((tm, D_MODEL), lambda i, r: (prev_tile(i), 0)),
        pl.BlockSpec((tm * ROW_TILE, LANES), lambda i, r: (prev_tile(i), 0)),
        pl.BlockSpec((tm, ROUTER_LANES), lambda i, r: (prev_tile(i), 0)),
        pl.BlockSpec((tm, ROUTER_LANES), lambda i, r: (prev_tile(i), 0)),
    )
    return pl.pallas_call(
        functools.partial(_outproj_kernel, n_ctx_tiles=n_ctx, n_tiles=n_tiles),
        grid_spec=pltpu.PrefetchScalarGridSpec(
            num_scalar_prefetch=1, grid=(n_tiles + 1,), in_specs=in_specs, out_specs=out_specs,
            scratch_shapes=[pltpu.VMEM((tm, D_MODEL), F32), pltpu.VMEM((tm, D_MODEL), F32)]),
        out_shape=out_shape,
        compiler_params=_params(1),
        name="out_projection",
    )(rows, *ctx, *lat, mod3, g_ffn, w_out_b, rw_split, rb)


def _gather_rows(idx_ref, base, n_rows, src_hbm, dst, sem, *, unrolled, both_queues=False):
    def one(r, priority):
        src_row = pl.multiple_of(idx_ref[base + r], ROW_TILE)
        pltpu.make_async_copy(
            src_hbm.at[pl.ds(src_row, ROW_TILE)], dst.at[pl.ds(r * ROW_TILE, ROW_TILE)],
            sem).start(priority=priority)

    if unrolled:
        for r in range(n_rows):
            one(r, r % 2 if both_queues else 0)
    else:
        def body(r, carry):
            one(r, 0)
            return carry
        lax.fori_loop(0, n_rows, body, 0, unroll=8)


def _wait_rows(src_hbm, dst, sem):
    pltpu.make_async_copy(src_hbm.at[pl.ds(0, dst.shape[0])], dst, sem).wait()


def _moe_kernel(be_ref, ne_ref, nu_ref, first_ref, tok_ref, h_hbm, wg_hbm, wu_hbm, wd_hbm, y_ref,
                xg0, xg1, xg2, sg, su, sd, wgb, wub, wdb, sem, wsem):
    b = pl.program_id(0)
    n_used = nu_ref[0]
    last = n_used - 1
    active = b < n_used
    bufs = (xg0, xg1, xg2)
    n_buf = len(bufs)

    def weight_copies(e):
        return (pltpu.make_async_copy(wg_hbm.at[e], sg, wsem.at[0]),
                pltpu.make_async_copy(wu_hbm.at[e], su, wsem.at[1]),
                pltpu.make_async_copy(wd_hbm.at[e], sd, wsem.at[2]))

    @pl.when(b == 0)
    def _():
        for cp in weight_copies(be_ref[0]):
            cp.start(priority=1)
        for i in range(n_buf - 1):
            _gather_rows(tok_ref, first_ref[jnp.minimum(i, last)], EXPERT_ROWS, h_hbm, bufs[i], sem.at[i],
                         unrolled=False)

    new_expert = (b == 0) | (be_ref[b] != be_ref[jnp.maximum(b - 1, 0)])

    @pl.when(active & new_expert)
    def _():
        for cp in weight_copies(be_ref[b]):
            cp.wait()
        wgb[...] = sg[...].astype(BF16)
        wub[...] = su[...].astype(BF16)
        wdb[...] = sd[...].astype(BF16)

        @pl.when(ne_ref[b] >= 0)
        def _():
            for cp in weight_copies(ne_ref[b]):
                cp.start(priority=1)

    def step(slot):
        cur = bufs[slot]
        ahead = bufs[(slot + n_buf - 1) % n_buf]
        sem_ahead = sem.at[(slot + n_buf - 1) % n_buf]
        _wait_rows(h_hbm, cur, sem.at[slot])
        _gather_rows(tok_ref, first_ref[jnp.minimum(b + n_buf - 1, last)], EXPERT_ROWS, h_hbm, ahead,
                     sem_ahead, unrolled=True, both_queues=True)
        lo, hi = _unpack_bf16_pairs(_load_row_tiles(cur, EXPERT_ROWS))
        lo = lo.astype(BF16)
        hi = hi.astype(BF16)
        a = _dot(lo, wgb[0:PACKED, :]) + _dot(hi, wgb[PACKED:D_MODEL, :])
        u = _dot(lo, wub[0:PACKED, :]) + _dot(hi, wub[PACKED:D_MODEL, :])
        act = (a / (1.0 + jnp.exp(-a)) * u).astype(BF16)
        _store_row_tiles(y_ref, _pack_bf16_pairs(_dot(act, wdb[...])))

        @pl.when(b == last)
        def _():
            for i in range(1, n_buf):
                _wait_rows(h_hbm, bufs[(slot + i) % n_buf], sem.at[(slot + i) % n_buf])

    for slot in range(n_buf):
        @pl.when(active & (b % n_buf == slot))
        def _(slot=slot):
            step(slot)

    @pl.when(b >= n_used)
    def _():
        y_ref[...] = jnp.zeros_like(y_ref)


def _experts(tables, cap, h2_packed, w_gate, w_up, w_down):
    bm = EXPERT_ROWS
    xg = pltpu.VMEM((bm * ROW_TILE, LANES), jnp.uint32)
    return pl.pallas_call(
        _moe_kernel,
        grid_spec=pltpu.PrefetchScalarGridSpec(
            num_scalar_prefetch=len(tables),
            grid=(cap // bm,),
            in_specs=[pl.BlockSpec(memory_space=pl.ANY)] * 4,
            out_specs=pl.BlockSpec((bm * ROW_TILE, LANES), lambda b, *_: (b, 0)),
            scratch_shapes=[
                xg, xg, xg,
                pltpu.VMEM((D_MODEL, D_EXPERT), F32), pltpu.VMEM((D_MODEL, D_EXPERT), F32),
                pltpu.VMEM((D_EXPERT, D_MODEL), F32),
                pltpu.VMEM((D_MODEL, D_EXPERT), BF16), pltpu.VMEM((D_MODEL, D_EXPERT), BF16),
                pltpu.VMEM((D_EXPERT, D_MODEL), BF16),
                pltpu.SemaphoreType.DMA((3,)), pltpu.SemaphoreType.DMA((3,)),
            ],
        ),
        out_shape=jax.ShapeDtypeStruct((cap * ROW_TILE, LANES), jnp.uint32),
        compiler_params=_params(1),
        name="experts",
    )(*tables, h2_packed, w_gate, w_up, w_down)


def _combine_kernel(slot_ref, rows_ref, y_hbm, x1_ref, wts_ref, mod_ref, g_ref, op_ref, os_ref, yg0, yg1, sem,
                    *, n_ctx_tiles, n_tiles):
    del rows_ref
    tm = TOKEN_TILE
    i = pl.program_id(0)
    n_slots = slot_ref.shape[0] // TOP_K

    def start(tile, buf, s, unrolled):
        for k in range(TOP_K):
            _gather_rows(slot_ref, k * n_slots + tile * tm, tm, y_hbm, buf.at[k], s, unrolled=unrolled,
                         both_queues=True)

    @pl.when(i == 0)
    def _():
        start(0, yg0, sem.at[0], False)

    def step(cur, nxt, sem_cur, sem_nxt):
        for k in range(TOP_K):
            _wait_rows(y_hbm, cur.at[k], sem_cur)
        start(jnp.minimum(i + 1, n_tiles - 1), nxt, sem_nxt, True)
        w = wts_ref[...]
        lo0, hi0 = _unpack_bf16_pairs(_load_row_tiles(cur.at[0], tm))
        lo1, hi1 = _unpack_bf16_pairs(_load_row_tiles(cur.at[1], tm))
        ffn = jnp.concatenate([lo0 * w[:, 0:1] + lo1 * w[:, 1:2], hi0 * w[:, 0:1] + hi1 * w[:, 1:2]], axis=-1)
        x2 = x1_ref[...] + mod_ref[0][5:6] * ffn
        out = _rms(x2) * g_ref[...]

        @pl.when(i < n_ctx_tiles)
        def _():
            op_ref[...] = out

        @pl.when(i >= n_ctx_tiles)
        def _():
            os_ref[...] = out

        @pl.when(i == n_tiles - 1)
        def _():
            for k in range(TOP_K):
                _wait_rows(y_hbm, nxt.at[k], sem_nxt)

    @pl.when(i % 2 == 0)
    def _():
        step(yg0, yg1, sem.at[0], sem.at[1])

    @pl.when(i % 2 == 1)
    def _():
        step(yg1, yg0, sem.at[1], sem.at[0])


def _combine(slots, rows, y_buf, x1, wts, mod3, g_final, t_ctx):
    t = x1.shape[0]
    tm = TOKEN_TILE
    n_tiles = t // tm
    n_ctx = t_ctx // tm
    return pl.pallas_call(
        functools.partial(_combine_kernel, n_ctx_tiles=n_ctx, n_tiles=n_tiles),
        grid_spec=pltpu.PrefetchScalarGridSpec(
            num_scalar_prefetch=2,
            grid=(n_tiles,),
            in_specs=[
                pl.BlockSpec(memory_space=pl.ANY),
                pl.BlockSpec((tm, D_MODEL), lambda i, s, r: (i, 0)),
                pl.BlockSpec((tm, ROUTER_LANES), lambda i, s, r: (i, 0)),
                pl.BlockSpec((1, 6, D_MODEL), lambda i, s, r: (r[i], 0, 0)),
                pl.BlockSpec((1, D_MODEL), lambda i, s, r: (0, 0)),
            ],
            out_specs=(
                pl.BlockSpec((tm, D_MODEL), lambda i, s, r: (jnp.minimum(i, n_ctx - 1), 0)),
                pl.BlockSpec((tm, D_MODEL), lambda i, s, r: (jnp.maximum(i - n_ctx, 0), 0)),
            ),
            scratch_shapes=[pltpu.VMEM((TOP_K, tm * ROW_TILE, LANES), jnp.uint32),
                            pltpu.VMEM((TOP_K, tm * ROW_TILE, LANES), jnp.uint32),
                            pltpu.SemaphoreType.DMA((2,))],
        ),
        out_shape=(jax.ShapeDtypeStruct((t_ctx, D_MODEL), F32),
                   jax.ShapeDtypeStruct((t - t_ctx, D_MODEL), F32)),
        compiler_params=_params(1),
        name="combine",
    )(slots, rows, y_buf, x1, wts, mod3, g_final)


def _rope_tables(n_tokens):
    rows = n_tokens // GRID_W
    row = jnp.repeat(jnp.arange(rows, dtype=F32), GRID_W)
    col = jnp.tile(jnp.arange(GRID_W, dtype=F32), rows)
    half = ROPE_AXIS_DIM // 2
    inv_freq = ROPE_BASE ** (-jnp.arange(half, dtype=F32) / half)
    ar = row[:, None] * inv_freq
    ac = col[:, None] * inv_freq
    cos = jnp.concatenate([jnp.cos(ar), jnp.cos(ar), jnp.cos(ac), jnp.cos(ac)], axis=-1)
    sin = jnp.concatenate([-jnp.sin(ar), jnp.sin(ar), -jnp.sin(ac), jnp.sin(ac)], axis=-1)
    return cos, sin


def _dispatch_tables(expert_ids):
    t = expert_ids[0].shape[0]
    bm = EXPERT_ROWS
    flat_e = jnp.concatenate(expert_ids)
    onehot = (flat_e[:, None] == jnp.arange(N_EXPERTS, dtype=jnp.int32)[None, :]).astype(jnp.int32)
    running = jnp.cumsum(onehot, axis=0)
    rank = jnp.sum(running * onehot, axis=1) - 1
    counts = running[-1]
    padded = (counts + bm - 1) // bm * bm
    pad_end = jnp.cumsum(padded)
    pad_start = pad_end - padded
    dest = (jnp.sum(pad_start[None, :] * onehot, axis=1) + rank).astype(jnp.int32)
    n_blocks = (t * TOP_K + N_EXPERTS * (bm - 1) + bm - 1) // bm
    keys = jnp.concatenate([flat_e, jnp.full((bm,), N_EXPERTS, flat_e.dtype)])
    token_tile = jnp.arange(t, dtype=jnp.int32) * ROW_TILE
    tiles = jnp.concatenate([token_tile] * TOP_K + [jnp.zeros((bm,), jnp.int32)])
    _, src_tile = lax.sort((keys, tiles), num_keys=1, is_stable=True)
    n_used = (pad_end[-1] // bm).astype(jnp.int32)
    blk = jnp.arange(n_blocks, dtype=jnp.int32)
    blk = jnp.minimum(blk, n_used - 1)
    block_expert = jnp.sum((pad_end[None, :] <= (blk * bm)[:, None]).astype(jnp.int32), axis=1)
    block_expert = jnp.minimum(block_expert, N_EXPERTS - 1)
    e_ids = jnp.arange(N_EXPERTS, dtype=jnp.int32)
    later = (e_ids[None, :] > e_ids[:, None]) & (counts[None, :] > 0)
    next_of = jnp.min(jnp.where(later, e_ids[None, :], N_EXPERTS), axis=1)
    next_of = jnp.where(next_of == N_EXPERTS, -1, next_of)
    of_block = (block_expert[:, None] == e_ids[None, :]).astype(jnp.int32)
    next_expert = jnp.sum(next_of[None, :] * of_block, axis=1)
    start = jnp.cumsum(counts) - counts
    first_src = jnp.sum((start - pad_start)[None, :] * of_block, axis=1) + blk * bm
    return (block_expert.astype(jnp.int32), next_expert.astype(jnp.int32), n_used.reshape(1),
            first_src.astype(jnp.int32), src_tile, dest * ROW_TILE, n_blocks * bm)


def kernel(x_prompt, x_sample, cache_k, cache_v, c, c_ctx, w_mod, b_mod, norm_mix_g, norm_ffn_g, w_in,
           pool_w, pool_scale, attn_sink, out_norm_pool_g, out_norm_attn_g, w_out, router_group_w,
           router_group_b, router_expert_w, router_expert_b, w_gate, w_up, w_down, final_norm_g):
    depth = w_mod.shape[0]
    assert depth == 1, "single trunk layer"
    bp, lp, _ = x_prompt.shape
    bs, ls, _ = x_sample.shape
    tp, ts = bp * lp, bs * ls
    tm = TOKEN_TILE
    l = 0

    cond8 = jnp.zeros((8, D_MODEL), F32).at[:bs].set(c).at[bs].set(c_ctx)
    mod3 = _modulation(cond8, w_mod[l], b_mod[l]).reshape(8, 6, D_MODEL)
    rows_p = jnp.full((tp // tm,), bs, jnp.int32)
    rows_s = jnp.arange(ts // tm, dtype=jnp.int32) // (ls // tm)

    w_in_b = w_in[l].astype(BF16)
    w_out_b = w_out[l].astype(BF16)
    pool_w_b = pool_w[l].astype(BF16)
    g_mix = norm_mix_g[l].reshape(1, D_MODEL)
    g_ffn = norm_ffn_g[l].reshape(1, D_MODEL)
    g_pool = out_norm_pool_g[l].reshape(1, POOL_WIDTH)
    g_attn = out_norm_attn_g[l].reshape(1, ATTN_WIDTH)
    p_scale = pool_scale[l].reshape(1, POOL_WIDTH)
    sink_b = jnp.broadcast_to(attn_sink[l][:, None], (N_HEADS, LANES))
    g_final = final_norm_g.reshape(1, D_MODEL)

    rw = jnp.concatenate(
        [router_group_w[l], jnp.transpose(router_expert_w[l], (1, 0, 2)).reshape(D_MODEL, N_EXPERTS)], axis=1)
    rw = jnp.pad(rw, ((0, 0), (0, ROUTER_LANES - rw.shape[1])))
    rw_hi = rw.astype(BF16)
    rw_split = jnp.concatenate([rw_hi, (rw - rw_hi.astype(F32)).astype(BF16)], axis=1)
    rb = jnp.concatenate([router_group_b[l], router_expert_b[l].reshape(N_EXPERTS)])
    rb = jnp.pad(rb, (0, ROUTER_LANES - rb.shape[0])).reshape(1, ROUTER_LANES)

    xp = x_prompt.reshape(tp, D_MODEL)
    xs = x_sample.reshape(ts, D_MODEL)

    up_p, q_p, k_p, v_p = _in_projection(xp, rows_p, mod3, g_mix, w_in_b, None, F32)
    pool_p = _pool_mixer(up_p.reshape(bp, lp, POOL_WIDTH), pool_w_b, p_scale, g_pool).reshape(tp, POOL_WIDTH)
    attn_p = _context_attention(q_p, k_p, v_p, sink_b, g_attn, lp)
    up_s, q_s, k_s, v_s = _in_projection(xs, rows_s, mod3, g_mix, w_in_b, _rope_tables(ls), BF16)
    pool_s = _pool_mixer(up_s.reshape(bs, ls, POOL_WIDTH), pool_w_b, p_scale, g_pool).reshape(ts, POOL_WIDTH)
    attn_s = _latent_attention(
        q_s, k_s.reshape(bs, ls, KV_WIDTH), v_s.reshape(bs, ls, KV_WIDTH),
        cache_k[:, l].reshape(bs, -1, KV_WIDTH), cache_v[:, l].reshape(bs, -1, KV_WIDTH), sink_b, g_attn)

    rows_all = jnp.concatenate([rows_p, rows_s])
    x1_all, h2_all, ids_all, wts_all = _out_projection(
        (pool_p, attn_p, xp), (pool_s, attn_s, xs), rows_all, mod3, g_ffn, w_out_b, rw_split, rb)

    *expert_tables, slots, cap = _dispatch_tables([ids_all[:, k] for k in range(TOP_K)])
    y_buf = _experts(expert_tables, cap, h2_all, w_gate[l], w_up[l], w_down[l])

    y_p, y_s = _combine(slots, rows_all, y_buf, x1_all, wts_all, mod3, g_final, tp)

    kv_shape = (bp, 1, lp, N_KV_HEADS, HEAD_DIM)
    return (y_p.reshape(bp, lp, D_MODEL), y_s.reshape(bs, ls, D_MODEL),
            k_p.reshape(kv_shape), v_p.reshape(kv_shape))
```

```python
import functools

import jax
import jax.numpy as jnp
from jax import lax
from jax.experimental import pallas as pl
from jax.experimental.pallas import tpu as pltpu

F32 = jnp.float32
BF16 = jnp.bfloat16

D_MODEL = 2048
GRID_W = 64
HEAD_DIM = 128
ATTN_WIDTH = 1024
POOL_WIDTH = 1024
N_HEADS = 8
N_KV_HEADS = 2
Q_PER_KV = 4
KV_WIDTH = 256
IN_WIDTH = 2560
WINDOW = 128
POOL_WINDOWS = (2, 4, 8, 16)
POOL_GROUP_DIM = 256
ROPE_BASE = 10000.0
ROPE_AXIS_DIM = 64
N_EXPERT_GROUPS = 4
EXPERTS_PER_GROUP = 8
N_EXPERTS = 32
TOP_K = 2
D_EXPERT = 512
NORM_EPS = 1e-6
NEG_INF = -1e30
ATTN_SCALE = HEAD_DIM ** -0.5
LOG2E = 1.4426950408889634
Q_SCALE = ATTN_SCALE * LOG2E

LANES = 128
Q_BLOCKS_PER_STEP = 4
CTX_SEQS_PER_STEP = 2
VMEM_LIMIT = 48 * 1024 * 1024
PACKED = D_MODEL // 2
ROW_TILE = 8
assert PACKED == ROW_TILE * LANES, "a packed row must fill exactly one tile"
MOD_TILE = 1024
TOKEN_TILE = 256
IN_TILE = 512
POOL_TILES_PER_STEP = 2
POOL_HALO = 16
Q_BLOCK = 128
EXPERT_ROWS = 256
ROUTER_LANES = 128


def _params(n_grid):
    return pltpu.CompilerParams(
        dimension_semantics=("arbitrary",) * n_grid, vmem_limit_bytes=VMEM_LIMIT)


def _rms(x):
    return x * lax.rsqrt(jnp.mean(x * x, axis=-1, keepdims=True) + NORM_EPS)


def _dot(a, b):
    return jnp.dot(a, b, preferred_element_type=F32)


def _dot_nt(a, b):
    return lax.dot_general(a, b, (((1,), (1,)), ((), ())), preferred_element_type=F32)


def _pack_bf16_pairs(x):
    c = x.shape[1] // 2
    lo = lax.bitcast_convert_type(x[:, :c].astype(BF16).astype(F32), jnp.uint32)
    hi = lax.bitcast_convert_type(x[:, c:].astype(BF16).astype(F32), jnp.uint32)
    return hi | (lo >> 16)


def _store_row_tiles(ref, x):
    n = x.shape[0]
    for c in range(PACKED // LANES):
        ref[pl.ds(c, n, stride=ROW_TILE), :] = x[:, c * LANES:(c + 1) * LANES]


def _load_row_tiles(ref, n):
    return jnp.concatenate(
        [ref[pl.ds(c, n, stride=ROW_TILE), :] for c in range(PACKED // LANES)], axis=-1)


def _unpack_bf16_pairs(u):
    lo = lax.bitcast_convert_type(u << 16, F32)
    hi = lax.bitcast_convert_type(u & jnp.uint32(0xFFFF0000), F32)
    return lo, hi


def _mod_kernel(cond_ref, w_ref, b_ref, o_ref):
    c = cond_ref[...]
    s = c / (1.0 + jnp.exp(-c))
    o_ref[...] = _dot(s.astype(BF16), w_ref[...].astype(BF16)) + b_ref[...]


def _modulation(cond8, w_mod, b_mod):
    n = w_mod.shape[1]
    tn = MOD_TILE
    return pl.pallas_call(
        _mod_kernel,
        grid=(n // tn,),
        in_specs=[
            pl.BlockSpec((8, D_MODEL), lambda j: (0, 0)),
            pl.BlockSpec((D_MODEL, tn), lambda j: (0, j)),
            pl.BlockSpec((1, tn), lambda j: (0, j)),
        ],
        out_specs=pl.BlockSpec((8, tn), lambda j: (0, j)),
        out_shape=jax.ShapeDtypeStruct((8, n), F32),
        compiler_params=_params(1),
        name="modulation",
    )(cond8, w_mod, b_mod.reshape(1, n))


def _rope(x, cos, sin_signed):
    lane = lax.broadcasted_iota(jnp.int32, x.shape, 1)
    partner = jnp.where((lane & 63) < 32,
                        pltpu.roll(x, HEAD_DIM - 32, axis=1),
                        pltpu.roll(x, 32, axis=1))
    return x * cos + partner * sin_signed


def _inproj_kernel(rows_ref, x_ref, mod_ref, g_ref, w_ref, *rest, rope):
    del rows_ref
    if rope:
        cos_ref, sin_ref, up_ref, q_ref, k_ref, v_ref = rest
    else:
        up_ref, q_ref, k_ref, v_ref = rest
    m = mod_ref[0]
    h = _rms(x_ref[...]) * g_ref[...]
    hb = (h * (1.0 + m[1:2]) + m[0:1]).astype(BF16)
    up_ref[...] = _dot(hb, w_ref[:, 0:POOL_WIDTH]).astype(BF16)
    q = _dot(hb, w_ref[:, POOL_WIDTH:POOL_WIDTH + ATTN_WIDTH])
    k = _dot(hb, w_ref[:, POOL_WIDTH + ATTN_WIDTH:IN_WIDTH - KV_WIDTH])
    v = _dot(hb, w_ref[:, IN_WIDTH - KV_WIDTH:IN_WIDTH])
    if rope:
        cos = cos_ref[...]
        sin = sin_ref[...]
        for hd in range(N_HEADS):
            sl = slice(hd * HEAD_DIM, (hd + 1) * HEAD_DIM)
            q_ref[:, sl] = (_rope(q[:, sl], cos, sin) * Q_SCALE).astype(BF16)
        for hd in range(N_KV_HEADS):
            sl = slice(hd * HEAD_DIM, (hd + 1) * HEAD_DIM)
            k_ref[:, sl] = _rope(k[:, sl], cos, sin).astype(k_ref.dtype)
    else:
        q_ref[...] = (q * Q_SCALE).astype(BF16)
        k_ref[...] = k.astype(k_ref.dtype)
    v_ref[...] = v.astype(v_ref.dtype)


def _in_projection(x, rows, mod3, g_mix, w_in_b, rope_tabs, kv_dtype):
    t = x.shape[0]
    tm = IN_TILE
    per = IN_TILE // TOKEN_TILE
    rope = rope_tabs is not None
    in_specs = [
        pl.BlockSpec((tm, D_MODEL), lambda i, r: (i, 0)),
        pl.BlockSpec((1, 6, D_MODEL), lambda i, r: (r[i * per], 0, 0)),
        pl.BlockSpec((1, D_MODEL), lambda i, r: (0, 0)),
        pl.BlockSpec((D_MODEL, IN_WIDTH), lambda i, r: (0, 0), pipeline_mode=pl.Buffered(1)),
    ]
    args = [x, mod3, g_mix, w_in_b]
    if rope:
        seq_tiles = rope_tabs[0].shape[0] // tm
        for tab in rope_tabs:
            in_specs.append(pl.BlockSpec((tm, HEAD_DIM), lambda i, r: (i % seq_tiles, 0)))
            args.append(tab)
    out_shape = (
        jax.ShapeDtypeStruct((t, POOL_WIDTH), BF16),
        jax.ShapeDtypeStruct((t, ATTN_WIDTH), BF16),
        jax.ShapeDtypeStruct((t, KV_WIDTH), kv_dtype),
        jax.ShapeDtypeStruct((t, KV_WIDTH), kv_dtype),
    )
    out_specs = (
        pl.BlockSpec((tm, POOL_WIDTH), lambda i, r: (i, 0)),
        pl.BlockSpec((tm, ATTN_WIDTH), lambda i, r: (i, 0)),
        pl.BlockSpec((tm, KV_WIDTH), lambda i, r: (i, 0)),
        pl.BlockSpec((tm, KV_WIDTH), lambda i, r: (i, 0)),
    )
    return pl.pallas_call(
        functools.partial(_inproj_kernel, rope=rope),
        grid_spec=pltpu.PrefetchScalarGridSpec(
            num_scalar_prefetch=1, grid=(t // tm,), in_specs=in_specs, out_specs=out_specs),
        out_shape=out_shape,
        compiler_params=_params(1),
        name="in_projection_rope" if rope else "in_projection",
    )(rows, *args)


def _pool_kernel(u_ref, pw_ref, ps_ref, g_ref, o_ref, *, seq_len):
    tq = TOKEN_TILE
    win = min(seq_len, tq + 2 * POOL_HALO)
    for sub in range(o_ref.shape[1] // tq):
        t0 = pl.program_id(1) * o_ref.shape[1] + sub * tq
        src0 = pl.multiple_of(jnp.clip(t0 - POOL_HALO, 0, seq_len - win), POOL_HALO)
        u = u_ref[0, pl.ds(src0, win), :]
        t = t0 + lax.broadcasted_iota(jnp.int32, (tq, win), 0)
        j = src0 + lax.broadcasted_iota(jnp.int32, (tq, win), 1)
        tc = t0 + lax.broadcasted_iota(jnp.int32, (tq, 1), 0)
        ys = []
        for gi, w in enumerate(POOL_WINDOWS):
            lo = jnp.maximum(t - w // 2, 0)
            hi = jnp.minimum(t + (w - w // 2), seq_len)
            cnt = (hi - lo).astype(F32)
            a = jnp.where((j >= lo) & (j < hi), 1.0, 0.0) - jnp.where(j == t, cnt, 0.0)
            cnt_col = (jnp.minimum(tc + (w - w // 2), seq_len) - jnp.maximum(tc - w // 2, 0)).astype(F32)
            sl = slice(gi * POOL_GROUP_DIM, (gi + 1) * POOL_GROUP_DIM)
            d = _dot(a.astype(BF16), u[:, sl]) / cnt_col
            ys.append(_dot(d.astype(BF16), pw_ref[gi]))
        y = jnp.concatenate(ys, axis=-1) * ps_ref[...]
        o_ref[0, sub * tq:(sub + 1) * tq, :] = (_rms(y) * g_ref[...]).astype(BF16)


def _pool_mixer(up, pool_w_b, pool_scale, g_out_pool):
    b, seq_len, _ = up.shape
    tq = TOKEN_TILE * min(POOL_TILES_PER_STEP, seq_len // TOKEN_TILE)
    return pl.pallas_call(
        functools.partial(_pool_kernel, seq_len=seq_len),
        grid=(b, seq_len // tq),
        in_specs=[
            pl.BlockSpec((1, seq_len, POOL_WIDTH), lambda bi, i: (bi, 0, 0)),
            pl.BlockSpec((len(POOL_WINDOWS), POOL_GROUP_DIM, POOL_GROUP_DIM), lambda bi, i: (0, 0, 0)),
            pl.BlockSpec((1, POOL_WIDTH), lambda bi, i: (0, 0)),
            pl.BlockSpec((1, POOL_WIDTH), lambda bi, i: (0, 0)),
        ],
        out_specs=pl.BlockSpec((1, tq, POOL_WIDTH), lambda bi, i: (bi, i, 0)),
        out_shape=jax.ShapeDtypeStruct((b, seq_len, POOL_WIDTH), BF16),
        compiler_params=_params(2),
        name=f"pool_mixer_{seq_len}",
    )(up, pool_w_b, pool_scale, g_out_pool)


def _stack_heads(q, kh, rows):
    return jnp.concatenate(
        [q[:, (kh * Q_PER_KV + g) * HEAD_DIM:(kh * Q_PER_KV + g + 1) * HEAD_DIM]
         for g in range(Q_PER_KV)], axis=0)


def _sink_column(sink_ref, kh, rows):
    return jnp.concatenate(
        [jnp.broadcast_to(sink_ref[kh * Q_PER_KV + g:kh * Q_PER_KV + g + 1, 0:1] * LOG2E, (rows, 1))
         for g in range(Q_PER_KV)], axis=0)


def _attend(q4, key_sets, value_sets, sk, band=None):
    s = [_dot_nt(q4, k) for k in key_sets]
    if band is not None:
        s[0] = jnp.where(band, s[0], NEG_INF)
    m = sk
    for si in s:
        m = jnp.maximum(m, jnp.max(si, axis=-1, keepdims=True))
    acc = None
    for si, v in zip(s, value_sets):
        v1 = jnp.concatenate([v, jnp.ones_like(v)], axis=-1)
        part = _dot(jnp.exp2(si - m).astype(BF16), v1)
        acc = part if acc is None else acc + part
    den = acc[:, HEAD_DIM:] + jnp.exp2(sk - m)
    return acc[:, :HEAD_DIM] / den


def _ctx_attn_kernel(q_ref, k_ref, v_ref, sink_ref, g_ref, o_ref, *, seq_len):
    rows = seq_len
    for j in range(q_ref.shape[0] // rows):
        rs = slice(j * rows, (j + 1) * rows)
        q = q_ref[rs, :]
        heads = [None] * N_HEADS
        for kh in range(N_KV_HEADS):
            sl = slice(kh * HEAD_DIM, (kh + 1) * HEAD_DIM)
            o = _attend(_stack_heads(q, kh, rows), [k_ref[rs, sl].astype(BF16)],
                        [v_ref[rs, sl].astype(BF16)], _sink_column(sink_ref, kh, rows))
            for g in range(Q_PER_KV):
                heads[kh * Q_PER_KV + g] = o[g * rows:(g + 1) * rows]
        y = jnp.concatenate(heads, axis=-1)
        o_ref[rs, :] = (_rms(y) * g_ref[...]).astype(BF16)


def _context_attention(q, k, v, sink_b, g_out_attn, seq_len):
    t = q.shape[0]
    blk = seq_len * CTX_SEQS_PER_STEP
    return pl.pallas_call(
        functools.partial(_ctx_attn_kernel, seq_len=seq_len),
        grid=(t // blk,),
        in_specs=[
            pl.BlockSpec((blk, ATTN_WIDTH), lambda b: (b, 0)),
            pl.BlockSpec((blk, KV_WIDTH), lambda b: (b, 0)),
            pl.BlockSpec((blk, KV_WIDTH), lambda b: (b, 0)),
            pl.BlockSpec((N_HEADS, LANES), lambda b: (0, 0)),
            pl.BlockSpec((1, ATTN_WIDTH), lambda b: (0, 0)),
        ],
        out_specs=pl.BlockSpec((blk, ATTN_WIDTH), lambda b: (b, 0)),
        out_shape=jax.ShapeDtypeStruct((t, ATTN_WIDTH), BF16),
        compiler_params=_params(1),
        name="context_attention",
    )(q, k, v, sink_b, g_out_attn)


def _lat_attn_kernel(q_ref, k_ref, v_ref, ck_ref, cv_ref, sink_ref, g_ref, o_ref, *, seq_len):
    rows = Q_BLOCK
    span = 3 * Q_BLOCK
    shape = (Q_PER_KV * rows, span)
    row_in_block = lax.broadcasted_iota(jnp.int32, shape, 0) & (rows - 1)
    col = lax.broadcasted_iota(jnp.int32, shape, 1)
    for j in range(Q_BLOCKS_PER_STEP):
        q0 = (pl.program_id(1) * Q_BLOCKS_PER_STEP + j) * rows
        start = pl.multiple_of(jnp.clip(q0 - Q_BLOCK, 0, seq_len - span), Q_BLOCK)
        q = q_ref[j * rows:(j + 1) * rows, :]
        kl = k_ref[0, pl.ds(start, span), :]
        vl = v_ref[0, pl.ds(start, span), :]
        band = jnp.abs(row_in_block - col + (q0 - start)) <= WINDOW
        heads = [None] * N_HEADS
        for kh in range(N_KV_HEADS):
            sl = slice(kh * HEAD_DIM, (kh + 1) * HEAD_DIM)
            o = _attend(_stack_heads(q, kh, rows),
                        [kl[:, sl], ck_ref[0, :, sl].astype(BF16)],
                        [vl[:, sl], cv_ref[0, :, sl].astype(BF16)],
                        _sink_column(sink_ref, kh, rows), band)
            for g in range(Q_PER_KV):
                heads[kh * Q_PER_KV + g] = o[g * rows:(g + 1) * rows]
        y = jnp.concatenate(heads, axis=-1)
        o_ref[j * rows:(j + 1) * rows, :] = (_rms(y) * g_ref[...]).astype(BF16)


def _latent_attention(q, k, v, cache_k, cache_v, sink_b, g_out_attn):
    b, seq_len, _ = k.shape
    q_rows = Q_BLOCK * Q_BLOCKS_PER_STEP
    nq = seq_len // q_rows
    past = cache_k.shape[1]
    return pl.pallas_call(
        functools.partial(_lat_attn_kernel, seq_len=seq_len),
        grid=(b, nq),
        in_specs=[
            pl.BlockSpec((q_rows, ATTN_WIDTH), lambda bi, n: (bi * nq + n, 0)),
            pl.BlockSpec((1, seq_len, KV_WIDTH), lambda bi, n: (bi, 0, 0)),
            pl.BlockSpec((1, seq_len, KV_WIDTH), lambda bi, n: (bi, 0, 0)),
            pl.BlockSpec((1, past, KV_WIDTH), lambda bi, n: (bi, 0, 0)),
            pl.BlockSpec((1, past, KV_WIDTH), lambda bi, n: (bi, 0, 0)),
            pl.BlockSpec((N_HEADS, LANES), lambda bi, n: (0, 0)),
            pl.BlockSpec((1, ATTN_WIDTH), lambda bi, n: (0, 0)),
        ],
        out_specs=pl.BlockSpec((q_rows, ATTN_WIDTH), lambda bi, n: (bi * nq + n, 0)),
        out_shape=jax.ShapeDtypeStruct((b * seq_len, ATTN_WIDTH), BF16),
        compiler_params=_params(2),
        name="latent_attention",
    )(q, k, v, cache_k, cache_v, sink_b, g_out_attn)


def _route(logits):
    lane = lax.broadcasted_iota(jnp.int32, logits.shape, 1).astype(F32)
    neg = -jnp.inf

    def first_argmax(x):
        mx = jnp.max(x, axis=-1, keepdims=True)
        return mx, jnp.min(jnp.where(x == mx, lane, float(ROUTER_LANES)), axis=-1, keepdims=True)

    gl = jnp.where(lane < N_EXPERT_GROUPS, logits, neg)
    gmax, g_idx = first_argmax(gl)
    p_g = 1.0 / jnp.sum(jnp.exp(gl - gmax), axis=-1, keepdims=True)
    base = N_EXPERT_GROUPS + EXPERTS_PER_GROUP * g_idx
    el = jnp.where((lane >= base) & (lane < base + EXPERTS_PER_GROUP), logits, neg)
    v1, i1 = first_argmax(el)
    v2, i2 = first_argmax(jnp.where(lane == i1, neg, el))
    e2 = jnp.exp(v2 - v1)
    w1 = p_g / (1.0 + e2)
    w2 = p_g * e2 / (1.0 + e2)
    ids = jnp.where(lane == 0.0, i1 - N_EXPERT_GROUPS, jnp.where(lane == 1.0, i2 - N_EXPERT_GROUPS, 0.0))
    wts = jnp.where(lane == 0.0, w1, jnp.where(lane == 1.0, w2, 0.0))
    return ids, wts


def _outproj_kernel(rows_ref, pool_p, attn_p, x_p, pool_s, attn_s, x_s, mod_ref, g_ref, w_ref, rw_ref,
                    rb_ref, x1_ref, h2_ref, ids_ref, wts_ref, mix0, mix1, *, n_ctx_tiles, n_tiles):
    del rows_ref
    i = pl.program_id(0)

    def project(pool_ref, attn_ref, mix_ref):
        mix_ref[...] = (_dot(pool_ref[...], w_ref[0:POOL_WIDTH, :])
                        + _dot(attn_ref[...], w_ref[POOL_WIDTH:POOL_WIDTH + ATTN_WIDTH, :]))

    def tail(x_ref, mix_ref):
        m = mod_ref[0]
        x1 = x_ref[...] + m[2:3] * mix_ref[...]
        x1_ref[...] = x1
        h2 = _rms(x1) * g_ref[...] * (1.0 + m[4:5]) + m[3:4]
        _store_row_tiles(h2_ref, _pack_bf16_pairs(h2))
        hi = h2.astype(BF16)
        lo = (h2 - hi.astype(F32)).astype(BF16)
        both = _dot(hi, rw_ref[...])
        logits = (both[:, :ROUTER_LANES] + both[:, ROUTER_LANES:] + _dot(lo, rw_ref[:, :ROUTER_LANES])
                  + rb_ref[...])
        ids, wts = _route(logits)
        ids_ref[...] = jnp.transpose(ids)[0:ROW_TILE, :].astype(jnp.int32)
        wts_ref[...] = wts

    ctx_in = (pool_p, attn_p)
    lat_in = (pool_s, attn_s)
    for parity, (cur, prev) in enumerate(((mix0, mix1), (mix1, mix0))):
        mine = (i % 2) == parity

        @pl.when(mine & (i == 0))
        def _(cur=cur):
            project(*ctx_in, cur)

        @pl.when(mine & (i >= 1) & (i < n_ctx_tiles))
        def _(cur=cur, prev=prev):
            tail(x_p, prev)
            project(*ctx_in, cur)

        @pl.when(mine & (i == n_ctx_tiles))
        def _(cur=cur, prev=prev):
            tail(x_p, prev)
            project(*lat_in, cur)

        @pl.when(mine & (i > n_ctx_tiles) & (i < n_tiles))
        def _(cur=cur, prev=prev):
            tail(x_s, prev)
            project(*lat_in, cur)

        @pl.when(mine & (i == n_tiles))
        def _(prev=prev):
            tail(x_s, prev)


def _out_projection(ctx, lat, rows, mod3, g_ffn, w_out_b, rw_split, rb):
    tm = TOKEN_TILE
    n_ctx = ctx[2].shape[0] // tm
    n_lat = lat[2].shape[0] // tm
    n_tiles = n_ctx + n_lat
    t_all = n_tiles * tm
    assert n_ctx >= 1 and n_lat >= 1

    def ctx_map(i, r):
        return (jnp.minimum(i, n_ctx - 1), 0)

    def lat_map(i, r):
        return (jnp.clip(i - n_ctx, 0, n_lat - 1), 0)

    def ctx_prev_map(i, r):
        return (jnp.clip(i - 1, 0, n_ctx - 1), 0)

    def lat_prev_map(i, r):
        return (jnp.clip(i - 1 - n_ctx, 0, n_lat - 1), 0)

    def prev_tile(i):
        return jnp.maximum(i - 1, 0)

    in_specs = [
        pl.BlockSpec((tm, POOL_WIDTH), ctx_map),
        pl.BlockSpec((tm, ATTN_WIDTH), ctx_map),
        pl.BlockSpec((tm, D_MODEL), ctx_prev_map),
        pl.BlockSpec((tm, POOL_WIDTH), lat_map),
        pl.BlockSpec((tm, ATTN_WIDTH), lat_map),
        pl.BlockSpec((tm, D_MODEL), lat_prev_map),
        pl.BlockSpec((1, 6, D_MODEL), lambda i, r: (r[prev_tile(i)], 0, 0)),
        pl.BlockSpec((1, D_MODEL), lambda i, r: (0, 0)),
        pl.BlockSpec((D_MODEL, D_MODEL), lambda i, r: (0, 0), pipeline_mode=pl.Buffered(1)),
        pl.BlockSpec((D_MODEL, 2 * ROUTER_LANES), lambda i, r: (0, 0)),
        pl.BlockSpec((1, ROUTER_LANES), lambda i, r: (0, 0)),
    ]
    out_shape = (
        jax.ShapeDtypeStruct((t_all, D_MODEL), F32),
        jax.ShapeDtypeStruct((t_all * ROW_TILE, LANES), jnp.uint32),
        jax.ShapeDtypeStruct((ROW_TILE, t_all), jnp.int32),
        jax.ShapeDtypeStruct((t_all, ROUTER_LANES), F32),
    )
    out_specs = (
        pl.BlockSpec((tm, D_MODEL), lambda i, r: (prev_tile(i), 0)),
        pl.BlockSpec((tm * ROW_TILE, LANES), lambda i, r: (prev_tile(i), 0)),
        pl.BlockSpec((ROW_TILE, tm), lambda i, r: (0, prev_tile(i))),
        pl.BlockSpec((tm, ROUTER_LANES), lambda i, r: (prev_tile(i), 0)),
    )
    return pl.pallas_call(
        functools.partial(_outproj_kernel, n_ctx_tiles=n_ctx, n_tiles=n_tiles),
        grid_spec=pltpu.PrefetchScalarGridSpec(
            num_scalar_prefetch=1, grid=(n_tiles + 1,), in_specs=in_specs, out_specs=out_specs,
            scratch_shapes=[pltpu.VMEM((tm, D_MODEL), F32), pltpu.VMEM((tm, D_MODEL), F32)]),
        out_shape=out_shape,
        compiler_params=_params(1),
        name="out_projection",
    )(rows, *ctx, *lat, mod3, g_ffn, w_out_b, rw_split, rb)


def _gather_rows(idx_ref, base, n_rows, src_hbm, dst, sem, *, unrolled, both_queues=False):
    def one(r, priority):
        src_row = pl.multiple_of(idx_ref[base + r], ROW_TILE)
        pltpu.make_async_copy(
            src_hbm.at[pl.ds(src_row, ROW_TILE)], dst.at[pl.ds(r * ROW_TILE, ROW_TILE)],
            sem).start(priority=priority)

    if unrolled:
        for r in range(n_rows):
            one(r, r % 2 if both_queues else 0)
    else:
        def body(r, carry):
            one(r, 0)
            return carry
        lax.fori_loop(0, n_rows, body, 0, unroll=8)


def _wait_rows(src_hbm, dst, sem):
    pltpu.make_async_copy(src_hbm.at[pl.ds(0, dst.shape[0])], dst, sem).wait()


def _moe_kernel(be_ref, ne_ref, nu_ref, first_ref, tok_ref, h_hbm, wg_hbm, wu_hbm, wd_hbm, y_ref,
                xg0, xg1, xg2, sg, su, sd, wgb, wub, wdb, sem, wsem):
    b = pl.program_id(0)
    n_used = nu_ref[0]
    last = n_used - 1
    active = b < n_used
    bufs = (xg0, xg1, xg2)
    n_buf = len(bufs)

    def weight_copies(e):
        return (pltpu.make_async_copy(wg_hbm.at[e], sg, wsem.at[0]),
                pltpu.make_async_copy(wu_hbm.at[e], su, wsem.at[1]),
                pltpu.make_async_copy(wd_hbm.at[e], sd, wsem.at[2]))

    @pl.when(b == 0)
    def _():
        for cp in weight_copies(be_ref[0]):
            cp.start(priority=1)
        for i in range(n_buf - 1):
            _gather_rows(tok_ref, first_ref[jnp.minimum(i, last)], EXPERT_ROWS, h_hbm, bufs[i], sem.at[i],
                         unrolled=False)

    new_expert = (b == 0) | (be_ref[b] != be_ref[jnp.maximum(b - 1, 0)])

    @pl.when(active & new_expert)
    def _():
        for cp in weight_copies(be_ref[b]):
            cp.wait()
        wgb[...] = sg[...].astype(BF16)
        wub[...] = su[...].astype(BF16)
        wdb[...] = sd[...].astype(BF16)

        @pl.when(ne_ref[b] >= 0)
        def _():
            for cp in weight_copies(ne_ref[b]):
                cp.start(priority=1)

    def step(slot):
        cur = bufs[slot]
        ahead = bufs[(slot + n_buf - 1) % n_buf]
        sem_ahead = sem.at[(slot + n_buf - 1) % n_buf]
        _wait_rows(h_hbm, cur, sem.at[slot])
        _gather_rows(tok_ref, first_ref[jnp.minimum(b + n_buf - 1, last)], EXPERT_ROWS, h_hbm, ahead,
                     sem_ahead, unrolled=True, both_queues=True)
        lo, hi = _unpack_bf16_pairs(_load_row_tiles(cur, EXPERT_ROWS))
        lo = lo.astype(BF16)
        hi = hi.astype(BF16)
        a = _dot(lo, wgb[0:PACKED, :]) + _dot(hi, wgb[PACKED:D_MODEL, :])
        u = _dot(lo, wub[0:PACKED, :]) + _dot(hi, wub[PACKED:D_MODEL, :])
        act = (a / (1.0 + jnp.exp(-a)) * u).astype(BF16)
        _store_row_tiles(y_ref, _pack_bf16_pairs(_dot(act, wdb[...])))

        @pl.when(b == last)
        def _():
            for i in range(1, n_buf):
                _wait_rows(h_hbm, bufs[(slot + i) % n_buf], sem.at[(slot + i) % n_buf])

    for slot in range(n_buf):
        @pl.when(active & (b % n_buf == slot))
        def _(slot=slot):
            step(slot)

    @pl.when(b >= n_used)
    def _():
        y_ref[...] = jnp.zeros_like(y_ref)


def _experts(tables, cap, h2_packed, w_gate, w_up, w_down):
    bm = EXPERT_ROWS
    xg = pltpu.VMEM((bm * ROW_TILE, LANES), jnp.uint32)
    return pl.pallas_call(
        _moe_kernel,
        grid_spec=pltpu.PrefetchScalarGridSpec(
            num_scalar_prefetch=len(tables),
            grid=(cap // bm,),
            in_specs=[pl.BlockSpec(memory_space=pl.ANY)] * 4,
            out_specs=pl.BlockSpec((bm * ROW_TILE, LANES), lambda b, *_: (b, 0)),
            scratch_shapes=[
                xg, xg, xg,
                pltpu.VMEM((D_MODEL, D_EXPERT), F32), pltpu.VMEM((D_MODEL, D_EXPERT), F32),
                pltpu.VMEM((D_EXPERT, D_MODEL), F32),
                pltpu.VMEM((D_MODEL, D_EXPERT), BF16), pltpu.VMEM((D_MODEL, D_EXPERT), BF16),
                pltpu.VMEM((D_EXPERT, D_MODEL), BF16),
                pltpu.SemaphoreType.DMA((3,)), pltpu.SemaphoreType.DMA((3,)),
            ],
        ),
        out_shape=jax.ShapeDtypeStruct((cap * ROW_TILE, LANES), jnp.uint32),
        compiler_params=_params(1),
        name="experts",
    )(*tables, h2_packed, w_gate, w_up, w_down)


def _combine_kernel(slot_ref, rows_ref, y_hbm, x1_ref, wts_ref, mod_ref, g_ref, op_ref, os_ref, yg0, yg1, sem,
                    *, n_ctx_tiles, n_tiles):
    del rows_ref
    tm = TOKEN_TILE
    i = pl.program_id(0)
    n_slots = slot_ref.shape[0] // TOP_K

    def start(tile, buf, s, unrolled):
        for k in range(TOP_K):
            _gather_rows(slot_ref, k * n_slots + tile * tm, tm, y_hbm, buf.at[k], s, unrolled=unrolled,
                         both_queues=True)

    @pl.when(i == 0)
    def _():
        start(0, yg0, sem.at[0], False)

    def step(cur, nxt, sem_cur, sem_nxt):
        for k in range(TOP_K):
            _wait_rows(y_hbm, cur.at[k], sem_cur)
        start(jnp.minimum(i + 1, n_tiles - 1), nxt, sem_nxt, True)
        w = wts_ref[...]
        lo0, hi0 = _unpack_bf16_pairs(_load_row_tiles(cur.at[0], tm))
        lo1, hi1 = _unpack_bf16_pairs(_load_row_tiles(cur.at[1], tm))
        ffn = jnp.concatenate([lo0 * w[:, 0:1] + lo1 * w[:, 1:2], hi0 * w[:, 0:1] + hi1 * w[:, 1:2]], axis=-1)
        x2 = x1_ref[...] + mod_ref[0][5:6] * ffn
        out = _rms(x2) * g_ref[...]

        @pl.when(i < n_ctx_tiles)
        def _():
            op_ref[...] = out

        @pl.when(i >= n_ctx_tiles)
        def _():
            os_ref[...] = out

        @pl.when(i == n_tiles - 1)
        def _():
            for k in range(TOP_K):
                _wait_rows(y_hbm, nxt.at[k], sem_nxt)

    @pl.when(i % 2 == 0)
    def _():
        step(yg0, yg1, sem.at[0], sem.at[1])

    @pl.when(i % 2 == 1)
    def _():
        step(yg1, yg0, sem.at[1], sem.at[0])


def _combine(slots, rows, y_buf, x1, wts, mod3, g_final, t_ctx):
    t = x1.shape[0]
    tm = TOKEN_TILE
    n_tiles = t // tm
    n_ctx = t_ctx // tm
    return pl.pallas_call(
        functools.partial(_combine_kernel, n_ctx_tiles=n_ctx, n_tiles=n_tiles),
        grid_spec=pltpu.PrefetchScalarGridSpec(
            num_scalar_prefetch=2,
            grid=(n_tiles,),
            in_specs=[
                pl.BlockSpec(memory_space=pl.ANY),
                pl.BlockSpec((tm, D_MODEL), lambda i, s, r: (i, 0)),
                pl.BlockSpec((tm, ROUTER_LANES), lambda i, s, r: (i, 0)),
                pl.BlockSpec((1, 6, D_MODEL), lambda i, s, r: (r[i], 0, 0)),
                pl.BlockSpec((1, D_MODEL), lambda i, s, r: (0, 0)),
            ],
            out_specs=(
                pl.BlockSpec((tm, D_MODEL), lambda i, s, r: (jnp.minimum(i, n_ctx - 1), 0)),
                pl.BlockSpec((tm, D_MODEL), lambda i, s, r: (jnp.maximum(i - n_ctx, 0), 0)),
            ),
            scratch_shapes=[pltpu.VMEM((TOP_K, tm * ROW_TILE, LANES), jnp.uint32),
                            pltpu.VMEM((TOP_K, tm * ROW_TILE, LANES), jnp.uint32),
                            pltpu.SemaphoreType.DMA((2,))],
        ),
        out_shape=(jax.ShapeDtypeStruct((t_ctx, D_MODEL), F32),
                   jax.ShapeDtypeStruct((t - t_ctx, D_MODEL), F32)),
        compiler_params=_params(1),
        name="combine",
    )(slots, rows, y_buf, x1, wts, mod3, g_final)


def _rope_tables(n_tokens):
    rows = n_tokens // GRID_W
    row = jnp.repeat(jnp.arange(rows, dtype=F32), GRID_W)
    col = jnp.tile(jnp.arange(GRID_W, dtype=F32), rows)
    half = ROPE_AXIS_DIM // 2
    inv_freq = ROPE_BASE ** (-jnp.arange(half, dtype=F32) / half)
    ar = row[:, None] * inv_freq
    ac = col[:, None] * inv_freq
    cos = jnp.concatenate([jnp.cos(ar), jnp.cos(ar), jnp.cos(ac), jnp.cos(ac)], axis=-1)
    sin = jnp.concatenate([-jnp.sin(ar), jnp.sin(ar), -jnp.sin(ac), jnp.sin(ac)], axis=-1)
    return cos, sin


def _dispatch_tables(expert_ids):
    t = expert_ids[0].shape[0]
    bm = EXPERT_ROWS
    flat_e = jnp.concatenate(expert_ids)
    onehot = (flat_e[:, None] == jnp.arange(N_EXPERTS, dtype=jnp.int32)[None, :]).astype(jnp.int32)
    running = jnp.cumsum(onehot, axis=0)
    rank = jnp.sum(running * onehot, axis=1) - 1
    counts = running[-1]
    padded = (counts + bm - 1) // bm * bm
    pad_end = jnp.cumsum(padded)
    pad_start = pad_end - padded
    dest = (jnp.sum(pad_start[None, :] * onehot, axis=1) + rank).astype(jnp.int32)
    n_blocks = (t * TOP_K + N_EXPERTS * (bm - 1) + bm - 1) // bm
    keys = jnp.concatenate([flat_e, jnp.full((bm,), N_EXPERTS, flat_e.dtype)])
    token_tile = jnp.arange(t, dtype=jnp.int32) * ROW_TILE
    tiles = jnp.concatenate([token_tile] * TOP_K + [jnp.zeros((bm,), jnp.int32)])
    _, src_tile = lax.sort((keys, tiles), num_keys=1, is_stable=True)
    n_used = (pad_end[-1] // bm).astype(jnp.int32)
    blk = jnp.arange(n_blocks, dtype=jnp.int32)
    blk = jnp.minimum(blk, n_used - 1)
    block_expert = jnp.sum((pad_end[None, :] <= (blk * bm)[:, None]).astype(jnp.int32), axis=1)
    block_expert = jnp.minimum(block_expert, N_EXPERTS - 1)
    e_ids = jnp.arange(N_EXPERTS, dtype=jnp.int32)
    later = (e_ids[None, :] > e_ids[:, None]) & (counts[None, :] > 0)
    next_of = jnp.min(jnp.where(later, e_ids[None, :], N_EXPERTS), axis=1)
    next_of = jnp.where(next_of == N_EXPERTS, -1, next_of)
    of_block = (block_expert[:, None] == e_ids[None, :]).astype(jnp.int32)
    next_expert = jnp.sum(next_of[None, :] * of_block, axis=1)
    start = jnp.cumsum(counts) - counts
    first_src = jnp.sum((start - pad_start)[None, :] * of_block, axis=1) + blk * bm
    return (block_expert.astype(jnp.int32), next_expert.astype(jnp.int32), n_used.reshape(1),
            first_src.astype(jnp.int32), src_tile, dest * ROW_TILE, n_blocks * bm)


def kernel(x_prompt, x_sample, cache_k, cache_v, c, c_ctx, w_mod, b_mod, norm_mix_g, norm_ffn_g, w_in,
           pool_w, pool_scale, attn_sink, out_norm_pool_g, out_norm_attn_g, w_out, router_group_w,
           router_group_b, router_expert_w, router_expert_b, w_gate, w_up, w_down, final_norm_g):
    depth = w_mod.shape[0]
    assert depth == 1, "single trunk layer"
    bp, lp, _ = x_prompt.shape
    bs, ls, _ = x_sample.shape
    tp, ts = bp * lp, bs * ls
    tm = TOKEN_TILE
    l = 0

    cond8 = jnp.zeros((8, D_MODEL), F32).at[:bs].set(c).at[bs].set(c_ctx)
    mod3 = _modulation(cond8, w_mod[l], b_mod[l]).reshape(8, 6, D_MODEL)
    rows_p = jnp.full((tp // tm,), bs, jnp.int32)
    rows_s = jnp.arange(ts // tm, dtype=jnp.int32) // (ls // tm)

    w_in_b = w_in[l].astype(BF16)
    w_out_b = w_out[l].astype(BF16)
    pool_w_b = pool_w[l].astype(BF16)
    g_mix = norm_mix_g[l].reshape(1, D_MODEL)
    g_ffn = norm_ffn_g[l].reshape(1, D_MODEL)
    g_pool = out_norm_pool_g[l].reshape(1, POOL_WIDTH)
    g_attn = out_norm_attn_g[l].reshape(1, ATTN_WIDTH)
    p_scale = pool_scale[l].reshape(1, POOL_WIDTH)
    sink_b = jnp.broadcast_to(attn_sink[l][:, None], (N_HEADS, LANES))
    g_final = final_norm_g.reshape(1, D_MODEL)

    rw = jnp.concatenate(
        [router_group_w[l], jnp.transpose(router_expert_w[l], (1, 0, 2)).reshape(D_MODEL, N_EXPERTS)], axis=1)
    rw = jnp.pad(rw, ((0, 0), (0, ROUTER_LANES - rw.shape[1])))
    rw_hi = rw.astype(BF16)
    rw_split = jnp.concatenate([rw_hi, (rw - rw_hi.astype(F32)).astype(BF16)], axis=1)
    rb = jnp.concatenate([router_group_b[l], router_expert_b[l].reshape(N_EXPERTS)])
    rb = jnp.pad(rb, (0, ROUTER_LANES - rb.shape[0])).reshape(1, ROUTER_LANES)

    xp = x_prompt.reshape(tp, D_MODEL)
    xs = x_sample.reshape(ts, D_MODEL)

    up_p, q_p, k_p, v_p = _in_projection(xp, rows_p, mod3, g_mix, w_in_b, None, F32)
    pool_p = _pool_mixer(up_p.reshape(bp, lp, POOL_WIDTH), pool_w_b, p_scale, g_pool).reshape(tp, POOL_WIDTH)
    attn_p = _context_attention(q_p, k_p, v_p, sink_b, g_attn, lp)
    up_s, q_s, k_s, v_s = _in_projection(xs, rows_s, mod3, g_mix, w_in_b, _rope_tables(ls), BF16)
    pool_s = _pool_mixer(up_s.reshape(bs, ls, POOL_WIDTH), pool_w_b, p_scale, g_pool).reshape(ts, POOL_WIDTH)
    attn_s = _latent_attention(
        q_s, k_s.reshape(bs, ls, KV_WIDTH), v_s.reshape(bs, ls, KV_WIDTH),
        cache_k[:, l].reshape(bs, -1, KV_WIDTH), cache_v[:, l].reshape(bs, -1, KV_WIDTH), sink_b, g_attn)

    rows_all = jnp.concatenate([rows_p, rows_s])
    x1_all, h2_all, ids_all, wts_all = _out_projection(
        (pool_p, attn_p, xp), (pool_s, attn_s, xs), rows_all, mod3, g_ffn, w_out_b, rw_split, rb)

    *expert_tables, slots, cap = _dispatch_tables([ids_all[k] for k in range(TOP_K)])
    y_buf = _experts(expert_tables, cap, h2_all, w_gate[l], w_up[l], w_down[l])

    y_p, y_s = _combine(slots, rows_all, y_buf, x1_all, wts_all, mod3, g_final, tp)

    kv_shape = (bp, 1, lp, N_KV_HEADS, HEAD_DIM)
    return (y_p.reshape(bp, lp, D_MODEL), y_s.reshape(bs, ls, D_MODEL),
            k_p.reshape(kv_shape), v_p.reshape(kv_shape))
```

```python
import functools

import jax
import jax.numpy as jnp
from jax import lax
from jax.experimental import pallas as pl
from jax.experimental.pallas import tpu as pltpu

F32 = jnp.float32
BF16 = jnp.bfloat16

D_MODEL = 2048
GRID_W = 64
HEAD_DIM = 128
ATTN_WIDTH = 1024
POOL_WIDTH = 1024
N_HEADS = 8
N_KV_HEADS = 2
Q_PER_KV = 4
KV_WIDTH = 256
IN_WIDTH = 2560
WINDOW = 128
POOL_WINDOWS = (2, 4, 8, 16)
POOL_GROUP_DIM = 256
ROPE_BASE = 10000.0
ROPE_AXIS_DIM = 64
N_EXPERT_GROUPS = 4
EXPERTS_PER_GROUP = 8
N_EXPERTS = 32
TOP_K = 2
D_EXPERT = 512
NORM_EPS = 1e-6
NEG_INF = -1e30
ATTN_SCALE = HEAD_DIM ** -0.5
LOG2E = 1.4426950408889634
Q_SCALE = ATTN_SCALE * LOG2E

LANES = 128
Q_BLOCKS_PER_STEP = 4
CTX_SEQS_PER_STEP = 2
VMEM_LIMIT = 48 * 1024 * 1024
PACKED = D_MODEL // 2
ROW_TILE = 8
assert PACKED == ROW_TILE * LANES, "a packed row must fill exactly one tile"
MOD_TILE = 1024
TOKEN_TILE = 256
IN_TILE = 512
POOL_TILES_PER_STEP = 2
POOL_HALO = 16
Q_BLOCK = 128
EXPERT_ROWS = 256
ROUTER_LANES = 128


def _params(n_grid):
    return pltpu.CompilerParams(
        dimension_semantics=("arbitrary",) * n_grid, vmem_limit_bytes=VMEM_LIMIT)


def _rms(x):
    return x * lax.rsqrt(jnp.mean(x * x, axis=-1, keepdims=True) + NORM_EPS)


def _dot(a, b):
    return jnp.dot(a, b, preferred_element_type=F32)


def _dot_nt(a, b):
    return lax.dot_general(a, b, (((1,), (1,)), ((), ())), preferred_element_type=F32)


def _pack_bf16_pairs(x):
    c = x.shape[1] // 2
    lo = lax.bitcast_convert_type(x[:, :c].astype(BF16).astype(F32), jnp.uint32)
    hi = lax.bitcast_convert_type(x[:, c:].astype(BF16).astype(F32), jnp.uint32)
    return hi | (lo >> 16)


def _store_row_tiles(ref, x):
    n = x.shape[0]
    for c in range(PACKED // LANES):
        ref[pl.ds(c, n, stride=ROW_TILE), :] = x[:, c * LANES:(c + 1) * LANES]


def _load_row_tiles(ref, n):
    return jnp.concatenate(
        [ref[pl.ds(c, n, stride=ROW_TILE), :] for c in range(PACKED // LANES)], axis=-1)


def _unpack_bf16_pairs(u):
    lo = lax.bitcast_convert_type(u << 16, F32)
    hi = lax.bitcast_convert_type(u & jnp.uint32(0xFFFF0000), F32)
    return lo, hi


def _mod_kernel(cond_ref, w_ref, b_ref, o_ref):
    c = cond_ref[...]
    s = c / (1.0 + jnp.exp(-c))
    o_ref[...] = _dot(s.astype(BF16), w_ref[...].astype(BF16)) + b_ref[...]


def _modulation(cond8, w_mod, b_mod):
    n = w_mod.shape[1]
    tn = MOD_TILE
    return pl.pallas_call(
        _mod_kernel,
        grid=(n // tn,),
        in_specs=[
            pl.BlockSpec((8, D_MODEL), lambda j: (0, 0)),
            pl.BlockSpec((D_MODEL, tn), lambda j: (0, j)),
            pl.BlockSpec((1, tn), lambda j: (0, j)),
        ],
        out_specs=pl.BlockSpec((8, tn), lambda j: (0, j)),
        out_shape=jax.ShapeDtypeStruct((8, n), F32),
        compiler_params=_params(1),
        name="modulation",
    )(cond8, w_mod, b_mod.reshape(1, n))


def _rope(x, cos, sin_signed):
    lane = lax.broadcasted_iota(jnp.int32, x.shape, 1)
    partner = jnp.where((lane & 63) < 32,
                        pltpu.roll(x, HEAD_DIM - 32, axis=1),
                        pltpu.roll(x, 32, axis=1))
    return x * cos + partner * sin_signed


def _inproj_kernel(rows_ref, x_ref, mod_ref, g_ref, w_ref, *rest, rope):
    del rows_ref
    if rope:
        cos_ref, sin_ref, up_ref, q_ref, k_ref, v_ref = rest
    else:
        up_ref, q_ref, k_ref, v_ref = rest
    m = mod_ref[0]
    h = _rms(x_ref[...]) * g_ref[...]
    hb = (h * (1.0 + m[1:2]) + m[0:1]).astype(BF16)
    up_ref[...] = _dot(hb, w_ref[:, 0:POOL_WIDTH]).astype(BF16)
    q = _dot(hb, w_ref[:, POOL_WIDTH:POOL_WIDTH + ATTN_WIDTH])
    k = _dot(hb, w_ref[:, POOL_WIDTH + ATTN_WIDTH:IN_WIDTH - KV_WIDTH])
    v = _dot(hb, w_ref[:, IN_WIDTH - KV_WIDTH:IN_WIDTH])
    if rope:
        cos = cos_ref[...]
        sin = sin_ref[...]
        for hd in range(N_HEADS):
            sl = slice(hd * HEAD_DIM, (hd + 1) * HEAD_DIM)
            q_ref[:, sl] = (_rope(q[:, sl], cos, sin) * Q_SCALE).astype(BF16)
        for hd in range(N_KV_HEADS):
            sl = slice(hd * HEAD_DIM, (hd + 1) * HEAD_DIM)
            k_ref[:, sl] = _rope(k[:, sl], cos, sin).astype(k_ref.dtype)
    else:
        q_ref[...] = (q * Q_SCALE).astype(BF16)
        k_ref[...] = k.astype(k_ref.dtype)
    v_ref[...] = v.astype(v_ref.dtype)


def _in_projection(x, rows, mod3, g_mix, w_in_b, rope_tabs, kv_dtype):
    t = x.shape[0]
    tm = IN_TILE
    per = IN_TILE // TOKEN_TILE
    rope = rope_tabs is not None
    in_specs = [
        pl.BlockSpec((tm, D_MODEL), lambda i, r: (i, 0)),
        pl.BlockSpec((1, 6, D_MODEL), lambda i, r: (r[i * per], 0, 0)),
        pl.BlockSpec((1, D_MODEL), lambda i, r: (0, 0)),
        pl.BlockSpec((D_MODEL, IN_WIDTH), lambda i, r: (0, 0), pipeline_mode=pl.Buffered(1)),
    ]
    args = [x, mod3, g_mix, w_in_b]
    if rope:
        seq_tiles = rope_tabs[0].shape[0] // tm
        for tab in rope_tabs:
            in_specs.append(pl.BlockSpec((tm, HEAD_DIM), lambda i, r: (i % seq_tiles, 0)))
            args.append(tab)
    out_shape = (
        jax.ShapeDtypeStruct((t, POOL_WIDTH), BF16),
        jax.ShapeDtypeStruct((t, ATTN_WIDTH), BF16),
        jax.ShapeDtypeStruct((t, KV_WIDTH), kv_dtype),
        jax.ShapeDtypeStruct((t, KV_WIDTH), kv_dtype),
    )
    out_specs = (
        pl.BlockSpec((tm, POOL_WIDTH), lambda i, r: (i, 0)),
        pl.BlockSpec((tm, ATTN_WIDTH), lambda i, r: (i, 0)),
        pl.BlockSpec((tm, KV_WIDTH), lambda i, r: (i, 0)),
        pl.BlockSpec((tm, KV_WIDTH), lambda i, r: (i, 0)),
    )
    return pl.pallas_call(
        functools.partial(_inproj_kernel, rope=rope),
        grid_spec=pltpu.PrefetchScalarGridSpec(
            num_scalar_prefetch=1, grid=(t // tm,), in_specs=in_specs, out_specs=out_specs),
        out_shape=out_shape,
        compiler_params=_params(1),
        name="in_projection_rope" if rope else "in_projection",
    )(rows, *args)


def _pool_kernel(u_ref, pw_ref, ps_ref, g_ref, o_ref, *, seq_len):
    tq = TOKEN_TILE
    win = min(seq_len, tq + 2 * POOL_HALO)
    for sub in range(o_ref.shape[1] // tq):
        t0 = pl.program_id(1) * o_ref.shape[1] + sub * tq
        src0 = pl.multiple_of(jnp.clip(t0 - POOL_HALO, 0, seq_len - win), POOL_HALO)
        u = u_ref[0, pl.ds(src0, win), :]
        t = t0 + lax.broadcasted_iota(jnp.int32, (tq, win), 0)
        j = src0 + lax.broadcasted_iota(jnp.int32, (tq, win), 1)
        tc = t0 + lax.broadcasted_iota(jnp.int32, (tq, 1), 0)
        ys = []
        for gi, w in enumerate(POOL_WINDOWS):
            lo = jnp.maximum(t - w // 2, 0)
            hi = jnp.minimum(t + (w - w // 2), seq_len)
            cnt = (hi - lo).astype(F32)
            a = jnp.where((j >= lo) & (j < hi), 1.0, 0.0) - jnp.where(j == t, cnt, 0.0)
            cnt_col = (jnp.minimum(tc + (w - w // 2), seq_len) - jnp.maximum(tc - w // 2, 0)).astype(F32)
            sl = slice(gi * POOL_GROUP_DIM, (gi + 1) * POOL_GROUP_DIM)
            d = _dot(a.astype(BF16), u[:, sl]) / cnt_col
            ys.append(_dot(d.astype(BF16), pw_ref[gi]))
        y = jnp.concatenate(ys, axis=-1) * ps_ref[...]
        o_ref[0, sub * tq:(sub + 1) * tq, :] = (_rms(y) * g_ref[...]).astype(BF16)


def _pool_mixer(up, pool_w_b, pool_scale, g_out_pool):
    b, seq_len, _ = up.shape
    tq = TOKEN_TILE * min(POOL_TILES_PER_STEP, seq_len // TOKEN_TILE)
    return pl.pallas_call(
        functools.partial(_pool_kernel, seq_len=seq_len),
        grid=(b, seq_len // tq),
        in_specs=[
            pl.BlockSpec((1, seq_len, POOL_WIDTH), lambda bi, i: (bi, 0, 0)),
            pl.BlockSpec((len(POOL_WINDOWS), POOL_GROUP_DIM, POOL_GROUP_DIM), lambda bi, i: (0, 0, 0)),
            pl.BlockSpec((1, POOL_WIDTH), lambda bi, i: (0, 0)),
            pl.BlockSpec((1, POOL_WIDTH), lambda bi, i: (0, 0)),
        ],
        out_specs=pl.BlockSpec((1, tq, POOL_WIDTH), lambda bi, i: (bi, i, 0)),
        out_shape=jax.ShapeDtypeStruct((b, seq_len, POOL_WIDTH), BF16),
        compiler_params=_params(2),
        name=f"pool_mixer_{seq_len}",
    )(up, pool_w_b, pool_scale, g_out_pool)


def _stack_heads(q, kh, rows):
    return jnp.concatenate(
        [q[:, (kh * Q_PER_KV + g) * HEAD_DIM:(kh * Q_PER_KV + g + 1) * HEAD_DIM]
         for g in range(Q_PER_KV)], axis=0)


def _sink_column(sink_ref, kh, rows):
    return jnp.concatenate(
        [jnp.broadcast_to(sink_ref[kh * Q_PER_KV + g:kh * Q_PER_KV + g + 1, 0:1] * LOG2E, (rows, 1))
         for g in range(Q_PER_KV)], axis=0)


def _attend(q4, key_sets, value_sets, sk, band=None):
    s = [_dot_nt(q4, k) for k in key_sets]
    if band is not None:
        s[0] = jnp.where(band, s[0], NEG_INF)
    m = sk
    for si in s:
        m = jnp.maximum(m, jnp.max(si, axis=-1, keepdims=True))
    acc = None
    for si, v in zip(s, value_sets):
        v1 = jnp.concatenate([v, jnp.ones_like(v)], axis=-1)
        part = _dot(jnp.exp2(si - m).astype(BF16), v1)
        acc = part if acc is None else acc + part
    den = acc[:, HEAD_DIM:] + jnp.exp2(sk - m)
    return acc[:, :HEAD_DIM] / den


def _ctx_attn_kernel(q_ref, k_ref, v_ref, sink_ref, g_ref, o_ref, *, seq_len):
    rows = seq_len
    for j in range(q_ref.shape[0] // rows):
        rs = slice(j * rows, (j + 1) * rows)
        q = q_ref[rs, :]
        heads = [None] * N_HEADS
        for kh in range(N_KV_HEADS):
            sl = slice(kh * HEAD_DIM, (kh + 1) * HEAD_DIM)
            o = _attend(_stack_heads(q, kh, rows), [k_ref[rs, sl].astype(BF16)],
                        [v_ref[rs, sl].astype(BF16)], _sink_column(sink_ref, kh, rows))
            for g in range(Q_PER_KV):
                heads[kh * Q_PER_KV + g] = o[g * rows:(g + 1) * rows]
        y = jnp.concatenate(heads, axis=-1)
        o_ref[rs, :] = (_rms(y) * g_ref[...]).astype(BF16)


def _context_attention(q, k, v, sink_b, g_out_attn, seq_len):
    t = q.shape[0]
    blk = seq_len * CTX_SEQS_PER_STEP
    return pl.pallas_call(
        functools.partial(_ctx_attn_kernel, seq_len=seq_len),
        grid=(t // blk,),
        in_specs=[
            pl.BlockSpec((blk, ATTN_WIDTH), lambda b: (b, 0)),
            pl.BlockSpec((blk, KV_WIDTH), lambda b: (b, 0)),
            pl.BlockSpec((blk, KV_WIDTH), lambda b: (b, 0)),
            pl.BlockSpec((N_HEADS, LANES), lambda b: (0, 0)),
            pl.BlockSpec((1, ATTN_WIDTH), lambda b: (0, 0)),
        ],
        out_specs=pl.BlockSpec((blk, ATTN_WIDTH), lambda b: (b, 0)),
        out_shape=jax.ShapeDtypeStruct((t, ATTN_WIDTH), BF16),
        compiler_params=_params(1),
        name="context_attention",
    )(q, k, v, sink_b, g_out_attn)


def _lat_attn_kernel(q_ref, k_ref, v_ref, ck_ref, cv_ref, sink_ref, g_ref, o_ref, *, seq_len):
    rows = Q_BLOCK
    span = 3 * Q_BLOCK
    shape = (Q_PER_KV * rows, span)
    row_in_block = lax.broadcasted_iota(jnp.int32, shape, 0) & (rows - 1)
    col = lax.broadcasted_iota(jnp.int32, shape, 1)
    for j in range(Q_BLOCKS_PER_STEP):
        q0 = (pl.program_id(1) * Q_BLOCKS_PER_STEP + j) * rows
        start = pl.multiple_of(jnp.clip(q0 - Q_BLOCK, 0, seq_len - span), Q_BLOCK)
        q = q_ref[j * rows:(j + 1) * rows, :]
        kl = k_ref[0, pl.ds(start, span), :]
        vl = v_ref[0, pl.ds(start, span), :]
        band = jnp.abs(row_in_block - col + (q0 - start)) <= WINDOW
        heads = [None] * N_HEADS
        for kh in range(N_KV_HEADS):
            sl = slice(kh * HEAD_DIM, (kh + 1) * HEAD_DIM)
            o = _attend(_stack_heads(q, kh, rows),
                        [kl[:, sl], ck_ref[0, :, sl].astype(BF16)],
                        [vl[:, sl], cv_ref[0, :, sl].astype(BF16)],
                        _sink_column(sink_ref, kh, rows), band)
            for g in range(Q_PER_KV):
                heads[kh * Q_PER_KV + g] = o[g * rows:(g + 1) * rows]
        y = jnp.concatenate(heads, axis=-1)
        o_ref[j * rows:(j + 1) * rows, :] = (_rms(y) * g_ref[...]).astype(BF16)


def _latent_attention(q, k, v, cache_k, cache_v, sink_b, g_out_attn):
    b, seq_len, _ = k.shape
    q_rows = Q_BLOCK * Q_BLOCKS_PER_STEP
    nq = seq_len // q_rows
    past = cache_k.shape[1]
    return pl.pallas_call(
        functools.partial(_lat_attn_kernel, seq_len=seq_len),
        grid=(b, nq),
        in_specs=[
            pl.BlockSpec((q_rows, ATTN_WIDTH), lambda bi, n: (bi * nq + n, 0)),
            pl.BlockSpec((1, seq_len, KV_WIDTH), lambda bi, n: (bi, 0, 0)),
            pl.BlockSpec((1, seq_len, KV_WIDTH), lambda bi, n: (bi, 0, 0)),
            pl.BlockSpec((1, past, KV_WIDTH), lambda bi, n: (bi, 0, 0)),
            pl.BlockSpec((1, past, KV_WIDTH), lambda bi, n: (bi, 0, 0)),
            pl.BlockSpec((N_HEADS, LANES), lambda bi, n: (0, 0)),
            pl.BlockSpec((1, ATTN_WIDTH), lambda bi, n: (0, 0)),
        ],
        out_specs=pl.BlockSpec((q_rows, ATTN_WIDTH), lambda bi, n: (bi * nq + n, 0)),
        out_shape=jax.ShapeDtypeStruct((b * seq_len, ATTN_WIDTH), BF16),
        compiler_params=_params(2),
        name="latent_attention",
    )(q, k, v, cache_k, cache_v, sink_b, g_out_attn)


def _route(logits):
    lane = lax.broadcasted_iota(jnp.int32, logits.shape, 1).astype(F32)
    neg = -jnp.inf

    def first_argmax(x):
        mx = jnp.max(x, axis=-1, keepdims=True)
        return mx, jnp.min(jnp.where(x == mx, lane, float(ROUTER_LANES)), axis=-1, keepdims=True)

    gl = jnp.where(lane < N_EXPERT_GROUPS, logits, neg)
    gmax, g_idx = first_argmax(gl)
    p_g = 1.0 / jnp.sum(jnp.exp(gl - gmax), axis=-1, keepdims=True)
    base = N_EXPERT_GROUPS + EXPERTS_PER_GROUP * g_idx
    el = jnp.where((lane >= base) & (lane < base + EXPERTS_PER_GROUP), logits, neg)
    v1, i1 = first_argmax(el)
    v2, i2 = first_argmax(jnp.where(lane == i1, neg, el))
    e2 = jnp.exp(v2 - v1)
    w1 = p_g / (1.0 + e2)
    w2 = p_g * e2 / (1.0 + e2)
    ids = jnp.where(lane == 0.0, i1 - N_EXPERT_GROUPS, jnp.where(lane == 1.0, i2 - N_EXPERT_GROUPS, 0.0))
    wts = jnp.where(lane == 0.0, w1, jnp.where(lane == 1.0, w2, 0.0))
    return ids, wts


def _outproj_kernel(rows_ref, pool_p, attn_p, x_p, pool_s, attn_s, x_s, mod_ref, g_ref, w_ref, rw_ref,
                    rb_ref, x1_ref, h2_ref, ids_ref, wts_ref, mix0, mix1, *, n_ctx_tiles, n_tiles):
    del rows_ref
    i = pl.program_id(0)

    def project(pool_ref, attn_ref, mix_ref):
        mix_ref[...] = (_dot(pool_ref[...], w_ref[0:POOL_WIDTH, :])
                        + _dot(attn_ref[...], w_ref[POOL_WIDTH:POOL_WIDTH + ATTN_WIDTH, :]))

    def tail(x_ref, mix_ref):
        m = mod_ref[0]
        x1 = x_ref[...] + m[2:3] * mix_ref[...]
        x1_ref[...] = x1
        h2 = _rms(x1) * g_ref[...] * (1.0 + m[4:5]) + m[3:4]
        _store_row_tiles(h2_ref, _pack_bf16_pairs(h2))
        hi = h2.astype(BF16)
        lo = (h2 - hi.astype(F32)).astype(BF16)
        both = _dot(hi, rw_ref[...])
        logits = (both[:, :ROUTER_LANES] + both[:, ROUTER_LANES:] + _dot(lo, rw_ref[:, :ROUTER_LANES])
                  + rb_ref[...])
        ids, wts = _route(logits)
        ids_ref[...] = jnp.transpose(ids)[0:ROW_TILE, :].astype(jnp.int32)
        wts_ref[...] = wts

    ctx_in = (pool_p, attn_p)
    lat_in = (pool_s, attn_s)
    for parity, (cur, prev) in enumerate(((mix0, mix1), (mix1, mix0))):
        mine = (i % 2) == parity

        @pl.when(mine & (i == 0))
        def _(cur=cur):
            project(*ctx_in, cur)

        @pl.when(mine & (i >= 1) & (i < n_ctx_tiles))
        def _(cur=cur, prev=prev):
            tail(x_p, prev)
            project(*ctx_in, cur)

        @pl.when(mine & (i == n_ctx_tiles))
        def _(cur=cur, prev=prev):
            tail(x_p, prev)
            project(*lat_in, cur)

        @pl.when(mine & (i > n_ctx_tiles) & (i < n_tiles))
        def _(cur=cur, prev=prev):
            tail(x_s, prev)
            project(*lat_in, cur)

        @pl.when(mine & (i == n_tiles))
        def _(prev=prev):
            tail(x_s, prev)


def _out_projection(ctx, lat, rows, mod3, g_ffn, w_out_b, rw_split, rb):
    tm = TOKEN_TILE
    n_ctx = ctx[2].shape[0] // tm
    n_lat = lat[2].shape[0] // tm
    n_tiles = n_ctx + n_lat
    t_all = n_tiles * tm
    assert n_ctx >= 1 and n_lat >= 1

    def ctx_map(i, r):
        return (jnp.minimum(i, n_ctx - 1), 0)

    def lat_map(i, r):
        return (jnp.clip(i - n_ctx, 0, n_lat - 1), 0)

    def ctx_prev_map(i, r):
        return (jnp.clip(i - 1, 0, n_ctx - 1), 0)

    def lat_prev_map(i, r):
        return (jnp.clip(i - 1 - n_ctx, 0, n_lat - 1), 0)

    def prev_tile(i):
        return jnp.maximum(i - 1, 0)

    in_specs = [
        pl.BlockSpec((tm, POOL_WIDTH), ctx_map),
        pl.BlockSpec((tm, ATTN_WIDTH), ctx_map),
        pl.BlockSpec((tm, D_MODEL), ctx_prev_map),
        pl.BlockSpec((tm, POOL_WIDTH), lat_map),
        pl.BlockSpec((tm, ATTN_WIDTH), lat_map),
        pl.BlockSpec((tm, D_MODEL), lat_prev_map),
        pl.BlockSpec((1, 6, D_MODEL), lambda i, r: (r[prev_tile(i)], 0, 0)),
        pl.BlockSpec((1, D_MODEL), lambda i, r: (0, 0)),
        pl.BlockSpec((D_MODEL, D_MODEL), lambda i, r: (0, 0), pipeline_mode=pl.Buffered(1)),
        pl.BlockSpec((D_MODEL, 2 * ROUTER_LANES), lambda i, r: (0, 0)),
        pl.BlockSpec((1, ROUTER_LANES), lambda i, r: (0, 0)),
    ]
    out_shape = (
        jax.ShapeDtypeStruct((t_all, D_MODEL), F32),
        jax.ShapeDtypeStruct((t_all * ROW_TILE, LANES), jnp.uint32),
        jax.ShapeDtypeStruct((ROW_TILE, t_all), jnp.int32),
        jax.ShapeDtypeStruct((t_all, ROUTER_LANES), F32),
    )
    out_specs = (
        pl.BlockSpec((tm, D_MODEL), lambda i, r: (prev_tile(i), 0)),
        pl.BlockSpec((tm * ROW_TILE, LANES), lambda i, r: (prev_tile(i), 0)),
        pl.BlockSpec((ROW_TILE, tm), lambda i, r: (0, prev_tile(i))),
        pl.BlockSpec((tm, ROUTER_LANES), lambda i, r: (prev_tile(i), 0)),
    )
    return pl.pallas_call(
        functools.partial(_outproj_kernel, n_ctx_tiles=n_ctx, n_tiles=n_tiles),
        grid_spec=pltpu.PrefetchScalarGridSpec(
            num_scalar_prefetch=1, grid=(n_tiles + 1,), in_specs=in_specs, out_specs=out_specs,
            scratch_shapes=[pltpu.VMEM((tm, D_MODEL), F32), pltpu.VMEM((tm, D_MODEL), F32)]),
        out_shape=out_shape,
        compiler_params=_params(1),
        name="out_projection",
    )(rows, *ctx, *lat, mod3, g_ffn, w_out_b, rw_split, rb)


def _gather_rows(idx_ref, base, n_rows, src_hbm, dst, sem, *, unrolled, both_queues=False):
    def one(r, priority):
        src_row = pl.multiple_of(idx_ref[base + r], ROW_TILE)
        pltpu.make_async_copy(
            src_hbm.at[pl.ds(src_row, ROW_TILE)], dst.at[pl.ds(r * ROW_TILE, ROW_TILE)],
            sem).start(priority=priority)

    if unrolled:
        for r in range(n_rows):
            one(r, r % 2 if both_queues else 0)
    else:
        def body(r, carry):
            one(r, 0)
            return carry
        lax.fori_loop(0, n_rows, body, 0, unroll=8)


def _wait_rows(src_hbm, dst, sem):
    pltpu.make_async_copy(src_hbm.at[pl.ds(0, dst.shape[0])], dst, sem).wait()


def _moe_kernel(be_ref, ne_ref, nu_ref, first_ref, tok_ref, h_hbm, wg_hbm, wu_hbm, wd_hbm, y_ref,
                xg0, xg1, xg2, sg, su, sd, wgb, wub, wdb, sem, wsem):
    b = pl.program_id(0)
    n_used = nu_ref[0]
    last = n_used - 1
    active = b < n_used
    bufs = (xg0, xg1, xg2)
    n_buf = len(bufs)

    def weight_copies(e):
        return (pltpu.make_async_copy(wg_hbm.at[e], sg, wsem.at[0]),
                pltpu.make_async_copy(wu_hbm.at[e], su, wsem.at[1]),
                pltpu.make_async_copy(wd_hbm.at[e], sd, wsem.at[2]))

    @pl.when(b == 0)
    def _():
        for cp in weight_copies(be_ref[0]):
            cp.start(priority=1)
        for i in range(n_buf - 1):
            _gather_rows(tok_ref, first_ref[jnp.minimum(i, last)], EXPERT_ROWS, h_hbm, bufs[i], sem.at[i],
                         unrolled=False)

    new_expert = (b == 0) | (be_ref[b] != be_ref[jnp.maximum(b - 1, 0)])

    @pl.when(active & new_expert)
    def _():
        for cp in weight_copies(be_ref[b]):
            cp.wait()
        wgb[...] = sg[...].astype(BF16)
        wub[...] = su[...].astype(BF16)
        wdb[...] = sd[...].astype(BF16)

        @pl.when(ne_ref[b] >= 0)
        def _():
            for cp in weight_copies(ne_ref[b]):
                cp.start(priority=1)

    def step(slot):
        cur = bufs[slot]
        ahead = bufs[(slot + n_buf - 1) % n_buf]
        sem_ahead = sem.at[(slot + n_buf - 1) % n_buf]
        _wait_rows(h_hbm, cur, sem.at[slot])
        _gather_rows(tok_ref, first_ref[jnp.minimum(b + n_buf - 1, last)], EXPERT_ROWS, h_hbm, ahead,
                     sem_ahead, unrolled=True, both_queues=True)
        lo, hi = _unpack_bf16_pairs(_load_row_tiles(cur, EXPERT_ROWS))
        lo = lo.astype(BF16)
        hi = hi.astype(BF16)
        a = _dot(lo, wgb[0:PACKED, :]) + _dot(hi, wgb[PACKED:D_MODEL, :])
        u = _dot(lo, wub[0:PACKED, :]) + _dot(hi, wub[PACKED:D_MODEL, :])
        act = (a / (1.0 + jnp.exp(-a)) * u).astype(BF16)
        _store_row_tiles(y_ref, _pack_bf16_pairs(_dot(act, wdb[...])))

        @pl.when(b == last)
        def _():
            for i in range(1, n_buf):
                _wait_rows(h_hbm, bufs[(slot + i) % n_buf], sem.at[(slot + i) % n_buf])

    for slot in range(n_buf):
        @pl.when(active & (b % n_buf == slot))
        def _(slot=slot):
            step(slot)

    @pl.when(b >= n_used)
    def _():
        y_ref[...] = jnp.zeros_like(y_ref)


def _experts(tables, cap, h2_packed, w_gate, w_up, w_down):
    bm = EXPERT_ROWS
    xg = pltpu.VMEM((bm * ROW_TILE, LANES), jnp.uint32)
    return pl.pallas_call(
        _moe_kernel,
        grid_spec=pltpu.PrefetchScalarGridSpec(
            num_scalar_prefetch=len(tables),
            grid=(cap // bm,),
            in_specs=[pl.BlockSpec(memory_space=pl.ANY)] * 4,
            out_specs=pl.BlockSpec((bm * ROW_TILE, LANES), lambda b, *_: (b, 0)),
            scratch_shapes=[
                xg, xg, xg,
                pltpu.VMEM((D_MODEL, D_EXPERT), F32), pltpu.VMEM((D_MODEL, D_EXPERT), F32),
                pltpu.VMEM((D_EXPERT, D_MODEL), F32),
                pltpu.VMEM((D_MODEL, D_EXPERT), BF16), pltpu.VMEM((D_MODEL, D_EXPERT), BF16),
                pltpu.VMEM((D_EXPERT, D_MODEL), BF16),
                pltpu.SemaphoreType.DMA((3,)), pltpu.SemaphoreType.DMA((3,)),
            ],
        ),
        out_shape=jax.ShapeDtypeStruct((cap * ROW_TILE, LANES), jnp.uint32),
        compiler_params=_params(1),
        name="experts",
    )(*tables, h2_packed, w_gate, w_up, w_down)


def _combine_kernel(slot_ref, rows_ref, y_hbm, x1_ref, wts_ref, mod_ref, g_ref, op_ref, os_ref, yg0, yg1, sem,
                    *, n_ctx_tiles, n_tiles):
    del rows_ref
    tm = TOKEN_TILE
    i = pl.program_id(0)
    n_slots = slot_ref.shape[0] // TOP_K

    def start(tile, buf, s, unrolled):
        for k in range(TOP_K):
            _gather_rows(slot_ref, k * n_slots + tile * tm, tm, y_hbm, buf.at[k], s, unrolled=unrolled,
                         both_queues=True)

    @pl.when(i == 0)
    def _():
        start(0, yg0, sem.at[0], False)

    def step(cur, nxt, sem_cur, sem_nxt):
        for k in range(TOP_K):
            _wait_rows(y_hbm, cur.at[k], sem_cur)
        start(jnp.minimum(i + 1, n_tiles - 1), nxt, sem_nxt, True)
        w = wts_ref[...]
        lo0, hi0 = _unpack_bf16_pairs(_load_row_tiles(cur.at[0], tm))
        lo1, hi1 = _unpack_bf16_pairs(_load_row_tiles(cur.at[1], tm))
        ffn = jnp.concatenate([lo0 * w[:, 0:1] + lo1 * w[:, 1:2], hi0 * w[:, 0:1] + hi1 * w[:, 1:2]], axis=-1)
        x2 = x1_ref[...] + mod_ref[0][5:6] * ffn
        out = _rms(x2) * g_ref[...]

        @pl.when(i < n_ctx_tiles)
        def _():
            op_ref[...] = out

        @pl.when(i >= n_ctx_tiles)
        def _():
            os_ref[...] = out

        @pl.when(i == n_tiles - 1)
        def _():
            for k in range(TOP_K):
                _wait_rows(y_hbm, nxt.at[k], sem_nxt)

    @pl.when(i % 2 == 0)
    def _():
        step(yg0, yg1, sem.at[0], sem.at[1])

    @pl.when(i % 2 == 1)
    def _():
        step(yg1, yg0, sem.at[1], sem.at[0])


def _combine(slots, rows, y_buf, x1, wts, mod3, g_final, t_ctx):
    t = x1.shape[0]
    tm = TOKEN_TILE
    n_tiles = t // tm
    n_ctx = t_ctx // tm
    return pl.pallas_call(
        functools.partial(_combine_kernel, n_ctx_tiles=n_ctx, n_tiles=n_tiles),
        grid_spec=pltpu.PrefetchScalarGridSpec(
            num_scalar_prefetch=2,
            grid=(n_tiles,),
            in_specs=[
                pl.BlockSpec(memory_space=pl.ANY),
                pl.BlockSpec((tm, D_MODEL), lambda i, s, r: (i, 0)),
                pl.BlockSpec((tm, ROUTER_LANES), lambda i, s, r: (i, 0)),
                pl.BlockSpec((1, 6, D_MODEL), lambda i, s, r: (r[i], 0, 0)),
                pl.BlockSpec((1, D_MODEL), lambda i, s, r: (0, 0)),
            ],
            out_specs=(
                pl.BlockSpec((tm, D_MODEL), lambda i, s, r: (jnp.minimum(i, n_ctx - 1), 0)),
                pl.BlockSpec((tm, D_MODEL), lambda i, s, r: (jnp.maximum(i - n_ctx, 0), 0)),
            ),
            scratch_shapes=[pltpu.VMEM((TOP_K, tm * ROW_TILE, LANES), jnp.uint32),
                            pltpu.VMEM((TOP_K, tm * ROW_TILE, LANES), jnp.uint32),
                            pltpu.SemaphoreType.DMA((2,))],
        ),
        out_shape=(jax.ShapeDtypeStruct((t_ctx, D_MODEL), F32),
                   jax.ShapeDtypeStruct((t - t_ctx, D_MODEL), F32)),
        compiler_params=_params(1),
        name="combine",
    )(slots, rows, y_buf, x1, wts, mod3, g_final)


def _rope_tables(n_tokens):
    rows = n_tokens // GRID_W
    half = ROPE_AXIS_DIM // 2
    inv_freq = ROPE_BASE ** (-jnp.arange(half, dtype=F32) / half)
    ar = jnp.arange(rows, dtype=F32)[:, None] * inv_freq
    ac = jnp.arange(GRID_W, dtype=F32)[:, None] * inv_freq

    def per_token(row_vals, col_vals):
        r = jnp.repeat(row_vals, GRID_W, axis=0)
        c = jnp.tile(col_vals, (rows, 1))
        return r, c

    cr, cc = per_token(jnp.cos(ar), jnp.cos(ac))
    sr, sc = per_token(jnp.sin(ar), jnp.sin(ac))
    return jnp.concatenate([cr, cr, cc, cc], axis=-1), jnp.concatenate([-sr, sr, -sc, sc], axis=-1)


def _dispatch_tables(expert_ids):
    t = expert_ids[0].shape[0]
    bm = EXPERT_ROWS
    flat_e = jnp.concatenate(expert_ids)
    onehot = (jnp.arange(N_EXPERTS, dtype=jnp.int32)[:, None] == flat_e[None, :]).astype(jnp.int32)
    running = jnp.cumsum(onehot, axis=1)
    rank = jnp.sum(running * onehot, axis=0) - 1
    counts = running[:, -1]
    padded = (counts + bm - 1) // bm * bm
    pad_end = jnp.cumsum(padded)
    pad_start = pad_end - padded
    dest = (jnp.sum(pad_start[:, None] * onehot, axis=0) + rank).astype(jnp.int32)
    n_blocks = (t * TOP_K + N_EXPERTS * (bm - 1) + bm - 1) // bm
    keys = jnp.concatenate([flat_e, jnp.full((bm,), N_EXPERTS, flat_e.dtype)])
    token_tile = jnp.arange(t, dtype=jnp.int32) * ROW_TILE
    tiles = jnp.concatenate([token_tile] * TOP_K + [jnp.zeros((bm,), jnp.int32)])
    _, src_tile = lax.sort((keys, tiles), num_keys=1, is_stable=True)
    n_used = (pad_end[-1] // bm).astype(jnp.int32)
    blk = jnp.arange(n_blocks, dtype=jnp.int32)
    blk = jnp.minimum(blk, n_used - 1)
    block_expert = jnp.sum((pad_end[None, :] <= (blk * bm)[:, None]).astype(jnp.int32), axis=1)
    block_expert = jnp.minimum(block_expert, N_EXPERTS - 1)
    e_ids = jnp.arange(N_EXPERTS, dtype=jnp.int32)
    later = (e_ids[None, :] > e_ids[:, None]) & (counts[None, :] > 0)
    next_of = jnp.min(jnp.where(later, e_ids[None, :], N_EXPERTS), axis=1)
    next_of = jnp.where(next_of == N_EXPERTS, -1, next_of)
    of_block = (block_expert[:, None] == e_ids[None, :]).astype(jnp.int32)
    next_expert = jnp.sum(next_of[None, :] * of_block, axis=1)
    start = jnp.cumsum(counts) - counts
    first_src = jnp.sum((start - pad_start)[None, :] * of_block, axis=1) + blk * bm
    return (block_expert.astype(jnp.int32), next_expert.astype(jnp.int32), n_used.reshape(1),
            first_src.astype(jnp.int32), src_tile, dest * ROW_TILE, n_blocks * bm)


def kernel(x_prompt, x_sample, cache_k, cache_v, c, c_ctx, w_mod, b_mod, norm_mix_g, norm_ffn_g, w_in,
           pool_w, pool_scale, attn_sink, out_norm_pool_g, out_norm_attn_g, w_out, router_group_w,
           router_group_b, router_expert_w, router_expert_b, w_gate, w_up, w_down, final_norm_g):
    depth = w_mod.shape[0]
    assert depth == 1, "single trunk layer"
    bp, lp, _ = x_prompt.shape
    bs, ls, _ = x_sample.shape
    tp, ts = bp * lp, bs * ls
    tm = TOKEN_TILE
    l = 0

    cond8 = jnp.zeros((8, D_MODEL), F32).at[:bs].set(c).at[bs].set(c_ctx)
    mod3 = _modulation(cond8, w_mod[l], b_mod[l]).reshape(8, 6, D_MODEL)
    rows_p = jnp.full((tp // tm,), bs, jnp.int32)
    rows_s = jnp.arange(ts // tm, dtype=jnp.int32) // (ls // tm)

    w_in_b = w_in[l].astype(BF16)
    w_out_b = w_out[l].astype(BF16)
    pool_w_b = pool_w[l].astype(BF16)
    g_mix = norm_mix_g[l].reshape(1, D_MODEL)
    g_ffn = norm_ffn_g[l].reshape(1, D_MODEL)
    g_pool = out_norm_pool_g[l].reshape(1, POOL_WIDTH)
    g_attn = out_norm_attn_g[l].reshape(1, ATTN_WIDTH)
    p_scale = pool_scale[l].reshape(1, POOL_WIDTH)
    sink_b = jnp.broadcast_to(attn_sink[l][:, None], (N_HEADS, LANES))
    g_final = final_norm_g.reshape(1, D_MODEL)

    rw = jnp.concatenate(
        [router_group_w[l], jnp.transpose(router_expert_w[l], (1, 0, 2)).reshape(D_MODEL, N_EXPERTS)], axis=1)
    rw = jnp.pad(rw, ((0, 0), (0, ROUTER_LANES - rw.shape[1])))
    rw_hi = rw.astype(BF16)
    rw_split = jnp.concatenate([rw_hi, (rw - rw_hi.astype(F32)).astype(BF16)], axis=1)
    rb = jnp.concatenate([router_group_b[l], router_expert_b[l].reshape(N_EXPERTS)])
    rb = jnp.pad(rb, (0, ROUTER_LANES - rb.shape[0])).reshape(1, ROUTER_LANES)

    xp = x_prompt.reshape(tp, D_MODEL)
    xs = x_sample.reshape(ts, D_MODEL)

    up_p, q_p, k_p, v_p = _in_projection(xp, rows_p, mod3, g_mix, w_in_b, None, F32)
    pool_p = _pool_mixer(up_p.reshape(bp, lp, POOL_WIDTH), pool_w_b, p_scale, g_pool).reshape(tp, POOL_WIDTH)
    attn_p = _context_attention(q_p, k_p, v_p, sink_b, g_attn, lp)
    up_s, q_s, k_s, v_s = _in_projection(xs, rows_s, mod3, g_mix, w_in_b, _rope_tables(ls), BF16)
    pool_s = _pool_mixer(up_s.reshape(bs, ls, POOL_WIDTH), pool_w_b, p_scale, g_pool).reshape(ts, POOL_WIDTH)
    attn_s = _latent_attention(
        q_s, k_s.reshape(bs, ls, KV_WIDTH), v_s.reshape(bs, ls, KV_WIDTH),
        cache_k[:, l].reshape(bs, -1, KV_WIDTH), cache_v[:, l].reshape(bs, -1, KV_WIDTH), sink_b, g_attn)

    rows_all = jnp.concatenate([rows_p, rows_s])
    x1_all, h2_all, ids_all, wts_all = _out_projection(
        (pool_p, attn_p, xp), (pool_s, attn_s, xs), rows_all, mod3, g_ffn, w_out_b, rw_split, rb)

    *expert_tables, slots, cap = _dispatch_tables([ids_all[k] for k in range(TOP_K)])
    y_buf = _experts(expert_tables, cap, h2_all, w_gate[l], w_up[l], w_down[l])

    y_p, y_s = _combine(slots, rows_all, y_buf, x1_all, wts_all, mod3, g_final, tp)

    kv_shape = (bp, 1, lp, N_KV_HEADS, HEAD_DIM)
    return (y_p.reshape(bp, lp, D_MODEL), y_s.reshape(bs, ls, D_MODEL),
            k_p.reshape(kv_shape), v_p.reshape(kv_shape))
```

```python
import functools

import jax
import jax.numpy as jnp
from jax import lax
from jax.experimental import pallas as pl
from jax.experimental.pallas import tpu as pltpu

F32 = jnp.float32
BF16 = jnp.bfloat16

D_MODEL = 2048
GRID_W = 64
HEAD_DIM = 128
ATTN_WIDTH = 1024
POOL_WIDTH = 1024
N_HEADS = 8
N_KV_HEADS = 2
Q_PER_KV = 4
KV_WIDTH = 256
IN_WIDTH = 2560
WINDOW = 128
POOL_WINDOWS = (2, 4, 8, 16)
POOL_GROUP_DIM = 256
ROPE_BASE = 10000.0
ROPE_AXIS_DIM = 64
N_EXPERT_GROUPS = 4
EXPERTS_PER_GROUP = 8
N_EXPERTS = 32
TOP_K = 2
D_EXPERT = 512
NORM_EPS = 1e-6
NEG_INF = -1e30
ATTN_SCALE = HEAD_DIM ** -0.5
LOG2E = 1.4426950408889634
Q_SCALE = ATTN_SCALE * LOG2E

LANES = 128
Q_BLOCKS_PER_STEP = 4
CTX_SEQS_PER_STEP = 4
VMEM_LIMIT = 48 * 1024 * 1024
PACKED = D_MODEL // 2
ROW_TILE = 8
assert PACKED == ROW_TILE * LANES, "a packed row must fill exactly one tile"
MOD_TILE = 1024
TOKEN_TILE = 256
IN_TILE = 512
POOL_TILES_PER_STEP = 2
POOL_HALO = 16
Q_BLOCK = 128
EXPERT_ROWS = 256
ROUTER_LANES = 128


def _params(n_grid):
    return pltpu.CompilerParams(
        dimension_semantics=("arbitrary",) * n_grid, vmem_limit_bytes=VMEM_LIMIT)


def _rms(x):
    return x * lax.rsqrt(jnp.mean(x * x, axis=-1, keepdims=True) + NORM_EPS)


def _dot(a, b):
    return jnp.dot(a, b, preferred_element_type=F32)


def _dot_nt(a, b):
    return lax.dot_general(a, b, (((1,), (1,)), ((), ())), preferred_element_type=F32)


def _pack_bf16_pairs(x):
    c = x.shape[1] // 2
    lo = lax.bitcast_convert_type(x[:, :c].astype(BF16).astype(F32), jnp.uint32)
    hi = lax.bitcast_convert_type(x[:, c:].astype(BF16).astype(F32), jnp.uint32)
    return hi | (lo >> 16)


def _store_row_tiles(ref, x):
    n = x.shape[0]
    for c in range(PACKED // LANES):
        ref[pl.ds(c, n, stride=ROW_TILE), :] = x[:, c * LANES:(c + 1) * LANES]


def _load_row_tiles(ref, n):
    return jnp.concatenate(
        [ref[pl.ds(c, n, stride=ROW_TILE), :] for c in range(PACKED // LANES)], axis=-1)


def _unpack_bf16_pairs(u):
    lo = lax.bitcast_convert_type(u << 16, F32)
    hi = lax.bitcast_convert_type(u & jnp.uint32(0xFFFF0000), F32)
    return lo, hi


def _mod_kernel(cond_ref, w_ref, b_ref, o_ref):
    c = cond_ref[...]
    s = c / (1.0 + jnp.exp(-c))
    o_ref[...] = _dot(s.astype(BF16), w_ref[...].astype(BF16)) + b_ref[...]


def _modulation(cond8, w_mod, b_mod):
    n = w_mod.shape[1]
    tn = MOD_TILE
    return pl.pallas_call(
        _mod_kernel,
        grid=(n // tn,),
        in_specs=[
            pl.BlockSpec((8, D_MODEL), lambda j: (0, 0)),
            pl.BlockSpec((D_MODEL, tn), lambda j: (0, j)),
            pl.BlockSpec((1, tn), lambda j: (0, j)),
        ],
        out_specs=pl.BlockSpec((8, tn), lambda j: (0, j)),
        out_shape=jax.ShapeDtypeStruct((8, n), F32),
        compiler_params=_params(1),
        name="modulation",
    )(cond8, w_mod, b_mod.reshape(1, n))


def _rope(x, cos, sin_signed):
    lane = lax.broadcasted_iota(jnp.int32, x.shape, 1)
    partner = jnp.where((lane & 63) < 32,
                        pltpu.roll(x, HEAD_DIM - 32, axis=1),
                        pltpu.roll(x, 32, axis=1))
    return x * cos + partner * sin_signed


def _inproj_kernel(rows_ref, x_ref, mod_ref, g_ref, w_ref, *rest, rope):
    del rows_ref
    if rope:
        cos_ref, sin_ref, up_ref, q_ref, k_ref, v_ref = rest
    else:
        up_ref, q_ref, k_ref, v_ref = rest
    m = mod_ref[0]
    h = _rms(x_ref[...]) * g_ref[...]
    hb = (h * (1.0 + m[1:2]) + m[0:1]).astype(BF16)
    up_ref[...] = _dot(hb, w_ref[:, 0:POOL_WIDTH]).astype(BF16)
    q = _dot(hb, w_ref[:, POOL_WIDTH:POOL_WIDTH + ATTN_WIDTH])
    k = _dot(hb, w_ref[:, POOL_WIDTH + ATTN_WIDTH:IN_WIDTH - KV_WIDTH])
    v = _dot(hb, w_ref[:, IN_WIDTH - KV_WIDTH:IN_WIDTH])
    if rope:
        cos = cos_ref[...]
        sin = sin_ref[...]
        for hd in range(N_HEADS):
            sl = slice(hd * HEAD_DIM, (hd + 1) * HEAD_DIM)
            q_ref[:, sl] = (_rope(q[:, sl], cos, sin) * Q_SCALE).astype(BF16)
        for hd in range(N_KV_HEADS):
            sl = slice(hd * HEAD_DIM, (hd + 1) * HEAD_DIM)
            k_ref[:, sl] = _rope(k[:, sl], cos, sin).astype(k_ref.dtype)
    else:
        q_ref[...] = (q * Q_SCALE).astype(BF16)
        k_ref[...] = k.astype(k_ref.dtype)
    v_ref[...] = v.astype(v_ref.dtype)


def _in_projection(x, rows, mod3, g_mix, w_in_b, rope_tabs, kv_dtype):
    t = x.shape[0]
    tm = IN_TILE
    per = IN_TILE // TOKEN_TILE
    rope = rope_tabs is not None
    in_specs = [
        pl.BlockSpec((tm, D_MODEL), lambda i, r: (i, 0)),
        pl.BlockSpec((1, 6, D_MODEL), lambda i, r: (r[i * per], 0, 0)),
        pl.BlockSpec((1, D_MODEL), lambda i, r: (0, 0)),
        pl.BlockSpec((D_MODEL, IN_WIDTH), lambda i, r: (0, 0), pipeline_mode=pl.Buffered(1)),
    ]
    args = [x, mod3, g_mix, w_in_b]
    if rope:
        seq_tiles = rope_tabs[0].shape[0] // tm
        for tab in rope_tabs:
            in_specs.append(pl.BlockSpec((tm, HEAD_DIM), lambda i, r: (i % seq_tiles, 0)))
            args.append(tab)
    out_shape = (
        jax.ShapeDtypeStruct((t, POOL_WIDTH), BF16),
        jax.ShapeDtypeStruct((t, ATTN_WIDTH), BF16),
        jax.ShapeDtypeStruct((t, KV_WIDTH), kv_dtype),
        jax.ShapeDtypeStruct((t, KV_WIDTH), kv_dtype),
    )
    out_specs = (
        pl.BlockSpec((tm, POOL_WIDTH), lambda i, r: (i, 0)),
        pl.BlockSpec((tm, ATTN_WIDTH), lambda i, r: (i, 0)),
        pl.BlockSpec((tm, KV_WIDTH), lambda i, r: (i, 0)),
        pl.BlockSpec((tm, KV_WIDTH), lambda i, r: (i, 0)),
    )
    return pl.pallas_call(
        functools.partial(_inproj_kernel, rope=rope),
        grid_spec=pltpu.PrefetchScalarGridSpec(
            num_scalar_prefetch=1, grid=(t // tm,), in_specs=in_specs, out_specs=out_specs),
        out_shape=out_shape,
        compiler_params=_params(1),
        name="in_projection_rope" if rope else "in_projection",
    )(rows, *args)


def _pool_kernel(u_ref, pw_ref, ps_ref, g_ref, o_ref, *, seq_len):
    tq = TOKEN_TILE
    win = min(seq_len, tq + 2 * POOL_HALO)
    for sub in range(o_ref.shape[1] // tq):
        t0 = pl.program_id(1) * o_ref.shape[1] + sub * tq
        src0 = pl.multiple_of(jnp.clip(t0 - POOL_HALO, 0, seq_len - win), POOL_HALO)
        u = u_ref[0, pl.ds(src0, win), :]
        t = t0 + lax.broadcasted_iota(jnp.int32, (tq, win), 0)
        j = src0 + lax.broadcasted_iota(jnp.int32, (tq, win), 1)
        tc = t0 + lax.broadcasted_iota(jnp.int32, (tq, 1), 0)
        ys = []
        for gi, w in enumerate(POOL_WINDOWS):
            lo = jnp.maximum(t - w // 2, 0)
            hi = jnp.minimum(t + (w - w // 2), seq_len)
            cnt = (hi - lo).astype(F32)
            a = jnp.where((j >= lo) & (j < hi), 1.0, 0.0) - jnp.where(j == t, cnt, 0.0)
            cnt_col = (jnp.minimum(tc + (w - w // 2), seq_len) - jnp.maximum(tc - w // 2, 0)).astype(F32)
            sl = slice(gi * POOL_GROUP_DIM, (gi + 1) * POOL_GROUP_DIM)
            d = _dot(a.astype(BF16), u[:, sl]) / cnt_col
            ys.append(_dot(d.astype(BF16), pw_ref[gi]))
        y = jnp.concatenate(ys, axis=-1) * ps_ref[...]
        o_ref[0, sub * tq:(sub + 1) * tq, :] = (_rms(y) * g_ref[...]).astype(BF16)


def _pool_mixer(up, pool_w_b, pool_scale, g_out_pool):
    b, seq_len, _ = up.shape
    tq = TOKEN_TILE * min(POOL_TILES_PER_STEP, seq_len // TOKEN_TILE)
    return pl.pallas_call(
        functools.partial(_pool_kernel, seq_len=seq_len),
        grid=(b, seq_len // tq),
        in_specs=[
            pl.BlockSpec((1, seq_len, POOL_WIDTH), lambda bi, i: (bi, 0, 0)),
            pl.BlockSpec((len(POOL_WINDOWS), POOL_GROUP_DIM, POOL_GROUP_DIM), lambda bi, i: (0, 0, 0)),
            pl.BlockSpec((1, POOL_WIDTH), lambda bi, i: (0, 0)),
            pl.BlockSpec((1, POOL_WIDTH), lambda bi, i: (0, 0)),
        ],
        out_specs=pl.BlockSpec((1, tq, POOL_WIDTH), lambda bi, i: (bi, i, 0)),
        out_shape=jax.ShapeDtypeStruct((b, seq_len, POOL_WIDTH), BF16),
        compiler_params=_params(2),
        name=f"pool_mixer_{seq_len}",
    )(up, pool_w_b, pool_scale, g_out_pool)


def _stack_heads(q, kh, rows):
    return jnp.concatenate(
        [q[:, (kh * Q_PER_KV + g) * HEAD_DIM:(kh * Q_PER_KV + g + 1) * HEAD_DIM]
         for g in range(Q_PER_KV)], axis=0)


def _sink_column(sink_ref, kh, rows):
    return jnp.concatenate(
        [jnp.broadcast_to(sink_ref[kh * Q_PER_KV + g:kh * Q_PER_KV + g + 1, 0:1] * LOG2E, (rows, 1))
         for g in range(Q_PER_KV)], axis=0)


def _attend(q4, key_sets, value_sets, sk, band=None):
    s = [_dot_nt(q4, k) for k in key_sets]
    if band is not None:
        s[0] = jnp.where(band, s[0], NEG_INF)
    m = sk
    for si in s:
        m = jnp.maximum(m, jnp.max(si, axis=-1, keepdims=True))
    acc = None
    for si, v in zip(s, value_sets):
        v1 = jnp.concatenate([v, jnp.ones_like(v)], axis=-1)
        part = _dot(jnp.exp2(si - m).astype(BF16), v1)
        acc = part if acc is None else acc + part
    den = acc[:, HEAD_DIM:] + jnp.exp2(sk - m)
    return acc[:, :HEAD_DIM] / den


def _ctx_attn_kernel(q_ref, k_ref, v_ref, sink_ref, g_ref, o_ref, *, seq_len):
    rows = seq_len
    for j in range(q_ref.shape[0] // rows):
        rs = slice(j * rows, (j + 1) * rows)
        q = q_ref[rs, :]
        heads = [None] * N_HEADS
        for kh in range(N_KV_HEADS):
            sl = slice(kh * HEAD_DIM, (kh + 1) * HEAD_DIM)
            o = _attend(_stack_heads(q, kh, rows), [k_ref[rs, sl].astype(BF16)],
                        [v_ref[rs, sl].astype(BF16)], _sink_column(sink_ref, kh, rows))
            for g in range(Q_PER_KV):
                heads[kh * Q_PER_KV + g] = o[g * rows:(g + 1) * rows]
        y = jnp.concatenate(heads, axis=-1)
        o_ref[rs, :] = (_rms(y) * g_ref[...]).astype(BF16)


def _context_attention(q, k, v, sink_b, g_out_attn, seq_len):
    t = q.shape[0]
    blk = seq_len * CTX_SEQS_PER_STEP
    return pl.pallas_call(
        functools.partial(_ctx_attn_kernel, seq_len=seq_len),
        grid=(t // blk,),
        in_specs=[
            pl.BlockSpec((blk, ATTN_WIDTH), lambda b: (b, 0)),
            pl.BlockSpec((blk, KV_WIDTH), lambda b: (b, 0)),
            pl.BlockSpec((blk, KV_WIDTH), lambda b: (b, 0)),
            pl.BlockSpec((N_HEADS, LANES), lambda b: (0, 0)),
            pl.BlockSpec((1, ATTN_WIDTH), lambda b: (0, 0)),
        ],
        out_specs=pl.BlockSpec((blk, ATTN_WIDTH), lambda b: (b, 0)),
        out_shape=jax.ShapeDtypeStruct((t, ATTN_WIDTH), BF16),
        compiler_params=_params(1),
        name="context_attention",
    )(q, k, v, sink_b, g_out_attn)


def _lat_attn_kernel(q_ref, k_ref, v_ref, ck_ref, cv_ref, sink_ref, g_ref, o_ref, *, seq_len):
    rows = Q_BLOCK
    span = 3 * Q_BLOCK
    shape = (Q_PER_KV * rows, span)
    row_in_block = lax.broadcasted_iota(jnp.int32, shape, 0) & (rows - 1)
    col = lax.broadcasted_iota(jnp.int32, shape, 1)
    for j in range(Q_BLOCKS_PER_STEP):
        q0 = (pl.program_id(1) * Q_BLOCKS_PER_STEP + j) * rows
        start = pl.multiple_of(jnp.clip(q0 - Q_BLOCK, 0, seq_len - span), Q_BLOCK)
        q = q_ref[j * rows:(j + 1) * rows, :]
        kl = k_ref[0, pl.ds(start, span), :]
        vl = v_ref[0, pl.ds(start, span), :]
        band = jnp.abs(row_in_block - col + (q0 - start)) <= WINDOW
        heads = [None] * N_HEADS
        for kh in range(N_KV_HEADS):
            sl = slice(kh * HEAD_DIM, (kh + 1) * HEAD_DIM)
            o = _attend(_stack_heads(q, kh, rows),
                        [kl[:, sl], ck_ref[0, :, sl].astype(BF16)],
                        [vl[:, sl], cv_ref[0, :, sl].astype(BF16)],
                        _sink_column(sink_ref, kh, rows), band)
            for g in range(Q_PER_KV):
                heads[kh * Q_PER_KV + g] = o[g * rows:(g + 1) * rows]
        y = jnp.concatenate(heads, axis=-1)
        o_ref[j * rows:(j + 1) * rows, :] = (_rms(y) * g_ref[...]).astype(BF16)


def _latent_attention(q, k, v, cache_k, cache_v, sink_b, g_out_attn):
    b, seq_len, _ = k.shape
    q_rows = Q_BLOCK * Q_BLOCKS_PER_STEP
    nq = seq_len // q_rows
    past = cache_k.shape[1]
    return pl.pallas_call(
        functools.partial(_lat_attn_kernel, seq_len=seq_len),
        grid=(b, nq),
        in_specs=[
            pl.BlockSpec((q_rows, ATTN_WIDTH), lambda bi, n: (bi * nq + n, 0)),
            pl.BlockSpec((1, seq_len, KV_WIDTH), lambda bi, n: (bi, 0, 0)),
            pl.BlockSpec((1, seq_len, KV_WIDTH), lambda bi, n: (bi, 0, 0)),
            pl.BlockSpec((1, past, KV_WIDTH), lambda bi, n: (bi, 0, 0)),
            pl.BlockSpec((1, past, KV_WIDTH), lambda bi, n: (bi, 0, 0)),
            pl.BlockSpec((N_HEADS, LANES), lambda bi, n: (0, 0)),
            pl.BlockSpec((1, ATTN_WIDTH), lambda bi, n: (0, 0)),
        ],
        out_specs=pl.BlockSpec((q_rows, ATTN_WIDTH), lambda bi, n: (bi * nq + n, 0)),
        out_shape=jax.ShapeDtypeStruct((b * seq_len, ATTN_WIDTH), BF16),
        compiler_params=_params(2),
        name="latent_attention",
    )(q, k, v, cache_k, cache_v, sink_b, g_out_attn)


def _route(logits):
    lane = lax.broadcasted_iota(jnp.int32, logits.shape, 1).astype(F32)
    neg = -jnp.inf

    def first_argmax(x):
        mx = jnp.max(x, axis=-1, keepdims=True)
        return mx, jnp.min(jnp.where(x == mx, lane, float(ROUTER_LANES)), axis=-1, keepdims=True)

    gl = jnp.where(lane < N_EXPERT_GROUPS, logits, neg)
    gmax, g_idx = first_argmax(gl)
    p_g = 1.0 / jnp.sum(jnp.exp(gl - gmax), axis=-1, keepdims=True)
    base = N_EXPERT_GROUPS + EXPERTS_PER_GROUP * g_idx
    el = jnp.where((lane >= base) & (lane < base + EXPERTS_PER_GROUP), logits, neg)
    v1, i1 = first_argmax(el)
    v2, i2 = first_argmax(jnp.where(lane == i1, neg, el))
    e2 = jnp.exp(v2 - v1)
    w1 = p_g / (1.0 + e2)
    w2 = p_g * e2 / (1.0 + e2)
    ids = jnp.where(lane == 0.0, i1 - N_EXPERT_GROUPS, jnp.where(lane == 1.0, i2 - N_EXPERT_GROUPS, 0.0))
    wts = jnp.where(lane == 0.0, w1, jnp.where(lane == 1.0, w2, 0.0))
    return ids, wts


def _outproj_kernel(rows_ref, pool_p, attn_p, x_p, pool_s, attn_s, x_s, mod_ref, g_ref, w_ref, rw_ref,
                    rb_ref, x1_ref, h2_ref, ids_ref, wts_ref, mix0, mix1, *, n_ctx_tiles, n_tiles):
    del rows_ref
    i = pl.program_id(0)

    def project(pool_ref, attn_ref, mix_ref):
        mix_ref[...] = (_dot(pool_ref[...], w_ref[0:POOL_WIDTH, :])
                        + _dot(attn_ref[...], w_ref[POOL_WIDTH:POOL_WIDTH + ATTN_WIDTH, :]))

    def tail(x_ref, mix_ref):
        m = mod_ref[0]
        x1 = x_ref[...] + m[2:3] * mix_ref[...]
        x1_ref[...] = x1
        h2 = _rms(x1) * g_ref[...] * (1.0 + m[4:5]) + m[3:4]
        _store_row_tiles(h2_ref, _pack_bf16_pairs(h2))
        hi = h2.astype(BF16)
        lo = (h2 - hi.astype(F32)).astype(BF16)
        both = _dot(hi, rw_ref[...])
        logits = (both[:, :ROUTER_LANES] + both[:, ROUTER_LANES:] + _dot(lo, rw_ref[:, :ROUTER_LANES])
                  + rb_ref[...])
        ids, wts = _route(logits)
        ids_ref[...] = jnp.transpose(ids)[0:ROW_TILE, :].astype(jnp.int32)
        wts_ref[...] = wts

    ctx_in = (pool_p, attn_p)
    lat_in = (pool_s, attn_s)
    for parity, (cur, prev) in enumerate(((mix0, mix1), (mix1, mix0))):
        mine = (i % 2) == parity

        @pl.when(mine & (i == 0))
        def _(cur=cur):
            project(*ctx_in, cur)

        @pl.when(mine & (i >= 1) & (i < n_ctx_tiles))
        def _(cur=cur, prev=prev):
            tail(x_p, prev)
            project(*ctx_in, cur)

        @pl.when(mine & (i == n_ctx_tiles))
        def _(cur=cur, prev=prev):
            tail(x_p, prev)
            project(*lat_in, cur)

        @pl.when(mine & (i > n_ctx_tiles) & (i < n_tiles))
        def _(cur=cur, prev=prev):
            tail(x_s, prev)
            project(*lat_in, cur)

        @pl.when(mine & (i == n_tiles))
        def _(prev=prev):
            tail(x_s, prev)


def _out_projection(ctx, lat, rows, mod3, g_ffn, w_out_b, rw_split, rb):
    tm = TOKEN_TILE
    n_ctx = ctx[2].shape[0] // tm
    n_lat = lat[2].shape[0] // tm
    n_tiles = n_ctx + n_lat
    t_all = n_tiles * tm
    assert n_ctx >= 1 and n_lat >= 1

    def ctx_map(i, r):
        return (jnp.minimum(i, n_ctx - 1), 0)

    def lat_map(i, r):
        return (jnp.clip(i - n_ctx, 0, n_lat - 1), 0)

    def ctx_prev_map(i, r):
        return (jnp.clip(i - 1, 0, n_ctx - 1), 0)

    def lat_prev_map(i, r):
        return (jnp.clip(i - 1 - n_ctx, 0, n_lat - 1), 0)

    def prev_tile(i):
        return jnp.maximum(i - 1, 0)

    in_specs = [
        pl.BlockSpec((tm, POOL_WIDTH), ctx_map),
        pl.BlockSpec((tm, ATTN_WIDTH), ctx_map),
        pl.BlockSpec((tm, D_MODEL), ctx_prev_map),
        pl.BlockSpec((tm, POOL_WIDTH), lat_map),
        pl.BlockSpec((tm, ATTN_WIDTH), lat_map),
        pl.BlockSpec((tm, D_MODEL), lat_prev_map),
        pl.BlockSpec((1, 6, D_MODEL), lambda i, r: (r[prev_tile(i)], 0, 0)),
        pl.BlockSpec((1, D_MODEL), lambda i, r: (0, 0)),
        pl.BlockSpec((D_MODEL, D_MODEL), lambda i, r: (0, 0), pipeline_mode=pl.Buffered(1)),
        pl.BlockSpec((D_MODEL, 2 * ROUTER_LANES), lambda i, r: (0, 0)),
        pl.BlockSpec((1, ROUTER_LANES), lambda i, r: (0, 0)),
    ]
    out_shape = (
        jax.ShapeDtypeStruct((t_all, D_MODEL), F32),
        jax.ShapeDtypeStruct((t_all * ROW_TILE, LANES), jnp.uint32),
        jax.ShapeDtypeStruct((ROW_TILE, t_all), jnp.int32),
        jax.ShapeDtypeStruct((t_all, ROUTER_LANES), F32),
    )
    out_specs = (
        pl.BlockSpec((tm, D_MODEL), lambda i, r: (prev_tile(i), 0)),
        pl.BlockSpec((tm * ROW_TILE, LANES), lambda i, r: (prev_tile(i), 0)),
        pl.BlockSpec((ROW_TILE, tm), lambda i, r: (0, prev_tile(i))),
        pl.BlockSpec((tm, ROUTER_LANES), lambda i, r: (prev_tile(i), 0)),
    )
    return pl.pallas_call(
        functools.partial(_outproj_kernel, n_ctx_tiles=n_ctx, n_tiles=n_tiles),
        grid_spec=pltpu.PrefetchScalarGridSpec(
            num_scalar_prefetch=1, grid=(n_tiles + 1,), in_specs=in_specs, out_specs=out_specs,
            scratch_shapes=[pltpu.VMEM((tm, D_MODEL), F32), pltpu.VMEM((tm, D_MODEL), F32)]),
        out_shape=out_shape,
        compiler_params=_params(1),
        name="out_projection",
    )(rows, *ctx, *lat, mod3, g_ffn, w_out_b, rw_split, rb)


def _gather_rows(idx_ref, base, n_rows, src_hbm, dst, sem, *, unrolled, both_queues=False):
    def one(r, priority):
        src_row = pl.multiple_of(idx_ref[base + r], ROW_TILE)
        pltpu.make_async_copy(
            src_hbm.at[pl.ds(src_row, ROW_TILE)], dst.at[pl.ds(r * ROW_TILE, ROW_TILE)],
            sem).start(priority=priority)

    if unrolled:
        for r in range(n_rows):
            one(r, r % 2 if both_queues else 0)
    else:
        def body(r, carry):
            one(r, 0)
            return carry
        lax.fori_loop(0, n_rows, body, 0, unroll=8)


def _wait_rows(src_hbm, dst, sem):
    pltpu.make_async_copy(src_hbm.at[pl.ds(0, dst.shape[0])], dst, sem).wait()


def _moe_kernel(be_ref, ne_ref, nu_ref, first_ref, tok_ref, h_hbm, wg_hbm, wu_hbm, wd_hbm, y_ref,
                xg0, xg1, xg2, sg, su, sd, wgb, wub, wdb, sem, wsem):
    b = pl.program_id(0)
    n_used = nu_ref[0]
    last = n_used - 1
    active = b < n_used
    bufs = (xg0, xg1, xg2)
    n_buf = len(bufs)

    def weight_copies(e):
        return (pltpu.make_async_copy(wg_hbm.at[e], sg, wsem.at[0]),
                pltpu.make_async_copy(wu_hbm.at[e], su, wsem.at[1]),
                pltpu.make_async_copy(wd_hbm.at[e], sd, wsem.at[2]))

    @pl.when(b == 0)
    def _():
        for cp in weight_copies(be_ref[0]):
            cp.start(priority=1)
        for i in range(n_buf - 1):
            _gather_rows(tok_ref, first_ref[jnp.minimum(i, last)], EXPERT_ROWS, h_hbm, bufs[i], sem.at[i],
                         unrolled=False)

    new_expert = (b == 0) | (be_ref[b] != be_ref[jnp.maximum(b - 1, 0)])

    @pl.when(active & new_expert)
    def _():
        for cp in weight_copies(be_ref[b]):
            cp.wait()
        wgb[...] = sg[...].astype(BF16)
        wub[...] = su[...].astype(BF16)
        wdb[...] = sd[...].astype(BF16)

        @pl.when(ne_ref[b] >= 0)
        def _():
            for cp in weight_copies(ne_ref[b]):
                cp.start(priority=1)

    def step(slot):
        cur = bufs[slot]
        ahead = bufs[(slot + n_buf - 1) % n_buf]
        sem_ahead = sem.at[(slot + n_buf - 1) % n_buf]
        _wait_rows(h_hbm, cur, sem.at[slot])
        _gather_rows(tok_ref, first_ref[jnp.minimum(b + n_buf - 1, last)], EXPERT_ROWS, h_hbm, ahead,
                     sem_ahead, unrolled=True, both_queues=True)
        lo, hi = _unpack_bf16_pairs(_load_row_tiles(cur, EXPERT_ROWS))
        lo = lo.astype(BF16)
        hi = hi.astype(BF16)
        a = _dot(lo, wgb[0:PACKED, :]) + _dot(hi, wgb[PACKED:D_MODEL, :])
        u = _dot(lo, wub[0:PACKED, :]) + _dot(hi, wub[PACKED:D_MODEL, :])
        act = (a / (1.0 + jnp.exp(-a)) * u).astype(BF16)
        _store_row_tiles(y_ref, _pack_bf16_pairs(_dot(act, wdb[...])))

        @pl.when(b == last)
        def _():
            for i in range(1, n_buf):
                _wait_rows(h_hbm, bufs[(slot + i) % n_buf], sem.at[(slot + i) % n_buf])

    for slot in range(n_buf):
        @pl.when(active & (b % n_buf == slot))
        def _(slot=slot):
            step(slot)

    @pl.when(b >= n_used)
    def _():
        y_ref[...] = jnp.zeros_like(y_ref)


def _experts(tables, cap, h2_packed, w_gate, w_up, w_down):
    bm = EXPERT_ROWS
    xg = pltpu.VMEM((bm * ROW_TILE, LANES), jnp.uint32)
    return pl.pallas_call(
        _moe_kernel,
        grid_spec=pltpu.PrefetchScalarGridSpec(
            num_scalar_prefetch=len(tables),
            grid=(cap // bm,),
            in_specs=[pl.BlockSpec(memory_space=pl.ANY)] * 4,
            out_specs=pl.BlockSpec((bm * ROW_TILE, LANES), lambda b, *_: (b, 0)),
            scratch_shapes=[
                xg, xg, xg,
                pltpu.VMEM((D_MODEL, D_EXPERT), F32), pltpu.VMEM((D_MODEL, D_EXPERT), F32),
                pltpu.VMEM((D_EXPERT, D_MODEL), F32),
                pltpu.VMEM((D_MODEL, D_EXPERT), BF16), pltpu.VMEM((D_MODEL, D_EXPERT), BF16),
                pltpu.VMEM((D_EXPERT, D_MODEL), BF16),
                pltpu.SemaphoreType.DMA((3,)), pltpu.SemaphoreType.DMA((3,)),
            ],
        ),
        out_shape=jax.ShapeDtypeStruct((cap * ROW_TILE, LANES), jnp.uint32),
        compiler_params=_params(1),
        name="experts",
    )(*tables, h2_packed, w_gate, w_up, w_down)


def _combine_kernel(slot_ref, rows_ref, y_hbm, x1_ref, wts_ref, mod_ref, g_ref, op_ref, os_ref, yg0, yg1, sem,
                    *, n_ctx_tiles, n_tiles):
    del rows_ref
    tm = TOKEN_TILE
    i = pl.program_id(0)
    n_slots = slot_ref.shape[0] // TOP_K

    def start(tile, buf, s, unrolled):
        for k in range(TOP_K):
            _gather_rows(slot_ref, k * n_slots + tile * tm, tm, y_hbm, buf.at[k], s, unrolled=unrolled,
                         both_queues=True)

    @pl.when(i == 0)
    def _():
        start(0, yg0, sem.at[0], False)

    def step(cur, nxt, sem_cur, sem_nxt):
        for k in range(TOP_K):
            _wait_rows(y_hbm, cur.at[k], sem_cur)
        start(jnp.minimum(i + 1, n_tiles - 1), nxt, sem_nxt, True)
        w = wts_ref[...]
        lo0, hi0 = _unpack_bf16_pairs(_load_row_tiles(cur.at[0], tm))
        lo1, hi1 = _unpack_bf16_pairs(_load_row_tiles(cur.at[1], tm))
        ffn = jnp.concatenate([lo0 * w[:, 0:1] + lo1 * w[:, 1:2], hi0 * w[:, 0:1] + hi1 * w[:, 1:2]], axis=-1)
        x2 = x1_ref[...] + mod_ref[0][5:6] * ffn
        out = _rms(x2) * g_ref[...]

        @pl.when(i < n_ctx_tiles)
        def _():
            op_ref[...] = out

        @pl.when(i >= n_ctx_tiles)
        def _():
            os_ref[...] = out

        @pl.when(i == n_tiles - 1)
        def _():
            for k in range(TOP_K):
                _wait_rows(y_hbm, nxt.at[k], sem_nxt)

    @pl.when(i % 2 == 0)
    def _():
        step(yg0, yg1, sem.at[0], sem.at[1])

    @pl.when(i % 2 == 1)
    def _():
        step(yg1, yg0, sem.at[1], sem.at[0])


def _combine(slots, rows, y_buf, x1, wts, mod3, g_final, t_ctx):
    t = x1.shape[0]
    tm = TOKEN_TILE
    n_tiles = t // tm
    n_ctx = t_ctx // tm
    return pl.pallas_call(
        functools.partial(_combine_kernel, n_ctx_tiles=n_ctx, n_tiles=n_tiles),
        grid_spec=pltpu.PrefetchScalarGridSpec(
            num_scalar_prefetch=2,
            grid=(n_tiles,),
            in_specs=[
                pl.BlockSpec(memory_space=pl.ANY),
                pl.BlockSpec((tm, D_MODEL), lambda i, s, r: (i, 0)),
                pl.BlockSpec((tm, ROUTER_LANES), lambda i, s, r: (i, 0)),
                pl.BlockSpec((1, 6, D_MODEL), lambda i, s, r: (r[i], 0, 0)),
                pl.BlockSpec((1, D_MODEL), lambda i, s, r: (0, 0)),
            ],
            out_specs=(
                pl.BlockSpec((tm, D_MODEL), lambda i, s, r: (jnp.minimum(i, n_ctx - 1), 0)),
                pl.BlockSpec((tm, D_MODEL), lambda i, s, r: (jnp.maximum(i - n_ctx, 0), 0)),
            ),
            scratch_shapes=[pltpu.VMEM((TOP_K, tm * ROW_TILE, LANES), jnp.uint32),
                            pltpu.VMEM((TOP_K, tm * ROW_TILE, LANES), jnp.uint32),
                            pltpu.SemaphoreType.DMA((2,))],
        ),
        out_shape=(jax.ShapeDtypeStruct((t_ctx, D_MODEL), F32),
                   jax.ShapeDtypeStruct((t - t_ctx, D_MODEL), F32)),
        compiler_params=_params(1),
        name="combine",
    )(slots, rows, y_buf, x1, wts, mod3, g_final)


def _rope_tables(n_tokens):
    rows = n_tokens // GRID_W
    half = ROPE_AXIS_DIM // 2
    inv_freq = ROPE_BASE ** (-jnp.arange(half, dtype=F32) / half)
    ar = jnp.arange(rows, dtype=F32)[:, None] * inv_freq
    ac = jnp.arange(GRID_W, dtype=F32)[:, None] * inv_freq

    def per_token(row_vals, col_vals):
        r = jnp.repeat(row_vals, GRID_W, axis=0)
        c = jnp.tile(col_vals, (rows, 1))
        return r, c

    cr, cc = per_token(jnp.cos(ar), jnp.cos(ac))
    sr, sc = per_token(jnp.sin(ar), jnp.sin(ac))
    return jnp.concatenate([cr, cr, cc, cc], axis=-1), jnp.concatenate([-sr, sr, -sc, sc], axis=-1)


def _dispatch_tables(expert_ids):
    t = expert_ids[0].shape[0]
    bm = EXPERT_ROWS
    flat_e = jnp.concatenate(expert_ids)
    onehot = (jnp.arange(N_EXPERTS, dtype=jnp.int32)[:, None] == flat_e[None, :]).astype(jnp.int32)
    running = jnp.cumsum(onehot, axis=1)
    rank = jnp.sum(running * onehot, axis=0) - 1
    counts = running[:, -1]
    padded = (counts + bm - 1) // bm * bm
    pad_end = jnp.cumsum(padded)
    pad_start = pad_end - padded
    dest = (jnp.sum(pad_start[:, None] * onehot, axis=0) + rank).astype(jnp.int32)
    n_blocks = (t * TOP_K + N_EXPERTS * (bm - 1) + bm - 1) // bm
    assert t >= bm
    keys = jnp.concatenate([flat_e, jnp.full((bm,), N_EXPERTS, flat_e.dtype)])
    token_tile = jnp.arange(t, dtype=jnp.int32) * ROW_TILE
    tiles = jnp.concatenate([token_tile] * TOP_K + [token_tile[:bm]])
    _, src_tile = lax.sort((keys, tiles), num_keys=1, is_stable=True)
    n_used = (pad_end[-1] // bm).astype(jnp.int32)
    blk = jnp.arange(n_blocks, dtype=jnp.int32)
    blk = jnp.minimum(blk, n_used - 1)
    block_expert = jnp.sum((pad_end[None, :] <= (blk * bm)[:, None]).astype(jnp.int32), axis=1)
    block_expert = jnp.minimum(block_expert, N_EXPERTS - 1)
    e_ids = jnp.arange(N_EXPERTS, dtype=jnp.int32)
    later = (e_ids[None, :] > e_ids[:, None]) & (counts[None, :] > 0)
    next_of = jnp.min(jnp.where(later, e_ids[None, :], N_EXPERTS), axis=1)
    next_of = jnp.where(next_of == N_EXPERTS, -1, next_of)
    of_block = (block_expert[:, None] == e_ids[None, :]).astype(jnp.int32)
    next_expert = jnp.sum(next_of[None, :] * of_block, axis=1)
    start = jnp.cumsum(counts) - counts
    first_src = jnp.sum((start - pad_start)[None, :] * of_block, axis=1) + blk * bm
    return (block_expert.astype(jnp.int32), next_expert.astype(jnp.int32), n_used.reshape(1),
            first_src.astype(jnp.int32), src_tile, dest * ROW_TILE, n_blocks * bm)


def kernel(x_prompt, x_sample, cache_k, cache_v, c, c_ctx, w_mod, b_mod, norm_mix_g, norm_ffn_g, w_in,
           pool_w, pool_scale, attn_sink, out_norm_pool_g, out_norm_attn_g, w_out, router_group_w,
           router_group_b, router_expert_w, router_expert_b, w_gate, w_up, w_down, final_norm_g):
    depth = w_mod.shape[0]
    assert depth == 1, "single trunk layer"
    bp, lp, _ = x_prompt.shape
    bs, ls, _ = x_sample.shape
    tp, ts = bp * lp, bs * ls
    tm = TOKEN_TILE
    l = 0

    cond8 = jnp.zeros((8, D_MODEL), F32).at[:bs].set(c).at[bs].set(c_ctx)
    mod3 = _modulation(cond8, w_mod[l], b_mod[l]).reshape(8, 6, D_MODEL)
    rows_p = jnp.full((tp // tm,), bs, jnp.int32)
    rows_s = jnp.arange(ts // tm, dtype=jnp.int32) // (ls // tm)

    w_in_b = w_in[l].astype(BF16)
    w_out_b = w_out[l].astype(BF16)
    pool_w_b = pool_w[l].astype(BF16)
    g_mix = norm_mix_g[l].reshape(1, D_MODEL)
    g_ffn = norm_ffn_g[l].reshape(1, D_MODEL)
    g_pool = out_norm_pool_g[l].reshape(1, POOL_WIDTH)
    g_attn = out_norm_attn_g[l].reshape(1, ATTN_WIDTH)
    p_scale = pool_scale[l].reshape(1, POOL_WIDTH)
    sink_b = jnp.broadcast_to(attn_sink[l][:, None], (N_HEADS, LANES))
    g_final = final_norm_g.reshape(1, D_MODEL)

    rw = jnp.concatenate(
        [router_group_w[l], jnp.transpose(router_expert_w[l], (1, 0, 2)).reshape(D_MODEL, N_EXPERTS)], axis=1)
    rw = jnp.pad(rw, ((0, 0), (0, ROUTER_LANES - rw.shape[1])))
    rw_hi = rw.astype(BF16)
    rw_split = jnp.concatenate([rw_hi, (rw - rw_hi.astype(F32)).astype(BF16)], axis=1)
    rb = jnp.concatenate([router_group_b[l], router_expert_b[l].reshape(N_EXPERTS)])
    rb = jnp.pad(rb, (0, ROUTER_LANES - rb.shape[0])).reshape(1, ROUTER_LANES)

    xp = x_prompt.reshape(tp, D_MODEL)
    xs = x_sample.reshape(ts, D_MODEL)

    up_p, q_p, k_p, v_p = _in_projection(xp, rows_p, mod3, g_mix, w_in_b, None, F32)
    pool_p = _pool_mixer(up_p.reshape(bp, lp, POOL_WIDTH), pool_w_b, p_scale, g_pool).reshape(tp, POOL_WIDTH)
    attn_p = _context_attention(q_p, k_p, v_p, sink_b, g_attn, lp)
    up_s, q_s, k_s, v_s = _in_projection(xs, rows_s, mod3, g_mix, w_in_b, _rope_tables(ls), BF16)
    pool_s = _pool_mixer(up_s.reshape(bs, ls, POOL_WIDTH), pool_w_b, p_scale, g_pool).reshape(ts, POOL_WIDTH)
    attn_s = _latent_attention(
        q_s, k_s.reshape(bs, ls, KV_WIDTH), v_s.reshape(bs, ls, KV_WIDTH),
        cache_k[:, l].reshape(bs, -1, KV_WIDTH), cache_v[:, l].reshape(bs, -1, KV_WIDTH), sink_b, g_attn)

    rows_all = jnp.concatenate([rows_p, rows_s])
    x1_all, h2_all, ids_all, wts_all = _out_projection(
        (pool_p, attn_p, xp), (pool_s, attn_s, xs), rows_all, mod3, g_ffn, w_out_b, rw_split, rb)

    *expert_tables, slots, cap = _dispatch_tables([ids_all[k] for k in range(TOP_K)])
    y_buf = _experts(expert_tables, cap, h2_all, w_gate[l], w_up[l], w_down[l])

    y_p, y_s = _combine(slots, rows_all, y_buf, x1_all, wts_all, mod3, g_final, tp)

    kv_shape = (bp, 1, lp, N_KV_HEADS, HEAD_DIM)
    return (y_p.reshape(bp, lp, D_MODEL), y_s.reshape(bs, ls, D_MODEL),
            k_p.reshape(kv_shape), v_p.reshape(kv_shape))
```

```python
import functools

import jax
import jax.numpy as jnp
from jax import lax
from jax.experimental import pallas as pl
from jax.experimental.pallas import tpu as pltpu

F32 = jnp.float32
BF16 = jnp.bfloat16

D_MODEL = 2048
GRID_W = 64
HEAD_DIM = 128
ATTN_WIDTH = 1024
POOL_WIDTH = 1024
N_HEADS = 8
N_KV_HEADS = 2
Q_PER_KV = 4
KV_WIDTH = 256
IN_WIDTH = 2560
WINDOW = 128
POOL_WINDOWS = (2, 4, 8, 16)
POOL_GROUP_DIM = 256
ROPE_BASE = 10000.0
ROPE_AXIS_DIM = 64
N_EXPERT_GROUPS = 4
EXPERTS_PER_GROUP = 8
N_EXPERTS = 32
TOP_K = 2
D_EXPERT = 512
NORM_EPS = 1e-6
NEG_INF = -1e30
ATTN_SCALE = HEAD_DIM ** -0.5
LOG2E = 1.4426950408889634
Q_SCALE = ATTN_SCALE * LOG2E

LANES = 128
Q_BLOCKS_PER_STEP = 4
CTX_SEQS_PER_STEP = 4
VMEM_LIMIT = 48 * 1024 * 1024
PACKED = D_MODEL // 2
ROW_TILE = 8
assert PACKED == ROW_TILE * LANES, "a packed row must fill exactly one tile"
MOD_TILE = 1024
TOKEN_TILE = 256
IN_TILE = 512
POOL_TILES_PER_STEP = 4
POOL_HALO = 16
Q_BLOCK = 128
EXPERT_ROWS = 256
ROUTER_LANES = 128


def _params(n_grid):
    return pltpu.CompilerParams(
        dimension_semantics=("arbitrary",) * n_grid, vmem_limit_bytes=VMEM_LIMIT)


def _rms(x):
    return x * lax.rsqrt(jnp.mean(x * x, axis=-1, keepdims=True) + NORM_EPS)


def _dot(a, b):
    return jnp.dot(a, b, preferred_element_type=F32)


def _dot_nt(a, b):
    return lax.dot_general(a, b, (((1,), (1,)), ((), ())), preferred_element_type=F32)


def _pack_bf16_pairs(x):
    c = x.shape[1] // 2
    lo = lax.bitcast_convert_type(x[:, :c].astype(BF16).astype(F32), jnp.uint32)
    hi = lax.bitcast_convert_type(x[:, c:].astype(BF16).astype(F32), jnp.uint32)
    return hi | (lo >> 16)


def _store_row_tiles(ref, x):
    n = x.shape[0]
    for c in range(PACKED // LANES):
        ref[pl.ds(c, n, stride=ROW_TILE), :] = x[:, c * LANES:(c + 1) * LANES]


def _load_row_tiles(ref, n):
    return jnp.concatenate(
        [ref[pl.ds(c, n, stride=ROW_TILE), :] for c in range(PACKED // LANES)], axis=-1)


def _unpack_bf16_pairs(u):
    lo = lax.bitcast_convert_type(u << 16, F32)
    hi = lax.bitcast_convert_type(u & jnp.uint32(0xFFFF0000), F32)
    return lo, hi


def _mod_kernel(cond_ref, w_ref, b_ref, o_ref):
    c = cond_ref[...]
    s = c / (1.0 + jnp.exp(-c))
    o_ref[...] = _dot(s.astype(BF16), w_ref[...].astype(BF16)) + b_ref[...]


def _modulation(cond8, w_mod, b_mod):
    n = w_mod.shape[1]
    tn = MOD_TILE
    return pl.pallas_call(
        _mod_kernel,
        grid=(n // tn,),
        in_specs=[
            pl.BlockSpec((8, D_MODEL), lambda j: (0, 0)),
            pl.BlockSpec((D_MODEL, tn), lambda j: (0, j)),
            pl.BlockSpec((1, tn), lambda j: (0, j)),
        ],
        out_specs=pl.BlockSpec((8, tn), lambda j: (0, j)),
        out_shape=jax.ShapeDtypeStruct((8, n), F32),
        compiler_params=_params(1),
        name="modulation",
    )(cond8, w_mod, b_mod.reshape(1, n))


def _rope(x, cos, sin_signed):
    lane = lax.broadcasted_iota(jnp.int32, x.shape, 1)
    partner = jnp.where((lane & 63) < 32,
                        pltpu.roll(x, HEAD_DIM - 32, axis=1),
                        pltpu.roll(x, 32, axis=1))
    return x * cos + partner * sin_signed


def _inproj_kernel(rows_ref, x_ref, mod_ref, g_ref, w_ref, *rest, rope):
    del rows_ref
    if rope:
        cos_ref, sin_ref, up_ref, q_ref, k_ref, v_ref = rest
    else:
        up_ref, q_ref, k_ref, v_ref = rest
    m = mod_ref[0]
    h = _rms(x_ref[...]) * g_ref[...]
    hb = (h * (1.0 + m[1:2]) + m[0:1]).astype(BF16)
    up_ref[...] = _dot(hb, w_ref[:, 0:POOL_WIDTH]).astype(BF16)
    q = _dot(hb, w_ref[:, POOL_WIDTH:POOL_WIDTH + ATTN_WIDTH])
    k = _dot(hb, w_ref[:, POOL_WIDTH + ATTN_WIDTH:IN_WIDTH - KV_WIDTH])
    v = _dot(hb, w_ref[:, IN_WIDTH - KV_WIDTH:IN_WIDTH])
    if rope:
        cos = cos_ref[...]
        sin = sin_ref[...]
        for hd in range(N_HEADS):
            sl = slice(hd * HEAD_DIM, (hd + 1) * HEAD_DIM)
            q_ref[:, sl] = (_rope(q[:, sl], cos, sin) * Q_SCALE).astype(BF16)
        for hd in range(N_KV_HEADS):
            sl = slice(hd * HEAD_DIM, (hd + 1) * HEAD_DIM)
            k_ref[:, sl] = _rope(k[:, sl], cos, sin).astype(k_ref.dtype)
    else:
        q_ref[...] = (q * Q_SCALE).astype(BF16)
        k_ref[...] = k.astype(k_ref.dtype)
    v_ref[...] = v.astype(v_ref.dtype)


def _in_projection(x, rows, mod3, g_mix, w_in_b, rope_tabs, kv_dtype):
    t = x.shape[0]
    tm = IN_TILE
    per = IN_TILE // TOKEN_TILE
    rope = rope_tabs is not None
    in_specs = [
        pl.BlockSpec((tm, D_MODEL), lambda i, r: (i, 0)),
        pl.BlockSpec((1, 6, D_MODEL), lambda i, r: (r[i * per], 0, 0)),
        pl.BlockSpec((1, D_MODEL), lambda i, r: (0, 0)),
        pl.BlockSpec((D_MODEL, IN_WIDTH), lambda i, r: (0, 0), pipeline_mode=pl.Buffered(1)),
    ]
    args = [x, mod3, g_mix, w_in_b]
    if rope:
        seq_tiles = rope_tabs[0].shape[0] // tm
        for tab in rope_tabs:
            in_specs.append(pl.BlockSpec((tm, HEAD_DIM), lambda i, r: (i % seq_tiles, 0)))
            args.append(tab)
    out_shape = (
        jax.ShapeDtypeStruct((t, POOL_WIDTH), BF16),
        jax.ShapeDtypeStruct((t, ATTN_WIDTH), BF16),
        jax.ShapeDtypeStruct((t, KV_WIDTH), kv_dtype),
        jax.ShapeDtypeStruct((t, KV_WIDTH), kv_dtype),
    )
    out_specs = (
        pl.BlockSpec((tm, POOL_WIDTH), lambda i, r: (i, 0)),
        pl.BlockSpec((tm, ATTN_WIDTH), lambda i, r: (i, 0)),
        pl.BlockSpec((tm, KV_WIDTH), lambda i, r: (i, 0)),
        pl.BlockSpec((tm, KV_WIDTH), lambda i, r: (i, 0)),
    )
    return pl.pallas_call(
        functools.partial(_inproj_kernel, rope=rope),
        grid_spec=pltpu.PrefetchScalarGridSpec(
            num_scalar_prefetch=1, grid=(t // tm,), in_specs=in_specs, out_specs=out_specs),
        out_shape=out_shape,
        compiler_params=_params(1),
        name="in_projection_rope" if rope else "in_projection",
    )(rows, *args)


def _pool_kernel(u_ref, pw_ref, ps_ref, g_ref, o_ref, *, seq_len):
    tq = TOKEN_TILE
    win = min(seq_len, tq + 2 * POOL_HALO)
    for sub in range(o_ref.shape[1] // tq):
        t0 = pl.program_id(1) * o_ref.shape[1] + sub * tq
        src0 = pl.multiple_of(jnp.clip(t0 - POOL_HALO, 0, seq_len - win), POOL_HALO)
        u = u_ref[0, pl.ds(src0, win), :]
        t = t0 + lax.broadcasted_iota(jnp.int32, (tq, win), 0)
        j = src0 + lax.broadcasted_iota(jnp.int32, (tq, win), 1)
        tc = t0 + lax.broadcasted_iota(jnp.int32, (tq, 1), 0)
        ys = []
        for gi, w in enumerate(POOL_WINDOWS):
            lo = jnp.maximum(t - w // 2, 0)
            hi = jnp.minimum(t + (w - w // 2), seq_len)
            cnt = (hi - lo).astype(F32)
            a = jnp.where((j >= lo) & (j < hi), 1.0, 0.0) - jnp.where(j == t, cnt, 0.0)
            cnt_col = (jnp.minimum(tc + (w - w // 2), seq_len) - jnp.maximum(tc - w // 2, 0)).astype(F32)
            sl = slice(gi * POOL_GROUP_DIM, (gi + 1) * POOL_GROUP_DIM)
            d = _dot(a.astype(BF16), u[:, sl]) / cnt_col
            ys.append(_dot(d.astype(BF16), pw_ref[gi]))
        y = jnp.concatenate(ys, axis=-1) * ps_ref[...]
        o_ref[0, sub * tq:(sub + 1) * tq, :] = (_rms(y) * g_ref[...]).astype(BF16)


def _pool_mixer(up, pool_w_b, pool_scale, g_out_pool):
    b, seq_len, _ = up.shape
    tq = TOKEN_TILE * min(POOL_TILES_PER_STEP, seq_len // TOKEN_TILE)
    return pl.pallas_call(
        functools.partial(_pool_kernel, seq_len=seq_len),
        grid=(b, seq_len // tq),
        in_specs=[
            pl.BlockSpec((1, seq_len, POOL_WIDTH), lambda bi, i: (bi, 0, 0)),
            pl.BlockSpec((len(POOL_WINDOWS), POOL_GROUP_DIM, POOL_GROUP_DIM), lambda bi, i: (0, 0, 0)),
            pl.BlockSpec((1, POOL_WIDTH), lambda bi, i: (0, 0)),
            pl.BlockSpec((1, POOL_WIDTH), lambda bi, i: (0, 0)),
        ],
        out_specs=pl.BlockSpec((1, tq, POOL_WIDTH), lambda bi, i: (bi, i, 0)),
        out_shape=jax.ShapeDtypeStruct((b, seq_len, POOL_WIDTH), BF16),
        compiler_params=_params(2),
        name=f"pool_mixer_{seq_len}",
    )(up, pool_w_b, pool_scale, g_out_pool)


def _stack_heads(q, kh, rows):
    return jnp.concatenate(
        [q[:, (kh * Q_PER_KV + g) * HEAD_DIM:(kh * Q_PER_KV + g + 1) * HEAD_DIM]
         for g in range(Q_PER_KV)], axis=0)


def _sink_column(sink_ref, kh, rows):
    return jnp.concatenate(
        [jnp.broadcast_to(sink_ref[kh * Q_PER_KV + g:kh * Q_PER_KV + g + 1, 0:1] * LOG2E, (rows, 1))
         for g in range(Q_PER_KV)], axis=0)


def _attend(q4, key_sets, value_sets, sk, band=None):
    s = [_dot_nt(q4, k) for k in key_sets]
    if band is not None:
        s[0] = jnp.where(band, s[0], NEG_INF)
    m = sk
    for si in s:
        m = jnp.maximum(m, jnp.max(si, axis=-1, keepdims=True))
    acc = None
    for si, v in zip(s, value_sets):
        v1 = jnp.concatenate([v, jnp.ones_like(v)], axis=-1)
        part = _dot(jnp.exp2(si - m).astype(BF16), v1)
        acc = part if acc is None else acc + part
    den = acc[:, HEAD_DIM:] + jnp.exp2(sk - m)
    return acc[:, :HEAD_DIM] / den


def _ctx_attn_kernel(q_ref, k_ref, v_ref, sink_ref, g_ref, o_ref, *, seq_len):
    rows = seq_len
    for j in range(q_ref.shape[0] // rows):
        rs = slice(j * rows, (j + 1) * rows)
        q = q_ref[rs, :]
        heads = [None] * N_HEADS
        for kh in range(N_KV_HEADS):
            sl = slice(kh * HEAD_DIM, (kh + 1) * HEAD_DIM)
            o = _attend(_stack_heads(q, kh, rows), [k_ref[rs, sl].astype(BF16)],
                        [v_ref[rs, sl].astype(BF16)], _sink_column(sink_ref, kh, rows))
            for g in range(Q_PER_KV):
                heads[kh * Q_PER_KV + g] = o[g * rows:(g + 1) * rows]
        y = jnp.concatenate(heads, axis=-1)
        o_ref[rs, :] = (_rms(y) * g_ref[...]).astype(BF16)


def _context_attention(q, k, v, sink_b, g_out_attn, seq_len):
    t = q.shape[0]
    blk = seq_len * CTX_SEQS_PER_STEP
    return pl.pallas_call(
        functools.partial(_ctx_attn_kernel, seq_len=seq_len),
        grid=(t // blk,),
        in_specs=[
            pl.BlockSpec((blk, ATTN_WIDTH), lambda b: (b, 0)),
            pl.BlockSpec((blk, KV_WIDTH), lambda b: (b, 0)),
            pl.BlockSpec((blk, KV_WIDTH), lambda b: (b, 0)),
            pl.BlockSpec((N_HEADS, LANES), lambda b: (0, 0)),
            pl.BlockSpec((1, ATTN_WIDTH), lambda b: (0, 0)),
        ],
        out_specs=pl.BlockSpec((blk, ATTN_WIDTH), lambda b: (b, 0)),
        out_shape=jax.ShapeDtypeStruct((t, ATTN_WIDTH), BF16),
        compiler_params=_params(1),
        name="context_attention",
    )(q, k, v, sink_b, g_out_attn)


def _lat_attn_kernel(q_ref, k_ref, v_ref, ck_ref, cv_ref, sink_ref, g_ref, o_ref, *, seq_len):
    rows = Q_BLOCK
    span = 3 * Q_BLOCK
    shape = (Q_PER_KV * rows, span)
    row_in_block = lax.broadcasted_iota(jnp.int32, shape, 0) & (rows - 1)
    col = lax.broadcasted_iota(jnp.int32, shape, 1)
    for j in range(Q_BLOCKS_PER_STEP):
        q0 = (pl.program_id(1) * Q_BLOCKS_PER_STEP + j) * rows
        start = pl.multiple_of(jnp.clip(q0 - Q_BLOCK, 0, seq_len - span), Q_BLOCK)
        q = q_ref[j * rows:(j + 1) * rows, :]
        kl = k_ref[0, pl.ds(start, span), :]
        vl = v_ref[0, pl.ds(start, span), :]
        band = jnp.abs(row_in_block - col + (q0 - start)) <= WINDOW
        heads = [None] * N_HEADS
        for kh in range(N_KV_HEADS):
            sl = slice(kh * HEAD_DIM, (kh + 1) * HEAD_DIM)
            o = _attend(_stack_heads(q, kh, rows),
                        [kl[:, sl], ck_ref[0, :, sl].astype(BF16)],
                        [vl[:, sl], cv_ref[0, :, sl].astype(BF16)],
                        _sink_column(sink_ref, kh, rows), band)
            for g in range(Q_PER_KV):
                heads[kh * Q_PER_KV + g] = o[g * rows:(g + 1) * rows]
        y = jnp.concatenate(heads, axis=-1)
        o_ref[j * rows:(j + 1) * rows, :] = (_rms(y) * g_ref[...]).astype(BF16)


def _latent_attention(q, k, v, cache_k, cache_v, sink_b, g_out_attn):
    b, seq_len, _ = k.shape
    q_rows = Q_BLOCK * Q_BLOCKS_PER_STEP
    nq = seq_len // q_rows
    past = cache_k.shape[1]
    return pl.pallas_call(
        functools.partial(_lat_attn_kernel, seq_len=seq_len),
        grid=(b, nq),
        in_specs=[
            pl.BlockSpec((q_rows, ATTN_WIDTH), lambda bi, n: (bi * nq + n, 0)),
            pl.BlockSpec((1, seq_len, KV_WIDTH), lambda bi, n: (bi, 0, 0)),
            pl.BlockSpec((1, seq_len, KV_WIDTH), lambda bi, n: (bi, 0, 0)),
            pl.BlockSpec((1, past, KV_WIDTH), lambda bi, n: (bi, 0, 0)),
            pl.BlockSpec((1, past, KV_WIDTH), lambda bi, n: (bi, 0, 0)),
            pl.BlockSpec((N_HEADS, LANES), lambda bi, n: (0, 0)),
            pl.BlockSpec((1, ATTN_WIDTH), lambda bi, n: (0, 0)),
        ],
        out_specs=pl.BlockSpec((q_rows, ATTN_WIDTH), lambda bi, n: (bi * nq + n, 0)),
        out_shape=jax.ShapeDtypeStruct((b * seq_len, ATTN_WIDTH), BF16),
        compiler_params=_params(2),
        name="latent_attention",
    )(q, k, v, cache_k, cache_v, sink_b, g_out_attn)


def _route(logits):
    lane = lax.broadcasted_iota(jnp.int32, logits.shape, 1).astype(F32)
    neg = -jnp.inf

    def first_argmax(x):
        mx = jnp.max(x, axis=-1, keepdims=True)
        return mx, jnp.min(jnp.where(x == mx, lane, float(ROUTER_LANES)), axis=-1, keepdims=True)

    gl = jnp.where(lane < N_EXPERT_GROUPS, logits, neg)
    gmax, g_idx = first_argmax(gl)
    p_g = 1.0 / jnp.sum(jnp.exp(gl - gmax), axis=-1, keepdims=True)
    base = N_EXPERT_GROUPS + EXPERTS_PER_GROUP * g_idx
    el = jnp.where((lane >= base) & (lane < base + EXPERTS_PER_GROUP), logits, neg)
    v1, i1 = first_argmax(el)
    v2, i2 = first_argmax(jnp.where(lane == i1, neg, el))
    e2 = jnp.exp(v2 - v1)
    w1 = p_g / (1.0 + e2)
    w2 = p_g * e2 / (1.0 + e2)
    ids = jnp.where(lane == 0.0, i1 - N_EXPERT_GROUPS, jnp.where(lane == 1.0, i2 - N_EXPERT_GROUPS, 0.0))
    wts = jnp.where(lane == 0.0, w1, jnp.where(lane == 1.0, w2, 0.0))
    return ids, wts


def _outproj_kernel(rows_ref, pool_p, attn_p, x_p, pool_s, attn_s, x_s, mod_ref, g_ref, w_ref, rw_ref,
                    rb_ref, x1_ref, h2_ref, ids_ref, wts_ref, mix0, mix1, *, n_ctx_tiles, n_tiles):
    del rows_ref
    i = pl.program_id(0)

    def project(pool_ref, attn_ref, mix_ref):
        mix_ref[...] = (_dot(pool_ref[...], w_ref[0:POOL_WIDTH, :])
                        + _dot(attn_ref[...], w_ref[POOL_WIDTH:POOL_WIDTH + ATTN_WIDTH, :]))

    def tail(x_ref, mix_ref):
        m = mod_ref[0]
        x1 = x_ref[...] + m[2:3] * mix_ref[...]
        x1_ref[...] = x1
        h2 = _rms(x1) * g_ref[...] * (1.0 + m[4:5]) + m[3:4]
        _store_row_tiles(h2_ref, _pack_bf16_pairs(h2))
        hi = h2.astype(BF16)
        lo = (h2 - hi.astype(F32)).astype(BF16)
        both = _dot(hi, rw_ref[...])
        logits = (both[:, :ROUTER_LANES] + both[:, ROUTER_LANES:] + _dot(lo, rw_ref[:, :ROUTER_LANES])
                  + rb_ref[...])
        ids, wts = _route(logits)
        ids_ref[...] = jnp.transpose(ids)[0:ROW_TILE, :].astype(jnp.int32)
        wts_ref[...] = wts

    ctx_in = (pool_p, attn_p)
    lat_in = (pool_s, attn_s)
    for parity, (cur, prev) in enumerate(((mix0, mix1), (mix1, mix0))):
        mine = (i % 2) == parity

        @pl.when(mine & (i == 0))
        def _(cur=cur):
            project(*ctx_in, cur)

        @pl.when(mine & (i >= 1) & (i < n_ctx_tiles))
        def _(cur=cur, prev=prev):
            tail(x_p, prev)
            project(*ctx_in, cur)

        @pl.when(mine & (i == n_ctx_tiles))
        def _(cur=cur, prev=prev):
            tail(x_p, prev)
            project(*lat_in, cur)

        @pl.when(mine & (i > n_ctx_tiles) & (i < n_tiles))
        def _(cur=cur, prev=prev):
            tail(x_s, prev)
            project(*lat_in, cur)

        @pl.when(mine & (i == n_tiles))
        def _(prev=prev):
            tail(x_s, prev)


def _out_projection(ctx, lat, rows, mod3, g_ffn, w_out_b, rw_split, rb):
    tm = TOKEN_TILE
    n_ctx = ctx[2].shape[0] // tm
    n_lat = lat[2].shape[0] // tm
    n_tiles = n_ctx + n_lat
    t_all = n_tiles * tm
    assert n_ctx >= 1 and n_lat >= 1

    def ctx_map(i, r):
        return (jnp.minimum(i, n_ctx - 1), 0)

    def lat_map(i, r):
        return (jnp.clip(i - n_ctx, 0, n_lat - 1), 0)

    def ctx_prev_map(i, r):
        return (jnp.clip(i - 1, 0, n_ctx - 1), 0)

    def lat_prev_map(i, r):
        return (jnp.clip(i - 1 - n_ctx, 0, n_lat - 1), 0)

    def prev_tile(i):
        return jnp.maximum(i - 1, 0)

    in_specs = [
        pl.BlockSpec((tm, POOL_WIDTH), ctx_map),
        pl.BlockSpec((tm, ATTN_WIDTH), ctx_map),
        pl.BlockSpec((tm, D_MODEL), ctx_prev_map),
        pl.BlockSpec((tm, POOL_WIDTH), lat_map),
        pl.BlockSpec((tm, ATTN_WIDTH), lat_map),
        pl.BlockSpec((tm, D_MODEL), lat_prev_map),
        pl.BlockSpec((1, 6, D_MODEL), lambda i, r: (r[prev_tile(i)], 0, 0)),
        pl.BlockSpec((1, D_MODEL), lambda i, r: (0, 0)),
        pl.BlockSpec((D_MODEL, D_MODEL), lambda i, r: (0, 0), pipeline_mode=pl.Buffered(1)),
        pl.BlockSpec((D_MODEL, 2 * ROUTER_LANES), lambda i, r: (0, 0)),
        pl.BlockSpec((1, ROUTER_LANES), lambda i, r: (0, 0)),
    ]
    out_shape = (
        jax.ShapeDtypeStruct((t_all, D_MODEL), F32),
        jax.ShapeDtypeStruct((t_all * ROW_TILE, LANES), jnp.uint32),
        jax.ShapeDtypeStruct((ROW_TILE, t_all), jnp.int32),
        jax.ShapeDtypeStruct((t_all, ROUTER_LANES), F32),
    )
    out_specs = (
        pl.BlockSpec((tm, D_MODEL), lambda i, r: (prev_tile(i), 0)),
        pl.BlockSpec((tm * ROW_TILE, LANES), lambda i, r: (prev_tile(i), 0)),
        pl.BlockSpec((ROW_TILE, tm), lambda i, r: (0, prev_tile(i))),
        pl.BlockSpec((tm, ROUTER_LANES), lambda i, r: (prev_tile(i), 0)),
    )
    return pl.pallas_call(
        functools.partial(_outproj_kernel, n_ctx_tiles=n_ctx, n_tiles=n_tiles),
        grid_spec=pltpu.PrefetchScalarGridSpec(
            num_scalar_prefetch=1, grid=(n_tiles + 1,), in_specs=in_specs, out_specs=out_specs,
            scratch_shapes=[pltpu.VMEM((tm, D_MODEL), F32), pltpu.VMEM((tm, D_MODEL), F32)]),
        out_shape=out_shape,
        compiler_params=_params(1),
        name="out_projection",
    )(rows, *ctx, *lat, mod3, g_ffn, w_out_b, rw_split, rb)


def _gather_rows(idx_ref, base, n_rows, src_hbm, dst, sem, *, unrolled, both_queues=False):
    def one(r, priority):
        src_row = pl.multiple_of(idx_ref[base + r], ROW_TILE)
        pltpu.make_async_copy(
            src_hbm.at[pl.ds(src_row, ROW_TILE)], dst.at[pl.ds(r * ROW_TILE, ROW_TILE)],
            sem).start(priority=priority)

    if unrolled:
        for r in range(n_rows):
            one(r, r % 2 if both_queues else 0)
    else:
        def body(r, carry):
            one(r, 0)
            return carry
        lax.fori_loop(0, n_rows, body, 0, unroll=8)


def _wait_rows(src_hbm, dst, sem):
    pltpu.make_async_copy(src_hbm.at[pl.ds(0, dst.shape[0])], dst, sem).wait()


def _moe_kernel(be_ref, ne_ref, nu_ref, first_ref, tok_ref, h_hbm, wg_hbm, wu_hbm, wd_hbm, y_ref,
                xg0, xg1, xg2, sg, su, sd, wgb, wub, wdb, sem, wsem):
    b = pl.program_id(0)
    n_used = nu_ref[0]
    last = n_used - 1
    active = b < n_used
    bufs = (xg0, xg1, xg2)
    n_buf = len(bufs)

    def weight_copies(e):
        return (pltpu.make_async_copy(wg_hbm.at[e], sg, wsem.at[0]),
                pltpu.make_async_copy(wu_hbm.at[e], su, wsem.at[1]),
                pltpu.make_async_copy(wd_hbm.at[e], sd, wsem.at[2]))

    @pl.when(b == 0)
    def _():
        for cp in weight_copies(be_ref[0]):
            cp.start(priority=1)
        for i in range(n_buf - 1):
            _gather_rows(tok_ref, first_ref[jnp.minimum(i, last)], EXPERT_ROWS, h_hbm, bufs[i], sem.at[i],
                         unrolled=False)

    new_expert = (b == 0) | (be_ref[b] != be_ref[jnp.maximum(b - 1, 0)])

    @pl.when(active & new_expert)
    def _():
        for cp in weight_copies(be_ref[b]):
            cp.wait()
        wgb[...] = sg[...].astype(BF16)
        wub[...] = su[...].astype(BF16)
        wdb[...] = sd[...].astype(BF16)

        @pl.when(ne_ref[b] >= 0)
        def _():
            for cp in weight_copies(ne_ref[b]):
                cp.start(priority=1)

    def step(slot):
        cur = bufs[slot]
        ahead = bufs[(slot + n_buf - 1) % n_buf]
        sem_ahead = sem.at[(slot + n_buf - 1) % n_buf]
        _wait_rows(h_hbm, cur, sem.at[slot])
        _gather_rows(tok_ref, first_ref[jnp.minimum(b + n_buf - 1, last)], EXPERT_ROWS, h_hbm, ahead,
                     sem_ahead, unrolled=True, both_queues=True)
        lo, hi = _unpack_bf16_pairs(_load_row_tiles(cur, EXPERT_ROWS))
        lo = lo.astype(BF16)
        hi = hi.astype(BF16)
        a = _dot(lo, wgb[0:PACKED, :]) + _dot(hi, wgb[PACKED:D_MODEL, :])
        u = _dot(lo, wub[0:PACKED, :]) + _dot(hi, wub[PACKED:D_MODEL, :])
        act = (a / (1.0 + jnp.exp(-a)) * u).astype(BF16)
        _store_row_tiles(y_ref, _pack_bf16_pairs(_dot(act, wdb[...])))

        @pl.when(b == last)
        def _():
            for i in range(1, n_buf):
                _wait_rows(h_hbm, bufs[(slot + i) % n_buf], sem.at[(slot + i) % n_buf])

    for slot in range(n_buf):
        @pl.when(active & (b % n_buf == slot))
        def _(slot=slot):
            step(slot)

    @pl.when(b >= n_used)
    def _():
        y_ref[...] = jnp.zeros_like(y_ref)


def _experts(tables, cap, h2_packed, w_gate, w_up, w_down):
    bm = EXPERT_ROWS
    xg = pltpu.VMEM((bm * ROW_TILE, LANES), jnp.uint32)
    return pl.pallas_call(
        _moe_kernel,
        grid_spec=pltpu.PrefetchScalarGridSpec(
            num_scalar_prefetch=len(tables),
            grid=(cap // bm,),
            in_specs=[pl.BlockSpec(memory_space=pl.ANY)] * 4,
            out_specs=pl.BlockSpec((bm * ROW_TILE, LANES), lambda b, *_: (b, 0)),
            scratch_shapes=[
                xg, xg, xg,
                pltpu.VMEM((D_MODEL, D_EXPERT), F32), pltpu.VMEM((D_MODEL, D_EXPERT), F32),
                pltpu.VMEM((D_EXPERT, D_MODEL), F32),
                pltpu.VMEM((D_MODEL, D_EXPERT), BF16), pltpu.VMEM((D_MODEL, D_EXPERT), BF16),
                pltpu.VMEM((D_EXPERT, D_MODEL), BF16),
                pltpu.SemaphoreType.DMA((3,)), pltpu.SemaphoreType.DMA((3,)),
            ],
        ),
        out_shape=jax.ShapeDtypeStruct((cap * ROW_TILE, LANES), jnp.uint32),
        compiler_params=_params(1),
        name="experts",
    )(*tables, h2_packed, w_gate, w_up, w_down)


def _combine_kernel(slot_ref, rows_ref, y_hbm, x1_ref, wts_ref, mod_ref, g_ref, op_ref, os_ref, yg0, yg1, sem,
                    *, n_ctx_tiles, n_tiles):
    del rows_ref
    tm = TOKEN_TILE
    i = pl.program_id(0)
    n_slots = slot_ref.shape[0] // TOP_K

    def start(tile, buf, s, unrolled):
        for k in range(TOP_K):
            _gather_rows(slot_ref, k * n_slots + tile * tm, tm, y_hbm, buf.at[k], s, unrolled=unrolled,
                         both_queues=True)

    @pl.when(i == 0)
    def _():
        start(0, yg0, sem.at[0], False)

    def step(cur, nxt, sem_cur, sem_nxt):
        for k in range(TOP_K):
            _wait_rows(y_hbm, cur.at[k], sem_cur)
        start(jnp.minimum(i + 1, n_tiles - 1), nxt, sem_nxt, True)
        w = wts_ref[...]
        lo0, hi0 = _unpack_bf16_pairs(_load_row_tiles(cur.at[0], tm))
        lo1, hi1 = _unpack_bf16_pairs(_load_row_tiles(cur.at[1], tm))
        ffn = jnp.concatenate([lo0 * w[:, 0:1] + lo1 * w[:, 1:2], hi0 * w[:, 0:1] + hi1 * w[:, 1:2]], axis=-1)
        x2 = x1_ref[...] + mod_ref[0][5:6] * ffn
        out = _rms(x2) * g_ref[...]

        @pl.when(i < n_ctx_tiles)
        def _():
            op_ref[...] = out

        @pl.when(i >= n_ctx_tiles)
        def _():
            os_ref[...] = out

        @pl.when(i == n_tiles - 1)
        def _():
            for k in range(TOP_K):
                _wait_rows(y_hbm, nxt.at[k], sem_nxt)

    @pl.when(i % 2 == 0)
    def _():
        step(yg0, yg1, sem.at[0], sem.at[1])

    @pl.when(i % 2 == 1)
    def _():
        step(yg1, yg0, sem.at[1], sem.at[0])


def _combine(slots, rows, y_buf, x1, wts, mod3, g_final, t_ctx):
    t = x1.shape[0]
    tm = TOKEN_TILE
    n_tiles = t // tm
    n_ctx = t_ctx // tm
    return pl.pallas_call(
        functools.partial(_combine_kernel, n_ctx_tiles=n_ctx, n_tiles=n_tiles),
        grid_spec=pltpu.PrefetchScalarGridSpec(
            num_scalar_prefetch=2,
            grid=(n_tiles,),
            in_specs=[
                pl.BlockSpec(memory_space=pl.ANY),
                pl.BlockSpec((tm, D_MODEL), lambda i, s, r: (i, 0)),
                pl.BlockSpec((tm, ROUTER_LANES), lambda i, s, r: (i, 0)),
                pl.BlockSpec((1, 6, D_MODEL), lambda i, s, r: (r[i], 0, 0)),
                pl.BlockSpec((1, D_MODEL), lambda i, s, r: (0, 0)),
            ],
            out_specs=(
                pl.BlockSpec((tm, D_MODEL), lambda i, s, r: (jnp.minimum(i, n_ctx - 1), 0)),
                pl.BlockSpec((tm, D_MODEL), lambda i, s, r: (jnp.maximum(i - n_ctx, 0), 0)),
            ),
            scratch_shapes=[pltpu.VMEM((TOP_K, tm * ROW_TILE, LANES), jnp.uint32),
                            pltpu.VMEM((TOP_K, tm * ROW_TILE, LANES), jnp.uint32),
                            pltpu.SemaphoreType.DMA((2,))],
        ),
        out_shape=(jax.ShapeDtypeStruct((t_ctx, D_MODEL), F32),
                   jax.ShapeDtypeStruct((t - t_ctx, D_MODEL), F32)),
        compiler_params=_params(1),
        name="combine",
    )(slots, rows, y_buf, x1, wts, mod3, g_final)


def _rope_tables(n_tokens):
    rows = n_tokens // GRID_W
    half = ROPE_AXIS_DIM // 2
    inv_freq = ROPE_BASE ** (-jnp.arange(half, dtype=F32) / half)
    ar = jnp.arange(rows, dtype=F32)[:, None] * inv_freq
    ac = jnp.arange(GRID_W, dtype=F32)[:, None] * inv_freq

    def per_token(row_vals, col_vals):
        r = jnp.repeat(row_vals, GRID_W, axis=0)
        c = jnp.tile(col_vals, (rows, 1))
        return r, c

    cr, cc = per_token(jnp.cos(ar), jnp.cos(ac))
    sr, sc = per_token(jnp.sin(ar), jnp.sin(ac))
    return jnp.concatenate([cr, cr, cc, cc], axis=-1), jnp.concatenate([-sr, sr, -sc, sc], axis=-1)


def _dispatch_tables(expert_ids):
    t = expert_ids[0].shape[0]
    bm = EXPERT_ROWS
    flat_e = jnp.concatenate(expert_ids)
    onehot = (jnp.arange(N_EXPERTS, dtype=jnp.int32)[:, None] == flat_e[None, :]).astype(jnp.int32)
    running = jnp.cumsum(onehot, axis=1)
    rank = jnp.sum(running * onehot, axis=0) - 1
    counts = running[:, -1]
    padded = (counts + bm - 1) // bm * bm
    pad_end = jnp.cumsum(padded)
    pad_start = pad_end - padded
    dest = (jnp.sum(pad_start[:, None] * onehot, axis=0) + rank).astype(jnp.int32)
    n_blocks = (t * TOP_K + N_EXPERTS * (bm - 1) + bm - 1) // bm
    assert t >= bm
    keys = jnp.concatenate([flat_e, jnp.full((bm,), N_EXPERTS, flat_e.dtype)])
    token_tile = jnp.arange(t, dtype=jnp.int32) * ROW_TILE
    tiles = jnp.concatenate([token_tile] * TOP_K + [token_tile[:bm]])
    _, src_tile = lax.sort((keys, tiles), num_keys=1, is_stable=True)
    n_used = (pad_end[-1] // bm).astype(jnp.int32)
    blk = jnp.arange(n_blocks, dtype=jnp.int32)
    blk = jnp.minimum(blk, n_used - 1)
    block_expert = jnp.sum((pad_end[None, :] <= (blk * bm)[:, None]).astype(jnp.int32), axis=1)
    block_expert = jnp.minimum(block_expert, N_EXPERTS - 1)
    e_ids = jnp.arange(N_EXPERTS, dtype=jnp.int32)
    later = (e_ids[None, :] > e_ids[:, None]) & (counts[None, :] > 0)
    next_of = jnp.min(jnp.where(later, e_ids[None, :], N_EXPERTS), axis=1)
    next_of = jnp.where(next_of == N_EXPERTS, -1, next_of)
    of_block = (block_expert[:, None] == e_ids[None, :]).astype(jnp.int32)
    next_expert = jnp.sum(next_of[None, :] * of_block, axis=1)
    start = jnp.cumsum(counts) - counts
    first_src = jnp.sum((start - pad_start)[None, :] * of_block, axis=1) + blk * bm
    return (block_expert.astype(jnp.int32), next_expert.astype(jnp.int32), n_used.reshape(1),
            first_src.astype(jnp.int32), src_tile, dest * ROW_TILE, n_blocks * bm)


def kernel(x_prompt, x_sample, cache_k, cache_v, c, c_ctx, w_mod, b_mod, norm_mix_g, norm_ffn_g, w_in,
           pool_w, pool_scale, attn_sink, out_norm_pool_g, out_norm_attn_g, w_out, router_group_w,
           router_group_b, router_expert_w, router_expert_b, w_gate, w_up, w_down, final_norm_g):
    depth = w_mod.shape[0]
    assert depth == 1, "single trunk layer"
    bp, lp, _ = x_prompt.shape
    bs, ls, _ = x_sample.shape
    tp, ts = bp * lp, bs * ls
    tm = TOKEN_TILE
    l = 0

    cond8 = jnp.zeros((8, D_MODEL), F32).at[:bs].set(c).at[bs].set(c_ctx)
    mod3 = _modulation(cond8, w_mod[l], b_mod[l]).reshape(8, 6, D_MODEL)
    rows_p = jnp.full((tp // tm,), bs, jnp.int32)
    rows_s = jnp.arange(ts // tm, dtype=jnp.int32) // (ls // tm)

    w_in_b = w_in[l].astype(BF16)
    w_out_b = w_out[l].astype(BF16)
    pool_w_b = pool_w[l].astype(BF16)
    g_mix = norm_mix_g[l].reshape(1, D_MODEL)
    g_ffn = norm_ffn_g[l].reshape(1, D_MODEL)
    g_pool = out_norm_pool_g[l].reshape(1, POOL_WIDTH)
    g_attn = out_norm_attn_g[l].reshape(1, ATTN_WIDTH)
    p_scale = pool_scale[l].reshape(1, POOL_WIDTH)
    sink_b = jnp.broadcast_to(attn_sink[l][:, None], (N_HEADS, LANES))
    g_final = final_norm_g.reshape(1, D_MODEL)

    rw = jnp.concatenate(
        [router_group_w[l], jnp.transpose(router_expert_w[l], (1, 0, 2)).reshape(D_MODEL, N_EXPERTS)], axis=1)
    rw = jnp.pad(rw, ((0, 0), (0, ROUTER_LANES - rw.shape[1])))
    rw_hi = rw.astype(BF16)
    rw_split = jnp.concatenate([rw_hi, (rw - rw_hi.astype(F32)).astype(BF16)], axis=1)
    rb = jnp.concatenate([router_group_b[l], router_expert_b[l].reshape(N_EXPERTS)])
    rb = jnp.pad(rb, (0, ROUTER_LANES - rb.shape[0])).reshape(1, ROUTER_LANES)

    xp = x_prompt.reshape(tp, D_MODEL)
    xs = x_sample.reshape(ts, D_MODEL)

    up_p, q_p, k_p, v_p = _in_projection(xp, rows_p, mod3, g_mix, w_in_b, None, F32)
    pool_p = _pool_mixer(up_p.reshape(bp, lp, POOL_WIDTH), pool_w_b, p_scale, g_pool).reshape(tp, POOL_WIDTH)
    attn_p = _context_attention(q_p, k_p, v_p, sink_b, g_attn, lp)
    up_s, q_s, k_s, v_s = _in_projection(xs, rows_s, mod3, g_mix, w_in_b, _rope_tables(ls), BF16)
    pool_s = _pool_mixer(up_s.reshape(bs, ls, POOL_WIDTH), pool_w_b, p_scale, g_pool).reshape(ts, POOL_WIDTH)
    attn_s = _latent_attention(
        q_s, k_s.reshape(bs, ls, KV_WIDTH), v_s.reshape(bs, ls, KV_WIDTH),
        cache_k[:, l].reshape(bs, -1, KV_WIDTH), cache_v[:, l].reshape(bs, -1, KV_WIDTH), sink_b, g_attn)

    rows_all = jnp.concatenate([rows_p, rows_s])
    x1_all, h2_all, ids_all, wts_all = _out_projection(
        (pool_p, attn_p, xp), (pool_s, attn_s, xs), rows_all, mod3, g_ffn, w_out_b, rw_split, rb)

    *expert_tables, slots, cap = _dispatch_tables([ids_all[k] for k in range(TOP_K)])
    y_buf = _experts(expert_tables, cap, h2_all, w_gate[l], w_up[l], w_down[l])

    y_p, y_s = _combine(slots, rows_all, y_buf, x1_all, wts_all, mod3, g_final, tp)

    kv_shape = (bp, 1, lp, N_KV_HEADS, HEAD_DIM)
    return (y_p.reshape(bp, lp, D_MODEL), y_s.reshape(bs, ls, D_MODEL),
            k_p.reshape(kv_shape), v_p.reshape(kv_shape))
```
